```python
import math
import jax, jax.numpy as jnp
from jax import lax
import numpy as np

D_MODEL = 1024
BATCH = 4
SEQ = 4096
DEPTH = 1

EPS = 1e-6
PLE_DIM = 256
GRID_W = 64
D_MIX = D_MODEL
N_HEADS = 8
N_KV_HEADS = 2
HEAD_DIM = 64
D_ATTN = N_HEADS * HEAD_DIM
D_KV = N_KV_HEADS * HEAD_DIM
Q_BLOCK = 128
ROPE_THETA = 10000.0
D_HYENA = D_MIX - D_ATTN
HYENA_HEAD = 64
HYENA_ORDER = 2
SHORT_CONV = 3
FILTER_EMB = 33
FILTER_HIDDEN = 64
FAST_DECAY_PCT = 0.3
SLOW_DECAY_PCT = 1.5
DECAY_TARGET = 1e-2
D_IN = D_ATTN + 2 * D_KV + (HYENA_ORDER + 1) * D_HYENA
N_GROUPS = 4
EXPERTS_PER_GROUP = 8
N_EXPERTS = N_GROUPS * EXPERTS_PER_GROUP
TOP_K = 2
D_EXPERT = 512
MOE_BLOCK = 128

kernel_name = "hymba_attn_hyena_hier_moe_block"


def rms_norm(x, gain):
    xf = x.astype(jnp.float32)
    y = xf * lax.rsqrt(jnp.mean(xf * xf, axis=-1, keepdims=True) + EPS)
    return (y * gain.astype(jnp.float32)).astype(x.dtype)


def group_rms(y, gain, width):
    B, S, W = y.shape
    yg = y.reshape(B, S, W // width, width)
    return rms_norm(yg, gain.reshape(W // width, width)).reshape(B, S, W)


def rope_tables_2d(S):
    rows = S // GRID_W
    r_idx, c_idx = jnp.meshgrid(jnp.arange(rows, dtype=jnp.float32),
                                jnp.arange(GRID_W, dtype=jnp.float32), indexing="ij")
    r_idx, c_idx = r_idx.reshape(S), c_idx.reshape(S)
    half = HEAD_DIM // 2
    inv = ROPE_THETA ** (-jnp.arange(0, half, 2, dtype=jnp.float32) / half)
    ang_r = r_idx[:, None] * inv[None]
    ang_c = c_idx[:, None] * inv[None]
    return jnp.cos(ang_r), jnp.sin(ang_r), jnp.cos(ang_c), jnp.sin(ang_c)


def apply_rope_2d(x, cos_r, sin_r, cos_c, sin_c):
    half = HEAD_DIM // 2
    xf = x.astype(jnp.float32)

    def rot(a, c, s):
        n = a.shape[-1] // 2
        a1, a2 = a[..., :n], a[..., n:]
        c = c[None, :, None, :]
        s = s[None, :, None, :]
        return jnp.concatenate([a1 * c - a2 * s, a2 * c + a1 * s], axis=-1)

    out = jnp.concatenate([rot(xf[..., :half], cos_r, sin_r),
                           rot(xf[..., half:], cos_c, sin_c)], axis=-1)
    return out.astype(x.dtype)


def attention_group(q, k, v, q_gain, k_gain):
    B, S, _ = q.shape
    G = N_HEADS // N_KV_HEADS
    q = rms_norm(q.reshape(B, S, N_HEADS, HEAD_DIM), q_gain)
    k = rms_norm(k.reshape(B, S, N_KV_HEADS, HEAD_DIM), k_gain)
    v = v.reshape(B, S, N_KV_HEADS, HEAD_DIM)
    cos_r, sin_r, cos_c, sin_c = rope_tables_2d(S)
    q = apply_rope_2d(q, cos_r, sin_r, cos_c, sin_c)
    k = apply_rope_2d(k, cos_r, sin_r, cos_c, sin_c)
    n_blk = S // Q_BLOCK
    qb = q.reshape(B, n_blk, Q_BLOCK, N_KV_HEADS, G, HEAD_DIM).transpose(1, 0, 3, 4, 2, 5)
    kt = k.transpose(0, 2, 1, 3)
    vt = v.transpose(0, 2, 1, 3)
    scale = HEAD_DIM ** -0.5

    def one_block(qi):
        s = jnp.einsum("bkgqd,bksd->bkgqs", qi, kt,
                       preferred_element_type=jnp.float32) * scale
        pr = jax.nn.softmax(s, axis=-1)
        return jnp.einsum("bkgqs,bksd->bkgqd", pr.astype(vt.dtype), vt)

    o = lax.map(one_block, qb)
    return o.transpose(1, 0, 4, 2, 3, 5).reshape(B, S, D_ATTN)


def filter_position_features(L):
    bands = (FILTER_EMB - 1) // 2
    t = jnp.linspace(0.0, 1.0, L, dtype=jnp.float32)[:, None]
    w = (2.0 * math.pi / L) * jnp.arange(L, dtype=jnp.float32)[:, None]
    f = jnp.linspace(1e-4, bands - 1, bands, dtype=jnp.float32)[None]
    z = f * w
    return jnp.concatenate([t, jnp.cos(z), -jnp.sin(z)], axis=-1)


def implicit_filters_freq(L, w_f1, b_f1, freq1, w_f2, b_f2, freq2, w_f3):
    z = filter_position_features(L)
    h = jnp.sin(freq1 * (z @ w_f1 + b_f1))
    h = jnp.sin(freq2 * (h @ w_f2 + b_f2))
    h = (h @ w_f3).astype(jnp.float32).reshape(L, HYENA_ORDER, 2, D_HYENA)
    max_decay = math.log(DECAY_TARGET) / FAST_DECAY_PCT
    min_decay = math.log(DECAY_TARGET) / SLOW_DECAY_PCT
    deltas = jnp.linspace(min_decay, max_decay, D_HYENA, dtype=jnp.float32)
    t = jnp.linspace(0.0, 1.0, L, dtype=jnp.float32)[:, None]
    decay = jnp.exp(-t * jnp.abs(deltas)[None])
    h = h * decay[:, None, None, :]
    h = h / (jnp.sum(jnp.abs(h), axis=0, keepdims=True) + EPS)
    hf, hb = h[:, :, 0], h[:, :, 1]
    circ = jnp.concatenate([hf[:1] + hb[:1], hf[1:], jnp.zeros_like(hf[:1]), hb[:0:-1]], axis=0)
    return jnp.fft.rfft(circ, n=2 * L, axis=0)


def long_conv(z, H, bias):
    L = z.shape[1]
    zf = z.astype(jnp.float32)
    Z = jnp.fft.rfft(zf, n=2 * L, axis=1)
    y = jnp.fft.irfft(Z * H[None], n=2 * L, axis=1)[:, :L]
    return (y + zf * bias.astype(jnp.float32)).astype(z.dtype)


def hyena_group(u, conv_w, conv_b, w_f1, b_f1, freq1, w_f2, b_f2, freq2, w_f3, filt_bias):
    L = u.shape[1]
    pad = SHORT_CONV // 2
    up = jnp.pad(u, ((0, 0), (pad, pad), (0, 0)))
    uc = conv_b + sum(up[:, j:j + L] * conv_w[j] for j in range(SHORT_CONV))
    v, x1, x2 = jnp.split(uc, HYENA_ORDER + 1, axis=-1)
    Hf = implicit_filters_freq(L, w_f1, b_f1, freq1, w_f2, b_f2, freq2, w_f3)
    z = v
    for o, gate in enumerate((x1, x2)):
        z = gate * long_conv(z, Hf[:, o], filt_bias[o])
    return z


def hierarchical_moe(h, w_group, b_group, w_router, b_router, w_gate, w_up, w_down):
    B, S, D = h.shape
    N = B * S
    hf = h.reshape(N, D)
    g_logits = (hf @ w_group).astype(jnp.float32) + b_group.astype(jnp.float32)
    g_prob = jax.nn.softmax(g_logits, axis=-1)
    g_top_p, g_sel = lax.top_k(g_prob, 1)
    e_logits = ((hf @ w_router).astype(jnp.float32) + b_router.astype(jnp.float32)
                ).reshape(N, N_GROUPS, EXPERTS_PER_GROUP)
    e_logits = jnp.take_along_axis(e_logits, g_sel[:, :, None], axis=1)[:, 0]
    e_prob = jax.nn.softmax(e_logits, axis=-1)
    top_p, top_i = lax.top_k(e_prob, TOP_K)
    top_p = top_p / jnp.sum(top_p, axis=-1, keepdims=True)
    weights = g_top_p * top_p
    expert = g_sel * EXPERTS_PER_GROUP + top_i

    T = MOE_BLOCK
    NK = N * TOP_K
    e_flat = expert.reshape(NK).astype(jnp.int32)
    w_flat = weights.reshape(NK)
    order = jnp.argsort(e_flat)
    e_sorted = e_flat[order]
    tok_sorted = (order // TOP_K).astype(jnp.int32)
    counts = jnp.bincount(e_flat, length=N_EXPERTS)
    starts = jnp.cumsum(counts) - counts
    padded = (counts + T - 1) // T * T
    pends = jnp.cumsum(padded)
    pstarts = pends - padded
    dest = pstarts[e_sorted] + jnp.arange(NK, dtype=jnp.int32) - starts[e_sorted]
    n_rows = -(-(NK + N_EXPERTS * (T - 1)) // T) * T
    n_blocks = n_rows // T
    row_tok = jnp.full((n_rows,), N, jnp.int32).at[dest].set(tok_sorted)
    row_w = jnp.zeros((n_rows,), jnp.float32).at[dest].set(w_flat[order])
    block_e = jnp.clip(jnp.searchsorted(pends, jnp.arange(n_blocks) * T, side="right"),
                       0, N_EXPERTS - 1)
    x_pad = jnp.concatenate([hf, jnp.zeros((1, D), hf.dtype)], axis=0)
    xb = x_pad[row_tok].reshape(n_blocks, T, D)

    def expert_block(args):
        xblk, e = args
        a = xblk @ w_gate[e]
        b = xblk @ w_up[e]
        return (jax.nn.silu(a) * b) @ w_down[e]

    yb = lax.map(expert_block, (xb, block_e)).reshape(n_rows, D)
    y = jax.ops.segment_sum(yb * row_w[:, None].astype(yb.dtype), row_tok, num_segments=N + 1)[:N]
    return y.reshape(B, S, D).astype(h.dtype)


def setup_inputs(seed: int = 0) -> dict:
    key = jax.random.key(seed)
    ks = iter(jax.random.split(key, 40))

    def nrm(shape, scale):
        return jax.random.normal(next(ks), shape, jnp.float32) * scale

    def gain(shape):
        return 1.0 + nrm(shape, 0.02)

    L = DEPTH
    return {
        "x": nrm((BATCH, SEQ, D_MODEL), 1.0),
        "p": nrm((DEPTH, BATCH, SEQ, PLE_DIM), 1.0),
        "g_mix": gain((L, D_MODEL)),
        "w_in": nrm((L, D_MODEL, D_IN), D_MODEL ** -0.5),
        "q_gain": gain((L, HEAD_DIM)),
        "k_gain": gain((L, HEAD_DIM)),
        "conv_w": nrm((L, SHORT_CONV, (HYENA_ORDER + 1) * D_HYENA), SHORT_CONV ** -0.5),
        "conv_b": nrm((L, (HYENA_ORDER + 1) * D_HYENA), 0.01),
        "w_f1": nrm((L, FILTER_EMB, FILTER_HIDDEN), FILTER_EMB ** -0.5),
        "b_f1": nrm((L, FILTER_HIDDEN), 0.1),
        "freq1": gain((L, FILTER_HIDDEN)),
        "w_f2": nrm((L, FILTER_HIDDEN, FILTER_HIDDEN), FILTER_HIDDEN ** -0.5),
        "b_f2": nrm((L, FILTER_HIDDEN), 0.1),
        "freq2": gain((L, FILTER_HIDDEN)),
        "w_f3": nrm((L, FILTER_HIDDEN, HYENA_ORDER * 2 * D_HYENA), FILTER_HIDDEN ** -0.5),
        "filt_bias": nrm((L, HYENA_ORDER, D_HYENA), 1.0),
        "g_attn_out": gain((L, D_ATTN)),
        "g_hyena_out": gain((L, D_HYENA)),
        "w_out": nrm((L, D_MIX, D_MODEL), D_MIX ** -0.5),
        "g_moe": gain((L, D_MODEL)),
        "w_group": nrm((L, D_MODEL, N_GROUPS), D_MODEL ** -0.5),
        "b_group": nrm((L, N_GROUPS), 0.01),
        "w_router": nrm((L, D_MODEL, N_EXPERTS), D_MODEL ** -0.5),
        "b_router": nrm((L, N_EXPERTS), 0.01),
        "w_gate": nrm((L, N_EXPERTS, D_MODEL, D_EXPERT), D_MODEL ** -0.5),
        "w_up": nrm((L, N_EXPERTS, D_MODEL, D_EXPERT), D_MODEL ** -0.5),
        "w_down": nrm((L, N_EXPERTS, D_EXPERT, D_MODEL), D_EXPERT ** -0.5),
        "g_ple": gain((L, D_MODEL)),
        "w_ple_gate": nrm((L, D_MODEL, D_MODEL), D_MODEL ** -0.5),
        "b_ple_gate": nrm((L, D_MODEL), 0.01),
        "w_ple": nrm((L, PLE_DIM, D_MODEL), PLE_DIM ** -0.5),
        "g_final": gain((D_MODEL,)),
    }


def reference(x, p, g_mix, w_in, q_gain, k_gain, conv_w, conv_b, w_f1, b_f1, freq1,
              w_f2, b_f2, freq2, w_f3, filt_bias, g_attn_out, g_hyena_out, w_out,
              g_moe, w_group, b_group, w_router, b_router, w_gate, w_up, w_down,
              g_ple, w_ple_gate, b_ple_gate, w_ple, g_final):
    for i in range(DEPTH):
        h = rms_norm(x, g_mix[i])
        proj = h @ w_in[i]
        q = proj[..., :D_ATTN]
        k = proj[..., D_ATTN:D_ATTN + D_KV]
        v = proj[..., D_ATTN + D_KV:D_ATTN + 2 * D_KV]
        u = proj[..., D_ATTN + 2 * D_KV:]
        ya = attention_group(q, k, v, q_gain[i], k_gain[i])
        yh = hyena_group(u, conv_w[i], conv_b[i], w_f1[i], b_f1[i], freq1[i],
                         w_f2[i], b_f2[i], freq2[i], w_f3[i], filt_bias[i])
        ya = group_rms(ya, g_attn_out[i], HEAD_DIM)
        yh = group_rms(yh, g_hyena_out[i], HYENA_HEAD)
        x = x + jnp.concatenate([ya, yh], axis=-1) @ w_out[i]
        x = x + hierarchical_moe(rms_norm(x, g_moe[i]), w_group[i], b_group[i],
                                 w_router[i], b_router[i], w_gate[i], w_up[i], w_down[i])
        gate = jax.nn.sigmoid(rms_norm(x, g_ple[i]) @ w_ple_gate[i] + b_ple_gate[i])
        x = x + (p[i] @ w_ple[i]) * gate
    return rms_norm(x, g_final)
```

```python
import functools
import math

import numpy as np
import jax
import jax.numpy as jnp
from jax import lax
from jax.experimental import pallas as pl
from jax.experimental.pallas import tpu as pltpu

F32 = jnp.float32
BF16 = jnp.bfloat16

D_MODEL = 1024
EPS = 1e-6
GRID_W = 64
N_HEADS = 8
N_KV_HEADS = 2
HEAD_DIM = 64
D_ATTN = N_HEADS * HEAD_DIM
D_KV = N_KV_HEADS * HEAD_DIM
ROPE_THETA = 10000.0
D_HYENA = 512
HYENA_HEAD = 64
FILTER_EMB = 33
FAST_DECAY_PCT = 0.3
SLOW_DECAY_PCT = 1.5
DECAY_TARGET = 1e-2
N_GROUPS = 4
EXPERTS_PER_GROUP = 8
N_EXPERTS = N_GROUPS * EXPERTS_PER_GROUP
TOP_K = 2
D_EXPERT = 512

LANES = 128
FFT_N1 = 64
FFT_N2 = 128
VMEM_LIMIT = 56 * 1024 * 1024

TM_PROJ = 512
TQ_ATTN = 256
C_HY = 32
TB_MOE = 256


def _cparams(sem):
    return pltpu.CompilerParams(dimension_semantics=sem, vmem_limit_bytes=VMEM_LIMIT)


def _rope_tables(S):
    half = HEAD_DIM // 2
    t = jnp.arange(S, dtype=F32)
    r_idx = jnp.floor(t / GRID_W)
    c_idx = t - r_idx * GRID_W
    inv = ROPE_THETA ** (-jnp.arange(0, half, 2, dtype=F32) / half)
    ang_r = r_idx[:, None] * inv[None]
    ang_c = c_idx[:, None] * inv[None]
    cos_h = jnp.concatenate([jnp.cos(ang_r), jnp.cos(ang_r), jnp.cos(ang_c), jnp.cos(ang_c)], axis=-1)
    sin_h = jnp.concatenate([-jnp.sin(ang_r), jnp.sin(ang_r), -jnp.sin(ang_c), jnp.sin(ang_c)], axis=-1)
    return jnp.tile(cos_h, (1, 2)), jnp.tile(sin_h, (1, 2))


def _dft_constants():
    n1, n2 = FFT_N1, FFT_N2
    n = n1 * n2
    a = np.arange(n1)
    ang = 2.0 * np.pi * np.outer(a, a) / n1
    far, fai = np.cos(ang), -np.sin(ang)
    hlf = n1 // 2
    ma = np.block([[far[:, :hlf], -fai[:, :hlf]], [fai[:, :hlf], far[:, :hlf]]])
    maf = np.concatenate([far, fai], axis=0)
    b = np.arange(n2)
    angt = 2.0 * np.pi * np.outer(a, b) / n
    tw = np.concatenate([np.cos(angt), -np.sin(angt)], axis=1)
    angb = 2.0 * np.pi * np.outer(b, b) / n2
    fbr, fbi = np.cos(angb), -np.sin(angb)
    g = np.block([[fbr, fbi], [-fbi, fbr]])
    ginv = np.block([[fbr, -fbi], [fbi, fbr]])
    minv_r = np.concatenate([far[:hlf], -fai[:hlf]], axis=0) / n
    minv_i = np.concatenate([fai[:hlf], far[:hlf]], axis=0) / n
    f = lambda m: jnp.asarray(m.astype(np.float32))
    return dict(ma=f(ma), maf=f(maf), tw=f(tw), g=f(g), ginv=f(ginv), minv_r=f(minv_r), minv_i=f(minv_i))


def _block_diag_ones(width, group):
    i = np.arange(width) // group
    return jnp.asarray((i[:, None] == i[None, :]).astype(np.float32)).astype(BF16)


def _group_sumsq(a, bd):
    sq = a * a
    hi = sq.astype(BF16)
    lo = (sq - hi.astype(F32)).astype(BF16)
    return (jnp.dot(hi, bd, preferred_element_type=F32) + jnp.dot(lo, bd, preferred_element_type=F32))


def _head_norm_rope(a, gain, bd, cos, sin):
    width = a.shape[-1]
    n = a * lax.rsqrt(_group_sumsq(a, bd) * (1.0 / HEAD_DIM) + EPS) * gain
    rep = width // LANES
    if rep > 1:
        cos = jnp.concatenate([cos] * rep, axis=-1)
        sin = jnp.concatenate([sin] * rep, axis=-1)
    fwd = pltpu.roll(n, width - 16, 1)
    bwd = pltpu.roll(n, 16, 1)
    lane = lax.broadcasted_iota(jnp.int32, n.shape, 1)
    sw = jnp.where((lane % 32) < 16, fwd, bwd)
    return n * cos + sw * sin


def _inproj_body(x_ref, g_ref, wqkv_ref, wu_ref, bd_ref, qg_ref, kg_ref, cos_ref, sin_ref,
                 q_ref, kt_ref, v_ref, ut_ref):
    x = x_ref[...]
    h = x * lax.rsqrt(jnp.mean(x * x, axis=-1, keepdims=True) + EPS) * g_ref[...]
    hb = h.astype(BF16)
    qkv = jnp.dot(hb, wqkv_ref[...], preferred_element_type=F32)
    cos = cos_ref[...]
    sin = sin_ref[...]
    bd = bd_ref[...]
    q = _head_norm_rope(qkv[:, :D_ATTN], qg_ref[...], bd, cos, sin)
    q_ref[...] = (q * (HEAD_DIM ** -0.5)).astype(BF16)
    k = _head_norm_rope(qkv[:, D_ATTN:D_ATTN + D_KV], kg_ref[...], bd[:D_KV, :D_KV], cos, sin)
    kt_ref[...] = k.T.astype(BF16)
    v_ref[...] = qkv[:, D_ATTN + D_KV:].astype(BF16)
    ut_ref[...] = lax.dot_general(wu_ref[...], hb, (((1,), (1,)), ((), ())),
                                  preferred_element_type=F32)


def _inproj(x, g_mix, wqkv, wut, bd, qg, kg, cos, sin):
    B, S, D = x.shape
    tm = TM_PROJ
    du = wut.shape[0]
    full = lambda shape: pl.BlockSpec(shape, lambda b, i: (0,) * len(shape))
    return pl.pallas_call(
        _inproj_body,
        grid=(B, S // tm),
        in_specs=[
            pl.BlockSpec((None, tm, D), lambda b, i: (b, i, 0)),
            full((1, D)), full(wqkv.shape), full(wut.shape), full(bd.shape),
            full((1, D_ATTN)), full((1, D_KV)),
            pl.BlockSpec((tm, LANES), lambda b, i: (i, 0)),
            pl.BlockSpec((tm, LANES), lambda b, i: (i, 0)),
        ],
        out_specs=[
            pl.BlockSpec((None, tm, D_ATTN), lambda b, i: (b, i, 0)),
            pl.BlockSpec((None, D_KV, tm), lambda b, i: (b, 0, i)),
            pl.BlockSpec((None, tm, D_KV), lambda b, i: (b, i, 0)),
            pl.BlockSpec((None, du, tm), lambda b, i: (b, 0, i)),
        ],
        out_shape=[
            jax.ShapeDtypeStruct((B, S, D_ATTN), BF16),
            jax.ShapeDtypeStruct((B, D_KV, S), BF16),
            jax.ShapeDtypeStruct((B, S, D_KV), BF16),
            jax.ShapeDtypeStruct((B, du, S), F32),
        ],
        compiler_params=_cparams(("parallel", "parallel")),
        name="inproj",
    )(x, g_mix, wqkv, wut, bd, qg, kg, cos, sin)


def _attn_body(q_ref, ke_ref, ko_ref, ve_ref, vo_ref, o_ref):
    q = q_ref[...]

    def one_head(kw_ref, vw_ref):
        s = jnp.dot(q, kw_ref[...], preferred_element_type=F32)
        m = jnp.max(s, axis=-1, keepdims=True)
        p = jnp.exp(s - m)
        l = jnp.sum(p, axis=-1, keepdims=True)
        o = jnp.dot(p.astype(BF16), vw_ref[...], preferred_element_type=F32)
        return o / l

    oe = one_head(ke_ref, ve_ref)
    oo = one_head(ko_ref, vo_ref)
    lane = lax.broadcasted_iota(jnp.int32, oe.shape, 1)
    o_ref[...] = jnp.where(lane < HEAD_DIM, oe, oo)


def _attention(q, kw, vw):
    B, S, _ = q.shape
    tq = TQ_ATTN
    n_pairs = D_ATTN // LANES
    return pl.pallas_call(
        _attn_body,
        grid=(B, S // tq, n_pairs),
        in_specs=[
            pl.BlockSpec((None, tq, LANES), lambda b, i, j: (b, i, j)),
            pl.BlockSpec((None, None, None, LANES, S), lambda b, i, j: (b, j // 2, 0, 0, 0)),
            pl.BlockSpec((None, None, None, LANES, S), lambda b, i, j: (b, j // 2, 1, 0, 0)),
            pl.BlockSpec((None, None, S, LANES), lambda b, i, j: (b, j // 2, 0, 0)),
            pl.BlockSpec((None, None, S, LANES), lambda b, i, j: (b, 1 - j // 2, 0, 0)),
        ],
        out_specs=pl.BlockSpec((None, tq, LANES), lambda b, i, j: (b, i, j)),
        out_shape=jax.ShapeDtypeStruct((B, S, D_ATTN), F32),
        compiler_params=_cparams(("parallel", "parallel", "arbitrary")),
        name="attention",
    )(q, kw, kw, vw, vw)


def _fwd_twiddle_store(y, tw_ref, s1_ref, row0):
    yr, yi = y[:FFT_N1], y[FFT_N1:]
    twr, twi = tw_ref[:, :LANES], tw_ref[:, LANES:]
    s1_ref[pl.ds(row0, FFT_N1), :LANES] = (yr * twr - yi * twi).astype(BF16)
    s1_ref[pl.ds(row0, FFT_N1), LANES:] = (yr * twi + yi * twr).astype(BF16)


def _filtfft_body(x_ref, maf_ref, tw_ref, g_ref, h_ref, s1_ref):
    C = x_ref.shape[0]

    def step_a(c, carry):
        y = jnp.dot(maf_ref[...], x_ref[c].astype(BF16), preferred_element_type=F32)
        _fwd_twiddle_store(y, tw_ref, s1_ref, pl.multiple_of(c * FFT_N1, FFT_N1))
        return carry

    lax.fori_loop(0, C, step_a, 0)
    z = jnp.dot(s1_ref[...], g_ref[...], preferred_element_type=F32)
    h_ref[...] = z.reshape(C, FFT_N1, 2 * LANES)


def _filter_fft(circ, cst):
    n_seq = circ.shape[0]
    C = C_HY
    full = lambda a: pl.BlockSpec(a.shape, lambda i: (0,) * a.ndim)
    maf, tw, g = cst["maf"].astype(BF16), cst["tw"], cst["g"].astype(BF16)
    return pl.pallas_call(
        _filtfft_body,
        grid=(n_seq // C,),
        in_specs=[pl.BlockSpec((C, FFT_N1, FFT_N2), lambda i: (i, 0, 0)), full(maf), full(tw), full(g)],
        out_specs=pl.BlockSpec((C, FFT_N1, 2 * LANES), lambda i: (i, 0, 0)),
        out_shape=jax.ShapeDtypeStruct((n_seq, FFT_N1, 2 * LANES), F32),
        scratch_shapes=[pltpu.VMEM((C * FFT_N1, 2 * LANES), BF16)],
        compiler_params=_cparams(("parallel",)),
        name="filter_fft",
    )(circ, maf, tw, g)


def _short_conv(x, par_ref, c):
    rows, lanes = x.shape
    a_i = lax.broadcasted_iota(jnp.int32, x.shape, 0)
    b_i = lax.broadcasted_iota(jnp.int32, x.shape, 1)
    l1 = pltpu.roll(x, 1, 1)
    l2 = pltpu.roll(l1, 1, 0)
    prev = jnp.where(b_i == 0, l2, l1)
    prev = jnp.where((a_i == 0) & (b_i == 0), 0.0, prev)
    r1 = pltpu.roll(x, lanes - 1, 1)
    r2 = pltpu.roll(r1, rows - 1, 0)
    nxt = jnp.where(b_i == lanes - 1, r2, r1)
    nxt = jnp.where((a_i == rows - 1) & (b_i == lanes - 1), 0.0, nxt)
    w0 = par_ref[0, pl.ds(c, 1), :]
    w1 = par_ref[1, pl.ds(c, 1), :]
    w2 = par_ref[2, pl.ds(c, 1), :]
    cb = par_ref[3, pl.ds(c, 1), :]
    return cb + prev * w0 + x * w1 + nxt * w2


def _hyena_body(v_ref, x1_ref, x2_ref, pv_ref, p1_ref, p2_ref, fb_ref, h_ref,
                ma_ref, tw_ref, g_ref, ginv_ref, mir_ref, mii_ref,
                o_ref, s1_ref, s2_ref, vc_ref, z1_ref):
    C = v_ref.shape[1]
    half = FFT_N1 // 2

    def spectral(order):
        z = jnp.dot(s1_ref[...], g_ref[...], preferred_element_type=F32)
        hs = h_ref[order].reshape(C * FFT_N1, 2 * LANES)
        zr, zi = z[:, :LANES], z[:, LANES:]
        hr, hi = hs[:, :LANES], hs[:, LANES:]
        pb = jnp.concatenate([zr * hr - zi * hi, zr * hi + zi * hr], axis=1).astype(BF16)
        s2_ref[...] = jnp.dot(pb, ginv_ref[...], preferred_element_type=F32)

    def inv_a(c):
        row0 = pl.multiple_of(c * FFT_N1, FFT_N1)
        y = s2_ref[pl.ds(row0, FFT_N1), :]
        yr, yi = y[:, :LANES], y[:, LANES:]
        twr, twi = tw_ref[:, :LANES], tw_ref[:, LANES:]
        ur = (yr * twr + yi * twi).astype(BF16)
        ui = (yi * twr - yr * twi).astype(BF16)
        out = (jnp.dot(mir_ref[...], ur, preferred_element_type=F32)
               + jnp.dot(mii_ref[...], ui, preferred_element_type=F32))
        return out[:half], out[half:]

    def fwd_a(c, xr, xi):
        xs = jnp.concatenate([xr, xi], axis=0).astype(BF16)
        y = jnp.dot(ma_ref[...], xs, preferred_element_type=F32)
        _fwd_twiddle_store(y, tw_ref, s1_ref, pl.multiple_of(c * FFT_N1, FFT_N1))

    def pass1_a(c, carry):
        vr = _short_conv(v_ref[0, c], pv_ref, c)
        vi = _short_conv(v_ref[1, c], pv_ref, c)
        vc_ref[0, c] = vr
        vc_ref[1, c] = vi
        fwd_a(c, vr, vi)
        return carry

    def pass1_b(c, carry):
        cr, ci = inv_a(c)
        bias = fb_ref[0, pl.ds(c, 1), :]
        zr = _short_conv(x1_ref[0, c], p1_ref, c) * (cr + bias * vc_ref[0, c])
        zi = _short_conv(x1_ref[1, c], p1_ref, c) * (ci + bias * vc_ref[1, c])
        z1_ref[0, c] = zr
        z1_ref[1, c] = zi
        fwd_a(c, zr, zi)
        return carry

    def pass2_b(c, carry):
        cr, ci = inv_a(c)
        bias = fb_ref[1, pl.ds(c, 1), :]
        o_ref[0, c] = _short_conv(x2_ref[0, c], p2_ref, c) * (cr + bias * z1_ref[0, c])
        o_ref[1, c] = _short_conv(x2_ref[1, c], p2_ref, c) * (ci + bias * z1_ref[1, c])
        return carry

    lax.fori_loop(0, C, pass1_a, 0)
    spectral(0)
    lax.fori_loop(0, C, pass1_b, 0)
    spectral(1)
    lax.fori_loop(0, C, pass2_b, 0)


def _hyena(u4, par_u, fb, hspec, cst):
    B = u4.shape[0]
    C = C_HY
    J = D_HYENA // C
    rows = u4.shape[2]
    full = lambda a: pl.BlockSpec(a.shape, lambda j, p: (0,) * a.ndim)
    ma, g, ginv = cst["ma"].astype(BF16), cst["g"].astype(BF16), cst["ginv"].astype(BF16)
    mir, mii = cst["minv_r"].astype(BF16), cst["minv_i"].astype(BF16)
    tw = cst["tw"]
    u_spec = lambda k: pl.BlockSpec((2, C, rows, LANES), lambda j, p, k=k: (p, j + k * J, 0, 0))
    par_spec = lambda k: pl.BlockSpec((4, C, LANES), lambda j, p, k=k: (0, j + k * J, 0))
    return pl.pallas_call(
        _hyena_body,
        grid=(J, B // 2),
        in_specs=[
            u_spec(0), u_spec(1), u_spec(2), par_spec(0), par_spec(1), par_spec(2),
            pl.BlockSpec((2, C, LANES), lambda j, p: (0, j, 0)),
            pl.BlockSpec((2, C, FFT_N1, 2 * LANES), lambda j, p: (0, j, 0, 0)),
            full(ma), full(tw), full(g), full(ginv), full(mir), full(mii),
        ],
        out_specs=pl.BlockSpec((2, C, rows, LANES), lambda j, p: (p, j, 0, 0)),
        out_shape=jax.ShapeDtypeStruct((B, D_HYENA, rows, LANES), F32),
        scratch_shapes=[
            pltpu.VMEM((C * FFT_N1, 2 * LANES), BF16),
            pltpu.VMEM((C * FFT_N1, 2 * LANES), F32),
            pltpu.VMEM((2, C, rows, LANES), F32),
            pltpu.VMEM((2, C, rows, LANES), F32),
        ],
        compiler_params=_cparams(("parallel", "arbitrary")),
        name="hyena",
    )(u4, u4, u4, par_u, par_u, par_u, fb, hspec, ma, tw, g, ginv, mir, mii)


def _filter_time_domain(L, w_f1, b_f1, freq1, w_f2, b_f2, freq2, w_f3):
    hp = lax.Precision.HIGHEST
    bands = (FILTER_EMB - 1) // 2
    t = jnp.linspace(0.0, 1.0, L, dtype=F32)[:, None]
    w = (2.0 * math.pi / L) * jnp.arange(L, dtype=F32)[:, None]
    f = jnp.linspace(1e-4, bands - 1, bands, dtype=F32)[None]
    zf = f * w
    z = jnp.concatenate([t, jnp.cos(zf), -jnp.sin(zf)], axis=-1)
    h = jnp.sin(freq1 * (jnp.dot(z, w_f1, precision=hp) + b_f1))
    h = jnp.sin(freq2 * (jnp.dot(h, w_f2, precision=hp) + b_f2))
    h = jnp.dot(h, w_f3, precision=hp).astype(F32).reshape(L, 2, 2, D_HYENA)
    max_decay = math.log(DECAY_TARGET) / FAST_DECAY_PCT
    min_decay = math.log(DECAY_TARGET) / SLOW_DECAY_PCT
    deltas = jnp.linspace(min_decay, max_decay, D_HYENA, dtype=F32)
    decay = jnp.exp(-t * jnp.abs(deltas)[None])
    h = h * decay[:, None, None, :]
    h = h / (jnp.sum(jnp.abs(h), axis=0, keepdims=True) + EPS)
    hf, hb = h[:, :, 0], h[:, :, 1]
    circ = jnp.concatenate([hf[:1] + hb[:1], hf[1:], jnp.zeros_like(hf[:1]), hb[:0:-1]], axis=0)
    return jnp.transpose(circ, (1, 2, 0))


def _outproj_body(ya_ref, yh_ref, x_ref, ga_ref, gh_ref, wo_ref, bd_ref, gm_ref, wrt_ref, brt_ref,
                  x1_ref, h2_ref, lg_ref):
    ya = ya_ref[...]
    yan = ya * lax.rsqrt(_group_sumsq(ya, bd_ref[...]) * (1.0 / HEAD_DIM) + EPS) * ga_ref[...]
    yh = yh_ref[...]
    tm = yh.shape[1]
    yh3 = yh.reshape(D_HYENA // HYENA_HEAD, HYENA_HEAD, tm)
    ms = jnp.mean(yh3 * yh3, axis=1, keepdims=True)
    yhn = (yh3 * lax.rsqrt(ms + EPS)).reshape(D_HYENA, tm) * gh_ref[...]
    mix = (jnp.dot(yan.astype(BF16), wo_ref[:D_ATTN, :], preferred_element_type=F32)
           + jnp.dot(yhn.T.astype(BF16), wo_ref[D_ATTN:, :], preferred_element_type=F32))
    x1 = x_ref[...] + mix
    x1_ref[...] = x1
    h2 = x1 * lax.rsqrt(jnp.mean(x1 * x1, axis=-1, keepdims=True) + EPS) * gm_ref[...]
    h2_ref[...] = h2.astype(BF16)
    lg_ref[...] = jnp.dot(h2, wrt_ref[...], precision=lax.Precision.HIGHEST,
                          preferred_element_type=F32) + brt_ref[...]


def _outproj(ya, yht, x, ga, gh, wo, bd, gm, wrt, brt):
    B, S, D = x.shape
    tm = TM_PROJ
    full = lambda a: pl.BlockSpec(a.shape, lambda b, i: (0,) * a.ndim)
    return pl.pallas_call(
        _outproj_body,
        grid=(B, S // tm),
        in_specs=[
            pl.BlockSpec((None, tm, D_ATTN), lambda b, i: (b, i, 0)),
            pl.BlockSpec((None, D_HYENA, tm), lambda b, i: (b, 0, i)),
            pl.BlockSpec((None, tm, D), lambda b, i: (b, i, 0)),
            full(ga), full(gh), full(wo), full(bd), full(gm), full(wrt), full(brt),
        ],
        out_specs=[
            pl.BlockSpec((None, tm, D), lambda b, i: (b, i, 0)),
            pl.BlockSpec((None, tm, D), lambda b, i: (b, i, 0)),
            pl.BlockSpec((None, tm, LANES), lambda b, i: (b, i, 0)),
        ],
        out_shape=[
            jax.ShapeDtypeStruct((B, S, D), F32),
            jax.ShapeDtypeStruct((B, S, D), BF16),
            jax.ShapeDtypeStruct((B, S, LANES), F32),
        ],
        compiler_params=_cparams(("parallel", "parallel")),
        name="outproj",
    )(ya, yht, x, ga, gh, wo, bd, gm, wrt, brt)


def _moe_body(be_ref, x_ref, rw_ref, wg_ref, wu_ref, wd_ref, y_ref, wg_s, wu_s, wd_s):
    i = pl.program_id(0)
    prev = be_ref[jnp.maximum(i - 1, 0)]

    @pl.when((i == 0) | (be_ref[i] != prev))
    def _():
        wg_s[...] = wg_ref[...].astype(BF16)
        wu_s[...] = wu_ref[...].astype(BF16)
        wd_s[...] = wd_ref[...].astype(BF16)

    x = x_ref[...]
    a = jnp.dot(x, wg_s[...], preferred_element_type=F32)
    b = jnp.dot(x, wu_s[...], preferred_element_type=F32)
    hmid = (a * jax.nn.sigmoid(a)) * b
    y = jnp.dot(hmid.astype(BF16), wd_s[...], preferred_element_type=F32)
    y_ref[...] = y * rw_ref[...]


def _moe_experts(block_e, xs, row_w, w_gate, w_up, w_down):
    n_rows, D = xs.shape
    T = TB_MOE
    grid_spec = pltpu.PrefetchScalarGridSpec(
        num_scalar_prefetch=1,
        grid=(n_rows // T,),
        in_specs=[
            pl.BlockSpec((T, D), lambda i, be: (i, 0)),
            pl.BlockSpec((T, 1), lambda i, be: (i, 0)),
            pl.BlockSpec((None, D, D_EXPERT), lambda i, be: (be[i], 0, 0)),
            pl.BlockSpec((None, D, D_EXPERT), lambda i, be: (be[i], 0, 0)),
            pl.BlockSpec((None, D_EXPERT, D), lambda i, be: (be[i], 0, 0)),
        ],
        out_specs=pl.BlockSpec((T, D), lambda i, be: (i, 0)),
        scratch_shapes=[
            pltpu.VMEM((D, D_EXPERT), BF16), pltpu.VMEM((D, D_EXPERT), BF16), pltpu.VMEM((D_EXPERT, D), BF16),
        ],
    )
    return pl.pallas_call(
        _moe_body,
        grid_spec=grid_spec,
        out_shape=jax.ShapeDtypeStruct((n_rows, D), F32),
        compiler_params=_cparams(("arbitrary",)),
        name="moe_experts",
    )(block_e, xs, row_w, w_gate, w_up, w_down)


def _route(logits):
    g_logits = logits[:, :N_GROUPS]
    g_prob = jax.nn.softmax(g_logits, axis=-1)
    g_top_p, g_sel = lax.top_k(g_prob, 1)
    e_logits = logits[:, N_GROUPS:N_GROUPS + N_EXPERTS].reshape(-1, N_GROUPS, EXPERTS_PER_GROUP)
    e_logits = jnp.take_along_axis(e_logits, g_sel[:, :, None], axis=1)[:, 0]
    e_prob = jax.nn.softmax(e_logits, axis=-1)
    top_p, top_i = lax.top_k(e_prob, TOP_K)
    top_p = top_p / jnp.sum(top_p, axis=-1, keepdims=True)
    return g_sel * EXPERTS_PER_GROUP + top_i, g_top_p * top_p


def _dispatch(expert, weights, N):
    T = TB_MOE
    NK = N * TOP_K
    e_flat = expert.reshape(NK).astype(jnp.int32)
    w_flat = weights.reshape(NK)
    order = jnp.argsort(e_flat)
    e_sorted = e_flat[order]
    tok_sorted = (order // TOP_K).astype(jnp.int32)
    counts = jnp.bincount(e_flat, length=N_EXPERTS)
    starts = jnp.cumsum(counts) - counts
    padded = (counts + T - 1) // T * T
    pends = jnp.cumsum(padded)
    pstarts = pends - padded
    dest = (pstarts[e_sorted] + jnp.arange(NK, dtype=jnp.int32) - starts[e_sorted]).astype(jnp.int32)
    n_rows = -(-(NK + N_EXPERTS * (T - 1)) // T) * T
    n_blocks = n_rows // T
    row_tok = jnp.full((n_rows,), N, jnp.int32).at[dest].set(tok_sorted)
    row_w = jnp.zeros((n_rows,), F32).at[dest].set(w_flat[order])
    block_e = jnp.clip(jnp.searchsorted(pends, jnp.arange(n_blocks) * T, side="right"),
                       0, N_EXPERTS - 1).astype(jnp.int32)
    pos = jnp.zeros((NK,), jnp.int32).at[order].set(dest).reshape(N, TOP_K)
    return row_tok, row_w, block_e, pos


def _final_body(x1_ref, y0_ref, y1_ref, p_ref, gp_ref, wg_ref, bg_ref, wp_ref, gf_ref, o_ref):
    x2 = x1_ref[...] + y0_ref[...] + y1_ref[...]
    hp = x2 * lax.rsqrt(jnp.mean(x2 * x2, axis=-1, keepdims=True) + EPS) * gp_ref[...]
    gate = jax.nn.sigmoid(jnp.dot(hp.astype(BF16), wg_ref[...], preferred_element_type=F32) + bg_ref[...])
    pe = jnp.dot(p_ref[...].astype(BF16), wp_ref[...], preferred_element_type=F32)
    x3 = x2 + pe * gate
    o_ref[...] = x3 * lax.rsqrt(jnp.mean(x3 * x3, axis=-1, keepdims=True) + EPS) * gf_ref[...]


def _final(x1, y0, y1, p, gp, wg, bg, wp, gf):
    N, D = x1.shape
    tm = TM_PROJ
    row = lambda w: pl.BlockSpec((tm, w), lambda i: (i, 0))
    full = lambda a: pl.BlockSpec(a.shape, lambda i: (0,) * a.ndim)
    return pl.pallas_call(
        _final_body,
        grid=(N // tm,),
        in_specs=[row(D), row(D), row(D), row(p.shape[1]), full(gp), full(wg), full(bg), full(wp), full(gf)],
        out_specs=row(D),
        out_shape=jax.ShapeDtypeStruct((N, D), F32),
        compiler_params=_cparams(("parallel",)),
        name="ple_final",
    )(x1, y0, y1, p, gp, wg, bg, wp, gf)


def kernel(x, p, g_mix, w_in, q_gain, k_gain, conv_w, conv_b, w_f1, b_f1, freq1, w_f2, b_f2, freq2, w_f3, filt_bias, g_attn_out, g_hyena_out, w_out, g_moe, w_group, b_group, w_router, b_router, w_gate, w_up, w_down, g_ple, w_ple_gate, b_ple_gate, w_ple, g_final):
    B, S, D = x.shape
    N = B * S
    assert p.shape[0] == 1 and S == (FFT_N1 // 2) * FFT_N2 and B % 2 == 0
    i = 0
    cst = _dft_constants()
    cos, sin = _rope_tables(S)
    bd = _block_diag_ones(D_ATTN, HEAD_DIM)

    n_qkv = D_ATTN + 2 * D_KV
    wqkv = w_in[i][:, :n_qkv].astype(BF16)
    wut = w_in[i][:, n_qkv:].T.astype(BF16)
    q, kt, v, ut = _inproj(x, g_mix[i][None], wqkv, wut, bd,
                           jnp.tile(q_gain[i], N_HEADS)[None], jnp.tile(k_gain[i], N_KV_HEADS)[None], cos, sin)

    zeros = jnp.zeros((B, HEAD_DIM, S), BF16)
    k0, k1 = kt[:, :HEAD_DIM], kt[:, HEAD_DIM:]
    kw = jnp.stack([jnp.stack([jnp.concatenate([k0, zeros], 1), jnp.concatenate([zeros, k0], 1)], 1),
                    jnp.stack([jnp.concatenate([k1, zeros], 1), jnp.concatenate([zeros, k1], 1)], 1)], 1)
    vw = jnp.stack([v, jnp.concatenate([v[..., HEAD_DIM:], v[..., :HEAD_DIM]], -1)], 1)
    ya = _attention(q, kw, vw)

    circ = _filter_time_domain(S, w_f1[i], b_f1[i], freq1[i], w_f2[i], b_f2[i], freq2[i], w_f3[i])
    hspec = _filter_fft(circ.reshape(2 * D_HYENA, FFT_N1, FFT_N2), cst)
    hspec = hspec.reshape(2, D_HYENA, FFT_N1, 2 * LANES)
    du = ut.shape[1]
    u4 = ut.reshape(B, du, S // LANES, LANES)
    par_u = jnp.broadcast_to(jnp.concatenate([conv_w[i], conv_b[i][None]], 0)[:, :, None], (4, du, LANES))
    fb = jnp.broadcast_to(filt_bias[i][:, :, None], (2, D_HYENA, LANES))
    yh4 = _hyena(u4, par_u, fb, hspec, cst)
    yht = yh4.reshape(B, D_HYENA, S)

    wrt = jnp.zeros((D, LANES), F32).at[:, :N_GROUPS].set(w_group[i]).at[:, N_GROUPS:N_GROUPS + N_EXPERTS].set(w_router[i])
    brt = jnp.zeros((1, LANES), F32).at[0, :N_GROUPS].set(b_group[i]).at[0, N_GROUPS:N_GROUPS + N_EXPERTS].set(b_router[i])
    x1, h2, logits = _outproj(ya, yht, x, g_attn_out[i][None], g_hyena_out[i][:, None], w_out[i].astype(BF16),
                              bd, g_moe[i][None], wrt, brt)

    expert, weights = _route(logits.reshape(N, LANES))
    row_tok, row_w, block_e, pos = _dispatch(expert, weights, N)
    h2p = jnp.concatenate([h2.reshape(N, D), jnp.zeros((1, D), BF16)], axis=0)
    xs = h2p[row_tok]
    yb = _moe_experts(block_e, xs, row_w[:, None], w_gate[i], w_up[i], w_down[i])
    y0 = yb[pos[:, 0]]
    y1 = yb[pos[:, 1]]

    out = _final(x1.reshape(N, D), y0, y1, p[i].reshape(N, -1), g_ple[i][None], w_ple_gate[i].astype(BF16),
                 b_ple_gate[i][None], w_ple[i].astype(BF16), g_final[None])
    return out.reshape(B, S, D)
```

```python
import functools
import math

import numpy as np
import jax
import jax.numpy as jnp
from jax import lax
from jax.experimental import pallas as pl
from jax.experimental.pallas import tpu as pltpu

F32 = jnp.float32
BF16 = jnp.bfloat16

D_MODEL = 1024
EPS = 1e-6
GRID_W = 64
N_HEADS = 8
N_KV_HEADS = 2
HEAD_DIM = 64
D_ATTN = N_HEADS * HEAD_DIM
D_KV = N_KV_HEADS * HEAD_DIM
ROPE_THETA = 10000.0
D_HYENA = 512
HYENA_HEAD = 64
FILTER_EMB = 33
FAST_DECAY_PCT = 0.3
SLOW_DECAY_PCT = 1.5
DECAY_TARGET = 1e-2
N_GROUPS = 4
EXPERTS_PER_GROUP = 8
N_EXPERTS = N_GROUPS * EXPERTS_PER_GROUP
TOP_K = 2
D_EXPERT = 512

LANES = 128
FFT_N1 = 64
FFT_N2 = 128
VMEM_LIMIT = 56 * 1024 * 1024

TM_PROJ = 512
TQ_ATTN = 256
C_HY = 32
TB_MOE = 256


def _cparams(sem):
    return pltpu.CompilerParams(dimension_semantics=sem, vmem_limit_bytes=VMEM_LIMIT)


def _rope_tables(S):
    half = HEAD_DIM // 2
    t = jnp.arange(S, dtype=F32)
    r_idx = jnp.floor(t / GRID_W)
    c_idx = t - r_idx * GRID_W
    inv = ROPE_THETA ** (-jnp.arange(0, half, 2, dtype=F32) / half)
    ang_r = r_idx[:, None] * inv[None]
    ang_c = c_idx[:, None] * inv[None]
    cos_h = jnp.concatenate([jnp.cos(ang_r), jnp.cos(ang_r), jnp.cos(ang_c), jnp.cos(ang_c)], axis=-1)
    sin_h = jnp.concatenate([-jnp.sin(ang_r), jnp.sin(ang_r), -jnp.sin(ang_c), jnp.sin(ang_c)], axis=-1)
    return jnp.tile(cos_h, (1, 2)), jnp.tile(sin_h, (1, 2))


def _dft_constants():
    n1, n2 = FFT_N1, FFT_N2
    n = n1 * n2
    a = np.arange(n1)
    ang = 2.0 * np.pi * np.outer(a, a) / n1
    far, fai = np.cos(ang), -np.sin(ang)
    hlf = n1 // 2
    ma = np.block([[far[:, :hlf], -fai[:, :hlf]], [fai[:, :hlf], far[:, :hlf]]])
    maf = np.concatenate([far, fai], axis=0)
    b = np.arange(n2)
    angt = 2.0 * np.pi * np.outer(a, b) / n
    tw = np.concatenate([np.cos(angt), -np.sin(angt)], axis=1)
    angb = 2.0 * np.pi * np.outer(b, b) / n2
    fbr, fbi = np.cos(angb), -np.sin(angb)
    g = np.block([[fbr, fbi], [-fbi, fbr]])
    ginv = np.block([[fbr, -fbi], [fbi, fbr]])
    minv_r = np.concatenate([far[:hlf], -fai[:hlf]], axis=0) / n
    minv_i = np.concatenate([fai[:hlf], far[:hlf]], axis=0) / n
    f = lambda m: jnp.asarray(m.astype(np.float32))
    return dict(ma=f(ma), maf=f(maf), tw=f(tw), g=f(g), ginv=f(ginv), minv_r=f(minv_r), minv_i=f(minv_i))


def _block_diag_ones(width, group):
    i = np.arange(width) // group
    return jnp.asarray((i[:, None] == i[None, :]).astype(np.float32)).astype(BF16)


def _group_sumsq(a, bd):
    sq = a * a
    hi = sq.astype(BF16)
    lo = (sq - hi.astype(F32)).astype(BF16)
    return (jnp.dot(hi, bd, preferred_element_type=F32) + jnp.dot(lo, bd, preferred_element_type=F32))


def _head_norm_rope(a, gain, bd, cos, sin):
    width = a.shape[-1]
    n = a * lax.rsqrt(_group_sumsq(a, bd) * (1.0 / HEAD_DIM) + EPS) * gain
    rep = width // LANES
    if rep > 1:
        cos = jnp.concatenate([cos] * rep, axis=-1)
        sin = jnp.concatenate([sin] * rep, axis=-1)
    fwd = pltpu.roll(n, width - 16, 1)
    bwd = pltpu.roll(n, 16, 1)
    lane = lax.broadcasted_iota(jnp.int32, n.shape, 1)
    sw = jnp.where((lane % 32) < 16, fwd, bwd)
    return n * cos + sw * sin


def _inproj_body(x_ref, g_ref, wqkv_ref, wu_ref, bd_ref, qg_ref, kg_ref, cos_ref, sin_ref,
                 q_ref, kt_ref, v_ref, ut_ref):
    x = x_ref[...]
    h = x * lax.rsqrt(jnp.mean(x * x, axis=-1, keepdims=True) + EPS) * g_ref[...]
    hb = h.astype(BF16)
    qkv = jnp.dot(hb, wqkv_ref[...], preferred_element_type=F32)
    cos = cos_ref[...]
    sin = sin_ref[...]
    bd = bd_ref[...]
    q = _head_norm_rope(qkv[:, :D_ATTN], qg_ref[...], bd, cos, sin)
    q_ref[...] = (q * (HEAD_DIM ** -0.5)).astype(BF16)
    k = _head_norm_rope(qkv[:, D_ATTN:D_ATTN + D_KV], kg_ref[...], bd[:D_KV, :D_KV], cos, sin)
    kt_ref[...] = k.T.astype(BF16)
    v_ref[...] = qkv[:, D_ATTN + D_KV:].astype(BF16)
    ut_ref[...] = lax.dot_general(wu_ref[...], hb, (((1,), (1,)), ((), ())),
                                  preferred_element_type=F32)


def _inproj(x, g_mix, wqkv, wut, bd, qg, kg, cos, sin):
    B, S, D = x.shape
    tm = TM_PROJ
    du = wut.shape[0]
    full = lambda shape: pl.BlockSpec(shape, lambda b, i: (0,) * len(shape))
    return pl.pallas_call(
        _inproj_body,
        grid=(B, S // tm),
        in_specs=[
            pl.BlockSpec((None, tm, D), lambda b, i: (b, i, 0)),
            full((1, D)), full(wqkv.shape), full(wut.shape), full(bd.shape),
            full((1, D_ATTN)), full((1, D_KV)),
            pl.BlockSpec((tm, LANES), lambda b, i: (i, 0)),
            pl.BlockSpec((tm, LANES), lambda b, i: (i, 0)),
        ],
        out_specs=[
            pl.BlockSpec((None, tm, D_ATTN), lambda b, i: (b, i, 0)),
            pl.BlockSpec((None, D_KV, tm), lambda b, i: (b, 0, i)),
            pl.BlockSpec((None, tm, D_KV), lambda b, i: (b, i, 0)),
            pl.BlockSpec((None, du, tm), lambda b, i: (b, 0, i)),
        ],
        out_shape=[
            jax.ShapeDtypeStruct((B, S, D_ATTN), BF16),
            jax.ShapeDtypeStruct((B, D_KV, S), BF16),
            jax.ShapeDtypeStruct((B, S, D_KV), BF16),
            jax.ShapeDtypeStruct((B, du, S), F32),
        ],
        compiler_params=_cparams(("parallel", "parallel")),
        name="inproj",
    )(x, g_mix, wqkv, wut, bd, qg, kg, cos, sin)


def _attn_body(q_ref, ke_ref, ko_ref, ve_ref, vo_ref, o_ref):
    q = q_ref[...]

    def one_head(kw_ref, vw_ref):
        s = jnp.dot(q, kw_ref[...], preferred_element_type=F32)
        m = jnp.max(s, axis=-1, keepdims=True)
        p = jnp.exp(s - m)
        l = jnp.sum(p, axis=-1, keepdims=True)
        o = jnp.dot(p.astype(BF16), vw_ref[...], preferred_element_type=F32)
        return o / l

    oe = one_head(ke_ref, ve_ref)
    oo = one_head(ko_ref, vo_ref)
    lane = lax.broadcasted_iota(jnp.int32, oe.shape, 1)
    o_ref[...] = jnp.where(lane < HEAD_DIM, oe, oo)


def _attention(q, kw, vw):
    B, S, _ = q.shape
    tq = TQ_ATTN
    n_pairs = D_ATTN // LANES
    return pl.pallas_call(
        _attn_body,
        grid=(B, S // tq, n_pairs),
        in_specs=[
            pl.BlockSpec((None, tq, LANES), lambda b, i, j: (b, i, j)),
            pl.BlockSpec((None, None, None, LANES, S), lambda b, i, j: (b, j // 2, 0, 0, 0)),
            pl.BlockSpec((None, None, None, LANES, S), lambda b, i, j: (b, j // 2, 1, 0, 0)),
            pl.BlockSpec((None, None, S, LANES), lambda b, i, j: (b, j // 2, 0, 0)),
            pl.BlockSpec((None, None, S, LANES), lambda b, i, j: (b, 1 - j // 2, 0, 0)),
        ],
        out_specs=pl.BlockSpec((None, tq, LANES), lambda b, i, j: (b, i, j)),
        out_shape=jax.ShapeDtypeStruct((B, S, D_ATTN), F32),
        compiler_params=_cparams(("parallel", "parallel", "arbitrary")),
        name="attention",
    )(q, kw, kw, vw, vw)


def _fwd_twiddle_store(y, tw_ref, s1_ref, row0):
    yr, yi = y[:FFT_N1], y[FFT_N1:]
    twr, twi = tw_ref[:, :LANES], tw_ref[:, LANES:]
    s1_ref[pl.ds(row0, FFT_N1), :LANES] = (yr * twr - yi * twi).astype(BF16)
    s1_ref[pl.ds(row0, FFT_N1), LANES:] = (yr * twi + yi * twr).astype(BF16)


def _filtfft_body(x_ref, maf_ref, tw_ref, g_ref, h_ref, s1_ref):
    C = x_ref.shape[0]

    def step_a(c, carry):
        y = jnp.dot(maf_ref[...], x_ref[c].astype(BF16), preferred_element_type=F32)
        _fwd_twiddle_store(y, tw_ref, s1_ref, pl.multiple_of(c * FFT_N1, FFT_N1))
        return carry

    lax.fori_loop(0, C, step_a, 0)
    z = jnp.dot(s1_ref[...], g_ref[...], preferred_element_type=F32)
    h_ref[...] = z.reshape(C, FFT_N1, 2 * LANES)


def _filter_fft(circ, cst):
    n_seq = circ.shape[0]
    C = C_HY
    full = lambda a: pl.BlockSpec(a.shape, lambda i: (0,) * a.ndim)
    maf, tw, g = cst["maf"].astype(BF16), cst["tw"], cst["g"].astype(BF16)
    return pl.pallas_call(
        _filtfft_body,
        grid=(n_seq // C,),
        in_specs=[pl.BlockSpec((C, FFT_N1, FFT_N2), lambda i: (i, 0, 0)), full(maf), full(tw), full(g)],
        out_specs=pl.BlockSpec((C, FFT_N1, 2 * LANES), lambda i: (i, 0, 0)),
        out_shape=jax.ShapeDtypeStruct((n_seq, FFT_N1, 2 * LANES), F32),
        scratch_shapes=[pltpu.VMEM((C * FFT_N1, 2 * LANES), BF16)],
        compiler_params=_cparams(("parallel",)),
        name="filter_fft",
    )(circ, maf, tw, g)


def _short_conv(x, par_ref, c):
    rows, lanes = x.shape
    a_i = lax.broadcasted_iota(jnp.int32, x.shape, 0)
    b_i = lax.broadcasted_iota(jnp.int32, x.shape, 1)
    l1 = pltpu.roll(x, 1, 1)
    l2 = pltpu.roll(l1, 1, 0)
    prev = jnp.where(b_i == 0, l2, l1)
    prev = jnp.where((a_i == 0) & (b_i == 0), 0.0, prev)
    r1 = pltpu.roll(x, lanes - 1, 1)
    r2 = pltpu.roll(r1, rows - 1, 0)
    nxt = jnp.where(b_i == lanes - 1, r2, r1)
    nxt = jnp.where((a_i == rows - 1) & (b_i == lanes - 1), 0.0, nxt)
    w0 = par_ref[0, pl.ds(c, 1), :]
    w1 = par_ref[1, pl.ds(c, 1), :]
    w2 = par_ref[2, pl.ds(c, 1), :]
    cb = par_ref[3, pl.ds(c, 1), :]
    return cb + prev * w0 + x * w1 + nxt * w2


def _hyena_body(v_ref, x1_ref, x2_ref, pv_ref, p1_ref, p2_ref, fb_ref, h_ref,
                ma_ref, tw_ref, g_ref, ginv_ref, mir_ref, mii_ref,
                o_ref, s1_ref, s2_ref, vc_ref, z1_ref):
    C = v_ref.shape[1]
    half = FFT_N1 // 2

    def spectral(order):
        z = jnp.dot(s1_ref[...], g_ref[...], preferred_element_type=F32)
        hs = h_ref[order].reshape(C * FFT_N1, 2 * LANES)
        zr, zi = z[:, :LANES], z[:, LANES:]
        hr, hi = hs[:, :LANES], hs[:, LANES:]
        pb = jnp.concatenate([zr * hr - zi * hi, zr * hi + zi * hr], axis=1).astype(BF16)
        s2_ref[...] = jnp.dot(pb, ginv_ref[...], preferred_element_type=F32)

    def inv_a(c):
        row0 = pl.multiple_of(c * FFT_N1, FFT_N1)
        y = s2_ref[pl.ds(row0, FFT_N1), :]
        yr, yi = y[:, :LANES], y[:, LANES:]
        twr, twi = tw_ref[:, :LANES], tw_ref[:, LANES:]
        ur = (yr * twr + yi * twi).astype(BF16)
        ui = (yi * twr - yr * twi).astype(BF16)
        out = (jnp.dot(mir_ref[...], ur, preferred_element_type=F32)
               + jnp.dot(mii_ref[...], ui, preferred_element_type=F32))
        return out[:half], out[half:]

    def fwd_a(c, xr, xi):
        xs = jnp.concatenate([xr, xi], axis=0).astype(BF16)
        y = jnp.dot(ma_ref[...], xs, preferred_element_type=F32)
        _fwd_twiddle_store(y, tw_ref, s1_ref, pl.multiple_of(c * FFT_N1, FFT_N1))

    def pass1_a(c, carry):
        vr = _short_conv(v_ref[0, c], pv_ref, c)
        vi = _short_conv(v_ref[1, c], pv_ref, c)
        vc_ref[0, c] = vr
        vc_ref[1, c] = vi
        fwd_a(c, vr, vi)
        return carry

    def pass1_b(c, carry):
        cr, ci = inv_a(c)
        bias = fb_ref[0, pl.ds(c, 1), :]
        zr = _short_conv(x1_ref[0, c], p1_ref, c) * (cr + bias * vc_ref[0, c])
        zi = _short_conv(x1_ref[1, c], p1_ref, c) * (ci + bias * vc_ref[1, c])
        z1_ref[0, c] = zr
        z1_ref[1, c] = zi
        fwd_a(c, zr, zi)
        return carry

    def pass2_b(c, carry):
        cr, ci = inv_a(c)
        bias = fb_ref[1, pl.ds(c, 1), :]
        o_ref[0, c] = _short_conv(x2_ref[0, c], p2_ref, c) * (cr + bias * z1_ref[0, c])
        o_ref[1, c] = _short_conv(x2_ref[1, c], p2_ref, c) * (ci + bias * z1_ref[1, c])
        return carry

    lax.fori_loop(0, C, pass1_a, 0)
    spectral(0)
    lax.fori_loop(0, C, pass1_b, 0)
    spectral(1)
    lax.fori_loop(0, C, pass2_b, 0)


def _hyena(u4, par_u, fb, hspec, cst):
    B = u4.shape[0]
    C = C_HY
    J = D_HYENA // C
    rows = u4.shape[2]
    full = lambda a: pl.BlockSpec(a.shape, lambda j, p: (0,) * a.ndim)
    ma, g, ginv = cst["ma"].astype(BF16), cst["g"].astype(BF16), cst["ginv"].astype(BF16)
    mir, mii = cst["minv_r"].astype(BF16), cst["minv_i"].astype(BF16)
    tw = cst["tw"]
    u_spec = lambda k: pl.BlockSpec((2, C, rows, LANES), lambda j, p, k=k: (p, j + k * J, 0, 0))
    par_spec = lambda k: pl.BlockSpec((4, C, LANES), lambda j, p, k=k: (0, j + k * J, 0))
    return pl.pallas_call(
        _hyena_body,
        grid=(J, B // 2),
        in_specs=[
            u_spec(0), u_spec(1), u_spec(2), par_spec(0), par_spec(1), par_spec(2),
            pl.BlockSpec((2, C, LANES), lambda j, p: (0, j, 0)),
            pl.BlockSpec((2, C, FFT_N1, 2 * LANES), lambda j, p: (0, j, 0, 0)),
            full(ma), full(tw), full(g), full(ginv), full(mir), full(mii),
        ],
        out_specs=pl.BlockSpec((2, C, rows, LANES), lambda j, p: (p, j, 0, 0)),
        out_shape=jax.ShapeDtypeStruct((B, D_HYENA, rows, LANES), F32),
        scratch_shapes=[
            pltpu.VMEM((C * FFT_N1, 2 * LANES), BF16),
            pltpu.VMEM((C * FFT_N1, 2 * LANES), F32),
            pltpu.VMEM((2, C, rows, LANES), F32),
            pltpu.VMEM((2, C, rows, LANES), F32),
        ],
        compiler_params=_cparams(("parallel", "arbitrary")),
        name="hyena",
    )(u4, u4, u4, par_u, par_u, par_u, fb, hspec, ma, tw, g, ginv, mir, mii)


def _filter_time_domain(L, w_f1, b_f1, freq1, w_f2, b_f2, freq2, w_f3):
    hp = lax.Precision.HIGHEST
    bands = (FILTER_EMB - 1) // 2
    t = jnp.linspace(0.0, 1.0, L, dtype=F32)[:, None]
    w = (2.0 * math.pi / L) * jnp.arange(L, dtype=F32)[:, None]
    f = jnp.linspace(1e-4, bands - 1, bands, dtype=F32)[None]
    zf = f * w
    z = jnp.concatenate([t, jnp.cos(zf), -jnp.sin(zf)], axis=-1)
    back = lambda a: jnp.roll(a[::-1], 1, axis=0)
    zz = jnp.concatenate([z, back(z)], axis=0)
    h = jnp.sin(freq1 * (jnp.dot(zz, w_f1, precision=hp) + b_f1))
    h = jnp.sin(freq2 * (jnp.dot(h, w_f2, precision=hp) + b_f2))
    w3 = w_f3.reshape(-1, 2, 2, D_HYENA)
    hf = jnp.einsum("hoc,lh->ocl", w3[:, :, 0], h[:L], precision=hp)
    hb = jnp.einsum("hoc,lh->ocl", w3[:, :, 1], h[L:], precision=hp)
    max_decay = math.log(DECAY_TARGET) / FAST_DECAY_PCT
    min_decay = math.log(DECAY_TARGET) / SLOW_DECAY_PCT
    deltas = jnp.linspace(min_decay, max_decay, D_HYENA, dtype=F32)
    decay_f = jnp.exp(-jnp.abs(deltas)[:, None] * t[None, :, 0])
    decay_b = jnp.exp(-jnp.abs(deltas)[:, None] * back(t)[None, :, 0])
    hf = hf * decay_f[None]
    hb = hb * decay_b[None]
    hf = hf / (jnp.sum(jnp.abs(hf), axis=-1, keepdims=True) + EPS)
    hb = hb / (jnp.sum(jnp.abs(hb), axis=-1, keepdims=True) + EPS)
    first = lax.broadcasted_iota(jnp.int32, hb.shape, 2) == 0
    return jnp.concatenate([hf + jnp.where(first, hb, 0.0), jnp.where(first, 0.0, hb)], axis=-1)


def _route_lanes(lg):
    neg = -1e30
    lane = lax.broadcasted_iota(jnp.int32, lg.shape, 1)
    gmask = lane < N_GROUPS
    gl = jnp.where(gmask, lg, neg)
    gm = jnp.max(gl, axis=-1, keepdims=True)
    gsum = jnp.sum(jnp.where(gmask, jnp.exp(gl - gm), 0.0), axis=-1, keepdims=True)
    g_top = 1.0 / gsum
    g_sel = jnp.min(jnp.where(gl == gm, lane, LANES), axis=-1, keepdims=True)
    lo = N_GROUPS + EXPERTS_PER_GROUP * g_sel
    el = jnp.where((lane >= lo) & (lane < lo + EXPERTS_PER_GROUP), lg, neg)
    m1 = jnp.max(el, axis=-1, keepdims=True)
    i1 = jnp.min(jnp.where(el == m1, lane, LANES), axis=-1, keepdims=True)
    el2 = jnp.where(lane == i1, neg, el)
    m2 = jnp.max(el2, axis=-1, keepdims=True)
    i2 = jnp.min(jnp.where(el2 == m2, lane, LANES), axis=-1, keepdims=True)
    d = jnp.exp(m2 - m1)
    p1 = 1.0 / (1.0 + d)
    p2 = d / (1.0 + d)
    e1 = (i1 - N_GROUPS).astype(F32)
    e2 = (i2 - N_GROUPS).astype(F32)
    return jnp.where(lane == 0, e1, jnp.where(lane == 1, e2, jnp.where(lane == 2, g_top * p1,
                     jnp.where(lane == 3, g_top * p2, 0.0))))


def _outproj_body(ya_ref, yh_ref, x_ref, ga_ref, gh_ref, wo_ref, bd_ref, gm_ref, wrt_ref, brt_ref,
                  x1_ref, h2_ref, rt_ref):
    ya = ya_ref[...]
    yan = ya * lax.rsqrt(_group_sumsq(ya, bd_ref[...]) * (1.0 / HEAD_DIM) + EPS) * ga_ref[...]
    yh = yh_ref[...]
    tm = yh.shape[1]
    yh3 = yh.reshape(D_HYENA // HYENA_HEAD, HYENA_HEAD, tm)
    ms = jnp.mean(yh3 * yh3, axis=1, keepdims=True)
    yhn = (yh3 * lax.rsqrt(ms + EPS)).reshape(D_HYENA, tm) * gh_ref[...]
    mix = (jnp.dot(yan.astype(BF16), wo_ref[:D_ATTN, :], preferred_element_type=F32)
           + jnp.dot(yhn.T.astype(BF16), wo_ref[D_ATTN:, :], preferred_element_type=F32))
    x1 = x_ref[...] + mix
    x1_ref[...] = x1
    h2 = x1 * lax.rsqrt(jnp.mean(x1 * x1, axis=-1, keepdims=True) + EPS) * gm_ref[...]
    h2_ref[...] = h2
    lg = jnp.dot(h2, wrt_ref[...], precision=lax.Precision.HIGHEST,
                 preferred_element_type=F32) + brt_ref[...]
    rt_ref[...] = _route_lanes(lg)


def _outproj(ya, yht, x, ga, gh, wo, bd, gm, wrt, brt):
    B, S, D = x.shape
    tm = TM_PROJ
    full = lambda a: pl.BlockSpec(a.shape, lambda b, i: (0,) * a.ndim)
    return pl.pallas_call(
        _outproj_body,
        grid=(B, S // tm),
        in_specs=[
            pl.BlockSpec((None, tm, D_ATTN), lambda b, i: (b, i, 0)),
            pl.BlockSpec((None, D_HYENA, tm), lambda b, i: (b, 0, i)),
            pl.BlockSpec((None, tm, D), lambda b, i: (b, i, 0)),
            full(ga), full(gh), full(wo), full(bd), full(gm), full(wrt), full(brt),
        ],
        out_specs=[
            pl.BlockSpec((None, tm, D), lambda b, i: (b, i, 0)),
            pl.BlockSpec((None, tm, D), lambda b, i: (b, i, 0)),
            pl.BlockSpec((None, tm, LANES), lambda b, i: (b, i, 0)),
        ],
        out_shape=[
            jax.ShapeDtypeStruct((B, S, D), F32),
            jax.ShapeDtypeStruct((B, S, D), F32),
            jax.ShapeDtypeStruct((B, S, LANES), F32),
        ],
        compiler_params=_cparams(("parallel", "parallel")),
        name="outproj",
    )(ya, yht, x, ga, gh, wo, bd, gm, wrt, brt)


def _moe_body(be_ref, x_ref, wg_ref, wu_ref, wd_ref, y_ref, wg_s, wu_s, wd_s):
    i = pl.program_id(0)
    prev = be_ref[jnp.maximum(i - 1, 0)]

    @pl.when((i == 0) | (be_ref[i] != prev))
    def _():
        wg_s[...] = wg_ref[...].astype(BF16)
        wu_s[...] = wu_ref[...].astype(BF16)
        wd_s[...] = wd_ref[...].astype(BF16)

    x = x_ref[...].astype(BF16)
    a = jnp.dot(x, wg_s[...], preferred_element_type=F32)
    b = jnp.dot(x, wu_s[...], preferred_element_type=F32)
    hmid = (a * jax.nn.sigmoid(a)) * b
    y_ref[...] = jnp.dot(hmid.astype(BF16), wd_s[...], preferred_element_type=F32)


def _moe_experts(block_e, xs, w_gate, w_up, w_down):
    n_rows, D = xs.shape
    T = TB_MOE
    grid_spec = pltpu.PrefetchScalarGridSpec(
        num_scalar_prefetch=1,
        grid=(n_rows // T,),
        in_specs=[
            pl.BlockSpec((T, D), lambda i, be: (i, 0)),
            pl.BlockSpec((None, D, D_EXPERT), lambda i, be: (be[i], 0, 0)),
            pl.BlockSpec((None, D, D_EXPERT), lambda i, be: (be[i], 0, 0)),
            pl.BlockSpec((None, D_EXPERT, D), lambda i, be: (be[i], 0, 0)),
        ],
        out_specs=pl.BlockSpec((T, D), lambda i, be: (i, 0)),
        scratch_shapes=[
            pltpu.VMEM((D, D_EXPERT), BF16), pltpu.VMEM((D, D_EXPERT), BF16), pltpu.VMEM((D_EXPERT, D), BF16),
        ],
    )
    return pl.pallas_call(
        _moe_body,
        grid_spec=grid_spec,
        out_shape=jax.ShapeDtypeStruct((n_rows, D), F32),
        compiler_params=_cparams(("arbitrary",)),
        name="moe_experts",
    )(block_e, xs, w_gate, w_up, w_down)


def _dispatch(expert, N):
    T = TB_MOE
    NK = N * TOP_K
    e_flat = expert.reshape(NK)
    experts = jnp.arange(N_EXPERTS, dtype=jnp.int32)
    order = jnp.argsort(e_flat).astype(jnp.int32)
    tok_sorted = order // TOP_K
    onehot = (e_flat[:, None] == experts[None]).astype(jnp.int32)
    counts = jnp.sum(onehot, axis=0)
    ends = jnp.cumsum(counts)
    starts = ends - counts
    padded = (counts + T - 1) // T * T
    pends = jnp.cumsum(padded)
    pstarts = pends - padded
    n_rows = -(-(NK + N_EXPERTS * (T - 1)) // T) * T
    n_blocks = n_rows // T
    blk_start = jnp.arange(n_blocks, dtype=jnp.int32) * T
    block_e = jnp.clip(jnp.sum((pends[None, :] <= blk_start[:, None]).astype(jnp.int32), axis=1),
                       0, N_EXPERTS - 1)
    oh_b = (block_e[:, None] == experts[None]).astype(jnp.int32)
    base = jnp.sum(oh_b * (starts - pstarts)[None], axis=1) + blk_start
    end_b = jnp.sum(oh_b * ends[None], axis=1)
    src = base[:, None] + jnp.arange(T, dtype=jnp.int32)[None]
    row_tok = jnp.where(src < end_b[:, None], tok_sorted[jnp.clip(src, 0, NK - 1)], N).reshape(n_rows)
    inv = jnp.argsort(order).astype(jnp.int32)
    pos = (inv + jnp.sum(onehot * (pstarts - starts)[None], axis=1)).reshape(N, TOP_K)
    return row_tok, block_e.astype(jnp.int32), pos


def _final_body(x1_ref, y0_ref, y1_ref, w0_ref, w1_ref, p_ref, gp_ref, wg_ref, bg_ref, wp_ref, gf_ref, o_ref):
    x2 = x1_ref[...] + (y0_ref[...] * w0_ref[...] + y1_ref[...] * w1_ref[...])
    hp = x2 * lax.rsqrt(jnp.mean(x2 * x2, axis=-1, keepdims=True) + EPS) * gp_ref[...]
    gate = jax.nn.sigmoid(jnp.dot(hp.astype(BF16), wg_ref[...], preferred_element_type=F32) + bg_ref[...])
    pe = jnp.dot(p_ref[...].astype(BF16), wp_ref[...], preferred_element_type=F32)
    x3 = x2 + pe * gate
    o_ref[...] = x3 * lax.rsqrt(jnp.mean(x3 * x3, axis=-1, keepdims=True) + EPS) * gf_ref[...]


def _final(x1, y0, y1, w0, w1, p, gp, wg, bg, wp, gf):
    N, D = x1.shape
    tm = TM_PROJ
    row = lambda w: pl.BlockSpec((tm, w), lambda i: (i, 0))
    full = lambda a: pl.BlockSpec(a.shape, lambda i: (0,) * a.ndim)
    return pl.pallas_call(
        _final_body,
        grid=(N // tm,),
        in_specs=[row(D), row(D), row(D), row(1), row(1), row(p.shape[1]),
                  full(gp), full(wg), full(bg), full(wp), full(gf)],
        out_specs=row(D),
        out_shape=jax.ShapeDtypeStruct((N, D), F32),
        compiler_params=_cparams(("parallel",)),
        name="ple_final",
    )(x1, y0, y1, w0, w1, p, gp, wg, bg, wp, gf)


def kernel(x, p, g_mix, w_in, q_gain, k_gain, conv_w, conv_b, w_f1, b_f1, freq1, w_f2, b_f2, freq2, w_f3, filt_bias, g_attn_out, g_hyena_out, w_out, g_moe, w_group, b_group, w_router, b_router, w_gate, w_up, w_down, g_ple, w_ple_gate, b_ple_gate, w_ple, g_final):
    B, S, D = x.shape
    N = B * S
    assert p.shape[0] == 1 and S == (FFT_N1 // 2) * FFT_N2 and B % 2 == 0
    i = 0
    cst = _dft_constants()
    cos, sin = _rope_tables(S)
    bd = _block_diag_ones(D_ATTN, HEAD_DIM)

    n_qkv = D_ATTN + 2 * D_KV
    wqkv = w_in[i][:, :n_qkv].astype(BF16)
    wut = w_in[i][:, n_qkv:].T.astype(BF16)
    q, kt, v, ut = _inproj(x, g_mix[i][None], wqkv, wut, bd,
                           jnp.tile(q_gain[i], N_HEADS)[None], jnp.tile(k_gain[i], N_KV_HEADS)[None], cos, sin)

    zeros = jnp.zeros((B, HEAD_DIM, S), BF16)
    k0, k1 = kt[:, :HEAD_DIM], kt[:, HEAD_DIM:]
    kw = jnp.stack([jnp.stack([jnp.concatenate([k0, zeros], 1), jnp.concatenate([zeros, k0], 1)], 1),
                    jnp.stack([jnp.concatenate([k1, zeros], 1), jnp.concatenate([zeros, k1], 1)], 1)], 1)
    vw = jnp.stack([v, jnp.concatenate([v[..., HEAD_DIM:], v[..., :HEAD_DIM]], -1)], 1)
    ya = _attention(q, kw, vw)

    circ = _filter_time_domain(S, w_f1[i], b_f1[i], freq1[i], w_f2[i], b_f2[i], freq2[i], w_f3[i])
    hspec = _filter_fft(circ.reshape(2 * D_HYENA, FFT_N1, FFT_N2), cst)
    hspec = hspec.reshape(2, D_HYENA, FFT_N1, 2 * LANES)
    du = ut.shape[1]
    u4 = ut.reshape(B, du, S // LANES, LANES)
    par_u = jnp.broadcast_to(jnp.concatenate([conv_w[i], conv_b[i][None]], 0)[:, :, None], (4, du, LANES))
    fb = jnp.broadcast_to(filt_bias[i][:, :, None], (2, D_HYENA, LANES))
    yh4 = _hyena(u4, par_u, fb, hspec, cst)
    yht = yh4.reshape(B, D_HYENA, S)

    wrt = jnp.zeros((D, LANES), F32).at[:, :N_GROUPS].set(w_group[i]).at[:, N_GROUPS:N_GROUPS + N_EXPERTS].set(w_router[i])
    brt = jnp.zeros((1, LANES), F32).at[0, :N_GROUPS].set(b_group[i]).at[0, N_GROUPS:N_GROUPS + N_EXPERTS].set(b_router[i])
    x1, h2, route = _outproj(ya, yht, x, g_attn_out[i][None], g_hyena_out[i][:, None], w_out[i].astype(BF16),
                             bd, g_moe[i][None], wrt, brt)

    route = route.reshape(N, LANES)
    expert = route[:, :TOP_K].astype(jnp.int32)
    row_tok, block_e, pos = _dispatch(expert, N)
    h2p = jnp.concatenate([h2.reshape(N, D), jnp.zeros((1, D), F32)], axis=0)
    xs = h2p[row_tok]
    yb = _moe_experts(block_e, xs, w_gate[i], w_up[i], w_down[i])
    y0 = yb[pos[:, 0]]
    y1 = yb[pos[:, 1]]

    out = _final(x1.reshape(N, D), y0, y1, route[:, 2:3], route[:, 3:4], p[i].reshape(N, -1), g_ple[i][None],
                 w_ple_gate[i].astype(BF16), b_ple_gate[i][None], w_ple[i].astype(BF16), g_final[None])
    return out.reshape(B, S, D)
```

```python
import functools
import math

import numpy as np
import jax
import jax.numpy as jnp
from jax import lax
from jax.experimental import pallas as pl
from jax.experimental.pallas import tpu as pltpu

F32 = jnp.float32
BF16 = jnp.bfloat16

D_MODEL = 1024
EPS = 1e-6
GRID_W = 64
N_HEADS = 8
N_KV_HEADS = 2
HEAD_DIM = 64
D_ATTN = N_HEADS * HEAD_DIM
D_KV = N_KV_HEADS * HEAD_DIM
ROPE_THETA = 10000.0
D_HYENA = 512
HYENA_HEAD = 64
FILTER_EMB = 33
FAST_DECAY_PCT = 0.3
SLOW_DECAY_PCT = 1.5
DECAY_TARGET = 1e-2
N_GROUPS = 4
EXPERTS_PER_GROUP = 8
N_EXPERTS = N_GROUPS * EXPERTS_PER_GROUP
TOP_K = 2
D_EXPERT = 512

LANES = 128
FFT_N1 = 64
FFT_N2 = 128
VMEM_LIMIT = 56 * 1024 * 1024

TM_PROJ = 512
TQ_ATTN = 256
C_HY = 32
SEQ_UNROLL = 8
TB_MOE = 256


def _cparams(sem):
    return pltpu.CompilerParams(dimension_semantics=sem, vmem_limit_bytes=VMEM_LIMIT)


def _rope_tables(S):
    half = HEAD_DIM // 2
    t = jnp.arange(S, dtype=F32)
    r_idx = jnp.floor(t / GRID_W)
    c_idx = t - r_idx * GRID_W
    inv = ROPE_THETA ** (-jnp.arange(0, half, 2, dtype=F32) / half)
    ang_r = r_idx[:, None] * inv[None]
    ang_c = c_idx[:, None] * inv[None]
    cos_h = jnp.concatenate([jnp.cos(ang_r), jnp.cos(ang_r), jnp.cos(ang_c), jnp.cos(ang_c)], axis=-1)
    sin_h = jnp.concatenate([-jnp.sin(ang_r), jnp.sin(ang_r), -jnp.sin(ang_c), jnp.sin(ang_c)], axis=-1)
    return jnp.tile(cos_h, (1, 2)), jnp.tile(sin_h, (1, 2))


def _dft_constants():
    n1, n2 = FFT_N1, FFT_N2
    n = n1 * n2
    a = np.arange(n1)
    ang = 2.0 * np.pi * np.outer(a, a) / n1
    far, fai = np.cos(ang), -np.sin(ang)
    hlf = n1 // 2
    ma = np.block([[far[:, :hlf], -fai[:, :hlf]], [fai[:, :hlf], far[:, :hlf]]])
    maf = np.concatenate([far, fai], axis=0)
    b = np.arange(n2)
    angt = 2.0 * np.pi * np.outer(a, b) / n
    tw = np.concatenate([np.cos(angt), -np.sin(angt)], axis=1)
    angb = 2.0 * np.pi * np.outer(b, b) / n2
    fbr, fbi = np.cos(angb), -np.sin(angb)
    g = np.block([[fbr, fbi], [-fbi, fbr]])
    ginv = np.block([[fbr, -fbi], [fbi, fbr]])
    minv_r = np.concatenate([far[:hlf], -fai[:hlf]], axis=0) / n
    minv_i = np.concatenate([fai[:hlf], far[:hlf]], axis=0) / n
    f = lambda m: jnp.asarray(m.astype(np.float32))
    return dict(ma=f(ma), maf=f(maf), tw=f(tw), g=f(g), ginv=f(ginv), minv_r=f(minv_r), minv_i=f(minv_i))


def _block_diag_ones(width, group):
    i = np.arange(width) // group
    return jnp.asarray((i[:, None] == i[None, :]).astype(np.float32)).astype(BF16)


def _group_sumsq(a, bd):
    sq = a * a
    hi = sq.astype(BF16)
    lo = (sq - hi.astype(F32)).astype(BF16)
    return (jnp.dot(hi, bd, preferred_element_type=F32) + jnp.dot(lo, bd, preferred_element_type=F32))


def _head_norm_rope(a, gain, bd, cos, sin):
    width = a.shape[-1]
    n = a * lax.rsqrt(_group_sumsq(a, bd) * (1.0 / HEAD_DIM) + EPS) * gain
    rep = width // LANES
    if rep > 1:
        cos = jnp.concatenate([cos] * rep, axis=-1)
        sin = jnp.concatenate([sin] * rep, axis=-1)
    fwd = pltpu.roll(n, width - 16, 1)
    bwd = pltpu.roll(n, 16, 1)
    lane = lax.broadcasted_iota(jnp.int32, n.shape, 1)
    sw = jnp.where((lane % 32) < 16, fwd, bwd)
    return n * cos + sw * sin


def _inproj_body(x_ref, g_ref, wqkv_ref, wu_ref, bd_ref, qg_ref, kg_ref, cos_ref, sin_ref,
                 q_ref, kt_ref, v_ref, ut_ref):
    x = x_ref[...]
    h = x * lax.rsqrt(jnp.mean(x * x, axis=-1, keepdims=True) + EPS) * g_ref[...]
    hb = h.astype(BF16)
    qkv = jnp.dot(hb, wqkv_ref[...], preferred_element_type=F32)
    cos = cos_ref[...]
    sin = sin_ref[...]
    bd = bd_ref[...]
    q = _head_norm_rope(qkv[:, :D_ATTN], qg_ref[...], bd, cos, sin)
    q_ref[...] = (q * (HEAD_DIM ** -0.5)).astype(BF16)
    k = _head_norm_rope(qkv[:, D_ATTN:D_ATTN + D_KV], kg_ref[...], bd[:D_KV, :D_KV], cos, sin)
    kt_ref[...] = k.T.astype(BF16)
    v_ref[...] = qkv[:, D_ATTN + D_KV:].astype(BF16)
    ut_ref[...] = lax.dot_general(wu_ref[...], hb, (((1,), (1,)), ((), ())),
                                  preferred_element_type=F32)


def _inproj(x, g_mix, wqkv, wut, bd, qg, kg, cos, sin):
    B, S, D = x.shape
    tm = TM_PROJ
    du = wut.shape[0]
    full = lambda shape: pl.BlockSpec(shape, lambda b, i: (0,) * len(shape))
    return pl.pallas_call(
        _inproj_body,
        grid=(B, S // tm),
        in_specs=[
            pl.BlockSpec((None, tm, D), lambda b, i: (b, i, 0)),
            full((1, D)), full(wqkv.shape), full(wut.shape), full(bd.shape),
            full((1, D_ATTN)), full((1, D_KV)),
            pl.BlockSpec((tm, LANES), lambda b, i: (i, 0)),
            pl.BlockSpec((tm, LANES), lambda b, i: (i, 0)),
        ],
        out_specs=[
            pl.BlockSpec((None, tm, D_ATTN), lambda b, i: (b, i, 0)),
            pl.BlockSpec((None, D_KV, tm), lambda b, i: (b, 0, i)),
            pl.BlockSpec((None, tm, D_KV), lambda b, i: (b, i, 0)),
            pl.BlockSpec((None, du, tm), lambda b, i: (b, 0, i)),
        ],
        out_shape=[
            jax.ShapeDtypeStruct((B, S, D_ATTN), BF16),
            jax.ShapeDtypeStruct((B, D_KV, S), BF16),
            jax.ShapeDtypeStruct((B, S, D_KV), BF16),
            jax.ShapeDtypeStruct((B, du, S), F32),
        ],
        compiler_params=_cparams(("parallel", "parallel")),
        name="inproj",
    )(x, g_mix, wqkv, wut, bd, qg, kg, cos, sin)


def _attn_body(q_ref, ke_ref, ko_ref, ve_ref, vo_ref, o_ref):
    q = q_ref[...]

    def one_head(kw_ref, vw_ref):
        s = jnp.dot(q, kw_ref[...], preferred_element_type=F32)
        m = jnp.max(s, axis=-1, keepdims=True)
        p = jnp.exp(s - m)
        l = jnp.sum(p, axis=-1, keepdims=True)
        o = jnp.dot(p.astype(BF16), vw_ref[...], preferred_element_type=F32)
        return o / l

    oe = one_head(ke_ref, ve_ref)
    oo = one_head(ko_ref, vo_ref)
    lane = lax.broadcasted_iota(jnp.int32, oe.shape, 1)
    o_ref[...] = jnp.where(lane < HEAD_DIM, oe, oo)


def _attention(q, kw, vw):
    B, S, _ = q.shape
    tq = TQ_ATTN
    n_pairs = D_ATTN // LANES
    return pl.pallas_call(
        _attn_body,
        grid=(B, S // tq, n_pairs),
        in_specs=[
            pl.BlockSpec((None, tq, LANES), lambda b, i, j: (b, i, j)),
            pl.BlockSpec((None, None, None, LANES, S), lambda b, i, j: (b, j // 2, 0, 0, 0)),
            pl.BlockSpec((None, None, None, LANES, S), lambda b, i, j: (b, j // 2, 1, 0, 0)),
            pl.BlockSpec((None, None, S, LANES), lambda b, i, j: (b, j // 2, 0, 0)),
            pl.BlockSpec((None, None, S, LANES), lambda b, i, j: (b, 1 - j // 2, 0, 0)),
        ],
        out_specs=pl.BlockSpec((None, tq, LANES), lambda b, i, j: (b, i, j)),
        out_shape=jax.ShapeDtypeStruct((B, S, D_ATTN), F32),
        compiler_params=_cparams(("parallel", "parallel", "arbitrary")),
        name="attention",
    )(q, kw, kw, vw, vw)


def _fwd_twiddle_store(y, tw_ref, s1_ref, row0):
    yr, yi = y[:FFT_N1], y[FFT_N1:]
    twr, twi = tw_ref[:, :LANES], tw_ref[:, LANES:]
    s1_ref[pl.ds(row0, FFT_N1), :LANES] = (yr * twr - yi * twi).astype(BF16)
    s1_ref[pl.ds(row0, FFT_N1), LANES:] = (yr * twi + yi * twr).astype(BF16)


def _filtfft_body(x_ref, maf_ref, tw_ref, g_ref, h_ref, s1_ref):
    C = x_ref.shape[0]

    def step_a(c, carry):
        y = jnp.dot(maf_ref[...], x_ref[c].astype(BF16), preferred_element_type=F32)
        _fwd_twiddle_store(y, tw_ref, s1_ref, pl.multiple_of(c * FFT_N1, FFT_N1))
        return carry

    lax.fori_loop(0, C, step_a, 0, unroll=SEQ_UNROLL)
    z = jnp.dot(s1_ref[...], g_ref[...], preferred_element_type=F32)
    h_ref[...] = z.reshape(C, FFT_N1, 2 * LANES)


def _filter_fft(circ, cst):
    n_seq = circ.shape[0]
    C = C_HY
    full = lambda a: pl.BlockSpec(a.shape, lambda i: (0,) * a.ndim)
    maf, tw, g = cst["maf"].astype(BF16), cst["tw"], cst["g"].astype(BF16)
    return pl.pallas_call(
        _filtfft_body,
        grid=(n_seq // C,),
        in_specs=[pl.BlockSpec((C, FFT_N1, FFT_N2), lambda i: (i, 0, 0)), full(maf), full(tw), full(g)],
        out_specs=pl.BlockSpec((C, FFT_N1, 2 * LANES), lambda i: (i, 0, 0)),
        out_shape=jax.ShapeDtypeStruct((n_seq, FFT_N1, 2 * LANES), F32),
        scratch_shapes=[pltpu.VMEM((C * FFT_N1, 2 * LANES), BF16)],
        compiler_params=_cparams(("parallel",)),
        name="filter_fft",
    )(circ, maf, tw, g)


def _short_conv(x, par_ref, c):
    rows, lanes = x.shape
    a_i = lax.broadcasted_iota(jnp.int32, x.shape, 0)
    b_i = lax.broadcasted_iota(jnp.int32, x.shape, 1)
    l1 = pltpu.roll(x, 1, 1)
    l2 = pltpu.roll(l1, 1, 0)
    prev = jnp.where(b_i == 0, l2, l1)
    prev = jnp.where((a_i == 0) & (b_i == 0), 0.0, prev)
    r1 = pltpu.roll(x, lanes - 1, 1)
    r2 = pltpu.roll(r1, rows - 1, 0)
    nxt = jnp.where(b_i == lanes - 1, r2, r1)
    nxt = jnp.where((a_i == rows - 1) & (b_i == lanes - 1), 0.0, nxt)
    w0 = par_ref[0, pl.ds(c, 1), :]
    w1 = par_ref[1, pl.ds(c, 1), :]
    w2 = par_ref[2, pl.ds(c, 1), :]
    cb = par_ref[3, pl.ds(c, 1), :]
    return cb + prev * w0 + x * w1 + nxt * w2


def _hyena_body(v_ref, x1_ref, x2_ref, pv_ref, p1_ref, p2_ref, fb_ref, h_ref,
                ma_ref, tw_ref, g_ref, ginv_ref, mir_ref, mii_ref,
                o_ref, s1_ref, s2_ref, vc_ref, z1_ref):
    C = v_ref.shape[1]
    half = FFT_N1 // 2

    def spectral(order):
        z = jnp.dot(s1_ref[...], g_ref[...], preferred_element_type=F32)
        hs = h_ref[order].reshape(C * FFT_N1, 2 * LANES)
        zr, zi = z[:, :LANES], z[:, LANES:]
        hr, hi = hs[:, :LANES], hs[:, LANES:]
        pb = jnp.concatenate([zr * hr - zi * hi, zr * hi + zi * hr], axis=1).astype(BF16)
        s2_ref[...] = jnp.dot(pb, ginv_ref[...], preferred_element_type=F32)

    def inv_a(c):
        row0 = pl.multiple_of(c * FFT_N1, FFT_N1)
        y = s2_ref[pl.ds(row0, FFT_N1), :]
        yr, yi = y[:, :LANES], y[:, LANES:]
        twr, twi = tw_ref[:, :LANES], tw_ref[:, LANES:]
        ur = (yr * twr + yi * twi).astype(BF16)
        ui = (yi * twr - yr * twi).astype(BF16)
        out = (jnp.dot(mir_ref[...], ur, preferred_element_type=F32)
               + jnp.dot(mii_ref[...], ui, preferred_element_type=F32))
        return out[:half], out[half:]

    def fwd_a(c, xr, xi):
        xs = jnp.concatenate([xr, xi], axis=0).astype(BF16)
        y = jnp.dot(ma_ref[...], xs, preferred_element_type=F32)
        _fwd_twiddle_store(y, tw_ref, s1_ref, pl.multiple_of(c * FFT_N1, FFT_N1))

    def pass1_a(c, carry):
        vr = _short_conv(v_ref[0, c], pv_ref, c)
        vi = _short_conv(v_ref[1, c], pv_ref, c)
        vc_ref[0, c] = vr
        vc_ref[1, c] = vi
        fwd_a(c, vr, vi)
        return carry

    def pass1_b(c, carry):
        cr, ci = inv_a(c)
        bias = fb_ref[0, pl.ds(c, 1), :]
        zr = _short_conv(x1_ref[0, c], p1_ref, c) * (cr + bias * vc_ref[0, c])
        zi = _short_conv(x1_ref[1, c], p1_ref, c) * (ci + bias * vc_ref[1, c])
        z1_ref[0, c] = zr
        z1_ref[1, c] = zi
        fwd_a(c, zr, zi)
        return carry

    def pass2_b(c, carry):
        cr, ci = inv_a(c)
        bias = fb_ref[1, pl.ds(c, 1), :]
        o_ref[0, c] = _short_conv(x2_ref[0, c], p2_ref, c) * (cr + bias * z1_ref[0, c])
        o_ref[1, c] = _short_conv(x2_ref[1, c], p2_ref, c) * (ci + bias * z1_ref[1, c])
        return carry

    lax.fori_loop(0, C, pass1_a, 0, unroll=SEQ_UNROLL)
    spectral(0)
    lax.fori_loop(0, C, pass1_b, 0, unroll=SEQ_UNROLL)
    spectral(1)
    lax.fori_loop(0, C, pass2_b, 0, unroll=SEQ_UNROLL)


def _hyena(u4, par_u, fb, hspec, cst):
    B = u4.shape[0]
    C = C_HY
    J = D_HYENA // C
    rows = u4.shape[2]
    full = lambda a: pl.BlockSpec(a.shape, lambda j, p: (0,) * a.ndim)
    ma, g, ginv = cst["ma"].astype(BF16), cst["g"].astype(BF16), cst["ginv"].astype(BF16)
    mir, mii = cst["minv_r"].astype(BF16), cst["minv_i"].astype(BF16)
    tw = cst["tw"]
    u_spec = lambda k: pl.BlockSpec((2, C, rows, LANES), lambda j, p, k=k: (p, j + k * J, 0, 0))
    par_spec = lambda k: pl.BlockSpec((4, C, LANES), lambda j, p, k=k: (0, j + k * J, 0))
    return pl.pallas_call(
        _hyena_body,
        grid=(J, B // 2),
        in_specs=[
            u_spec(0), u_spec(1), u_spec(2), par_spec(0), par_spec(1), par_spec(2),
            pl.BlockSpec((2, C, LANES), lambda j, p: (0, j, 0)),
            pl.BlockSpec((2, C, FFT_N1, 2 * LANES), lambda j, p: (0, j, 0, 0)),
            full(ma), full(tw), full(g), full(ginv), full(mir), full(mii),
        ],
        out_specs=pl.BlockSpec((2, C, rows, LANES), lambda j, p: (p, j, 0, 0)),
        out_shape=jax.ShapeDtypeStruct((B, D_HYENA, rows, LANES), F32),
        scratch_shapes=[
            pltpu.VMEM((C * FFT_N1, 2 * LANES), BF16),
            pltpu.VMEM((C * FFT_N1, 2 * LANES), F32),
            pltpu.VMEM((2, C, rows, LANES), F32),
            pltpu.VMEM((2, C, rows, LANES), F32),
        ],
        compiler_params=_cparams(("parallel", "arbitrary")),
        name="hyena",
    )(u4, u4, u4, par_u, par_u, par_u, fb, hspec, ma, tw, g, ginv, mir, mii)


def _filter_time_domain(L, w_f1, b_f1, freq1, w_f2, b_f2, freq2, w_f3):
    hp = lax.Precision.HIGHEST
    bands = (FILTER_EMB - 1) // 2
    t = jnp.linspace(0.0, 1.0, L, dtype=F32)[:, None]
    w = (2.0 * math.pi / L) * jnp.arange(L, dtype=F32)[:, None]
    f = jnp.linspace(1e-4, bands - 1, bands, dtype=F32)[None]
    zf = f * w
    z = jnp.concatenate([t, jnp.cos(zf), -jnp.sin(zf)], axis=-1)
    back = lambda a: jnp.roll(a[::-1], 1, axis=0)
    zz = jnp.concatenate([z, back(z)], axis=0)
    h = jnp.sin(freq1 * (jnp.dot(zz, w_f1, precision=hp) + b_f1))
    h = jnp.sin(freq2 * (jnp.dot(h, w_f2, precision=hp) + b_f2))
    w3 = w_f3.reshape(-1, 2, 2, D_HYENA)
    hf = jnp.einsum("hoc,lh->ocl", w3[:, :, 0], h[:L], precision=hp)
    hb = jnp.einsum("hoc,lh->ocl", w3[:, :, 1], h[L:], precision=hp)
    max_decay = math.log(DECAY_TARGET) / FAST_DECAY_PCT
    min_decay = math.log(DECAY_TARGET) / SLOW_DECAY_PCT
    deltas = jnp.linspace(min_decay, max_decay, D_HYENA, dtype=F32)
    decay_f = jnp.exp(-jnp.abs(deltas)[:, None] * t[None, :, 0])
    decay_b = jnp.exp(-jnp.abs(deltas)[:, None] * back(t)[None, :, 0])
    hf = hf * decay_f[None]
    hb = hb * decay_b[None]
    hf = hf / (jnp.sum(jnp.abs(hf), axis=-1, keepdims=True) + EPS)
    hb = hb / (jnp.sum(jnp.abs(hb), axis=-1, keepdims=True) + EPS)
    first = lax.broadcasted_iota(jnp.int32, hb.shape, 2) == 0
    return jnp.concatenate([hf + jnp.where(first, hb, 0.0), jnp.where(first, 0.0, hb)], axis=-1)


def _route_lanes(lg):
    neg = -1e30
    lane = lax.broadcasted_iota(jnp.int32, lg.shape, 1)
    gmask = lane < N_GROUPS
    gl = jnp.where(gmask, lg, neg)
    gm = jnp.max(gl, axis=-1, keepdims=True)
    gsum = jnp.sum(jnp.where(gmask, jnp.exp(gl - gm), 0.0), axis=-1, keepdims=True)
    g_top = 1.0 / gsum
    g_sel = jnp.min(jnp.where(gl == gm, lane, LANES), axis=-1, keepdims=True)
    lo = N_GROUPS + EXPERTS_PER_GROUP * g_sel
    el = jnp.where((lane >= lo) & (lane < lo + EXPERTS_PER_GROUP), lg, neg)
    m1 = jnp.max(el, axis=-1, keepdims=True)
    i1 = jnp.min(jnp.where(el == m1, lane, LANES), axis=-1, keepdims=True)
    el2 = jnp.where(lane == i1, neg, el)
    m2 = jnp.max(el2, axis=-1, keepdims=True)
    i2 = jnp.min(jnp.where(el2 == m2, lane, LANES), axis=-1, keepdims=True)
    d = jnp.exp(m2 - m1)
    p1 = 1.0 / (1.0 + d)
    p2 = d / (1.0 + d)
    e1 = (i1 - N_GROUPS).astype(F32)
    e2 = (i2 - N_GROUPS).astype(F32)
    return jnp.where(lane == 0, e1, jnp.where(lane == 1, e2, jnp.where(lane == 2, g_top * p1,
                     jnp.where(lane == 3, g_top * p2, 0.0))))


def _outproj_body(ya_ref, yh_ref, x_ref, ga_ref, gh_ref, wo_ref, bd_ref, gm_ref, wrt_ref, brt_ref,
                  x1_ref, h2_ref, rt_ref):
    ya = ya_ref[...]
    yan = ya * lax.rsqrt(_group_sumsq(ya, bd_ref[...]) * (1.0 / HEAD_DIM) + EPS) * ga_ref[...]
    yh = yh_ref[...]
    tm = yh.shape[1]
    yh3 = yh.reshape(D_HYENA // HYENA_HEAD, HYENA_HEAD, tm)
    ms = jnp.mean(yh3 * yh3, axis=1, keepdims=True)
    yhn = (yh3 * lax.rsqrt(ms + EPS)).reshape(D_HYENA, tm) * gh_ref[...]
    mix = (jnp.dot(yan.astype(BF16), wo_ref[:D_ATTN, :], preferred_element_type=F32)
           + jnp.dot(yhn.T.astype(BF16), wo_ref[D_ATTN:, :], preferred_element_type=F32))
    x1 = x_ref[...] + mix
    x1_ref[...] = x1
    h2 = x1 * lax.rsqrt(jnp.mean(x1 * x1, axis=-1, keepdims=True) + EPS) * gm_ref[...]
    h2_ref[...] = h2
    lg = jnp.dot(h2, wrt_ref[...], precision=lax.Precision.HIGHEST,
                 preferred_element_type=F32) + brt_ref[...]
    rt_ref[...] = _route_lanes(lg)


def _outproj(ya, yht, x, ga, gh, wo, bd, gm, wrt, brt):
    B, S, D = x.shape
    tm = TM_PROJ
    full = lambda a: pl.BlockSpec(a.shape, lambda b, i: (0,) * a.ndim)
    return pl.pallas_call(
        _outproj_body,
        grid=(B, S // tm),
        in_specs=[
            pl.BlockSpec((None, tm, D_ATTN), lambda b, i: (b, i, 0)),
            pl.BlockSpec((None, D_HYENA, tm), lambda b, i: (b, 0, i)),
            pl.BlockSpec((None, tm, D), lambda b, i: (b, i, 0)),
            full(ga), full(gh), full(wo), full(bd), full(gm), full(wrt), full(brt),
        ],
        out_specs=[
            pl.BlockSpec((None, tm, D), lambda b, i: (b, i, 0)),
            pl.BlockSpec((None, tm, D), lambda b, i: (b, i, 0)),
            pl.BlockSpec((None, tm, LANES), lambda b, i: (b, i, 0)),
        ],
        out_shape=[
            jax.ShapeDtypeStruct((B, S, D), F32),
            jax.ShapeDtypeStruct((B, S, D), F32),
            jax.ShapeDtypeStruct((B, S, LANES), F32),
        ],
        compiler_params=_cparams(("parallel", "parallel")),
        name="outproj",
    )(ya, yht, x, ga, gh, wo, bd, gm, wrt, brt)


def _moe_body(be_ref, x_ref, wg_ref, wu_ref, wd_ref, y_ref, wg_s, wu_s, wd_s):
    i = pl.program_id(0)
    prev = be_ref[jnp.maximum(i - 1, 0)]

    @pl.when((i == 0) | (be_ref[i] != prev))
    def _():
        wg_s[...] = wg_ref[...].astype(BF16)
        wu_s[...] = wu_ref[...].astype(BF16)
        wd_s[...] = wd_ref[...].astype(BF16)

    x = x_ref[...].astype(BF16)
    a = jnp.dot(x, wg_s[...], preferred_element_type=F32)
    b = jnp.dot(x, wu_s[...], preferred_element_type=F32)
    hmid = (a * jax.nn.sigmoid(a)) * b
    y_ref[...] = jnp.dot(hmid.astype(BF16), wd_s[...], preferred_element_type=F32)


def _moe_experts(block_e, xs, w_gate, w_up, w_down):
    n_rows, D = xs.shape
    T = TB_MOE
    grid_spec = pltpu.PrefetchScalarGridSpec(
        num_scalar_prefetch=1,
        grid=(n_rows // T,),
        in_specs=[
            pl.BlockSpec((T, D), lambda i, be: (i, 0)),
            pl.BlockSpec((None, D, D_EXPERT), lambda i, be: (be[i], 0, 0)),
            pl.BlockSpec((None, D, D_EXPERT), lambda i, be: (be[i], 0, 0)),
            pl.BlockSpec((None, D_EXPERT, D), lambda i, be: (be[i], 0, 0)),
        ],
        out_specs=pl.BlockSpec((T, D), lambda i, be: (i, 0)),
        scratch_shapes=[
            pltpu.VMEM((D, D_EXPERT), BF16), pltpu.VMEM((D, D_EXPERT), BF16), pltpu.VMEM((D_EXPERT, D), BF16),
        ],
    )
    return pl.pallas_call(
        _moe_body,
        grid_spec=grid_spec,
        out_shape=jax.ShapeDtypeStruct((n_rows, D), F32),
        compiler_params=_cparams(("arbitrary",)),
        name="moe_experts",
    )(block_e, xs, w_gate, w_up, w_down)


def _dispatch(expert, N):
    T = TB_MOE
    NK = N * TOP_K
    e_flat = expert.reshape(NK)
    experts = jnp.arange(N_EXPERTS, dtype=jnp.int32)
    order = jnp.argsort(e_flat).astype(jnp.int32)
    tok_sorted = order // TOP_K
    onehot = (e_flat[:, None] == experts[None]).astype(jnp.int32)
    counts = jnp.sum(onehot, axis=0)
    ends = jnp.cumsum(counts)
    starts = ends - counts
    padded = (counts + T - 1) // T * T
    pends = jnp.cumsum(padded)
    pstarts = pends - padded
    n_rows = -(-(NK + N_EXPERTS * (T - 1)) // T) * T
    n_blocks = n_rows // T
    blk_start = jnp.arange(n_blocks, dtype=jnp.int32) * T
    block_e = jnp.clip(jnp.sum((pends[None, :] <= blk_start[:, None]).astype(jnp.int32), axis=1),
                       0, N_EXPERTS - 1)
    oh_b = (block_e[:, None] == experts[None]).astype(jnp.int32)
    base = jnp.sum(oh_b * (starts - pstarts)[None], axis=1) + blk_start
    end_b = jnp.sum(oh_b * ends[None], axis=1)
    src = base[:, None] + jnp.arange(T, dtype=jnp.int32)[None]
    filler = (blk_start[:, None] + jnp.arange(T, dtype=jnp.int32)[None]) % N
    row_tok = jnp.where(src < end_b[:, None], tok_sorted[jnp.clip(src, 0, NK - 1)], filler).reshape(n_rows)
    inv = jnp.argsort(order).astype(jnp.int32)
    pos = (inv + jnp.sum(onehot * (pstarts - starts)[None], axis=1)).reshape(N, TOP_K)
    return row_tok, block_e.astype(jnp.int32), pos


def _final_body(x1_ref, y0_ref, y1_ref, w0_ref, w1_ref, p_ref, gp_ref, wg_ref, bg_ref, wp_ref, gf_ref, o_ref):
    x2 = x1_ref[...] + (y0_ref[...] * w0_ref[...] + y1_ref[...] * w1_ref[...])
    hp = x2 * lax.rsqrt(jnp.mean(x2 * x2, axis=-1, keepdims=True) + EPS) * gp_ref[...]
    gate = jax.nn.sigmoid(jnp.dot(hp.astype(BF16), wg_ref[...], preferred_element_type=F32) + bg_ref[...])
    pe = jnp.dot(p_ref[...].astype(BF16), wp_ref[...], preferred_element_type=F32)
    x3 = x2 + pe * gate
    o_ref[...] = x3 * lax.rsqrt(jnp.mean(x3 * x3, axis=-1, keepdims=True) + EPS) * gf_ref[...]


def _final(x1, y0, y1, w0, w1, p, gp, wg, bg, wp, gf):
    N, D = x1.shape
    tm = TM_PROJ
    row = lambda w: pl.BlockSpec((tm, w), lambda i: (i, 0))
    full = lambda a: pl.BlockSpec(a.shape, lambda i: (0,) * a.ndim)
    return pl.pallas_call(
        _final_body,
        grid=(N // tm,),
        in_specs=[row(D), row(D), row(D), row(1), row(1), row(p.shape[1]),
                  full(gp), full(wg), full(bg), full(wp), full(gf)],
        out_specs=row(D),
        out_shape=jax.ShapeDtypeStruct((N, D), F32),
        compiler_params=_cparams(("parallel",)),
        name="ple_final",
    )(x1, y0, y1, w0, w1, p, gp, wg, bg, wp, gf)


def kernel(x, p, g_mix, w_in, q_gain, k_gain, conv_w, conv_b, w_f1, b_f1, freq1, w_f2, b_f2, freq2, w_f3, filt_bias, g_attn_out, g_hyena_out, w_out, g_moe, w_group, b_group, w_router, b_router, w_gate, w_up, w_down, g_ple, w_ple_gate, b_ple_gate, w_ple, g_final):
    B, S, D = x.shape
    N = B * S
    assert p.shape[0] == 1 and S == (FFT_N1 // 2) * FFT_N2 and B % 2 == 0
    i = 0
    cst = _dft_constants()
    cos, sin = _rope_tables(S)
    bd = _block_diag_ones(D_ATTN, HEAD_DIM)

    n_qkv = D_ATTN + 2 * D_KV
    wqkv = w_in[i][:, :n_qkv].astype(BF16)
    wut = w_in[i][:, n_qkv:].T.astype(BF16)
    q, kt, v, ut = _inproj(x, g_mix[i][None], wqkv, wut, bd,
                           jnp.tile(q_gain[i], N_HEADS)[None], jnp.tile(k_gain[i], N_KV_HEADS)[None], cos, sin)

    zeros = jnp.zeros((B, HEAD_DIM, S), BF16)
    k0, k1 = kt[:, :HEAD_DIM], kt[:, HEAD_DIM:]
    kw = jnp.stack([jnp.stack([jnp.concatenate([k0, zeros], 1), jnp.concatenate([zeros, k0], 1)], 1),
                    jnp.stack([jnp.concatenate([k1, zeros], 1), jnp.concatenate([zeros, k1], 1)], 1)], 1)
    vw = jnp.stack([v, jnp.concatenate([v[..., HEAD_DIM:], v[..., :HEAD_DIM]], -1)], 1)
    ya = _attention(q, kw, vw)

    circ = _filter_time_domain(S, w_f1[i], b_f1[i], freq1[i], w_f2[i], b_f2[i], freq2[i], w_f3[i])
    hspec = _filter_fft(circ.reshape(2 * D_HYENA, FFT_N1, FFT_N2), cst)
    hspec = hspec.reshape(2, D_HYENA, FFT_N1, 2 * LANES)
    du = ut.shape[1]
    u4 = ut.reshape(B, du, S // LANES, LANES)
    par_u = jnp.broadcast_to(jnp.concatenate([conv_w[i], conv_b[i][None]], 0)[:, :, None], (4, du, LANES))
    fb = jnp.broadcast_to(filt_bias[i][:, :, None], (2, D_HYENA, LANES))
    yh4 = _hyena(u4, par_u, fb, hspec, cst)
    yht = yh4.reshape(B, D_HYENA, S)

    wrt = jnp.zeros((D, LANES), F32).at[:, :N_GROUPS].set(w_group[i]).at[:, N_GROUPS:N_GROUPS + N_EXPERTS].set(w_router[i])
    brt = jnp.zeros((1, LANES), F32).at[0, :N_GROUPS].set(b_group[i]).at[0, N_GROUPS:N_GROUPS + N_EXPERTS].set(b_router[i])
    x1, h2, route = _outproj(ya, yht, x, g_attn_out[i][None], g_hyena_out[i][:, None], w_out[i].astype(BF16),
                             bd, g_moe[i][None], wrt, brt)

    route = route.reshape(N, LANES)
    expert = route[:, :TOP_K].astype(jnp.int32)
    row_tok, block_e, pos = _dispatch(expert, N)
    xs = h2.reshape(N, D)[row_tok]
    yb = _moe_experts(block_e, xs, w_gate[i], w_up[i], w_down[i])
    y0 = yb[pos[:, 0]]
    y1 = yb[pos[:, 1]]

    out = _final(x1.reshape(N, D), y0, y1, route[:, 2:3], route[:, 3:4], p[i].reshape(N, -1), g_ple[i][None],
                 w_ple_gate[i].astype(BF16), b_ple_gate[i][None], w_ple[i].astype(BF16), g_final[None])
    return out.reshape(B, S, D)
```

```python
import functools
import math

import numpy as np
import jax
import jax.numpy as jnp
from jax import lax
from jax.experimental import pallas as pl
from jax.experimental.pallas import tpu as pltpu

F32 = jnp.float32
BF16 = jnp.bfloat16

D_MODEL = 1024
EPS = 1e-6
GRID_W = 64
N_HEADS = 8
N_KV_HEADS = 2
HEAD_DIM = 64
D_ATTN = N_HEADS * HEAD_DIM
D_KV = N_KV_HEADS * HEAD_DIM
ROPE_THETA = 10000.0
D_HYENA = 512
HYENA_HEAD = 64
FILTER_EMB = 33
FAST_DECAY_PCT = 0.3
SLOW_DECAY_PCT = 1.5
DECAY_TARGET = 1e-2
N_GROUPS = 4
EXPERTS_PER_GROUP = 8
N_EXPERTS = N_GROUPS * EXPERTS_PER_GROUP
TOP_K = 2
D_EXPERT = 512

LANES = 128
FFT_N1 = 64
FFT_N2 = 128
VMEM_LIMIT = 56 * 1024 * 1024

TM_PROJ = 512
TQ_ATTN = 256
C_HY = 32
SEQ_UNROLL = 8
TB_MOE = 256


def _cparams(sem):
    return pltpu.CompilerParams(dimension_semantics=sem, vmem_limit_bytes=VMEM_LIMIT)


def _rope_tables(S):
    half = HEAD_DIM // 2
    t = jnp.arange(S, dtype=F32)
    r_idx = jnp.floor(t / GRID_W)
    c_idx = t - r_idx * GRID_W
    inv = ROPE_THETA ** (-jnp.arange(0, half, 2, dtype=F32) / half)
    ang_r = r_idx[:, None] * inv[None]
    ang_c = c_idx[:, None] * inv[None]
    cos_h = jnp.concatenate([jnp.cos(ang_r), jnp.cos(ang_r), jnp.cos(ang_c), jnp.cos(ang_c)], axis=-1)
    sin_h = jnp.concatenate([-jnp.sin(ang_r), jnp.sin(ang_r), -jnp.sin(ang_c), jnp.sin(ang_c)], axis=-1)
    return jnp.tile(cos_h, (1, 2)), jnp.tile(sin_h, (1, 2))


def _dft_constants():
    n1, n2 = FFT_N1, FFT_N2
    n = n1 * n2
    a = np.arange(n1)
    ang = 2.0 * np.pi * np.outer(a, a) / n1
    far, fai = np.cos(ang), -np.sin(ang)
    hlf = n1 // 2
    ma = np.block([[far[:, :hlf], -fai[:, :hlf]], [fai[:, :hlf], far[:, :hlf]]])
    maf = np.concatenate([far, fai], axis=0)
    b = np.arange(n2)
    angt = 2.0 * np.pi * np.outer(a, b) / n
    tw = np.concatenate([np.cos(angt), -np.sin(angt)], axis=1)
    angb = 2.0 * np.pi * np.outer(b, b) / n2
    fbr, fbi = np.cos(angb), -np.sin(angb)
    g = np.block([[fbr, fbi], [-fbi, fbr]])
    ginv = np.block([[fbr, -fbi], [fbi, fbr]])
    minv_r = np.concatenate([far[:hlf], -fai[:hlf]], axis=0) / n
    minv_i = np.concatenate([fai[:hlf], far[:hlf]], axis=0) / n
    f = lambda m: jnp.asarray(m.astype(np.float32))
    return dict(ma=f(ma), maf=f(maf), tw=f(tw), g=f(g), ginv=f(ginv), minv_r=f(minv_r), minv_i=f(minv_i))


def _block_diag_ones(width, group):
    i = np.arange(width) // group
    return jnp.asarray((i[:, None] == i[None, :]).astype(np.float32)).astype(BF16)


def _group_sumsq(a, bd):
    sq = a * a
    hi = sq.astype(BF16)
    lo = (sq - hi.astype(F32)).astype(BF16)
    return (jnp.dot(hi, bd, preferred_element_type=F32) + jnp.dot(lo, bd, preferred_element_type=F32))


def _head_norm_rope(a, gain, bd, cos, sin):
    width = a.shape[-1]
    n = a * lax.rsqrt(_group_sumsq(a, bd) * (1.0 / HEAD_DIM) + EPS) * gain
    rep = width // LANES
    if rep > 1:
        cos = jnp.concatenate([cos] * rep, axis=-1)
        sin = jnp.concatenate([sin] * rep, axis=-1)
    fwd = pltpu.roll(n, width - 16, 1)
    bwd = pltpu.roll(n, 16, 1)
    lane = lax.broadcasted_iota(jnp.int32, n.shape, 1)
    sw = jnp.where((lane % 32) < 16, fwd, bwd)
    return n * cos + sw * sin


def _inproj_body(x_ref, g_ref, wqkv_ref, wu_ref, bd_ref, qg_ref, kg_ref, cos_ref, sin_ref,
                 q_ref, kt_ref, v_ref, ut_ref):
    x = x_ref[...]
    h = x * lax.rsqrt(jnp.mean(x * x, axis=-1, keepdims=True) + EPS) * g_ref[...]
    hb = h.astype(BF16)
    qkv = jnp.dot(hb, wqkv_ref[...], preferred_element_type=F32)
    cos = cos_ref[...]
    sin = sin_ref[...]
    bd = bd_ref[...]
    q = _head_norm_rope(qkv[:, :D_ATTN], qg_ref[...], bd, cos, sin)
    q_ref[...] = (q * (HEAD_DIM ** -0.5 * math.log2(math.e))).astype(BF16)
    k = _head_norm_rope(qkv[:, D_ATTN:D_ATTN + D_KV], kg_ref[...], bd[:D_KV, :D_KV], cos, sin)
    kt_ref[...] = k.T.astype(BF16)
    v_ref[...] = qkv[:, D_ATTN + D_KV:].astype(BF16)
    ut_ref[...] = lax.dot_general(wu_ref[...], hb, (((1,), (1,)), ((), ())),
                                  preferred_element_type=F32)


def _inproj(x, g_mix, wqkv, wut, bd, qg, kg, cos, sin):
    B, S, D = x.shape
    tm = TM_PROJ
    du = wut.shape[0]
    full = lambda shape: pl.BlockSpec(shape, lambda b, i: (0,) * len(shape))
    return pl.pallas_call(
        _inproj_body,
        grid=(B, S // tm),
        in_specs=[
            pl.BlockSpec((None, tm, D), lambda b, i: (b, i, 0)),
            full((1, D)), full(wqkv.shape), full(wut.shape), full(bd.shape),
            full((1, D_ATTN)), full((1, D_KV)),
            pl.BlockSpec((tm, LANES), lambda b, i: (i, 0)),
            pl.BlockSpec((tm, LANES), lambda b, i: (i, 0)),
        ],
        out_specs=[
            pl.BlockSpec((None, tm, D_ATTN), lambda b, i: (b, i, 0)),
            pl.BlockSpec((None, D_KV, tm), lambda b, i: (b, 0, i)),
            pl.BlockSpec((None, tm, D_KV), lambda b, i: (b, i, 0)),
            pl.BlockSpec((None, du, tm), lambda b, i: (b, 0, i)),
        ],
        out_shape=[
            jax.ShapeDtypeStruct((B, S, D_ATTN), BF16),
            jax.ShapeDtypeStruct((B, D_KV, S), BF16),
            jax.ShapeDtypeStruct((B, S, D_KV), BF16),
            jax.ShapeDtypeStruct((B, du, S), F32),
        ],
        compiler_params=_cparams(("parallel", "parallel")),
        name="inproj",
    )(x, g_mix, wqkv, wut, bd, qg, kg, cos, sin)


def _attn_body(q_ref, kw_ref, vw_ref, o_ref):

    def one_head(q, kw, vw):
        s = jnp.dot(q, kw, preferred_element_type=F32)
        m = jnp.max(s, axis=-1, keepdims=True)
        p = jnp.exp2(s - m)
        l = jnp.sum(p, axis=-1, keepdims=True)
        o = jnp.dot(p.astype(BF16), vw, preferred_element_type=F32)
        return o / l

    for pair in range(D_ATTN // LANES):
        h = pair // (N_HEADS // N_KV_HEADS // 2)
        q = q_ref[:, pair * LANES:(pair + 1) * LANES]
        oe = one_head(q, kw_ref[h, 0], vw_ref[h])
        oo = one_head(q, kw_ref[h, 1], vw_ref[1 - h])
        lane = lax.broadcasted_iota(jnp.int32, oe.shape, 1)
        o_ref[:, pair * LANES:(pair + 1) * LANES] = jnp.where(lane < HEAD_DIM, oe, oo)


def _attention(q, kw, vw):
    B, S, _ = q.shape
    tq = TQ_ATTN
    return pl.pallas_call(
        _attn_body,
        grid=(B, S // tq),
        in_specs=[
            pl.BlockSpec((None, tq, D_ATTN), lambda b, i: (b, i, 0)),
            pl.BlockSpec((None, N_KV_HEADS, 2, LANES, S), lambda b, i: (b, 0, 0, 0, 0)),
            pl.BlockSpec((None, N_KV_HEADS, S, LANES), lambda b, i: (b, 0, 0, 0)),
        ],
        out_specs=pl.BlockSpec((None, tq, D_ATTN), lambda b, i: (b, i, 0)),
        out_shape=jax.ShapeDtypeStruct((B, S, D_ATTN), F32),
        compiler_params=_cparams(("parallel", "arbitrary")),
        name="attention",
    )(q, kw, vw)


def _fwd_twiddle_store(y, tw_ref, s1_ref, row0):
    yr, yi = y[:FFT_N1], y[FFT_N1:]
    twr, twi = tw_ref[:, :LANES], tw_ref[:, LANES:]
    s1_ref[pl.ds(row0, FFT_N1), :LANES] = (yr * twr - yi * twi).astype(BF16)
    s1_ref[pl.ds(row0, FFT_N1), LANES:] = (yr * twi + yi * twr).astype(BF16)


def _filtfft_body(x_ref, maf_ref, tw_ref, g_ref, h_ref, s1_ref):
    C = x_ref.shape[0]

    def step_a(c, carry):
        y = jnp.dot(maf_ref[...], x_ref[c].astype(BF16), preferred_element_type=F32)
        _fwd_twiddle_store(y, tw_ref, s1_ref, pl.multiple_of(c * FFT_N1, FFT_N1))
        return carry

    lax.fori_loop(0, C, step_a, 0, unroll=SEQ_UNROLL)
    z = jnp.dot(s1_ref[...], g_ref[...], preferred_element_type=F32)
    h_ref[...] = z.reshape(C, FFT_N1, 2 * LANES)


def _filter_fft(circ, cst):
    n_seq = circ.shape[0]
    C = C_HY
    full = lambda a: pl.BlockSpec(a.shape, lambda i: (0,) * a.ndim)
    maf, tw, g = cst["maf"].astype(BF16), cst["tw"], cst["g"].astype(BF16)
    return pl.pallas_call(
        _filtfft_body,
        grid=(n_seq // C,),
        in_specs=[pl.BlockSpec((C, FFT_N1, FFT_N2), lambda i: (i, 0, 0)), full(maf), full(tw), full(g)],
        out_specs=pl.BlockSpec((C, FFT_N1, 2 * LANES), lambda i: (i, 0, 0)),
        out_shape=jax.ShapeDtypeStruct((n_seq, FFT_N1, 2 * LANES), F32),
        scratch_shapes=[pltpu.VMEM((C * FFT_N1, 2 * LANES), BF16)],
        compiler_params=_cparams(("parallel",)),
        name="filter_fft",
    )(circ, maf, tw, g)


def _short_conv(x, par_ref, c):
    rows, lanes = x.shape
    a_i = lax.broadcasted_iota(jnp.int32, x.shape, 0)
    b_i = lax.broadcasted_iota(jnp.int32, x.shape, 1)
    l1 = pltpu.roll(x, 1, 1)
    l2 = pltpu.roll(l1, 1, 0)
    prev = jnp.where(b_i == 0, l2, l1)
    prev = jnp.where((a_i == 0) & (b_i == 0), 0.0, prev)
    r1 = pltpu.roll(x, lanes - 1, 1)
    r2 = pltpu.roll(r1, rows - 1, 0)
    nxt = jnp.where(b_i == lanes - 1, r2, r1)
    nxt = jnp.where((a_i == rows - 1) & (b_i == lanes - 1), 0.0, nxt)
    w0 = par_ref[0, pl.ds(c, 1), :]
    w1 = par_ref[1, pl.ds(c, 1), :]
    w2 = par_ref[2, pl.ds(c, 1), :]
    cb = par_ref[3, pl.ds(c, 1), :]
    return cb + prev * w0 + x * w1 + nxt * w2


def _hyena_body(v_ref, x1_ref, x2_ref, pv_ref, p1_ref, p2_ref, fb_ref, h_ref,
                ma_ref, tw_ref, g_ref, ginv_ref, mir_ref, mii_ref,
                o_ref, s1_ref, s2_ref, vc_ref, z1_ref):
    C = v_ref.shape[1]
    half = FFT_N1 // 2

    def spectral(order):
        z = jnp.dot(s1_ref[...], g_ref[...], preferred_element_type=F32)
        hs = h_ref[order].reshape(C * FFT_N1, 2 * LANES)
        zr, zi = z[:, :LANES], z[:, LANES:]
        hr, hi = hs[:, :LANES], hs[:, LANES:]
        pb = jnp.concatenate([zr * hr - zi * hi, zr * hi + zi * hr], axis=1).astype(BF16)
        s2_ref[...] = jnp.dot(pb, ginv_ref[...], preferred_element_type=F32)

    def inv_a(c):
        row0 = pl.multiple_of(c * FFT_N1, FFT_N1)
        y = s2_ref[pl.ds(row0, FFT_N1), :]
        yr, yi = y[:, :LANES], y[:, LANES:]
        twr, twi = tw_ref[:, :LANES], tw_ref[:, LANES:]
        ur = (yr * twr + yi * twi).astype(BF16)
        ui = (yi * twr - yr * twi).astype(BF16)
        out = (jnp.dot(mir_ref[...], ur, preferred_element_type=F32)
               + jnp.dot(mii_ref[...], ui, preferred_element_type=F32))
        return out[:half], out[half:]

    def fwd_a(c, xr, xi):
        xs = jnp.concatenate([xr, xi], axis=0).astype(BF16)
        y = jnp.dot(ma_ref[...], xs, preferred_element_type=F32)
        _fwd_twiddle_store(y, tw_ref, s1_ref, pl.multiple_of(c * FFT_N1, FFT_N1))

    def pass1_a(c, carry):
        vr = _short_conv(v_ref[0, c], pv_ref, c)
        vi = _short_conv(v_ref[1, c], pv_ref, c)
        vc_ref[0, c] = vr
        vc_ref[1, c] = vi
        fwd_a(c, vr, vi)
        return carry

    def pass1_b(c, carry):
        cr, ci = inv_a(c)
        bias = fb_ref[0, pl.ds(c, 1), :]
        zr = _short_conv(x1_ref[0, c], p1_ref, c) * (cr + bias * vc_ref[0, c])
        zi = _short_conv(x1_ref[1, c], p1_ref, c) * (ci + bias * vc_ref[1, c])
        z1_ref[0, c] = zr
        z1_ref[1, c] = zi
        fwd_a(c, zr, zi)
        return carry

    def pass2_b(c, carry):
        cr, ci = inv_a(c)
        bias = fb_ref[1, pl.ds(c, 1), :]
        o_ref[0, c] = _short_conv(x2_ref[0, c], p2_ref, c) * (cr + bias * z1_ref[0, c])
        o_ref[1, c] = _short_conv(x2_ref[1, c], p2_ref, c) * (ci + bias * z1_ref[1, c])
        return carry

    lax.fori_loop(0, C, pass1_a, 0, unroll=SEQ_UNROLL)
    spectral(0)
    lax.fori_loop(0, C, pass1_b, 0, unroll=SEQ_UNROLL)
    spectral(1)
    lax.fori_loop(0, C, pass2_b, 0, unroll=SEQ_UNROLL)


def _hyena(u4, par_u, fb, hspec, cst):
    B = u4.shape[0]
    C = C_HY
    J = D_HYENA // C
    rows = u4.shape[2]
    full = lambda a: pl.BlockSpec(a.shape, lambda j, p: (0,) * a.ndim)
    ma, g, ginv = cst["ma"].astype(BF16), cst["g"].astype(BF16), cst["ginv"].astype(BF16)
    mir, mii = cst["minv_r"].astype(BF16), cst["minv_i"].astype(BF16)
    tw = cst["tw"]
    u_spec = lambda k: pl.BlockSpec((2, C, rows, LANES), lambda j, p, k=k: (p, j + k * J, 0, 0))
    par_spec = lambda k: pl.BlockSpec((4, C, LANES), lambda j, p, k=k: (0, j + k * J, 0))
    return pl.pallas_call(
        _hyena_body,
        grid=(J, B // 2),
        in_specs=[
            u_spec(0), u_spec(1), u_spec(2), par_spec(0), par_spec(1), par_spec(2),
            pl.BlockSpec((2, C, LANES), lambda j, p: (0, j, 0)),
            pl.BlockSpec((2, C, FFT_N1, 2 * LANES), lambda j, p: (0, j, 0, 0)),
            full(ma), full(tw), full(g), full(ginv), full(mir), full(mii),
        ],
        out_specs=pl.BlockSpec((2, C, rows, LANES), lambda j, p: (p, j, 0, 0)),
        out_shape=jax.ShapeDtypeStruct((B, D_HYENA, rows, LANES), F32),
        scratch_shapes=[
            pltpu.VMEM((C * FFT_N1, 2 * LANES), BF16),
            pltpu.VMEM((C * FFT_N1, 2 * LANES), F32),
            pltpu.VMEM((2, C, rows, LANES), F32),
            pltpu.VMEM((2, C, rows, LANES), F32),
        ],
        compiler_params=_cparams(("parallel", "arbitrary")),
        name="hyena",
    )(u4, u4, u4, par_u, par_u, par_u, fb, hspec, ma, tw, g, ginv, mir, mii)


def _filter_time_domain(L, w_f1, b_f1, freq1, w_f2, b_f2, freq2, w_f3):
    hp = lax.Precision.HIGHEST
    bands = (FILTER_EMB - 1) // 2
    t = jnp.linspace(0.0, 1.0, L, dtype=F32)[:, None]
    w = (2.0 * math.pi / L) * jnp.arange(L, dtype=F32)[:, None]
    f = jnp.linspace(1e-4, bands - 1, bands, dtype=F32)[None]
    zf = f * w
    z = jnp.concatenate([t, jnp.cos(zf), -jnp.sin(zf)], axis=-1)
    back = lambda a: jnp.roll(a[::-1], 1, axis=0)
    zz = jnp.concatenate([z, back(z)], axis=0)
    h = jnp.sin(freq1 * (jnp.dot(zz, w_f1, precision=hp) + b_f1))
    h = jnp.sin(freq2 * (jnp.dot(h, w_f2, precision=hp) + b_f2))
    w3 = w_f3.reshape(-1, 2, 2, D_HYENA)
    hf = jnp.einsum("hoc,lh->ocl", w3[:, :, 0], h[:L], precision=hp)
    hb = jnp.einsum("hoc,lh->ocl", w3[:, :, 1], h[L:], precision=hp)
    max_decay = math.log(DECAY_TARGET) / FAST_DECAY_PCT
    min_decay = math.log(DECAY_TARGET) / SLOW_DECAY_PCT
    deltas = jnp.linspace(min_decay, max_decay, D_HYENA, dtype=F32)
    decay_f = jnp.exp(-jnp.abs(deltas)[:, None] * t[None, :, 0])
    decay_b = jnp.exp(-jnp.abs(deltas)[:, None] * back(t)[None, :, 0])
    hf = hf * decay_f[None]
    hb = hb * decay_b[None]
    hf = hf / (jnp.sum(jnp.abs(hf), axis=-1, keepdims=True) + EPS)
    hb = hb / (jnp.sum(jnp.abs(hb), axis=-1, keepdims=True) + EPS)
    first = lax.broadcasted_iota(jnp.int32, hb.shape, 2) == 0
    return jnp.concatenate([hf + jnp.where(first, hb, 0.0), jnp.where(first, 0.0, hb)], axis=-1)


def _route_lanes(lg):
    neg = -1e30
    lane = lax.broadcasted_iota(jnp.int32, lg.shape, 1)
    gmask = lane < N_GROUPS
    gl = jnp.where(gmask, lg, neg)
    gm = jnp.max(gl, axis=-1, keepdims=True)
    gsum = jnp.sum(jnp.where(gmask, jnp.exp(gl - gm), 0.0), axis=-1, keepdims=True)
    g_top = 1.0 / gsum
    g_sel = jnp.min(jnp.where(gl == gm, lane, LANES), axis=-1, keepdims=True)
    lo = N_GROUPS + EXPERTS_PER_GROUP * g_sel
    el = jnp.where((lane >= lo) & (lane < lo + EXPERTS_PER_GROUP), lg, neg)
    m1 = jnp.max(el, axis=-1, keepdims=True)
    i1 = jnp.min(jnp.where(el == m1, lane, LANES), axis=-1, keepdims=True)
    el2 = jnp.where(lane == i1, neg, el)
    m2 = jnp.max(el2, axis=-1, keepdims=True)
    i2 = jnp.min(jnp.where(el2 == m2, lane, LANES), axis=-1, keepdims=True)
    d = jnp.exp(m2 - m1)
    p1 = 1.0 / (1.0 + d)
    p2 = d / (1.0 + d)
    e1 = (i1 - N_GROUPS).astype(F32)
    e2 = (i2 - N_GROUPS).astype(F32)
    return jnp.where(lane == 0, e1, jnp.where(lane == 1, e2, jnp.where(lane == 2, g_top * p1,
                     jnp.where(lane == 3, g_top * p2, 0.0))))


def _outproj_body(ya_ref, yh_ref, x_ref, ga_ref, gh_ref, wo_ref, bd_ref, gm_ref, wrt_ref, brt_ref,
                  x1_ref, h2_ref, rt_ref):
    ya = ya_ref[...]
    yan = ya * lax.rsqrt(_group_sumsq(ya, bd_ref[...]) * (1.0 / HEAD_DIM) + EPS) * ga_ref[...]
    yh = yh_ref[...]
    tm = yh.shape[1]
    yh3 = yh.reshape(D_HYENA // HYENA_HEAD, HYENA_HEAD, tm)
    ms = jnp.mean(yh3 * yh3, axis=1, keepdims=True)
    yhn = (yh3 * lax.rsqrt(ms + EPS)).reshape(D_HYENA, tm) * gh_ref[...]
    mix = (jnp.dot(yan.astype(BF16), wo_ref[:D_ATTN, :], preferred_element_type=F32)
           + jnp.dot(yhn.T.astype(BF16), wo_ref[D_ATTN:, :], preferred_element_type=F32))
    x1 = x_ref[...] + mix
    x1_ref[...] = x1
    h2 = x1 * lax.rsqrt(jnp.mean(x1 * x1, axis=-1, keepdims=True) + EPS) * gm_ref[...]
    h2_ref[...] = h2
    lg = jnp.dot(h2, wrt_ref[...], precision=lax.Precision.HIGHEST,
                 preferred_element_type=F32) + brt_ref[...]
    rt_ref[...] = _route_lanes(lg)


def _outproj(ya, yht, x, ga, gh, wo, bd, gm, wrt, brt):
    B, S, D = x.shape
    tm = TM_PROJ
    full = lambda a: pl.BlockSpec(a.shape, lambda b, i: (0,) * a.ndim)
    return pl.pallas_call(
        _outproj_body,
        grid=(B, S // tm),
        in_specs=[
            pl.BlockSpec((None, tm, D_ATTN), lambda b, i: (b, i, 0)),
            pl.BlockSpec((None, D_HYENA, tm), lambda b, i: (b, 0, i)),
            pl.BlockSpec((None, tm, D), lambda b, i: (b, i, 0)),
            full(ga), full(gh), full(wo), full(bd), full(gm), full(wrt), full(brt),
        ],
        out_specs=[
            pl.BlockSpec((None, tm, D), lambda b, i: (b, i, 0)),
            pl.BlockSpec((None, tm, D), lambda b, i: (b, i, 0)),
            pl.BlockSpec((None, tm, LANES), lambda b, i: (b, i, 0)),
        ],
        out_shape=[
            jax.ShapeDtypeStruct((B, S, D), F32),
            jax.ShapeDtypeStruct((B, S, D), F32),
            jax.ShapeDtypeStruct((B, S, LANES), F32),
        ],
        compiler_params=_cparams(("parallel", "parallel")),
        name="outproj",
    )(ya, yht, x, ga, gh, wo, bd, gm, wrt, brt)


def _moe_body(be_ref, x_ref, wg_ref, wu_ref, wd_ref, y_ref, wg_s, wu_s, wd_s):
    i = pl.program_id(0)
    prev = be_ref[jnp.maximum(i - 1, 0)]

    @pl.when((i == 0) | (be_ref[i] != prev))
    def _():
        wg_s[...] = wg_ref[...].astype(BF16)
        wu_s[...] = wu_ref[...].astype(BF16)
        wd_s[...] = wd_ref[...].astype(BF16)

    x = x_ref[...].astype(BF16)
    a = jnp.dot(x, wg_s[...], preferred_element_type=F32)
    b = jnp.dot(x, wu_s[...], preferred_element_type=F32)
    hmid = (a * jax.nn.sigmoid(a)) * b
    y_ref[...] = jnp.dot(hmid.astype(BF16), wd_s[...], preferred_element_type=F32)


def _moe_experts(block_e, xs, w_gate, w_up, w_down):
    n_rows, D = xs.shape
    T = TB_MOE
    grid_spec = pltpu.PrefetchScalarGridSpec(
        num_scalar_prefetch=1,
        grid=(n_rows // T,),
        in_specs=[
            pl.BlockSpec((T, D), lambda i, be: (i, 0)),
            pl.BlockSpec((None, D, D_EXPERT), lambda i, be: (be[i], 0, 0)),
            pl.BlockSpec((None, D, D_EXPERT), lambda i, be: (be[i], 0, 0)),
            pl.BlockSpec((None, D_EXPERT, D), lambda i, be: (be[i], 0, 0)),
        ],
        out_specs=pl.BlockSpec((T, D), lambda i, be: (i, 0)),
        scratch_shapes=[
            pltpu.VMEM((D, D_EXPERT), BF16), pltpu.VMEM((D, D_EXPERT), BF16), pltpu.VMEM((D_EXPERT, D), BF16),
        ],
    )
    return pl.pallas_call(
        _moe_body,
        grid_spec=grid_spec,
        out_shape=jax.ShapeDtypeStruct((n_rows, D), F32),
        compiler_params=_cparams(("arbitrary",)),
        name="moe_experts",
    )(block_e, xs, w_gate, w_up, w_down)


def _dispatch(expert, N):
    T = TB_MOE
    NK = N * TOP_K
    e_flat = expert.reshape(NK)
    experts = jnp.arange(N_EXPERTS, dtype=jnp.int32)
    order = jnp.argsort(e_flat).astype(jnp.int32)
    tok_sorted = order // TOP_K
    onehot = (e_flat[:, None] == experts[None]).astype(jnp.int32)
    counts = jnp.sum(onehot, axis=0)
    ends = jnp.cumsum(counts)
    starts = ends - counts
    padded = (counts + T - 1) // T * T
    pends = jnp.cumsum(padded)
    pstarts = pends - padded
    n_rows = -(-(NK + N_EXPERTS * (T - 1)) // T) * T
    n_blocks = n_rows // T
    blk_start = jnp.arange(n_blocks, dtype=jnp.int32) * T
    block_e = jnp.clip(jnp.sum((pends[None, :] <= blk_start[:, None]).astype(jnp.int32), axis=1),
                       0, N_EXPERTS - 1)
    oh_b = (block_e[:, None] == experts[None]).astype(jnp.int32)
    base = jnp.sum(oh_b * (starts - pstarts)[None], axis=1) + blk_start
    end_b = jnp.sum(oh_b * ends[None], axis=1)
    src = base[:, None] + jnp.arange(T, dtype=jnp.int32)[None]
    filler = (blk_start[:, None] + jnp.arange(T, dtype=jnp.int32)[None]) % N
    row_tok = jnp.where(src < end_b[:, None], tok_sorted[jnp.clip(src, 0, NK - 1)], filler).reshape(n_rows)
    inv = jnp.argsort(order).astype(jnp.int32)
    pos = (inv + jnp.sum(onehot * (pstarts - starts)[None], axis=1)).reshape(N, TOP_K)
    return row_tok, block_e.astype(jnp.int32), pos


def _final_body(x1_ref, y0_ref, y1_ref, w0_ref, w1_ref, p_ref, gp_ref, wg_ref, bg_ref, wp_ref, gf_ref, o_ref):
    x2 = x1_ref[...] + (y0_ref[...] * w0_ref[...] + y1_ref[...] * w1_ref[...])
    hp = x2 * lax.rsqrt(jnp.mean(x2 * x2, axis=-1, keepdims=True) + EPS) * gp_ref[...]
    gate = jax.nn.sigmoid(jnp.dot(hp.astype(BF16), wg_ref[...], preferred_element_type=F32) + bg_ref[...])
    pe = jnp.dot(p_ref[...].astype(BF16), wp_ref[...], preferred_element_type=F32)
    x3 = x2 + pe * gate
    o_ref[...] = x3 * lax.rsqrt(jnp.mean(x3 * x3, axis=-1, keepdims=True) + EPS) * gf_ref[...]


def _final(x1, y0, y1, w0, w1, p, gp, wg, bg, wp, gf):
    N, D = x1.shape
    tm = TM_PROJ
    row = lambda w: pl.BlockSpec((tm, w), lambda i: (i, 0))
    full = lambda a: pl.BlockSpec(a.shape, lambda i: (0,) * a.ndim)
    return pl.pallas_call(
        _final_body,
        grid=(N // tm,),
        in_specs=[row(D), row(D), row(D), row(1), row(1), row(p.shape[1]),
                  full(gp), full(wg), full(bg), full(wp), full(gf)],
        out_specs=row(D),
        out_shape=jax.ShapeDtypeStruct((N, D), F32),
        compiler_params=_cparams(("parallel",)),
        name="ple_final",
    )(x1, y0, y1, w0, w1, p, gp, wg, bg, wp, gf)


def kernel(x, p, g_mix, w_in, q_gain, k_gain, conv_w, conv_b, w_f1, b_f1, freq1, w_f2, b_f2, freq2, w_f3, filt_bias, g_attn_out, g_hyena_out, w_out, g_moe, w_group, b_group, w_router, b_router, w_gate, w_up, w_down, g_ple, w_ple_gate, b_ple_gate, w_ple, g_final):
    B, S, D = x.shape
    N = B * S
    assert p.shape[0] == 1 and S == (FFT_N1 // 2) * FFT_N2 and B % 2 == 0
    i = 0
    cst = _dft_constants()
    cos, sin = _rope_tables(S)
    bd = _block_diag_ones(D_ATTN, HEAD_DIM)

    n_qkv = D_ATTN + 2 * D_KV
    wqkv = w_in[i][:, :n_qkv].astype(BF16)
    wut = w_in[i][:, n_qkv:].T.astype(BF16)
    q, kt, v, ut = _inproj(x, g_mix[i][None], wqkv, wut, bd,
                           jnp.tile(q_gain[i], N_HEADS)[None], jnp.tile(k_gain[i], N_KV_HEADS)[None], cos, sin)

    zeros = jnp.zeros((B, HEAD_DIM, S), BF16)
    k0, k1 = kt[:, :HEAD_DIM], kt[:, HEAD_DIM:]
    kw = jnp.stack([jnp.stack([jnp.concatenate([k0, zeros], 1), jnp.concatenate([zeros, k0], 1)], 1),
                    jnp.stack([jnp.concatenate([k1, zeros], 1), jnp.concatenate([zeros, k1], 1)], 1)], 1)
    vw = jnp.stack([v, jnp.concatenate([v[..., HEAD_DIM:], v[..., :HEAD_DIM]], -1)], 1)
    ya = _attention(q, kw, vw)

    circ = _filter_time_domain(S, w_f1[i], b_f1[i], freq1[i], w_f2[i], b_f2[i], freq2[i], w_f3[i])
    hspec = _filter_fft(circ.reshape(2 * D_HYENA, FFT_N1, FFT_N2), cst)
    hspec = hspec.reshape(2, D_HYENA, FFT_N1, 2 * LANES)
    du = ut.shape[1]
    u4 = ut.reshape(B, du, S // LANES, LANES)
    par_u = jnp.broadcast_to(jnp.concatenate([conv_w[i], conv_b[i][None]], 0)[:, :, None], (4, du, LANES))
    fb = jnp.broadcast_to(filt_bias[i][:, :, None], (2, D_HYENA, LANES))
    yh4 = _hyena(u4, par_u, fb, hspec, cst)
    yht = yh4.reshape(B, D_HYENA, S)

    wrt = jnp.zeros((D, LANES), F32).at[:, :N_GROUPS].set(w_group[i]).at[:, N_GROUPS:N_GROUPS + N_EXPERTS].set(w_router[i])
    brt = jnp.zeros((1, LANES), F32).at[0, :N_GROUPS].set(b_group[i]).at[0, N_GROUPS:N_GROUPS + N_EXPERTS].set(b_router[i])
    x1, h2, route = _outproj(ya, yht, x, g_attn_out[i][None], g_hyena_out[i][:, None], w_out[i].astype(BF16),
                             bd, g_moe[i][None], wrt, brt)

    route = route.reshape(N, LANES)
    expert = route[:, :TOP_K].astype(jnp.int32)
    row_tok, block_e, pos = _dispatch(expert, N)
    xs = h2.reshape(N, D)[row_tok]
    yb = _moe_experts(block_e, xs, w_gate[i], w_up[i], w_down[i])
    y0 = yb[pos[:, 0]]
    y1 = yb[pos[:, 1]]

    out = _final(x1.reshape(N, D), y0, y1, route[:, 2:3], route[:, 3:4], p[i].reshape(N, -1), g_ple[i][None],
                 w_ple_gate[i].astype(BF16), b_ple_gate[i][None], w_ple[i].astype(BF16), g_final[None])
    return out.reshape(B, S, D)
```

```python
import functools
import math

import numpy as np
import jax
import jax.numpy as jnp
from jax import lax
from jax.experimental import pallas as pl
from jax.experimental.pallas import tpu as pltpu

F32 = jnp.float32
BF16 = jnp.bfloat16

D_MODEL = 1024
EPS = 1e-6
GRID_W = 64
N_HEADS = 8
N_KV_HEADS = 2
HEAD_DIM = 64
D_ATTN = N_HEADS * HEAD_DIM
D_KV = N_KV_HEADS * HEAD_DIM
ROPE_THETA = 10000.0
D_HYENA = 512
HYENA_HEAD = 64
FILTER_EMB = 33
FAST_DECAY_PCT = 0.3
SLOW_DECAY_PCT = 1.5
DECAY_TARGET = 1e-2
N_GROUPS = 4
EXPERTS_PER_GROUP = 8
N_EXPERTS = N_GROUPS * EXPERTS_PER_GROUP
TOP_K = 2
D_EXPERT = 512

LANES = 128
FFT_N1 = 64
FFT_N2 = 128
VMEM_LIMIT = 56 * 1024 * 1024

TM_PROJ = 512
TQ_ATTN = 256
C_HY = 32
SEQ_UNROLL = 8
TB_MOE = 256


def _cparams(sem):
    return pltpu.CompilerParams(dimension_semantics=sem, vmem_limit_bytes=VMEM_LIMIT)


def _rope_tables(S):
    half = HEAD_DIM // 2
    t = jnp.arange(S, dtype=F32)
    r_idx = jnp.floor(t / GRID_W)
    c_idx = t - r_idx * GRID_W
    inv = ROPE_THETA ** (-jnp.arange(0, half, 2, dtype=F32) / half)
    ang_r = r_idx[:, None] * inv[None]
    ang_c = c_idx[:, None] * inv[None]
    cos_h = jnp.concatenate([jnp.cos(ang_r), jnp.cos(ang_r), jnp.cos(ang_c), jnp.cos(ang_c)], axis=-1)
    sin_h = jnp.concatenate([-jnp.sin(ang_r), jnp.sin(ang_r), -jnp.sin(ang_c), jnp.sin(ang_c)], axis=-1)
    return jnp.tile(cos_h, (1, 2)), jnp.tile(sin_h, (1, 2))


def _dft_constants():
    n1, n2 = FFT_N1, FFT_N2
    n = n1 * n2
    a = np.arange(n1)
    ang = 2.0 * np.pi * np.outer(a, a) / n1
    far, fai = np.cos(ang), -np.sin(ang)
    hlf = n1 // 2
    ma = np.block([[far[:, :hlf], -fai[:, :hlf]], [fai[:, :hlf], far[:, :hlf]]])
    maf = np.concatenate([far, fai], axis=0)
    b = np.arange(n2)
    angt = 2.0 * np.pi * np.outer(a, b) / n
    tw = np.concatenate([np.cos(angt), -np.sin(angt)], axis=1)
    angb = 2.0 * np.pi * np.outer(b, b) / n2
    fbr, fbi = np.cos(angb), -np.sin(angb)
    g = np.block([[fbr, fbi], [-fbi, fbr]])
    ginv = np.block([[fbr, -fbi], [fbi, fbr]])
    minv_r = np.concatenate([far[:hlf], -fai[:hlf]], axis=0) / n
    minv_i = np.concatenate([fai[:hlf], far[:hlf]], axis=0) / n
    f = lambda m: jnp.asarray(m.astype(np.float32))
    return dict(ma=f(ma), maf=f(maf), tw=f(tw), g=f(g), ginv=f(ginv), minv_r=f(minv_r), minv_i=f(minv_i))


def _block_diag_ones(width, group):
    i = np.arange(width) // group
    return jnp.asarray((i[:, None] == i[None, :]).astype(np.float32)).astype(BF16)


def _group_sumsq(a, bd):
    sq = a * a
    hi = sq.astype(BF16)
    lo = (sq - hi.astype(F32)).astype(BF16)
    return (jnp.dot(hi, bd, preferred_element_type=F32) + jnp.dot(lo, bd, preferred_element_type=F32))


def _head_norm_rope(a, gain, bd, cos, sin):
    width = a.shape[-1]
    n = a * lax.rsqrt(_group_sumsq(a, bd) * (1.0 / HEAD_DIM) + EPS) * gain
    rep = width // LANES
    if rep > 1:
        cos = jnp.concatenate([cos] * rep, axis=-1)
        sin = jnp.concatenate([sin] * rep, axis=-1)
    fwd = pltpu.roll(n, width - 16, 1)
    bwd = pltpu.roll(n, 16, 1)
    lane = lax.broadcasted_iota(jnp.int32, n.shape, 1)
    sw = jnp.where((lane % 32) < 16, fwd, bwd)
    return n * cos + sw * sin


def _inproj_body(x_ref, g_ref, wqkv_ref, wu_ref, bd_ref, qg_ref, kg_ref, cos_ref, sin_ref,
                 q_ref, kw_ref, vw_ref, ut_ref):
    x = x_ref[...]
    h = x * lax.rsqrt(jnp.mean(x * x, axis=-1, keepdims=True) + EPS) * g_ref[...]
    hb = h.astype(BF16)
    qkv = jnp.dot(hb, wqkv_ref[...], preferred_element_type=F32)
    cos = cos_ref[...]
    sin = sin_ref[...]
    bd = bd_ref[...]
    q = _head_norm_rope(qkv[:, :D_ATTN], qg_ref[...], bd, cos, sin)
    q_ref[...] = (q * (HEAD_DIM ** -0.5 * math.log2(math.e))).astype(BF16)
    k = _head_norm_rope(qkv[:, D_ATTN:D_ATTN + D_KV], kg_ref[...], bd[:D_KV, :D_KV], cos, sin)
    kt = k.T.astype(BF16)
    zero = jnp.zeros((HEAD_DIM, kt.shape[1]), BF16)
    for h in range(N_KV_HEADS):
        kh = kt[h * HEAD_DIM:(h + 1) * HEAD_DIM]
        kw_ref[h, 0, :HEAD_DIM] = kh
        kw_ref[h, 0, HEAD_DIM:] = zero
        kw_ref[h, 1, :HEAD_DIM] = zero
        kw_ref[h, 1, HEAD_DIM:] = kh
    v = qkv[:, D_ATTN + D_KV:]
    vw_ref[0] = v.astype(BF16)
    vw_ref[1] = pltpu.roll(v, HEAD_DIM, 1).astype(BF16)
    ut_ref[...] = lax.dot_general(wu_ref[...], hb, (((1,), (1,)), ((), ())),
                                  preferred_element_type=F32)


def _inproj(x, g_mix, wqkv, wut, bd, qg, kg, cos, sin):
    B, S, D = x.shape
    tm = TM_PROJ
    du = wut.shape[0]
    full = lambda shape: pl.BlockSpec(shape, lambda b, i: (0,) * len(shape))
    return pl.pallas_call(
        _inproj_body,
        grid=(B, S // tm),
        in_specs=[
            pl.BlockSpec((None, tm, D), lambda b, i: (b, i, 0)),
            full((1, D)), full(wqkv.shape), full(wut.shape), full(bd.shape),
            full((1, D_ATTN)), full((1, D_KV)),
            pl.BlockSpec((tm, LANES), lambda b, i: (i, 0)),
            pl.BlockSpec((tm, LANES), lambda b, i: (i, 0)),
        ],
        out_specs=[
            pl.BlockSpec((None, tm, D_ATTN), lambda b, i: (b, i, 0)),
            pl.BlockSpec((None, N_KV_HEADS, 2, LANES, tm), lambda b, i: (b, 0, 0, 0, i)),
            pl.BlockSpec((None, N_KV_HEADS, tm, LANES), lambda b, i: (b, 0, i, 0)),
            pl.BlockSpec((None, du, tm), lambda b, i: (b, 0, i)),
        ],
        out_shape=[
            jax.ShapeDtypeStruct((B, S, D_ATTN), BF16),
            jax.ShapeDtypeStruct((B, N_KV_HEADS, 2, LANES, S), BF16),
            jax.ShapeDtypeStruct((B, N_KV_HEADS, S, LANES), BF16),
            jax.ShapeDtypeStruct((B, du, S), F32),
        ],
        compiler_params=_cparams(("parallel", "parallel")),
        name="inproj",
    )(x, g_mix, wqkv, wut, bd, qg, kg, cos, sin)


def _attn_body(q_ref, kw_ref, vw_ref, o_ref):

    def one_head(q, kw, vw):
        s = jnp.dot(q, kw, preferred_element_type=F32)
        m = jnp.max(s, axis=-1, keepdims=True)
        p = jnp.exp2(s - m)
        l = jnp.sum(p, axis=-1, keepdims=True)
        o = jnp.dot(p.astype(BF16), vw, preferred_element_type=F32)
        return o / l

    for pair in range(D_ATTN // LANES):
        h = pair // (N_HEADS // N_KV_HEADS // 2)
        q = q_ref[:, pair * LANES:(pair + 1) * LANES]
        oe = one_head(q, kw_ref[h, 0], vw_ref[h])
        oo = one_head(q, kw_ref[h, 1], vw_ref[1 - h])
        lane = lax.broadcasted_iota(jnp.int32, oe.shape, 1)
        o_ref[:, pair * LANES:(pair + 1) * LANES] = jnp.where(lane < HEAD_DIM, oe, oo)


def _attention(q, kw, vw):
    B, S, _ = q.shape
    tq = TQ_ATTN
    return pl.pallas_call(
        _attn_body,
        grid=(B, S // tq),
        in_specs=[
            pl.BlockSpec((None, tq, D_ATTN), lambda b, i: (b, i, 0)),
            pl.BlockSpec((None, N_KV_HEADS, 2, LANES, S), lambda b, i: (b, 0, 0, 0, 0)),
            pl.BlockSpec((None, N_KV_HEADS, S, LANES), lambda b, i: (b, 0, 0, 0)),
        ],
        out_specs=pl.BlockSpec((None, tq, D_ATTN), lambda b, i: (b, i, 0)),
        out_shape=jax.ShapeDtypeStruct((B, S, D_ATTN), F32),
        compiler_params=_cparams(("parallel", "arbitrary")),
        name="attention",
    )(q, kw, vw)


def _fwd_twiddle_store(y, tw_ref, s1_ref, row0):
    yr, yi = y[:FFT_N1], y[FFT_N1:]
    twr, twi = tw_ref[:, :LANES], tw_ref[:, LANES:]
    s1_ref[pl.ds(row0, FFT_N1), :LANES] = (yr * twr - yi * twi).astype(BF16)
    s1_ref[pl.ds(row0, FFT_N1), LANES:] = (yr * twi + yi * twr).astype(BF16)


def _filtfft_body(x_ref, maf_ref, tw_ref, g_ref, h_ref, s1_ref):
    C = x_ref.shape[0]

    def step_a(c, carry):
        y = jnp.dot(maf_ref[...], x_ref[c].astype(BF16), preferred_element_type=F32)
        _fwd_twiddle_store(y, tw_ref, s1_ref, pl.multiple_of(c * FFT_N1, FFT_N1))
        return carry

    lax.fori_loop(0, C, step_a, 0, unroll=SEQ_UNROLL)
    z = jnp.dot(s1_ref[...], g_ref[...], preferred_element_type=F32)
    h_ref[...] = z.reshape(C, FFT_N1, 2 * LANES)


def _filter_fft(circ, cst):
    n_seq = circ.shape[0]
    C = C_HY
    full = lambda a: pl.BlockSpec(a.shape, lambda i: (0,) * a.ndim)
    maf, tw, g = cst["maf"].astype(BF16), cst["tw"], cst["g"].astype(BF16)
    return pl.pallas_call(
        _filtfft_body,
        grid=(n_seq // C,),
        in_specs=[pl.BlockSpec((C, FFT_N1, FFT_N2), lambda i: (i, 0, 0)), full(maf), full(tw), full(g)],
        out_specs=pl.BlockSpec((C, FFT_N1, 2 * LANES), lambda i: (i, 0, 0)),
        out_shape=jax.ShapeDtypeStruct((n_seq, FFT_N1, 2 * LANES), F32),
        scratch_shapes=[pltpu.VMEM((C * FFT_N1, 2 * LANES), BF16)],
        compiler_params=_cparams(("parallel",)),
        name="filter_fft",
    )(circ, maf, tw, g)


def _short_conv(x, par_ref, c):
    rows, lanes = x.shape
    a_i = lax.broadcasted_iota(jnp.int32, x.shape, 0)
    b_i = lax.broadcasted_iota(jnp.int32, x.shape, 1)
    l1 = pltpu.roll(x, 1, 1)
    l2 = pltpu.roll(l1, 1, 0)
    prev = jnp.where(b_i == 0, l2, l1)
    prev = jnp.where((a_i == 0) & (b_i == 0), 0.0, prev)
    r1 = pltpu.roll(x, lanes - 1, 1)
    r2 = pltpu.roll(r1, rows - 1, 0)
    nxt = jnp.where(b_i == lanes - 1, r2, r1)
    nxt = jnp.where((a_i == rows - 1) & (b_i == lanes - 1), 0.0, nxt)
    w0 = par_ref[0, pl.ds(c, 1), :]
    w1 = par_ref[1, pl.ds(c, 1), :]
    w2 = par_ref[2, pl.ds(c, 1), :]
    cb = par_ref[3, pl.ds(c, 1), :]
    return cb + prev * w0 + x * w1 + nxt * w2


def _hyena_body(v_ref, x1_ref, x2_ref, pv_ref, p1_ref, p2_ref, fb_ref, h_ref,
                ma_ref, tw_ref, g_ref, ginv_ref, mir_ref, mii_ref,
                o_ref, s1_ref, s2_ref, vc_ref, z1_ref):
    C = v_ref.shape[1]
    half = FFT_N1 // 2

    def spectral(order):
        z = jnp.dot(s1_ref[...], g_ref[...], preferred_element_type=F32)
        hs = h_ref[order].reshape(C * FFT_N1, 2 * LANES)
        zr, zi = z[:, :LANES], z[:, LANES:]
        hr, hi = hs[:, :LANES], hs[:, LANES:]
        pb = jnp.concatenate([zr * hr - zi * hi, zr * hi + zi * hr], axis=1).astype(BF16)
        s2_ref[...] = jnp.dot(pb, ginv_ref[...], preferred_element_type=F32)

    def inv_a(c):
        row0 = pl.multiple_of(c * FFT_N1, FFT_N1)
        y = s2_ref[pl.ds(row0, FFT_N1), :]
        yr, yi = y[:, :LANES], y[:, LANES:]
        twr, twi = tw_ref[:, :LANES], tw_ref[:, LANES:]
        ur = (yr * twr + yi * twi).astype(BF16)
        ui = (yi * twr - yr * twi).astype(BF16)
        out = (jnp.dot(mir_ref[...], ur, preferred_element_type=F32)
               + jnp.dot(mii_ref[...], ui, preferred_element_type=F32))
        return out[:half], out[half:]

    def fwd_a(c, xr, xi):
        xs = jnp.concatenate([xr, xi], axis=0).astype(BF16)
        y = jnp.dot(ma_ref[...], xs, preferred_element_type=F32)
        _fwd_twiddle_store(y, tw_ref, s1_ref, pl.multiple_of(c * FFT_N1, FFT_N1))

    def pass1_a(c, carry):
        vr = _short_conv(v_ref[0, c], pv_ref, c)
        vi = _short_conv(v_ref[1, c], pv_ref, c)
        vc_ref[0, c] = vr
        vc_ref[1, c] = vi
        fwd_a(c, vr, vi)
        return carry

    def pass1_b(c, carry):
        cr, ci = inv_a(c)
        bias = fb_ref[0, pl.ds(c, 1), :]
        zr = _short_conv(x1_ref[0, c], p1_ref, c) * (cr + bias * vc_ref[0, c])
        zi = _short_conv(x1_ref[1, c], p1_ref, c) * (ci + bias * vc_ref[1, c])
        z1_ref[0, c] = zr
        z1_ref[1, c] = zi
        fwd_a(c, zr, zi)
        return carry

    def pass2_b(c, carry):
        cr, ci = inv_a(c)
        bias = fb_ref[1, pl.ds(c, 1), :]
        o_ref[0, c] = _short_conv(x2_ref[0, c], p2_ref, c) * (cr + bias * z1_ref[0, c])
        o_ref[1, c] = _short_conv(x2_ref[1, c], p2_ref, c) * (ci + bias * z1_ref[1, c])
        return carry

    lax.fori_loop(0, C, pass1_a, 0, unroll=SEQ_UNROLL)
    spectral(0)
    lax.fori_loop(0, C, pass1_b, 0, unroll=SEQ_UNROLL)
    spectral(1)
    lax.fori_loop(0, C, pass2_b, 0, unroll=SEQ_UNROLL)


def _hyena(u4, par_u, fb, hspec, cst):
    B = u4.shape[0]
    C = C_HY
    J = D_HYENA // C
    rows = u4.shape[2]
    full = lambda a: pl.BlockSpec(a.shape, lambda j, p: (0,) * a.ndim)
    ma, g, ginv = cst["ma"].astype(BF16), cst["g"].astype(BF16), cst["ginv"].astype(BF16)
    mir, mii = cst["minv_r"].astype(BF16), cst["minv_i"].astype(BF16)
    tw = cst["tw"]
    u_spec = lambda k: pl.BlockSpec((2, C, rows, LANES), lambda j, p, k=k: (p, j + k * J, 0, 0))
    par_spec = lambda k: pl.BlockSpec((4, C, LANES), lambda j, p, k=k: (0, j + k * J, 0))
    return pl.pallas_call(
        _hyena_body,
        grid=(J, B // 2),
        in_specs=[
            u_spec(0), u_spec(1), u_spec(2), par_spec(0), par_spec(1), par_spec(2),
            pl.BlockSpec((2, C, LANES), lambda j, p: (0, j, 0)),
            pl.BlockSpec((2, C, FFT_N1, 2 * LANES), lambda j, p: (0, j, 0, 0)),
            full(ma), full(tw), full(g), full(ginv), full(mir), full(mii),
        ],
        out_specs=pl.BlockSpec((2, C, rows, LANES), lambda j, p: (p, j, 0, 0)),
        out_shape=jax.ShapeDtypeStruct((B, D_HYENA, rows, LANES), F32),
        scratch_shapes=[
            pltpu.VMEM((C * FFT_N1, 2 * LANES), BF16),
            pltpu.VMEM((C * FFT_N1, 2 * LANES), F32),
            pltpu.VMEM((2, C, rows, LANES), F32),
            pltpu.VMEM((2, C, rows, LANES), F32),
        ],
        compiler_params=_cparams(("parallel", "arbitrary")),
        name="hyena",
    )(u4, u4, u4, par_u, par_u, par_u, fb, hspec, ma, tw, g, ginv, mir, mii)


def _filter_time_domain(L, w_f1, b_f1, freq1, w_f2, b_f2, freq2, w_f3):
    hp = lax.Precision.HIGHEST
    bands = (FILTER_EMB - 1) // 2
    t = jnp.linspace(0.0, 1.0, L, dtype=F32)[:, None]
    w = (2.0 * math.pi / L) * jnp.arange(L, dtype=F32)[:, None]
    f = jnp.linspace(1e-4, bands - 1, bands, dtype=F32)[None]
    zf = f * w
    z = jnp.concatenate([t, jnp.cos(zf), -jnp.sin(zf)], axis=-1)
    back = lambda a: jnp.roll(a[::-1], 1, axis=0)
    zz = jnp.concatenate([z, back(z)], axis=0)
    h = jnp.sin(freq1 * (jnp.dot(zz, w_f1, precision=hp) + b_f1))
    h = jnp.sin(freq2 * (jnp.dot(h, w_f2, precision=hp) + b_f2))
    w3 = w_f3.reshape(-1, 2, 2, D_HYENA)
    hf = jnp.einsum("hoc,lh->ocl", w3[:, :, 0], h[:L], precision=hp)
    hb = jnp.einsum("hoc,lh->ocl", w3[:, :, 1], h[L:], precision=hp)
    max_decay = math.log(DECAY_TARGET) / FAST_DECAY_PCT
    min_decay = math.log(DECAY_TARGET) / SLOW_DECAY_PCT
    deltas = jnp.linspace(min_decay, max_decay, D_HYENA, dtype=F32)
    decay_f = jnp.exp(-jnp.abs(deltas)[:, None] * t[None, :, 0])
    decay_b = jnp.exp(-jnp.abs(deltas)[:, None] * back(t)[None, :, 0])
    hf = hf * decay_f[None]
    hb = hb * decay_b[None]
    hf = hf / (jnp.sum(jnp.abs(hf), axis=-1, keepdims=True) + EPS)
    hb = hb / (jnp.sum(jnp.abs(hb), axis=-1, keepdims=True) + EPS)
    first = lax.broadcasted_iota(jnp.int32, hb.shape, 2) == 0
    return jnp.concatenate([hf + jnp.where(first, hb, 0.0), jnp.where(first, 0.0, hb)], axis=-1)


def _route_lanes(lg):
    neg = -1e30
    lane = lax.broadcasted_iota(jnp.int32, lg.shape, 1)
    gmask = lane < N_GROUPS
    gl = jnp.where(gmask, lg, neg)
    gm = jnp.max(gl, axis=-1, keepdims=True)
    gsum = jnp.sum(jnp.where(gmask, jnp.exp(gl - gm), 0.0), axis=-1, keepdims=True)
    g_top = 1.0 / gsum
    g_sel = jnp.min(jnp.where(gl == gm, lane, LANES), axis=-1, keepdims=True)
    lo = N_GROUPS + EXPERTS_PER_GROUP * g_sel
    el = jnp.where((lane >= lo) & (lane < lo + EXPERTS_PER_GROUP), lg, neg)
    m1 = jnp.max(el, axis=-1, keepdims=True)
    i1 = jnp.min(jnp.where(el == m1, lane, LANES), axis=-1, keepdims=True)
    el2 = jnp.where(lane == i1, neg, el)
    m2 = jnp.max(el2, axis=-1, keepdims=True)
    i2 = jnp.min(jnp.where(el2 == m2, lane, LANES), axis=-1, keepdims=True)
    d = jnp.exp(m2 - m1)
    p1 = 1.0 / (1.0 + d)
    p2 = d / (1.0 + d)
    e1 = (i1 - N_GROUPS).astype(F32)
    e2 = (i2 - N_GROUPS).astype(F32)
    return jnp.where(lane == 0, e1, jnp.where(lane == 1, e2, jnp.where(lane == 2, g_top * p1,
                     jnp.where(lane == 3, g_top * p2, 0.0))))


def _pack_bf16_halves(a):
    w = a.shape[1] // 2
    bits = pltpu.bitcast(a.astype(BF16).astype(F32), jnp.uint32)
    return (bits[:, :w] >> 16) | (bits[:, w:] & jnp.uint32(0xFFFF0000))


def _unpack_bf16_halves(wd):
    lo = pltpu.bitcast(wd << 16, F32)
    hi = pltpu.bitcast(wd & jnp.uint32(0xFFFF0000), F32)
    return jnp.concatenate([lo, hi], axis=1)


def _outproj_body(ya_ref, yh_ref, x_ref, ga_ref, gh_ref, wo_ref, bd_ref, gm_ref, wrh_ref, wrl_ref, brt_ref,
                  x1_ref, h2_ref, rt_ref, rtt_ref):
    ya = ya_ref[...]
    yan = ya * lax.rsqrt(_group_sumsq(ya, bd_ref[...]) * (1.0 / HEAD_DIM) + EPS) * ga_ref[...]
    yh = yh_ref[...]
    tm = yh.shape[1]
    yh3 = yh.reshape(D_HYENA // HYENA_HEAD, HYENA_HEAD, tm)
    ms = jnp.mean(yh3 * yh3, axis=1, keepdims=True)
    yhn = (yh3 * lax.rsqrt(ms + EPS)).reshape(D_HYENA, tm) * gh_ref[...]
    mix = (jnp.dot(yan.astype(BF16), wo_ref[:D_ATTN, :], preferred_element_type=F32)
           + jnp.dot(yhn.T.astype(BF16), wo_ref[D_ATTN:, :], preferred_element_type=F32))
    x1 = x_ref[...] + mix
    x1_ref[...] = x1
    h2 = x1 * lax.rsqrt(jnp.mean(x1 * x1, axis=-1, keepdims=True) + EPS) * gm_ref[...]
    h2_ref[...] = _pack_bf16_halves(h2)
    hi = h2.astype(BF16)
    lo = (h2 - hi.astype(F32)).astype(BF16)
    lg = (jnp.dot(hi, wrh_ref[...], preferred_element_type=F32)
          + jnp.dot(lo, wrh_ref[...], preferred_element_type=F32)
          + jnp.dot(hi, wrl_ref[...], preferred_element_type=F32)) + brt_ref[...]
    route = _route_lanes(lg)
    rt_ref[...] = route
    rtt_ref[...] = route.T[:8]


def _outproj(ya, yht, x, ga, gh, wo, bd, gm, wrh, wrl, brt):
    B, S, D = x.shape
    tm = TM_PROJ
    full = lambda a: pl.BlockSpec(a.shape, lambda b, i: (0,) * a.ndim)
    return pl.pallas_call(
        _outproj_body,
        grid=(B, S // tm),
        in_specs=[
            pl.BlockSpec((None, tm, D_ATTN), lambda b, i: (b, i, 0)),
            pl.BlockSpec((None, D_HYENA, tm), lambda b, i: (b, 0, i)),
            pl.BlockSpec((None, tm, D), lambda b, i: (b, i, 0)),
            full(ga), full(gh), full(wo), full(bd), full(gm), full(wrh), full(wrl), full(brt),
        ],
        out_specs=[
            pl.BlockSpec((None, tm, D), lambda b, i: (b, i, 0)),
            pl.BlockSpec((None, tm, D // 2), lambda b, i: (b, i, 0)),
            pl.BlockSpec((None, tm, LANES), lambda b, i: (b, i, 0)),
            pl.BlockSpec((None, 8, tm), lambda b, i: (b, 0, i)),
        ],
        out_shape=[
            jax.ShapeDtypeStruct((B, S, D), F32),
            jax.ShapeDtypeStruct((B, S, D // 2), jnp.uint32),
            jax.ShapeDtypeStruct((B, S, LANES), F32),
            jax.ShapeDtypeStruct((B, 8, S), F32),
        ],
        compiler_params=_cparams(("parallel", "parallel")),
        name="outproj",
    )(ya, yht, x, ga, gh, wo, bd, gm, wrh, wrl, brt)


def _moe_body(be_ref, x_ref, wg_ref, wu_ref, wd_ref, y_ref, wg_s, wu_s, wd_s):
    i = pl.program_id(0)
    prev = be_ref[jnp.maximum(i - 1, 0)]

    @pl.when((i == 0) | (be_ref[i] != prev))
    def _():
        wg_s[...] = wg_ref[...].astype(BF16)
        wu_s[...] = wu_ref[...].astype(BF16)
        wd_s[...] = wd_ref[...].astype(BF16)

    x = _unpack_bf16_halves(x_ref[...]).astype(BF16)
    a = jnp.dot(x, wg_s[...], preferred_element_type=F32)
    b = jnp.dot(x, wu_s[...], preferred_element_type=F32)
    hmid = (a * jax.nn.sigmoid(a)) * b
    y_ref[...] = _pack_bf16_halves(jnp.dot(hmid.astype(BF16), wd_s[...], preferred_element_type=F32))


def _moe_experts(block_e, xs, w_gate, w_up, w_down):
    n_rows = xs.shape[0]
    D = w_gate.shape[1]
    T = TB_MOE
    grid_spec = pltpu.PrefetchScalarGridSpec(
        num_scalar_prefetch=1,
        grid=(n_rows // T,),
        in_specs=[
            pl.BlockSpec((T, D // 2), lambda i, be: (i, 0)),
            pl.BlockSpec((None, D, D_EXPERT), lambda i, be: (be[i], 0, 0)),
            pl.BlockSpec((None, D, D_EXPERT), lambda i, be: (be[i], 0, 0)),
            pl.BlockSpec((None, D_EXPERT, D), lambda i, be: (be[i], 0, 0)),
        ],
        out_specs=pl.BlockSpec((T, D // 2), lambda i, be: (i, 0)),
        scratch_shapes=[
            pltpu.VMEM((D, D_EXPERT), BF16), pltpu.VMEM((D, D_EXPERT), BF16), pltpu.VMEM((D_EXPERT, D), BF16),
        ],
    )
    return pl.pallas_call(
        _moe_body,
        grid_spec=grid_spec,
        out_shape=jax.ShapeDtypeStruct((n_rows, D // 2), jnp.uint32),
        compiler_params=_cparams(("arbitrary",)),
        name="moe_experts",
    )(block_e, xs, w_gate, w_up, w_down)


def _dispatch(e_flat, N):
    T = TB_MOE
    NK = N * TOP_K
    experts = jnp.arange(N_EXPERTS, dtype=jnp.int32)
    order = jnp.argsort(e_flat).astype(jnp.int32)
    tok_sorted = order % N
    onehot = (e_flat[:, None] == experts[None]).astype(jnp.int32)
    counts = jnp.sum(onehot, axis=0)
    ends = jnp.cumsum(counts)
    starts = ends - counts
    padded = (counts + T - 1) // T * T
    pends = jnp.cumsum(padded)
    pstarts = pends - padded
    n_rows = -(-(NK + N_EXPERTS * (T - 1)) // T) * T
    n_blocks = n_rows // T
    blk_start = jnp.arange(n_blocks, dtype=jnp.int32) * T
    block_e = jnp.clip(jnp.sum((pends[None, :] <= blk_start[:, None]).astype(jnp.int32), axis=1),
                       0, N_EXPERTS - 1)
    oh_b = (block_e[:, None] == experts[None]).astype(jnp.int32)
    base = jnp.sum(oh_b * (starts - pstarts)[None], axis=1) + blk_start
    end_b = jnp.sum(oh_b * ends[None], axis=1)
    src = base[:, None] + jnp.arange(T, dtype=jnp.int32)[None]
    filler = (blk_start[:, None] + jnp.arange(T, dtype=jnp.int32)[None]) % N
    row_tok = jnp.where(src < end_b[:, None], tok_sorted[jnp.clip(src, 0, NK - 1)], filler).reshape(n_rows)
    inv = jnp.argsort(order).astype(jnp.int32)
    pos = (inv + jnp.sum(onehot * (pstarts - starts)[None], axis=1)).reshape(TOP_K, N)
    return row_tok, block_e.astype(jnp.int32), pos


def _final_body(x1_ref, y0_ref, y1_ref, rt_ref, p_ref, gp_ref, wg_ref, bg_ref, wp_ref, gf_ref, o_ref):
    w0 = rt_ref[:, 2:3]
    w1 = rt_ref[:, 3:4]
    x2 = x1_ref[...] + (_unpack_bf16_halves(y0_ref[...]) * w0 + _unpack_bf16_halves(y1_ref[...]) * w1)
    hp = x2 * lax.rsqrt(jnp.mean(x2 * x2, axis=-1, keepdims=True) + EPS) * gp_ref[...]
    gate = jax.nn.sigmoid(jnp.dot(hp.astype(BF16), wg_ref[...], preferred_element_type=F32) + bg_ref[...])
    pe = jnp.dot(p_ref[...].astype(BF16), wp_ref[...], preferred_element_type=F32)
    x3 = x2 + pe * gate
    o_ref[...] = x3 * lax.rsqrt(jnp.mean(x3 * x3, axis=-1, keepdims=True) + EPS) * gf_ref[...]


def _final(x1, y0, y1, route, p, gp, wg, bg, wp, gf):
    N, D = x1.shape
    tm = TM_PROJ
    row = lambda w: pl.BlockSpec((tm, w), lambda i: (i, 0))
    full = lambda a: pl.BlockSpec(a.shape, lambda i: (0,) * a.ndim)
    return pl.pallas_call(
        _final_body,
        grid=(N // tm,),
        in_specs=[row(D), row(D // 2), row(D // 2), row(LANES), row(p.shape[1]),
                  full(gp), full(wg), full(bg), full(wp), full(gf)],
        out_specs=row(D),
        out_shape=jax.ShapeDtypeStruct((N, D), F32),
        compiler_params=_cparams(("parallel",)),
        name="ple_final",
    )(x1, y0, y1, route, p, gp, wg, bg, wp, gf)


def kernel(x, p, g_mix, w_in, q_gain, k_gain, conv_w, conv_b, w_f1, b_f1, freq1, w_f2, b_f2, freq2, w_f3, filt_bias, g_attn_out, g_hyena_out, w_out, g_moe, w_group, b_group, w_router, b_router, w_gate, w_up, w_down, g_ple, w_ple_gate, b_ple_gate, w_ple, g_final):
    B, S, D = x.shape
    N = B * S
    assert p.shape[0] == 1 and S == (FFT_N1 // 2) * FFT_N2 and B % 2 == 0
    i = 0
    cst = _dft_constants()
    cos, sin = _rope_tables(S)
    bd = _block_diag_ones(D_ATTN, HEAD_DIM)

    n_qkv = D_ATTN + 2 * D_KV
    wqkv = w_in[i][:, :n_qkv].astype(BF16)
    wut = w_in[i][:, n_qkv:].T.astype(BF16)
    q, kw, vw, ut = _inproj(x, g_mix[i][None], wqkv, wut, bd,
                            jnp.tile(q_gain[i], N_HEADS)[None], jnp.tile(k_gain[i], N_KV_HEADS)[None], cos, sin)

    ya = _attention(q, kw, vw)

    circ = _filter_time_domain(S, w_f1[i], b_f1[i], freq1[i], w_f2[i], b_f2[i], freq2[i], w_f3[i])
    hspec = _filter_fft(circ.reshape(2 * D_HYENA, FFT_N1, FFT_N2), cst)
    hspec = hspec.reshape(2, D_HYENA, FFT_N1, 2 * LANES)
    du = ut.shape[1]
    u4 = ut.reshape(B, du, S // LANES, LANES)
    par_u = jnp.broadcast_to(jnp.concatenate([conv_w[i], conv_b[i][None]], 0)[:, :, None], (4, du, LANES))
    fb = jnp.broadcast_to(filt_bias[i][:, :, None], (2, D_HYENA, LANES))
    yh4 = _hyena(u4, par_u, fb, hspec, cst)
    yht = yh4.reshape(B, D_HYENA, S)

    wrt = jnp.zeros((D, LANES), F32).at[:, :N_GROUPS].set(w_group[i]).at[:, N_GROUPS:N_GROUPS + N_EXPERTS].set(w_router[i])
    brt = jnp.zeros((1, LANES), F32).at[0, :N_GROUPS].set(b_group[i]).at[0, N_GROUPS:N_GROUPS + N_EXPERTS].set(b_router[i])
    wrh = wrt.astype(BF16)
    wrl = (wrt - wrh.astype(F32)).astype(BF16)
    x1, h2, route, route_t = _outproj(ya, yht, x, g_attn_out[i][None], g_hyena_out[i][:, None],
                                      w_out[i].astype(BF16), bd, g_moe[i][None], wrh, wrl, brt)

    e_flat = jnp.transpose(route_t[:, :TOP_K], (1, 0, 2)).reshape(TOP_K * N).astype(jnp.int32)
    row_tok, block_e, pos = _dispatch(e_flat, N)
    xs = h2.reshape(N, D // 2)[row_tok]
    yb = _moe_experts(block_e, xs, w_gate[i], w_up[i], w_down[i])
    y0 = yb[pos[0]]
    y1 = yb[pos[1]]

    out = _final(x1.reshape(N, D), y0, y1, route.reshape(N, LANES), p[i].reshape(N, -1), g_ple[i][None],
                 w_ple_gate[i].astype(BF16), b_ple_gate[i][None], w_ple[i].astype(BF16), g_final[None])
    return out.reshape(B, S, D)
```

```python
import functools
import math

import numpy as np
import jax
import jax.numpy as jnp
from jax import lax
from jax.experimental import pallas as pl
from jax.experimental.pallas import tpu as pltpu

F32 = jnp.float32
BF16 = jnp.bfloat16

D_MODEL = 1024
EPS = 1e-6
GRID_W = 64
N_HEADS = 8
N_KV_HEADS = 2
HEAD_DIM = 64
D_ATTN = N_HEADS * HEAD_DIM
D_KV = N_KV_HEADS * HEAD_DIM
ROPE_THETA = 10000.0
D_HYENA = 512
HYENA_HEAD = 64
FILTER_EMB = 33
FAST_DECAY_PCT = 0.3
SLOW_DECAY_PCT = 1.5
DECAY_TARGET = 1e-2
N_GROUPS = 4
EXPERTS_PER_GROUP = 8
N_EXPERTS = N_GROUPS * EXPERTS_PER_GROUP
TOP_K = 2
D_EXPERT = 512

LANES = 128
FFT_N1 = 64
FFT_N2 = 128
VMEM_LIMIT = 56 * 1024 * 1024

TM_PROJ = 512
TQ_ATTN = 256
C_HY = 32
SEQ_UNROLL = 8
TB_MOE = 256


def _cparams(sem):
    return pltpu.CompilerParams(dimension_semantics=sem, vmem_limit_bytes=VMEM_LIMIT)


def _rope_tables(S):
    half = HEAD_DIM // 2
    t = jnp.arange(S, dtype=F32)
    r_idx = jnp.floor(t / GRID_W)
    c_idx = t - r_idx * GRID_W
    inv = ROPE_THETA ** (-jnp.arange(0, half, 2, dtype=F32) / half)
    ang_r = r_idx[:, None] * inv[None]
    ang_c = c_idx[:, None] * inv[None]
    cos_h = jnp.concatenate([jnp.cos(ang_r), jnp.cos(ang_r), jnp.cos(ang_c), jnp.cos(ang_c)], axis=-1)
    sin_h = jnp.concatenate([-jnp.sin(ang_r), jnp.sin(ang_r), -jnp.sin(ang_c), jnp.sin(ang_c)], axis=-1)
    return jnp.tile(cos_h, (1, 2)), jnp.tile(sin_h, (1, 2))


def _dft_constants():
    n1, n2 = FFT_N1, FFT_N2
    n = n1 * n2
    a = np.arange(n1)
    ang = 2.0 * np.pi * np.outer(a, a) / n1
    far, fai = np.cos(ang), -np.sin(ang)
    hlf = n1 // 2
    ma = np.block([[far[:, :hlf], -fai[:, :hlf]], [fai[:, :hlf], far[:, :hlf]]])
    maf = np.concatenate([far, fai], axis=0)
    b = np.arange(n2)
    angt = 2.0 * np.pi * np.outer(a, b) / n
    tw = np.concatenate([np.cos(angt), -np.sin(angt)], axis=1)
    angb = 2.0 * np.pi * np.outer(b, b) / n2
    fbr, fbi = np.cos(angb), -np.sin(angb)
    g = np.block([[fbr, fbi], [-fbi, fbr]])
    ginv = np.block([[fbr, -fbi], [fbi, fbr]])
    minv_r = np.concatenate([far[:hlf], -fai[:hlf]], axis=0) / n
    minv_i = np.concatenate([fai[:hlf], far[:hlf]], axis=0) / n
    f = lambda m: jnp.asarray(m.astype(np.float32))
    return dict(ma=f(ma), maf=f(maf), tw=f(tw), g=f(g), ginv=f(ginv), minv_r=f(minv_r), minv_i=f(minv_i))


def _block_diag_ones(width, group):
    i = np.arange(width) // group
    return jnp.asarray((i[:, None] == i[None, :]).astype(np.float32)).astype(BF16)


def _group_sumsq(a, bd):
    sq = a * a
    hi = sq.astype(BF16)
    lo = (sq - hi.astype(F32)).astype(BF16)
    return (jnp.dot(hi, bd, preferred_element_type=F32) + jnp.dot(lo, bd, preferred_element_type=F32))


def _head_norm_rope(a, gain, bd, cos, sin):
    width = a.shape[-1]
    n = a * lax.rsqrt(_group_sumsq(a, bd) * (1.0 / HEAD_DIM) + EPS) * gain
    rep = width // LANES
    if rep > 1:
        cos = jnp.concatenate([cos] * rep, axis=-1)
        sin = jnp.concatenate([sin] * rep, axis=-1)
    fwd = pltpu.roll(n, width - 16, 1)
    bwd = pltpu.roll(n, 16, 1)
    lane = lax.broadcasted_iota(jnp.int32, n.shape, 1)
    sw = jnp.where((lane % 32) < 16, fwd, bwd)
    return n * cos + sw * sin


def _inproj_body(x_ref, g_ref, wqkv_ref, wu_ref, bd_ref, qg_ref, kg_ref, cos_ref, sin_ref,
                 q_ref, kw_ref, vw_ref, ut_ref):
    x = x_ref[...]
    h = x * lax.rsqrt(jnp.mean(x * x, axis=-1, keepdims=True) + EPS) * g_ref[...]
    hb = h.astype(BF16)
    qkv = jnp.dot(hb, wqkv_ref[...], preferred_element_type=F32)
    cos = cos_ref[...]
    sin = sin_ref[...]
    bd = bd_ref[...]
    q = _head_norm_rope(qkv[:, :D_ATTN], qg_ref[...], bd, cos, sin)
    q_ref[...] = (q * (HEAD_DIM ** -0.5 * math.log2(math.e))).astype(BF16)
    k = _head_norm_rope(qkv[:, D_ATTN:D_ATTN + D_KV], kg_ref[...], bd[:D_KV, :D_KV], cos, sin)
    kt = k.T.astype(BF16)
    zero = jnp.zeros((HEAD_DIM, kt.shape[1]), BF16)
    for h in range(N_KV_HEADS):
        kh = kt[h * HEAD_DIM:(h + 1) * HEAD_DIM]
        kw_ref[h, 0, :HEAD_DIM] = kh
        kw_ref[h, 0, HEAD_DIM:] = zero
        kw_ref[h, 1, :HEAD_DIM] = zero
        kw_ref[h, 1, HEAD_DIM:] = kh
    v = qkv[:, D_ATTN + D_KV:]
    vr = pltpu.roll(v, HEAD_DIM, 1)
    first = lax.broadcasted_iota(jnp.int32, v.shape, 1) < HEAD_DIM
    vw_ref[0, 0] = jnp.where(first, v, 1.0).astype(BF16)
    vw_ref[0, 1] = jnp.where(first, 1.0, vr).astype(BF16)
    vw_ref[1, 0] = jnp.where(first, vr, 1.0).astype(BF16)
    vw_ref[1, 1] = jnp.where(first, 1.0, v).astype(BF16)
    ut_ref[...] = lax.dot_general(wu_ref[...], hb, (((1,), (1,)), ((), ())),
                                  preferred_element_type=F32)


def _inproj(x, g_mix, wqkv, wut, bd, qg, kg, cos, sin):
    B, S, D = x.shape
    tm = TM_PROJ
    du = wut.shape[0]
    full = lambda shape: pl.BlockSpec(shape, lambda b, i: (0,) * len(shape))
    return pl.pallas_call(
        _inproj_body,
        grid=(B, S // tm),
        in_specs=[
            pl.BlockSpec((None, tm, D), lambda b, i: (b, i, 0)),
            full((1, D)), full(wqkv.shape), full(wut.shape), full(bd.shape),
            full((1, D_ATTN)), full((1, D_KV)),
            pl.BlockSpec((tm, LANES), lambda b, i: (i, 0)),
            pl.BlockSpec((tm, LANES), lambda b, i: (i, 0)),
        ],
        out_specs=[
            pl.BlockSpec((None, tm, D_ATTN), lambda b, i: (b, i, 0)),
            pl.BlockSpec((None, N_KV_HEADS, 2, LANES, tm), lambda b, i: (b, 0, 0, 0, i)),
            pl.BlockSpec((None, N_KV_HEADS, 2, tm, LANES), lambda b, i: (b, 0, 0, i, 0)),
            pl.BlockSpec((None, du, tm), lambda b, i: (b, 0, i)),
        ],
        out_shape=[
            jax.ShapeDtypeStruct((B, S, D_ATTN), BF16),
            jax.ShapeDtypeStruct((B, N_KV_HEADS, 2, LANES, S), BF16),
            jax.ShapeDtypeStruct((B, N_KV_HEADS, 2, S, LANES), BF16),
            jax.ShapeDtypeStruct((B, du, S), F32),
        ],
        compiler_params=_cparams(("parallel", "parallel")),
        name="inproj",
    )(x, g_mix, wqkv, wut, bd, qg, kg, cos, sin)


def _attn_body(q_ref, kw_ref, vw_ref, o_ref):

    def one_head(q, kw, vw):
        s = jnp.dot(q, kw, preferred_element_type=F32)
        m = jnp.max(s, axis=-1, keepdims=True)
        p = jnp.exp2(s - m).astype(BF16)
        return jnp.dot(p, vw, preferred_element_type=F32)

    for pair in range(D_ATTN // LANES):
        h = pair // (N_HEADS // N_KV_HEADS // 2)
        q = q_ref[:, pair * LANES:(pair + 1) * LANES]
        oe = one_head(q, kw_ref[h, 0], vw_ref[h, 0])
        oo = one_head(q, kw_ref[h, 1], vw_ref[h, 1])
        first = lax.broadcasted_iota(jnp.int32, oe.shape, 1) < HEAD_DIM
        num = jnp.where(first, oe, oo)
        den = jnp.where(first, pltpu.roll(oe, HEAD_DIM, 1), pltpu.roll(oo, HEAD_DIM, 1))
        o_ref[:, pair * LANES:(pair + 1) * LANES] = num / den


def _attention(q, kw, vw):
    B, S, _ = q.shape
    tq = TQ_ATTN
    return pl.pallas_call(
        _attn_body,
        grid=(B, S // tq),
        in_specs=[
            pl.BlockSpec((None, tq, D_ATTN), lambda b, i: (b, i, 0)),
            pl.BlockSpec((None, N_KV_HEADS, 2, LANES, S), lambda b, i: (b, 0, 0, 0, 0)),
            pl.BlockSpec((None, N_KV_HEADS, 2, S, LANES), lambda b, i: (b, 0, 0, 0, 0)),
        ],
        out_specs=pl.BlockSpec((None, tq, D_ATTN), lambda b, i: (b, i, 0)),
        out_shape=jax.ShapeDtypeStruct((B, S, D_ATTN), F32),
        compiler_params=_cparams(("parallel", "arbitrary")),
        name="attention",
    )(q, kw, vw)


def _fwd_twiddle_store(y, tw_ref, s1_ref, row0):
    yr, yi = y[:FFT_N1], y[FFT_N1:]
    twr, twi = tw_ref[:, :LANES], tw_ref[:, LANES:]
    s1_ref[pl.ds(row0, FFT_N1), :LANES] = (yr * twr - yi * twi).astype(BF16)
    s1_ref[pl.ds(row0, FFT_N1), LANES:] = (yr * twi + yi * twr).astype(BF16)


def _filtfft_body(x_ref, maf_ref, tw_ref, g_ref, h_ref, s1_ref):
    C = x_ref.shape[0]

    def step_a(c, carry):
        y = jnp.dot(maf_ref[...], x_ref[c].astype(BF16), preferred_element_type=F32)
        _fwd_twiddle_store(y, tw_ref, s1_ref, pl.multiple_of(c * FFT_N1, FFT_N1))
        return carry

    lax.fori_loop(0, C, step_a, 0, unroll=SEQ_UNROLL)
    z = jnp.dot(s1_ref[...], g_ref[...], preferred_element_type=F32)
    h_ref[...] = z.reshape(C, FFT_N1, 2 * LANES)


def _filter_fft(circ, cst):
    n_seq = circ.shape[0]
    C = C_HY
    full = lambda a: pl.BlockSpec(a.shape, lambda i: (0,) * a.ndim)
    maf, tw, g = cst["maf"].astype(BF16), cst["tw"], cst["g"].astype(BF16)
    return pl.pallas_call(
        _filtfft_body,
        grid=(n_seq // C,),
        in_specs=[pl.BlockSpec((C, FFT_N1, FFT_N2), lambda i: (i, 0, 0)), full(maf), full(tw), full(g)],
        out_specs=pl.BlockSpec((C, FFT_N1, 2 * LANES), lambda i: (i, 0, 0)),
        out_shape=jax.ShapeDtypeStruct((n_seq, FFT_N1, 2 * LANES), F32),
        scratch_shapes=[pltpu.VMEM((C * FFT_N1, 2 * LANES), BF16)],
        compiler_params=_cparams(("parallel",)),
        name="filter_fft",
    )(circ, maf, tw, g)


def _short_conv(x, par_ref, c):
    rows, lanes = x.shape
    a_i = lax.broadcasted_iota(jnp.int32, x.shape, 0)
    b_i = lax.broadcasted_iota(jnp.int32, x.shape, 1)
    l1 = pltpu.roll(x, 1, 1)
    l2 = pltpu.roll(l1, 1, 0)
    prev = jnp.where(b_i == 0, l2, l1)
    prev = jnp.where((a_i == 0) & (b_i == 0), 0.0, prev)
    r1 = pltpu.roll(x, lanes - 1, 1)
    r2 = pltpu.roll(r1, rows - 1, 0)
    nxt = jnp.where(b_i == lanes - 1, r2, r1)
    nxt = jnp.where((a_i == rows - 1) & (b_i == lanes - 1), 0.0, nxt)
    w0 = par_ref[0, pl.ds(c, 1), :]
    w1 = par_ref[1, pl.ds(c, 1), :]
    w2 = par_ref[2, pl.ds(c, 1), :]
    cb = par_ref[3, pl.ds(c, 1), :]
    return cb + prev * w0 + x * w1 + nxt * w2


def _hyena_body(v_ref, x1_ref, x2_ref, pv_ref, p1_ref, p2_ref, fb_ref, h_ref,
                ma_ref, tw_ref, g_ref, ginv_ref, mir_ref, mii_ref,
                o_ref, s1_ref, s2_ref, vc_ref, z1_ref):
    C = v_ref.shape[1]
    half = FFT_N1 // 2

    def spectral(order):
        z = jnp.dot(s1_ref[...], g_ref[...], preferred_element_type=F32)
        hs = h_ref[order].reshape(C * FFT_N1, 2 * LANES)
        zr, zi = z[:, :LANES], z[:, LANES:]
        hr, hi = hs[:, :LANES], hs[:, LANES:]
        pb = jnp.concatenate([zr * hr - zi * hi, zr * hi + zi * hr], axis=1).astype(BF16)
        s2_ref[...] = jnp.dot(pb, ginv_ref[...], preferred_element_type=F32)

    def inv_a(c):
        row0 = pl.multiple_of(c * FFT_N1, FFT_N1)
        y = s2_ref[pl.ds(row0, FFT_N1), :]
        yr, yi = y[:, :LANES], y[:, LANES:]
        twr, twi = tw_ref[:, :LANES], tw_ref[:, LANES:]
        ur = (yr * twr + yi * twi).astype(BF16)
        ui = (yi * twr - yr * twi).astype(BF16)
        out = (jnp.dot(mir_ref[...], ur, preferred_element_type=F32)
               + jnp.dot(mii_ref[...], ui, preferred_element_type=F32))
        return out[:half], out[half:]

    def fwd_a(c, xr, xi):
        xs = jnp.concatenate([xr, xi], axis=0).astype(BF16)
        y = jnp.dot(ma_ref[...], xs, preferred_element_type=F32)
        _fwd_twiddle_store(y, tw_ref, s1_ref, pl.multiple_of(c * FFT_N1, FFT_N1))

    def pass1_a(c, carry):
        vr = _short_conv(v_ref[0, c], pv_ref, c)
        vi = _short_conv(v_ref[1, c], pv_ref, c)
        vc_ref[0, c] = vr
        vc_ref[1, c] = vi
        fwd_a(c, vr, vi)
        return carry

    def pass1_b(c, carry):
        cr, ci = inv_a(c)
        bias = fb_ref[0, pl.ds(c, 1), :]
        zr = _short_conv(x1_ref[0, c], p1_ref, c) * (cr + bias * vc_ref[0, c])
        zi = _short_conv(x1_ref[1, c], p1_ref, c) * (ci + bias * vc_ref[1, c])
        z1_ref[0, c] = zr
        z1_ref[1, c] = zi
        fwd_a(c, zr, zi)
        return carry

    def pass2_b(c, carry):
        cr, ci = inv_a(c)
        bias = fb_ref[1, pl.ds(c, 1), :]
        o_ref[0, c] = _short_conv(x2_ref[0, c], p2_ref, c) * (cr + bias * z1_ref[0, c])
        o_ref[1, c] = _short_conv(x2_ref[1, c], p2_ref, c) * (ci + bias * z1_ref[1, c])
        return carry

    lax.fori_loop(0, C, pass1_a, 0, unroll=SEQ_UNROLL)
    spectral(0)
    lax.fori_loop(0, C, pass1_b, 0, unroll=SEQ_UNROLL)
    spectral(1)
    lax.fori_loop(0, C, pass2_b, 0, unroll=SEQ_UNROLL)


def _hyena(u4, par_u, fb, hspec, cst):
    B = u4.shape[0]
    C = C_HY
    J = D_HYENA // C
    rows = u4.shape[2]
    full = lambda a: pl.BlockSpec(a.shape, lambda j, p: (0,) * a.ndim)
    ma, g, ginv = cst["ma"].astype(BF16), cst["g"].astype(BF16), cst["ginv"].astype(BF16)
    mir, mii = cst["minv_r"].astype(BF16), cst["minv_i"].astype(BF16)
    tw = cst["tw"]
    u_spec = lambda k: pl.BlockSpec((2, C, rows, LANES), lambda j, p, k=k: (p, j + k * J, 0, 0))
    par_spec = lambda k: pl.BlockSpec((4, C, LANES), lambda j, p, k=k: (0, j + k * J, 0))
    return pl.pallas_call(
        _hyena_body,
        grid=(J, B // 2),
        in_specs=[
            u_spec(0), u_spec(1), u_spec(2), par_spec(0), par_spec(1), par_spec(2),
            pl.BlockSpec((2, C, LANES), lambda j, p: (0, j, 0)),
            pl.BlockSpec((2, C, FFT_N1, 2 * LANES), lambda j, p: (0, j, 0, 0)),
            full(ma), full(tw), full(g), full(ginv), full(mir), full(mii),
        ],
        out_specs=pl.BlockSpec((2, C, rows, LANES), lambda j, p: (p, j, 0, 0)),
        out_shape=jax.ShapeDtypeStruct((B, D_HYENA, rows, LANES), F32),
        scratch_shapes=[
            pltpu.VMEM((C * FFT_N1, 2 * LANES), BF16),
            pltpu.VMEM((C * FFT_N1, 2 * LANES), F32),
            pltpu.VMEM((2, C, rows, LANES), F32),
            pltpu.VMEM((2, C, rows, LANES), F32),
        ],
        compiler_params=_cparams(("parallel", "arbitrary")),
        name="hyena",
    )(u4, u4, u4, par_u, par_u, par_u, fb, hspec, ma, tw, g, ginv, mir, mii)


def _filter_time_domain(L, w_f1, b_f1, freq1, w_f2, b_f2, freq2, w_f3):
    hp = lax.Precision.HIGHEST
    bands = (FILTER_EMB - 1) // 2
    t = jnp.linspace(0.0, 1.0, L, dtype=F32)[:, None]
    w = (2.0 * math.pi / L) * jnp.arange(L, dtype=F32)[:, None]
    f = jnp.linspace(1e-4, bands - 1, bands, dtype=F32)[None]
    zf = f * w
    z = jnp.concatenate([t, jnp.cos(zf), -jnp.sin(zf)], axis=-1)
    back = lambda a: jnp.roll(a[::-1], 1, axis=0)
    zz = jnp.concatenate([z, back(z)], axis=0)
    h = jnp.sin(freq1 * (jnp.dot(zz, w_f1, precision=hp) + b_f1))
    h = jnp.sin(freq2 * (jnp.dot(h, w_f2, precision=hp) + b_f2))
    w3 = w_f3.reshape(-1, 2, 2, D_HYENA)
    hf = jnp.einsum("hoc,lh->ocl", w3[:, :, 0], h[:L], precision=hp)
    hb = jnp.einsum("hoc,lh->ocl", w3[:, :, 1], h[L:], precision=hp)
    max_decay = math.log(DECAY_TARGET) / FAST_DECAY_PCT
    min_decay = math.log(DECAY_TARGET) / SLOW_DECAY_PCT
    deltas = jnp.linspace(min_decay, max_decay, D_HYENA, dtype=F32)
    decay_f = jnp.exp(-jnp.abs(deltas)[:, None] * t[None, :, 0])
    decay_b = jnp.exp(-jnp.abs(deltas)[:, None] * back(t)[None, :, 0])
    hf = hf * decay_f[None]
    hb = hb * decay_b[None]
    hf = hf / (jnp.sum(jnp.abs(hf), axis=-1, keepdims=True) + EPS)
    hb = hb / (jnp.sum(jnp.abs(hb), axis=-1, keepdims=True) + EPS)
    first = lax.broadcasted_iota(jnp.int32, hb.shape, 2) == 0
    return jnp.concatenate([hf + jnp.where(first, hb, 0.0), jnp.where(first, 0.0, hb)], axis=-1)


def _route_lanes(lg):
    neg = -1e30
    lane = lax.broadcasted_iota(jnp.int32, lg.shape, 1)
    gmask = lane < N_GROUPS
    gl = jnp.where(gmask, lg, neg)
    gm = jnp.max(gl, axis=-1, keepdims=True)
    gsum = jnp.sum(jnp.where(gmask, jnp.exp(gl - gm), 0.0), axis=-1, keepdims=True)
    g_top = 1.0 / gsum
    g_sel = jnp.min(jnp.where(gl == gm, lane, LANES), axis=-1, keepdims=True)
    lo = N_GROUPS + EXPERTS_PER_GROUP * g_sel
    el = jnp.where((lane >= lo) & (lane < lo + EXPERTS_PER_GROUP), lg, neg)
    m1 = jnp.max(el, axis=-1, keepdims=True)
    i1 = jnp.min(jnp.where(el == m1, lane, LANES), axis=-1, keepdims=True)
    el2 = jnp.where(lane == i1, neg, el)
    m2 = jnp.max(el2, axis=-1, keepdims=True)
    i2 = jnp.min(jnp.where(el2 == m2, lane, LANES), axis=-1, keepdims=True)
    d = jnp.exp(m2 - m1)
    p1 = 1.0 / (1.0 + d)
    p2 = d / (1.0 + d)
    e1 = (i1 - N_GROUPS).astype(F32)
    e2 = (i2 - N_GROUPS).astype(F32)
    return jnp.where(lane == 0, e1, jnp.where(lane == 1, e2, jnp.where(lane == 2, g_top * p1,
                     jnp.where(lane == 3, g_top * p2, 0.0))))


def _pack_bf16_halves(a):
    w = a.shape[1] // 2
    bits = pltpu.bitcast(a.astype(BF16).astype(F32), jnp.uint32)
    return (bits[:, :w] >> 16) | (bits[:, w:] & jnp.uint32(0xFFFF0000))


def _unpack_bf16_halves(wd):
    lo = pltpu.bitcast(wd << 16, F32)
    hi = pltpu.bitcast(wd & jnp.uint32(0xFFFF0000), F32)
    return jnp.concatenate([lo, hi], axis=1)


def _outproj_body(ya_ref, yh_ref, x_ref, ga_ref, gh_ref, wo_ref, bd_ref, gm_ref, wrh_ref, wrl_ref, brt_ref,
                  x1_ref, h2_ref, rt_ref, rtt_ref):
    ya = ya_ref[...]
    yan = ya * lax.rsqrt(_group_sumsq(ya, bd_ref[...]) * (1.0 / HEAD_DIM) + EPS) * ga_ref[...]
    yh = yh_ref[...]
    tm = yh.shape[1]
    yh3 = yh.reshape(D_HYENA // HYENA_HEAD, HYENA_HEAD, tm)
    ms = jnp.mean(yh3 * yh3, axis=1, keepdims=True)
    yhn = (yh3 * lax.rsqrt(ms + EPS)).reshape(D_HYENA, tm) * gh_ref[...]
    mix = (jnp.dot(yan.astype(BF16), wo_ref[:D_ATTN, :], preferred_element_type=F32)
           + jnp.dot(yhn.T.astype(BF16), wo_ref[D_ATTN:, :], preferred_element_type=F32))
    x1 = x_ref[...] + mix
    x1_ref[...] = x1
    h2 = x1 * lax.rsqrt(jnp.mean(x1 * x1, axis=-1, keepdims=True) + EPS) * gm_ref[...]
    h2_ref[...] = _pack_bf16_halves(h2)
    hi = h2.astype(BF16)
    lo = (h2 - hi.astype(F32)).astype(BF16)
    lg = (jnp.dot(hi, wrh_ref[...], preferred_element_type=F32)
          + jnp.dot(lo, wrh_ref[...], preferred_element_type=F32)
          + jnp.dot(hi, wrl_ref[...], preferred_element_type=F32)) + brt_ref[...]
    route = _route_lanes(lg)
    rt_ref[...] = route
    rtt_ref[...] = route.T[:8]


def _outproj(ya, yht, x, ga, gh, wo, bd, gm, wrh, wrl, brt):
    B, S, D = x.shape
    tm = TM_PROJ
    full = lambda a: pl.BlockSpec(a.shape, lambda b, i: (0,) * a.ndim)
    return pl.pallas_call(
        _outproj_body,
        grid=(B, S // tm),
        in_specs=[
            pl.BlockSpec((None, tm, D_ATTN), lambda b, i: (b, i, 0)),
            pl.BlockSpec((None, D_HYENA, tm), lambda b, i: (b, 0, i)),
            pl.BlockSpec((None, tm, D), lambda b, i: (b, i, 0)),
            full(ga), full(gh), full(wo), full(bd), full(gm), full(wrh), full(wrl), full(brt),
        ],
        out_specs=[
            pl.BlockSpec((None, tm, D), lambda b, i: (b, i, 0)),
            pl.BlockSpec((None, tm, D // 2), lambda b, i: (b, i, 0)),
            pl.BlockSpec((None, tm, LANES), lambda b, i: (b, i, 0)),
            pl.BlockSpec((None, 8, tm), lambda b, i: (b, 0, i)),
        ],
        out_shape=[
            jax.ShapeDtypeStruct((B, S, D), F32),
            jax.ShapeDtypeStruct((B, S, D // 2), jnp.uint32),
            jax.ShapeDtypeStruct((B, S, LANES), F32),
            jax.ShapeDtypeStruct((B, 8, S), F32),
        ],
        compiler_params=_cparams(("parallel", "parallel")),
        name="outproj",
    )(ya, yht, x, ga, gh, wo, bd, gm, wrh, wrl, brt)


def _moe_body(be_ref, x_ref, wg_ref, wu_ref, wd_ref, y_ref, wg_s, wu_s, wd_s):
    i = pl.program_id(0)
    prev = be_ref[jnp.maximum(i - 1, 0)]

    @pl.when((i == 0) | (be_ref[i] != prev))
    def _():
        wg_s[...] = wg_ref[...].astype(BF16)
        wu_s[...] = wu_ref[...].astype(BF16)
        wd_s[...] = wd_ref[...].astype(BF16)

    x = _unpack_bf16_halves(x_ref[...]).astype(BF16)
    a = jnp.dot(x, wg_s[...], preferred_element_type=F32)
    b = jnp.dot(x, wu_s[...], preferred_element_type=F32)
    hmid = (a * jax.nn.sigmoid(a)) * b
    y_ref[...] = _pack_bf16_halves(jnp.dot(hmid.astype(BF16), wd_s[...], preferred_element_type=F32))


def _moe_experts(block_e, xs, w_gate, w_up, w_down):
    n_rows = xs.shape[0]
    D = w_gate.shape[1]
    T = TB_MOE
    grid_spec = pltpu.PrefetchScalarGridSpec(
        num_scalar_prefetch=1,
        grid=(n_rows // T,),
        in_specs=[
            pl.BlockSpec((T, D // 2), lambda i, be: (i, 0)),
            pl.BlockSpec((None, D, D_EXPERT), lambda i, be: (be[i], 0, 0)),
            pl.BlockSpec((None, D, D_EXPERT), lambda i, be: (be[i], 0, 0)),
            pl.BlockSpec((None, D_EXPERT, D), lambda i, be: (be[i], 0, 0)),
        ],
        out_specs=pl.BlockSpec((T, D // 2), lambda i, be: (i, 0)),
        scratch_shapes=[
            pltpu.VMEM((D, D_EXPERT), BF16), pltpu.VMEM((D, D_EXPERT), BF16), pltpu.VMEM((D_EXPERT, D), BF16),
        ],
    )
    return pl.pallas_call(
        _moe_body,
        grid_spec=grid_spec,
        out_shape=jax.ShapeDtypeStruct((n_rows, D // 2), jnp.uint32),
        compiler_params=_cparams(("arbitrary",)),
        name="moe_experts",
    )(block_e, xs, w_gate, w_up, w_down)


def _dispatch(e_flat, N):
    T = TB_MOE
    NK = N * TOP_K
    experts = jnp.arange(N_EXPERTS, dtype=jnp.int32)
    order = jnp.argsort(e_flat).astype(jnp.int32)
    tok_sorted = order % N
    onehot = (e_flat[:, None] == experts[None]).astype(jnp.int32)
    counts = jnp.sum(onehot, axis=0)
    ends = jnp.cumsum(counts)
    starts = ends - counts
    padded = (counts + T - 1) // T * T
    pends = jnp.cumsum(padded)
    pstarts = pends - padded
    n_rows = -(-(NK + N_EXPERTS * (T - 1)) // T) * T
    n_blocks = n_rows // T
    blk_start = jnp.arange(n_blocks, dtype=jnp.int32) * T
    block_e = jnp.clip(jnp.sum((pends[None, :] <= blk_start[:, None]).astype(jnp.int32), axis=1),
                       0, N_EXPERTS - 1)
    oh_b = (block_e[:, None] == experts[None]).astype(jnp.int32)
    base = jnp.sum(oh_b * (starts - pstarts)[None], axis=1) + blk_start
    end_b = jnp.sum(oh_b * ends[None], axis=1)
    src = base[:, None] + jnp.arange(T, dtype=jnp.int32)[None]
    filler = (blk_start[:, None] + jnp.arange(T, dtype=jnp.int32)[None]) % N
    row_tok = jnp.where(src < end_b[:, None], tok_sorted[jnp.clip(src, 0, NK - 1)], filler).reshape(n_rows)
    inv = jnp.argsort(order).astype(jnp.int32)
    pos = (inv + jnp.sum(onehot * (pstarts - starts)[None], axis=1)).reshape(TOP_K, N)
    return row_tok, block_e.astype(jnp.int32), pos


def _final_body(x1_ref, y0_ref, y1_ref, rt_ref, p_ref, gp_ref, wg_ref, bg_ref, wp_ref, gf_ref, o_ref):
    w0 = rt_ref[:, 2:3]
    w1 = rt_ref[:, 3:4]
    x2 = x1_ref[...] + (_unpack_bf16_halves(y0_ref[...]) * w0 + _unpack_bf16_halves(y1_ref[...]) * w1)
    hp = x2 * lax.rsqrt(jnp.mean(x2 * x2, axis=-1, keepdims=True) + EPS) * gp_ref[...]
    gate = jax.nn.sigmoid(jnp.dot(hp.astype(BF16), wg_ref[...], preferred_element_type=F32) + bg_ref[...])
    pe = jnp.dot(p_ref[...].astype(BF16), wp_ref[...], preferred_element_type=F32)
    x3 = x2 + pe * gate
    o_ref[...] = x3 * lax.rsqrt(jnp.mean(x3 * x3, axis=-1, keepdims=True) + EPS) * gf_ref[...]


def _final(x1, y0, y1, route, p, gp, wg, bg, wp, gf):
    N, D = x1.shape
    tm = TM_PROJ
    row = lambda w: pl.BlockSpec((tm, w), lambda i: (i, 0))
    full = lambda a: pl.BlockSpec(a.shape, lambda i: (0,) * a.ndim)
    return pl.pallas_call(
        _final_body,
        grid=(N // tm,),
        in_specs=[row(D), row(D // 2), row(D // 2), row(LANES), row(p.shape[1]),
                  full(gp), full(wg), full(bg), full(wp), full(gf)],
        out_specs=row(D),
        out_shape=jax.ShapeDtypeStruct((N, D), F32),
        compiler_params=_cparams(("parallel",)),
        name="ple_final",
    )(x1, y0, y1, route, p, gp, wg, bg, wp, gf)


def kernel(x, p, g_mix, w_in, q_gain, k_gain, conv_w, conv_b, w_f1, b_f1, freq1, w_f2, b_f2, freq2, w_f3, filt_bias, g_attn_out, g_hyena_out, w_out, g_moe, w_group, b_group, w_router, b_router, w_gate, w_up, w_down, g_ple, w_ple_gate, b_ple_gate, w_ple, g_final):
    B, S, D = x.shape
    N = B * S
    assert p.shape[0] == 1 and S == (FFT_N1 // 2) * FFT_N2 and B % 2 == 0
    i = 0
    cst = _dft_constants()
    cos, sin = _rope_tables(S)
    bd = _block_diag_ones(D_ATTN, HEAD_DIM)

    n_qkv = D_ATTN + 2 * D_KV
    wqkv = w_in[i][:, :n_qkv].astype(BF16)
    wut = w_in[i][:, n_qkv:].T.astype(BF16)
    q, kw, vw, ut = _inproj(x, g_mix[i][None], wqkv, wut, bd,
                            jnp.tile(q_gain[i], N_HEADS)[None], jnp.tile(k_gain[i], N_KV_HEADS)[None], cos, sin)

    ya = _attention(q, kw, vw)

    circ = _filter_time_domain(S, w_f1[i], b_f1[i], freq1[i], w_f2[i], b_f2[i], freq2[i], w_f3[i])
    hspec = _filter_fft(circ.reshape(2 * D_HYENA, FFT_N1, FFT_N2), cst)
    hspec = hspec.reshape(2, D_HYENA, FFT_N1, 2 * LANES)
    du = ut.shape[1]
    u4 = ut.reshape(B, du, S // LANES, LANES)
    par_u = jnp.broadcast_to(jnp.concatenate([conv_w[i], conv_b[i][None]], 0)[:, :, None], (4, du, LANES))
    fb = jnp.broadcast_to(filt_bias[i][:, :, None], (2, D_HYENA, LANES))
    yh4 = _hyena(u4, par_u, fb, hspec, cst)
    yht = yh4.reshape(B, D_HYENA, S)

    wrt = jnp.zeros((D, LANES), F32).at[:, :N_GROUPS].set(w_group[i]).at[:, N_GROUPS:N_GROUPS + N_EXPERTS].set(w_router[i])
    brt = jnp.zeros((1, LANES), F32).at[0, :N_GROUPS].set(b_group[i]).at[0, N_GROUPS:N_GROUPS + N_EXPERTS].set(b_router[i])
    wrh = wrt.astype(BF16)
    wrl = (wrt - wrh.astype(F32)).astype(BF16)
    x1, h2, route, route_t = _outproj(ya, yht, x, g_attn_out[i][None], g_hyena_out[i][:, None],
                                      w_out[i].astype(BF16), bd, g_moe[i][None], wrh, wrl, brt)

    e_flat = jnp.transpose(route_t[:, :TOP_K], (1, 0, 2)).reshape(TOP_K * N).astype(jnp.int32)
    row_tok, block_e, pos = _dispatch(e_flat, N)
    xs = h2.reshape(N, D // 2)[row_tok]
    yb = _moe_experts(block_e, xs, w_gate[i], w_up[i], w_down[i])
    y0 = yb[pos[0]]
    y1 = yb[pos[1]]

    out = _final(x1.reshape(N, D), y0, y1, route.reshape(N, LANES), p[i].reshape(N, -1), g_ple[i][None],
                 w_ple_gate[i].astype(BF16), b_ple_gate[i][None], w_ple[i].astype(BF16), g_final[None])
    return out.reshape(B, S, D)
```

```python
import functools
import math

import numpy as np
import jax
import jax.numpy as jnp
from jax import lax
from jax.experimental import pallas as pl
from jax.experimental.pallas import tpu as pltpu

F32 = jnp.float32
BF16 = jnp.bfloat16

D_MODEL = 1024
EPS = 1e-6
GRID_W = 64
N_HEADS = 8
N_KV_HEADS = 2
HEAD_DIM = 64
D_ATTN = N_HEADS * HEAD_DIM
D_KV = N_KV_HEADS * HEAD_DIM
ROPE_THETA = 10000.0
D_HYENA = 512
HYENA_HEAD = 64
FILTER_EMB = 33
FAST_DECAY_PCT = 0.3
SLOW_DECAY_PCT = 1.5
DECAY_TARGET = 1e-2
N_GROUPS = 4
EXPERTS_PER_GROUP = 8
N_EXPERTS = N_GROUPS * EXPERTS_PER_GROUP
TOP_K = 2
D_EXPERT = 512

LANES = 128
FFT_N1 = 64
FFT_N2 = 128
VMEM_LIMIT = 56 * 1024 * 1024

TM_PROJ = 512
TQ_ATTN = 256
C_HY = 32
SEQ_UNROLL = 8
TB_MOE = 256


def _cparams(sem):
    return pltpu.CompilerParams(dimension_semantics=sem, vmem_limit_bytes=VMEM_LIMIT)


def _rope_tables(S):
    half = HEAD_DIM // 2
    t = jnp.arange(S, dtype=F32)
    r_idx = jnp.floor(t / GRID_W)
    c_idx = t - r_idx * GRID_W
    inv = ROPE_THETA ** (-jnp.arange(0, half, 2, dtype=F32) / half)
    ang_r = r_idx[:, None] * inv[None]
    ang_c = c_idx[:, None] * inv[None]
    cos_h = jnp.concatenate([jnp.cos(ang_r), jnp.cos(ang_r), jnp.cos(ang_c), jnp.cos(ang_c)], axis=-1)
    sin_h = jnp.concatenate([-jnp.sin(ang_r), jnp.sin(ang_r), -jnp.sin(ang_c), jnp.sin(ang_c)], axis=-1)
    return jnp.tile(cos_h, (1, 2)), jnp.tile(sin_h, (1, 2))


def _dft_constants():
    n1, n2 = FFT_N1, FFT_N2
    n = n1 * n2
    a = np.arange(n1)
    ang = 2.0 * np.pi * np.outer(a, a) / n1
    far, fai = np.cos(ang), -np.sin(ang)
    hlf = n1 // 2
    ma = np.block([[far[:, :hlf], -fai[:, :hlf]], [fai[:, :hlf], far[:, :hlf]]])
    maf = np.concatenate([far, fai], axis=0)
    b = np.arange(n2)
    angt = 2.0 * np.pi * np.outer(a, b) / n
    tw = np.concatenate([np.cos(angt), -np.sin(angt)], axis=1)
    angb = 2.0 * np.pi * np.outer(b, b) / n2
    fbr, fbi = np.cos(angb), -np.sin(angb)
    g = np.block([[fbr, fbi], [-fbi, fbr]])
    ginv = np.block([[fbr, -fbi], [fbi, fbr]])
    minv_r = np.concatenate([far[:hlf], -fai[:hlf]], axis=0) / n
    minv_i = np.concatenate([fai[:hlf], far[:hlf]], axis=0) / n
    f = lambda m: jnp.asarray(m.astype(np.float32))
    return dict(ma=f(ma), maf=f(maf), tw=f(tw), g=f(g), ginv=f(ginv), minv_r=f(minv_r), minv_i=f(minv_i))


def _block_diag_ones(width, group):
    i = np.arange(width) // group
    return jnp.asarray((i[:, None] == i[None, :]).astype(np.float32)).astype(BF16)


def _group_sumsq(a, bd):
    sq = a * a
    hi = sq.astype(BF16)
    lo = (sq - hi.astype(F32)).astype(BF16)
    return (jnp.dot(hi, bd, preferred_element_type=F32) + jnp.dot(lo, bd, preferred_element_type=F32))


def _head_norm_rope(a, gain, bd, cos, sin):
    width = a.shape[-1]
    n = a * lax.rsqrt(_group_sumsq(a, bd) * (1.0 / HEAD_DIM) + EPS) * gain
    rep = width // LANES
    if rep > 1:
        cos = jnp.concatenate([cos] * rep, axis=-1)
        sin = jnp.concatenate([sin] * rep, axis=-1)
    fwd = pltpu.roll(n, width - 16, 1)
    bwd = pltpu.roll(n, 16, 1)
    lane = lax.broadcasted_iota(jnp.int32, n.shape, 1)
    sw = jnp.where((lane % 32) < 16, fwd, bwd)
    return n * cos + sw * sin


def _inproj_body(x_ref, g_ref, wqkv_ref, wu_ref, bd_ref, qg_ref, kg_ref, cos_ref, sin_ref,
                 q_ref, kw_ref, vw_ref, ut_ref):
    x = x_ref[...]
    h = x * lax.rsqrt(jnp.mean(x * x, axis=-1, keepdims=True) + EPS) * g_ref[...]
    hb = h.astype(BF16)
    qkv = jnp.dot(hb, wqkv_ref[...], preferred_element_type=F32)
    cos = cos_ref[...]
    sin = sin_ref[...]
    bd = bd_ref[...]
    q = _head_norm_rope(qkv[:, :D_ATTN], qg_ref[...], bd, cos, sin)
    q_ref[...] = (q * (HEAD_DIM ** -0.5 * math.log2(math.e))).astype(BF16)
    k = _head_norm_rope(qkv[:, D_ATTN:D_ATTN + D_KV], kg_ref[...], bd[:D_KV, :D_KV], cos, sin)
    kt = k.T.astype(BF16)
    zero = jnp.zeros((HEAD_DIM, kt.shape[1]), BF16)
    for h in range(N_KV_HEADS):
        kh = kt[h * HEAD_DIM:(h + 1) * HEAD_DIM]
        kw_ref[h, 0, :HEAD_DIM] = kh
        kw_ref[h, 0, HEAD_DIM:] = zero
        kw_ref[h, 1, :HEAD_DIM] = zero
        kw_ref[h, 1, HEAD_DIM:] = kh
    v = qkv[:, D_ATTN + D_KV:]
    vr = pltpu.roll(v, HEAD_DIM, 1)
    first = lax.broadcasted_iota(jnp.int32, v.shape, 1) < HEAD_DIM
    vw_ref[0, 0] = jnp.where(first, v, 1.0).astype(BF16)
    vw_ref[0, 1] = jnp.where(first, 1.0, vr).astype(BF16)
    vw_ref[1, 0] = jnp.where(first, vr, 1.0).astype(BF16)
    vw_ref[1, 1] = jnp.where(first, 1.0, v).astype(BF16)
    ut_ref[...] = lax.dot_general(wu_ref[...], hb, (((1,), (1,)), ((), ())),
                                  preferred_element_type=F32)


def _inproj(x, g_mix, wqkv, wut, bd, qg, kg, cos, sin):
    B, S, D = x.shape
    tm = TM_PROJ
    du = wut.shape[0]
    full = lambda shape: pl.BlockSpec(shape, lambda b, i: (0,) * len(shape))
    return pl.pallas_call(
        _inproj_body,
        grid=(B, S // tm),
        in_specs=[
            pl.BlockSpec((None, tm, D), lambda b, i: (b, i, 0)),
            full((1, D)), full(wqkv.shape), full(wut.shape), full(bd.shape),
            full((1, D_ATTN)), full((1, D_KV)),
            pl.BlockSpec((tm, LANES), lambda b, i: (i, 0)),
            pl.BlockSpec((tm, LANES), lambda b, i: (i, 0)),
        ],
        out_specs=[
            pl.BlockSpec((None, tm, D_ATTN), lambda b, i: (b, i, 0)),
            pl.BlockSpec((None, N_KV_HEADS, 2, LANES, tm), lambda b, i: (b, 0, 0, 0, i)),
            pl.BlockSpec((None, N_KV_HEADS, 2, tm, LANES), lambda b, i: (b, 0, 0, i, 0)),
            pl.BlockSpec((None, du, tm), lambda b, i: (b, 0, i)),
        ],
        out_shape=[
            jax.ShapeDtypeStruct((B, S, D_ATTN), BF16),
            jax.ShapeDtypeStruct((B, N_KV_HEADS, 2, LANES, S), BF16),
            jax.ShapeDtypeStruct((B, N_KV_HEADS, 2, S, LANES), BF16),
            jax.ShapeDtypeStruct((B, du, S), F32),
        ],
        compiler_params=_cparams(("parallel", "parallel")),
        name="inproj",
    )(x, g_mix, wqkv, wut, bd, qg, kg, cos, sin)


def _attn_body(q_ref, kw_ref, vw_ref, o_ref):

    def one_head(q, kw, vw):
        s = jnp.dot(q, kw, preferred_element_type=F32)
        m = jnp.max(s, axis=-1, keepdims=True)
        p = jnp.exp2(s - m).astype(BF16)
        return jnp.dot(p, vw, preferred_element_type=F32)

    for pair in range(D_ATTN // LANES):
        h = pair // (N_HEADS // N_KV_HEADS // 2)
        q = q_ref[:, pair * LANES:(pair + 1) * LANES]
        oe = one_head(q, kw_ref[h, 0], vw_ref[h, 0])
        oo = one_head(q, kw_ref[h, 1], vw_ref[h, 1])
        first = lax.broadcasted_iota(jnp.int32, oe.shape, 1) < HEAD_DIM
        num = jnp.where(first, oe, oo)
        den = jnp.where(first, pltpu.roll(oe, HEAD_DIM, 1), pltpu.roll(oo, HEAD_DIM, 1))
        o_ref[:, pair * LANES:(pair + 1) * LANES] = num / den


def _attention(q, kw, vw):
    B, S, _ = q.shape
    tq = TQ_ATTN
    return pl.pallas_call(
        _attn_body,
        grid=(B, S // tq),
        in_specs=[
            pl.BlockSpec((None, tq, D_ATTN), lambda b, i: (b, i, 0)),
            pl.BlockSpec((None, N_KV_HEADS, 2, LANES, S), lambda b, i: (b, 0, 0, 0, 0)),
            pl.BlockSpec((None, N_KV_HEADS, 2, S, LANES), lambda b, i: (b, 0, 0, 0, 0)),
        ],
        out_specs=pl.BlockSpec((None, tq, D_ATTN), lambda b, i: (b, i, 0)),
        out_shape=jax.ShapeDtypeStruct((B, S, D_ATTN), F32),
        compiler_params=_cparams(("parallel", "arbitrary")),
        name="attention",
    )(q, kw, vw)


def _fwd_twiddle_store(y, tw_ref, s1_ref, row0):
    yr, yi = y[:FFT_N1], y[FFT_N1:]
    twr, twi = tw_ref[:, :LANES], tw_ref[:, LANES:]
    s1_ref[pl.ds(row0, FFT_N1), :LANES] = (yr * twr - yi * twi).astype(BF16)
    s1_ref[pl.ds(row0, FFT_N1), LANES:] = (yr * twi + yi * twr).astype(BF16)


def _filtfft_body(x_ref, maf_ref, tw_ref, g_ref, h_ref, s1_ref):
    C = x_ref.shape[0]

    def step_a(c, carry):
        y = jnp.dot(maf_ref[...], x_ref[c].astype(BF16), preferred_element_type=F32)
        _fwd_twiddle_store(y, tw_ref, s1_ref, pl.multiple_of(c * FFT_N1, FFT_N1))
        return carry

    lax.fori_loop(0, C, step_a, 0, unroll=SEQ_UNROLL)
    z = jnp.dot(s1_ref[...], g_ref[...], preferred_element_type=F32)
    h_ref[...] = z.reshape(C, FFT_N1, 2 * LANES)


def _filter_fft(circ, cst):
    n_seq = circ.shape[0]
    C = C_HY
    full = lambda a: pl.BlockSpec(a.shape, lambda i: (0,) * a.ndim)
    maf, tw, g = cst["maf"].astype(BF16), cst["tw"], cst["g"].astype(BF16)
    return pl.pallas_call(
        _filtfft_body,
        grid=(n_seq // C,),
        in_specs=[pl.BlockSpec((C, FFT_N1, FFT_N2), lambda i: (i, 0, 0)), full(maf), full(tw), full(g)],
        out_specs=pl.BlockSpec((C, FFT_N1, 2 * LANES), lambda i: (i, 0, 0)),
        out_shape=jax.ShapeDtypeStruct((n_seq, FFT_N1, 2 * LANES), F32),
        scratch_shapes=[pltpu.VMEM((C * FFT_N1, 2 * LANES), BF16)],
        compiler_params=_cparams(("parallel",)),
        name="filter_fft",
    )(circ, maf, tw, g)


def _short_conv(x, par_ref, c):
    rows, lanes = x.shape
    a_i = lax.broadcasted_iota(jnp.int32, x.shape, 0)
    b_i = lax.broadcasted_iota(jnp.int32, x.shape, 1)
    l1 = pltpu.roll(x, 1, 1)
    l2 = pltpu.roll(l1, 1, 0)
    prev = jnp.where(b_i == 0, l2, l1)
    prev = jnp.where((a_i == 0) & (b_i == 0), 0.0, prev)
    r1 = pltpu.roll(x, lanes - 1, 1)
    r2 = pltpu.roll(r1, rows - 1, 0)
    nxt = jnp.where(b_i == lanes - 1, r2, r1)
    nxt = jnp.where((a_i == rows - 1) & (b_i == lanes - 1), 0.0, nxt)
    w0 = par_ref[0, pl.ds(c, 1), :]
    w1 = par_ref[1, pl.ds(c, 1), :]
    w2 = par_ref[2, pl.ds(c, 1), :]
    cb = par_ref[3, pl.ds(c, 1), :]
    return cb + prev * w0 + x * w1 + nxt * w2


def _hyena_body(v_ref, x1_ref, x2_ref, pv_ref, p1_ref, p2_ref, fb_ref, h_ref,
                ma_ref, tw_ref, g_ref, ginv_ref, mir_ref, mii_ref,
                o_ref, s1_ref, s2_ref, vc_ref, z1_ref):
    C = v_ref.shape[1]
    half = FFT_N1 // 2

    def spectral(order):
        z = jnp.dot(s1_ref[...], g_ref[...], preferred_element_type=F32)
        hs = h_ref[order].reshape(C * FFT_N1, 2 * LANES)
        zr, zi = z[:, :LANES], z[:, LANES:]
        hr, hi = hs[:, :LANES], hs[:, LANES:]
        pb = jnp.concatenate([zr * hr - zi * hi, zr * hi + zi * hr], axis=1).astype(BF16)
        s2_ref[...] = jnp.dot(pb, ginv_ref[...], preferred_element_type=F32)

    def inv_a(c):
        row0 = pl.multiple_of(c * FFT_N1, FFT_N1)
        y = s2_ref[pl.ds(row0, FFT_N1), :]
        yr, yi = y[:, :LANES], y[:, LANES:]
        twr, twi = tw_ref[:, :LANES], tw_ref[:, LANES:]
        ur = (yr * twr + yi * twi).astype(BF16)
        ui = (yi * twr - yr * twi).astype(BF16)
        out = (jnp.dot(mir_ref[...], ur, preferred_element_type=F32)
               + jnp.dot(mii_ref[...], ui, preferred_element_type=F32))
        return out[:half], out[half:]

    def fwd_a(c, xr, xi):
        xs = jnp.concatenate([xr, xi], axis=0).astype(BF16)
        y = jnp.dot(ma_ref[...], xs, preferred_element_type=F32)
        _fwd_twiddle_store(y, tw_ref, s1_ref, pl.multiple_of(c * FFT_N1, FFT_N1))

    def pass1_a(c, carry):
        vr = _short_conv(v_ref[0, c], pv_ref, c)
        vi = _short_conv(v_ref[1, c], pv_ref, c)
        vc_ref[0, c] = vr
        vc_ref[1, c] = vi
        fwd_a(c, vr, vi)
        return carry

    def pass1_b(c, carry):
        cr, ci = inv_a(c)
        bias = fb_ref[0, pl.ds(c, 1), :]
        zr = _short_conv(x1_ref[0, c], p1_ref, c) * (cr + bias * vc_ref[0, c])
        zi = _short_conv(x1_ref[1, c], p1_ref, c) * (ci + bias * vc_ref[1, c])
        z1_ref[0, c] = zr
        z1_ref[1, c] = zi
        fwd_a(c, zr, zi)
        return carry

    def pass2_b(c, carry):
        cr, ci = inv_a(c)
        bias = fb_ref[1, pl.ds(c, 1), :]
        o_ref[0, c] = _short_conv(x2_ref[0, c], p2_ref, c) * (cr + bias * z1_ref[0, c])
        o_ref[1, c] = _short_conv(x2_ref[1, c], p2_ref, c) * (ci + bias * z1_ref[1, c])
        return carry

    lax.fori_loop(0, C, pass1_a, 0, unroll=SEQ_UNROLL)
    spectral(0)
    lax.fori_loop(0, C, pass1_b, 0, unroll=SEQ_UNROLL)
    spectral(1)
    lax.fori_loop(0, C, pass2_b, 0, unroll=SEQ_UNROLL)


def _hyena(u4, par_u, fb, hspec, cst):
    B = u4.shape[0]
    C = C_HY
    J = D_HYENA // C
    rows = u4.shape[2]
    full = lambda a: pl.BlockSpec(a.shape, lambda j, p: (0,) * a.ndim)
    ma, g, ginv = cst["ma"].astype(BF16), cst["g"].astype(BF16), cst["ginv"].astype(BF16)
    mir, mii = cst["minv_r"].astype(BF16), cst["minv_i"].astype(BF16)
    tw = cst["tw"]
    u_spec = lambda k: pl.BlockSpec((2, C, rows, LANES), lambda j, p, k=k: (p, j + k * J, 0, 0))
    par_spec = lambda k: pl.BlockSpec((4, C, LANES), lambda j, p, k=k: (0, j + k * J, 0))
    return pl.pallas_call(
        _hyena_body,
        grid=(J, B // 2),
        in_specs=[
            u_spec(0), u_spec(1), u_spec(2), par_spec(0), par_spec(1), par_spec(2),
            pl.BlockSpec((2, C, LANES), lambda j, p: (0, j, 0)),
            pl.BlockSpec((2, C, FFT_N1, 2 * LANES), lambda j, p: (0, j, 0, 0)),
            full(ma), full(tw), full(g), full(ginv), full(mir), full(mii),
        ],
        out_specs=pl.BlockSpec((2, C, rows, LANES), lambda j, p: (p, j, 0, 0)),
        out_shape=jax.ShapeDtypeStruct((B, D_HYENA, rows, LANES), F32),
        scratch_shapes=[
            pltpu.VMEM((C * FFT_N1, 2 * LANES), BF16),
            pltpu.VMEM((C * FFT_N1, 2 * LANES), F32),
            pltpu.VMEM((2, C, rows, LANES), F32),
            pltpu.VMEM((2, C, rows, LANES), F32),
        ],
        compiler_params=_cparams(("parallel", "arbitrary")),
        name="hyena",
    )(u4, u4, u4, par_u, par_u, par_u, fb, hspec, ma, tw, g, ginv, mir, mii)


def _filter_time_domain(L, w_f1, b_f1, freq1, w_f2, b_f2, freq2, w_f3):
    hp = lax.Precision.HIGHEST
    bands = (FILTER_EMB - 1) // 2
    t = jnp.linspace(0.0, 1.0, L, dtype=F32)[:, None]
    w = (2.0 * math.pi / L) * jnp.arange(L, dtype=F32)[:, None]
    f = jnp.linspace(1e-4, bands - 1, bands, dtype=F32)[None]
    zf = f * w
    z = jnp.concatenate([t, jnp.cos(zf), -jnp.sin(zf)], axis=-1)
    back = lambda a: jnp.roll(a[::-1], 1, axis=0)
    zz = jnp.concatenate([z, back(z)], axis=0)
    h = jnp.sin(freq1 * (jnp.dot(zz, w_f1, precision=hp) + b_f1))
    h = jnp.sin(freq2 * (jnp.dot(h, w_f2, precision=hp) + b_f2))
    w3 = w_f3.reshape(-1, 2, 2, D_HYENA)
    hf = jnp.einsum("hoc,lh->ocl", w3[:, :, 0], h[:L], precision=hp)
    hb = jnp.einsum("hoc,lh->ocl", w3[:, :, 1], h[L:], precision=hp)
    max_decay = math.log(DECAY_TARGET) / FAST_DECAY_PCT
    min_decay = math.log(DECAY_TARGET) / SLOW_DECAY_PCT
    deltas = jnp.linspace(min_decay, max_decay, D_HYENA, dtype=F32)
    decay_f = jnp.exp(-jnp.abs(deltas)[:, None] * t[None, :, 0])
    decay_b = jnp.exp(-jnp.abs(deltas)[:, None] * back(t)[None, :, 0])
    hf = hf * decay_f[None]
    hb = hb * decay_b[None]
    hf = hf / (jnp.sum(jnp.abs(hf), axis=-1, keepdims=True) + EPS)
    hb = hb / (jnp.sum(jnp.abs(hb), axis=-1, keepdims=True) + EPS)
    first = lax.broadcasted_iota(jnp.int32, hb.shape, 2) == 0
    return jnp.concatenate([hf + jnp.where(first, hb, 0.0), jnp.where(first, 0.0, hb)], axis=-1)


def _route_lanes(lg):
    neg = -1e30
    lane = lax.broadcasted_iota(jnp.int32, lg.shape, 1)
    gmask = lane < N_GROUPS
    gl = jnp.where(gmask, lg, neg)
    gm = jnp.max(gl, axis=-1, keepdims=True)
    gsum = jnp.sum(jnp.where(gmask, jnp.exp(gl - gm), 0.0), axis=-1, keepdims=True)
    g_top = 1.0 / gsum
    g_sel = jnp.min(jnp.where(gl == gm, lane, LANES), axis=-1, keepdims=True)
    lo = N_GROUPS + EXPERTS_PER_GROUP * g_sel
    el = jnp.where((lane >= lo) & (lane < lo + EXPERTS_PER_GROUP), lg, neg)
    m1 = jnp.max(el, axis=-1, keepdims=True)
    i1 = jnp.min(jnp.where(el == m1, lane, LANES), axis=-1, keepdims=True)
    el2 = jnp.where(lane == i1, neg, el)
    m2 = jnp.max(el2, axis=-1, keepdims=True)
    i2 = jnp.min(jnp.where(el2 == m2, lane, LANES), axis=-1, keepdims=True)
    d = jnp.exp(m2 - m1)
    p1 = 1.0 / (1.0 + d)
    p2 = d / (1.0 + d)
    e1 = (i1 - N_GROUPS).astype(F32)
    e2 = (i2 - N_GROUPS).astype(F32)
    return jnp.where(lane == 0, e1, jnp.where(lane == 1, e2, jnp.where(lane == 2, g_top * p1,
                     jnp.where(lane == 3, g_top * p2, 0.0))))


def _pack_bf16_halves(a):
    w = a.shape[1] // 2
    bits = pltpu.bitcast(a.astype(BF16).astype(F32), jnp.uint32)
    return (bits[:, :w] >> 16) | (bits[:, w:] & jnp.uint32(0xFFFF0000))


def _unpack_bf16_halves(wd):
    lo = pltpu.bitcast(wd << 16, F32)
    hi = pltpu.bitcast(wd & jnp.uint32(0xFFFF0000), F32)
    return jnp.concatenate([lo, hi], axis=1)


def _outproj_body(ya_ref, yh_ref, x_ref, ga_ref, gh_ref, wo_ref, bd_ref, gm_ref, wrh_ref, wrl_ref, brt_ref,
                  x1_ref, h2_ref, rt_ref, rtt_ref):
    ya = ya_ref[...]
    yan = ya * lax.rsqrt(_group_sumsq(ya, bd_ref[...]) * (1.0 / HEAD_DIM) + EPS) * ga_ref[...]
    yh = yh_ref[...]
    tm = yh.shape[1]
    yh3 = yh.reshape(D_HYENA // HYENA_HEAD, HYENA_HEAD, tm)
    ms = jnp.mean(yh3 * yh3, axis=1, keepdims=True)
    yhn = (yh3 * lax.rsqrt(ms + EPS)).reshape(D_HYENA, tm) * gh_ref[...]
    mix = (jnp.dot(yan.astype(BF16), wo_ref[:D_ATTN, :], preferred_element_type=F32)
           + jnp.dot(yhn.T.astype(BF16), wo_ref[D_ATTN:, :], preferred_element_type=F32))
    x1 = x_ref[...] + mix
    x1_ref[...] = x1
    h2 = x1 * lax.rsqrt(jnp.mean(x1 * x1, axis=-1, keepdims=True) + EPS) * gm_ref[...]
    h2_ref[...] = _pack_bf16_halves(h2)
    hi = h2.astype(BF16)
    lo = (h2 - hi.astype(F32)).astype(BF16)
    lg = (jnp.dot(hi, wrh_ref[...], preferred_element_type=F32)
          + jnp.dot(lo, wrh_ref[...], preferred_element_type=F32)
          + jnp.dot(hi, wrl_ref[...], preferred_element_type=F32)) + brt_ref[...]
    route = _route_lanes(lg)
    rt_ref[...] = route
    rtt_ref[...] = route.T[:8]


def _outproj(ya, yht, x, ga, gh, wo, bd, gm, wrh, wrl, brt):
    B, S, D = x.shape
    tm = TM_PROJ
    full = lambda a: pl.BlockSpec(a.shape, lambda b, i: (0,) * a.ndim)
    return pl.pallas_call(
        _outproj_body,
        grid=(B, S // tm),
        in_specs=[
            pl.BlockSpec((None, tm, D_ATTN), lambda b, i: (b, i, 0)),
            pl.BlockSpec((None, D_HYENA, tm), lambda b, i: (b, 0, i)),
            pl.BlockSpec((None, tm, D), lambda b, i: (b, i, 0)),
            full(ga), full(gh), full(wo), full(bd), full(gm), full(wrh), full(wrl), full(brt),
        ],
        out_specs=[
            pl.BlockSpec((None, tm, D), lambda b, i: (b, i, 0)),
            pl.BlockSpec((None, tm, D // 2), lambda b, i: (b, i, 0)),
            pl.BlockSpec((None, tm, LANES), lambda b, i: (b, i, 0)),
            pl.BlockSpec((None, 8, tm), lambda b, i: (b, 0, i)),
        ],
        out_shape=[
            jax.ShapeDtypeStruct((B, S, D), F32),
            jax.ShapeDtypeStruct((B, S, D // 2), jnp.uint32),
            jax.ShapeDtypeStruct((B, S, LANES), F32),
            jax.ShapeDtypeStruct((B, 8, S), F32),
        ],
        compiler_params=_cparams(("parallel", "parallel")),
        name="outproj",
    )(ya, yht, x, ga, gh, wo, bd, gm, wrh, wrl, brt)


def _moe_body(be_ref, ra_ref, h2_hbm, wg_ref, wu_ref, wd_ref, y_hbm,
              wg_s, wu_s, wd_s, xbuf, ybuf, sem_in, sem_out, *, n_tok):
    i = pl.program_id(0)
    nb = pl.num_programs(0)
    T = xbuf.shape[1]
    slot = i % 2

    def gather_copy(blk, sl, r):
        tok = ra_ref[blk * T + r] & (n_tok - 1)
        return pltpu.make_async_copy(h2_hbm.at[pl.ds(tok, 1)], xbuf.at[sl, pl.ds(r, 1)], sem_in.at[sl])

    def scatter_copy(blk, sl, r):
        dst = ra_ref[blk * T + r]
        return pltpu.make_async_copy(ybuf.at[sl, pl.ds(r, 1)], y_hbm.at[pl.ds(dst, 1)], sem_out.at[sl])

    def block_in_wait(sl):
        pltpu.make_async_copy(h2_hbm.at[pl.ds(0, T)], xbuf.at[sl], sem_in.at[sl]).wait()

    def block_out_wait(sl):
        pltpu.make_async_copy(ybuf.at[sl], y_hbm.at[pl.ds(0, T)], sem_out.at[sl]).wait()

    @pl.when(i == 0)
    def _():
        for r in range(T):
            gather_copy(0, 0, r).start()

    @pl.when(i + 1 < nb)
    def _():
        for r in range(T):
            gather_copy(i + 1, 1 - slot, r).start()

    prev = be_ref[jnp.maximum(i - 1, 0)]

    @pl.when((i == 0) | (be_ref[i] != prev))
    def _():
        wg_s[...] = wg_ref[...].astype(BF16)
        wu_s[...] = wu_ref[...].astype(BF16)
        wd_s[...] = wd_ref[...].astype(BF16)

    block_in_wait(slot)
    x = _unpack_bf16_halves(xbuf[slot]).astype(BF16)
    a = jnp.dot(x, wg_s[...], preferred_element_type=F32)
    b = jnp.dot(x, wu_s[...], preferred_element_type=F32)
    hmid = (a * jax.nn.sigmoid(a)) * b
    y = _pack_bf16_halves(jnp.dot(hmid.astype(BF16), wd_s[...], preferred_element_type=F32))

    @pl.when(i >= 2)
    def _():
        block_out_wait(slot)

    ybuf[slot] = y
    for r in range(T):
        scatter_copy(i, slot, r).start()

    @pl.when(i == nb - 1)
    def _():
        block_out_wait(slot)

        @pl.when(nb >= 2)
        def _():
            block_out_wait(1 - slot)


def _moe_experts(block_e, row_a, h2p, w_gate, w_up, w_down, n_out_rows):
    n_tok, W = h2p.shape
    D = w_gate.shape[1]
    T = TB_MOE
    assert n_tok & (n_tok - 1) == 0
    grid_spec = pltpu.PrefetchScalarGridSpec(
        num_scalar_prefetch=2,
        grid=(row_a.shape[0] // T,),
        in_specs=[
            pl.BlockSpec(memory_space=pl.ANY),
            pl.BlockSpec((None, D, D_EXPERT), lambda i, be, ra: (be[i], 0, 0)),
            pl.BlockSpec((None, D, D_EXPERT), lambda i, be, ra: (be[i], 0, 0)),
            pl.BlockSpec((None, D_EXPERT, D), lambda i, be, ra: (be[i], 0, 0)),
        ],
        out_specs=pl.BlockSpec(memory_space=pl.ANY),
        scratch_shapes=[
            pltpu.VMEM((D, D_EXPERT), BF16), pltpu.VMEM((D, D_EXPERT), BF16), pltpu.VMEM((D_EXPERT, D), BF16),
            pltpu.VMEM((2, T, W), jnp.uint32), pltpu.VMEM((2, T, W), jnp.uint32),
            pltpu.SemaphoreType.DMA((2,)), pltpu.SemaphoreType.DMA((2,)),
        ],
    )
    return pl.pallas_call(
        functools.partial(_moe_body, n_tok=n_tok),
        grid_spec=grid_spec,
        out_shape=jax.ShapeDtypeStruct((n_out_rows, W), jnp.uint32),
        compiler_params=_cparams(("arbitrary",)),
        name="moe_experts",
    )(block_e, row_a, h2p, w_gate, w_up, w_down)


def _dispatch(e_flat, N):
    T = TB_MOE
    NK = N * TOP_K
    experts = jnp.arange(N_EXPERTS, dtype=jnp.int32)
    order = jnp.argsort(e_flat).astype(jnp.int32)
    onehot = (e_flat[:, None] == experts[None]).astype(jnp.int32)
    counts = jnp.sum(onehot, axis=0)
    ends = jnp.cumsum(counts)
    starts = ends - counts
    padded = (counts + T - 1) // T * T
    pends = jnp.cumsum(padded)
    pstarts = pends - padded
    n_rows = -(-(NK + N_EXPERTS * (T - 1)) // T) * T
    n_blocks = n_rows // T
    blk_start = jnp.arange(n_blocks, dtype=jnp.int32) * T
    block_e = jnp.clip(jnp.sum((pends[None, :] <= blk_start[:, None]).astype(jnp.int32), axis=1),
                       0, N_EXPERTS - 1)
    oh_b = (block_e[:, None] == experts[None]).astype(jnp.int32)
    base = jnp.sum(oh_b * (starts - pstarts)[None], axis=1) + blk_start
    end_b = jnp.sum(oh_b * ends[None], axis=1)
    lane = jnp.arange(T, dtype=jnp.int32)[None]
    src = base[:, None] + lane
    pad_id = NK + blk_start[:, None] + lane - end_b[:, None]
    row_a = jnp.where(src < end_b[:, None], order[jnp.clip(src, 0, NK - 1)], pad_id)
    return block_e.astype(jnp.int32), row_a.reshape(n_rows).astype(jnp.int32)


def _final_body(x1_ref, y0_ref, y1_ref, rt_ref, p_ref, gp_ref, wg_ref, bg_ref, wp_ref, gf_ref, o_ref):
    w0 = rt_ref[:, 2:3]
    w1 = rt_ref[:, 3:4]
    x2 = x1_ref[...] + (_unpack_bf16_halves(y0_ref[...]) * w0 + _unpack_bf16_halves(y1_ref[...]) * w1)
    hp = x2 * lax.rsqrt(jnp.mean(x2 * x2, axis=-1, keepdims=True) + EPS) * gp_ref[...]
    gate = jax.nn.sigmoid(jnp.dot(hp.astype(BF16), wg_ref[...], preferred_element_type=F32) + bg_ref[...])
    pe = jnp.dot(p_ref[...].astype(BF16), wp_ref[...], preferred_element_type=F32)
    x3 = x2 + pe * gate
    o_ref[...] = x3 * lax.rsqrt(jnp.mean(x3 * x3, axis=-1, keepdims=True) + EPS) * gf_ref[...]


def _final(x1, y, route, p, gp, wg, bg, wp, gf):
    N, D = x1.shape
    tm = TM_PROJ
    row = lambda w: pl.BlockSpec((tm, w), lambda i: (i, 0))
    full = lambda a: pl.BlockSpec(a.shape, lambda i: (0,) * a.ndim)
    y0, y1 = y, y
    return pl.pallas_call(
        _final_body,
        grid=(N // tm,),
        in_specs=[row(D), row(D // 2), pl.BlockSpec((tm, D // 2), lambda i: (i + N // tm, 0)),
                  row(LANES), row(p.shape[1]),
                  full(gp), full(wg), full(bg), full(wp), full(gf)],
        out_specs=row(D),
        out_shape=jax.ShapeDtypeStruct((N, D), F32),
        compiler_params=_cparams(("parallel",)),
        name="ple_final",
    )(x1, y0, y1, route, p, gp, wg, bg, wp, gf)


def kernel(x, p, g_mix, w_in, q_gain, k_gain, conv_w, conv_b, w_f1, b_f1, freq1, w_f2, b_f2, freq2, w_f3, filt_bias, g_attn_out, g_hyena_out, w_out, g_moe, w_group, b_group, w_router, b_router, w_gate, w_up, w_down, g_ple, w_ple_gate, b_ple_gate, w_ple, g_final):
    B, S, D = x.shape
    N = B * S
    assert p.shape[0] == 1 and S == (FFT_N1 // 2) * FFT_N2 and B % 2 == 0
    i = 0
    cst = _dft_constants()
    cos, sin = _rope_tables(S)
    bd = _block_diag_ones(D_ATTN, HEAD_DIM)

    n_qkv = D_ATTN + 2 * D_KV
    wqkv = w_in[i][:, :n_qkv].astype(BF16)
    wut = w_in[i][:, n_qkv:].T.astype(BF16)
    q, kw, vw, ut = _inproj(x, g_mix[i][None], wqkv, wut, bd,
                            jnp.tile(q_gain[i], N_HEADS)[None], jnp.tile(k_gain[i], N_KV_HEADS)[None], cos, sin)

    ya = _attention(q, kw, vw)

    circ = _filter_time_domain(S, w_f1[i], b_f1[i], freq1[i], w_f2[i], b_f2[i], freq2[i], w_f3[i])
    hspec = _filter_fft(circ.reshape(2 * D_HYENA, FFT_N1, FFT_N2), cst)
    hspec = hspec.reshape(2, D_HYENA, FFT_N1, 2 * LANES)
    du = ut.shape[1]
    u4 = ut.reshape(B, du, S // LANES, LANES)
    par_u = jnp.broadcast_to(jnp.concatenate([conv_w[i], conv_b[i][None]], 0)[:, :, None], (4, du, LANES))
    fb = jnp.broadcast_to(filt_bias[i][:, :, None], (2, D_HYENA, LANES))
    yh4 = _hyena(u4, par_u, fb, hspec, cst)
    yht = yh4.reshape(B, D_HYENA, S)

    wrt = jnp.zeros((D, LANES), F32).at[:, :N_GROUPS].set(w_group[i]).at[:, N_GROUPS:N_GROUPS + N_EXPERTS].set(w_router[i])
    brt = jnp.zeros((1, LANES), F32).at[0, :N_GROUPS].set(b_group[i]).at[0, N_GROUPS:N_GROUPS + N_EXPERTS].set(b_router[i])
    wrh = wrt.astype(BF16)
    wrl = (wrt - wrh.astype(F32)).astype(BF16)
    x1, h2, route, route_t = _outproj(ya, yht, x, g_attn_out[i][None], g_hyena_out[i][:, None],
                                      w_out[i].astype(BF16), bd, g_moe[i][None], wrh, wrl, brt)

    e_flat = jnp.transpose(route_t[:, :TOP_K], (1, 0, 2)).reshape(TOP_K * N).astype(jnp.int32)
    block_e, row_a = _dispatch(e_flat, N)
    y = _moe_experts(block_e, row_a, h2.reshape(N, D // 2), w_gate[i], w_up[i], w_down[i], row_a.shape[0])

    out = _final(x1.reshape(N, D), y, route.reshape(N, LANES), p[i].reshape(N, -1), g_ple[i][None],
                 w_ple_gate[i].astype(BF16), b_ple_gate[i][None], w_ple[i].astype(BF16), g_final[None])
    return out.reshape(B, S, D)
```

```python
import functools
import math

import numpy as np
import jax
import jax.numpy as jnp
from jax import lax
from jax.experimental import pallas as pl
from jax.experimental.pallas import tpu as pltpu

F32 = jnp.float32
BF16 = jnp.bfloat16

D_MODEL = 1024
EPS = 1e-6
GRID_W = 64
N_HEADS = 8
N_KV_HEADS = 2
HEAD_DIM = 64
D_ATTN = N_HEADS * HEAD_DIM
D_KV = N_KV_HEADS * HEAD_DIM
ROPE_THETA = 10000.0
D_HYENA = 512
HYENA_HEAD = 64
FILTER_EMB = 33
FAST_DECAY_PCT = 0.3
SLOW_DECAY_PCT = 1.5
DECAY_TARGET = 1e-2
N_GROUPS = 4
EXPERTS_PER_GROUP = 8
N_EXPERTS = N_GROUPS * EXPERTS_PER_GROUP
TOP_K = 2
D_EXPERT = 512

LANES = 128
FFT_N1 = 64
FFT_N2 = 128
VMEM_LIMIT = 56 * 1024 * 1024

TM_PROJ = 512
TQ_ATTN = 256
C_HY = 32
ROW_CHUNKS = D_MODEL // 2 // LANES
SEQ_UNROLL = 8
TB_MOE = 256


def _cparams(sem):
    return pltpu.CompilerParams(dimension_semantics=sem, vmem_limit_bytes=VMEM_LIMIT)


def _rope_tables(S):
    half = HEAD_DIM // 2
    t = jnp.arange(S, dtype=F32)
    r_idx = jnp.floor(t / GRID_W)
    c_idx = t - r_idx * GRID_W
    inv = ROPE_THETA ** (-jnp.arange(0, half, 2, dtype=F32) / half)
    ang_r = r_idx[:, None] * inv[None]
    ang_c = c_idx[:, None] * inv[None]
    cos_h = jnp.concatenate([jnp.cos(ang_r), jnp.cos(ang_r), jnp.cos(ang_c), jnp.cos(ang_c)], axis=-1)
    sin_h = jnp.concatenate([-jnp.sin(ang_r), jnp.sin(ang_r), -jnp.sin(ang_c), jnp.sin(ang_c)], axis=-1)
    return jnp.tile(cos_h, (1, 2)), jnp.tile(sin_h, (1, 2))


def _dft_constants():
    n1, n2 = FFT_N1, FFT_N2
    n = n1 * n2
    a = np.arange(n1)
    ang = 2.0 * np.pi * np.outer(a, a) / n1
    far, fai = np.cos(ang), -np.sin(ang)
    hlf = n1 // 2
    ma = np.block([[far[:, :hlf], -fai[:, :hlf]], [fai[:, :hlf], far[:, :hlf]]])
    maf = np.concatenate([far, fai], axis=0)
    b = np.arange(n2)
    angt = 2.0 * np.pi * np.outer(a, b) / n
    tw = np.concatenate([np.cos(angt), -np.sin(angt)], axis=1)
    angb = 2.0 * np.pi * np.outer(b, b) / n2
    fbr, fbi = np.cos(angb), -np.sin(angb)
    g = np.block([[fbr, fbi], [-fbi, fbr]])
    ginv = np.block([[fbr, -fbi], [fbi, fbr]])
    minv_r = np.concatenate([far[:hlf], -fai[:hlf]], axis=0) / n
    minv_i = np.concatenate([fai[:hlf], far[:hlf]], axis=0) / n
    f = lambda m: jnp.asarray(m.astype(np.float32))
    return dict(ma=f(ma), maf=f(maf), tw=f(tw), g=f(g), ginv=f(ginv), minv_r=f(minv_r), minv_i=f(minv_i))


def _block_diag_ones(width, group):
    i = np.arange(width) // group
    return jnp.asarray((i[:, None] == i[None, :]).astype(np.float32)).astype(BF16)


def _group_sumsq(a, bd):
    sq = a * a
    hi = sq.astype(BF16)
    lo = (sq - hi.astype(F32)).astype(BF16)
    return (jnp.dot(hi, bd, preferred_element_type=F32) + jnp.dot(lo, bd, preferred_element_type=F32))


def _head_norm_rope(a, gain, bd, cos, sin):
    width = a.shape[-1]
    n = a * lax.rsqrt(_group_sumsq(a, bd) * (1.0 / HEAD_DIM) + EPS) * gain
    rep = width // LANES
    if rep > 1:
        cos = jnp.concatenate([cos] * rep, axis=-1)
        sin = jnp.concatenate([sin] * rep, axis=-1)
    fwd = pltpu.roll(n, width - 16, 1)
    bwd = pltpu.roll(n, 16, 1)
    lane = lax.broadcasted_iota(jnp.int32, n.shape, 1)
    sw = jnp.where((lane % 32) < 16, fwd, bwd)
    return n * cos + sw * sin


def _inproj_body(x_ref, g_ref, wqkv_ref, wu_ref, bd_ref, qg_ref, kg_ref, cos_ref, sin_ref,
                 q_ref, kw_ref, vw_ref, ut_ref):
    x = x_ref[...]
    h = x * lax.rsqrt(jnp.mean(x * x, axis=-1, keepdims=True) + EPS) * g_ref[...]
    hb = h.astype(BF16)
    qkv = jnp.dot(hb, wqkv_ref[...], preferred_element_type=F32)
    cos = cos_ref[...]
    sin = sin_ref[...]
    bd = bd_ref[...]
    q = _head_norm_rope(qkv[:, :D_ATTN], qg_ref[...], bd, cos, sin)
    q_ref[...] = (q * (HEAD_DIM ** -0.5 * math.log2(math.e))).astype(BF16)
    k = _head_norm_rope(qkv[:, D_ATTN:D_ATTN + D_KV], kg_ref[...], bd[:D_KV, :D_KV], cos, sin)
    kt = k.T.astype(BF16)
    zero = jnp.zeros((HEAD_DIM, kt.shape[1]), BF16)
    for h in range(N_KV_HEADS):
        kh = kt[h * HEAD_DIM:(h + 1) * HEAD_DIM]
        kw_ref[h, 0, :HEAD_DIM] = kh
        kw_ref[h, 0, HEAD_DIM:] = zero
        kw_ref[h, 1, :HEAD_DIM] = zero
        kw_ref[h, 1, HEAD_DIM:] = kh
    v = qkv[:, D_ATTN + D_KV:]
    vr = pltpu.roll(v, HEAD_DIM, 1)
    first = lax.broadcasted_iota(jnp.int32, v.shape, 1) < HEAD_DIM
    vw_ref[0, 0] = jnp.where(first, v, 1.0).astype(BF16)
    vw_ref[0, 1] = jnp.where(first, 1.0, vr).astype(BF16)
    vw_ref[1, 0] = jnp.where(first, vr, 1.0).astype(BF16)
    vw_ref[1, 1] = jnp.where(first, 1.0, v).astype(BF16)
    ut_ref[...] = lax.dot_general(wu_ref[...], hb, (((1,), (1,)), ((), ())),
                                  preferred_element_type=F32)


def _inproj(x, g_mix, wqkv, wut, bd, qg, kg, cos, sin):
    B, S, D = x.shape
    tm = TM_PROJ
    du = wut.shape[0]
    full = lambda shape: pl.BlockSpec(shape, lambda b, i: (0,) * len(shape))
    return pl.pallas_call(
        _inproj_body,
        grid=(B, S // tm),
        in_specs=[
            pl.BlockSpec((None, tm, D), lambda b, i: (b, i, 0)),
            full((1, D)), full(wqkv.shape), full(wut.shape), full(bd.shape),
            full((1, D_ATTN)), full((1, D_KV)),
            pl.BlockSpec((tm, LANES), lambda b, i: (i, 0)),
            pl.BlockSpec((tm, LANES), lambda b, i: (i, 0)),
        ],
        out_specs=[
            pl.BlockSpec((None, tm, D_ATTN), lambda b, i: (b, i, 0)),
            pl.BlockSpec((None, N_KV_HEADS, 2, LANES, tm), lambda b, i: (b, 0, 0, 0, i)),
            pl.BlockSpec((None, N_KV_HEADS, 2, tm, LANES), lambda b, i: (b, 0, 0, i, 0)),
            pl.BlockSpec((None, du, tm), lambda b, i: (b, 0, i)),
        ],
        out_shape=[
            jax.ShapeDtypeStruct((B, S, D_ATTN), BF16),
            jax.ShapeDtypeStruct((B, N_KV_HEADS, 2, LANES, S), BF16),
            jax.ShapeDtypeStruct((B, N_KV_HEADS, 2, S, LANES), BF16),
            jax.ShapeDtypeStruct((B, du, S), F32),
        ],
        compiler_params=_cparams(("parallel", "parallel")),
        name="inproj",
    )(x, g_mix, wqkv, wut, bd, qg, kg, cos, sin)


def _attn_body(q_ref, kw_ref, vw_ref, o_ref):

    def one_head(q, kw, vw):
        s = jnp.dot(q, kw, preferred_element_type=F32)
        m = jnp.max(s, axis=-1, keepdims=True)
        p = jnp.exp2(s - m).astype(BF16)
        return jnp.dot(p, vw, preferred_element_type=F32)

    for pair in range(D_ATTN // LANES):
        h = pair // (N_HEADS // N_KV_HEADS // 2)
        q = q_ref[:, pair * LANES:(pair + 1) * LANES]
        oe = one_head(q, kw_ref[h, 0], vw_ref[h, 0])
        oo = one_head(q, kw_ref[h, 1], vw_ref[h, 1])
        first = lax.broadcasted_iota(jnp.int32, oe.shape, 1) < HEAD_DIM
        num = jnp.where(first, oe, oo)
        den = jnp.where(first, pltpu.roll(oe, HEAD_DIM, 1), pltpu.roll(oo, HEAD_DIM, 1))
        o_ref[:, pair * LANES:(pair + 1) * LANES] = num / den


def _attention(q, kw, vw):
    B, S, _ = q.shape
    tq = TQ_ATTN
    return pl.pallas_call(
        _attn_body,
        grid=(B, S // tq),
        in_specs=[
            pl.BlockSpec((None, tq, D_ATTN), lambda b, i: (b, i, 0)),
            pl.BlockSpec((None, N_KV_HEADS, 2, LANES, S), lambda b, i: (b, 0, 0, 0, 0)),
            pl.BlockSpec((None, N_KV_HEADS, 2, S, LANES), lambda b, i: (b, 0, 0, 0, 0)),
        ],
        out_specs=pl.BlockSpec((None, tq, D_ATTN), lambda b, i: (b, i, 0)),
        out_shape=jax.ShapeDtypeStruct((B, S, D_ATTN), F32),
        compiler_params=_cparams(("parallel", "arbitrary")),
        name="attention",
    )(q, kw, vw)


def _fwd_twiddle_store(y, tw_ref, s1_ref, row0):
    yr, yi = y[:FFT_N1], y[FFT_N1:]
    twr, twi = tw_ref[:, :LANES], tw_ref[:, LANES:]
    s1_ref[pl.ds(row0, FFT_N1), :LANES] = (yr * twr - yi * twi).astype(BF16)
    s1_ref[pl.ds(row0, FFT_N1), LANES:] = (yr * twi + yi * twr).astype(BF16)


def _filtfft_body(x_ref, maf_ref, tw_ref, g_ref, h_ref, s1_ref):
    C = x_ref.shape[0]

    def step_a(c, carry):
        y = jnp.dot(maf_ref[...], x_ref[c].astype(BF16), preferred_element_type=F32)
        _fwd_twiddle_store(y, tw_ref, s1_ref, pl.multiple_of(c * FFT_N1, FFT_N1))
        return carry

    lax.fori_loop(0, C, step_a, 0, unroll=SEQ_UNROLL)
    z = jnp.dot(s1_ref[...], g_ref[...], preferred_element_type=F32)
    h_ref[...] = z.reshape(C, FFT_N1, 2 * LANES)


def _filter_fft(circ, cst):
    n_seq = circ.shape[0]
    C = C_HY
    full = lambda a: pl.BlockSpec(a.shape, lambda i: (0,) * a.ndim)
    maf, tw, g = cst["maf"].astype(BF16), cst["tw"], cst["g"].astype(BF16)
    return pl.pallas_call(
        _filtfft_body,
        grid=(n_seq // C,),
        in_specs=[pl.BlockSpec((C, FFT_N1, FFT_N2), lambda i: (i, 0, 0)), full(maf), full(tw), full(g)],
        out_specs=pl.BlockSpec((C, FFT_N1, 2 * LANES), lambda i: (i, 0, 0)),
        out_shape=jax.ShapeDtypeStruct((n_seq, FFT_N1, 2 * LANES), F32),
        scratch_shapes=[pltpu.VMEM((C * FFT_N1, 2 * LANES), BF16)],
        compiler_params=_cparams(("parallel",)),
        name="filter_fft",
    )(circ, maf, tw, g)


def _short_conv(x, par_ref, c):
    rows, lanes = x.shape
    a_i = lax.broadcasted_iota(jnp.int32, x.shape, 0)
    b_i = lax.broadcasted_iota(jnp.int32, x.shape, 1)
    l1 = pltpu.roll(x, 1, 1)
    l2 = pltpu.roll(l1, 1, 0)
    prev = jnp.where(b_i == 0, l2, l1)
    prev = jnp.where((a_i == 0) & (b_i == 0), 0.0, prev)
    r1 = pltpu.roll(x, lanes - 1, 1)
    r2 = pltpu.roll(r1, rows - 1, 0)
    nxt = jnp.where(b_i == lanes - 1, r2, r1)
    nxt = jnp.where((a_i == rows - 1) & (b_i == lanes - 1), 0.0, nxt)
    w0 = par_ref[0, pl.ds(c, 1), :]
    w1 = par_ref[1, pl.ds(c, 1), :]
    w2 = par_ref[2, pl.ds(c, 1), :]
    cb = par_ref[3, pl.ds(c, 1), :]
    return cb + prev * w0 + x * w1 + nxt * w2


def _hyena_body(v_ref, x1_ref, x2_ref, pv_ref, p1_ref, p2_ref, fb_ref, h_ref,
                ma_ref, tw_ref, g_ref, ginv_ref, mir_ref, mii_ref,
                o_ref, s1_ref, s2_ref, vc_ref, z1_ref):
    C = v_ref.shape[1]
    half = FFT_N1 // 2

    def spectral(order):
        z = jnp.dot(s1_ref[...], g_ref[...], preferred_element_type=F32)
        hs = h_ref[order].reshape(C * FFT_N1, 2 * LANES)
        zr, zi = z[:, :LANES], z[:, LANES:]
        hr, hi = hs[:, :LANES], hs[:, LANES:]
        pb = jnp.concatenate([zr * hr - zi * hi, zr * hi + zi * hr], axis=1).astype(BF16)
        s2_ref[...] = jnp.dot(pb, ginv_ref[...], preferred_element_type=F32)

    def inv_a(c):
        row0 = pl.multiple_of(c * FFT_N1, FFT_N1)
        y = s2_ref[pl.ds(row0, FFT_N1), :]
        yr, yi = y[:, :LANES], y[:, LANES:]
        twr, twi = tw_ref[:, :LANES], tw_ref[:, LANES:]
        ur = (yr * twr + yi * twi).astype(BF16)
        ui = (yi * twr - yr * twi).astype(BF16)
        out = (jnp.dot(mir_ref[...], ur, preferred_element_type=F32)
               + jnp.dot(mii_ref[...], ui, preferred_element_type=F32))
        return out[:half], out[half:]

    def fwd_a(c, xr, xi):
        xs = jnp.concatenate([xr, xi], axis=0).astype(BF16)
        y = jnp.dot(ma_ref[...], xs, preferred_element_type=F32)
        _fwd_twiddle_store(y, tw_ref, s1_ref, pl.multiple_of(c * FFT_N1, FFT_N1))

    def pass1_a(c, carry):
        vr = _short_conv(v_ref[0, c], pv_ref, c)
        vi = _short_conv(v_ref[1, c], pv_ref, c)
        vc_ref[0, c] = vr
        vc_ref[1, c] = vi
        fwd_a(c, vr, vi)
        return carry

    def pass1_b(c, carry):
        cr, ci = inv_a(c)
        bias = fb_ref[0, pl.ds(c, 1), :]
        zr = _short_conv(x1_ref[0, c], p1_ref, c) * (cr + bias * vc_ref[0, c])
        zi = _short_conv(x1_ref[1, c], p1_ref, c) * (ci + bias * vc_ref[1, c])
        z1_ref[0, c] = zr
        z1_ref[1, c] = zi
        fwd_a(c, zr, zi)
        return carry

    def pass2_b(c, carry):
        cr, ci = inv_a(c)
        bias = fb_ref[1, pl.ds(c, 1), :]
        o_ref[0, c] = _short_conv(x2_ref[0, c], p2_ref, c) * (cr + bias * z1_ref[0, c])
        o_ref[1, c] = _short_conv(x2_ref[1, c], p2_ref, c) * (ci + bias * z1_ref[1, c])
        return carry

    lax.fori_loop(0, C, pass1_a, 0, unroll=SEQ_UNROLL)
    spectral(0)
    lax.fori_loop(0, C, pass1_b, 0, unroll=SEQ_UNROLL)
    spectral(1)
    lax.fori_loop(0, C, pass2_b, 0, unroll=SEQ_UNROLL)


def _hyena(u4, par_u, fb, hspec, cst):
    B = u4.shape[0]
    C = C_HY
    J = D_HYENA // C
    rows = u4.shape[2]
    full = lambda a: pl.BlockSpec(a.shape, lambda j, p: (0,) * a.ndim)
    ma, g, ginv = cst["ma"].astype(BF16), cst["g"].astype(BF16), cst["ginv"].astype(BF16)
    mir, mii = cst["minv_r"].astype(BF16), cst["minv_i"].astype(BF16)
    tw = cst["tw"]
    u_spec = lambda k: pl.BlockSpec((2, C, rows, LANES), lambda j, p, k=k: (p, j + k * J, 0, 0))
    par_spec = lambda k: pl.BlockSpec((4, C, LANES), lambda j, p, k=k: (0, j + k * J, 0))
    return pl.pallas_call(
        _hyena_body,
        grid=(J, B // 2),
        in_specs=[
            u_spec(0), u_spec(1), u_spec(2), par_spec(0), par_spec(1), par_spec(2),
            pl.BlockSpec((2, C, LANES), lambda j, p: (0, j, 0)),
            pl.BlockSpec((2, C, FFT_N1, 2 * LANES), lambda j, p: (0, j, 0, 0)),
            full(ma), full(tw), full(g), full(ginv), full(mir), full(mii),
        ],
        out_specs=pl.BlockSpec((2, C, rows, LANES), lambda j, p: (p, j, 0, 0)),
        out_shape=jax.ShapeDtypeStruct((B, D_HYENA, rows, LANES), F32),
        scratch_shapes=[
            pltpu.VMEM((C * FFT_N1, 2 * LANES), BF16),
            pltpu.VMEM((C * FFT_N1, 2 * LANES), F32),
            pltpu.VMEM((2, C, rows, LANES), F32),
            pltpu.VMEM((2, C, rows, LANES), F32),
        ],
        compiler_params=_cparams(("parallel", "arbitrary")),
        name="hyena",
    )(u4, u4, u4, par_u, par_u, par_u, fb, hspec, ma, tw, g, ginv, mir, mii)


def _filter_time_domain(L, w_f1, b_f1, freq1, w_f2, b_f2, freq2, w_f3):
    hp = lax.Precision.HIGHEST
    bands = (FILTER_EMB - 1) // 2
    t = jnp.linspace(0.0, 1.0, L, dtype=F32)[:, None]
    w = (2.0 * math.pi / L) * jnp.arange(L, dtype=F32)[:, None]
    f = jnp.linspace(1e-4, bands - 1, bands, dtype=F32)[None]
    zf = f * w
    z = jnp.concatenate([t, jnp.cos(zf), -jnp.sin(zf)], axis=-1)
    back = lambda a: jnp.roll(a[::-1], 1, axis=0)
    zz = jnp.concatenate([z, back(z)], axis=0)
    h = jnp.sin(freq1 * (jnp.dot(zz, w_f1, precision=hp) + b_f1))
    h = jnp.sin(freq2 * (jnp.dot(h, w_f2, precision=hp) + b_f2))
    w3 = w_f3.reshape(-1, 2, 2, D_HYENA)
    hf = jnp.einsum("hoc,lh->ocl", w3[:, :, 0], h[:L], precision=hp)
    hb = jnp.einsum("hoc,lh->ocl", w3[:, :, 1], h[L:], precision=hp)
    max_decay = math.log(DECAY_TARGET) / FAST_DECAY_PCT
    min_decay = math.log(DECAY_TARGET) / SLOW_DECAY_PCT
    deltas = jnp.linspace(min_decay, max_decay, D_HYENA, dtype=F32)
    decay_f = jnp.exp(-jnp.abs(deltas)[:, None] * t[None, :, 0])
    decay_b = jnp.exp(-jnp.abs(deltas)[:, None] * back(t)[None, :, 0])
    hf = hf * decay_f[None]
    hb = hb * decay_b[None]
    hf = hf / (jnp.sum(jnp.abs(hf), axis=-1, keepdims=True) + EPS)
    hb = hb / (jnp.sum(jnp.abs(hb), axis=-1, keepdims=True) + EPS)
    first = lax.broadcasted_iota(jnp.int32, hb.shape, 2) == 0
    return jnp.concatenate([hf + jnp.where(first, hb, 0.0), jnp.where(first, 0.0, hb)], axis=-1)


def _route_lanes(lg):
    neg = -1e30
    lane = lax.broadcasted_iota(jnp.int32, lg.shape, 1)
    gmask = lane < N_GROUPS
    gl = jnp.where(gmask, lg, neg)
    gm = jnp.max(gl, axis=-1, keepdims=True)
    gsum = jnp.sum(jnp.where(gmask, jnp.exp(gl - gm), 0.0), axis=-1, keepdims=True)
    g_top = 1.0 / gsum
    g_sel = jnp.min(jnp.where(gl == gm, lane, LANES), axis=-1, keepdims=True)
    lo = N_GROUPS + EXPERTS_PER_GROUP * g_sel
    el = jnp.where((lane >= lo) & (lane < lo + EXPERTS_PER_GROUP), lg, neg)
    m1 = jnp.max(el, axis=-1, keepdims=True)
    i1 = jnp.min(jnp.where(el == m1, lane, LANES), axis=-1, keepdims=True)
    el2 = jnp.where(lane == i1, neg, el)
    m2 = jnp.max(el2, axis=-1, keepdims=True)
    i2 = jnp.min(jnp.where(el2 == m2, lane, LANES), axis=-1, keepdims=True)
    d = jnp.exp(m2 - m1)
    p1 = 1.0 / (1.0 + d)
    p2 = d / (1.0 + d)
    e1 = (i1 - N_GROUPS).astype(F32)
    e2 = (i2 - N_GROUPS).astype(F32)
    return jnp.where(lane == 0, e1, jnp.where(lane == 1, e2, jnp.where(lane == 2, g_top * p1,
                     jnp.where(lane == 3, g_top * p2, 0.0))))


def _pack_bf16_halves(a):
    w = a.shape[1] // 2
    bits = pltpu.bitcast(a.astype(BF16).astype(F32), jnp.uint32)
    return (bits[:, :w] >> 16) | (bits[:, w:] & jnp.uint32(0xFFFF0000))


def _unpack_bf16_halves(wd):
    lo = pltpu.bitcast(wd << 16, F32)
    hi = pltpu.bitcast(wd & jnp.uint32(0xFFFF0000), F32)
    return jnp.concatenate([lo, hi], axis=1)


def _store_row_tiles(ref, packed):
    for j in range(ROW_CHUNKS):
        ref[:, j, :] = packed[:, j * LANES:(j + 1) * LANES]


def _load_row_tiles(ref):
    return jnp.concatenate([ref[:, j, :] for j in range(ROW_CHUNKS)], axis=1)


def _outproj_body(ya_ref, yh_ref, x_ref, ga_ref, gh_ref, wo_ref, bd_ref, gm_ref, wrh_ref, wrl_ref, brt_ref,
                  x1_ref, h2_ref, rt_ref, rtt_ref):
    ya = ya_ref[...]
    yan = ya * lax.rsqrt(_group_sumsq(ya, bd_ref[...]) * (1.0 / HEAD_DIM) + EPS) * ga_ref[...]
    yh = yh_ref[...]
    tm = yh.shape[1]
    yh3 = yh.reshape(D_HYENA // HYENA_HEAD, HYENA_HEAD, tm)
    ms = jnp.mean(yh3 * yh3, axis=1, keepdims=True)
    yhn = (yh3 * lax.rsqrt(ms + EPS)).reshape(D_HYENA, tm) * gh_ref[...]
    mix = (jnp.dot(yan.astype(BF16), wo_ref[:D_ATTN, :], preferred_element_type=F32)
           + jnp.dot(yhn.T.astype(BF16), wo_ref[D_ATTN:, :], preferred_element_type=F32))
    x1 = x_ref[...] + mix
    x1_ref[...] = x1
    h2 = x1 * lax.rsqrt(jnp.mean(x1 * x1, axis=-1, keepdims=True) + EPS) * gm_ref[...]
    _store_row_tiles(h2_ref, _pack_bf16_halves(h2))
    hi = h2.astype(BF16)
    lo = (h2 - hi.astype(F32)).astype(BF16)
    lg = (jnp.dot(hi, wrh_ref[...], preferred_element_type=F32)
          + jnp.dot(lo, wrh_ref[...], preferred_element_type=F32)
          + jnp.dot(hi, wrl_ref[...], preferred_element_type=F32)) + brt_ref[...]
    route = _route_lanes(lg)
    rt_ref[...] = route
    rtt_ref[...] = route.T[:8]


def _outproj(ya, yht, x, ga, gh, wo, bd, gm, wrh, wrl, brt):
    B, S, D = x.shape
    tm = TM_PROJ
    full = lambda a: pl.BlockSpec(a.shape, lambda b, i: (0,) * a.ndim)
    return pl.pallas_call(
        _outproj_body,
        grid=(B, S // tm),
        in_specs=[
            pl.BlockSpec((None, tm, D_ATTN), lambda b, i: (b, i, 0)),
            pl.BlockSpec((None, D_HYENA, tm), lambda b, i: (b, 0, i)),
            pl.BlockSpec((None, tm, D), lambda b, i: (b, i, 0)),
            full(ga), full(gh), full(wo), full(bd), full(gm), full(wrh), full(wrl), full(brt),
        ],
        out_specs=[
            pl.BlockSpec((None, tm, D), lambda b, i: (b, i, 0)),
            pl.BlockSpec((None, tm, ROW_CHUNKS, LANES), lambda b, i: (b, i, 0, 0)),
            pl.BlockSpec((None, tm, LANES), lambda b, i: (b, i, 0)),
            pl.BlockSpec((None, 8, tm), lambda b, i: (b, 0, i)),
        ],
        out_shape=[
            jax.ShapeDtypeStruct((B, S, D), F32),
            jax.ShapeDtypeStruct((B, S, ROW_CHUNKS, LANES), jnp.uint32),
            jax.ShapeDtypeStruct((B, S, LANES), F32),
            jax.ShapeDtypeStruct((B, 8, S), F32),
        ],
        compiler_params=_cparams(("parallel", "parallel")),
        name="outproj",
    )(ya, yht, x, ga, gh, wo, bd, gm, wrh, wrl, brt)


def _moe_body(be_ref, ra_ref, h2_hbm, wg_ref, wu_ref, wd_ref, y_hbm,
              wg_s, wu_s, wd_s, xbuf, ybuf, sem_in, sem_out, *, n_tok):
    i = pl.program_id(0)
    nb = pl.num_programs(0)
    T = xbuf.shape[1]
    slot = i % 2

    def issue_gathers(blk, sl):
        for r in range(T):
            tok = ra_ref[blk * T + r] & (n_tok - 1)
            pltpu.make_async_copy(h2_hbm.at[tok], xbuf.at[sl, r], sem_in.at[sl]).start()

    def issue_scatters(blk, sl):
        for r in range(T):
            pltpu.make_async_copy(ybuf.at[sl, r], y_hbm.at[ra_ref[blk * T + r]], sem_out.at[sl]).start()

    def block_in_wait(sl):
        pltpu.make_async_copy(h2_hbm.at[pl.ds(0, T)], xbuf.at[sl], sem_in.at[sl]).wait()

    def block_out_wait(sl):
        pltpu.make_async_copy(ybuf.at[sl], y_hbm.at[pl.ds(0, T)], sem_out.at[sl]).wait()

    @pl.when(i == 0)
    def _():
        issue_gathers(0, 0)

    for sl in range(2):
        @pl.when((i + 1 < nb) & (slot == sl))
        def _():
            issue_gathers(i + 1, 1 - sl)

    prev = be_ref[jnp.maximum(i - 1, 0)]

    @pl.when((i == 0) | (be_ref[i] != prev))
    def _():
        wg_s[...] = wg_ref[...].astype(BF16)
        wu_s[...] = wu_ref[...].astype(BF16)
        wd_s[...] = wd_ref[...].astype(BF16)

    block_in_wait(slot)
    x = _unpack_bf16_halves(_load_row_tiles(xbuf.at[slot])).astype(BF16)
    a = jnp.dot(x, wg_s[...], preferred_element_type=F32)
    b = jnp.dot(x, wu_s[...], preferred_element_type=F32)
    hmid = (a * jax.nn.sigmoid(a)) * b
    y = _pack_bf16_halves(jnp.dot(hmid.astype(BF16), wd_s[...], preferred_element_type=F32))

    @pl.when(i >= 2)
    def _():
        block_out_wait(slot)

    _store_row_tiles(ybuf.at[slot], y)
    for sl in range(2):
        @pl.when(slot == sl)
        def _():
            issue_scatters(i, sl)

    @pl.when(i == nb - 1)
    def _():
        block_out_wait(slot)

        @pl.when(nb >= 2)
        def _():
            block_out_wait(1 - slot)


def _moe_experts(block_e, row_a, h2p, w_gate, w_up, w_down, n_out_rows):
    n_tok = h2p.shape[0]
    row = h2p.shape[1:]
    D = w_gate.shape[1]
    T = TB_MOE
    assert n_tok & (n_tok - 1) == 0
    grid_spec = pltpu.PrefetchScalarGridSpec(
        num_scalar_prefetch=2,
        grid=(row_a.shape[0] // T,),
        in_specs=[
            pl.BlockSpec(memory_space=pl.ANY),
            pl.BlockSpec((None, D, D_EXPERT), lambda i, be, ra: (be[i], 0, 0)),
            pl.BlockSpec((None, D, D_EXPERT), lambda i, be, ra: (be[i], 0, 0)),
            pl.BlockSpec((None, D_EXPERT, D), lambda i, be, ra: (be[i], 0, 0)),
        ],
        out_specs=pl.BlockSpec(memory_space=pl.ANY),
        scratch_shapes=[
            pltpu.VMEM((D, D_EXPERT), BF16), pltpu.VMEM((D, D_EXPERT), BF16), pltpu.VMEM((D_EXPERT, D), BF16),
            pltpu.VMEM((2, T) + row, jnp.uint32), pltpu.VMEM((2, T) + row, jnp.uint32),
            pltpu.SemaphoreType.DMA((2,)), pltpu.SemaphoreType.DMA((2,)),
        ],
    )
    return pl.pallas_call(
        functools.partial(_moe_body, n_tok=n_tok),
        grid_spec=grid_spec,
        out_shape=jax.ShapeDtypeStruct((n_out_rows,) + row, jnp.uint32),
        compiler_params=_cparams(("arbitrary",)),
        name="moe_experts",
    )(block_e, row_a, h2p, w_gate, w_up, w_down)


def _dispatch(e_flat, N):
    T = TB_MOE
    NK = N * TOP_K
    experts = jnp.arange(N_EXPERTS, dtype=jnp.int32)
    order = jnp.argsort(e_flat).astype(jnp.int32)
    onehot = (e_flat[:, None] == experts[None]).astype(jnp.int32)
    counts = jnp.sum(onehot, axis=0)
    ends = jnp.cumsum(counts)
    starts = ends - counts
    padded = (counts + T - 1) // T * T
    pends = jnp.cumsum(padded)
    pstarts = pends - padded
    n_rows = -(-(NK + N_EXPERTS * (T - 1)) // T) * T
    n_blocks = n_rows // T
    blk_start = jnp.arange(n_blocks, dtype=jnp.int32) * T
    block_e = jnp.clip(jnp.sum((pends[None, :] <= blk_start[:, None]).astype(jnp.int32), axis=1),
                       0, N_EXPERTS - 1)
    oh_b = (block_e[:, None] == experts[None]).astype(jnp.int32)
    base = jnp.sum(oh_b * (starts - pstarts)[None], axis=1) + blk_start
    end_b = jnp.sum(oh_b * ends[None], axis=1)
    lane = jnp.arange(T, dtype=jnp.int32)[None]
    src = base[:, None] + lane
    pad_id = NK + blk_start[:, None] + lane - end_b[:, None]
    row_a = jnp.where(src < end_b[:, None], order[jnp.clip(src, 0, NK - 1)], pad_id)
    return block_e.astype(jnp.int32), row_a.reshape(n_rows).astype(jnp.int32)


def _final_body(x1_ref, y0_ref, y1_ref, rt_ref, p_ref, gp_ref, wg_ref, bg_ref, wp_ref, gf_ref, o_ref):
    w0 = rt_ref[:, 2:3]
    w1 = rt_ref[:, 3:4]
    y0 = _unpack_bf16_halves(_load_row_tiles(y0_ref))
    y1 = _unpack_bf16_halves(_load_row_tiles(y1_ref))
    x2 = x1_ref[...] + (y0 * w0 + y1 * w1)
    hp = x2 * lax.rsqrt(jnp.mean(x2 * x2, axis=-1, keepdims=True) + EPS) * gp_ref[...]
    gate = jax.nn.sigmoid(jnp.dot(hp.astype(BF16), wg_ref[...], preferred_element_type=F32) + bg_ref[...])
    pe = jnp.dot(p_ref[...].astype(BF16), wp_ref[...], preferred_element_type=F32)
    x3 = x2 + pe * gate
    o_ref[...] = x3 * lax.rsqrt(jnp.mean(x3 * x3, axis=-1, keepdims=True) + EPS) * gf_ref[...]


def _final(x1, y, route, p, gp, wg, bg, wp, gf):
    N, D = x1.shape
    tm = TM_PROJ
    row = lambda w: pl.BlockSpec((tm, w), lambda i: (i, 0))
    full = lambda a: pl.BlockSpec(a.shape, lambda i: (0,) * a.ndim)
    y0, y1 = y, y
    return pl.pallas_call(
        _final_body,
        grid=(N // tm,),
        in_specs=[row(D), pl.BlockSpec((tm, ROW_CHUNKS, LANES), lambda i: (i, 0, 0)),
                  pl.BlockSpec((tm, ROW_CHUNKS, LANES), lambda i: (i + N // tm, 0, 0)),
                  row(LANES), row(p.shape[1]),
                  full(gp), full(wg), full(bg), full(wp), full(gf)],
        out_specs=row(D),
        out_shape=jax.ShapeDtypeStruct((N, D), F32),
        compiler_params=_cparams(("parallel",)),
        name="ple_final",
    )(x1, y0, y1, route, p, gp, wg, bg, wp, gf)


def kernel(x, p, g_mix, w_in, q_gain, k_gain, conv_w, conv_b, w_f1, b_f1, freq1, w_f2, b_f2, freq2, w_f3, filt_bias, g_attn_out, g_hyena_out, w_out, g_moe, w_group, b_group, w_router, b_router, w_gate, w_up, w_down, g_ple, w_ple_gate, b_ple_gate, w_ple, g_final):
    B, S, D = x.shape
    N = B * S
    assert p.shape[0] == 1 and S == (FFT_N1 // 2) * FFT_N2 and B % 2 == 0
    i = 0
    cst = _dft_constants()
    cos, sin = _rope_tables(S)
    bd = _block_diag_ones(D_ATTN, HEAD_DIM)

    n_qkv = D_ATTN + 2 * D_KV
    wqkv = w_in[i][:, :n_qkv].astype(BF16)
    wut = w_in[i][:, n_qkv:].T.astype(BF16)
    q, kw, vw, ut = _inproj(x, g_mix[i][None], wqkv, wut, bd,
                            jnp.tile(q_gain[i], N_HEADS)[None], jnp.tile(k_gain[i], N_KV_HEADS)[None], cos, sin)

    ya = _attention(q, kw, vw)

    circ = _filter_time_domain(S, w_f1[i], b_f1[i], freq1[i], w_f2[i], b_f2[i], freq2[i], w_f3[i])
    hspec = _filter_fft(circ.reshape(2 * D_HYENA, FFT_N1, FFT_N2), cst)
    hspec = hspec.reshape(2, D_HYENA, FFT_N1, 2 * LANES)
    du = ut.shape[1]
    u4 = ut.reshape(B, du, S // LANES, LANES)
    par_u = jnp.broadcast_to(jnp.concatenate([conv_w[i], conv_b[i][None]], 0)[:, :, None], (4, du, LANES))
    fb = jnp.broadcast_to(filt_bias[i][:, :, None], (2, D_HYENA, LANES))
    yh4 = _hyena(u4, par_u, fb, hspec, cst)
    yht = yh4.reshape(B, D_HYENA, S)

    wrt = jnp.zeros((D, LANES), F32).at[:, :N_GROUPS].set(w_group[i]).at[:, N_GROUPS:N_GROUPS + N_EXPERTS].set(w_router[i])
    brt = jnp.zeros((1, LANES), F32).at[0, :N_GROUPS].set(b_group[i]).at[0, N_GROUPS:N_GROUPS + N_EXPERTS].set(b_router[i])
    wrh = wrt.astype(BF16)
    wrl = (wrt - wrh.astype(F32)).astype(BF16)
    x1, h2, route, route_t = _outproj(ya, yht, x, g_attn_out[i][None], g_hyena_out[i][:, None],
                                      w_out[i].astype(BF16), bd, g_moe[i][None], wrh, wrl, brt)

    e_flat = jnp.transpose(route_t[:, :TOP_K], (1, 0, 2)).reshape(TOP_K * N).astype(jnp.int32)
    block_e, row_a = _dispatch(e_flat, N)
    y = _moe_experts(block_e, row_a, h2.reshape(N, ROW_CHUNKS, LANES), w_gate[i], w_up[i], w_down[i],
                     row_a.shape[0])

    out = _final(x1.reshape(N, D), y, route.reshape(N, LANES), p[i].reshape(N, -1), g_ple[i][None],
                 w_ple_gate[i].astype(BF16), b_ple_gate[i][None], w_ple[i].astype(BF16), g_final[None])
    return out.reshape(B, S, D)
```

```python
import functools
import math

import numpy as np
import jax
import jax.numpy as jnp
from jax import lax
from jax.experimental import pallas as pl
from jax.experimental.pallas import tpu as pltpu

F32 = jnp.float32
BF16 = jnp.bfloat16

D_MODEL = 1024
EPS = 1e-6
GRID_W = 64
N_HEADS = 8
N_KV_HEADS = 2
HEAD_DIM = 64
D_ATTN = N_HEADS * HEAD_DIM
D_KV = N_KV_HEADS * HEAD_DIM
ROPE_THETA = 10000.0
D_HYENA = 512
HYENA_HEAD = 64
FILTER_EMB = 33
FAST_DECAY_PCT = 0.3
SLOW_DECAY_PCT = 1.5
DECAY_TARGET = 1e-2
N_GROUPS = 4
EXPERTS_PER_GROUP = 8
N_EXPERTS = N_GROUPS * EXPERTS_PER_GROUP
TOP_K = 2
D_EXPERT = 512

LANES = 128
FFT_N1 = 64
FFT_N2 = 128
VMEM_LIMIT = 56 * 1024 * 1024

TM_PROJ = 512
TQ_ATTN = 256
C_HY = 32
ROW_CHUNKS = D_MODEL // 2 // LANES
SEQ_UNROLL = 8
TB_MOE = 256


def _cparams(sem):
    return pltpu.CompilerParams(dimension_semantics=sem, vmem_limit_bytes=VMEM_LIMIT)


def _rope_tables(S):
    half = HEAD_DIM // 2
    t = jnp.arange(S, dtype=F32)
    r_idx = jnp.floor(t / GRID_W)
    c_idx = t - r_idx * GRID_W
    inv = ROPE_THETA ** (-jnp.arange(0, half, 2, dtype=F32) / half)
    ang_r = r_idx[:, None] * inv[None]
    ang_c = c_idx[:, None] * inv[None]
    cos_h = jnp.concatenate([jnp.cos(ang_r), jnp.cos(ang_r), jnp.cos(ang_c), jnp.cos(ang_c)], axis=-1)
    sin_h = jnp.concatenate([-jnp.sin(ang_r), jnp.sin(ang_r), -jnp.sin(ang_c), jnp.sin(ang_c)], axis=-1)
    return jnp.tile(cos_h, (1, 2)), jnp.tile(sin_h, (1, 2))


def _dft_constants():
    n1, n2 = FFT_N1, FFT_N2
    n = n1 * n2
    a = np.arange(n1)
    ang = 2.0 * np.pi * np.outer(a, a) / n1
    far, fai = np.cos(ang), -np.sin(ang)
    hlf = n1 // 2
    ma = np.block([[far[:, :hlf], -fai[:, :hlf]], [fai[:, :hlf], far[:, :hlf]]])
    maf = np.concatenate([far, fai], axis=0)
    b = np.arange(n2)
    angt = 2.0 * np.pi * np.outer(a, b) / n
    tw = np.concatenate([np.cos(angt), -np.sin(angt)], axis=1)
    angb = 2.0 * np.pi * np.outer(b, b) / n2
    fbr, fbi = np.cos(angb), -np.sin(angb)
    g = np.block([[fbr, fbi], [-fbi, fbr]])
    ginv = np.block([[fbr, -fbi], [fbi, fbr]])
    minv_r = np.concatenate([far[:hlf], -fai[:hlf]], axis=0) / n
    minv_i = np.concatenate([fai[:hlf], far[:hlf]], axis=0) / n
    f = lambda m: jnp.asarray(m.astype(np.float32))
    return dict(ma=f(ma), maf=f(maf), tw=f(tw), g=f(g), ginv=f(ginv), minv_r=f(minv_r), minv_i=f(minv_i))


def _block_diag_ones(width, group):
    i = np.arange(width) // group
    return jnp.asarray((i[:, None] == i[None, :]).astype(np.float32)).astype(BF16)


def _group_sumsq(a, bd):
    sq = a * a
    hi = sq.astype(BF16)
    lo = (sq - hi.astype(F32)).astype(BF16)
    return (jnp.dot(hi, bd, preferred_element_type=F32) + jnp.dot(lo, bd, preferred_element_type=F32))


def _head_norm_rope(a, gain, bd, cos, sin):
    width = a.shape[-1]
    n = a * lax.rsqrt(_group_sumsq(a, bd) * (1.0 / HEAD_DIM) + EPS) * gain
    rep = width // LANES
    if rep > 1:
        cos = jnp.concatenate([cos] * rep, axis=-1)
        sin = jnp.concatenate([sin] * rep, axis=-1)
    fwd = pltpu.roll(n, width - 16, 1)
    bwd = pltpu.roll(n, 16, 1)
    lane = lax.broadcasted_iota(jnp.int32, n.shape, 1)
    sw = jnp.where((lane % 32) < 16, fwd, bwd)
    return n * cos + sw * sin


def _inproj_body(x_ref, g_ref, wqkv_ref, wu_ref, bd_ref, qg_ref, kg_ref, cos_ref, sin_ref,
                 q_ref, kw_ref, vw_ref, ut_ref):
    x = x_ref[...]
    h = x * lax.rsqrt(jnp.mean(x * x, axis=-1, keepdims=True) + EPS) * g_ref[...]
    hb = h.astype(BF16)
    qkv = jnp.dot(hb, wqkv_ref[...], preferred_element_type=F32)
    cos = cos_ref[...]
    sin = sin_ref[...]
    bd = bd_ref[...]
    q = _head_norm_rope(qkv[:, :D_ATTN], qg_ref[...], bd, cos, sin)
    q_ref[...] = (q * (HEAD_DIM ** -0.5 * math.log2(math.e))).astype(BF16)
    k = _head_norm_rope(qkv[:, D_ATTN:D_ATTN + D_KV], kg_ref[...], bd[:D_KV, :D_KV], cos, sin)
    kt = k.T.astype(BF16)
    zero = jnp.zeros((HEAD_DIM, kt.shape[1]), BF16)
    for h in range(N_KV_HEADS):
        kh = kt[h * HEAD_DIM:(h + 1) * HEAD_DIM]
        kw_ref[h, 0, :HEAD_DIM] = kh
        kw_ref[h, 0, HEAD_DIM:] = zero
        kw_ref[h, 1, :HEAD_DIM] = zero
        kw_ref[h, 1, HEAD_DIM:] = kh
    v = qkv[:, D_ATTN + D_KV:]
    vr = pltpu.roll(v, HEAD_DIM, 1)
    first = lax.broadcasted_iota(jnp.int32, v.shape, 1) < HEAD_DIM
    vw_ref[0, 0] = jnp.where(first, v, 1.0).astype(BF16)
    vw_ref[0, 1] = jnp.where(first, 1.0, vr).astype(BF16)
    vw_ref[1, 0] = jnp.where(first, vr, 1.0).astype(BF16)
    vw_ref[1, 1] = jnp.where(first, 1.0, v).astype(BF16)
    ut_ref[...] = lax.dot_general(wu_ref[...], hb, (((1,), (1,)), ((), ())),
                                  preferred_element_type=F32)


def _inproj(x, g_mix, wqkv, wut, bd, qg, kg, cos, sin):
    B, S, D = x.shape
    tm = TM_PROJ
    du = wut.shape[0]
    full = lambda shape: pl.BlockSpec(shape, lambda b, i: (0,) * len(shape))
    return pl.pallas_call(
        _inproj_body,
        grid=(B, S // tm),
        in_specs=[
            pl.BlockSpec((None, tm, D), lambda b, i: (b, i, 0)),
            full((1, D)), full(wqkv.shape), full(wut.shape), full(bd.shape),
            full((1, D_ATTN)), full((1, D_KV)),
            pl.BlockSpec((tm, LANES), lambda b, i: (i, 0)),
            pl.BlockSpec((tm, LANES), lambda b, i: (i, 0)),
        ],
        out_specs=[
            pl.BlockSpec((None, tm, D_ATTN), lambda b, i: (b, i, 0)),
            pl.BlockSpec((None, N_KV_HEADS, 2, LANES, tm), lambda b, i: (b, 0, 0, 0, i)),
            pl.BlockSpec((None, N_KV_HEADS, 2, tm, LANES), lambda b, i: (b, 0, 0, i, 0)),
            pl.BlockSpec((None, du, tm), lambda b, i: (b, 0, i)),
        ],
        out_shape=[
            jax.ShapeDtypeStruct((B, S, D_ATTN), BF16),
            jax.ShapeDtypeStruct((B, N_KV_HEADS, 2, LANES, S), BF16),
            jax.ShapeDtypeStruct((B, N_KV_HEADS, 2, S, LANES), BF16),
            jax.ShapeDtypeStruct((B, du, S), F32),
        ],
        compiler_params=_cparams(("parallel", "parallel")),
        name="inproj",
    )(x, g_mix, wqkv, wut, bd, qg, kg, cos, sin)


def _attn_body(q_ref, kw_ref, vw_ref, o_ref):

    def one_head(q, kw, vw):
        s = jnp.dot(q, kw, preferred_element_type=F32)
        m = jnp.max(s, axis=-1, keepdims=True)
        p = jnp.exp2(s - m).astype(BF16)
        return jnp.dot(p, vw, preferred_element_type=F32)

    for pair in range(D_ATTN // LANES):
        h = pair // (N_HEADS // N_KV_HEADS // 2)
        q = q_ref[:, pair * LANES:(pair + 1) * LANES]
        oe = one_head(q, kw_ref[h, 0], vw_ref[h, 0])
        oo = one_head(q, kw_ref[h, 1], vw_ref[h, 1])
        first = lax.broadcasted_iota(jnp.int32, oe.shape, 1) < HEAD_DIM
        num = jnp.where(first, oe, oo)
        den = jnp.where(first, pltpu.roll(oe, HEAD_DIM, 1), pltpu.roll(oo, HEAD_DIM, 1))
        o_ref[:, pair * LANES:(pair + 1) * LANES] = num / den


def _attention(q, kw, vw):
    B, S, _ = q.shape
    tq = TQ_ATTN
    return pl.pallas_call(
        _attn_body,
        grid=(B, S // tq),
        in_specs=[
            pl.BlockSpec((None, tq, D_ATTN), lambda b, i: (b, i, 0)),
            pl.BlockSpec((None, N_KV_HEADS, 2, LANES, S), lambda b, i: (b, 0, 0, 0, 0)),
            pl.BlockSpec((None, N_KV_HEADS, 2, S, LANES), lambda b, i: (b, 0, 0, 0, 0)),
        ],
        out_specs=pl.BlockSpec((None, tq, D_ATTN), lambda b, i: (b, i, 0)),
        out_shape=jax.ShapeDtypeStruct((B, S, D_ATTN), F32),
        compiler_params=_cparams(("parallel", "arbitrary")),
        name="attention",
    )(q, kw, vw)


def _fwd_twiddle_store(y, tw_ref, s1_ref, row0):
    yr, yi = y[:FFT_N1], y[FFT_N1:]
    twr, twi = tw_ref[:, :LANES], tw_ref[:, LANES:]
    s1_ref[pl.ds(row0, FFT_N1), :LANES] = (yr * twr - yi * twi).astype(BF16)
    s1_ref[pl.ds(row0, FFT_N1), LANES:] = (yr * twi + yi * twr).astype(BF16)


def _filtfft_body(x_ref, maf_ref, tw_ref, g_ref, h_ref, s1_ref):
    C = x_ref.shape[0]

    def step_a(c, carry):
        y = jnp.dot(maf_ref[...], x_ref[c].astype(BF16), preferred_element_type=F32)
        _fwd_twiddle_store(y, tw_ref, s1_ref, pl.multiple_of(c * FFT_N1, FFT_N1))
        return carry

    lax.fori_loop(0, C, step_a, 0, unroll=SEQ_UNROLL)
    z = jnp.dot(s1_ref[...], g_ref[...], preferred_element_type=F32)
    h_ref[...] = z.reshape(C, FFT_N1, 2 * LANES)


def _filter_fft(circ, cst):
    n_seq = circ.shape[0]
    C = C_HY
    full = lambda a: pl.BlockSpec(a.shape, lambda i: (0,) * a.ndim)
    maf, tw, g = cst["maf"].astype(BF16), cst["tw"], cst["g"].astype(BF16)
    return pl.pallas_call(
        _filtfft_body,
        grid=(n_seq // C,),
        in_specs=[pl.BlockSpec((C, FFT_N1, FFT_N2), lambda i: (i, 0, 0)), full(maf), full(tw), full(g)],
        out_specs=pl.BlockSpec((C, FFT_N1, 2 * LANES), lambda i: (i, 0, 0)),
        out_shape=jax.ShapeDtypeStruct((n_seq, FFT_N1, 2 * LANES), F32),
        scratch_shapes=[pltpu.VMEM((C * FFT_N1, 2 * LANES), BF16)],
        compiler_params=_cparams(("parallel",)),
        name="filter_fft",
    )(circ, maf, tw, g)


def _short_conv(x, par_ref, c):
    rows, lanes = x.shape
    a_i = lax.broadcasted_iota(jnp.int32, x.shape, 0)
    b_i = lax.broadcasted_iota(jnp.int32, x.shape, 1)
    l1 = pltpu.roll(x, 1, 1)
    l2 = pltpu.roll(l1, 1, 0)
    prev = jnp.where(b_i == 0, l2, l1)
    prev = jnp.where((a_i == 0) & (b_i == 0), 0.0, prev)
    r1 = pltpu.roll(x, lanes - 1, 1)
    r2 = pltpu.roll(r1, rows - 1, 0)
    nxt = jnp.where(b_i == lanes - 1, r2, r1)
    nxt = jnp.where((a_i == rows - 1) & (b_i == lanes - 1), 0.0, nxt)
    w0 = par_ref[0, pl.ds(c, 1), :]
    w1 = par_ref[1, pl.ds(c, 1), :]
    w2 = par_ref[2, pl.ds(c, 1), :]
    cb = par_ref[3, pl.ds(c, 1), :]
    return cb + prev * w0 + x * w1 + nxt * w2


def _hyena_body(v_ref, x1_ref, x2_ref, pv_ref, p1_ref, p2_ref, fb_ref, h_ref,
                ma_ref, tw_ref, g_ref, ginv_ref, mir_ref, mii_ref,
                o_ref, s1_ref, s2_ref, vc_ref, z1_ref):
    C = v_ref.shape[1]
    half = FFT_N1 // 2

    def spectral(order):
        z = jnp.dot(s1_ref[...], g_ref[...], preferred_element_type=F32)
        hs = h_ref[order].reshape(C * FFT_N1, 2 * LANES)
        zr, zi = z[:, :LANES], z[:, LANES:]
        hr, hi = hs[:, :LANES], hs[:, LANES:]
        pb = jnp.concatenate([zr * hr - zi * hi, zr * hi + zi * hr], axis=1).astype(BF16)
        s2_ref[...] = jnp.dot(pb, ginv_ref[...], preferred_element_type=F32)

    def inv_a(c):
        row0 = pl.multiple_of(c * FFT_N1, FFT_N1)
        y = s2_ref[pl.ds(row0, FFT_N1), :]
        yr, yi = y[:, :LANES], y[:, LANES:]
        twr, twi = tw_ref[:, :LANES], tw_ref[:, LANES:]
        ur = (yr * twr + yi * twi).astype(BF16)
        ui = (yi * twr - yr * twi).astype(BF16)
        out = (jnp.dot(mir_ref[...], ur, preferred_element_type=F32)
               + jnp.dot(mii_ref[...], ui, preferred_element_type=F32))
        return out[:half], out[half:]

    def fwd_a(c, xr, xi):
        xs = jnp.concatenate([xr, xi], axis=0).astype(BF16)
        y = jnp.dot(ma_ref[...], xs, preferred_element_type=F32)
        _fwd_twiddle_store(y, tw_ref, s1_ref, pl.multiple_of(c * FFT_N1, FFT_N1))

    def pass1_a(c, carry):
        vr = _short_conv(v_ref[0, c], pv_ref, c)
        vi = _short_conv(v_ref[1, c], pv_ref, c)
        vc_ref[0, c] = vr
        vc_ref[1, c] = vi
        fwd_a(c, vr, vi)
        return carry

    def pass1_b(c, carry):
        cr, ci = inv_a(c)
        bias = fb_ref[0, pl.ds(c, 1), :]
        zr = _short_conv(x1_ref[0, c], p1_ref, c) * (cr + bias * vc_ref[0, c])
        zi = _short_conv(x1_ref[1, c], p1_ref, c) * (ci + bias * vc_ref[1, c])
        z1_ref[0, c] = zr
        z1_ref[1, c] = zi
        fwd_a(c, zr, zi)
        return carry

    def pass2_b(c, carry):
        cr, ci = inv_a(c)
        bias = fb_ref[1, pl.ds(c, 1), :]
        vc_ref[0, c] = _short_conv(x2_ref[0, c], p2_ref, c) * (cr + bias * z1_ref[0, c])
        vc_ref[1, c] = _short_conv(x2_ref[1, c], p2_ref, c) * (ci + bias * z1_ref[1, c])
        return carry

    lax.fori_loop(0, C, pass1_a, 0, unroll=SEQ_UNROLL)
    spectral(0)
    lax.fori_loop(0, C, pass1_b, 0, unroll=SEQ_UNROLL)
    spectral(1)
    lax.fori_loop(0, C, pass2_b, 0, unroll=SEQ_UNROLL)
    for b2 in range(2):
        for a in range(vc_ref.shape[2]):
            o_ref[b2, :, a * LANES:(a + 1) * LANES] = vc_ref[b2, :, a, :]


def _hyena(u4, par_u, fb, hspec, cst):
    B = u4.shape[0]
    C = C_HY
    J = D_HYENA // C
    rows = u4.shape[2]
    full = lambda a: pl.BlockSpec(a.shape, lambda j, p: (0,) * a.ndim)
    ma, g, ginv = cst["ma"].astype(BF16), cst["g"].astype(BF16), cst["ginv"].astype(BF16)
    mir, mii = cst["minv_r"].astype(BF16), cst["minv_i"].astype(BF16)
    tw = cst["tw"]
    u_spec = lambda k: pl.BlockSpec((2, C, rows, LANES), lambda j, p, k=k: (p, j + k * J, 0, 0))
    par_spec = lambda k: pl.BlockSpec((4, C, LANES), lambda j, p, k=k: (0, j + k * J, 0))
    return pl.pallas_call(
        _hyena_body,
        grid=(J, B // 2),
        in_specs=[
            u_spec(0), u_spec(1), u_spec(2), par_spec(0), par_spec(1), par_spec(2),
            pl.BlockSpec((2, C, LANES), lambda j, p: (0, j, 0)),
            pl.BlockSpec((2, C, FFT_N1, 2 * LANES), lambda j, p: (0, j, 0, 0)),
            full(ma), full(tw), full(g), full(ginv), full(mir), full(mii),
        ],
        out_specs=pl.BlockSpec((2, C, rows * LANES), lambda j, p: (p, j, 0)),
        out_shape=jax.ShapeDtypeStruct((B, D_HYENA, rows * LANES), F32),
        scratch_shapes=[
            pltpu.VMEM((C * FFT_N1, 2 * LANES), BF16),
            pltpu.VMEM((C * FFT_N1, 2 * LANES), F32),
            pltpu.VMEM((2, C, rows, LANES), F32),
            pltpu.VMEM((2, C, rows, LANES), F32),
        ],
        compiler_params=_cparams(("parallel", "arbitrary")),
        name="hyena",
    )(u4, u4, u4, par_u, par_u, par_u, fb, hspec, ma, tw, g, ginv, mir, mii)


def _dot3(a, b):
    ah = a.astype(BF16)
    al = (a - ah.astype(F32)).astype(BF16)
    bh = b.astype(BF16)
    bl = (b - bh.astype(F32)).astype(BF16)
    return (jnp.dot(ah, bh, preferred_element_type=F32) + jnp.dot(al, bh, preferred_element_type=F32)
            + jnp.dot(ah, bl, preferred_element_type=F32))


def _filtgen_body(zt_ref, w1_ref, b1_ref, f1_ref, w2_ref, b2_ref, f2_ref, w3f_ref, w3b_ref, ad_ref, tt_ref,
                  o_ref, hid_ref):
    L = hid_ref.shape[1] // 2

    @pl.when(pl.program_id(0) == 0)
    def _():
        h1 = jnp.sin(f1_ref[...] * (_dot3(w1_ref[...], zt_ref[...]) + b1_ref[...]))
        hid_ref[...] = jnp.sin(f2_ref[...] * (_dot3(w2_ref[...], h1) + b2_ref[...]))

    ad = ad_ref[...]
    hf = _dot3(w3f_ref[...], hid_ref[:, :L]) * jnp.exp(-ad * tt_ref[:, :L])
    hb = _dot3(w3b_ref[...], hid_ref[:, L:]) * jnp.exp(-ad * tt_ref[:, L:])
    hf = hf / (jnp.sum(jnp.abs(hf), axis=-1, keepdims=True) + EPS)
    hb = hb / (jnp.sum(jnp.abs(hb), axis=-1, keepdims=True) + EPS)
    first = lax.broadcasted_iota(jnp.int32, hb.shape, 1) == 0
    cf = hf + jnp.where(first, hb, 0.0)
    cb = jnp.where(first, 0.0, hb)
    half = L // LANES
    for a in range(half):
        o_ref[:, a, :] = cf[:, a * LANES:(a + 1) * LANES]
        o_ref[:, half + a, :] = cb[:, a * LANES:(a + 1) * LANES]


def _filter_gen(L, w_f1, b_f1, freq1, w_f2, b_f2, freq2, w_f3):
    bands = (FILTER_EMB - 1) // 2
    t = jnp.linspace(0.0, 1.0, L, dtype=F32)[:, None]
    w = (2.0 * math.pi / L) * jnp.arange(L, dtype=F32)[:, None]
    f = jnp.linspace(1e-4, bands - 1, bands, dtype=F32)[None]
    zf = f * w
    z = jnp.concatenate([t, jnp.cos(zf), -jnp.sin(zf)], axis=-1)
    back = lambda a: jnp.roll(a[::-1], 1, axis=0)
    kpad = 48
    zt = jnp.pad(jnp.concatenate([z, back(z)], axis=0).T, ((0, kpad - FILTER_EMB), (0, 0)))
    tt = jnp.concatenate([t, back(t)], axis=0).T
    w1t = jnp.pad(w_f1.T, ((0, 0), (0, kpad - FILTER_EMB)))
    w3 = w_f3.reshape(-1, 2, 2, D_HYENA)
    w3f = jnp.transpose(w3[:, :, 0], (1, 2, 0)).reshape(2 * D_HYENA, -1)
    w3b = jnp.transpose(w3[:, :, 1], (1, 2, 0)).reshape(2 * D_HYENA, -1)
    max_decay = math.log(DECAY_TARGET) / FAST_DECAY_PCT
    min_decay = math.log(DECAY_TARGET) / SLOW_DECAY_PCT
    deltas = jnp.linspace(min_decay, max_decay, D_HYENA, dtype=F32)
    ad = jnp.tile(jnp.abs(deltas), 2)[:, None]
    col = lambda v: v[:, None]
    R = 128
    n_rows = 2 * D_HYENA
    full = lambda a: pl.BlockSpec(a.shape, lambda i: (0,) * a.ndim)
    rows = lambda a: pl.BlockSpec((R, a.shape[1]), lambda i: (i, 0))
    args = (zt, w1t, col(b_f1), col(freq1), w_f2.T, col(b_f2), col(freq2))
    return pl.pallas_call(
        _filtgen_body,
        grid=(n_rows // R,),
        in_specs=[full(a) for a in args] + [rows(w3f), rows(w3b), rows(ad), full(tt)],
        out_specs=pl.BlockSpec((R, 2 * L // LANES, LANES), lambda i: (i, 0, 0)),
        out_shape=jax.ShapeDtypeStruct((n_rows, 2 * L // LANES, LANES), F32),
        scratch_shapes=[pltpu.VMEM((w_f2.shape[1], 2 * L), F32)],
        compiler_params=_cparams(("arbitrary",)),
        name="filter_gen",
    )(*args, w3f, w3b, ad, tt)


def _route_lanes(lg):
    neg = -1e30
    lane = lax.broadcasted_iota(jnp.int32, lg.shape, 1)
    gmask = lane < N_GROUPS
    gl = jnp.where(gmask, lg, neg)
    gm = jnp.max(gl, axis=-1, keepdims=True)
    gsum = jnp.sum(jnp.where(gmask, jnp.exp(gl - gm), 0.0), axis=-1, keepdims=True)
    g_top = 1.0 / gsum
    g_sel = jnp.min(jnp.where(gl == gm, lane, LANES), axis=-1, keepdims=True)
    lo = N_GROUPS + EXPERTS_PER_GROUP * g_sel
    el = jnp.where((lane >= lo) & (lane < lo + EXPERTS_PER_GROUP), lg, neg)
    m1 = jnp.max(el, axis=-1, keepdims=True)
    i1 = jnp.min(jnp.where(el == m1, lane, LANES), axis=-1, keepdims=True)
    el2 = jnp.where(lane == i1, neg, el)
    m2 = jnp.max(el2, axis=-1, keepdims=True)
    i2 = jnp.min(jnp.where(el2 == m2, lane, LANES), axis=-1, keepdims=True)
    d = jnp.exp(m2 - m1)
    p1 = 1.0 / (1.0 + d)
    p2 = d / (1.0 + d)
    e1 = (i1 - N_GROUPS).astype(F32)
    e2 = (i2 - N_GROUPS).astype(F32)
    return jnp.where(lane == 0, e1, jnp.where(lane == 1, e2, jnp.where(lane == 2, g_top * p1,
                     jnp.where(lane == 3, g_top * p2, 0.0))))


def _pack_bf16_halves(a):
    w = a.shape[1] // 2
    bits = pltpu.bitcast(a.astype(BF16).astype(F32), jnp.uint32)
    return (bits[:, :w] >> 16) | (bits[:, w:] & jnp.uint32(0xFFFF0000))


def _unpack_bf16_halves(wd):
    lo = pltpu.bitcast(wd << 16, F32)
    hi = pltpu.bitcast(wd & jnp.uint32(0xFFFF0000), F32)
    return jnp.concatenate([lo, hi], axis=1)


def _store_row_tiles(ref, packed):
    for j in range(ROW_CHUNKS):
        ref[:, j, :] = packed[:, j * LANES:(j + 1) * LANES]


def _load_row_tiles(ref):
    return jnp.concatenate([ref[:, j, :] for j in range(ROW_CHUNKS)], axis=1)


def _outproj_body(ya_ref, yh_ref, x_ref, ga_ref, gh_ref, wo_ref, bd_ref, gm_ref, wrh_ref, wrl_ref, brt_ref,
                  x1_ref, h2_ref, rt_ref, rtt_ref):
    ya = ya_ref[...]
    yan = ya * lax.rsqrt(_group_sumsq(ya, bd_ref[...]) * (1.0 / HEAD_DIM) + EPS) * ga_ref[...]
    yh = yh_ref[...]
    tm = yh.shape[1]
    yh3 = yh.reshape(D_HYENA // HYENA_HEAD, HYENA_HEAD, tm)
    ms = jnp.mean(yh3 * yh3, axis=1, keepdims=True)
    yhn = (yh3 * lax.rsqrt(ms + EPS)).reshape(D_HYENA, tm) * gh_ref[...]
    mix = (jnp.dot(yan.astype(BF16), wo_ref[:D_ATTN, :], preferred_element_type=F32)
           + jnp.dot(yhn.T.astype(BF16), wo_ref[D_ATTN:, :], preferred_element_type=F32))
    x1 = x_ref[...] + mix
    x1_ref[...] = x1
    h2 = x1 * lax.rsqrt(jnp.mean(x1 * x1, axis=-1, keepdims=True) + EPS) * gm_ref[...]
    _store_row_tiles(h2_ref, _pack_bf16_halves(h2))
    hi = h2.astype(BF16)
    lo = (h2 - hi.astype(F32)).astype(BF16)
    lg = (jnp.dot(hi, wrh_ref[...], preferred_element_type=F32)
          + jnp.dot(lo, wrh_ref[...], preferred_element_type=F32)
          + jnp.dot(hi, wrl_ref[...], preferred_element_type=F32)) + brt_ref[...]
    route = _route_lanes(lg)
    rt_ref[...] = route
    rtt_ref[...] = route.T[:8]


def _outproj(ya, yht, x, ga, gh, wo, bd, gm, wrh, wrl, brt):
    B, S, D = x.shape
    tm = TM_PROJ
    full = lambda a: pl.BlockSpec(a.shape, lambda b, i: (0,) * a.ndim)
    return pl.pallas_call(
        _outproj_body,
        grid=(B, S // tm),
        in_specs=[
            pl.BlockSpec((None, tm, D_ATTN), lambda b, i: (b, i, 0)),
            pl.BlockSpec((None, D_HYENA, tm), lambda b, i: (b, 0, i)),
            pl.BlockSpec((None, tm, D), lambda b, i: (b, i, 0)),
            full(ga), full(gh), full(wo), full(bd), full(gm), full(wrh), full(wrl), full(brt),
        ],
        out_specs=[
            pl.BlockSpec((None, tm, D), lambda b, i: (b, i, 0)),
            pl.BlockSpec((None, tm, ROW_CHUNKS, LANES), lambda b, i: (b, i, 0, 0)),
            pl.BlockSpec((None, tm, LANES), lambda b, i: (b, i, 0)),
            pl.BlockSpec((None, 8, tm), lambda b, i: (b, 0, i)),
        ],
        out_shape=[
            jax.ShapeDtypeStruct((B, S, D), F32),
            jax.ShapeDtypeStruct((B, S, ROW_CHUNKS, LANES), jnp.uint32),
            jax.ShapeDtypeStruct((B, S, LANES), F32),
            jax.ShapeDtypeStruct((B, 8, S), F32),
        ],
        compiler_params=_cparams(("parallel", "parallel")),
        name="outproj",
    )(ya, yht, x, ga, gh, wo, bd, gm, wrh, wrl, brt)


def _moe_body(be_ref, ra_ref, h2_hbm, wg_ref, wu_ref, wd_ref, y_hbm,
              wg_s, wu_s, wd_s, xbuf, ybuf, sem_in, sem_out, *, n_tok):
    i = pl.program_id(0)
    nb = pl.num_programs(0)
    T = xbuf.shape[1]
    slot = i % 2

    def issue_gathers(blk, sl):
        for r in range(T):
            tok = ra_ref[blk * T + r] & (n_tok - 1)
            pltpu.make_async_copy(h2_hbm.at[tok], xbuf.at[sl, r], sem_in.at[sl]).start()

    def issue_scatters(blk, sl):
        for r in range(T):
            pltpu.make_async_copy(ybuf.at[sl, r], y_hbm.at[ra_ref[blk * T + r]], sem_out.at[sl]).start()

    def block_in_wait(sl):
        pltpu.make_async_copy(h2_hbm.at[pl.ds(0, T)], xbuf.at[sl], sem_in.at[sl]).wait()

    def block_out_wait(sl):
        pltpu.make_async_copy(ybuf.at[sl], y_hbm.at[pl.ds(0, T)], sem_out.at[sl]).wait()

    @pl.when(i == 0)
    def _():
        issue_gathers(0, 0)

    for sl in range(2):
        @pl.when((i + 1 < nb) & (slot == sl))
        def _():
            issue_gathers(i + 1, 1 - sl)

    prev = be_ref[jnp.maximum(i - 1, 0)]

    @pl.when((i == 0) | (be_ref[i] != prev))
    def _():
        wg_s[...] = wg_ref[...].astype(BF16)
        wu_s[...] = wu_ref[...].astype(BF16)
        wd_s[...] = wd_ref[...].astype(BF16)

    block_in_wait(slot)
    x = _unpack_bf16_halves(_load_row_tiles(xbuf.at[slot])).astype(BF16)
    a = jnp.dot(x, wg_s[...], preferred_element_type=F32)
    b = jnp.dot(x, wu_s[...], preferred_element_type=F32)
    hmid = (a * jax.nn.sigmoid(a)) * b
    y = _pack_bf16_halves(jnp.dot(hmid.astype(BF16), wd_s[...], preferred_element_type=F32))

    @pl.when(i >= 2)
    def _():
        block_out_wait(slot)

    _store_row_tiles(ybuf.at[slot], y)
    for sl in range(2):
        @pl.when(slot == sl)
        def _():
            issue_scatters(i, sl)

    @pl.when(i == nb - 1)
    def _():
        block_out_wait(slot)

        @pl.when(nb >= 2)
        def _():
            block_out_wait(1 - slot)


def _moe_experts(block_e, row_a, h2p, w_gate, w_up, w_down, n_out_rows):
    n_tok = h2p.shape[0]
    row = h2p.shape[1:]
    D = w_gate.shape[1]
    T = TB_MOE
    assert n_tok & (n_tok - 1) == 0
    grid_spec = pltpu.PrefetchScalarGridSpec(
        num_scalar_prefetch=2,
        grid=(row_a.shape[0] // T,),
        in_specs=[
            pl.BlockSpec(memory_space=pl.ANY),
            pl.BlockSpec((None, D, D_EXPERT), lambda i, be, ra: (be[i], 0, 0)),
            pl.BlockSpec((None, D, D_EXPERT), lambda i, be, ra: (be[i], 0, 0)),
            pl.BlockSpec((None, D_EXPERT, D), lambda i, be, ra: (be[i], 0, 0)),
        ],
        out_specs=pl.BlockSpec(memory_space=pl.ANY),
        scratch_shapes=[
            pltpu.VMEM((D, D_EXPERT), BF16), pltpu.VMEM((D, D_EXPERT), BF16), pltpu.VMEM((D_EXPERT, D), BF16),
            pltpu.VMEM((2, T) + row, jnp.uint32), pltpu.VMEM((2, T) + row, jnp.uint32),
            pltpu.SemaphoreType.DMA((2,)), pltpu.SemaphoreType.DMA((2,)),
        ],
    )
    return pl.pallas_call(
        functools.partial(_moe_body, n_tok=n_tok),
        grid_spec=grid_spec,
        out_shape=jax.ShapeDtypeStruct((n_out_rows,) + row, jnp.uint32),
        compiler_params=_cparams(("arbitrary",)),
        name="moe_experts",
    )(block_e, row_a, h2p, w_gate, w_up, w_down)


def _dispatch(e_flat, N):
    T = TB_MOE
    NK = N * TOP_K
    experts = jnp.arange(N_EXPERTS, dtype=jnp.int32)
    order = jnp.argsort(e_flat).astype(jnp.int32)
    onehot = (e_flat[:, None] == experts[None]).astype(jnp.int32)
    counts = jnp.sum(onehot, axis=0)
    ends = jnp.cumsum(counts)
    starts = ends - counts
    padded = (counts + T - 1) // T * T
    pends = jnp.cumsum(padded)
    pstarts = pends - padded
    n_rows = -(-(NK + N_EXPERTS * (T - 1)) // T) * T
    n_blocks = n_rows // T
    blk_start = jnp.arange(n_blocks, dtype=jnp.int32) * T
    block_e = jnp.clip(jnp.sum((pends[None, :] <= blk_start[:, None]).astype(jnp.int32), axis=1),
                       0, N_EXPERTS - 1)
    oh_b = (block_e[:, None] == experts[None]).astype(jnp.int32)
    base = jnp.sum(oh_b * (starts - pstarts)[None], axis=1) + blk_start
    end_b = jnp.sum(oh_b * ends[None], axis=1)
    lane = jnp.arange(T, dtype=jnp.int32)[None]
    src = base[:, None] + lane
    pad_id = NK + blk_start[:, None] + lane - end_b[:, None]
    row_a = jnp.where(src < end_b[:, None], order[jnp.clip(src, 0, NK - 1)], pad_id)
    return block_e.astype(jnp.int32), row_a.reshape(n_rows).astype(jnp.int32)


def _final_body(x1_ref, y0_ref, y1_ref, rt_ref, p_ref, gp_ref, wg_ref, bg_ref, wp_ref, gf_ref, o_ref):
    w0 = rt_ref[:, 2:3]
    w1 = rt_ref[:, 3:4]
    y0 = _unpack_bf16_halves(_load_row_tiles(y0_ref))
    y1 = _unpack_bf16_halves(_load_row_tiles(y1_ref))
    x2 = x1_ref[...] + (y0 * w0 + y1 * w1)
    hp = x2 * lax.rsqrt(jnp.mean(x2 * x2, axis=-1, keepdims=True) + EPS) * gp_ref[...]
    gate = jax.nn.sigmoid(jnp.dot(hp.astype(BF16), wg_ref[...], preferred_element_type=F32) + bg_ref[...])
    pe = jnp.dot(p_ref[...].astype(BF16), wp_ref[...], preferred_element_type=F32)
    x3 = x2 + pe * gate
    o_ref[...] = x3 * lax.rsqrt(jnp.mean(x3 * x3, axis=-1, keepdims=True) + EPS) * gf_ref[...]


def _final(x1, y, route, p, gp, wg, bg, wp, gf):
    N, D = x1.shape
    tm = TM_PROJ
    row = lambda w: pl.BlockSpec((tm, w), lambda i: (i, 0))
    full = lambda a: pl.BlockSpec(a.shape, lambda i: (0,) * a.ndim)
    y0, y1 = y, y
    return pl.pallas_call(
        _final_body,
        grid=(N // tm,),
        in_specs=[row(D), pl.BlockSpec((tm, ROW_CHUNKS, LANES), lambda i: (i, 0, 0)),
                  pl.BlockSpec((tm, ROW_CHUNKS, LANES), lambda i: (i + N // tm, 0, 0)),
                  row(LANES), row(p.shape[1]),
                  full(gp), full(wg), full(bg), full(wp), full(gf)],
        out_specs=row(D),
        out_shape=jax.ShapeDtypeStruct((N, D), F32),
        compiler_params=_cparams(("parallel",)),
        name="ple_final",
    )(x1, y0, y1, route, p, gp, wg, bg, wp, gf)


def kernel(x, p, g_mix, w_in, q_gain, k_gain, conv_w, conv_b, w_f1, b_f1, freq1, w_f2, b_f2, freq2, w_f3, filt_bias, g_attn_out, g_hyena_out, w_out, g_moe, w_group, b_group, w_router, b_router, w_gate, w_up, w_down, g_ple, w_ple_gate, b_ple_gate, w_ple, g_final):
    B, S, D = x.shape
    N = B * S
    assert p.shape[0] == 1 and S == (FFT_N1 // 2) * FFT_N2 and B % 2 == 0
    i = 0
    cst = _dft_constants()
    cos, sin = _rope_tables(S)
    bd = _block_diag_ones(D_ATTN, HEAD_DIM)

    n_qkv = D_ATTN + 2 * D_KV
    wqkv = w_in[i][:, :n_qkv].astype(BF16)
    wut = w_in[i][:, n_qkv:].T.astype(BF16)
    q, kw, vw, ut = _inproj(x, g_mix[i][None], wqkv, wut, bd,
                            jnp.tile(q_gain[i], N_HEADS)[None], jnp.tile(k_gain[i], N_KV_HEADS)[None], cos, sin)

    ya = _attention(q, kw, vw)

    circ = _filter_gen(S, w_f1[i], b_f1[i], freq1[i], w_f2[i], b_f2[i], freq2[i], w_f3[i])
    hspec = _filter_fft(circ, cst)
    hspec = hspec.reshape(2, D_HYENA, FFT_N1, 2 * LANES)
    du = ut.shape[1]
    u4 = ut.reshape(B, du, S // LANES, LANES)
    par_u = jnp.broadcast_to(jnp.concatenate([conv_w[i], conv_b[i][None]], 0)[:, :, None], (4, du, LANES))
    fb = jnp.broadcast_to(filt_bias[i][:, :, None], (2, D_HYENA, LANES))
    yht = _hyena(u4, par_u, fb, hspec, cst)

    wrt = jnp.zeros((D, LANES), F32).at[:, :N_GROUPS].set(w_group[i]).at[:, N_GROUPS:N_GROUPS + N_EXPERTS].set(w_router[i])
    brt = jnp.zeros((1, LANES), F32).at[0, :N_GROUPS].set(b_group[i]).at[0, N_GROUPS:N_GROUPS + N_EXPERTS].set(b_router[i])
    wrh = wrt.astype(BF16)
    wrl = (wrt - wrh.astype(F32)).astype(BF16)
    x1, h2, route, route_t = _outproj(ya, yht, x, g_attn_out[i][None], g_hyena_out[i][:, None],
                                      w_out[i].astype(BF16), bd, g_moe[i][None], wrh, wrl, brt)

    e_flat = jnp.transpose(route_t[:, :TOP_K], (1, 0, 2)).reshape(TOP_K * N).astype(jnp.int32)
    block_e, row_a = _dispatch(e_flat, N)
    y = _moe_experts(block_e, row_a, h2.reshape(N, ROW_CHUNKS, LANES), w_gate[i], w_up[i], w_down[i],
                     row_a.shape[0])

    out = _final(x1.reshape(N, D), y, route.reshape(N, LANES), p[i].reshape(N, -1), g_ple[i][None],
                 w_ple_gate[i].astype(BF16), b_ple_gate[i][None], w_ple[i].astype(BF16), g_final[None])
    return out.reshape(B, S, D)
```

```python
import functools
import math

import numpy as np
import jax
import jax.numpy as jnp
from jax import lax
from jax.experimental import pallas as pl
from jax.experimental.pallas import tpu as pltpu

F32 = jnp.float32
BF16 = jnp.bfloat16

D_MODEL = 1024
EPS = 1e-6
GRID_W = 64
N_HEADS = 8
N_KV_HEADS = 2
HEAD_DIM = 64
D_ATTN = N_HEADS * HEAD_DIM
D_KV = N_KV_HEADS * HEAD_DIM
ROPE_THETA = 10000.0
D_HYENA = 512
HYENA_HEAD = 64
FILTER_EMB = 33
FAST_DECAY_PCT = 0.3
SLOW_DECAY_PCT = 1.5
DECAY_TARGET = 1e-2
N_GROUPS = 4
EXPERTS_PER_GROUP = 8
N_EXPERTS = N_GROUPS * EXPERTS_PER_GROUP
TOP_K = 2
D_EXPERT = 512

LANES = 128
FFT_N1 = 64
FFT_N2 = 128
VMEM_LIMIT = 56 * 1024 * 1024

TM_PROJ = 512
TQ_ATTN = 256
C_HY = 32
ROW_CHUNKS = D_MODEL // 2 // LANES
SEQ_UNROLL = 8
TB_MOE = 256


def _cparams(sem):
    return pltpu.CompilerParams(dimension_semantics=sem, vmem_limit_bytes=VMEM_LIMIT)


def _rope_tables(S):
    half = HEAD_DIM // 2
    t = jnp.arange(S, dtype=F32)
    r_idx = jnp.floor(t / GRID_W)
    c_idx = t - r_idx * GRID_W
    inv = ROPE_THETA ** (-jnp.arange(0, half, 2, dtype=F32) / half)
    ang_r = r_idx[:, None] * inv[None]
    ang_c = c_idx[:, None] * inv[None]
    cos_h = jnp.concatenate([jnp.cos(ang_r), jnp.cos(ang_r), jnp.cos(ang_c), jnp.cos(ang_c)], axis=-1)
    sin_h = jnp.concatenate([-jnp.sin(ang_r), jnp.sin(ang_r), -jnp.sin(ang_c), jnp.sin(ang_c)], axis=-1)
    return jnp.tile(cos_h, (1, 2)), jnp.tile(sin_h, (1, 2))


def _dft_constants():
    n1, n2 = FFT_N1, FFT_N2
    n = n1 * n2
    a = np.arange(n1)
    ang = 2.0 * np.pi * np.outer(a, a) / n1
    far, fai = np.cos(ang), -np.sin(ang)
    hlf = n1 // 2
    ma = np.block([[far[:, :hlf], -fai[:, :hlf]], [fai[:, :hlf], far[:, :hlf]]])
    maf = np.concatenate([far, fai], axis=0)
    b = np.arange(n2)
    angt = 2.0 * np.pi * np.outer(a, b) / n
    tw = np.concatenate([np.cos(angt), -np.sin(angt)], axis=1)
    angb = 2.0 * np.pi * np.outer(b, b) / n2
    fbr, fbi = np.cos(angb), -np.sin(angb)
    g = np.block([[fbr, fbi], [-fbi, fbr]])
    ginv = np.block([[fbr, -fbi], [fbi, fbr]])
    minv_r = np.concatenate([far[:hlf], -fai[:hlf]], axis=0) / n
    minv_i = np.concatenate([fai[:hlf], far[:hlf]], axis=0) / n
    f = lambda m: jnp.asarray(m.astype(np.float32))
    return dict(ma=f(ma), maf=f(maf), tw=f(tw), g=f(g), ginv=f(ginv), minv_r=f(minv_r), minv_i=f(minv_i))


def _block_diag_ones(width, group):
    i = np.arange(width) // group
    return jnp.asarray((i[:, None] == i[None, :]).astype(np.float32)).astype(BF16)


def _group_sumsq(a, bd):
    sq = a * a
    hi = sq.astype(BF16)
    lo = (sq - hi.astype(F32)).astype(BF16)
    return (jnp.dot(hi, bd, preferred_element_type=F32) + jnp.dot(lo, bd, preferred_element_type=F32))


def _head_norm_rope(a, gain, bd, cos, sin):
    width = a.shape[-1]
    n = a * lax.rsqrt(_group_sumsq(a, bd) * (1.0 / HEAD_DIM) + EPS) * gain
    rep = width // LANES
    if rep > 1:
        cos = jnp.concatenate([cos] * rep, axis=-1)
        sin = jnp.concatenate([sin] * rep, axis=-1)
    fwd = pltpu.roll(n, width - 16, 1)
    bwd = pltpu.roll(n, 16, 1)
    lane = lax.broadcasted_iota(jnp.int32, n.shape, 1)
    sw = jnp.where((lane % 32) < 16, fwd, bwd)
    return n * cos + sw * sin


def _inproj_body(x_ref, g_ref, wqkv_ref, wu_ref, bd_ref, qg_ref, kg_ref, cos_ref, sin_ref,
                 q_ref, kw_ref, vw_ref, ut_ref):
    x = x_ref[...]
    h = x * lax.rsqrt(jnp.mean(x * x, axis=-1, keepdims=True) + EPS) * g_ref[...]
    hb = h.astype(BF16)
    qkv = jnp.dot(hb, wqkv_ref[...], preferred_element_type=F32)
    cos = cos_ref[...]
    sin = sin_ref[...]
    bd = bd_ref[...]
    q = _head_norm_rope(qkv[:, :D_ATTN], qg_ref[...], bd, cos, sin)
    q_ref[...] = (q * (HEAD_DIM ** -0.5 * math.log2(math.e))).astype(BF16)
    k = _head_norm_rope(qkv[:, D_ATTN:D_ATTN + D_KV], kg_ref[...], bd[:D_KV, :D_KV], cos, sin)
    kt = k.T.astype(BF16)
    zero = jnp.zeros((HEAD_DIM, kt.shape[1]), BF16)
    for h in range(N_KV_HEADS):
        kh = kt[h * HEAD_DIM:(h + 1) * HEAD_DIM]
        kw_ref[h, 0, :HEAD_DIM] = kh
        kw_ref[h, 0, HEAD_DIM:] = zero
        kw_ref[h, 1, :HEAD_DIM] = zero
        kw_ref[h, 1, HEAD_DIM:] = kh
    v = qkv[:, D_ATTN + D_KV:]
    vr = pltpu.roll(v, HEAD_DIM, 1)
    first = lax.broadcasted_iota(jnp.int32, v.shape, 1) < HEAD_DIM
    vw_ref[0, 0] = jnp.where(first, v, 1.0).astype(BF16)
    vw_ref[0, 1] = jnp.where(first, 1.0, vr).astype(BF16)
    vw_ref[1, 0] = jnp.where(first, vr, 1.0).astype(BF16)
    vw_ref[1, 1] = jnp.where(first, 1.0, v).astype(BF16)
    ut_ref[...] = lax.dot_general(wu_ref[...], hb, (((1,), (1,)), ((), ())),
                                  preferred_element_type=F32)


def _inproj(x, g_mix, wqkv, wut, bd, qg, kg, cos, sin):
    B, S, D = x.shape
    tm = TM_PROJ
    du = wut.shape[0]
    full = lambda shape: pl.BlockSpec(shape, lambda b, i: (0,) * len(shape))
    return pl.pallas_call(
        _inproj_body,
        grid=(B, S // tm),
        in_specs=[
            pl.BlockSpec((None, tm, D), lambda b, i: (b, i, 0)),
            full((1, D)), full(wqkv.shape), full(wut.shape), full(bd.shape),
            full((1, D_ATTN)), full((1, D_KV)),
            pl.BlockSpec((tm, LANES), lambda b, i: (i, 0)),
            pl.BlockSpec((tm, LANES), lambda b, i: (i, 0)),
        ],
        out_specs=[
            pl.BlockSpec((None, tm, D_ATTN), lambda b, i: (b, i, 0)),
            pl.BlockSpec((None, N_KV_HEADS, 2, LANES, tm), lambda b, i: (b, 0, 0, 0, i)),
            pl.BlockSpec((None, N_KV_HEADS, 2, tm, LANES), lambda b, i: (b, 0, 0, i, 0)),
            pl.BlockSpec((None, du, tm), lambda b, i: (b, 0, i)),
        ],
        out_shape=[
            jax.ShapeDtypeStruct((B, S, D_ATTN), BF16),
            jax.ShapeDtypeStruct((B, N_KV_HEADS, 2, LANES, S), BF16),
            jax.ShapeDtypeStruct((B, N_KV_HEADS, 2, S, LANES), BF16),
            jax.ShapeDtypeStruct((B, du, S), F32),
        ],
        compiler_params=_cparams(("parallel", "parallel")),
        name="inproj",
    )(x, g_mix, wqkv, wut, bd, qg, kg, cos, sin)


def _attn_body(q_ref, kw_ref, vw_ref, o_ref):

    def one_head(q, kw, vw):
        s = jnp.dot(q, kw, preferred_element_type=F32)
        m = jnp.max(s, axis=-1, keepdims=True)
        p = jnp.exp2(s - m).astype(BF16)
        return jnp.dot(p, vw, preferred_element_type=F32)

    for pair in range(D_ATTN // LANES):
        h = pair // (N_HEADS // N_KV_HEADS // 2)
        q = q_ref[:, pair * LANES:(pair + 1) * LANES]
        oe = one_head(q, kw_ref[h, 0], vw_ref[h, 0])
        oo = one_head(q, kw_ref[h, 1], vw_ref[h, 1])
        first = lax.broadcasted_iota(jnp.int32, oe.shape, 1) < HEAD_DIM
        num = jnp.where(first, oe, oo)
        den = jnp.where(first, pltpu.roll(oe, HEAD_DIM, 1), pltpu.roll(oo, HEAD_DIM, 1))
        o_ref[:, pair * LANES:(pair + 1) * LANES] = num / den


def _attention(q, kw, vw):
    B, S, _ = q.shape
    tq = TQ_ATTN
    return pl.pallas_call(
        _attn_body,
        grid=(B, S // tq),
        in_specs=[
            pl.BlockSpec((None, tq, D_ATTN), lambda b, i: (b, i, 0)),
            pl.BlockSpec((None, N_KV_HEADS, 2, LANES, S), lambda b, i: (b, 0, 0, 0, 0)),
            pl.BlockSpec((None, N_KV_HEADS, 2, S, LANES), lambda b, i: (b, 0, 0, 0, 0)),
        ],
        out_specs=pl.BlockSpec((None, tq, D_ATTN), lambda b, i: (b, i, 0)),
        out_shape=jax.ShapeDtypeStruct((B, S, D_ATTN), F32),
        compiler_params=_cparams(("parallel", "arbitrary")),
        name="attention",
    )(q, kw, vw)


def _fwd_twiddle_store(y, tw_ref, s1_ref, row0):
    yr, yi = y[:FFT_N1], y[FFT_N1:]
    twr, twi = tw_ref[:, :LANES], tw_ref[:, LANES:]
    s1_ref[pl.ds(row0, FFT_N1), :LANES] = (yr * twr - yi * twi).astype(BF16)
    s1_ref[pl.ds(row0, FFT_N1), LANES:] = (yr * twi + yi * twr).astype(BF16)


def _filtfft_body(x_ref, maf_ref, tw_ref, g_ref, h_ref, s1_ref):
    C = x_ref.shape[0]

    def step_a(c, carry):
        y = jnp.dot(maf_ref[...], x_ref[c].astype(BF16), preferred_element_type=F32)
        _fwd_twiddle_store(y, tw_ref, s1_ref, pl.multiple_of(c * FFT_N1, FFT_N1))
        return carry

    lax.fori_loop(0, C, step_a, 0, unroll=SEQ_UNROLL)
    z = jnp.dot(s1_ref[...], g_ref[...], preferred_element_type=F32)
    h_ref[...] = z.reshape(C, FFT_N1, 2 * LANES)


def _filter_fft(circ, cst):
    n_seq = circ.shape[0]
    C = C_HY
    full = lambda a: pl.BlockSpec(a.shape, lambda i: (0,) * a.ndim)
    maf, tw, g = cst["maf"].astype(BF16), cst["tw"], cst["g"].astype(BF16)
    return pl.pallas_call(
        _filtfft_body,
        grid=(n_seq // C,),
        in_specs=[pl.BlockSpec((C, FFT_N1, FFT_N2), lambda i: (i, 0, 0)), full(maf), full(tw), full(g)],
        out_specs=pl.BlockSpec((C, FFT_N1, 2 * LANES), lambda i: (i, 0, 0)),
        out_shape=jax.ShapeDtypeStruct((n_seq, FFT_N1, 2 * LANES), F32),
        scratch_shapes=[pltpu.VMEM((C * FFT_N1, 2 * LANES), BF16)],
        compiler_params=_cparams(("parallel",)),
        name="filter_fft",
    )(circ, maf, tw, g)


def _short_conv(x, par_ref, c):
    rows, lanes = x.shape
    a_i = lax.broadcasted_iota(jnp.int32, x.shape, 0)
    b_i = lax.broadcasted_iota(jnp.int32, x.shape, 1)
    l1 = pltpu.roll(x, 1, 1)
    l2 = pltpu.roll(l1, 1, 0)
    prev = jnp.where(b_i == 0, l2, l1)
    prev = jnp.where((a_i == 0) & (b_i == 0), 0.0, prev)
    r1 = pltpu.roll(x, lanes - 1, 1)
    r2 = pltpu.roll(r1, rows - 1, 0)
    nxt = jnp.where(b_i == lanes - 1, r2, r1)
    nxt = jnp.where((a_i == rows - 1) & (b_i == lanes - 1), 0.0, nxt)
    w0 = par_ref[0, pl.ds(c, 1), :]
    w1 = par_ref[1, pl.ds(c, 1), :]
    w2 = par_ref[2, pl.ds(c, 1), :]
    cb = par_ref[3, pl.ds(c, 1), :]
    return cb + prev * w0 + x * w1 + nxt * w2


def _hyena_body(v_ref, x1_ref, x2_ref, pv_ref, p1_ref, p2_ref, fb_ref, h_ref,
                ma_ref, tw_ref, g_ref, ginv_ref, mir_ref, mii_ref,
                o_ref, s1_ref, s2_ref, vc_ref, z1_ref):
    C = v_ref.shape[1]
    half = FFT_N1 // 2

    def spectral(order):
        z = jnp.dot(s1_ref[...], g_ref[...], preferred_element_type=F32)
        hs = h_ref[order].reshape(C * FFT_N1, 2 * LANES)
        zr, zi = z[:, :LANES], z[:, LANES:]
        hr, hi = hs[:, :LANES], hs[:, LANES:]
        pb = jnp.concatenate([zr * hr - zi * hi, zr * hi + zi * hr], axis=1).astype(BF16)
        s2_ref[...] = jnp.dot(pb, ginv_ref[...], preferred_element_type=F32)

    def inv_a(c):
        row0 = pl.multiple_of(c * FFT_N1, FFT_N1)
        y = s2_ref[pl.ds(row0, FFT_N1), :]
        yr, yi = y[:, :LANES], y[:, LANES:]
        twr, twi = tw_ref[:, :LANES], tw_ref[:, LANES:]
        ur = (yr * twr + yi * twi).astype(BF16)
        ui = (yi * twr - yr * twi).astype(BF16)
        out = (jnp.dot(mir_ref[...], ur, preferred_element_type=F32)
               + jnp.dot(mii_ref[...], ui, preferred_element_type=F32))
        return out[:half], out[half:]

    def fwd_a(c, xr, xi):
        xs = jnp.concatenate([xr, xi], axis=0).astype(BF16)
        y = jnp.dot(ma_ref[...], xs, preferred_element_type=F32)
        _fwd_twiddle_store(y, tw_ref, s1_ref, pl.multiple_of(c * FFT_N1, FFT_N1))

    def pass1_a(c, carry):
        vr = _short_conv(v_ref[0, c], pv_ref, c)
        vi = _short_conv(v_ref[1, c], pv_ref, c)
        vc_ref[0, c] = vr
        vc_ref[1, c] = vi
        fwd_a(c, vr, vi)
        return carry

    def pass1_b(c, carry):
        cr, ci = inv_a(c)
        bias = fb_ref[0, pl.ds(c, 1), :]
        zr = _short_conv(x1_ref[0, c], p1_ref, c) * (cr + bias * vc_ref[0, c])
        zi = _short_conv(x1_ref[1, c], p1_ref, c) * (ci + bias * vc_ref[1, c])
        z1_ref[0, c] = zr
        z1_ref[1, c] = zi
        fwd_a(c, zr, zi)
        return carry

    def pass2_b(c, carry):
        cr, ci = inv_a(c)
        bias = fb_ref[1, pl.ds(c, 1), :]
        vc_ref[0, c] = _short_conv(x2_ref[0, c], p2_ref, c) * (cr + bias * z1_ref[0, c])
        vc_ref[1, c] = _short_conv(x2_ref[1, c], p2_ref, c) * (ci + bias * z1_ref[1, c])
        return carry

    lax.fori_loop(0, C, pass1_a, 0, unroll=SEQ_UNROLL)
    spectral(0)
    lax.fori_loop(0, C, pass1_b, 0, unroll=SEQ_UNROLL)
    spectral(1)
    lax.fori_loop(0, C, pass2_b, 0, unroll=SEQ_UNROLL)
    for b2 in range(2):
        for a in range(vc_ref.shape[2]):
            o_ref[b2, :, a * LANES:(a + 1) * LANES] = vc_ref[b2, :, a, :]


def _hyena(u4, par_u, fb, hspec, cst):
    B = u4.shape[0]
    C = C_HY
    J = D_HYENA // C
    rows = u4.shape[2]
    full = lambda a: pl.BlockSpec(a.shape, lambda j, p: (0,) * a.ndim)
    ma, g, ginv = cst["ma"].astype(BF16), cst["g"].astype(BF16), cst["ginv"].astype(BF16)
    mir, mii = cst["minv_r"].astype(BF16), cst["minv_i"].astype(BF16)
    tw = cst["tw"]
    u_spec = lambda k: pl.BlockSpec((2, C, rows, LANES), lambda j, p, k=k: (p, j + k * J, 0, 0))
    par_spec = lambda k: pl.BlockSpec((4, C, LANES), lambda j, p, k=k: (0, j + k * J, 0))
    return pl.pallas_call(
        _hyena_body,
        grid=(J, B // 2),
        in_specs=[
            u_spec(0), u_spec(1), u_spec(2), par_spec(0), par_spec(1), par_spec(2),
            pl.BlockSpec((2, C, LANES), lambda j, p: (0, j, 0)),
            pl.BlockSpec((2, C, FFT_N1, 2 * LANES), lambda j, p: (0, j, 0, 0)),
            full(ma), full(tw), full(g), full(ginv), full(mir), full(mii),
        ],
        out_specs=pl.BlockSpec((2, C, rows * LANES), lambda j, p: (p, j, 0)),
        out_shape=jax.ShapeDtypeStruct((B, D_HYENA, rows * LANES), F32),
        scratch_shapes=[
            pltpu.VMEM((C * FFT_N1, 2 * LANES), BF16),
            pltpu.VMEM((C * FFT_N1, 2 * LANES), F32),
            pltpu.VMEM((2, C, rows, LANES), F32),
            pltpu.VMEM((2, C, rows, LANES), F32),
        ],
        compiler_params=_cparams(("parallel", "arbitrary")),
        name="hyena",
    )(u4, u4, u4, par_u, par_u, par_u, fb, hspec, ma, tw, g, ginv, mir, mii)


def _dot3(a, b):
    ah = a.astype(BF16)
    al = (a - ah.astype(F32)).astype(BF16)
    bh = b.astype(BF16)
    bl = (b - bh.astype(F32)).astype(BF16)
    return (jnp.dot(ah, bh, preferred_element_type=F32) + jnp.dot(al, bh, preferred_element_type=F32)
            + jnp.dot(ah, bl, preferred_element_type=F32))


def _filtgen_body(zt_ref, w1_ref, b1_ref, f1_ref, w2_ref, b2_ref, f2_ref, w3f_ref, w3b_ref, ad_ref, tt_ref,
                  o_ref, hid_ref):
    L = hid_ref.shape[1] // 2

    @pl.when(pl.program_id(0) == 0)
    def _():
        h1 = jnp.sin(f1_ref[...] * (_dot3(w1_ref[...], zt_ref[...]) + b1_ref[...]))
        hid_ref[...] = jnp.sin(f2_ref[...] * (_dot3(w2_ref[...], h1) + b2_ref[...]))

    ad = ad_ref[...]
    hf = _dot3(w3f_ref[...], hid_ref[:, :L]) * jnp.exp(-ad * tt_ref[:, :L])
    hb = _dot3(w3b_ref[...], hid_ref[:, L:]) * jnp.exp(-ad * tt_ref[:, L:])
    hf = hf / (jnp.sum(jnp.abs(hf), axis=-1, keepdims=True) + EPS)
    hb = hb / (jnp.sum(jnp.abs(hb), axis=-1, keepdims=True) + EPS)
    first = lax.broadcasted_iota(jnp.int32, hb.shape, 1) == 0
    cf = hf + jnp.where(first, hb, 0.0)
    cb = jnp.where(first, 0.0, hb)
    half = L // LANES
    for a in range(half):
        o_ref[:, a, :] = cf[:, a * LANES:(a + 1) * LANES]
        o_ref[:, half + a, :] = cb[:, a * LANES:(a + 1) * LANES]


def _filter_gen(L, w_f1, b_f1, freq1, w_f2, b_f2, freq2, w_f3):
    bands = (FILTER_EMB - 1) // 2
    t = jnp.linspace(0.0, 1.0, L, dtype=F32)[:, None]
    w = (2.0 * math.pi / L) * jnp.arange(L, dtype=F32)[:, None]
    f = jnp.linspace(1e-4, bands - 1, bands, dtype=F32)[None]
    zf = f * w
    z = jnp.concatenate([t, jnp.cos(zf), -jnp.sin(zf)], axis=-1)
    back = lambda a: jnp.roll(a[::-1], 1, axis=0)
    kpad = 48
    zt = jnp.pad(jnp.concatenate([z, back(z)], axis=0).T, ((0, kpad - FILTER_EMB), (0, 0)))
    tt = jnp.concatenate([t, back(t)], axis=0).T
    w1t = jnp.pad(w_f1.T, ((0, 0), (0, kpad - FILTER_EMB)))
    w3 = w_f3.reshape(-1, 2, 2, D_HYENA)
    w3f = jnp.transpose(w3[:, :, 0], (1, 2, 0)).reshape(2 * D_HYENA, -1)
    w3b = jnp.transpose(w3[:, :, 1], (1, 2, 0)).reshape(2 * D_HYENA, -1)
    max_decay = math.log(DECAY_TARGET) / FAST_DECAY_PCT
    min_decay = math.log(DECAY_TARGET) / SLOW_DECAY_PCT
    deltas = jnp.linspace(min_decay, max_decay, D_HYENA, dtype=F32)
    ad = jnp.tile(jnp.abs(deltas), 2)[:, None]
    col = lambda v: v[:, None]
    R = 128
    n_rows = 2 * D_HYENA
    full = lambda a: pl.BlockSpec(a.shape, lambda i: (0,) * a.ndim)
    rows = lambda a: pl.BlockSpec((R, a.shape[1]), lambda i: (i, 0))
    args = (zt, w1t, col(b_f1), col(freq1), w_f2.T, col(b_f2), col(freq2))
    return pl.pallas_call(
        _filtgen_body,
        grid=(n_rows // R,),
        in_specs=[full(a) for a in args] + [rows(w3f), rows(w3b), rows(ad), full(tt)],
        out_specs=pl.BlockSpec((R, 2 * L // LANES, LANES), lambda i: (i, 0, 0)),
        out_shape=jax.ShapeDtypeStruct((n_rows, 2 * L // LANES, LANES), F32),
        scratch_shapes=[pltpu.VMEM((w_f2.shape[1], 2 * L), F32)],
        compiler_params=_cparams(("arbitrary",)),
        name="filter_gen",
    )(*args, w3f, w3b, ad, tt)


def _route_lanes(lg):
    neg = -1e30
    lane = lax.broadcasted_iota(jnp.int32, lg.shape, 1)
    gmask = lane < N_GROUPS
    gl = jnp.where(gmask, lg, neg)
    gm = jnp.max(gl, axis=-1, keepdims=True)
    gsum = jnp.sum(jnp.where(gmask, jnp.exp(gl - gm), 0.0), axis=-1, keepdims=True)
    g_top = 1.0 / gsum
    g_sel = jnp.min(jnp.where(gl == gm, lane, LANES), axis=-1, keepdims=True)
    lo = N_GROUPS + EXPERTS_PER_GROUP * g_sel
    el = jnp.where((lane >= lo) & (lane < lo + EXPERTS_PER_GROUP), lg, neg)
    m1 = jnp.max(el, axis=-1, keepdims=True)
    i1 = jnp.min(jnp.where(el == m1, lane, LANES), axis=-1, keepdims=True)
    el2 = jnp.where(lane == i1, neg, el)
    m2 = jnp.max(el2, axis=-1, keepdims=True)
    i2 = jnp.min(jnp.where(el2 == m2, lane, LANES), axis=-1, keepdims=True)
    d = jnp.exp(m2 - m1)
    p1 = 1.0 / (1.0 + d)
    p2 = d / (1.0 + d)
    e1 = (i1 - N_GROUPS).astype(F32)
    e2 = (i2 - N_GROUPS).astype(F32)
    return jnp.where(lane == 0, e1, jnp.where(lane == 1, e2, jnp.where(lane == 2, g_top * p1,
                     jnp.where(lane == 3, g_top * p2, 0.0))))


def _pack_bf16_halves(a):
    w = a.shape[1] // 2
    bits = pltpu.bitcast(a.astype(BF16).astype(F32), jnp.uint32)
    return (bits[:, :w] >> 16) | (bits[:, w:] & jnp.uint32(0xFFFF0000))


def _unpack_bf16_halves(wd):
    lo = pltpu.bitcast(wd << 16, F32)
    hi = pltpu.bitcast(wd & jnp.uint32(0xFFFF0000), F32)
    return jnp.concatenate([lo, hi], axis=1)


def _store_row_tiles(ref, packed):
    for j in range(ROW_CHUNKS):
        ref[:, j, :] = packed[:, j * LANES:(j + 1) * LANES]


def _load_row_tiles(ref):
    return jnp.concatenate([ref[:, j, :] for j in range(ROW_CHUNKS)], axis=1)


def _outproj_body(ya_ref, yh_ref, x_ref, ga_ref, gh_ref, wo_ref, bd_ref, gm_ref, wrh_ref, wrl_ref, brt_ref,
                  x1_ref, h2_ref, rt_ref, rtt_ref):
    ya = ya_ref[...]
    yan = ya * lax.rsqrt(_group_sumsq(ya, bd_ref[...]) * (1.0 / HEAD_DIM) + EPS) * ga_ref[...]
    yh = yh_ref[...]
    tm = yh.shape[1]
    yh3 = yh.reshape(D_HYENA // HYENA_HEAD, HYENA_HEAD, tm)
    ms = jnp.mean(yh3 * yh3, axis=1, keepdims=True)
    yhn = (yh3 * lax.rsqrt(ms + EPS)).reshape(D_HYENA, tm) * gh_ref[...]
    mix = (jnp.dot(yan.astype(BF16), wo_ref[:D_ATTN, :], preferred_element_type=F32)
           + jnp.dot(yhn.T.astype(BF16), wo_ref[D_ATTN:, :], preferred_element_type=F32))
    x1 = x_ref[...] + mix
    x1_ref[...] = x1
    h2 = x1 * lax.rsqrt(jnp.mean(x1 * x1, axis=-1, keepdims=True) + EPS) * gm_ref[...]
    _store_row_tiles(h2_ref, _pack_bf16_halves(h2))
    hi = h2.astype(BF16)
    lo = (h2 - hi.astype(F32)).astype(BF16)
    lg = (jnp.dot(hi, wrh_ref[...], preferred_element_type=F32)
          + jnp.dot(lo, wrh_ref[...], preferred_element_type=F32)
          + jnp.dot(hi, wrl_ref[...], preferred_element_type=F32)) + brt_ref[...]
    route = _route_lanes(lg)
    rt_ref[...] = route
    rtt_ref[...] = route.T[:8]


def _outproj(ya, yht, x, ga, gh, wo, bd, gm, wrh, wrl, brt):
    B, S, D = x.shape
    tm = TM_PROJ
    full = lambda a: pl.BlockSpec(a.shape, lambda b, i: (0,) * a.ndim)
    return pl.pallas_call(
        _outproj_body,
        grid=(B, S // tm),
        in_specs=[
            pl.BlockSpec((None, tm, D_ATTN), lambda b, i: (b, i, 0)),
            pl.BlockSpec((None, D_HYENA, tm), lambda b, i: (b, 0, i)),
            pl.BlockSpec((None, tm, D), lambda b, i: (b, i, 0)),
            full(ga), full(gh), full(wo), full(bd), full(gm), full(wrh), full(wrl), full(brt),
        ],
        out_specs=[
            pl.BlockSpec((None, tm, D), lambda b, i: (b, i, 0)),
            pl.BlockSpec((None, tm, ROW_CHUNKS, LANES), lambda b, i: (b, i, 0, 0)),
            pl.BlockSpec((None, tm, LANES), lambda b, i: (b, i, 0)),
            pl.BlockSpec((None, 8, tm), lambda b, i: (b, 0, i)),
        ],
        out_shape=[
            jax.ShapeDtypeStruct((B, S, D), F32),
            jax.ShapeDtypeStruct((B, S, ROW_CHUNKS, LANES), jnp.uint32),
            jax.ShapeDtypeStruct((B, S, LANES), F32),
            jax.ShapeDtypeStruct((B, 8, S), F32),
        ],
        compiler_params=_cparams(("parallel", "parallel")),
        name="outproj",
    )(ya, yht, x, ga, gh, wo, bd, gm, wrh, wrl, brt)


def _moe_body(be_ref, ra_ref, h2_hbm, wg_ref, wu_ref, wd_ref, y_hbm,
              wg_s, wu_s, wd_s, xbuf, ybuf, sem_in, sem_out, *, n_tok, n_rows):
    i = pl.program_id(0)
    nb = pl.num_programs(0)
    T = xbuf.shape[1]
    slot = i % 2

    def issue_gathers(blk, sl):
        for r in range(T):
            tok = ra_ref[blk * T + r] & (n_tok - 1)
            pltpu.make_async_copy(h2_hbm.at[tok], xbuf.at[sl, r], sem_in.at[sl]).start()

    def issue_scatters(blk, sl, spare):
        for r in range(T):
            dst = jnp.where(spare, n_rows + r, ra_ref[blk * T + r])
            pltpu.make_async_copy(ybuf.at[sl, r], y_hbm.at[dst], sem_out.at[sl]).start()

    def block_in_wait(sl):
        pltpu.make_async_copy(h2_hbm.at[pl.ds(0, T)], xbuf.at[sl], sem_in.at[sl]).wait()

    def block_out_wait(sl):
        pltpu.make_async_copy(ybuf.at[sl], y_hbm.at[pl.ds(0, T)], sem_out.at[sl]).wait()

    @pl.when(i == 0)
    def _():
        ybuf[...] = jnp.zeros(ybuf.shape, ybuf.dtype)
        issue_gathers(0, 0)

    prev = be_ref[jnp.maximum(i - 1, 0)]

    @pl.when((i == 0) | (be_ref[i] != prev))
    def _():
        wg_s[...] = wg_ref[...].astype(BF16)
        wu_s[...] = wu_ref[...].astype(BF16)
        wd_s[...] = wd_ref[...].astype(BF16)

    block_in_wait(slot)
    x = _unpack_bf16_halves(_load_row_tiles(xbuf.at[slot])).astype(BF16)
    issue_gathers(jnp.minimum(i + 1, nb - 1), 1 - slot)
    issue_scatters(jnp.maximum(i - 1, 0), 1 - slot, i == 0)
    a = jnp.dot(x, wg_s[...], preferred_element_type=F32)
    b = jnp.dot(x, wu_s[...], preferred_element_type=F32)
    hmid = (a * jax.nn.sigmoid(a)) * b
    y = _pack_bf16_halves(jnp.dot(hmid.astype(BF16), wd_s[...], preferred_element_type=F32))

    @pl.when(i >= 1)
    def _():
        block_out_wait(slot)

    _store_row_tiles(ybuf.at[slot], y)

    @pl.when(i == nb - 1)
    def _():
        issue_scatters(i, slot, False)
        block_in_wait(1 - slot)
        block_out_wait(1 - slot)
        block_out_wait(slot)


def _moe_experts(block_e, row_a, h2p, w_gate, w_up, w_down):
    n_tok = h2p.shape[0]
    n_rows = row_a.shape[0]
    row = h2p.shape[1:]
    D = w_gate.shape[1]
    T = TB_MOE
    assert n_tok & (n_tok - 1) == 0
    grid_spec = pltpu.PrefetchScalarGridSpec(
        num_scalar_prefetch=2,
        grid=(row_a.shape[0] // T,),
        in_specs=[
            pl.BlockSpec(memory_space=pl.ANY),
            pl.BlockSpec((None, D, D_EXPERT), lambda i, be, ra: (be[i], 0, 0)),
            pl.BlockSpec((None, D, D_EXPERT), lambda i, be, ra: (be[i], 0, 0)),
            pl.BlockSpec((None, D_EXPERT, D), lambda i, be, ra: (be[i], 0, 0)),
        ],
        out_specs=pl.BlockSpec(memory_space=pl.ANY),
        scratch_shapes=[
            pltpu.VMEM((D, D_EXPERT), BF16), pltpu.VMEM((D, D_EXPERT), BF16), pltpu.VMEM((D_EXPERT, D), BF16),
            pltpu.VMEM((2, T) + row, jnp.uint32), pltpu.VMEM((2, T) + row, jnp.uint32),
            pltpu.SemaphoreType.DMA((2,)), pltpu.SemaphoreType.DMA((2,)),
        ],
    )
    return pl.pallas_call(
        functools.partial(_moe_body, n_tok=n_tok, n_rows=n_rows),
        grid_spec=grid_spec,
        out_shape=jax.ShapeDtypeStruct((n_rows + T,) + row, jnp.uint32),
        compiler_params=_cparams(("arbitrary",)),
        name="moe_experts",
    )(block_e, row_a, h2p, w_gate, w_up, w_down)


def _dispatch(e_flat, N):
    T = TB_MOE
    NK = N * TOP_K
    experts = jnp.arange(N_EXPERTS, dtype=jnp.int32)
    order = jnp.argsort(e_flat).astype(jnp.int32)
    onehot = (e_flat[:, None] == experts[None]).astype(jnp.int32)
    counts = jnp.sum(onehot, axis=0)
    ends = jnp.cumsum(counts)
    starts = ends - counts
    padded = (counts + T - 1) // T * T
    pends = jnp.cumsum(padded)
    pstarts = pends - padded
    n_rows = -(-(NK + N_EXPERTS * (T - 1)) // T) * T
    n_blocks = n_rows // T
    blk_start = jnp.arange(n_blocks, dtype=jnp.int32) * T
    block_e = jnp.clip(jnp.sum((pends[None, :] <= blk_start[:, None]).astype(jnp.int32), axis=1),
                       0, N_EXPERTS - 1)
    oh_b = (block_e[:, None] == experts[None]).astype(jnp.int32)
    base = jnp.sum(oh_b * (starts - pstarts)[None], axis=1) + blk_start
    end_b = jnp.sum(oh_b * ends[None], axis=1)
    lane = jnp.arange(T, dtype=jnp.int32)[None]
    src = base[:, None] + lane
    pad_id = NK + blk_start[:, None] + lane - end_b[:, None]
    row_a = jnp.where(src < end_b[:, None], order[jnp.clip(src, 0, NK - 1)], pad_id)
    return block_e.astype(jnp.int32), row_a.reshape(n_rows).astype(jnp.int32)


def _final_body(x1_ref, y0_ref, y1_ref, rt_ref, p_ref, gp_ref, wg_ref, bg_ref, wp_ref, gf_ref, o_ref):
    w0 = rt_ref[:, 2:3]
    w1 = rt_ref[:, 3:4]
    y0 = _unpack_bf16_halves(_load_row_tiles(y0_ref))
    y1 = _unpack_bf16_halves(_load_row_tiles(y1_ref))
    x2 = x1_ref[...] + (y0 * w0 + y1 * w1)
    hp = x2 * lax.rsqrt(jnp.mean(x2 * x2, axis=-1, keepdims=True) + EPS) * gp_ref[...]
    gate = jax.nn.sigmoid(jnp.dot(hp.astype(BF16), wg_ref[...], preferred_element_type=F32) + bg_ref[...])
    pe = jnp.dot(p_ref[...].astype(BF16), wp_ref[...], preferred_element_type=F32)
    x3 = x2 + pe * gate
    o_ref[...] = x3 * lax.rsqrt(jnp.mean(x3 * x3, axis=-1, keepdims=True) + EPS) * gf_ref[...]


def _final(x1, y, route, p, gp, wg, bg, wp, gf):
    N, D = x1.shape
    tm = TM_PROJ
    row = lambda w: pl.BlockSpec((tm, w), lambda i: (i, 0))
    full = lambda a: pl.BlockSpec(a.shape, lambda i: (0,) * a.ndim)
    y0, y1 = y, y
    return pl.pallas_call(
        _final_body,
        grid=(N // tm,),
        in_specs=[row(D), pl.BlockSpec((tm, ROW_CHUNKS, LANES), lambda i: (i, 0, 0)),
                  pl.BlockSpec((tm, ROW_CHUNKS, LANES), lambda i: (i + N // tm, 0, 0)),
                  row(LANES), row(p.shape[1]),
                  full(gp), full(wg), full(bg), full(wp), full(gf)],
        out_specs=row(D),
        out_shape=jax.ShapeDtypeStruct((N, D), F32),
        compiler_params=_cparams(("parallel",)),
        name="ple_final",
    )(x1, y0, y1, route, p, gp, wg, bg, wp, gf)


def kernel(x, p, g_mix, w_in, q_gain, k_gain, conv_w, conv_b, w_f1, b_f1, freq1, w_f2, b_f2, freq2, w_f3, filt_bias, g_attn_out, g_hyena_out, w_out, g_moe, w_group, b_group, w_router, b_router, w_gate, w_up, w_down, g_ple, w_ple_gate, b_ple_gate, w_ple, g_final):
    B, S, D = x.shape
    N = B * S
    assert p.shape[0] == 1 and S == (FFT_N1 // 2) * FFT_N2 and B % 2 == 0
    i = 0
    cst = _dft_constants()
    cos, sin = _rope_tables(S)
    bd = _block_diag_ones(D_ATTN, HEAD_DIM)

    n_qkv = D_ATTN + 2 * D_KV
    wqkv = w_in[i][:, :n_qkv].astype(BF16)
    wut = w_in[i][:, n_qkv:].T.astype(BF16)
    q, kw, vw, ut = _inproj(x, g_mix[i][None], wqkv, wut, bd,
                            jnp.tile(q_gain[i], N_HEADS)[None], jnp.tile(k_gain[i], N_KV_HEADS)[None], cos, sin)

    ya = _attention(q, kw, vw)

    circ = _filter_gen(S, w_f1[i], b_f1[i], freq1[i], w_f2[i], b_f2[i], freq2[i], w_f3[i])
    hspec = _filter_fft(circ, cst)
    hspec = hspec.reshape(2, D_HYENA, FFT_N1, 2 * LANES)
    du = ut.shape[1]
    u4 = ut.reshape(B, du, S // LANES, LANES)
    par_u = jnp.broadcast_to(jnp.concatenate([conv_w[i], conv_b[i][None]], 0)[:, :, None], (4, du, LANES))
    fb = jnp.broadcast_to(filt_bias[i][:, :, None], (2, D_HYENA, LANES))
    yht = _hyena(u4, par_u, fb, hspec, cst)

    wrt = jnp.zeros((D, LANES), F32).at[:, :N_GROUPS].set(w_group[i]).at[:, N_GROUPS:N_GROUPS + N_EXPERTS].set(w_router[i])
    brt = jnp.zeros((1, LANES), F32).at[0, :N_GROUPS].set(b_group[i]).at[0, N_GROUPS:N_GROUPS + N_EXPERTS].set(b_router[i])
    wrh = wrt.astype(BF16)
    wrl = (wrt - wrh.astype(F32)).astype(BF16)
    x1, h2, route, route_t = _outproj(ya, yht, x, g_attn_out[i][None], g_hyena_out[i][:, None],
                                      w_out[i].astype(BF16), bd, g_moe[i][None], wrh, wrl, brt)

    e_flat = jnp.transpose(route_t[:, :TOP_K], (1, 0, 2)).reshape(TOP_K * N).astype(jnp.int32)
    block_e, row_a = _dispatch(e_flat, N)
    y = _moe_experts(block_e, row_a, h2.reshape(N, ROW_CHUNKS, LANES), w_gate[i], w_up[i], w_down[i])

    out = _final(x1.reshape(N, D), y, route.reshape(N, LANES), p[i].reshape(N, -1), g_ple[i][None],
                 w_ple_gate[i].astype(BF16), b_ple_gate[i][None], w_ple[i].astype(BF16), g_final[None])
    return out.reshape(B, S, D)
```

```python
import functools
import math

import numpy as np
import jax
import jax.numpy as jnp
from jax import lax
from jax.experimental import pallas as pl
from jax.experimental.pallas import tpu as pltpu

F32 = jnp.float32
BF16 = jnp.bfloat16

D_MODEL = 1024
EPS = 1e-6
GRID_W = 64
N_HEADS = 8
N_KV_HEADS = 2
HEAD_DIM = 64
D_ATTN = N_HEADS * HEAD_DIM
D_KV = N_KV_HEADS * HEAD_DIM
ROPE_THETA = 10000.0
D_HYENA = 512
HYENA_HEAD = 64
FILTER_EMB = 33
FAST_DECAY_PCT = 0.3
SLOW_DECAY_PCT = 1.5
DECAY_TARGET = 1e-2
N_GROUPS = 4
EXPERTS_PER_GROUP = 8
N_EXPERTS = N_GROUPS * EXPERTS_PER_GROUP
TOP_K = 2
D_EXPERT = 512

LANES = 128
FFT_N1 = 64
FFT_N2 = 128
VMEM_LIMIT = 56 * 1024 * 1024

TM_PROJ = 512
TQ_ATTN = 256
C_HY = 32
ROW_CHUNKS = D_MODEL // 2 // LANES
SEQ_UNROLL = 8
TB_MOE = 256


def _cparams(sem):
    return pltpu.CompilerParams(dimension_semantics=sem, vmem_limit_bytes=VMEM_LIMIT)


def _rope_tables(S):
    half = HEAD_DIM // 2
    t = jnp.arange(S, dtype=F32)
    r_idx = jnp.floor(t / GRID_W)
    c_idx = t - r_idx * GRID_W
    inv = ROPE_THETA ** (-jnp.arange(0, half, 2, dtype=F32) / half)
    ang_r = r_idx[:, None] * inv[None]
    ang_c = c_idx[:, None] * inv[None]
    cos_h = jnp.concatenate([jnp.cos(ang_r), jnp.cos(ang_r), jnp.cos(ang_c), jnp.cos(ang_c)], axis=-1)
    sin_h = jnp.concatenate([-jnp.sin(ang_r), jnp.sin(ang_r), -jnp.sin(ang_c), jnp.sin(ang_c)], axis=-1)
    return jnp.tile(cos_h, (1, 2)), jnp.tile(sin_h, (1, 2))


def _dft_constants():
    n1, n2 = FFT_N1, FFT_N2
    n = n1 * n2
    a = np.arange(n1)
    ang = 2.0 * np.pi * np.outer(a, a) / n1
    far, fai = np.cos(ang), -np.sin(ang)
    hlf = n1 // 2
    ma = np.block([[far[:, :hlf], -fai[:, :hlf]], [fai[:, :hlf], far[:, :hlf]]])
    maf = np.concatenate([far, fai], axis=0)
    b = np.arange(n2)
    angt = 2.0 * np.pi * np.outer(a, b) / n
    tw = np.concatenate([np.cos(angt), -np.sin(angt)], axis=1)
    angb = 2.0 * np.pi * np.outer(b, b) / n2
    fbr, fbi = np.cos(angb), -np.sin(angb)
    g = np.block([[fbr, fbi], [-fbi, fbr]])
    ginv = np.block([[fbr, -fbi], [fbi, fbr]])
    minv_r = np.concatenate([far[:hlf], -fai[:hlf]], axis=0) / n
    minv_i = np.concatenate([fai[:hlf], far[:hlf]], axis=0) / n
    f = lambda m: jnp.asarray(m.astype(np.float32))
    return dict(ma=f(ma), maf=f(maf), tw=f(tw), g=f(g), ginv=f(ginv), minv_r=f(minv_r), minv_i=f(minv_i))


def _block_diag_ones(width, group):
    i = np.arange(width) // group
    return jnp.asarray((i[:, None] == i[None, :]).astype(np.float32)).astype(BF16)


def _group_sumsq(a, bd):
    sq = a * a
    hi = sq.astype(BF16)
    lo = (sq - hi.astype(F32)).astype(BF16)
    return (jnp.dot(hi, bd, preferred_element_type=F32) + jnp.dot(lo, bd, preferred_element_type=F32))


def _head_norm_rope(a, gain, bd, cos, sin):
    width = a.shape[-1]
    n = a * lax.rsqrt(_group_sumsq(a, bd) * (1.0 / HEAD_DIM) + EPS) * gain
    rep = width // LANES
    if rep > 1:
        cos = jnp.concatenate([cos] * rep, axis=-1)
        sin = jnp.concatenate([sin] * rep, axis=-1)
    fwd = pltpu.roll(n, width - 16, 1)
    bwd = pltpu.roll(n, 16, 1)
    lane = lax.broadcasted_iota(jnp.int32, n.shape, 1)
    sw = jnp.where((lane % 32) < 16, fwd, bwd)
    return n * cos + sw * sin


def _inproj_body(x_ref, g_ref, wqkv_ref, wu_ref, bd_ref, qg_ref, kg_ref, cos_ref, sin_ref,
                 q_ref, kw_ref, vw_ref, ut_ref):
    x = x_ref[...]
    h = x * lax.rsqrt(jnp.mean(x * x, axis=-1, keepdims=True) + EPS) * g_ref[...]
    hb = h.astype(BF16)
    qkv = jnp.dot(hb, wqkv_ref[...], preferred_element_type=F32)
    cos = cos_ref[...]
    sin = sin_ref[...]
    bd = bd_ref[...]
    q = _head_norm_rope(qkv[:, :D_ATTN], qg_ref[...], bd, cos, sin)
    q_ref[...] = (q * (HEAD_DIM ** -0.5 * math.log2(math.e))).astype(BF16)
    k = _head_norm_rope(qkv[:, D_ATTN:D_ATTN + D_KV], kg_ref[...], bd[:D_KV, :D_KV], cos, sin)
    kt = k.T.astype(BF16)
    zero = jnp.zeros((HEAD_DIM, kt.shape[1]), BF16)
    for h in range(N_KV_HEADS):
        kh = kt[h * HEAD_DIM:(h + 1) * HEAD_DIM]
        kw_ref[h, 0, :HEAD_DIM] = kh
        kw_ref[h, 0, HEAD_DIM:] = zero
        kw_ref[h, 1, :HEAD_DIM] = zero
        kw_ref[h, 1, HEAD_DIM:] = kh
    vt = qkv[:, D_ATTN + D_KV:].T.astype(BF16)
    one = jnp.ones((HEAD_DIM, vt.shape[1]), BF16)
    for h in range(N_KV_HEADS):
        vh = vt[h * HEAD_DIM:(h + 1) * HEAD_DIM]
        vw_ref[h, 0, :HEAD_DIM] = vh
        vw_ref[h, 0, HEAD_DIM:] = one
        vw_ref[h, 1, :HEAD_DIM] = one
        vw_ref[h, 1, HEAD_DIM:] = vh
    ut_ref[...] = lax.dot_general(wu_ref[...], hb, (((1,), (1,)), ((), ())),
                                  preferred_element_type=F32)


def _inproj(x, g_mix, wqkv, wut, bd, qg, kg, cos, sin):
    B, S, D = x.shape
    tm = TM_PROJ
    du = wut.shape[0]
    full = lambda shape: pl.BlockSpec(shape, lambda b, i: (0,) * len(shape))
    return pl.pallas_call(
        _inproj_body,
        grid=(B, S // tm),
        in_specs=[
            pl.BlockSpec((None, tm, D), lambda b, i: (b, i, 0)),
            full((1, D)), full(wqkv.shape), full(wut.shape), full(bd.shape),
            full((1, D_ATTN)), full((1, D_KV)),
            pl.BlockSpec((tm, LANES), lambda b, i: (i, 0)),
            pl.BlockSpec((tm, LANES), lambda b, i: (i, 0)),
        ],
        out_specs=[
            pl.BlockSpec((None, tm, D_ATTN), lambda b, i: (b, i, 0)),
            pl.BlockSpec((None, N_KV_HEADS, 2, LANES, tm), lambda b, i: (b, 0, 0, 0, i)),
            pl.BlockSpec((None, N_KV_HEADS, 2, LANES, tm), lambda b, i: (b, 0, 0, 0, i)),
            pl.BlockSpec((None, du, tm), lambda b, i: (b, 0, i)),
        ],
        out_shape=[
            jax.ShapeDtypeStruct((B, S, D_ATTN), BF16),
            jax.ShapeDtypeStruct((B, N_KV_HEADS, 2, LANES, S), BF16),
            jax.ShapeDtypeStruct((B, N_KV_HEADS, 2, LANES, S), BF16),
            jax.ShapeDtypeStruct((B, du, S), F32),
        ],
        compiler_params=_cparams(("parallel", "parallel")),
        name="inproj",
    )(x, g_mix, wqkv, wut, bd, qg, kg, cos, sin)


def _attn_body(q_ref, kw_ref, vw_ref, o_ref):

    def one_head(q, kw, vwt):
        s = jnp.dot(q, kw, preferred_element_type=F32)
        m = jnp.max(s, axis=-1, keepdims=True)
        p = jnp.exp2(s - m).astype(BF16)
        return lax.dot_general(vwt, p, (((1,), (1,)), ((), ())), preferred_element_type=F32)

    swap = lambda t: jnp.concatenate([t[HEAD_DIM:], t[:HEAD_DIM]], axis=0)
    for pair in range(D_ATTN // LANES):
        h = pair // (N_HEADS // N_KV_HEADS // 2)
        q = q_ref[:, pair * LANES:(pair + 1) * LANES]
        oe = one_head(q, kw_ref[h, 0], vw_ref[h, 0])
        oo = one_head(q, kw_ref[h, 1], vw_ref[h, 1])
        first = lax.broadcasted_iota(jnp.int32, oe.shape, 0) < HEAD_DIM
        num = jnp.where(first, oe, oo)
        den = jnp.where(first, swap(oe), swap(oo))
        o_ref[:, pair * LANES:(pair + 1) * LANES] = (num / den).T


def _attention(q, kw, vw):
    B, S, _ = q.shape
    tq = TQ_ATTN
    return pl.pallas_call(
        _attn_body,
        grid=(B, S // tq),
        in_specs=[
            pl.BlockSpec((None, tq, D_ATTN), lambda b, i: (b, i, 0)),
            pl.BlockSpec((None, N_KV_HEADS, 2, LANES, S), lambda b, i: (b, 0, 0, 0, 0)),
            pl.BlockSpec((None, N_KV_HEADS, 2, LANES, S), lambda b, i: (b, 0, 0, 0, 0)),
        ],
        out_specs=pl.BlockSpec((None, tq, D_ATTN), lambda b, i: (b, i, 0)),
        out_shape=jax.ShapeDtypeStruct((B, S, D_ATTN), F32),
        compiler_params=_cparams(("parallel", "arbitrary")),
        name="attention",
    )(q, kw, vw)


def _fwd_twiddle_store(y, tw_ref, s1_ref, row0):
    yr, yi = y[:FFT_N1], y[FFT_N1:]
    twr, twi = tw_ref[:, :LANES], tw_ref[:, LANES:]
    s1_ref[pl.ds(row0, FFT_N1), :LANES] = (yr * twr - yi * twi).astype(BF16)
    s1_ref[pl.ds(row0, FFT_N1), LANES:] = (yr * twi + yi * twr).astype(BF16)


def _filtfft_body(x_ref, maf_ref, tw_ref, g_ref, h_ref, s1_ref):
    C = x_ref.shape[0]

    def step_a(c, carry):
        y = jnp.dot(maf_ref[...], x_ref[c].astype(BF16), preferred_element_type=F32)
        _fwd_twiddle_store(y, tw_ref, s1_ref, pl.multiple_of(c * FFT_N1, FFT_N1))
        return carry

    lax.fori_loop(0, C, step_a, 0, unroll=SEQ_UNROLL)
    z = jnp.dot(s1_ref[...], g_ref[...], preferred_element_type=F32)
    h_ref[...] = z.reshape(C, FFT_N1, 2 * LANES)


def _filter_fft(circ, cst):
    n_seq = circ.shape[0]
    C = C_HY
    full = lambda a: pl.BlockSpec(a.shape, lambda i: (0,) * a.ndim)
    maf, tw, g = cst["maf"].astype(BF16), cst["tw"], cst["g"].astype(BF16)
    return pl.pallas_call(
        _filtfft_body,
        grid=(n_seq // C,),
        in_specs=[pl.BlockSpec((C, FFT_N1, FFT_N2), lambda i: (i, 0, 0)), full(maf), full(tw), full(g)],
        out_specs=pl.BlockSpec((C, FFT_N1, 2 * LANES), lambda i: (i, 0, 0)),
        out_shape=jax.ShapeDtypeStruct((n_seq, FFT_N1, 2 * LANES), F32),
        scratch_shapes=[pltpu.VMEM((C * FFT_N1, 2 * LANES), BF16)],
        compiler_params=_cparams(("parallel",)),
        name="filter_fft",
    )(circ, maf, tw, g)


def _short_conv(x, par_ref, c):
    rows, lanes = x.shape
    a_i = lax.broadcasted_iota(jnp.int32, x.shape, 0)
    b_i = lax.broadcasted_iota(jnp.int32, x.shape, 1)
    l1 = pltpu.roll(x, 1, 1)
    l2 = pltpu.roll(l1, 1, 0)
    prev = jnp.where(b_i == 0, l2, l1)
    prev = jnp.where((a_i == 0) & (b_i == 0), 0.0, prev)
    r1 = pltpu.roll(x, lanes - 1, 1)
    r2 = pltpu.roll(r1, rows - 1, 0)
    nxt = jnp.where(b_i == lanes - 1, r2, r1)
    nxt = jnp.where((a_i == rows - 1) & (b_i == lanes - 1), 0.0, nxt)
    w0 = par_ref[0, pl.ds(c, 1), :]
    w1 = par_ref[1, pl.ds(c, 1), :]
    w2 = par_ref[2, pl.ds(c, 1), :]
    cb = par_ref[3, pl.ds(c, 1), :]
    return cb + prev * w0 + x * w1 + nxt * w2


def _hyena_body(v_ref, x1_ref, x2_ref, pv_ref, p1_ref, p2_ref, fb_ref, h_ref,
                ma_ref, tw_ref, g_ref, ginv_ref, mir_ref, mii_ref,
                o_ref, s1_ref, s2_ref, vc_ref, z1_ref):
    C = v_ref.shape[1]
    half = FFT_N1 // 2

    def spectral(order):
        z = jnp.dot(s1_ref[...], g_ref[...], preferred_element_type=F32)
        hs = h_ref[order].reshape(C * FFT_N1, 2 * LANES)
        zr, zi = z[:, :LANES], z[:, LANES:]
        hr, hi = hs[:, :LANES], hs[:, LANES:]
        pb = jnp.concatenate([zr * hr - zi * hi, zr * hi + zi * hr], axis=1).astype(BF16)
        s2_ref[...] = jnp.dot(pb, ginv_ref[...], preferred_element_type=F32)

    def inv_a(c):
        row0 = pl.multiple_of(c * FFT_N1, FFT_N1)
        y = s2_ref[pl.ds(row0, FFT_N1), :]
        yr, yi = y[:, :LANES], y[:, LANES:]
        twr, twi = tw_ref[:, :LANES], tw_ref[:, LANES:]
        ur = (yr * twr + yi * twi).astype(BF16)
        ui = (yi * twr - yr * twi).astype(BF16)
        out = (jnp.dot(mir_ref[...], ur, preferred_element_type=F32)
               + jnp.dot(mii_ref[...], ui, preferred_element_type=F32))
        return out[:half], out[half:]

    def fwd_a(c, xr, xi):
        xs = jnp.concatenate([xr, xi], axis=0).astype(BF16)
        y = jnp.dot(ma_ref[...], xs, preferred_element_type=F32)
        _fwd_twiddle_store(y, tw_ref, s1_ref, pl.multiple_of(c * FFT_N1, FFT_N1))

    def pass1_a(c, carry):
        vr = _short_conv(v_ref[0, c], pv_ref, c)
        vi = _short_conv(v_ref[1, c], pv_ref, c)
        vc_ref[0, c] = vr
        vc_ref[1, c] = vi
        fwd_a(c, vr, vi)
        return carry

    def pass1_b(c, carry):
        cr, ci = inv_a(c)
        bias = fb_ref[0, pl.ds(c, 1), :]
        zr = _short_conv(x1_ref[0, c], p1_ref, c) * (cr + bias * vc_ref[0, c])
        zi = _short_conv(x1_ref[1, c], p1_ref, c) * (ci + bias * vc_ref[1, c])
        z1_ref[0, c] = zr
        z1_ref[1, c] = zi
        fwd_a(c, zr, zi)
        return carry

    def pass2_b(c, carry):
        cr, ci = inv_a(c)
        bias = fb_ref[1, pl.ds(c, 1), :]
        vc_ref[0, c] = _short_conv(x2_ref[0, c], p2_ref, c) * (cr + bias * z1_ref[0, c])
        vc_ref[1, c] = _short_conv(x2_ref[1, c], p2_ref, c) * (ci + bias * z1_ref[1, c])
        return carry

    lax.fori_loop(0, C, pass1_a, 0, unroll=SEQ_UNROLL)
    spectral(0)
    lax.fori_loop(0, C, pass1_b, 0, unroll=SEQ_UNROLL)
    spectral(1)
    lax.fori_loop(0, C, pass2_b, 0, unroll=SEQ_UNROLL)
    for b2 in range(2):
        for a in range(vc_ref.shape[2]):
            o_ref[b2, :, a * LANES:(a + 1) * LANES] = vc_ref[b2, :, a, :]


def _hyena(u4, par_u, fb, hspec, cst):
    B = u4.shape[0]
    C = C_HY
    J = D_HYENA // C
    rows = u4.shape[2]
    full = lambda a: pl.BlockSpec(a.shape, lambda j, p: (0,) * a.ndim)
    ma, g, ginv = cst["ma"].astype(BF16), cst["g"].astype(BF16), cst["ginv"].astype(BF16)
    mir, mii = cst["minv_r"].astype(BF16), cst["minv_i"].astype(BF16)
    tw = cst["tw"]
    u_spec = lambda k: pl.BlockSpec((2, C, rows, LANES), lambda j, p, k=k: (p, j + k * J, 0, 0))
    par_spec = lambda k: pl.BlockSpec((4, C, LANES), lambda j, p, k=k: (0, j + k * J, 0))
    return pl.pallas_call(
        _hyena_body,
        grid=(J, B // 2),
        in_specs=[
            u_spec(0), u_spec(1), u_spec(2), par_spec(0), par_spec(1), par_spec(2),
            pl.BlockSpec((2, C, LANES), lambda j, p: (0, j, 0)),
            pl.BlockSpec((2, C, FFT_N1, 2 * LANES), lambda j, p: (0, j, 0, 0)),
            full(ma), full(tw), full(g), full(ginv), full(mir), full(mii),
        ],
        out_specs=pl.BlockSpec((2, C, rows * LANES), lambda j, p: (p, j, 0)),
        out_shape=jax.ShapeDtypeStruct((B, D_HYENA, rows * LANES), F32),
        scratch_shapes=[
            pltpu.VMEM((C * FFT_N1, 2 * LANES), BF16),
            pltpu.VMEM((C * FFT_N1, 2 * LANES), F32),
            pltpu.VMEM((2, C, rows, LANES), F32),
            pltpu.VMEM((2, C, rows, LANES), F32),
        ],
        compiler_params=_cparams(("parallel", "arbitrary")),
        name="hyena",
    )(u4, u4, u4, par_u, par_u, par_u, fb, hspec, ma, tw, g, ginv, mir, mii)


def _dot3(a, b):
    ah = a.astype(BF16)
    al = (a - ah.astype(F32)).astype(BF16)
    bh = b.astype(BF16)
    bl = (b - bh.astype(F32)).astype(BF16)
    return (jnp.dot(ah, bh, preferred_element_type=F32) + jnp.dot(al, bh, preferred_element_type=F32)
            + jnp.dot(ah, bl, preferred_element_type=F32))


def _filtgen_body(zt_ref, w1_ref, b1_ref, f1_ref, w2_ref, b2_ref, f2_ref, w3f_ref, w3b_ref, ad_ref, tt_ref,
                  o_ref, hid_ref):
    L = hid_ref.shape[1] // 2

    @pl.when(pl.program_id(0) == 0)
    def _():
        h1 = jnp.sin(f1_ref[...] * (_dot3(w1_ref[...], zt_ref[...]) + b1_ref[...]))
        hid_ref[...] = jnp.sin(f2_ref[...] * (_dot3(w2_ref[...], h1) + b2_ref[...]))

    ad = ad_ref[...]
    hf = _dot3(w3f_ref[...], hid_ref[:, :L]) * jnp.exp(-ad * tt_ref[:, :L])
    hb = _dot3(w3b_ref[...], hid_ref[:, L:]) * jnp.exp(-ad * tt_ref[:, L:])
    hf = hf / (jnp.sum(jnp.abs(hf), axis=-1, keepdims=True) + EPS)
    hb = hb / (jnp.sum(jnp.abs(hb), axis=-1, keepdims=True) + EPS)
    first = lax.broadcasted_iota(jnp.int32, hb.shape, 1) == 0
    cf = hf + jnp.where(first, hb, 0.0)
    cb = jnp.where(first, 0.0, hb)
    half = L // LANES
    for a in range(half):
        o_ref[:, a, :] = cf[:, a * LANES:(a + 1) * LANES]
        o_ref[:, half + a, :] = cb[:, a * LANES:(a + 1) * LANES]


def _filter_gen(L, w_f1, b_f1, freq1, w_f2, b_f2, freq2, w_f3):
    bands = (FILTER_EMB - 1) // 2
    t = jnp.linspace(0.0, 1.0, L, dtype=F32)[:, None]
    w = (2.0 * math.pi / L) * jnp.arange(L, dtype=F32)[:, None]
    f = jnp.linspace(1e-4, bands - 1, bands, dtype=F32)[None]
    zf = f * w
    z = jnp.concatenate([t, jnp.cos(zf), -jnp.sin(zf)], axis=-1)
    back = lambda a: jnp.roll(a[::-1], 1, axis=0)
    kpad = 48
    zt = jnp.pad(jnp.concatenate([z, back(z)], axis=0).T, ((0, kpad - FILTER_EMB), (0, 0)))
    tt = jnp.concatenate([t, back(t)], axis=0).T
    w1t = jnp.pad(w_f1.T, ((0, 0), (0, kpad - FILTER_EMB)))
    w3 = w_f3.reshape(-1, 2, 2, D_HYENA)
    w3f = jnp.transpose(w3[:, :, 0], (1, 2, 0)).reshape(2 * D_HYENA, -1)
    w3b = jnp.transpose(w3[:, :, 1], (1, 2, 0)).reshape(2 * D_HYENA, -1)
    max_decay = math.log(DECAY_TARGET) / FAST_DECAY_PCT
    min_decay = math.log(DECAY_TARGET) / SLOW_DECAY_PCT
    deltas = jnp.linspace(min_decay, max_decay, D_HYENA, dtype=F32)
    ad = jnp.tile(jnp.abs(deltas), 2)[:, None]
    col = lambda v: v[:, None]
    R = 128
    n_rows = 2 * D_HYENA
    full = lambda a: pl.BlockSpec(a.shape, lambda i: (0,) * a.ndim)
    rows = lambda a: pl.BlockSpec((R, a.shape[1]), lambda i: (i, 0))
    args = (zt, w1t, col(b_f1), col(freq1), w_f2.T, col(b_f2), col(freq2))
    return pl.pallas_call(
        _filtgen_body,
        grid=(n_rows // R,),
        in_specs=[full(a) for a in args] + [rows(w3f), rows(w3b), rows(ad), full(tt)],
        out_specs=pl.BlockSpec((R, 2 * L // LANES, LANES), lambda i: (i, 0, 0)),
        out_shape=jax.ShapeDtypeStruct((n_rows, 2 * L // LANES, LANES), F32),
        scratch_shapes=[pltpu.VMEM((w_f2.shape[1], 2 * L), F32)],
        compiler_params=_cparams(("arbitrary",)),
        name="filter_gen",
    )(*args, w3f, w3b, ad, tt)


def _route_lanes(lg):
    neg = -1e30
    lane = lax.broadcasted_iota(jnp.int32, lg.shape, 1)
    gmask = lane < N_GROUPS
    gl = jnp.where(gmask, lg, neg)
    gm = jnp.max(gl, axis=-1, keepdims=True)
    gsum = jnp.sum(jnp.where(gmask, jnp.exp(gl - gm), 0.0), axis=-1, keepdims=True)
    g_top = 1.0 / gsum
    g_sel = jnp.min(jnp.where(gl == gm, lane, LANES), axis=-1, keepdims=True)
    lo = N_GROUPS + EXPERTS_PER_GROUP * g_sel
    el = jnp.where((lane >= lo) & (lane < lo + EXPERTS_PER_GROUP), lg, neg)
    m1 = jnp.max(el, axis=-1, keepdims=True)
    i1 = jnp.min(jnp.where(el == m1, lane, LANES), axis=-1, keepdims=True)
    el2 = jnp.where(lane == i1, neg, el)
    m2 = jnp.max(el2, axis=-1, keepdims=True)
    i2 = jnp.min(jnp.where(el2 == m2, lane, LANES), axis=-1, keepdims=True)
    d = jnp.exp(m2 - m1)
    p1 = 1.0 / (1.0 + d)
    p2 = d / (1.0 + d)
    e1 = (i1 - N_GROUPS).astype(F32)
    e2 = (i2 - N_GROUPS).astype(F32)
    return jnp.where(lane == 0, e1, jnp.where(lane == 1, e2, jnp.where(lane == 2, g_top * p1,
                     jnp.where(lane == 3, g_top * p2, 0.0))))


def _pack_bf16_halves(a):
    w = a.shape[1] // 2
    bits = pltpu.bitcast(a.astype(BF16).astype(F32), jnp.uint32)
    return (bits[:, :w] >> 16) | (bits[:, w:] & jnp.uint32(0xFFFF0000))


def _unpack_bf16_halves(wd):
    lo = pltpu.bitcast(wd << 16, F32)
    hi = pltpu.bitcast(wd & jnp.uint32(0xFFFF0000), F32)
    return jnp.concatenate([lo, hi], axis=1)


def _store_row_tiles(ref, packed):
    for j in range(ROW_CHUNKS):
        ref[:, j, :] = packed[:, j * LANES:(j + 1) * LANES]


def _load_row_tiles(ref):
    return jnp.concatenate([ref[:, j, :] for j in range(ROW_CHUNKS)], axis=1)


def _outproj_body(ya_ref, yh_ref, x_ref, ga_ref, gh_ref, wo_ref, bd_ref, gm_ref, wrh_ref, wrl_ref, brt_ref,
                  x1_ref, h2_ref, rt_ref, rtt_ref):
    ya = ya_ref[...]
    yan = ya * lax.rsqrt(_group_sumsq(ya, bd_ref[...]) * (1.0 / HEAD_DIM) + EPS) * ga_ref[...]
    yh = yh_ref[...]
    tm = yh.shape[1]
    yh3 = yh.reshape(D_HYENA // HYENA_HEAD, HYENA_HEAD, tm)
    ms = jnp.mean(yh3 * yh3, axis=1, keepdims=True)
    yhn = (yh3 * lax.rsqrt(ms + EPS)).reshape(D_HYENA, tm) * gh_ref[...]
    mix = (jnp.dot(yan.astype(BF16), wo_ref[:D_ATTN, :], preferred_element_type=F32)
           + jnp.dot(yhn.T.astype(BF16), wo_ref[D_ATTN:, :], preferred_element_type=F32))
    x1 = x_ref[...] + mix
    x1_ref[...] = x1
    h2 = x1 * lax.rsqrt(jnp.mean(x1 * x1, axis=-1, keepdims=True) + EPS) * gm_ref[...]
    _store_row_tiles(h2_ref, _pack_bf16_halves(h2))
    hi = h2.astype(BF16)
    lo = (h2 - hi.astype(F32)).astype(BF16)
    lg = (jnp.dot(hi, wrh_ref[...], preferred_element_type=F32)
          + jnp.dot(lo, wrh_ref[...], preferred_element_type=F32)
          + jnp.dot(hi, wrl_ref[...], preferred_element_type=F32)) + brt_ref[...]
    route = _route_lanes(lg)
    rt_ref[...] = route
    rtt_ref[...] = route.T[:8]


def _outproj(ya, yht, x, ga, gh, wo, bd, gm, wrh, wrl, brt):
    B, S, D = x.shape
    tm = TM_PROJ
    full = lambda a: pl.BlockSpec(a.shape, lambda b, i: (0,) * a.ndim)
    return pl.pallas_call(
        _outproj_body,
        grid=(B, S // tm),
        in_specs=[
            pl.BlockSpec((None, tm, D_ATTN), lambda b, i: (b, i, 0)),
            pl.BlockSpec((None, D_HYENA, tm), lambda b, i: (b, 0, i)),
            pl.BlockSpec((None, tm, D), lambda b, i: (b, i, 0)),
            full(ga), full(gh), full(wo), full(bd), full(gm), full(wrh), full(wrl), full(brt),
        ],
        out_specs=[
            pl.BlockSpec((None, tm, D), lambda b, i: (b, i, 0)),
            pl.BlockSpec((None, tm, ROW_CHUNKS, LANES), lambda b, i: (b, i, 0, 0)),
            pl.BlockSpec((None, tm, LANES), lambda b, i: (b, i, 0)),
            pl.BlockSpec((None, 8, tm), lambda b, i: (b, 0, i)),
        ],
        out_shape=[
            jax.ShapeDtypeStruct((B, S, D), F32),
            jax.ShapeDtypeStruct((B, S, ROW_CHUNKS, LANES), jnp.uint32),
            jax.ShapeDtypeStruct((B, S, LANES), F32),
            jax.ShapeDtypeStruct((B, 8, S), F32),
        ],
        compiler_params=_cparams(("parallel", "parallel")),
        name="outproj",
    )(ya, yht, x, ga, gh, wo, bd, gm, wrh, wrl, brt)


def _moe_body(be_ref, ra_ref, h2_hbm, wg_ref, wu_ref, wd_ref, y_hbm,
              wg_s, wu_s, wd_s, xbuf, ybuf, sem_in, sem_out, *, n_tok, n_rows):
    i = pl.program_id(0)
    nb = pl.num_programs(0)
    T = xbuf.shape[1]
    slot = i % 2

    def issue_gathers(blk, sl):
        for r in range(T):
            tok = ra_ref[blk * T + r] & (n_tok - 1)
            pltpu.make_async_copy(h2_hbm.at[tok], xbuf.at[sl, r], sem_in.at[sl]).start(priority=r % 2)

    def issue_scatters(blk, sl, spare):
        for r in range(T):
            dst = jnp.where(spare, n_rows + r, ra_ref[blk * T + r])
            pltpu.make_async_copy(ybuf.at[sl, r], y_hbm.at[dst], sem_out.at[sl]).start(priority=r % 2)

    def block_in_wait(sl):
        pltpu.make_async_copy(h2_hbm.at[pl.ds(0, T)], xbuf.at[sl], sem_in.at[sl]).wait()

    def block_out_wait(sl):
        pltpu.make_async_copy(ybuf.at[sl], y_hbm.at[pl.ds(0, T)], sem_out.at[sl]).wait()

    @pl.when(i == 0)
    def _():
        ybuf[...] = jnp.zeros(ybuf.shape, ybuf.dtype)
        issue_gathers(0, 0)

    prev = be_ref[jnp.maximum(i - 1, 0)]

    @pl.when((i == 0) | (be_ref[i] != prev))
    def _():
        wg_s[...] = wg_ref[...].astype(BF16)
        wu_s[...] = wu_ref[...].astype(BF16)
        wd_s[...] = wd_ref[...].astype(BF16)

    block_in_wait(slot)
    x = _unpack_bf16_halves(_load_row_tiles(xbuf.at[slot])).astype(BF16)
    issue_gathers(jnp.minimum(i + 1, nb - 1), 1 - slot)
    issue_scatters(jnp.maximum(i - 1, 0), 1 - slot, i == 0)
    a = jnp.dot(x, wg_s[...], preferred_element_type=F32)
    b = jnp.dot(x, wu_s[...], preferred_element_type=F32)
    hmid = (a * jax.nn.sigmoid(a)) * b
    y = _pack_bf16_halves(jnp.dot(hmid.astype(BF16), wd_s[...], preferred_element_type=F32))

    @pl.when(i >= 1)
    def _():
        block_out_wait(slot)

    _store_row_tiles(ybuf.at[slot], y)

    @pl.when(i == nb - 1)
    def _():
        issue_scatters(i, slot, False)
        block_in_wait(1 - slot)
        block_out_wait(1 - slot)
        block_out_wait(slot)


def _moe_experts(block_e, row_a, h2p, w_gate, w_up, w_down):
    n_tok = h2p.shape[0]
    n_rows = row_a.shape[0]
    row = h2p.shape[1:]
    D = w_gate.shape[1]
    T = TB_MOE
    assert n_tok & (n_tok - 1) == 0
    grid_spec = pltpu.PrefetchScalarGridSpec(
        num_scalar_prefetch=2,
        grid=(row_a.shape[0] // T,),
        in_specs=[
            pl.BlockSpec(memory_space=pl.ANY),
            pl.BlockSpec((None, D, D_EXPERT), lambda i, be, ra: (be[i], 0, 0)),
            pl.BlockSpec((None, D, D_EXPERT), lambda i, be, ra: (be[i], 0, 0)),
            pl.BlockSpec((None, D_EXPERT, D), lambda i, be, ra: (be[i], 0, 0)),
        ],
        out_specs=pl.BlockSpec(memory_space=pl.ANY),
        scratch_shapes=[
            pltpu.VMEM((D, D_EXPERT), BF16), pltpu.VMEM((D, D_EXPERT), BF16), pltpu.VMEM((D_EXPERT, D), BF16),
            pltpu.VMEM((2, T) + row, jnp.uint32), pltpu.VMEM((2, T) + row, jnp.uint32),
            pltpu.SemaphoreType.DMA((2,)), pltpu.SemaphoreType.DMA((2,)),
        ],
    )
    return pl.pallas_call(
        functools.partial(_moe_body, n_tok=n_tok, n_rows=n_rows),
        grid_spec=grid_spec,
        out_shape=jax.ShapeDtypeStruct((n_rows + T,) + row, jnp.uint32),
        compiler_params=_cparams(("arbitrary",)),
        name="moe_experts",
    )(block_e, row_a, h2p, w_gate, w_up, w_down)


def _dispatch(e_flat, N):
    T = TB_MOE
    NK = N * TOP_K
    experts = jnp.arange(N_EXPERTS, dtype=jnp.int32)
    order = jnp.argsort(e_flat).astype(jnp.int32)
    onehot = (e_flat[:, None] == experts[None]).astype(jnp.int32)
    counts = jnp.sum(onehot, axis=0)
    ends = jnp.cumsum(counts)
    starts = ends - counts
    padded = (counts + T - 1) // T * T
    pends = jnp.cumsum(padded)
    pstarts = pends - padded
    n_rows = -(-(NK + N_EXPERTS * (T - 1)) // T) * T
    n_blocks = n_rows // T
    blk_start = jnp.arange(n_blocks, dtype=jnp.int32) * T
    block_e = jnp.clip(jnp.sum((pends[None, :] <= blk_start[:, None]).astype(jnp.int32), axis=1),
                       0, N_EXPERTS - 1)
    oh_b = (block_e[:, None] == experts[None]).astype(jnp.int32)
    base = jnp.sum(oh_b * (starts - pstarts)[None], axis=1) + blk_start
    end_b = jnp.sum(oh_b * ends[None], axis=1)
    lane = jnp.arange(T, dtype=jnp.int32)[None]
    src = base[:, None] + lane
    pad_id = NK + blk_start[:, None] + lane - end_b[:, None]
    row_a = jnp.where(src < end_b[:, None], order[jnp.clip(src, 0, NK - 1)], pad_id)
    return block_e.astype(jnp.int32), row_a.reshape(n_rows).astype(jnp.int32)


def _final_body(x1_ref, y0_ref, y1_ref, rt_ref, p_ref, gp_ref, wg_ref, bg_ref, wp_ref, gf_ref, o_ref):
    w0 = rt_ref[:, 2:3]
    w1 = rt_ref[:, 3:4]
    y0 = _unpack_bf16_halves(_load_row_tiles(y0_ref))
    y1 = _unpack_bf16_halves(_load_row_tiles(y1_ref))
    x2 = x1_ref[...] + (y0 * w0 + y1 * w1)
    hp = x2 * lax.rsqrt(jnp.mean(x2 * x2, axis=-1, keepdims=True) + EPS) * gp_ref[...]
    gate = jax.nn.sigmoid(jnp.dot(hp.astype(BF16), wg_ref[...], preferred_element_type=F32) + bg_ref[...])
    pe = jnp.dot(p_ref[...].astype(BF16), wp_ref[...], preferred_element_type=F32)
    x3 = x2 + pe * gate
    o_ref[...] = x3 * lax.rsqrt(jnp.mean(x3 * x3, axis=-1, keepdims=True) + EPS) * gf_ref[...]


def _final(x1, y, route, p, gp, wg, bg, wp, gf):
    N, D = x1.shape
    tm = TM_PROJ
    row = lambda w: pl.BlockSpec((tm, w), lambda i: (i, 0))
    full = lambda a: pl.BlockSpec(a.shape, lambda i: (0,) * a.ndim)
    y0, y1 = y, y
    return pl.pallas_call(
        _final_body,
        grid=(N // tm,),
        in_specs=[row(D), pl.BlockSpec((tm, ROW_CHUNKS, LANES), lambda i: (i, 0, 0)),
                  pl.BlockSpec((tm, ROW_CHUNKS, LANES), lambda i: (i + N // tm, 0, 0)),
                  row(LANES), row(p.shape[1]),
                  full(gp), full(wg), full(bg), full(wp), full(gf)],
        out_specs=row(D),
        out_shape=jax.ShapeDtypeStruct((N, D), F32),
        compiler_params=_cparams(("parallel",)),
        name="ple_final",
    )(x1, y0, y1, route, p, gp, wg, bg, wp, gf)


def kernel(x, p, g_mix, w_in, q_gain, k_gain, conv_w, conv_b, w_f1, b_f1, freq1, w_f2, b_f2, freq2, w_f3, filt_bias, g_attn_out, g_hyena_out, w_out, g_moe, w_group, b_group, w_router, b_router, w_gate, w_up, w_down, g_ple, w_ple_gate, b_ple_gate, w_ple, g_final):
    B, S, D = x.shape
    N = B * S
    assert p.shape[0] == 1 and S == (FFT_N1 // 2) * FFT_N2 and B % 2 == 0
    i = 0
    cst = _dft_constants()
    cos, sin = _rope_tables(S)
    bd = _block_diag_ones(D_ATTN, HEAD_DIM)

    n_qkv = D_ATTN + 2 * D_KV
    wqkv = w_in[i][:, :n_qkv].astype(BF16)
    wut = w_in[i][:, n_qkv:].T.astype(BF16)
    q, kw, vw, ut = _inproj(x, g_mix[i][None], wqkv, wut, bd,
                            jnp.tile(q_gain[i], N_HEADS)[None], jnp.tile(k_gain[i], N_KV_HEADS)[None], cos, sin)

    ya = _attention(q, kw, vw)

    circ = _filter_gen(S, w_f1[i], b_f1[i], freq1[i], w_f2[i], b_f2[i], freq2[i], w_f3[i])
    hspec = _filter_fft(circ, cst)
    hspec = hspec.reshape(2, D_HYENA, FFT_N1, 2 * LANES)
    du = ut.shape[1]
    u4 = ut.reshape(B, du, S // LANES, LANES)
    par_u = jnp.broadcast_to(jnp.concatenate([conv_w[i], conv_b[i][None]], 0)[:, :, None], (4, du, LANES))
    fb = jnp.broadcast_to(filt_bias[i][:, :, None], (2, D_HYENA, LANES))
    yht = _hyena(u4, par_u, fb, hspec, cst)

    wrt = jnp.zeros((D, LANES), F32).at[:, :N_GROUPS].set(w_group[i]).at[:, N_GROUPS:N_GROUPS + N_EXPERTS].set(w_router[i])
    brt = jnp.zeros((1, LANES), F32).at[0, :N_GROUPS].set(b_group[i]).at[0, N_GROUPS:N_GROUPS + N_EXPERTS].set(b_router[i])
    wrh = wrt.astype(BF16)
    wrl = (wrt - wrh.astype(F32)).astype(BF16)
    x1, h2, route, route_t = _outproj(ya, yht, x, g_attn_out[i][None], g_hyena_out[i][:, None],
                                      w_out[i].astype(BF16), bd, g_moe[i][None], wrh, wrl, brt)

    e_flat = jnp.transpose(route_t[:, :TOP_K], (1, 0, 2)).reshape(TOP_K * N).astype(jnp.int32)
    block_e, row_a = _dispatch(e_flat, N)
    y = _moe_experts(block_e, row_a, h2.reshape(N, ROW_CHUNKS, LANES), w_gate[i], w_up[i], w_down[i])

    out = _final(x1.reshape(N, D), y, route.reshape(N, LANES), p[i].reshape(N, -1), g_ple[i][None],
                 w_ple_gate[i].astype(BF16), b_ple_gate[i][None], w_ple[i].astype(BF16), g_final[None])
    return out.reshape(B, S, D)
```

```python
import functools
import math

import numpy as np
import jax
import jax.numpy as jnp
from jax import lax
from jax.experimental import pallas as pl
from jax.experimental.pallas import tpu as pltpu

F32 = jnp.float32
BF16 = jnp.bfloat16

D_MODEL = 1024
EPS = 1e-6
GRID_W = 64
N_HEADS = 8
N_KV_HEADS = 2
HEAD_DIM = 64
D_ATTN = N_HEADS * HEAD_DIM
D_KV = N_KV_HEADS * HEAD_DIM
ROPE_THETA = 10000.0
D_HYENA = 512
HYENA_HEAD = 64
FILTER_EMB = 33
FAST_DECAY_PCT = 0.3
SLOW_DECAY_PCT = 1.5
DECAY_TARGET = 1e-2
N_GROUPS = 4
EXPERTS_PER_GROUP = 8
N_EXPERTS = N_GROUPS * EXPERTS_PER_GROUP
TOP_K = 2
D_EXPERT = 512

LANES = 128
FFT_N1 = 64
FFT_N2 = 128
VMEM_LIMIT = 56 * 1024 * 1024

TM_PROJ = 512
TQ_ATTN = 256
C_HY = 32
ROW_CHUNKS = D_MODEL // 2 // LANES
SEQ_UNROLL = 32
TB_MOE = 256


def _cparams(sem):
    return pltpu.CompilerParams(dimension_semantics=sem, vmem_limit_bytes=VMEM_LIMIT)


def _rope_tables(S):
    half = HEAD_DIM // 2
    t = jnp.arange(S, dtype=F32)
    r_idx = jnp.floor(t / GRID_W)
    c_idx = t - r_idx * GRID_W
    inv = ROPE_THETA ** (-jnp.arange(0, half, 2, dtype=F32) / half)
    ang_r = r_idx[:, None] * inv[None]
    ang_c = c_idx[:, None] * inv[None]
    cos_h = jnp.concatenate([jnp.cos(ang_r), jnp.cos(ang_r), jnp.cos(ang_c), jnp.cos(ang_c)], axis=-1)
    sin_h = jnp.concatenate([-jnp.sin(ang_r), jnp.sin(ang_r), -jnp.sin(ang_c), jnp.sin(ang_c)], axis=-1)
    return jnp.tile(cos_h, (1, 2)), jnp.tile(sin_h, (1, 2))


def _dft_constants():
    n1, n2 = FFT_N1, FFT_N2
    n = n1 * n2
    a = np.arange(n1)
    ang = 2.0 * np.pi * np.outer(a, a) / n1
    far, fai = np.cos(ang), -np.sin(ang)
    hlf = n1 // 2
    ma = np.block([[far[:, :hlf], -fai[:, :hlf]], [fai[:, :hlf], far[:, :hlf]]])
    maf = np.concatenate([far, fai], axis=0)
    b = np.arange(n2)
    angt = 2.0 * np.pi * np.outer(a, b) / n
    tw = np.concatenate([np.cos(angt), -np.sin(angt)], axis=1)
    angb = 2.0 * np.pi * np.outer(b, b) / n2
    fbr, fbi = np.cos(angb), -np.sin(angb)
    g = np.block([[fbr, fbi], [-fbi, fbr]])
    ginv = np.block([[fbr, -fbi], [fbi, fbr]])
    minv_r = np.concatenate([far[:hlf], -fai[:hlf]], axis=0) / n
    minv_i = np.concatenate([fai[:hlf], far[:hlf]], axis=0) / n
    f = lambda m: jnp.asarray(m.astype(np.float32))
    return dict(ma=f(ma), maf=f(maf), tw=f(tw), g=f(g), ginv=f(ginv), minv_r=f(minv_r), minv_i=f(minv_i))


def _block_diag_ones(width, group):
    i = np.arange(width) // group
    return jnp.asarray((i[:, None] == i[None, :]).astype(np.float32)).astype(BF16)


def _group_sumsq(a, bd):
    sq = a * a
    hi = sq.astype(BF16)
    lo = (sq - hi.astype(F32)).astype(BF16)
    return (jnp.dot(hi, bd, preferred_element_type=F32) + jnp.dot(lo, bd, preferred_element_type=F32))


def _head_norm_rope(a, gain, bd, cos, sin):
    width = a.shape[-1]
    n = a * lax.rsqrt(_group_sumsq(a, bd) * (1.0 / HEAD_DIM) + EPS) * gain
    rep = width // LANES
    if rep > 1:
        cos = jnp.concatenate([cos] * rep, axis=-1)
        sin = jnp.concatenate([sin] * rep, axis=-1)
    fwd = pltpu.roll(n, width - 16, 1)
    bwd = pltpu.roll(n, 16, 1)
    lane = lax.broadcasted_iota(jnp.int32, n.shape, 1)
    sw = jnp.where((lane % 32) < 16, fwd, bwd)
    return n * cos + sw * sin


def _inproj_body(x_ref, g_ref, wqkv_ref, wu_ref, bd_ref, qg_ref, kg_ref, cos_ref, sin_ref,
                 q_ref, kw_ref, vw_ref, ut_ref):
    x = x_ref[...]
    h = x * lax.rsqrt(jnp.mean(x * x, axis=-1, keepdims=True) + EPS) * g_ref[...]
    hb = h.astype(BF16)
    qkv = jnp.dot(hb, wqkv_ref[...], preferred_element_type=F32)
    cos = cos_ref[...]
    sin = sin_ref[...]
    bd = bd_ref[...]
    q = _head_norm_rope(qkv[:, :D_ATTN], qg_ref[...], bd, cos, sin)
    q_ref[...] = (q * (HEAD_DIM ** -0.5 * math.log2(math.e))).astype(BF16)
    k = _head_norm_rope(qkv[:, D_ATTN:D_ATTN + D_KV], kg_ref[...], bd[:D_KV, :D_KV], cos, sin)
    kt = k.T.astype(BF16)
    zero = jnp.zeros((HEAD_DIM, kt.shape[1]), BF16)
    for h in range(N_KV_HEADS):
        kh = kt[h * HEAD_DIM:(h + 1) * HEAD_DIM]
        kw_ref[h, 0, :HEAD_DIM] = kh
        kw_ref[h, 0, HEAD_DIM:] = zero
        kw_ref[h, 1, :HEAD_DIM] = zero
        kw_ref[h, 1, HEAD_DIM:] = kh
    v = qkv[:, D_ATTN + D_KV:]
    vr = pltpu.roll(v, HEAD_DIM, 1)
    first = lax.broadcasted_iota(jnp.int32, v.shape, 1) < HEAD_DIM
    vw_ref[0, 0] = jnp.where(first, v, 1.0).astype(BF16)
    vw_ref[0, 1] = jnp.where(first, 1.0, vr).astype(BF16)
    vw_ref[1, 0] = jnp.where(first, vr, 1.0).astype(BF16)
    vw_ref[1, 1] = jnp.where(first, 1.0, v).astype(BF16)
    ut_ref[...] = lax.dot_general(wu_ref[...], hb, (((1,), (1,)), ((), ())),
                                  preferred_element_type=F32)


def _inproj(x, g_mix, wqkv, wut, bd, qg, kg, cos, sin):
    B, S, D = x.shape
    tm = TM_PROJ
    du = wut.shape[0]
    full = lambda shape: pl.BlockSpec(shape, lambda b, i: (0,) * len(shape))
    return pl.pallas_call(
        _inproj_body,
        grid=(B, S // tm),
        in_specs=[
            pl.BlockSpec((None, tm, D), lambda b, i: (b, i, 0)),
            full((1, D)), full(wqkv.shape), full(wut.shape), full(bd.shape),
            full((1, D_ATTN)), full((1, D_KV)),
            pl.BlockSpec((tm, LANES), lambda b, i: (i, 0)),
            pl.BlockSpec((tm, LANES), lambda b, i: (i, 0)),
        ],
        out_specs=[
            pl.BlockSpec((None, tm, D_ATTN), lambda b, i: (b, i, 0)),
            pl.BlockSpec((None, N_KV_HEADS, 2, LANES, tm), lambda b, i: (b, 0, 0, 0, i)),
            pl.BlockSpec((None, N_KV_HEADS, 2, tm, LANES), lambda b, i: (b, 0, 0, i, 0)),
            pl.BlockSpec((None, du, tm), lambda b, i: (b, 0, i)),
        ],
        out_shape=[
            jax.ShapeDtypeStruct((B, S, D_ATTN), BF16),
            jax.ShapeDtypeStruct((B, N_KV_HEADS, 2, LANES, S), BF16),
            jax.ShapeDtypeStruct((B, N_KV_HEADS, 2, S, LANES), BF16),
            jax.ShapeDtypeStruct((B, du, S), F32),
        ],
        compiler_params=_cparams(("parallel", "parallel")),
        name="inproj",
    )(x, g_mix, wqkv, wut, bd, qg, kg, cos, sin)


def _attn_body(q_ref, kw_ref, vw_ref, o_ref):

    def one_head(q, kw, vw):
        s = jnp.dot(q, kw, preferred_element_type=F32)
        m = jnp.max(s, axis=-1, keepdims=True)
        p = jnp.exp2(s - m).astype(BF16)
        return jnp.dot(p, vw, preferred_element_type=F32)

    for pair in range(D_ATTN // LANES):
        h = pair // (N_HEADS // N_KV_HEADS // 2)
        q = q_ref[:, pair * LANES:(pair + 1) * LANES]
        oe = one_head(q, kw_ref[h, 0], vw_ref[h, 0])
        oo = one_head(q, kw_ref[h, 1], vw_ref[h, 1])
        first = lax.broadcasted_iota(jnp.int32, oe.shape, 1) < HEAD_DIM
        num = jnp.where(first, oe, oo)
        den = jnp.where(first, pltpu.roll(oe, HEAD_DIM, 1), pltpu.roll(oo, HEAD_DIM, 1))
        o_ref[:, pair * LANES:(pair + 1) * LANES] = num / den


def _attention(q, kw, vw):
    B, S, _ = q.shape
    tq = TQ_ATTN
    return pl.pallas_call(
        _attn_body,
        grid=(B, S // tq),
        in_specs=[
            pl.BlockSpec((None, tq, D_ATTN), lambda b, i: (b, i, 0)),
            pl.BlockSpec((None, N_KV_HEADS, 2, LANES, S), lambda b, i: (b, 0, 0, 0, 0)),
            pl.BlockSpec((None, N_KV_HEADS, 2, S, LANES), lambda b, i: (b, 0, 0, 0, 0)),
        ],
        out_specs=pl.BlockSpec((None, tq, D_ATTN), lambda b, i: (b, i, 0)),
        out_shape=jax.ShapeDtypeStruct((B, S, D_ATTN), F32),
        compiler_params=_cparams(("parallel", "arbitrary")),
        name="attention",
    )(q, kw, vw)


def _fwd_twiddle_store(y, tw_ref, s1_ref, row0):
    yr, yi = y[:FFT_N1], y[FFT_N1:]
    twr, twi = tw_ref[:, :LANES], tw_ref[:, LANES:]
    s1_ref[pl.ds(row0, FFT_N1), :LANES] = (yr * twr - yi * twi).astype(BF16)
    s1_ref[pl.ds(row0, FFT_N1), LANES:] = (yr * twi + yi * twr).astype(BF16)


def _filtfft_body(x_ref, maf_ref, tw_ref, g_ref, h_ref, s1_ref):
    C = x_ref.shape[0]

    def step_a(c, carry):
        y = jnp.dot(maf_ref[...], x_ref[c].astype(BF16), preferred_element_type=F32)
        _fwd_twiddle_store(y, tw_ref, s1_ref, pl.multiple_of(c * FFT_N1, FFT_N1))
        return carry

    lax.fori_loop(0, C, step_a, 0, unroll=SEQ_UNROLL)
    z = jnp.dot(s1_ref[...], g_ref[...], preferred_element_type=F32)
    h_ref[...] = z.reshape(C, FFT_N1, 2 * LANES)


def _filter_fft(circ, cst):
    n_seq = circ.shape[0]
    C = C_HY
    full = lambda a: pl.BlockSpec(a.shape, lambda i: (0,) * a.ndim)
    maf, tw, g = cst["maf"].astype(BF16), cst["tw"], cst["g"].astype(BF16)
    return pl.pallas_call(
        _filtfft_body,
        grid=(n_seq // C,),
        in_specs=[pl.BlockSpec((C, FFT_N1, FFT_N2), lambda i: (i, 0, 0)), full(maf), full(tw), full(g)],
        out_specs=pl.BlockSpec((C, FFT_N1, 2 * LANES), lambda i: (i, 0, 0)),
        out_shape=jax.ShapeDtypeStruct((n_seq, FFT_N1, 2 * LANES), F32),
        scratch_shapes=[pltpu.VMEM((C * FFT_N1, 2 * LANES), BF16)],
        compiler_params=_cparams(("parallel",)),
        name="filter_fft",
    )(circ, maf, tw, g)


def _short_conv(x, par_ref, c):
    rows, lanes = x.shape
    a_i = lax.broadcasted_iota(jnp.int32, x.shape, 0)
    b_i = lax.broadcasted_iota(jnp.int32, x.shape, 1)
    l1 = pltpu.roll(x, 1, 1)
    l2 = pltpu.roll(l1, 1, 0)
    prev = jnp.where(b_i == 0, l2, l1)
    prev = jnp.where((a_i == 0) & (b_i == 0), 0.0, prev)
    r1 = pltpu.roll(x, lanes - 1, 1)
    r2 = pltpu.roll(r1, rows - 1, 0)
    nxt = jnp.where(b_i == lanes - 1, r2, r1)
    nxt = jnp.where((a_i == rows - 1) & (b_i == lanes - 1), 0.0, nxt)
    w0 = par_ref[0, pl.ds(c, 1), :]
    w1 = par_ref[1, pl.ds(c, 1), :]
    w2 = par_ref[2, pl.ds(c, 1), :]
    cb = par_ref[3, pl.ds(c, 1), :]
    return cb + prev * w0 + x * w1 + nxt * w2


def _hyena_body(v_ref, x1_ref, x2_ref, pv_ref, p1_ref, p2_ref, fb_ref, h_ref,
                ma_ref, tw_ref, g_ref, ginv_ref, mir_ref, mii_ref,
                o_ref, s1_ref, s2_ref, vc_ref, z1_ref):
    C = v_ref.shape[1]
    half = FFT_N1 // 2

    def spectral(order):
        z = jnp.dot(s1_ref[...], g_ref[...], preferred_element_type=F32)
        hs = h_ref[order].reshape(C * FFT_N1, 2 * LANES)
        zr, zi = z[:, :LANES], z[:, LANES:]
        hr, hi = hs[:, :LANES], hs[:, LANES:]
        pb = jnp.concatenate([zr * hr - zi * hi, zr * hi + zi * hr], axis=1).astype(BF16)
        s2_ref[...] = jnp.dot(pb, ginv_ref[...], preferred_element_type=F32)

    def inv_a(c):
        row0 = pl.multiple_of(c * FFT_N1, FFT_N1)
        y = s2_ref[pl.ds(row0, FFT_N1), :]
        yr, yi = y[:, :LANES], y[:, LANES:]
        twr, twi = tw_ref[:, :LANES], tw_ref[:, LANES:]
        ur = (yr * twr + yi * twi).astype(BF16)
        ui = (yi * twr - yr * twi).astype(BF16)
        out = (jnp.dot(mir_ref[...], ur, preferred_element_type=F32)
               + jnp.dot(mii_ref[...], ui, preferred_element_type=F32))
        return out[:half], out[half:]

    def fwd_a(c, xr, xi):
        xs = jnp.concatenate([xr, xi], axis=0).astype(BF16)
        y = jnp.dot(ma_ref[...], xs, preferred_element_type=F32)
        _fwd_twiddle_store(y, tw_ref, s1_ref, pl.multiple_of(c * FFT_N1, FFT_N1))

    def pass1_a(c, carry):
        vr = _short_conv(v_ref[0, c], pv_ref, c)
        vi = _short_conv(v_ref[1, c], pv_ref, c)
        vc_ref[0, c] = vr
        vc_ref[1, c] = vi
        fwd_a(c, vr, vi)
        return carry

    def pass1_b(c, carry):
        cr, ci = inv_a(c)
        bias = fb_ref[0, pl.ds(c, 1), :]
        zr = _short_conv(x1_ref[0, c], p1_ref, c) * (cr + bias * vc_ref[0, c])
        zi = _short_conv(x1_ref[1, c], p1_ref, c) * (ci + bias * vc_ref[1, c])
        z1_ref[0, c] = zr
        z1_ref[1, c] = zi
        fwd_a(c, zr, zi)
        return carry

    def pass2_b(c, carry):
        cr, ci = inv_a(c)
        bias = fb_ref[1, pl.ds(c, 1), :]
        vc_ref[0, c] = _short_conv(x2_ref[0, c], p2_ref, c) * (cr + bias * z1_ref[0, c])
        vc_ref[1, c] = _short_conv(x2_ref[1, c], p2_ref, c) * (ci + bias * z1_ref[1, c])
        return carry

    lax.fori_loop(0, C, pass1_a, 0, unroll=SEQ_UNROLL)
    spectral(0)
    lax.fori_loop(0, C, pass1_b, 0, unroll=SEQ_UNROLL)
    spectral(1)
    lax.fori_loop(0, C, pass2_b, 0, unroll=SEQ_UNROLL)
    for b2 in range(2):
        for a in range(vc_ref.shape[2]):
            o_ref[b2, :, a * LANES:(a + 1) * LANES] = vc_ref[b2, :, a, :]


def _hyena(u4, par_u, fb, hspec, cst):
    B = u4.shape[0]
    C = C_HY
    J = D_HYENA // C
    rows = u4.shape[2]
    full = lambda a: pl.BlockSpec(a.shape, lambda j, p: (0,) * a.ndim)
    ma, g, ginv = cst["ma"].astype(BF16), cst["g"].astype(BF16), cst["ginv"].astype(BF16)
    mir, mii = cst["minv_r"].astype(BF16), cst["minv_i"].astype(BF16)
    tw = cst["tw"]
    u_spec = lambda k: pl.BlockSpec((2, C, rows, LANES), lambda j, p, k=k: (p, j + k * J, 0, 0))
    par_spec = lambda k: pl.BlockSpec((4, C, LANES), lambda j, p, k=k: (0, j + k * J, 0))
    return pl.pallas_call(
        _hyena_body,
        grid=(J, B // 2),
        in_specs=[
            u_spec(0), u_spec(1), u_spec(2), par_spec(0), par_spec(1), par_spec(2),
            pl.BlockSpec((2, C, LANES), lambda j, p: (0, j, 0)),
            pl.BlockSpec((2, C, FFT_N1, 2 * LANES), lambda j, p: (0, j, 0, 0)),
            full(ma), full(tw), full(g), full(ginv), full(mir), full(mii),
        ],
        out_specs=pl.BlockSpec((2, C, rows * LANES), lambda j, p: (p, j, 0)),
        out_shape=jax.ShapeDtypeStruct((B, D_HYENA, rows * LANES), F32),
        scratch_shapes=[
            pltpu.VMEM((C * FFT_N1, 2 * LANES), BF16),
            pltpu.VMEM((C * FFT_N1, 2 * LANES), F32),
            pltpu.VMEM((2, C, rows, LANES), F32),
            pltpu.VMEM((2, C, rows, LANES), F32),
        ],
        compiler_params=_cparams(("parallel", "arbitrary")),
        name="hyena",
    )(u4, u4, u4, par_u, par_u, par_u, fb, hspec, ma, tw, g, ginv, mir, mii)


def _dot3(a, b):
    ah = a.astype(BF16)
    al = (a - ah.astype(F32)).astype(BF16)
    bh = b.astype(BF16)
    bl = (b - bh.astype(F32)).astype(BF16)
    return (jnp.dot(ah, bh, preferred_element_type=F32) + jnp.dot(al, bh, preferred_element_type=F32)
            + jnp.dot(ah, bl, preferred_element_type=F32))


def _filtgen_body(zt_ref, w1_ref, b1_ref, f1_ref, w2_ref, b2_ref, f2_ref, w3f_ref, w3b_ref, ad_ref, tt_ref,
                  o_ref, hid_ref):
    L = hid_ref.shape[1] // 2

    @pl.when(pl.program_id(0) == 0)
    def _():
        h1 = jnp.sin(f1_ref[...] * (_dot3(w1_ref[...], zt_ref[...]) + b1_ref[...]))
        hid_ref[...] = jnp.sin(f2_ref[...] * (_dot3(w2_ref[...], h1) + b2_ref[...]))

    ad = ad_ref[...]
    hf = _dot3(w3f_ref[...], hid_ref[:, :L]) * jnp.exp(-ad * tt_ref[:, :L])
    hb = _dot3(w3b_ref[...], hid_ref[:, L:]) * jnp.exp(-ad * tt_ref[:, L:])
    hf = hf / (jnp.sum(jnp.abs(hf), axis=-1, keepdims=True) + EPS)
    hb = hb / (jnp.sum(jnp.abs(hb), axis=-1, keepdims=True) + EPS)
    first = lax.broadcasted_iota(jnp.int32, hb.shape, 1) == 0
    cf = hf + jnp.where(first, hb, 0.0)
    cb = jnp.where(first, 0.0, hb)
    half = L // LANES
    for a in range(half):
        o_ref[:, a, :] = cf[:, a * LANES:(a + 1) * LANES]
        o_ref[:, half + a, :] = cb[:, a * LANES:(a + 1) * LANES]


def _filter_gen(L, w_f1, b_f1, freq1, w_f2, b_f2, freq2, w_f3):
    bands = (FILTER_EMB - 1) // 2
    t = jnp.linspace(0.0, 1.0, L, dtype=F32)[:, None]
    w = (2.0 * math.pi / L) * jnp.arange(L, dtype=F32)[:, None]
    f = jnp.linspace(1e-4, bands - 1, bands, dtype=F32)[None]
    zf = f * w
    z = jnp.concatenate([t, jnp.cos(zf), -jnp.sin(zf)], axis=-1)
    back = lambda a: jnp.roll(a[::-1], 1, axis=0)
    kpad = 48
    zt = jnp.pad(jnp.concatenate([z, back(z)], axis=0).T, ((0, kpad - FILTER_EMB), (0, 0)))
    tt = jnp.concatenate([t, back(t)], axis=0).T
    w1t = jnp.pad(w_f1.T, ((0, 0), (0, kpad - FILTER_EMB)))
    w3 = w_f3.reshape(-1, 2, 2, D_HYENA)
    w3f = jnp.transpose(w3[:, :, 0], (1, 2, 0)).reshape(2 * D_HYENA, -1)
    w3b = jnp.transpose(w3[:, :, 1], (1, 2, 0)).reshape(2 * D_HYENA, -1)
    max_decay = math.log(DECAY_TARGET) / FAST_DECAY_PCT
    min_decay = math.log(DECAY_TARGET) / SLOW_DECAY_PCT
    deltas = jnp.linspace(min_decay, max_decay, D_HYENA, dtype=F32)
    ad = jnp.tile(jnp.abs(deltas), 2)[:, None]
    col = lambda v: v[:, None]
    R = 128
    n_rows = 2 * D_HYENA
    full = lambda a: pl.BlockSpec(a.shape, lambda i: (0,) * a.ndim)
    rows = lambda a: pl.BlockSpec((R, a.shape[1]), lambda i: (i, 0))
    args = (zt, w1t, col(b_f1), col(freq1), w_f2.T, col(b_f2), col(freq2))
    return pl.pallas_call(
        _filtgen_body,
        grid=(n_rows // R,),
        in_specs=[full(a) for a in args] + [rows(w3f), rows(w3b), rows(ad), full(tt)],
        out_specs=pl.BlockSpec((R, 2 * L // LANES, LANES), lambda i: (i, 0, 0)),
        out_shape=jax.ShapeDtypeStruct((n_rows, 2 * L // LANES, LANES), F32),
        scratch_shapes=[pltpu.VMEM((w_f2.shape[1], 2 * L), F32)],
        compiler_params=_cparams(("arbitrary",)),
        name="filter_gen",
    )(*args, w3f, w3b, ad, tt)


def _route_lanes(lg):
    neg = -1e30
    lane = lax.broadcasted_iota(jnp.int32, lg.shape, 1)
    gmask = lane < N_GROUPS
    gl = jnp.where(gmask, lg, neg)
    gm = jnp.max(gl, axis=-1, keepdims=True)
    gsum = jnp.sum(jnp.where(gmask, jnp.exp(gl - gm), 0.0), axis=-1, keepdims=True)
    g_top = 1.0 / gsum
    g_sel = jnp.min(jnp.where(gl == gm, lane, LANES), axis=-1, keepdims=True)
    lo = N_GROUPS + EXPERTS_PER_GROUP * g_sel
    el = jnp.where((lane >= lo) & (lane < lo + EXPERTS_PER_GROUP), lg, neg)
    m1 = jnp.max(el, axis=-1, keepdims=True)
    i1 = jnp.min(jnp.where(el == m1, lane, LANES), axis=-1, keepdims=True)
    el2 = jnp.where(lane == i1, neg, el)
    m2 = jnp.max(el2, axis=-1, keepdims=True)
    i2 = jnp.min(jnp.where(el2 == m2, lane, LANES), axis=-1, keepdims=True)
    d = jnp.exp(m2 - m1)
    p1 = 1.0 / (1.0 + d)
    p2 = d / (1.0 + d)
    e1 = (i1 - N_GROUPS).astype(F32)
    e2 = (i2 - N_GROUPS).astype(F32)
    return jnp.where(lane == 0, e1, jnp.where(lane == 1, e2, jnp.where(lane == 2, g_top * p1,
                     jnp.where(lane == 3, g_top * p2, 0.0))))


def _pack_bf16_halves(a):
    w = a.shape[1] // 2
    bits = pltpu.bitcast(a.astype(BF16).astype(F32), jnp.uint32)
    return (bits[:, :w] >> 16) | (bits[:, w:] & jnp.uint32(0xFFFF0000))


def _unpack_bf16_halves(wd):
    lo = pltpu.bitcast(wd << 16, F32)
    hi = pltpu.bitcast(wd & jnp.uint32(0xFFFF0000), F32)
    return jnp.concatenate([lo, hi], axis=1)


def _store_row_tiles(ref, packed):
    for j in range(ROW_CHUNKS):
        ref[:, j, :] = packed[:, j * LANES:(j + 1) * LANES]


def _load_row_tiles(ref):
    return jnp.concatenate([ref[:, j, :] for j in range(ROW_CHUNKS)], axis=1)


def _outproj_body(ya_ref, yh_ref, x_ref, ga_ref, gh_ref, wo_ref, bd_ref, gm_ref, wrh_ref, wrl_ref, brt_ref,
                  x1_ref, h2_ref, rt_ref, rtt_ref):
    ya = ya_ref[...]
    yan = ya * lax.rsqrt(_group_sumsq(ya, bd_ref[...]) * (1.0 / HEAD_DIM) + EPS) * ga_ref[...]
    yh = yh_ref[...]
    tm = yh.shape[1]
    yh3 = yh.reshape(D_HYENA // HYENA_HEAD, HYENA_HEAD, tm)
    ms = jnp.mean(yh3 * yh3, axis=1, keepdims=True)
    yhn = (yh3 * lax.rsqrt(ms + EPS)).reshape(D_HYENA, tm) * gh_ref[...]
    mix = (jnp.dot(yan.astype(BF16), wo_ref[:D_ATTN, :], preferred_element_type=F32)
           + jnp.dot(yhn.T.astype(BF16), wo_ref[D_ATTN:, :], preferred_element_type=F32))
    x1 = x_ref[...] + mix
    x1_ref[...] = x1
    h2 = x1 * lax.rsqrt(jnp.mean(x1 * x1, axis=-1, keepdims=True) + EPS) * gm_ref[...]
    _store_row_tiles(h2_ref, _pack_bf16_halves(h2))
    hi = h2.astype(BF16)
    lo = (h2 - hi.astype(F32)).astype(BF16)
    lg = (jnp.dot(hi, wrh_ref[...], preferred_element_type=F32)
          + jnp.dot(lo, wrh_ref[...], preferred_element_type=F32)
          + jnp.dot(hi, wrl_ref[...], preferred_element_type=F32)) + brt_ref[...]
    route = _route_lanes(lg)
    rt_ref[...] = route
    rtt_ref[...] = route.T[:8]


def _outproj(ya, yht, x, ga, gh, wo, bd, gm, wrh, wrl, brt):
    B, S, D = x.shape
    tm = TM_PROJ
    full = lambda a: pl.BlockSpec(a.shape, lambda b, i: (0,) * a.ndim)
    return pl.pallas_call(
        _outproj_body,
        grid=(B, S // tm),
        in_specs=[
            pl.BlockSpec((None, tm, D_ATTN), lambda b, i: (b, i, 0)),
            pl.BlockSpec((None, D_HYENA, tm), lambda b, i: (b, 0, i)),
            pl.BlockSpec((None, tm, D), lambda b, i: (b, i, 0)),
            full(ga), full(gh), full(wo), full(bd), full(gm), full(wrh), full(wrl), full(brt),
        ],
        out_specs=[
            pl.BlockSpec((None, tm, D), lambda b, i: (b, i, 0)),
            pl.BlockSpec((None, tm, ROW_CHUNKS, LANES), lambda b, i: (b, i, 0, 0)),
            pl.BlockSpec((None, tm, LANES), lambda b, i: (b, i, 0)),
            pl.BlockSpec((None, 8, tm), lambda b, i: (b, 0, i)),
        ],
        out_shape=[
            jax.ShapeDtypeStruct((B, S, D), F32),
            jax.ShapeDtypeStruct((B, S, ROW_CHUNKS, LANES), jnp.uint32),
            jax.ShapeDtypeStruct((B, S, LANES), F32),
            jax.ShapeDtypeStruct((B, 8, S), F32),
        ],
        compiler_params=_cparams(("parallel", "parallel")),
        name="outproj",
    )(ya, yht, x, ga, gh, wo, bd, gm, wrh, wrl, brt)


def _moe_body(be_ref, ra_ref, h2_hbm, wg_ref, wu_ref, wd_ref, y_hbm,
              wg_s, wu_s, wd_s, xbuf, ybuf, sem_in, sem_out, *, n_tok, n_rows):
    i = pl.program_id(0)
    nb = pl.num_programs(0)
    T = xbuf.shape[1]
    slot = i % 2

    def issue_gathers(blk, sl):
        for r in range(T):
            tok = ra_ref[blk * T + r] & (n_tok - 1)
            pltpu.make_async_copy(h2_hbm.at[tok], xbuf.at[sl, r], sem_in.at[sl]).start(priority=r % 2)

    def issue_scatters(blk, sl, spare):
        for r in range(T):
            dst = jnp.where(spare, n_rows + r, ra_ref[blk * T + r])
            pltpu.make_async_copy(ybuf.at[sl, r], y_hbm.at[dst], sem_out.at[sl]).start(priority=r % 2)

    def block_in_wait(sl):
        pltpu.make_async_copy(h2_hbm.at[pl.ds(0, T)], xbuf.at[sl], sem_in.at[sl]).wait()

    def block_out_wait(sl):
        pltpu.make_async_copy(ybuf.at[sl], y_hbm.at[pl.ds(0, T)], sem_out.at[sl]).wait()

    @pl.when(i == 0)
    def _():
        ybuf[...] = jnp.zeros(ybuf.shape, ybuf.dtype)
        issue_gathers(0, 0)

    prev = be_ref[jnp.maximum(i - 1, 0)]

    @pl.when((i == 0) | (be_ref[i] != prev))
    def _():
        wg_s[...] = wg_ref[...].astype(BF16)
        wu_s[...] = wu_ref[...].astype(BF16)
        wd_s[...] = wd_ref[...].astype(BF16)

    block_in_wait(slot)
    x = _unpack_bf16_halves(_load_row_tiles(xbuf.at[slot])).astype(BF16)
    issue_gathers(jnp.minimum(i + 1, nb - 1), 1 - slot)
    issue_scatters(jnp.maximum(i - 1, 0), 1 - slot, i == 0)
    a = jnp.dot(x, wg_s[...], preferred_element_type=F32)
    b = jnp.dot(x, wu_s[...], preferred_element_type=F32)
    hmid = (a * jax.nn.sigmoid(a)) * b
    y = _pack_bf16_halves(jnp.dot(hmid.astype(BF16), wd_s[...], preferred_element_type=F32))

    @pl.when(i >= 1)
    def _():
        block_out_wait(slot)

    _store_row_tiles(ybuf.at[slot], y)

    @pl.when(i == nb - 1)
    def _():
        issue_scatters(i, slot, False)
        block_in_wait(1 - slot)
        block_out_wait(1 - slot)
        block_out_wait(slot)


def _moe_experts(block_e, row_a, h2p, w_gate, w_up, w_down):
    n_tok = h2p.shape[0]
    n_rows = row_a.shape[0]
    row = h2p.shape[1:]
    D = w_gate.shape[1]
    T = TB_MOE
    assert n_tok & (n_tok - 1) == 0
    grid_spec = pltpu.PrefetchScalarGridSpec(
        num_scalar_prefetch=2,
        grid=(row_a.shape[0] // T,),
        in_specs=[
            pl.BlockSpec(memory_space=pl.ANY),
            pl.BlockSpec((None, D, D_EXPERT), lambda i, be, ra: (be[i], 0, 0)),
            pl.BlockSpec((None, D, D_EXPERT), lambda i, be, ra: (be[i], 0, 0)),
            pl.BlockSpec((None, D_EXPERT, D), lambda i, be, ra: (be[i], 0, 0)),
        ],
        out_specs=pl.BlockSpec(memory_space=pl.ANY),
        scratch_shapes=[
            pltpu.VMEM((D, D_EXPERT), BF16), pltpu.VMEM((D, D_EXPERT), BF16), pltpu.VMEM((D_EXPERT, D), BF16),
            pltpu.VMEM((2, T) + row, jnp.uint32), pltpu.VMEM((2, T) + row, jnp.uint32),
            pltpu.SemaphoreType.DMA((2,)), pltpu.SemaphoreType.DMA((2,)),
        ],
    )
    return pl.pallas_call(
        functools.partial(_moe_body, n_tok=n_tok, n_rows=n_rows),
        grid_spec=grid_spec,
        out_shape=jax.ShapeDtypeStruct((n_rows + T,) + row, jnp.uint32),
        compiler_params=_cparams(("arbitrary",)),
        name="moe_experts",
    )(block_e, row_a, h2p, w_gate, w_up, w_down)


def _dispatch(e_flat, N):
    T = TB_MOE
    NK = N * TOP_K
    experts = jnp.arange(N_EXPERTS, dtype=jnp.int32)
    order = jnp.argsort(e_flat).astype(jnp.int32)
    onehot = (e_flat[:, None] == experts[None]).astype(jnp.int32)
    counts = jnp.sum(onehot, axis=0)
    ends = jnp.cumsum(counts)
    starts = ends - counts
    padded = (counts + T - 1) // T * T
    pends = jnp.cumsum(padded)
    pstarts = pends - padded
    n_rows = -(-(NK + N_EXPERTS * (T - 1)) // T) * T
    n_blocks = n_rows // T
    blk_start = jnp.arange(n_blocks, dtype=jnp.int32) * T
    block_e = jnp.clip(jnp.sum((pends[None, :] <= blk_start[:, None]).astype(jnp.int32), axis=1),
                       0, N_EXPERTS - 1)
    oh_b = (block_e[:, None] == experts[None]).astype(jnp.int32)
    base = jnp.sum(oh_b * (starts - pstarts)[None], axis=1) + blk_start
    end_b = jnp.sum(oh_b * ends[None], axis=1)
    lane = jnp.arange(T, dtype=jnp.int32)[None]
    src = base[:, None] + lane
    pad_id = NK + blk_start[:, None] + lane - end_b[:, None]
    row_a = jnp.where(src < end_b[:, None], order[jnp.clip(src, 0, NK - 1)], pad_id)
    return block_e.astype(jnp.int32), row_a.reshape(n_rows).astype(jnp.int32)


def _final_body(x1_ref, y0_ref, y1_ref, rt_ref, p_ref, gp_ref, wg_ref, bg_ref, wp_ref, gf_ref, o_ref):
    w0 = rt_ref[:, 2:3]
    w1 = rt_ref[:, 3:4]
    y0 = _unpack_bf16_halves(_load_row_tiles(y0_ref))
    y1 = _unpack_bf16_halves(_load_row_tiles(y1_ref))
    x2 = x1_ref[...] + (y0 * w0 + y1 * w1)
    hp = x2 * lax.rsqrt(jnp.mean(x2 * x2, axis=-1, keepdims=True) + EPS) * gp_ref[...]
    gate = jax.nn.sigmoid(jnp.dot(hp.astype(BF16), wg_ref[...], preferred_element_type=F32) + bg_ref[...])
    pe = jnp.dot(p_ref[...].astype(BF16), wp_ref[...], preferred_element_type=F32)
    x3 = x2 + pe * gate
    o_ref[...] = x3 * lax.rsqrt(jnp.mean(x3 * x3, axis=-1, keepdims=True) + EPS) * gf_ref[...]


def _final(x1, y, route, p, gp, wg, bg, wp, gf):
    N, D = x1.shape
    tm = TM_PROJ
    row = lambda w: pl.BlockSpec((tm, w), lambda i: (i, 0))
    full = lambda a: pl.BlockSpec(a.shape, lambda i: (0,) * a.ndim)
    y0, y1 = y, y
    return pl.pallas_call(
        _final_body,
        grid=(N // tm,),
        in_specs=[row(D), pl.BlockSpec((tm, ROW_CHUNKS, LANES), lambda i: (i, 0, 0)),
                  pl.BlockSpec((tm, ROW_CHUNKS, LANES), lambda i: (i + N // tm, 0, 0)),
                  row(LANES), row(p.shape[1]),
                  full(gp), full(wg), full(bg), full(wp), full(gf)],
        out_specs=row(D),
        out_shape=jax.ShapeDtypeStruct((N, D), F32),
        compiler_params=_cparams(("parallel",)),
        name="ple_final",
    )(x1, y0, y1, route, p, gp, wg, bg, wp, gf)


def kernel(x, p, g_mix, w_in, q_gain, k_gain, conv_w, conv_b, w_f1, b_f1, freq1, w_f2, b_f2, freq2, w_f3, filt_bias, g_attn_out, g_hyena_out, w_out, g_moe, w_group, b_group, w_router, b_router, w_gate, w_up, w_down, g_ple, w_ple_gate, b_ple_gate, w_ple, g_final):
    B, S, D = x.shape
    N = B * S
    assert p.shape[0] == 1 and S == (FFT_N1 // 2) * FFT_N2 and B % 2 == 0
    i = 0
    cst = _dft_constants()
    cos, sin = _rope_tables(S)
    bd = _block_diag_ones(D_ATTN, HEAD_DIM)

    n_qkv = D_ATTN + 2 * D_KV
    wqkv = w_in[i][:, :n_qkv].astype(BF16)
    wut = w_in[i][:, n_qkv:].T.astype(BF16)
    q, kw, vw, ut = _inproj(x, g_mix[i][None], wqkv, wut, bd,
                            jnp.tile(q_gain[i], N_HEADS)[None], jnp.tile(k_gain[i], N_KV_HEADS)[None], cos, sin)

    ya = _attention(q, kw, vw)

    circ = _filter_gen(S, w_f1[i], b_f1[i], freq1[i], w_f2[i], b_f2[i], freq2[i], w_f3[i])
    hspec = _filter_fft(circ, cst)
    hspec = hspec.reshape(2, D_HYENA, FFT_N1, 2 * LANES)
    du = ut.shape[1]
    u4 = ut.reshape(B, du, S // LANES, LANES)
    par_u = jnp.broadcast_to(jnp.concatenate([conv_w[i], conv_b[i][None]], 0)[:, :, None], (4, du, LANES))
    fb = jnp.broadcast_to(filt_bias[i][:, :, None], (2, D_HYENA, LANES))
    yht = _hyena(u4, par_u, fb, hspec, cst)

    wrt = jnp.zeros((D, LANES), F32).at[:, :N_GROUPS].set(w_group[i]).at[:, N_GROUPS:N_GROUPS + N_EXPERTS].set(w_router[i])
    brt = jnp.zeros((1, LANES), F32).at[0, :N_GROUPS].set(b_group[i]).at[0, N_GROUPS:N_GROUPS + N_EXPERTS].set(b_router[i])
    wrh = wrt.astype(BF16)
    wrl = (wrt - wrh.astype(F32)).astype(BF16)
    x1, h2, route, route_t = _outproj(ya, yht, x, g_attn_out[i][None], g_hyena_out[i][:, None],
                                      w_out[i].astype(BF16), bd, g_moe[i][None], wrh, wrl, brt)

    e_flat = jnp.transpose(route_t[:, :TOP_K], (1, 0, 2)).reshape(TOP_K * N).astype(jnp.int32)
    block_e, row_a = _dispatch(e_flat, N)
    y = _moe_experts(block_e, row_a, h2.reshape(N, ROW_CHUNKS, LANES), w_gate[i], w_up[i], w_down[i])

    out = _final(x1.reshape(N, D), y, route.reshape(N, LANES), p[i].reshape(N, -1), g_ple[i][None],
                 w_ple_gate[i].astype(BF16), b_ple_gate[i][None], w_ple[i].astype(BF16), g_final[None])
    return out.reshape(B, S, D)
```

```python
import functools
import math

import numpy as np
import jax
import jax.numpy as jnp
from jax import lax
from jax.experimental import pallas as pl
from jax.experimental.pallas import tpu as pltpu

F32 = jnp.float32
BF16 = jnp.bfloat16

D_MODEL = 1024
EPS = 1e-6
GRID_W = 64
N_HEADS = 8
N_KV_HEADS = 2
HEAD_DIM = 64
D_ATTN = N_HEADS * HEAD_DIM
D_KV = N_KV_HEADS * HEAD_DIM
ROPE_THETA = 10000.0
D_HYENA = 512
HYENA_HEAD = 64
FILTER_EMB = 33
FAST_DECAY_PCT = 0.3
SLOW_DECAY_PCT = 1.5
DECAY_TARGET = 1e-2
N_GROUPS = 4
EXPERTS_PER_GROUP = 8
N_EXPERTS = N_GROUPS * EXPERTS_PER_GROUP
TOP_K = 2
D_EXPERT = 512

LANES = 128
MXU_TILE = 256
FFT_N1 = 64
FFT_N2 = 128
VMEM_LIMIT = 56 * 1024 * 1024

TM_PROJ = 512
TQ_ATTN = 256
C_HY = 32
ROW_CHUNKS = D_MODEL // 2 // LANES
SEQ_UNROLL = 32
TB_MOE = 256


def _cparams(sem):
    return pltpu.CompilerParams(dimension_semantics=sem, vmem_limit_bytes=VMEM_LIMIT)


def _rope_tables(S):
    half = HEAD_DIM // 2
    t = jnp.arange(S, dtype=F32)
    r_idx = jnp.floor(t / GRID_W)
    c_idx = t - r_idx * GRID_W
    inv = ROPE_THETA ** (-jnp.arange(0, half, 2, dtype=F32) / half)
    ang_r = r_idx[:, None] * inv[None]
    ang_c = c_idx[:, None] * inv[None]
    cos_h = jnp.concatenate([jnp.cos(ang_r), jnp.cos(ang_r), jnp.cos(ang_c), jnp.cos(ang_c)], axis=-1)
    sin_h = jnp.concatenate([-jnp.sin(ang_r), jnp.sin(ang_r), -jnp.sin(ang_c), jnp.sin(ang_c)], axis=-1)
    return jnp.tile(cos_h, (1, 2)), jnp.tile(sin_h, (1, 2))


def _dft_constants():
    n1, n2 = FFT_N1, FFT_N2
    n = n1 * n2
    a = np.arange(n1)
    ang = 2.0 * np.pi * np.outer(a, a) / n1
    far, fai = np.cos(ang), -np.sin(ang)
    hlf = n1 // 2
    ma = np.block([[far[:, :hlf], -fai[:, :hlf]], [fai[:, :hlf], far[:, :hlf]]])
    maf = np.concatenate([far, fai], axis=0)
    b = np.arange(n2)
    angt = 2.0 * np.pi * np.outer(a, b) / n
    tw = np.concatenate([np.cos(angt), -np.sin(angt)], axis=1)
    angb = 2.0 * np.pi * np.outer(b, b) / n2
    fbr, fbi = np.cos(angb), -np.sin(angb)
    g = np.block([[fbr, fbi], [-fbi, fbr]])
    ginv = np.block([[fbr, -fbi], [fbi, fbr]])
    minv_r = np.concatenate([far[:hlf], -fai[:hlf]], axis=0) / n
    minv_i = np.concatenate([fai[:hlf], far[:hlf]], axis=0) / n
    f = lambda m: jnp.asarray(m.astype(np.float32))
    return dict(ma=f(ma), maf=f(maf), tw=f(tw), g=f(g), ginv=f(ginv), minv_r=f(minv_r), minv_i=f(minv_i))


def _block_diag_ones(width, group):
    i = np.arange(width) // group
    return jnp.asarray((i[:, None] == i[None, :]).astype(np.float32)).astype(BF16)


def _group_sumsq(a, bd):
    sq = a * a
    hi = sq.astype(BF16)
    lo = (sq - hi.astype(F32)).astype(BF16)
    w = min(a.shape[-1], MXU_TILE)
    return jnp.concatenate(
        [jnp.dot(hi[:, c:c + w], bd[c:c + w, c:c + w], preferred_element_type=F32)
         + jnp.dot(lo[:, c:c + w], bd[c:c + w, c:c + w], preferred_element_type=F32)
         for c in range(0, a.shape[-1], w)], axis=-1)


def _head_norm_rope(a, gain, bd, cos, sin):
    width = a.shape[-1]
    n = a * lax.rsqrt(_group_sumsq(a, bd) * (1.0 / HEAD_DIM) + EPS) * gain
    rep = width // LANES
    if rep > 1:
        cos = jnp.concatenate([cos] * rep, axis=-1)
        sin = jnp.concatenate([sin] * rep, axis=-1)
    fwd = pltpu.roll(n, width - 16, 1)
    bwd = pltpu.roll(n, 16, 1)
    lane = lax.broadcasted_iota(jnp.int32, n.shape, 1)
    sw = jnp.where((lane % 32) < 16, fwd, bwd)
    return n * cos + sw * sin


def _inproj_body(x_ref, g_ref, wqkv_ref, wu_ref, bd_ref, qg_ref, kg_ref, cos_ref, sin_ref,
                 q_ref, kw_ref, vw_ref, ut_ref):
    x = x_ref[...]
    h = x * lax.rsqrt(jnp.mean(x * x, axis=-1, keepdims=True) + EPS) * g_ref[...]
    hb = h.astype(BF16)
    qkv = jnp.dot(hb, wqkv_ref[...], preferred_element_type=F32)
    cos = cos_ref[...]
    sin = sin_ref[...]
    bd = bd_ref[...]
    q = _head_norm_rope(qkv[:, :D_ATTN], qg_ref[...], bd, cos, sin)
    q_ref[...] = (q * (HEAD_DIM ** -0.5 * math.log2(math.e))).astype(BF16)
    k = _head_norm_rope(qkv[:, D_ATTN:D_ATTN + D_KV], kg_ref[...], bd[:D_KV, :D_KV], cos, sin)
    kt = k.T.astype(BF16)
    zero = jnp.zeros((HEAD_DIM, kt.shape[1]), BF16)
    for h in range(N_KV_HEADS):
        kh = kt[h * HEAD_DIM:(h + 1) * HEAD_DIM]
        kw_ref[h, 0, :HEAD_DIM] = kh
        kw_ref[h, 0, HEAD_DIM:] = zero
        kw_ref[h, 1, :HEAD_DIM] = zero
        kw_ref[h, 1, HEAD_DIM:] = kh
    v = qkv[:, D_ATTN + D_KV:]
    vr = pltpu.roll(v, HEAD_DIM, 1)
    first = lax.broadcasted_iota(jnp.int32, v.shape, 1) < HEAD_DIM
    vw_ref[0, 0] = jnp.where(first, v, 1.0).astype(BF16)
    vw_ref[0, 1] = jnp.where(first, 1.0, vr).astype(BF16)
    vw_ref[1, 0] = jnp.where(first, vr, 1.0).astype(BF16)
    vw_ref[1, 1] = jnp.where(first, 1.0, v).astype(BF16)
    ut_ref[...] = lax.dot_general(wu_ref[...], hb, (((1,), (1,)), ((), ())),
                                  preferred_element_type=F32)


def _inproj(x, g_mix, wqkv, wut, bd, qg, kg, cos, sin):
    B, S, D = x.shape
    tm = TM_PROJ
    du = wut.shape[0]
    full = lambda shape: pl.BlockSpec(shape, lambda b, i: (0,) * len(shape))
    return pl.pallas_call(
        _inproj_body,
        grid=(B, S // tm),
        in_specs=[
            pl.BlockSpec((None, tm, D), lambda b, i: (b, i, 0)),
            full((1, D)), full(wqkv.shape), full(wut.shape), full(bd.shape),
            full((1, D_ATTN)), full((1, D_KV)),
            pl.BlockSpec((tm, LANES), lambda b, i: (i, 0)),
            pl.BlockSpec((tm, LANES), lambda b, i: (i, 0)),
        ],
        out_specs=[
            pl.BlockSpec((None, tm, D_ATTN), lambda b, i: (b, i, 0)),
            pl.BlockSpec((None, N_KV_HEADS, 2, LANES, tm), lambda b, i: (b, 0, 0, 0, i)),
            pl.BlockSpec((None, N_KV_HEADS, 2, tm, LANES), lambda b, i: (b, 0, 0, i, 0)),
            pl.BlockSpec((None, du, tm), lambda b, i: (b, 0, i)),
        ],
        out_shape=[
            jax.ShapeDtypeStruct((B, S, D_ATTN), BF16),
            jax.ShapeDtypeStruct((B, N_KV_HEADS, 2, LANES, S), BF16),
            jax.ShapeDtypeStruct((B, N_KV_HEADS, 2, S, LANES), BF16),
            jax.ShapeDtypeStruct((B, du, S), F32),
        ],
        compiler_params=_cparams(("parallel", "parallel")),
        name="inproj",
    )(x, g_mix, wqkv, wut, bd, qg, kg, cos, sin)


def _attn_body(q_ref, kw_ref, vw_ref, o_ref):

    def one_head(q, kw, vw):
        s = jnp.dot(q, kw, preferred_element_type=F32)
        m = jnp.max(s, axis=-1, keepdims=True)
        p = jnp.exp2(s - m).astype(BF16)
        return jnp.dot(p, vw, preferred_element_type=F32)

    for pair in range(D_ATTN // LANES):
        h = pair // (N_HEADS // N_KV_HEADS // 2)
        q = q_ref[:, pair * LANES:(pair + 1) * LANES]
        oe = one_head(q, kw_ref[h, 0], vw_ref[h, 0])
        oo = one_head(q, kw_ref[h, 1], vw_ref[h, 1])
        first = lax.broadcasted_iota(jnp.int32, oe.shape, 1) < HEAD_DIM
        num = jnp.where(first, oe, oo)
        den = jnp.where(first, pltpu.roll(oe, HEAD_DIM, 1), pltpu.roll(oo, HEAD_DIM, 1))
        o_ref[:, pair * LANES:(pair + 1) * LANES] = num / den


def _attention(q, kw, vw):
    B, S, _ = q.shape
    tq = TQ_ATTN
    return pl.pallas_call(
        _attn_body,
        grid=(B, S // tq),
        in_specs=[
            pl.BlockSpec((None, tq, D_ATTN), lambda b, i: (b, i, 0)),
            pl.BlockSpec((None, N_KV_HEADS, 2, LANES, S), lambda b, i: (b, 0, 0, 0, 0)),
            pl.BlockSpec((None, N_KV_HEADS, 2, S, LANES), lambda b, i: (b, 0, 0, 0, 0)),
        ],
        out_specs=pl.BlockSpec((None, tq, D_ATTN), lambda b, i: (b, i, 0)),
        out_shape=jax.ShapeDtypeStruct((B, S, D_ATTN), F32),
        compiler_params=_cparams(("parallel", "arbitrary")),
        name="attention",
    )(q, kw, vw)


def _fwd_twiddle_store(y, tw_ref, s1_ref, row0):
    yr, yi = y[:FFT_N1], y[FFT_N1:]
    twr, twi = tw_ref[:, :LANES], tw_ref[:, LANES:]
    s1_ref[pl.ds(row0, FFT_N1), :LANES] = (yr * twr - yi * twi).astype(BF16)
    s1_ref[pl.ds(row0, FFT_N1), LANES:] = (yr * twi + yi * twr).astype(BF16)


def _filtfft_body(x_ref, maf_ref, tw_ref, g_ref, h_ref, s1_ref):
    C = x_ref.shape[0]

    def step_a(c, carry):
        y = jnp.dot(maf_ref[...], x_ref[c].astype(BF16), preferred_element_type=F32)
        _fwd_twiddle_store(y, tw_ref, s1_ref, pl.multiple_of(c * FFT_N1, FFT_N1))
        return carry

    lax.fori_loop(0, C, step_a, 0, unroll=SEQ_UNROLL)
    z = jnp.dot(s1_ref[...], g_ref[...], preferred_element_type=F32)
    h_ref[...] = z.reshape(C, FFT_N1, 2 * LANES)


def _filter_fft(circ, cst):
    n_seq = circ.shape[0]
    C = C_HY
    full = lambda a: pl.BlockSpec(a.shape, lambda i: (0,) * a.ndim)
    maf, tw, g = cst["maf"].astype(BF16), cst["tw"], cst["g"].astype(BF16)
    return pl.pallas_call(
        _filtfft_body,
        grid=(n_seq // C,),
        in_specs=[pl.BlockSpec((C, FFT_N1, FFT_N2), lambda i: (i, 0, 0)), full(maf), full(tw), full(g)],
        out_specs=pl.BlockSpec((C, FFT_N1, 2 * LANES), lambda i: (i, 0, 0)),
        out_shape=jax.ShapeDtypeStruct((n_seq, FFT_N1, 2 * LANES), F32),
        scratch_shapes=[pltpu.VMEM((C * FFT_N1, 2 * LANES), BF16)],
        compiler_params=_cparams(("parallel",)),
        name="filter_fft",
    )(circ, maf, tw, g)


def _short_conv(x, par_ref, c):
    rows, lanes = x.shape
    a_i = lax.broadcasted_iota(jnp.int32, x.shape, 0)
    b_i = lax.broadcasted_iota(jnp.int32, x.shape, 1)
    l1 = pltpu.roll(x, 1, 1)
    l2 = pltpu.roll(l1, 1, 0)
    prev = jnp.where(b_i == 0, l2, l1)
    prev = jnp.where((a_i == 0) & (b_i == 0), 0.0, prev)
    r1 = pltpu.roll(x, lanes - 1, 1)
    r2 = pltpu.roll(r1, rows - 1, 0)
    nxt = jnp.where(b_i == lanes - 1, r2, r1)
    nxt = jnp.where((a_i == rows - 1) & (b_i == lanes - 1), 0.0, nxt)
    w0 = par_ref[0, pl.ds(c, 1), :]
    w1 = par_ref[1, pl.ds(c, 1), :]
    w2 = par_ref[2, pl.ds(c, 1), :]
    cb = par_ref[3, pl.ds(c, 1), :]
    return cb + prev * w0 + x * w1 + nxt * w2


def _hyena_body(v_ref, x1_ref, x2_ref, pv_ref, p1_ref, p2_ref, fb_ref, h_ref,
                ma_ref, tw_ref, g_ref, ginv_ref, mir_ref, mii_ref,
                o_ref, s1_ref, s2_ref, vc_ref, z1_ref):
    C = v_ref.shape[1]
    half = FFT_N1 // 2

    def spectral(order):
        z = jnp.dot(s1_ref[...], g_ref[...], preferred_element_type=F32)
        hs = h_ref[order].reshape(C * FFT_N1, 2 * LANES)
        zr, zi = z[:, :LANES], z[:, LANES:]
        hr, hi = hs[:, :LANES], hs[:, LANES:]
        pb = jnp.concatenate([zr * hr - zi * hi, zr * hi + zi * hr], axis=1).astype(BF16)
        s2_ref[...] = jnp.dot(pb, ginv_ref[...], preferred_element_type=F32)

    def inv_a(c):
        row0 = pl.multiple_of(c * FFT_N1, FFT_N1)
        y = s2_ref[pl.ds(row0, FFT_N1), :]
        yr, yi = y[:, :LANES], y[:, LANES:]
        twr, twi = tw_ref[:, :LANES], tw_ref[:, LANES:]
        ur = (yr * twr + yi * twi).astype(BF16)
        ui = (yi * twr - yr * twi).astype(BF16)
        out = (jnp.dot(mir_ref[...], ur, preferred_element_type=F32)
               + jnp.dot(mii_ref[...], ui, preferred_element_type=F32))
        return out[:half], out[half:]

    def fwd_a(c, xr, xi):
        xs = jnp.concatenate([xr, xi], axis=0).astype(BF16)
        y = jnp.dot(ma_ref[...], xs, preferred_element_type=F32)
        _fwd_twiddle_store(y, tw_ref, s1_ref, pl.multiple_of(c * FFT_N1, FFT_N1))

    def pass1_a(c, carry):
        vr = _short_conv(v_ref[0, c], pv_ref, c)
        vi = _short_conv(v_ref[1, c], pv_ref, c)
        vc_ref[0, c] = vr
        vc_ref[1, c] = vi
        fwd_a(c, vr, vi)
        return carry

    def pass1_b(c, carry):
        cr, ci = inv_a(c)
        bias = fb_ref[0, pl.ds(c, 1), :]
        zr = _short_conv(x1_ref[0, c], p1_ref, c) * (cr + bias * vc_ref[0, c])
        zi = _short_conv(x1_ref[1, c], p1_ref, c) * (ci + bias * vc_ref[1, c])
        z1_ref[0, c] = zr
        z1_ref[1, c] = zi
        fwd_a(c, zr, zi)
        return carry

    def pass2_b(c, carry):
        cr, ci = inv_a(c)
        bias = fb_ref[1, pl.ds(c, 1), :]
        vc_ref[0, c] = _short_conv(x2_ref[0, c], p2_ref, c) * (cr + bias * z1_ref[0, c])
        vc_ref[1, c] = _short_conv(x2_ref[1, c], p2_ref, c) * (ci + bias * z1_ref[1, c])
        return carry

    lax.fori_loop(0, C, pass1_a, 0, unroll=SEQ_UNROLL)
    spectral(0)
    lax.fori_loop(0, C, pass1_b, 0, unroll=SEQ_UNROLL)
    spectral(1)
    lax.fori_loop(0, C, pass2_b, 0, unroll=SEQ_UNROLL)
    for b2 in range(2):
        for a in range(vc_ref.shape[2]):
            o_ref[b2, :, a * LANES:(a + 1) * LANES] = vc_ref[b2, :, a, :]


def _hyena(u4, par_u, fb, hspec, cst):
    B = u4.shape[0]
    C = C_HY
    J = D_HYENA // C
    rows = u4.shape[2]
    full = lambda a: pl.BlockSpec(a.shape, lambda j, p: (0,) * a.ndim)
    ma, g, ginv = cst["ma"].astype(BF16), cst["g"].astype(BF16), cst["ginv"].astype(BF16)
    mir, mii = cst["minv_r"].astype(BF16), cst["minv_i"].astype(BF16)
    tw = cst["tw"]
    u_spec = lambda k: pl.BlockSpec((2, C, rows, LANES), lambda j, p, k=k: (p, j + k * J, 0, 0))
    par_spec = lambda k: pl.BlockSpec((4, C, LANES), lambda j, p, k=k: (0, j + k * J, 0))
    return pl.pallas_call(
        _hyena_body,
        grid=(J, B // 2),
        in_specs=[
            u_spec(0), u_spec(1), u_spec(2), par_spec(0), par_spec(1), par_spec(2),
            pl.BlockSpec((2, C, LANES), lambda j, p: (0, j, 0)),
            pl.BlockSpec((2, C, FFT_N1, 2 * LANES), lambda j, p: (0, j, 0, 0)),
            full(ma), full(tw), full(g), full(ginv), full(mir), full(mii),
        ],
        out_specs=pl.BlockSpec((2, C, rows * LANES), lambda j, p: (p, j, 0)),
        out_shape=jax.ShapeDtypeStruct((B, D_HYENA, rows * LANES), F32),
        scratch_shapes=[
            pltpu.VMEM((C * FFT_N1, 2 * LANES), BF16),
            pltpu.VMEM((C * FFT_N1, 2 * LANES), F32),
            pltpu.VMEM((2, C, rows, LANES), F32),
            pltpu.VMEM((2, C, rows, LANES), F32),
        ],
        compiler_params=_cparams(("parallel", "arbitrary")),
        name="hyena",
    )(u4, u4, u4, par_u, par_u, par_u, fb, hspec, ma, tw, g, ginv, mir, mii)


def _dot3(a, b):
    ah = a.astype(BF16)
    al = (a - ah.astype(F32)).astype(BF16)
    bh = b.astype(BF16)
    bl = (b - bh.astype(F32)).astype(BF16)
    return (jnp.dot(ah, bh, preferred_element_type=F32) + jnp.dot(al, bh, preferred_element_type=F32)
            + jnp.dot(ah, bl, preferred_element_type=F32))


def _filtgen_body(zt_ref, w1_ref, b1_ref, f1_ref, w2_ref, b2_ref, f2_ref, w3f_ref, w3b_ref, ad_ref, tt_ref,
                  o_ref, hid_ref):
    L = hid_ref.shape[1] // 2

    @pl.when(pl.program_id(0) == 0)
    def _():
        h1 = jnp.sin(f1_ref[...] * (_dot3(w1_ref[...], zt_ref[...]) + b1_ref[...]))
        hid_ref[...] = jnp.sin(f2_ref[...] * (_dot3(w2_ref[...], h1) + b2_ref[...]))

    ad = ad_ref[...]
    hf = _dot3(w3f_ref[...], hid_ref[:, :L]) * jnp.exp(-ad * tt_ref[:, :L])
    hb = _dot3(w3b_ref[...], hid_ref[:, L:]) * jnp.exp(-ad * tt_ref[:, L:])
    hf = hf / (jnp.sum(jnp.abs(hf), axis=-1, keepdims=True) + EPS)
    hb = hb / (jnp.sum(jnp.abs(hb), axis=-1, keepdims=True) + EPS)
    first = lax.broadcasted_iota(jnp.int32, hb.shape, 1) == 0
    cf = hf + jnp.where(first, hb, 0.0)
    cb = jnp.where(first, 0.0, hb)
    half = L // LANES
    for a in range(half):
        o_ref[:, a, :] = cf[:, a * LANES:(a + 1) * LANES]
        o_ref[:, half + a, :] = cb[:, a * LANES:(a + 1) * LANES]


def _filter_gen(L, w_f1, b_f1, freq1, w_f2, b_f2, freq2, w_f3):
    bands = (FILTER_EMB - 1) // 2
    t = jnp.linspace(0.0, 1.0, L, dtype=F32)[:, None]
    w = (2.0 * math.pi / L) * jnp.arange(L, dtype=F32)[:, None]
    f = jnp.linspace(1e-4, bands - 1, bands, dtype=F32)[None]
    zf = f * w
    z = jnp.concatenate([t, jnp.cos(zf), -jnp.sin(zf)], axis=-1)
    back = lambda a: jnp.roll(a[::-1], 1, axis=0)
    kpad = 48
    zt = jnp.pad(jnp.concatenate([z, back(z)], axis=0).T, ((0, kpad - FILTER_EMB), (0, 0)))
    tt = jnp.concatenate([t, back(t)], axis=0).T
    w1t = jnp.pad(w_f1.T, ((0, 0), (0, kpad - FILTER_EMB)))
    w3 = w_f3.reshape(-1, 2, 2, D_HYENA)
    w3f = jnp.transpose(w3[:, :, 0], (1, 2, 0)).reshape(2 * D_HYENA, -1)
    w3b = jnp.transpose(w3[:, :, 1], (1, 2, 0)).reshape(2 * D_HYENA, -1)
    max_decay = math.log(DECAY_TARGET) / FAST_DECAY_PCT
    min_decay = math.log(DECAY_TARGET) / SLOW_DECAY_PCT
    deltas = jnp.linspace(min_decay, max_decay, D_HYENA, dtype=F32)
    ad = jnp.tile(jnp.abs(deltas), 2)[:, None]
    col = lambda v: v[:, None]
    R = 128
    n_rows = 2 * D_HYENA
    full = lambda a: pl.BlockSpec(a.shape, lambda i: (0,) * a.ndim)
    rows = lambda a: pl.BlockSpec((R, a.shape[1]), lambda i: (i, 0))
    args = (zt, w1t, col(b_f1), col(freq1), w_f2.T, col(b_f2), col(freq2))
    return pl.pallas_call(
        _filtgen_body,
        grid=(n_rows // R,),
        in_specs=[full(a) for a in args] + [rows(w3f), rows(w3b), rows(ad), full(tt)],
        out_specs=pl.BlockSpec((R, 2 * L // LANES, LANES), lambda i: (i, 0, 0)),
        out_shape=jax.ShapeDtypeStruct((n_rows, 2 * L // LANES, LANES), F32),
        scratch_shapes=[pltpu.VMEM((w_f2.shape[1], 2 * L), F32)],
        compiler_params=_cparams(("arbitrary",)),
        name="filter_gen",
    )(*args, w3f, w3b, ad, tt)


def _route_lanes(lg):
    neg = -1e30
    lane = lax.broadcasted_iota(jnp.int32, lg.shape, 1)
    gmask = lane < N_GROUPS
    gl = jnp.where(gmask, lg, neg)
    gm = jnp.max(gl, axis=-1, keepdims=True)
    gsum = jnp.sum(jnp.where(gmask, jnp.exp(gl - gm), 0.0), axis=-1, keepdims=True)
    g_top = 1.0 / gsum
    g_sel = jnp.min(jnp.where(gl == gm, lane, LANES), axis=-1, keepdims=True)
    lo = N_GROUPS + EXPERTS_PER_GROUP * g_sel
    el = jnp.where((lane >= lo) & (lane < lo + EXPERTS_PER_GROUP), lg, neg)
    m1 = jnp.max(el, axis=-1, keepdims=True)
    i1 = jnp.min(jnp.where(el == m1, lane, LANES), axis=-1, keepdims=True)
    el2 = jnp.where(lane == i1, neg, el)
    m2 = jnp.max(el2, axis=-1, keepdims=True)
    i2 = jnp.min(jnp.where(el2 == m2, lane, LANES), axis=-1, keepdims=True)
    d = jnp.exp(m2 - m1)
    p1 = 1.0 / (1.0 + d)
    p2 = d / (1.0 + d)
    e1 = (i1 - N_GROUPS).astype(F32)
    e2 = (i2 - N_GROUPS).astype(F32)
    return jnp.where(lane == 0, e1, jnp.where(lane == 1, e2, jnp.where(lane == 2, g_top * p1,
                     jnp.where(lane == 3, g_top * p2, 0.0))))


def _pack_bf16_halves(a):
    w = a.shape[1] // 2
    bits = pltpu.bitcast(a.astype(BF16).astype(F32), jnp.uint32)
    return (bits[:, :w] >> 16) | (bits[:, w:] & jnp.uint32(0xFFFF0000))


def _unpack_bf16_halves(wd):
    lo = pltpu.bitcast(wd << 16, F32)
    hi = pltpu.bitcast(wd & jnp.uint32(0xFFFF0000), F32)
    return jnp.concatenate([lo, hi], axis=1)


def _store_row_tiles(ref, packed):
    for j in range(ROW_CHUNKS):
        ref[:, j, :] = packed[:, j * LANES:(j + 1) * LANES]


def _load_row_tiles(ref):
    return jnp.concatenate([ref[:, j, :] for j in range(ROW_CHUNKS)], axis=1)


def _outproj_body(ya_ref, yh_ref, x_ref, ga_ref, gh_ref, wo_ref, bd_ref, gm_ref, wrh_ref, wrl_ref, brt_ref,
                  x1_ref, h2_ref, rt_ref, rtt_ref):
    ya = ya_ref[...]
    yan = ya * lax.rsqrt(_group_sumsq(ya, bd_ref[...]) * (1.0 / HEAD_DIM) + EPS) * ga_ref[...]
    yh = yh_ref[...]
    tm = yh.shape[1]
    yh3 = yh.reshape(D_HYENA // HYENA_HEAD, HYENA_HEAD, tm)
    ms = jnp.mean(yh3 * yh3, axis=1, keepdims=True)
    yhn = (yh3 * lax.rsqrt(ms + EPS)).reshape(D_HYENA, tm) * gh_ref[...]
    mix = (jnp.dot(yan.astype(BF16), wo_ref[:D_ATTN, :], preferred_element_type=F32)
           + jnp.dot(yhn.T.astype(BF16), wo_ref[D_ATTN:, :], preferred_element_type=F32))
    x1 = x_ref[...] + mix
    x1_ref[...] = x1
    h2 = x1 * lax.rsqrt(jnp.mean(x1 * x1, axis=-1, keepdims=True) + EPS) * gm_ref[...]
    _store_row_tiles(h2_ref, _pack_bf16_halves(h2))
    hi = h2.astype(BF16)
    lo = (h2 - hi.astype(F32)).astype(BF16)
    lg = (jnp.dot(hi, wrh_ref[...], preferred_element_type=F32)
          + jnp.dot(lo, wrh_ref[...], preferred_element_type=F32)
          + jnp.dot(hi, wrl_ref[...], preferred_element_type=F32)) + brt_ref[...]
    route = _route_lanes(lg)
    rt_ref[...] = route
    rtt_ref[...] = route.T[:8]


def _outproj(ya, yht, x, ga, gh, wo, bd, gm, wrh, wrl, brt):
    B, S, D = x.shape
    tm = TM_PROJ
    full = lambda a: pl.BlockSpec(a.shape, lambda b, i: (0,) * a.ndim)
    return pl.pallas_call(
        _outproj_body,
        grid=(B, S // tm),
        in_specs=[
            pl.BlockSpec((None, tm, D_ATTN), lambda b, i: (b, i, 0)),
            pl.BlockSpec((None, D_HYENA, tm), lambda b, i: (b, 0, i)),
            pl.BlockSpec((None, tm, D), lambda b, i: (b, i, 0)),
            full(ga), full(gh), full(wo), full(bd), full(gm), full(wrh), full(wrl), full(brt),
        ],
        out_specs=[
            pl.BlockSpec((None, tm, D), lambda b, i: (b, i, 0)),
            pl.BlockSpec((None, tm, ROW_CHUNKS, LANES), lambda b, i: (b, i, 0, 0)),
            pl.BlockSpec((None, tm, LANES), lambda b, i: (b, i, 0)),
            pl.BlockSpec((None, 8, tm), lambda b, i: (b, 0, i)),
        ],
        out_shape=[
            jax.ShapeDtypeStruct((B, S, D), F32),
            jax.ShapeDtypeStruct((B, S, ROW_CHUNKS, LANES), jnp.uint32),
            jax.ShapeDtypeStruct((B, S, LANES), F32),
            jax.ShapeDtypeStruct((B, 8, S), F32),
        ],
        compiler_params=_cparams(("parallel", "parallel")),
        name="outproj",
    )(ya, yht, x, ga, gh, wo, bd, gm, wrh, wrl, brt)


def _moe_body(be_ref, ra_ref, h2_hbm, wg_ref, wu_ref, wd_ref, y_hbm,
              wg_s, wu_s, wd_s, xbuf, ybuf, sem_in, sem_out, *, n_tok, n_rows):
    i = pl.program_id(0)
    nb = pl.num_programs(0)
    T = xbuf.shape[1]
    slot = i % 2

    def issue_gathers(blk, sl):
        for r in range(T):
            tok = ra_ref[blk * T + r] & (n_tok - 1)
            pltpu.make_async_copy(h2_hbm.at[tok], xbuf.at[sl, r], sem_in.at[sl]).start(priority=r % 2)

    def issue_scatters(blk, sl, spare):
        for r in range(T):
            dst = jnp.where(spare, n_rows + r, ra_ref[blk * T + r])
            pltpu.make_async_copy(ybuf.at[sl, r], y_hbm.at[dst], sem_out.at[sl]).start(priority=r % 2)

    def block_in_wait(sl):
        pltpu.make_async_copy(h2_hbm.at[pl.ds(0, T)], xbuf.at[sl], sem_in.at[sl]).wait()

    def block_out_wait(sl):
        pltpu.make_async_copy(ybuf.at[sl], y_hbm.at[pl.ds(0, T)], sem_out.at[sl]).wait()

    @pl.when(i == 0)
    def _():
        ybuf[...] = jnp.zeros(ybuf.shape, ybuf.dtype)
        issue_gathers(0, 0)

    prev = be_ref[jnp.maximum(i - 1, 0)]

    @pl.when((i == 0) | (be_ref[i] != prev))
    def _():
        wg_s[...] = wg_ref[...].astype(BF16)
        wu_s[...] = wu_ref[...].astype(BF16)
        wd_s[...] = wd_ref[...].astype(BF16)

    block_in_wait(slot)
    x = _unpack_bf16_halves(_load_row_tiles(xbuf.at[slot])).astype(BF16)
    issue_gathers(jnp.minimum(i + 1, nb - 1), 1 - slot)
    issue_scatters(jnp.maximum(i - 1, 0), 1 - slot, i == 0)
    a = jnp.dot(x, wg_s[...], preferred_element_type=F32)
    b = jnp.dot(x, wu_s[...], preferred_element_type=F32)
    hmid = (a * jax.nn.sigmoid(a)) * b
    y = _pack_bf16_halves(jnp.dot(hmid.astype(BF16), wd_s[...], preferred_element_type=F32))

    @pl.when(i >= 1)
    def _():
        block_out_wait(slot)

    _store_row_tiles(ybuf.at[slot], y)

    @pl.when(i == nb - 1)
    def _():
        issue_scatters(i, slot, False)
        block_in_wait(1 - slot)
        block_out_wait(1 - slot)
        block_out_wait(slot)


def _moe_experts(block_e, row_a, h2p, w_gate, w_up, w_down):
    n_tok = h2p.shape[0]
    n_rows = row_a.shape[0]
    row = h2p.shape[1:]
    D = w_gate.shape[1]
    T = TB_MOE
    assert n_tok & (n_tok - 1) == 0
    grid_spec = pltpu.PrefetchScalarGridSpec(
        num_scalar_prefetch=2,
        grid=(row_a.shape[0] // T,),
        in_specs=[
            pl.BlockSpec(memory_space=pl.ANY),
            pl.BlockSpec((None, D, D_EXPERT), lambda i, be, ra: (be[i], 0, 0)),
            pl.BlockSpec((None, D, D_EXPERT), lambda i, be, ra: (be[i], 0, 0)),
            pl.BlockSpec((None, D_EXPERT, D), lambda i, be, ra: (be[i], 0, 0)),
        ],
        out_specs=pl.BlockSpec(memory_space=pl.ANY),
        scratch_shapes=[
            pltpu.VMEM((D, D_EXPERT), BF16), pltpu.VMEM((D, D_EXPERT), BF16), pltpu.VMEM((D_EXPERT, D), BF16),
            pltpu.VMEM((2, T) + row, jnp.uint32), pltpu.VMEM((2, T) + row, jnp.uint32),
            pltpu.SemaphoreType.DMA((2,)), pltpu.SemaphoreType.DMA((2,)),
        ],
    )
    return pl.pallas_call(
        functools.partial(_moe_body, n_tok=n_tok, n_rows=n_rows),
        grid_spec=grid_spec,
        out_shape=jax.ShapeDtypeStruct((n_rows + T,) + row, jnp.uint32),
        compiler_params=_cparams(("arbitrary",)),
        name="moe_experts",
    )(block_e, row_a, h2p, w_gate, w_up, w_down)


def _dispatch(e_flat, N):
    T = TB_MOE
    NK = N * TOP_K
    experts = jnp.arange(N_EXPERTS, dtype=jnp.int32)
    order = jnp.argsort(e_flat).astype(jnp.int32)
    onehot = (e_flat[:, None] == experts[None]).astype(jnp.int32)
    counts = jnp.sum(onehot, axis=0)
    ends = jnp.cumsum(counts)
    starts = ends - counts
    padded = (counts + T - 1) // T * T
    pends = jnp.cumsum(padded)
    pstarts = pends - padded
    n_rows = -(-(NK + N_EXPERTS * (T - 1)) // T) * T
    n_blocks = n_rows // T
    blk_start = jnp.arange(n_blocks, dtype=jnp.int32) * T
    block_e = jnp.clip(jnp.sum((pends[None, :] <= blk_start[:, None]).astype(jnp.int32), axis=1),
                       0, N_EXPERTS - 1)
    oh_b = (block_e[:, None] == experts[None]).astype(jnp.int32)
    base = jnp.sum(oh_b * (starts - pstarts)[None], axis=1) + blk_start
    end_b = jnp.sum(oh_b * ends[None], axis=1)
    lane = jnp.arange(T, dtype=jnp.int32)[None]
    src = base[:, None] + lane
    pad_id = NK + blk_start[:, None] + lane - end_b[:, None]
    row_a = jnp.where(src < end_b[:, None], order[jnp.clip(src, 0, NK - 1)], pad_id)
    return block_e.astype(jnp.int32), row_a.reshape(n_rows).astype(jnp.int32)


def _final_body(x1_ref, y0_ref, y1_ref, rt_ref, p_ref, gp_ref, wg_ref, bg_ref, wp_ref, gf_ref, o_ref):
    w0 = rt_ref[:, 2:3]
    w1 = rt_ref[:, 3:4]
    y0 = _unpack_bf16_halves(_load_row_tiles(y0_ref))
    y1 = _unpack_bf16_halves(_load_row_tiles(y1_ref))
    x2 = x1_ref[...] + (y0 * w0 + y1 * w1)
    hp = x2 * lax.rsqrt(jnp.mean(x2 * x2, axis=-1, keepdims=True) + EPS) * gp_ref[...]
    gate = jax.nn.sigmoid(jnp.dot(hp.astype(BF16), wg_ref[...], preferred_element_type=F32) + bg_ref[...])
    pe = jnp.dot(p_ref[...].astype(BF16), wp_ref[...], preferred_element_type=F32)
    x3 = x2 + pe * gate
    o_ref[...] = x3 * lax.rsqrt(jnp.mean(x3 * x3, axis=-1, keepdims=True) + EPS) * gf_ref[...]


def _final(x1, y, route, p, gp, wg, bg, wp, gf):
    N, D = x1.shape
    tm = TM_PROJ
    row = lambda w: pl.BlockSpec((tm, w), lambda i: (i, 0))
    full = lambda a: pl.BlockSpec(a.shape, lambda i: (0,) * a.ndim)
    y0, y1 = y, y
    return pl.pallas_call(
        _final_body,
        grid=(N // tm,),
        in_specs=[row(D), pl.BlockSpec((tm, ROW_CHUNKS, LANES), lambda i: (i, 0, 0)),
                  pl.BlockSpec((tm, ROW_CHUNKS, LANES), lambda i: (i + N // tm, 0, 0)),
                  row(LANES), row(p.shape[1]),
                  full(gp), full(wg), full(bg), full(wp), full(gf)],
        out_specs=row(D),
        out_shape=jax.ShapeDtypeStruct((N, D), F32),
        compiler_params=_cparams(("parallel",)),
        name="ple_final",
    )(x1, y0, y1, route, p, gp, wg, bg, wp, gf)


def kernel(x, p, g_mix, w_in, q_gain, k_gain, conv_w, conv_b, w_f1, b_f1, freq1, w_f2, b_f2, freq2, w_f3, filt_bias, g_attn_out, g_hyena_out, w_out, g_moe, w_group, b_group, w_router, b_router, w_gate, w_up, w_down, g_ple, w_ple_gate, b_ple_gate, w_ple, g_final):
    B, S, D = x.shape
    N = B * S
    assert p.shape[0] == 1 and S == (FFT_N1 // 2) * FFT_N2 and B % 2 == 0
    i = 0
    cst = _dft_constants()
    cos, sin = _rope_tables(S)
    bd = _block_diag_ones(D_ATTN, HEAD_DIM)

    n_qkv = D_ATTN + 2 * D_KV
    wqkv = w_in[i][:, :n_qkv].astype(BF16)
    wut = w_in[i][:, n_qkv:].T.astype(BF16)
    q, kw, vw, ut = _inproj(x, g_mix[i][None], wqkv, wut, bd,
                            jnp.tile(q_gain[i], N_HEADS)[None], jnp.tile(k_gain[i], N_KV_HEADS)[None], cos, sin)

    ya = _attention(q, kw, vw)

    circ = _filter_gen(S, w_f1[i], b_f1[i], freq1[i], w_f2[i], b_f2[i], freq2[i], w_f3[i])
    hspec = _filter_fft(circ, cst)
    hspec = hspec.reshape(2, D_HYENA, FFT_N1, 2 * LANES)
    du = ut.shape[1]
    u4 = ut.reshape(B, du, S // LANES, LANES)
    par_u = jnp.broadcast_to(jnp.concatenate([conv_w[i], conv_b[i][None]], 0)[:, :, None], (4, du, LANES))
    fb = jnp.broadcast_to(filt_bias[i][:, :, None], (2, D_HYENA, LANES))
    yht = _hyena(u4, par_u, fb, hspec, cst)

    wrt = jnp.zeros((D, LANES), F32).at[:, :N_GROUPS].set(w_group[i]).at[:, N_GROUPS:N_GROUPS + N_EXPERTS].set(w_router[i])
    brt = jnp.zeros((1, LANES), F32).at[0, :N_GROUPS].set(b_group[i]).at[0, N_GROUPS:N_GROUPS + N_EXPERTS].set(b_router[i])
    wrh = wrt.astype(BF16)
    wrl = (wrt - wrh.astype(F32)).astype(BF16)
    x1, h2, route, route_t = _outproj(ya, yht, x, g_attn_out[i][None], g_hyena_out[i][:, None],
                                      w_out[i].astype(BF16), bd, g_moe[i][None], wrh, wrl, brt)

    e_flat = jnp.transpose(route_t[:, :TOP_K], (1, 0, 2)).reshape(TOP_K * N).astype(jnp.int32)
    block_e, row_a = _dispatch(e_flat, N)
    y = _moe_experts(block_e, row_a, h2.reshape(N, ROW_CHUNKS, LANES), w_gate[i], w_up[i], w_down[i])

    out = _final(x1.reshape(N, D), y, route.reshape(N, LANES), p[i].reshape(N, -1), g_ple[i][None],
                 w_ple_gate[i].astype(BF16), b_ple_gate[i][None], w_ple[i].astype(BF16), g_final[None])
    return out.reshape(B, S, D)
```

```python
import functools
import math

import numpy as np
import jax
import jax.numpy as jnp
from jax import lax
from jax.experimental import pallas as pl
from jax.experimental.pallas import tpu as pltpu

F32 = jnp.float32
BF16 = jnp.bfloat16

D_MODEL = 1024
EPS = 1e-6
GRID_W = 64
N_HEADS = 8
N_KV_HEADS = 2
HEAD_DIM = 64
D_ATTN = N_HEADS * HEAD_DIM
D_KV = N_KV_HEADS * HEAD_DIM
ROPE_THETA = 10000.0
D_HYENA = 512
HYENA_HEAD = 64
FILTER_EMB = 33
FAST_DECAY_PCT = 0.3
SLOW_DECAY_PCT = 1.5
DECAY_TARGET = 1e-2
N_GROUPS = 4
EXPERTS_PER_GROUP = 8
N_EXPERTS = N_GROUPS * EXPERTS_PER_GROUP
TOP_K = 2
D_EXPERT = 512

LANES = 128
MXU_TILE = 256
FFT_N1 = 64
FFT_N2 = 128
VMEM_LIMIT = 56 * 1024 * 1024

TM_PROJ = 512
TQ_ATTN = 256
C_HY = 32
ROW_CHUNKS = D_MODEL // 2 // LANES
SEQ_UNROLL = 32
TB_MOE = 256


def _cparams(sem):
    return pltpu.CompilerParams(dimension_semantics=sem, vmem_limit_bytes=VMEM_LIMIT)


def _rope_tables(S):
    half = HEAD_DIM // 2
    t = jnp.arange(S, dtype=F32)
    r_idx = jnp.floor(t / GRID_W)
    c_idx = t - r_idx * GRID_W
    inv = ROPE_THETA ** (-jnp.arange(0, half, 2, dtype=F32) / half)
    ang_r = r_idx[:, None] * inv[None]
    ang_c = c_idx[:, None] * inv[None]
    cos_h = jnp.concatenate([jnp.cos(ang_r), jnp.cos(ang_r), jnp.cos(ang_c), jnp.cos(ang_c)], axis=-1)
    sin_h = jnp.concatenate([-jnp.sin(ang_r), jnp.sin(ang_r), -jnp.sin(ang_c), jnp.sin(ang_c)], axis=-1)
    return jnp.tile(cos_h, (1, 2)), jnp.tile(sin_h, (1, 2))


def _dft_constants():
    n1, n2 = FFT_N1, FFT_N2
    n = n1 * n2
    a = np.arange(n1)
    ang = 2.0 * np.pi * np.outer(a, a) / n1
    far, fai = np.cos(ang), -np.sin(ang)
    hlf = n1 // 2
    ma = np.block([[far[:, :hlf], -fai[:, :hlf]], [fai[:, :hlf], far[:, :hlf]]])
    maf = np.concatenate([far, fai], axis=0)
    b = np.arange(n2)
    angt = 2.0 * np.pi * np.outer(a, b) / n
    tw = np.concatenate([np.cos(angt), -np.sin(angt)], axis=1)
    angb = 2.0 * np.pi * np.outer(b, b) / n2
    fbr, fbi = np.cos(angb), -np.sin(angb)
    g = np.block([[fbr, fbi], [-fbi, fbr]])
    ginv = np.block([[fbr, -fbi], [fbi, fbr]])
    minv_r = np.concatenate([far[:hlf], -fai[:hlf]], axis=0) / n
    minv_i = np.concatenate([fai[:hlf], far[:hlf]], axis=0) / n
    f = lambda m: jnp.asarray(m.astype(np.float32))
    return dict(ma=f(ma), maf=f(maf), tw=f(tw), g=f(g), ginv=f(ginv), minv_r=f(minv_r), minv_i=f(minv_i))


def _block_diag_ones(width, group):
    i = np.arange(width) // group
    return jnp.asarray((i[:, None] == i[None, :]).astype(np.float32)).astype(BF16)


def _group_sumsq(a, bd):
    sq = a * a
    hi = sq.astype(BF16)
    lo = (sq - hi.astype(F32)).astype(BF16)
    w = min(a.shape[-1], MXU_TILE)
    return jnp.concatenate(
        [jnp.dot(hi[:, c:c + w], bd[c:c + w, c:c + w], preferred_element_type=F32)
         + jnp.dot(lo[:, c:c + w], bd[c:c + w, c:c + w], preferred_element_type=F32)
         for c in range(0, a.shape[-1], w)], axis=-1)


def _head_norm_rope(a, gain, bd, cos, sin):
    width = a.shape[-1]
    n = a * lax.rsqrt(_group_sumsq(a, bd) * (1.0 / HEAD_DIM) + EPS) * gain
    rep = width // LANES
    if rep > 1:
        cos = jnp.concatenate([cos] * rep, axis=-1)
        sin = jnp.concatenate([sin] * rep, axis=-1)
    fwd = pltpu.roll(n, width - 16, 1)
    bwd = pltpu.roll(n, 16, 1)
    lane = lax.broadcasted_iota(jnp.int32, n.shape, 1)
    sw = jnp.where((lane % 32) < 16, fwd, bwd)
    return n * cos + sw * sin


def _inproj_body(x_ref, g_ref, wqkv_ref, wu_ref, bd_ref, qg_ref, kg_ref, cos_ref, sin_ref,
                 q_ref, kw_ref, vw_ref, ut_ref):
    x = x_ref[...]
    h = x * lax.rsqrt(jnp.mean(x * x, axis=-1, keepdims=True) + EPS) * g_ref[...]
    hb = h.astype(BF16)
    qkv = jnp.dot(hb, wqkv_ref[...], preferred_element_type=F32)
    cos = cos_ref[...]
    sin = sin_ref[...]
    bd = bd_ref[...]
    q = _head_norm_rope(qkv[:, :D_ATTN], qg_ref[...], bd, cos, sin)
    q_ref[...] = (q * (HEAD_DIM ** -0.5 * math.log2(math.e))).astype(BF16)
    k = _head_norm_rope(qkv[:, D_ATTN:D_ATTN + D_KV], kg_ref[...], bd[:D_KV, :D_KV], cos, sin)
    kt = k.T.astype(BF16)
    zero = jnp.zeros((HEAD_DIM, kt.shape[1]), BF16)
    for h in range(N_KV_HEADS):
        kh = kt[h * HEAD_DIM:(h + 1) * HEAD_DIM]
        kw_ref[h, 0, :HEAD_DIM] = kh
        kw_ref[h, 0, HEAD_DIM:] = zero
        kw_ref[h, 1, :HEAD_DIM] = zero
        kw_ref[h, 1, HEAD_DIM:] = kh
    v = qkv[:, D_ATTN + D_KV:]
    vr = pltpu.roll(v, HEAD_DIM, 1)
    first = lax.broadcasted_iota(jnp.int32, v.shape, 1) < HEAD_DIM
    vw_ref[0, 0] = jnp.where(first, v, 1.0).astype(BF16)
    vw_ref[0, 1] = jnp.where(first, 1.0, vr).astype(BF16)
    vw_ref[1, 0] = jnp.where(first, vr, 1.0).astype(BF16)
    vw_ref[1, 1] = jnp.where(first, 1.0, v).astype(BF16)
    ut_ref[...] = lax.dot_general(wu_ref[...], hb, (((1,), (1,)), ((), ())),
                                  preferred_element_type=F32)


def _inproj(x, g_mix, wqkv, wut, bd, qg, kg, cos, sin):
    B, S, D = x.shape
    tm = TM_PROJ
    du = wut.shape[0]
    full = lambda shape: pl.BlockSpec(shape, lambda b, i: (0,) * len(shape))
    return pl.pallas_call(
        _inproj_body,
        grid=(B, S // tm),
        in_specs=[
            pl.BlockSpec((None, tm, D), lambda b, i: (b, i, 0)),
            full((1, D)), full(wqkv.shape), full(wut.shape), full(bd.shape),
            full((1, D_ATTN)), full((1, D_KV)),
            pl.BlockSpec((tm, LANES), lambda b, i: (i, 0)),
            pl.BlockSpec((tm, LANES), lambda b, i: (i, 0)),
        ],
        out_specs=[
            pl.BlockSpec((None, tm, D_ATTN), lambda b, i: (b, i, 0)),
            pl.BlockSpec((None, N_KV_HEADS, 2, LANES, tm), lambda b, i: (b, 0, 0, 0, i)),
            pl.BlockSpec((None, N_KV_HEADS, 2, tm, LANES), lambda b, i: (b, 0, 0, i, 0)),
            pl.BlockSpec((None, du, tm), lambda b, i: (b, 0, i)),
        ],
        out_shape=[
            jax.ShapeDtypeStruct((B, S, D_ATTN), BF16),
            jax.ShapeDtypeStruct((B, N_KV_HEADS, 2, LANES, S), BF16),
            jax.ShapeDtypeStruct((B, N_KV_HEADS, 2, S, LANES), BF16),
            jax.ShapeDtypeStruct((B, du, S), F32),
        ],
        compiler_params=_cparams(("parallel", "parallel")),
        name="inproj",
    )(x, g_mix, wqkv, wut, bd, qg, kg, cos, sin)


def _attn_body(q_ref, kw_ref, vw_ref, o_ref):

    def one_head(q, kw, vw):
        s = jnp.dot(q, kw, preferred_element_type=F32)
        m = jnp.max(s, axis=-1, keepdims=True)
        p = jnp.exp2(s - m).astype(BF16)
        return jnp.dot(p, vw, preferred_element_type=F32)

    for pair in range(D_ATTN // LANES):
        h = pair // (N_HEADS // N_KV_HEADS // 2)
        q = q_ref[:, pair * LANES:(pair + 1) * LANES]
        oe = one_head(q, kw_ref[h, 0], vw_ref[h, 0])
        oo = one_head(q, kw_ref[h, 1], vw_ref[h, 1])
        first = lax.broadcasted_iota(jnp.int32, oe.shape, 1) < HEAD_DIM
        num = jnp.where(first, oe, oo)
        den = jnp.where(first, pltpu.roll(oe, HEAD_DIM, 1), pltpu.roll(oo, HEAD_DIM, 1))
        o_ref[:, pair * LANES:(pair + 1) * LANES] = num / den


def _attention(q, kw, vw):
    B, S, _ = q.shape
    tq = TQ_ATTN
    return pl.pallas_call(
        _attn_body,
        grid=(B, S // tq),
        in_specs=[
            pl.BlockSpec((None, tq, D_ATTN), lambda b, i: (b, i, 0)),
            pl.BlockSpec((None, N_KV_HEADS, 2, LANES, S), lambda b, i: (b, 0, 0, 0, 0)),
            pl.BlockSpec((None, N_KV_HEADS, 2, S, LANES), lambda b, i: (b, 0, 0, 0, 0)),
        ],
        out_specs=pl.BlockSpec((None, tq, D_ATTN), lambda b, i: (b, i, 0)),
        out_shape=jax.ShapeDtypeStruct((B, S, D_ATTN), F32),
        compiler_params=_cparams(("parallel", "arbitrary")),
        name="attention",
    )(q, kw, vw)


def _fwd_twiddle_store(y, tw_ref, s1_ref, row0):
    yr, yi = y[:FFT_N1], y[FFT_N1:]
    twr, twi = tw_ref[:, :LANES], tw_ref[:, LANES:]
    s1_ref[pl.ds(row0, FFT_N1), :LANES] = (yr * twr - yi * twi).astype(BF16)
    s1_ref[pl.ds(row0, FFT_N1), LANES:] = (yr * twi + yi * twr).astype(BF16)


def _filtfft_body(x_ref, maf_ref, tw_ref, g_ref, h_ref, s1_ref):
    C = x_ref.shape[0]

    def step_a(c, carry):
        y = jnp.dot(maf_ref[...], x_ref[c].astype(BF16), preferred_element_type=F32)
        _fwd_twiddle_store(y, tw_ref, s1_ref, pl.multiple_of(c * FFT_N1, FFT_N1))
        return carry

    lax.fori_loop(0, C, step_a, 0, unroll=SEQ_UNROLL)
    z = jnp.dot(s1_ref[...], g_ref[...], preferred_element_type=F32)
    h_ref[...] = z.reshape(C, FFT_N1, 2 * LANES)


def _filter_fft(circ, cst):
    n_seq = circ.shape[0]
    C = C_HY
    full = lambda a: pl.BlockSpec(a.shape, lambda i: (0,) * a.ndim)
    maf, tw, g = cst["maf"].astype(BF16), cst["tw"], cst["g"].astype(BF16)
    return pl.pallas_call(
        _filtfft_body,
        grid=(n_seq // C,),
        in_specs=[pl.BlockSpec((C, FFT_N1, FFT_N2), lambda i: (i, 0, 0)), full(maf), full(tw), full(g)],
        out_specs=pl.BlockSpec((C, FFT_N1, 2 * LANES), lambda i: (i, 0, 0)),
        out_shape=jax.ShapeDtypeStruct((n_seq, FFT_N1, 2 * LANES), F32),
        scratch_shapes=[pltpu.VMEM((C * FFT_N1, 2 * LANES), BF16)],
        compiler_params=_cparams(("parallel",)),
        name="filter_fft",
    )(circ, maf, tw, g)


def _short_conv(x, par_ref, c):
    rows, lanes = x.shape
    a_i = lax.broadcasted_iota(jnp.int32, x.shape, 0)
    b_i = lax.broadcasted_iota(jnp.int32, x.shape, 1)
    l1 = pltpu.roll(x, 1, 1)
    l2 = pltpu.roll(l1, 1, 0)
    prev = jnp.where(b_i == 0, l2, l1)
    prev = jnp.where((a_i == 0) & (b_i == 0), 0.0, prev)
    r1 = pltpu.roll(x, lanes - 1, 1)
    r2 = pltpu.roll(r1, rows - 1, 0)
    nxt = jnp.where(b_i == lanes - 1, r2, r1)
    nxt = jnp.where((a_i == rows - 1) & (b_i == lanes - 1), 0.0, nxt)
    w0 = par_ref[0, pl.ds(c, 1), :]
    w1 = par_ref[1, pl.ds(c, 1), :]
    w2 = par_ref[2, pl.ds(c, 1), :]
    cb = par_ref[3, pl.ds(c, 1), :]
    return cb + prev * w0 + x * w1 + nxt * w2


def _hyena_body(v_ref, x1_ref, x2_ref, pv_ref, p1_ref, p2_ref, fb_ref, h_ref,
                ma_ref, tw_ref, g_ref, ginv_ref, mir_ref, mii_ref,
                o_ref, s1_ref, s2_ref, vc_ref, z1_ref):
    C = v_ref.shape[1]
    half = FFT_N1 // 2

    def spectral(order):
        z = jnp.dot(s1_ref[...], g_ref[...], preferred_element_type=F32)
        hs = h_ref[order].reshape(C * FFT_N1, 2 * LANES)
        zr, zi = z[:, :LANES], z[:, LANES:]
        hr, hi = hs[:, :LANES], hs[:, LANES:]
        pb = jnp.concatenate([zr * hr - zi * hi, zr * hi + zi * hr], axis=1).astype(BF16)
        s2_ref[...] = jnp.dot(pb, ginv_ref[...], preferred_element_type=F32)

    def inv_a(c):
        row0 = pl.multiple_of(c * FFT_N1, FFT_N1)
        y = s2_ref[pl.ds(row0, FFT_N1), :]
        yr, yi = y[:, :LANES], y[:, LANES:]
        twr, twi = tw_ref[:, :LANES], tw_ref[:, LANES:]
        ur = (yr * twr + yi * twi).astype(BF16)
        ui = (yi * twr - yr * twi).astype(BF16)
        out = (jnp.dot(mir_ref[...], ur, preferred_element_type=F32)
               + jnp.dot(mii_ref[...], ui, preferred_element_type=F32))
        return out[:half], out[half:]

    def fwd_a(c, xr, xi):
        xs = jnp.concatenate([xr, xi], axis=0).astype(BF16)
        y = jnp.dot(ma_ref[...], xs, preferred_element_type=F32)
        _fwd_twiddle_store(y, tw_ref, s1_ref, pl.multiple_of(c * FFT_N1, FFT_N1))

    def pass1_a(c, carry):
        vr = _short_conv(v_ref[0, c], pv_ref, c)
        vi = _short_conv(v_ref[1, c], pv_ref, c)
        vc_ref[0, c] = vr
        vc_ref[1, c] = vi
        fwd_a(c, vr, vi)
        return carry

    def pass1_b(c, carry):
        cr, ci = inv_a(c)
        bias = fb_ref[0, pl.ds(c, 1), :]
        zr = _short_conv(x1_ref[0, c], p1_ref, c) * (cr + bias * vc_ref[0, c])
        zi = _short_conv(x1_ref[1, c], p1_ref, c) * (ci + bias * vc_ref[1, c])
        z1_ref[0, c] = zr
        z1_ref[1, c] = zi
        fwd_a(c, zr, zi)
        return carry

    def pass2_b(c, carry):
        cr, ci = inv_a(c)
        bias = fb_ref[1, pl.ds(c, 1), :]
        vc_ref[0, c] = _short_conv(x2_ref[0, c], p2_ref, c) * (cr + bias * z1_ref[0, c])
        vc_ref[1, c] = _short_conv(x2_ref[1, c], p2_ref, c) * (ci + bias * z1_ref[1, c])
        return carry

    lax.fori_loop(0, C, pass1_a, 0, unroll=SEQ_UNROLL)
    spectral(0)
    lax.fori_loop(0, C, pass1_b, 0, unroll=SEQ_UNROLL)
    spectral(1)
    lax.fori_loop(0, C, pass2_b, 0, unroll=SEQ_UNROLL)
    for b2 in range(2):
        for a in range(vc_ref.shape[2]):
            o_ref[b2, :, a * LANES:(a + 1) * LANES] = vc_ref[b2, :, a, :]


def _hyena(u4, par_u, fb, hspec, cst):
    B = u4.shape[0]
    C = C_HY
    J = D_HYENA // C
    rows = u4.shape[2]
    full = lambda a: pl.BlockSpec(a.shape, lambda j, p: (0,) * a.ndim)
    ma, g, ginv = cst["ma"].astype(BF16), cst["g"].astype(BF16), cst["ginv"].astype(BF16)
    mir, mii = cst["minv_r"].astype(BF16), cst["minv_i"].astype(BF16)
    tw = cst["tw"]
    u_spec = lambda k: pl.BlockSpec((2, C, rows, LANES), lambda j, p, k=k: (p, j + k * J, 0, 0))
    par_spec = lambda k: pl.BlockSpec((4, C, LANES), lambda j, p, k=k: (0, j + k * J, 0))
    return pl.pallas_call(
        _hyena_body,
        grid=(J, B // 2),
        in_specs=[
            u_spec(0), u_spec(1), u_spec(2), par_spec(0), par_spec(1), par_spec(2),
            pl.BlockSpec((2, C, LANES), lambda j, p: (0, j, 0)),
            pl.BlockSpec((2, C, FFT_N1, 2 * LANES), lambda j, p: (0, j, 0, 0)),
            full(ma), full(tw), full(g), full(ginv), full(mir), full(mii),
        ],
        out_specs=pl.BlockSpec((2, C, rows * LANES), lambda j, p: (p, j, 0)),
        out_shape=jax.ShapeDtypeStruct((B, D_HYENA, rows * LANES), F32),
        scratch_shapes=[
            pltpu.VMEM((C * FFT_N1, 2 * LANES), BF16),
            pltpu.VMEM((C * FFT_N1, 2 * LANES), F32),
            pltpu.VMEM((2, C, rows, LANES), F32),
            pltpu.VMEM((2, C, rows, LANES), F32),
        ],
        compiler_params=_cparams(("parallel", "arbitrary")),
        name="hyena",
    )(u4, u4, u4, par_u, par_u, par_u, fb, hspec, ma, tw, g, ginv, mir, mii)


def _dot3(a, b):
    ah = a.astype(BF16)
    al = (a - ah.astype(F32)).astype(BF16)
    bh = b.astype(BF16)
    bl = (b - bh.astype(F32)).astype(BF16)
    return (jnp.dot(ah, bh, preferred_element_type=F32) + jnp.dot(al, bh, preferred_element_type=F32)
            + jnp.dot(ah, bl, preferred_element_type=F32))


def _filtgen_body(zt_ref, w1_ref, b1_ref, f1_ref, w2_ref, b2_ref, f2_ref, w3f_ref, w3b_ref, ad_ref, tt_ref,
                  o_ref, hid_ref):
    L = hid_ref.shape[1] // 2

    @pl.when(pl.program_id(0) == 0)
    def _():
        h1 = jnp.sin(f1_ref[...] * (_dot3(w1_ref[...], zt_ref[...]) + b1_ref[...]))
        hid_ref[...] = jnp.sin(f2_ref[...] * (_dot3(w2_ref[...], h1) + b2_ref[...]))

    ad = ad_ref[...]
    hf = _dot3(w3f_ref[...], hid_ref[:, :L]) * jnp.exp(-ad * tt_ref[:, :L])
    hb = _dot3(w3b_ref[...], hid_ref[:, L:]) * jnp.exp(-ad * tt_ref[:, L:])
    hf = hf / (jnp.sum(jnp.abs(hf), axis=-1, keepdims=True) + EPS)
    hb = hb / (jnp.sum(jnp.abs(hb), axis=-1, keepdims=True) + EPS)
    first = lax.broadcasted_iota(jnp.int32, hb.shape, 1) == 0
    cf = hf + jnp.where(first, hb, 0.0)
    cb = jnp.where(first, 0.0, hb)
    half = L // LANES
    for a in range(half):
        o_ref[:, a, :] = cf[:, a * LANES:(a + 1) * LANES]
        o_ref[:, half + a, :] = cb[:, a * LANES:(a + 1) * LANES]


def _filter_gen(L, w_f1, b_f1, freq1, w_f2, b_f2, freq2, w_f3):
    bands = (FILTER_EMB - 1) // 2
    t = jnp.linspace(0.0, 1.0, L, dtype=F32)[:, None]
    w = (2.0 * math.pi / L) * jnp.arange(L, dtype=F32)[:, None]
    f = jnp.linspace(1e-4, bands - 1, bands, dtype=F32)[None]
    zf = f * w
    z = jnp.concatenate([t, jnp.cos(zf), -jnp.sin(zf)], axis=-1)
    back = lambda a: jnp.roll(a[::-1], 1, axis=0)
    kpad = 48
    zt = jnp.pad(jnp.concatenate([z, back(z)], axis=0).T, ((0, kpad - FILTER_EMB), (0, 0)))
    tt = jnp.concatenate([t, back(t)], axis=0).T
    w1t = jnp.pad(w_f1.T, ((0, 0), (0, kpad - FILTER_EMB)))
    w3 = w_f3.reshape(-1, 2, 2, D_HYENA)
    w3f = jnp.transpose(w3[:, :, 0], (1, 2, 0)).reshape(2 * D_HYENA, -1)
    w3b = jnp.transpose(w3[:, :, 1], (1, 2, 0)).reshape(2 * D_HYENA, -1)
    max_decay = math.log(DECAY_TARGET) / FAST_DECAY_PCT
    min_decay = math.log(DECAY_TARGET) / SLOW_DECAY_PCT
    deltas = jnp.linspace(min_decay, max_decay, D_HYENA, dtype=F32)
    ad = jnp.tile(jnp.abs(deltas), 2)[:, None]
    col = lambda v: v[:, None]
    R = 128
    n_rows = 2 * D_HYENA
    full = lambda a: pl.BlockSpec(a.shape, lambda i: (0,) * a.ndim)
    rows = lambda a: pl.BlockSpec((R, a.shape[1]), lambda i: (i, 0))
    args = (zt, w1t, col(b_f1), col(freq1), w_f2.T, col(b_f2), col(freq2))
    return pl.pallas_call(
        _filtgen_body,
        grid=(n_rows // R,),
        in_specs=[full(a) for a in args] + [rows(w3f), rows(w3b), rows(ad), full(tt)],
        out_specs=pl.BlockSpec((R, 2 * L // LANES, LANES), lambda i: (i, 0, 0)),
        out_shape=jax.ShapeDtypeStruct((n_rows, 2 * L // LANES, LANES), F32),
        scratch_shapes=[pltpu.VMEM((w_f2.shape[1], 2 * L), F32)],
        compiler_params=_cparams(("arbitrary",)),
        name="filter_gen",
    )(*args, w3f, w3b, ad, tt)


def _route_lanes(lg):
    neg = -1e30
    lane = lax.broadcasted_iota(jnp.int32, lg.shape, 1)
    gmask = lane < N_GROUPS
    gl = jnp.where(gmask, lg, neg)
    gm = jnp.max(gl, axis=-1, keepdims=True)
    gsum = jnp.sum(jnp.where(gmask, jnp.exp(gl - gm), 0.0), axis=-1, keepdims=True)
    g_top = 1.0 / gsum
    g_sel = jnp.min(jnp.where(gl == gm, lane, LANES), axis=-1, keepdims=True)
    lo = N_GROUPS + EXPERTS_PER_GROUP * g_sel
    el = jnp.where((lane >= lo) & (lane < lo + EXPERTS_PER_GROUP), lg, neg)
    m1 = jnp.max(el, axis=-1, keepdims=True)
    i1 = jnp.min(jnp.where(el == m1, lane, LANES), axis=-1, keepdims=True)
    el2 = jnp.where(lane == i1, neg, el)
    m2 = jnp.max(el2, axis=-1, keepdims=True)
    i2 = jnp.min(jnp.where(el2 == m2, lane, LANES), axis=-1, keepdims=True)
    d = jnp.exp(m2 - m1)
    p1 = 1.0 / (1.0 + d)
    p2 = d / (1.0 + d)
    e1 = (i1 - N_GROUPS).astype(F32)
    e2 = (i2 - N_GROUPS).astype(F32)
    return jnp.where(lane == 0, e1, jnp.where(lane == 1, e2, jnp.where(lane == 2, g_top * p1,
                     jnp.where(lane == 3, g_top * p2, 0.0))))


def _pack_bf16_halves(a):
    w = a.shape[1] // 2
    bits = pltpu.bitcast(a.astype(BF16).astype(F32), jnp.uint32)
    return (bits[:, :w] >> 16) | (bits[:, w:] & jnp.uint32(0xFFFF0000))


def _unpack_bf16_halves(wd):
    lo = pltpu.bitcast(wd << 16, F32)
    hi = pltpu.bitcast(wd & jnp.uint32(0xFFFF0000), F32)
    return jnp.concatenate([lo, hi], axis=1)


def _store_row_tiles(ref, packed):
    for j in range(ROW_CHUNKS):
        ref[:, j, :] = packed[:, j * LANES:(j + 1) * LANES]


def _load_row_tiles(ref):
    return jnp.concatenate([ref[:, j, :] for j in range(ROW_CHUNKS)], axis=1)


def _outproj_body(ya_ref, yh_ref, x_ref, ga_ref, gh_ref, wo_ref, bd_ref, gm_ref, wrh_ref, wrl_ref, brt_ref,
                  x1_ref, h2_ref, rt_ref, rtt_ref):
    ya = ya_ref[...]
    yan = ya * lax.rsqrt(_group_sumsq(ya, bd_ref[...]) * (1.0 / HEAD_DIM) + EPS) * ga_ref[...]
    yh = yh_ref[...]
    tm = yh.shape[1]
    yh3 = yh.reshape(D_HYENA // HYENA_HEAD, HYENA_HEAD, tm)
    ms = jnp.mean(yh3 * yh3, axis=1, keepdims=True)
    yhn = (yh3 * lax.rsqrt(ms + EPS)).reshape(D_HYENA, tm) * gh_ref[...]
    mix = (jnp.dot(yan.astype(BF16), wo_ref[:D_ATTN, :], preferred_element_type=F32)
           + jnp.dot(yhn.T.astype(BF16), wo_ref[D_ATTN:, :], preferred_element_type=F32))
    x1 = x_ref[...] + mix
    x1_ref[...] = x1
    h2 = x1 * lax.rsqrt(jnp.mean(x1 * x1, axis=-1, keepdims=True) + EPS) * gm_ref[...]
    _store_row_tiles(h2_ref, _pack_bf16_halves(h2))
    hi = h2.astype(BF16)
    lo = (h2 - hi.astype(F32)).astype(BF16)
    lg = (jnp.dot(hi, wrh_ref[...], preferred_element_type=F32)
          + jnp.dot(lo, wrh_ref[...], preferred_element_type=F32)
          + jnp.dot(hi, wrl_ref[...], preferred_element_type=F32)) + brt_ref[...]
    route = _route_lanes(lg)
    rt_ref[...] = route
    rtt_ref[...] = route.T[:8]


def _outproj(ya, yht, x, ga, gh, wo, bd, gm, wrh, wrl, brt):
    B, S, D = x.shape
    tm = TM_PROJ
    full = lambda a: pl.BlockSpec(a.shape, lambda b, i: (0,) * a.ndim)
    return pl.pallas_call(
        _outproj_body,
        grid=(B, S // tm),
        in_specs=[
            pl.BlockSpec((None, tm, D_ATTN), lambda b, i: (b, i, 0)),
            pl.BlockSpec((None, D_HYENA, tm), lambda b, i: (b, 0, i)),
            pl.BlockSpec((None, tm, D), lambda b, i: (b, i, 0)),
            full(ga), full(gh), full(wo), full(bd), full(gm), full(wrh), full(wrl), full(brt),
        ],
        out_specs=[
            pl.BlockSpec((None, tm, D), lambda b, i: (b, i, 0)),
            pl.BlockSpec((None, tm, ROW_CHUNKS, LANES), lambda b, i: (b, i, 0, 0)),
            pl.BlockSpec((None, tm, LANES), lambda b, i: (b, i, 0)),
            pl.BlockSpec((None, 8, tm), lambda b, i: (b, 0, i)),
        ],
        out_shape=[
            jax.ShapeDtypeStruct((B, S, D), F32),
            jax.ShapeDtypeStruct((B, S, ROW_CHUNKS, LANES), jnp.uint32),
            jax.ShapeDtypeStruct((B, S, LANES), F32),
            jax.ShapeDtypeStruct((B, 8, S), F32),
        ],
        compiler_params=_cparams(("parallel", "parallel")),
        name="outproj",
    )(ya, yht, x, ga, gh, wo, bd, gm, wrh, wrl, brt)


def _moe_body(be_ref, ra_ref, nlive_ref, h2_hbm, wg_ref, wu_ref, wd_ref, y_hbm,
              wg_s, wu_s, wd_s, xbuf, ybuf, zbuf, sem_in, sem_out, sem_z, *, n_tok, n_rows):
    i = pl.program_id(0)
    nb = nlive_ref[0]
    T = xbuf.shape[1]
    slot = i % 2

    def issue_gathers(blk, sl):
        for r in range(T):
            tok = ra_ref[blk * T + r] & (n_tok - 1)
            pltpu.make_async_copy(h2_hbm.at[tok], xbuf.at[sl, r], sem_in.at[sl]).start(priority=r % 2)

    def issue_scatters(blk, sl, spare):
        for r in range(T):
            dst = jnp.where(spare, n_rows + r, ra_ref[blk * T + r])
            pltpu.make_async_copy(ybuf.at[sl, r], y_hbm.at[dst], sem_out.at[sl]).start(priority=r % 2)

    def block_in_wait(sl):
        pltpu.make_async_copy(h2_hbm.at[pl.ds(0, T)], xbuf.at[sl], sem_in.at[sl]).wait()

    def block_out_wait(sl):
        pltpu.make_async_copy(ybuf.at[sl], y_hbm.at[pl.ds(0, T)], sem_out.at[sl]).wait()

    @pl.when(i == 0)
    def _():
        ybuf[...] = jnp.zeros(ybuf.shape, ybuf.dtype)
        zbuf[...] = jnp.zeros(zbuf.shape, zbuf.dtype)
        issue_gathers(0, 0)

    @pl.when(i >= nb)
    def _():
        fill = pltpu.make_async_copy(zbuf, y_hbm.at[pl.ds(i * T, T)], sem_z.at[0])
        fill.start()
        fill.wait()

    prev = be_ref[jnp.maximum(i - 1, 0)]

    @pl.when((i == 0) | (be_ref[i] != prev))
    def _():
        wg_s[...] = wg_ref[...].astype(BF16)
        wu_s[...] = wu_ref[...].astype(BF16)
        wd_s[...] = wd_ref[...].astype(BF16)

    @pl.when(i < nb)
    def _():
        block_in_wait(slot)
        x = _unpack_bf16_halves(_load_row_tiles(xbuf.at[slot])).astype(BF16)
        issue_gathers(jnp.minimum(i + 1, nb - 1), 1 - slot)
        issue_scatters(jnp.maximum(i - 1, 0), 1 - slot, i == 0)
        a = jnp.dot(x, wg_s[...], preferred_element_type=F32)
        b = jnp.dot(x, wu_s[...], preferred_element_type=F32)
        hmid = (a * jax.nn.sigmoid(a)) * b
        y = _pack_bf16_halves(jnp.dot(hmid.astype(BF16), wd_s[...], preferred_element_type=F32))

        @pl.when(i >= 1)
        def _():
            block_out_wait(slot)

        _store_row_tiles(ybuf.at[slot], y)

        @pl.when(i == nb - 1)
        def _():
            issue_scatters(i, slot, False)
            block_in_wait(1 - slot)
            block_out_wait(1 - slot)
            block_out_wait(slot)


def _moe_experts(block_e, row_a, n_live, h2p, w_gate, w_up, w_down):
    n_tok = h2p.shape[0]
    n_rows = row_a.shape[0]
    row = h2p.shape[1:]
    D = w_gate.shape[1]
    T = TB_MOE
    assert n_tok & (n_tok - 1) == 0
    grid_spec = pltpu.PrefetchScalarGridSpec(
        num_scalar_prefetch=3,
        grid=(row_a.shape[0] // T,),
        in_specs=[
            pl.BlockSpec(memory_space=pl.ANY),
            pl.BlockSpec((None, D, D_EXPERT), lambda i, be, ra, nl: (be[i], 0, 0)),
            pl.BlockSpec((None, D, D_EXPERT), lambda i, be, ra, nl: (be[i], 0, 0)),
            pl.BlockSpec((None, D_EXPERT, D), lambda i, be, ra, nl: (be[i], 0, 0)),
        ],
        out_specs=pl.BlockSpec(memory_space=pl.ANY),
        scratch_shapes=[
            pltpu.VMEM((D, D_EXPERT), BF16), pltpu.VMEM((D, D_EXPERT), BF16), pltpu.VMEM((D_EXPERT, D), BF16),
            pltpu.VMEM((2, T) + row, jnp.uint32), pltpu.VMEM((2, T) + row, jnp.uint32),
            pltpu.VMEM((T,) + row, jnp.uint32),
            pltpu.SemaphoreType.DMA((2,)), pltpu.SemaphoreType.DMA((2,)), pltpu.SemaphoreType.DMA((1,)),
        ],
    )
    return pl.pallas_call(
        functools.partial(_moe_body, n_tok=n_tok, n_rows=n_rows),
        grid_spec=grid_spec,
        out_shape=jax.ShapeDtypeStruct((n_rows + T,) + row, jnp.uint32),
        compiler_params=_cparams(("arbitrary",)),
        name="moe_experts",
    )(block_e, row_a, n_live, h2p, w_gate, w_up, w_down)


def _dispatch(e_flat, N):
    T = TB_MOE
    NK = N * TOP_K
    experts = jnp.arange(N_EXPERTS, dtype=jnp.int32)
    order = jnp.argsort(e_flat).astype(jnp.int32)
    onehot = (e_flat[:, None] == experts[None]).astype(jnp.int32)
    counts = jnp.sum(onehot, axis=0)
    ends = jnp.cumsum(counts)
    starts = ends - counts
    padded = (counts + T - 1) // T * T
    pends = jnp.cumsum(padded)
    pstarts = pends - padded
    n_rows = -(-(NK + N_EXPERTS * (T - 1)) // T) * T
    n_blocks = n_rows // T
    blk_start = jnp.arange(n_blocks, dtype=jnp.int32) * T
    block_e = jnp.clip(jnp.sum((pends[None, :] <= blk_start[:, None]).astype(jnp.int32), axis=1),
                       0, N_EXPERTS - 1)
    oh_b = (block_e[:, None] == experts[None]).astype(jnp.int32)
    base = jnp.sum(oh_b * (starts - pstarts)[None], axis=1) + blk_start
    end_b = jnp.sum(oh_b * ends[None], axis=1)
    lane = jnp.arange(T, dtype=jnp.int32)[None]
    src = base[:, None] + lane
    pad_id = NK + blk_start[:, None] + lane - end_b[:, None]
    row_a = jnp.where(src < end_b[:, None], order[jnp.clip(src, 0, NK - 1)], pad_id)
    n_live = (pends[-1:] // T).astype(jnp.int32)
    return block_e.astype(jnp.int32), row_a.reshape(n_rows).astype(jnp.int32), n_live


def _final_body(x1_ref, y0_ref, y1_ref, rt_ref, p_ref, gp_ref, wg_ref, bg_ref, wp_ref, gf_ref, o_ref):
    w0 = rt_ref[:, 2:3]
    w1 = rt_ref[:, 3:4]
    y0 = _unpack_bf16_halves(_load_row_tiles(y0_ref))
    y1 = _unpack_bf16_halves(_load_row_tiles(y1_ref))
    x2 = x1_ref[...] + (y0 * w0 + y1 * w1)
    hp = x2 * lax.rsqrt(jnp.mean(x2 * x2, axis=-1, keepdims=True) + EPS) * gp_ref[...]
    gate = jax.nn.sigmoid(jnp.dot(hp.astype(BF16), wg_ref[...], preferred_element_type=F32) + bg_ref[...])
    pe = jnp.dot(p_ref[...].astype(BF16), wp_ref[...], preferred_element_type=F32)
    x3 = x2 + pe * gate
    o_ref[...] = x3 * lax.rsqrt(jnp.mean(x3 * x3, axis=-1, keepdims=True) + EPS) * gf_ref[...]


def _final(x1, y, route, p, gp, wg, bg, wp, gf):
    N, D = x1.shape
    tm = TM_PROJ
    row = lambda w: pl.BlockSpec((tm, w), lambda i: (i, 0))
    full = lambda a: pl.BlockSpec(a.shape, lambda i: (0,) * a.ndim)
    y0, y1 = y, y
    return pl.pallas_call(
        _final_body,
        grid=(N // tm,),
        in_specs=[row(D), pl.BlockSpec((tm, ROW_CHUNKS, LANES), lambda i: (i, 0, 0)),
                  pl.BlockSpec((tm, ROW_CHUNKS, LANES), lambda i: (i + N // tm, 0, 0)),
                  row(LANES), row(p.shape[1]),
                  full(gp), full(wg), full(bg), full(wp), full(gf)],
        out_specs=row(D),
        out_shape=jax.ShapeDtypeStruct((N, D), F32),
        compiler_params=_cparams(("parallel",)),
        name="ple_final",
    )(x1, y0, y1, route, p, gp, wg, bg, wp, gf)


def kernel(x, p, g_mix, w_in, q_gain, k_gain, conv_w, conv_b, w_f1, b_f1, freq1, w_f2, b_f2, freq2, w_f3, filt_bias, g_attn_out, g_hyena_out, w_out, g_moe, w_group, b_group, w_router, b_router, w_gate, w_up, w_down, g_ple, w_ple_gate, b_ple_gate, w_ple, g_final):
    B, S, D = x.shape
    N = B * S
    assert p.shape[0] == 1 and S == (FFT_N1 // 2) * FFT_N2 and B % 2 == 0
    i = 0
    cst = _dft_constants()
    cos, sin = _rope_tables(S)
    bd = _block_diag_ones(D_ATTN, HEAD_DIM)

    n_qkv = D_ATTN + 2 * D_KV
    wqkv = w_in[i][:, :n_qkv].astype(BF16)
    wut = w_in[i][:, n_qkv:].T.astype(BF16)
    q, kw, vw, ut = _inproj(x, g_mix[i][None], wqkv, wut, bd,
                            jnp.tile(q_gain[i], N_HEADS)[None], jnp.tile(k_gain[i], N_KV_HEADS)[None], cos, sin)

    ya = _attention(q, kw, vw)

    circ = _filter_gen(S, w_f1[i], b_f1[i], freq1[i], w_f2[i], b_f2[i], freq2[i], w_f3[i])
    hspec = _filter_fft(circ, cst)
    hspec = hspec.reshape(2, D_HYENA, FFT_N1, 2 * LANES)
    du = ut.shape[1]
    u4 = ut.reshape(B, du, S // LANES, LANES)
    par_u = jnp.broadcast_to(jnp.concatenate([conv_w[i], conv_b[i][None]], 0)[:, :, None], (4, du, LANES))
    fb = jnp.broadcast_to(filt_bias[i][:, :, None], (2, D_HYENA, LANES))
    yht = _hyena(u4, par_u, fb, hspec, cst)

    wrt = jnp.zeros((D, LANES), F32).at[:, :N_GROUPS].set(w_group[i]).at[:, N_GROUPS:N_GROUPS + N_EXPERTS].set(w_router[i])
    brt = jnp.zeros((1, LANES), F32).at[0, :N_GROUPS].set(b_group[i]).at[0, N_GROUPS:N_GROUPS + N_EXPERTS].set(b_router[i])
    wrh = wrt.astype(BF16)
    wrl = (wrt - wrh.astype(F32)).astype(BF16)
    x1, h2, route, route_t = _outproj(ya, yht, x, g_attn_out[i][None], g_hyena_out[i][:, None],
                                      w_out[i].astype(BF16), bd, g_moe[i][None], wrh, wrl, brt)

    e_flat = jnp.transpose(route_t[:, :TOP_K], (1, 0, 2)).reshape(TOP_K * N).astype(jnp.int32)
    block_e, row_a, n_live = _dispatch(e_flat, N)
    y = _moe_experts(block_e, row_a, n_live, h2.reshape(N, ROW_CHUNKS, LANES), w_gate[i], w_up[i], w_down[i])

    out = _final(x1.reshape(N, D), y, route.reshape(N, LANES), p[i].reshape(N, -1), g_ple[i][None],
                 w_ple_gate[i].astype(BF16), b_ple_gate[i][None], w_ple[i].astype(BF16), g_final[None])
    return out.reshape(B, S, D)
```

```python
import functools
import math

import numpy as np
import jax
import jax.numpy as jnp
from jax import lax
from jax.experimental import pallas as pl
from jax.experimental.pallas import tpu as pltpu

F32 = jnp.float32
BF16 = jnp.bfloat16

D_MODEL = 1024
EPS = 1e-6
GRID_W = 64
N_HEADS = 8
N_KV_HEADS = 2
HEAD_DIM = 64
D_ATTN = N_HEADS * HEAD_DIM
D_KV = N_KV_HEADS * HEAD_DIM
ROPE_THETA = 10000.0
D_HYENA = 512
HYENA_HEAD = 64
FILTER_EMB = 33
FAST_DECAY_PCT = 0.3
SLOW_DECAY_PCT = 1.5
DECAY_TARGET = 1e-2
N_GROUPS = 4
EXPERTS_PER_GROUP = 8
N_EXPERTS = N_GROUPS * EXPERTS_PER_GROUP
TOP_K = 2
D_EXPERT = 512

LANES = 128
MXU_TILE = 256
FFT_N1 = 64
FFT_N2 = 128
VMEM_LIMIT = 56 * 1024 * 1024

TM_PROJ = 512
TQ_ATTN = 256
C_HY = 32
ROW_CHUNKS = D_MODEL // 2 // LANES
SEQ_UNROLL = 32
TB_MOE = 256


def _cparams(sem):
    return pltpu.CompilerParams(dimension_semantics=sem, vmem_limit_bytes=VMEM_LIMIT)


def _rope_tables(S):
    half = HEAD_DIM // 2
    t = jnp.arange(S, dtype=F32)
    r_idx = jnp.floor(t / GRID_W)
    c_idx = t - r_idx * GRID_W
    inv = ROPE_THETA ** (-jnp.arange(0, half, 2, dtype=F32) / half)
    ang_r = r_idx[:, None] * inv[None]
    ang_c = c_idx[:, None] * inv[None]
    cos_h = jnp.concatenate([jnp.cos(ang_r), jnp.cos(ang_r), jnp.cos(ang_c), jnp.cos(ang_c)], axis=-1)
    sin_h = jnp.concatenate([-jnp.sin(ang_r), jnp.sin(ang_r), -jnp.sin(ang_c), jnp.sin(ang_c)], axis=-1)
    return jnp.tile(cos_h, (1, 2)), jnp.tile(sin_h, (1, 2))


def _dft_constants():
    n1, n2 = FFT_N1, FFT_N2
    n = n1 * n2
    a = np.arange(n1)
    ang = 2.0 * np.pi * np.outer(a, a) / n1
    far, fai = np.cos(ang), -np.sin(ang)
    hlf = n1 // 2
    ma = np.block([[far[:, :hlf], -fai[:, :hlf]], [fai[:, :hlf], far[:, :hlf]]])
    maf = np.concatenate([far, fai], axis=0)
    b = np.arange(n2)
    angt = 2.0 * np.pi * np.outer(a, b) / n
    tw = np.concatenate([np.cos(angt), -np.sin(angt)], axis=1)
    angb = 2.0 * np.pi * np.outer(b, b) / n2
    fbr, fbi = np.cos(angb), -np.sin(angb)
    g = np.block([[fbr, fbi], [-fbi, fbr]])
    ginv = np.block([[fbr, -fbi], [fbi, fbr]])
    minv_r = np.concatenate([far[:hlf], -fai[:hlf]], axis=0) / n
    minv_i = np.concatenate([fai[:hlf], far[:hlf]], axis=0) / n
    f = lambda m: jnp.asarray(m.astype(np.float32))
    return dict(ma=f(ma), maf=f(maf), tw=f(tw), g=f(g), ginv=f(ginv), minv_r=f(minv_r), minv_i=f(minv_i))


def _block_diag_ones(width, group):
    i = np.arange(width) // group
    return jnp.asarray((i[:, None] == i[None, :]).astype(np.float32)).astype(BF16)


def _group_sumsq(a, bd):
    sq = a * a
    hi = sq.astype(BF16)
    lo = (sq - hi.astype(F32)).astype(BF16)
    w = min(a.shape[-1], MXU_TILE)
    return jnp.concatenate(
        [jnp.dot(hi[:, c:c + w], bd[c:c + w, c:c + w], preferred_element_type=F32)
         + jnp.dot(lo[:, c:c + w], bd[c:c + w, c:c + w], preferred_element_type=F32)
         for c in range(0, a.shape[-1], w)], axis=-1)


def _head_norm_rope(a, gain, bd, cos, sin):
    width = a.shape[-1]
    n = a * lax.rsqrt(_group_sumsq(a, bd) * (1.0 / HEAD_DIM) + EPS) * gain
    rep = width // LANES
    if rep > 1:
        cos = jnp.concatenate([cos] * rep, axis=-1)
        sin = jnp.concatenate([sin] * rep, axis=-1)
    fwd = pltpu.roll(n, width - 16, 1)
    bwd = pltpu.roll(n, 16, 1)
    lane = lax.broadcasted_iota(jnp.int32, n.shape, 1)
    sw = jnp.where((lane % 32) < 16, fwd, bwd)
    return n * cos + sw * sin


def _inproj_body(x_ref, g_ref, wqkv_ref, wu_ref, bd_ref, qg_ref, kg_ref, cos_ref, sin_ref,
                 q_ref, kw_ref, vw_ref, ut_ref):
    x = x_ref[...]
    h = x * lax.rsqrt(jnp.mean(x * x, axis=-1, keepdims=True) + EPS) * g_ref[...]
    hb = h.astype(BF16)
    qkv = jnp.dot(hb, wqkv_ref[...], preferred_element_type=F32)
    cos = cos_ref[...]
    sin = sin_ref[...]
    bd = bd_ref[...]
    q = _head_norm_rope(qkv[:, :D_ATTN], qg_ref[...], bd, cos, sin)
    q_ref[...] = (q * (HEAD_DIM ** -0.5 * math.log2(math.e))).astype(BF16)
    k = _head_norm_rope(qkv[:, D_ATTN:D_ATTN + D_KV], kg_ref[...], bd[:D_KV, :D_KV], cos, sin)
    kt = k.T.astype(BF16)
    zero = jnp.zeros((HEAD_DIM, kt.shape[1]), BF16)
    for h in range(N_KV_HEADS):
        kh = kt[h * HEAD_DIM:(h + 1) * HEAD_DIM]
        kw_ref[h, 0, :HEAD_DIM] = kh
        kw_ref[h, 0, HEAD_DIM:] = zero
        kw_ref[h, 1, :HEAD_DIM] = zero
        kw_ref[h, 1, HEAD_DIM:] = kh
    v = qkv[:, D_ATTN + D_KV:]
    vr = pltpu.roll(v, HEAD_DIM, 1)
    first = lax.broadcasted_iota(jnp.int32, v.shape, 1) < HEAD_DIM
    vw_ref[0, 0] = jnp.where(first, v, 1.0).astype(BF16)
    vw_ref[0, 1] = jnp.where(first, 1.0, vr).astype(BF16)
    vw_ref[1, 0] = jnp.where(first, vr, 1.0).astype(BF16)
    vw_ref[1, 1] = jnp.where(first, 1.0, v).astype(BF16)
    ut_ref[...] = lax.dot_general(wu_ref[...], hb, (((1,), (1,)), ((), ())),
                                  preferred_element_type=F32)


def _inproj(x, g_mix, wqkv, wut, bd, qg, kg, cos, sin):
    B, S, D = x.shape
    tm = TM_PROJ
    du = wut.shape[0]
    full = lambda shape: pl.BlockSpec(shape, lambda b, i: (0,) * len(shape))
    return pl.pallas_call(
        _inproj_body,
        grid=(B, S // tm),
        in_specs=[
            pl.BlockSpec((None, tm, D), lambda b, i: (b, i, 0)),
            full((1, D)), full(wqkv.shape), full(wut.shape), full(bd.shape),
            full((1, D_ATTN)), full((1, D_KV)),
            pl.BlockSpec((tm, LANES), lambda b, i: (i, 0)),
            pl.BlockSpec((tm, LANES), lambda b, i: (i, 0)),
        ],
        out_specs=[
            pl.BlockSpec((None, tm, D_ATTN), lambda b, i: (b, i, 0)),
            pl.BlockSpec((None, N_KV_HEADS, 2, LANES, tm), lambda b, i: (b, 0, 0, 0, i)),
            pl.BlockSpec((None, N_KV_HEADS, 2, tm, LANES), lambda b, i: (b, 0, 0, i, 0)),
            pl.BlockSpec((None, du, tm), lambda b, i: (b, 0, i)),
        ],
        out_shape=[
            jax.ShapeDtypeStruct((B, S, D_ATTN), BF16),
            jax.ShapeDtypeStruct((B, N_KV_HEADS, 2, LANES, S), BF16),
            jax.ShapeDtypeStruct((B, N_KV_HEADS, 2, S, LANES), BF16),
            jax.ShapeDtypeStruct((B, du, S), F32),
        ],
        compiler_params=_cparams(("parallel", "parallel")),
        name="inproj",
    )(x, g_mix, wqkv, wut, bd, qg, kg, cos, sin)


def _attn_body(q_ref, kw_ref, vw_ref, o_ref):

    def one_head(q, kw, vw):
        s = jnp.dot(q, kw, preferred_element_type=F32)
        m = jnp.max(s, axis=-1, keepdims=True)
        p = jnp.exp2(s - m).astype(BF16)
        return jnp.dot(p, vw, preferred_element_type=F32)

    for pair in range(D_ATTN // LANES):
        h = pair // (N_HEADS // N_KV_HEADS // 2)
        q = q_ref[:, pair * LANES:(pair + 1) * LANES]
        oe = one_head(q, kw_ref[h, 0], vw_ref[h, 0])
        oo = one_head(q, kw_ref[h, 1], vw_ref[h, 1])
        first = lax.broadcasted_iota(jnp.int32, oe.shape, 1) < HEAD_DIM
        num = jnp.where(first, oe, oo)
        den = jnp.where(first, pltpu.roll(oe, HEAD_DIM, 1), pltpu.roll(oo, HEAD_DIM, 1))
        o_ref[:, pair * LANES:(pair + 1) * LANES] = num / den


def _attention(q, kw, vw):
    B, S, _ = q.shape
    tq = TQ_ATTN
    return pl.pallas_call(
        _attn_body,
        grid=(B, S // tq),
        in_specs=[
            pl.BlockSpec((None, tq, D_ATTN), lambda b, i: (b, i, 0)),
            pl.BlockSpec((None, N_KV_HEADS, 2, LANES, S), lambda b, i: (b, 0, 0, 0, 0)),
            pl.BlockSpec((None, N_KV_HEADS, 2, S, LANES), lambda b, i: (b, 0, 0, 0, 0)),
        ],
        out_specs=pl.BlockSpec((None, tq, D_ATTN), lambda b, i: (b, i, 0)),
        out_shape=jax.ShapeDtypeStruct((B, S, D_ATTN), F32),
        compiler_params=_cparams(("parallel", "arbitrary")),
        name="attention",
    )(q, kw, vw)


def _fwd_twiddle_store(y, tw_ref, s1_ref, row0):
    yr, yi = y[:FFT_N1], y[FFT_N1:]
    twr, twi = tw_ref[:, :LANES], tw_ref[:, LANES:]
    s1_ref[pl.ds(row0, FFT_N1), :LANES] = (yr * twr - yi * twi).astype(BF16)
    s1_ref[pl.ds(row0, FFT_N1), LANES:] = (yr * twi + yi * twr).astype(BF16)


def _filtfft_body(x_ref, maf_ref, tw_ref, g_ref, h_ref, s1_ref):
    C = x_ref.shape[0]

    def step_a(c, carry):
        y = jnp.dot(maf_ref[...], x_ref[c].astype(BF16), preferred_element_type=F32)
        _fwd_twiddle_store(y, tw_ref, s1_ref, pl.multiple_of(c * FFT_N1, FFT_N1))
        return carry

    lax.fori_loop(0, C, step_a, 0, unroll=SEQ_UNROLL)
    z = jnp.dot(s1_ref[...], g_ref[...], preferred_element_type=F32)
    h_ref[...] = z.reshape(C, FFT_N1, 2 * LANES)


def _filter_fft(circ, cst):
    n_seq = circ.shape[0]
    C = C_HY
    full = lambda a: pl.BlockSpec(a.shape, lambda i: (0,) * a.ndim)
    maf, tw, g = cst["maf"].astype(BF16), cst["tw"], cst["g"].astype(BF16)
    return pl.pallas_call(
        _filtfft_body,
        grid=(n_seq // C,),
        in_specs=[pl.BlockSpec((C, FFT_N1, FFT_N2), lambda i: (i, 0, 0)), full(maf), full(tw), full(g)],
        out_specs=pl.BlockSpec((C, FFT_N1, 2 * LANES), lambda i: (i, 0, 0)),
        out_shape=jax.ShapeDtypeStruct((n_seq, FFT_N1, 2 * LANES), F32),
        scratch_shapes=[pltpu.VMEM((C * FFT_N1, 2 * LANES), BF16)],
        compiler_params=_cparams(("parallel",)),
        name="filter_fft",
    )(circ, maf, tw, g)


def _short_conv(x, par_ref, c):
    rows, lanes = x.shape
    a_i = lax.broadcasted_iota(jnp.int32, x.shape, 0)
    b_i = lax.broadcasted_iota(jnp.int32, x.shape, 1)
    l1 = pltpu.roll(x, 1, 1)
    l2 = pltpu.roll(l1, 1, 0)
    prev = jnp.where(b_i == 0, l2, l1)
    prev = jnp.where((a_i == 0) & (b_i == 0), 0.0, prev)
    r1 = pltpu.roll(x, lanes - 1, 1)
    r2 = pltpu.roll(r1, rows - 1, 0)
    nxt = jnp.where(b_i == lanes - 1, r2, r1)
    nxt = jnp.where((a_i == rows - 1) & (b_i == lanes - 1), 0.0, nxt)
    w0 = par_ref[0, pl.ds(c, 1), :]
    w1 = par_ref[1, pl.ds(c, 1), :]
    w2 = par_ref[2, pl.ds(c, 1), :]
    cb = par_ref[3, pl.ds(c, 1), :]
    return cb + prev * w0 + x * w1 + nxt * w2


def _hyena_body(v_ref, x1_ref, x2_ref, pv_ref, p1_ref, p2_ref, fb_ref, h_ref,
                ma_ref, tw_ref, g_ref, ginv_ref, mir_ref, mii_ref,
                o_ref, s1_ref, s2_ref, vc_ref, z1_ref):
    C = v_ref.shape[1]
    half = FFT_N1 // 2

    def spectral(order):
        z = jnp.dot(s1_ref[...], g_ref[...], preferred_element_type=F32)
        hs = h_ref[order].reshape(C * FFT_N1, 2 * LANES)
        zr, zi = z[:, :LANES], z[:, LANES:]
        hr, hi = hs[:, :LANES], hs[:, LANES:]
        pb = jnp.concatenate([zr * hr - zi * hi, zr * hi + zi * hr], axis=1).astype(BF16)
        s2_ref[...] = jnp.dot(pb, ginv_ref[...], preferred_element_type=F32)

    def inv_a(c):
        row0 = pl.multiple_of(c * FFT_N1, FFT_N1)
        y = s2_ref[pl.ds(row0, FFT_N1), :]
        yr, yi = y[:, :LANES], y[:, LANES:]
        twr, twi = tw_ref[:, :LANES], tw_ref[:, LANES:]
        ur = (yr * twr + yi * twi).astype(BF16)
        ui = (yi * twr - yr * twi).astype(BF16)
        out = (jnp.dot(mir_ref[...], ur, preferred_element_type=F32)
               + jnp.dot(mii_ref[...], ui, preferred_element_type=F32))
        return out[:half], out[half:]

    def fwd_a(c, xr, xi):
        xs = jnp.concatenate([xr, xi], axis=0).astype(BF16)
        y = jnp.dot(ma_ref[...], xs, preferred_element_type=F32)
        _fwd_twiddle_store(y, tw_ref, s1_ref, pl.multiple_of(c * FFT_N1, FFT_N1))

    def pass1_a(c, carry):
        vr = _short_conv(v_ref[0, c], pv_ref, c)
        vi = _short_conv(v_ref[1, c], pv_ref, c)
        vc_ref[0, c] = vr
        vc_ref[1, c] = vi
        fwd_a(c, vr, vi)
        return carry

    def pass1_b(c, carry):
        cr, ci = inv_a(c)
        bias = fb_ref[0, pl.ds(c, 1), :]
        zr = _short_conv(x1_ref[0, c], p1_ref, c) * (cr + bias * vc_ref[0, c])
        zi = _short_conv(x1_ref[1, c], p1_ref, c) * (ci + bias * vc_ref[1, c])
        z1_ref[0, c] = zr
        z1_ref[1, c] = zi
        fwd_a(c, zr, zi)
        return carry

    def pass2_b(c, carry):
        cr, ci = inv_a(c)
        bias = fb_ref[1, pl.ds(c, 1), :]
        vc_ref[0, c] = _short_conv(x2_ref[0, c], p2_ref, c) * (cr + bias * z1_ref[0, c])
        vc_ref[1, c] = _short_conv(x2_ref[1, c], p2_ref, c) * (ci + bias * z1_ref[1, c])
        return carry

    lax.fori_loop(0, C, pass1_a, 0, unroll=SEQ_UNROLL)
    spectral(0)
    lax.fori_loop(0, C, pass1_b, 0, unroll=SEQ_UNROLL)
    spectral(1)
    lax.fori_loop(0, C, pass2_b, 0, unroll=SEQ_UNROLL)
    for b2 in range(2):
        tiles = pltpu.einshape("cab->acb", vc_ref[b2])
        for a in range(tiles.shape[0]):
            o_ref[b2, :, a * LANES:(a + 1) * LANES] = tiles[a]


def _hyena(u4, par_u, fb, hspec, cst):
    B = u4.shape[0]
    C = C_HY
    J = D_HYENA // C
    rows = u4.shape[2]
    full = lambda a: pl.BlockSpec(a.shape, lambda j, p: (0,) * a.ndim)
    ma, g, ginv = cst["ma"].astype(BF16), cst["g"].astype(BF16), cst["ginv"].astype(BF16)
    mir, mii = cst["minv_r"].astype(BF16), cst["minv_i"].astype(BF16)
    tw = cst["tw"]
    u_spec = lambda k: pl.BlockSpec((2, C, rows, LANES), lambda j, p, k=k: (p, j + k * J, 0, 0))
    par_spec = lambda k: pl.BlockSpec((4, C, LANES), lambda j, p, k=k: (0, j + k * J, 0))
    return pl.pallas_call(
        _hyena_body,
        grid=(J, B // 2),
        in_specs=[
            u_spec(0), u_spec(1), u_spec(2), par_spec(0), par_spec(1), par_spec(2),
            pl.BlockSpec((2, C, LANES), lambda j, p: (0, j, 0)),
            pl.BlockSpec((2, C, FFT_N1, 2 * LANES), lambda j, p: (0, j, 0, 0)),
            full(ma), full(tw), full(g), full(ginv), full(mir), full(mii),
        ],
        out_specs=pl.BlockSpec((2, C, rows * LANES), lambda j, p: (p, j, 0)),
        out_shape=jax.ShapeDtypeStruct((B, D_HYENA, rows * LANES), F32),
        scratch_shapes=[
            pltpu.VMEM((C * FFT_N1, 2 * LANES), BF16),
            pltpu.VMEM((C * FFT_N1, 2 * LANES), F32),
            pltpu.VMEM((2, C, rows, LANES), F32),
            pltpu.VMEM((2, C, rows, LANES), F32),
        ],
        compiler_params=_cparams(("parallel", "arbitrary")),
        name="hyena",
    )(u4, u4, u4, par_u, par_u, par_u, fb, hspec, ma, tw, g, ginv, mir, mii)


def _dot3(a, b):
    ah = a.astype(BF16)
    al = (a - ah.astype(F32)).astype(BF16)
    bh = b.astype(BF16)
    bl = (b - bh.astype(F32)).astype(BF16)
    return (jnp.dot(ah, bh, preferred_element_type=F32) + jnp.dot(al, bh, preferred_element_type=F32)
            + jnp.dot(ah, bl, preferred_element_type=F32))


def _filtgen_body(zt_ref, w1_ref, b1_ref, f1_ref, w2_ref, b2_ref, f2_ref, w3f_ref, w3b_ref, ad_ref, tt_ref,
                  o_ref, hid_ref):
    L = hid_ref.shape[1] // 2

    @pl.when(pl.program_id(0) == 0)
    def _():
        h1 = jnp.sin(f1_ref[...] * (_dot3(w1_ref[...], zt_ref[...]) + b1_ref[...]))
        hid_ref[...] = jnp.sin(f2_ref[...] * (_dot3(w2_ref[...], h1) + b2_ref[...]))

    ad = ad_ref[...]
    hf = _dot3(w3f_ref[...], hid_ref[:, :L]) * jnp.exp(-ad * tt_ref[:, :L])
    hb = _dot3(w3b_ref[...], hid_ref[:, L:]) * jnp.exp(-ad * tt_ref[:, L:])
    hf = hf / (jnp.sum(jnp.abs(hf), axis=-1, keepdims=True) + EPS)
    hb = hb / (jnp.sum(jnp.abs(hb), axis=-1, keepdims=True) + EPS)
    first = lax.broadcasted_iota(jnp.int32, hb.shape, 1) == 0
    cf = hf + jnp.where(first, hb, 0.0)
    cb = jnp.where(first, 0.0, hb)
    chunks = [c[:, a * LANES:(a + 1) * LANES] for c in (cf, cb) for a in range(L // LANES)]
    o_ref[...] = pltpu.einshape("arl->ral", jnp.stack(chunks, axis=0))


def _filter_gen(L, w_f1, b_f1, freq1, w_f2, b_f2, freq2, w_f3):
    bands = (FILTER_EMB - 1) // 2
    t = jnp.linspace(0.0, 1.0, L, dtype=F32)[:, None]
    w = (2.0 * math.pi / L) * jnp.arange(L, dtype=F32)[:, None]
    f = jnp.linspace(1e-4, bands - 1, bands, dtype=F32)[None]
    zf = f * w
    z = jnp.concatenate([t, jnp.cos(zf), -jnp.sin(zf)], axis=-1)
    back = lambda a: jnp.roll(a[::-1], 1, axis=0)
    kpad = 48
    zt = jnp.pad(jnp.concatenate([z, back(z)], axis=0).T, ((0, kpad - FILTER_EMB), (0, 0)))
    tt = jnp.concatenate([t, back(t)], axis=0).T
    w1t = jnp.pad(w_f1.T, ((0, 0), (0, kpad - FILTER_EMB)))
    w3 = w_f3.reshape(-1, 2, 2, D_HYENA)
    w3f = jnp.transpose(w3[:, :, 0], (1, 2, 0)).reshape(2 * D_HYENA, -1)
    w3b = jnp.transpose(w3[:, :, 1], (1, 2, 0)).reshape(2 * D_HYENA, -1)
    max_decay = math.log(DECAY_TARGET) / FAST_DECAY_PCT
    min_decay = math.log(DECAY_TARGET) / SLOW_DECAY_PCT
    deltas = jnp.linspace(min_decay, max_decay, D_HYENA, dtype=F32)
    ad = jnp.tile(jnp.abs(deltas), 2)[:, None]
    col = lambda v: v[:, None]
    R = 128
    n_rows = 2 * D_HYENA
    full = lambda a: pl.BlockSpec(a.shape, lambda i: (0,) * a.ndim)
    rows = lambda a: pl.BlockSpec((R, a.shape[1]), lambda i: (i, 0))
    args = (zt, w1t, col(b_f1), col(freq1), w_f2.T, col(b_f2), col(freq2))
    return pl.pallas_call(
        _filtgen_body,
        grid=(n_rows // R,),
        in_specs=[full(a) for a in args] + [rows(w3f), rows(w3b), rows(ad), full(tt)],
        out_specs=pl.BlockSpec((R, 2 * L // LANES, LANES), lambda i: (i, 0, 0)),
        out_shape=jax.ShapeDtypeStruct((n_rows, 2 * L // LANES, LANES), F32),
        scratch_shapes=[pltpu.VMEM((w_f2.shape[1], 2 * L), F32)],
        compiler_params=_cparams(("arbitrary",)),
        name="filter_gen",
    )(*args, w3f, w3b, ad, tt)


def _route_lanes(lg):
    neg = -1e30
    lane = lax.broadcasted_iota(jnp.int32, lg.shape, 1)
    gmask = lane < N_GROUPS
    gl = jnp.where(gmask, lg, neg)
    gm = jnp.max(gl, axis=-1, keepdims=True)
    gsum = jnp.sum(jnp.where(gmask, jnp.exp(gl - gm), 0.0), axis=-1, keepdims=True)
    g_top = 1.0 / gsum
    g_sel = jnp.min(jnp.where(gl == gm, lane, LANES), axis=-1, keepdims=True)
    lo = N_GROUPS + EXPERTS_PER_GROUP * g_sel
    el = jnp.where((lane >= lo) & (lane < lo + EXPERTS_PER_GROUP), lg, neg)
    m1 = jnp.max(el, axis=-1, keepdims=True)
    i1 = jnp.min(jnp.where(el == m1, lane, LANES), axis=-1, keepdims=True)
    el2 = jnp.where(lane == i1, neg, el)
    m2 = jnp.max(el2, axis=-1, keepdims=True)
    i2 = jnp.min(jnp.where(el2 == m2, lane, LANES), axis=-1, keepdims=True)
    d = jnp.exp(m2 - m1)
    p1 = 1.0 / (1.0 + d)
    p2 = d / (1.0 + d)
    e1 = (i1 - N_GROUPS).astype(F32)
    e2 = (i2 - N_GROUPS).astype(F32)
    return jnp.where(lane == 0, e1, jnp.where(lane == 1, e2, jnp.where(lane == 2, g_top * p1,
                     jnp.where(lane == 3, g_top * p2, 0.0))))


def _pack_bf16_halves(a):
    w = a.shape[1] // 2
    bits = pltpu.bitcast(a.astype(BF16).astype(F32), jnp.uint32)
    return (bits[:, :w] >> 16) | (bits[:, w:] & jnp.uint32(0xFFFF0000))


def _unpack_bf16_halves(wd):
    lo = pltpu.bitcast(wd << 16, F32)
    hi = pltpu.bitcast(wd & jnp.uint32(0xFFFF0000), F32)
    return jnp.concatenate([lo, hi], axis=1)


def _store_row_tiles(ref, packed):
    chunks = jnp.stack([packed[:, j * LANES:(j + 1) * LANES] for j in range(ROW_CHUNKS)], axis=0)
    ref[...] = pltpu.einshape("jrl->rjl", chunks)


def _load_row_tiles(ref):
    chunks = pltpu.einshape("rjl->jrl", ref[...])
    return jnp.concatenate([chunks[j] for j in range(ROW_CHUNKS)], axis=1)


def _outproj_body(ya_ref, yh_ref, x_ref, ga_ref, gh_ref, wo_ref, bd_ref, gm_ref, wrh_ref, wrl_ref, brt_ref,
                  x1_ref, h2_ref, rt_ref, rtt_ref):
    ya = ya_ref[...]
    yan = ya * lax.rsqrt(_group_sumsq(ya, bd_ref[...]) * (1.0 / HEAD_DIM) + EPS) * ga_ref[...]
    yh = yh_ref[...]
    tm = yh.shape[1]
    yh3 = yh.reshape(D_HYENA // HYENA_HEAD, HYENA_HEAD, tm)
    ms = jnp.mean(yh3 * yh3, axis=1, keepdims=True)
    yhn = (yh3 * lax.rsqrt(ms + EPS)).reshape(D_HYENA, tm) * gh_ref[...]
    mix = (jnp.dot(yan.astype(BF16), wo_ref[:D_ATTN, :], preferred_element_type=F32)
           + jnp.dot(yhn.T.astype(BF16), wo_ref[D_ATTN:, :], preferred_element_type=F32))
    x1 = x_ref[...] + mix
    x1_ref[...] = x1
    h2 = x1 * lax.rsqrt(jnp.mean(x1 * x1, axis=-1, keepdims=True) + EPS) * gm_ref[...]
    _store_row_tiles(h2_ref, _pack_bf16_halves(h2))
    hi = h2.astype(BF16)
    lo = (h2 - hi.astype(F32)).astype(BF16)
    lg = (jnp.dot(hi, wrh_ref[...], preferred_element_type=F32)
          + jnp.dot(lo, wrh_ref[...], preferred_element_type=F32)
          + jnp.dot(hi, wrl_ref[...], preferred_element_type=F32)) + brt_ref[...]
    route = _route_lanes(lg)
    rt_ref[...] = route
    rtt_ref[...] = route.T[:8]


def _outproj(ya, yht, x, ga, gh, wo, bd, gm, wrh, wrl, brt):
    B, S, D = x.shape
    tm = TM_PROJ
    full = lambda a: pl.BlockSpec(a.shape, lambda b, i: (0,) * a.ndim)
    return pl.pallas_call(
        _outproj_body,
        grid=(B, S // tm),
        in_specs=[
            pl.BlockSpec((None, tm, D_ATTN), lambda b, i: (b, i, 0)),
            pl.BlockSpec((None, D_HYENA, tm), lambda b, i: (b, 0, i)),
            pl.BlockSpec((None, tm, D), lambda b, i: (b, i, 0)),
            full(ga), full(gh), full(wo), full(bd), full(gm), full(wrh), full(wrl), full(brt),
        ],
        out_specs=[
            pl.BlockSpec((None, tm, D), lambda b, i: (b, i, 0)),
            pl.BlockSpec((None, tm, ROW_CHUNKS, LANES), lambda b, i: (b, i, 0, 0)),
            pl.BlockSpec((None, tm, LANES), lambda b, i: (b, i, 0)),
            pl.BlockSpec((None, 8, tm), lambda b, i: (b, 0, i)),
        ],
        out_shape=[
            jax.ShapeDtypeStruct((B, S, D), F32),
            jax.ShapeDtypeStruct((B, S, ROW_CHUNKS, LANES), jnp.uint32),
            jax.ShapeDtypeStruct((B, S, LANES), F32),
            jax.ShapeDtypeStruct((B, 8, S), F32),
        ],
        compiler_params=_cparams(("parallel", "parallel")),
        name="outproj",
    )(ya, yht, x, ga, gh, wo, bd, gm, wrh, wrl, brt)


def _moe_body(be_ref, ra_ref, nlive_ref, h2_hbm, wg_ref, wu_ref, wd_ref, y_hbm,
              wg_s, wu_s, wd_s, xbuf, ybuf, zbuf, sem_in, sem_out, sem_z, *, n_tok, n_rows):
    i = pl.program_id(0)
    nb = nlive_ref[0]
    T = xbuf.shape[1]
    slot = i % 2

    def issue_gathers(blk, sl):
        for r in range(T):
            tok = ra_ref[blk * T + r] & (n_tok - 1)
            pltpu.make_async_copy(h2_hbm.at[tok], xbuf.at[sl, r], sem_in.at[sl]).start(priority=r % 2)

    def issue_scatters(blk, sl, spare):
        for r in range(T):
            dst = jnp.where(spare, n_rows + r, ra_ref[blk * T + r])
            pltpu.make_async_copy(ybuf.at[sl, r], y_hbm.at[dst], sem_out.at[sl]).start(priority=r % 2)

    def block_in_wait(sl):
        pltpu.make_async_copy(h2_hbm.at[pl.ds(0, T)], xbuf.at[sl], sem_in.at[sl]).wait()

    def block_out_wait(sl):
        pltpu.make_async_copy(ybuf.at[sl], y_hbm.at[pl.ds(0, T)], sem_out.at[sl]).wait()

    @pl.when(i == 0)
    def _():
        ybuf[...] = jnp.zeros(ybuf.shape, ybuf.dtype)
        zbuf[...] = jnp.zeros(zbuf.shape, zbuf.dtype)
        issue_gathers(0, 0)

    @pl.when(i >= nb)
    def _():
        fill = pltpu.make_async_copy(zbuf, y_hbm.at[pl.ds(i * T, T)], sem_z.at[0])
        fill.start()
        fill.wait()

    prev = be_ref[jnp.maximum(i - 1, 0)]

    @pl.when((i == 0) | (be_ref[i] != prev))
    def _():
        wg_s[...] = wg_ref[...].astype(BF16)
        wu_s[...] = wu_ref[...].astype(BF16)
        wd_s[...] = wd_ref[...].astype(BF16)

    @pl.when(i < nb)
    def _():
        block_in_wait(slot)
        x = _unpack_bf16_halves(_load_row_tiles(xbuf.at[slot])).astype(BF16)
        issue_gathers(jnp.minimum(i + 1, nb - 1), 1 - slot)
        issue_scatters(jnp.maximum(i - 1, 0), 1 - slot, i == 0)
        a = jnp.dot(x, wg_s[...], preferred_element_type=F32)
        b = jnp.dot(x, wu_s[...], preferred_element_type=F32)
        hmid = (a * jax.nn.sigmoid(a)) * b
        y = _pack_bf16_halves(jnp.dot(hmid.astype(BF16), wd_s[...], preferred_element_type=F32))

        @pl.when(i >= 1)
        def _():
            block_out_wait(slot)

        _store_row_tiles(ybuf.at[slot], y)

        @pl.when(i == nb - 1)
        def _():
            issue_scatters(i, slot, False)
            block_in_wait(1 - slot)
            block_out_wait(1 - slot)
            block_out_wait(slot)


def _moe_experts(block_e, row_a, n_live, h2p, w_gate, w_up, w_down):
    n_tok = h2p.shape[0]
    n_rows = row_a.shape[0]
    row = h2p.shape[1:]
    D = w_gate.shape[1]
    T = TB_MOE
    assert n_tok & (n_tok - 1) == 0
    grid_spec = pltpu.PrefetchScalarGridSpec(
        num_scalar_prefetch=3,
        grid=(row_a.shape[0] // T,),
        in_specs=[
            pl.BlockSpec(memory_space=pl.ANY),
            pl.BlockSpec((None, D, D_EXPERT), lambda i, be, ra, nl: (be[i], 0, 0)),
            pl.BlockSpec((None, D, D_EXPERT), lambda i, be, ra, nl: (be[i], 0, 0)),
            pl.BlockSpec((None, D_EXPERT, D), lambda i, be, ra, nl: (be[i], 0, 0)),
        ],
        out_specs=pl.BlockSpec(memory_space=pl.ANY),
        scratch_shapes=[
            pltpu.VMEM((D, D_EXPERT), BF16), pltpu.VMEM((D, D_EXPERT), BF16), pltpu.VMEM((D_EXPERT, D), BF16),
            pltpu.VMEM((2, T) + row, jnp.uint32), pltpu.VMEM((2, T) + row, jnp.uint32),
            pltpu.VMEM((T,) + row, jnp.uint32),
            pltpu.SemaphoreType.DMA((2,)), pltpu.SemaphoreType.DMA((2,)), pltpu.SemaphoreType.DMA((1,)),
        ],
    )
    return pl.pallas_call(
        functools.partial(_moe_body, n_tok=n_tok, n_rows=n_rows),
        grid_spec=grid_spec,
        out_shape=jax.ShapeDtypeStruct((n_rows + T,) + row, jnp.uint32),
        compiler_params=_cparams(("arbitrary",)),
        name="moe_experts",
    )(block_e, row_a, n_live, h2p, w_gate, w_up, w_down)


def _dispatch(e_flat, N):
    T = TB_MOE
    NK = N * TOP_K
    experts = jnp.arange(N_EXPERTS, dtype=jnp.int32)
    order = jnp.argsort(e_flat).astype(jnp.int32)
    onehot = (e_flat[:, None] == experts[None]).astype(jnp.int32)
    counts = jnp.sum(onehot, axis=0)
    ends = jnp.cumsum(counts)
    starts = ends - counts
    padded = (counts + T - 1) // T * T
    pends = jnp.cumsum(padded)
    pstarts = pends - padded
    n_rows = -(-(NK + N_EXPERTS * (T - 1)) // T) * T
    n_blocks = n_rows // T
    blk_start = jnp.arange(n_blocks, dtype=jnp.int32) * T
    block_e = jnp.clip(jnp.sum((pends[None, :] <= blk_start[:, None]).astype(jnp.int32), axis=1),
                       0, N_EXPERTS - 1)
    oh_b = (block_e[:, None] == experts[None]).astype(jnp.int32)
    base = jnp.sum(oh_b * (starts - pstarts)[None], axis=1) + blk_start
    end_b = jnp.sum(oh_b * ends[None], axis=1)
    lane = jnp.arange(T, dtype=jnp.int32)[None]
    src = base[:, None] + lane
    pad_id = NK + blk_start[:, None] + lane - end_b[:, None]
    row_a = jnp.where(src < end_b[:, None], order[jnp.clip(src, 0, NK - 1)], pad_id)
    n_live = (pends[-1:] // T).astype(jnp.int32)
    return block_e.astype(jnp.int32), row_a.reshape(n_rows).astype(jnp.int32), n_live


def _final_body(x1_ref, y0_ref, y1_ref, rt_ref, p_ref, gp_ref, wg_ref, bg_ref, wp_ref, gf_ref, o_ref):
    w0 = rt_ref[:, 2:3]
    w1 = rt_ref[:, 3:4]
    y0 = _unpack_bf16_halves(_load_row_tiles(y0_ref))
    y1 = _unpack_bf16_halves(_load_row_tiles(y1_ref))
    x2 = x1_ref[...] + (y0 * w0 + y1 * w1)
    hp = x2 * lax.rsqrt(jnp.mean(x2 * x2, axis=-1, keepdims=True) + EPS) * gp_ref[...]
    gate = jax.nn.sigmoid(jnp.dot(hp.astype(BF16), wg_ref[...], preferred_element_type=F32) + bg_ref[...])
    pe = jnp.dot(p_ref[...].astype(BF16), wp_ref[...], preferred_element_type=F32)
    x3 = x2 + pe * gate
    o_ref[...] = x3 * lax.rsqrt(jnp.mean(x3 * x3, axis=-1, keepdims=True) + EPS) * gf_ref[...]


def _final(x1, y, route, p, gp, wg, bg, wp, gf):
    N, D = x1.shape
    tm = TM_PROJ
    row = lambda w: pl.BlockSpec((tm, w), lambda i: (i, 0))
    full = lambda a: pl.BlockSpec(a.shape, lambda i: (0,) * a.ndim)
    y0, y1 = y, y
    return pl.pallas_call(
        _final_body,
        grid=(N // tm,),
        in_specs=[row(D), pl.BlockSpec((tm, ROW_CHUNKS, LANES), lambda i: (i, 0, 0)),
                  pl.BlockSpec((tm, ROW_CHUNKS, LANES), lambda i: (i + N // tm, 0, 0)),
                  row(LANES), row(p.shape[1]),
                  full(gp), full(wg), full(bg), full(wp), full(gf)],
        out_specs=row(D),
        out_shape=jax.ShapeDtypeStruct((N, D), F32),
        compiler_params=_cparams(("parallel",)),
        name="ple_final",
    )(x1, y0, y1, route, p, gp, wg, bg, wp, gf)


def kernel(x, p, g_mix, w_in, q_gain, k_gain, conv_w, conv_b, w_f1, b_f1, freq1, w_f2, b_f2, freq2, w_f3, filt_bias, g_attn_out, g_hyena_out, w_out, g_moe, w_group, b_group, w_router, b_router, w_gate, w_up, w_down, g_ple, w_ple_gate, b_ple_gate, w_ple, g_final):
    B, S, D = x.shape
    N = B * S
    assert p.shape[0] == 1 and S == (FFT_N1 // 2) * FFT_N2 and B % 2 == 0
    i = 0
    cst = _dft_constants()
    cos, sin = _rope_tables(S)
    bd = _block_diag_ones(D_ATTN, HEAD_DIM)

    n_qkv = D_ATTN + 2 * D_KV
    wqkv = w_in[i][:, :n_qkv].astype(BF16)
    wut = w_in[i][:, n_qkv:].T.astype(BF16)
    q, kw, vw, ut = _inproj(x, g_mix[i][None], wqkv, wut, bd,
                            jnp.tile(q_gain[i], N_HEADS)[None], jnp.tile(k_gain[i], N_KV_HEADS)[None], cos, sin)

    ya = _attention(q, kw, vw)

    circ = _filter_gen(S, w_f1[i], b_f1[i], freq1[i], w_f2[i], b_f2[i], freq2[i], w_f3[i])
    hspec = _filter_fft(circ, cst)
    hspec = hspec.reshape(2, D_HYENA, FFT_N1, 2 * LANES)
    du = ut.shape[1]
    u4 = ut.reshape(B, du, S // LANES, LANES)
    par_u = jnp.broadcast_to(jnp.concatenate([conv_w[i], conv_b[i][None]], 0)[:, :, None], (4, du, LANES))
    fb = jnp.broadcast_to(filt_bias[i][:, :, None], (2, D_HYENA, LANES))
    yht = _hyena(u4, par_u, fb, hspec, cst)

    wrt = jnp.zeros((D, LANES), F32).at[:, :N_GROUPS].set(w_group[i]).at[:, N_GROUPS:N_GROUPS + N_EXPERTS].set(w_router[i])
    brt = jnp.zeros((1, LANES), F32).at[0, :N_GROUPS].set(b_group[i]).at[0, N_GROUPS:N_GROUPS + N_EXPERTS].set(b_router[i])
    wrh = wrt.astype(BF16)
    wrl = (wrt - wrh.astype(F32)).astype(BF16)
    x1, h2, route, route_t = _outproj(ya, yht, x, g_attn_out[i][None], g_hyena_out[i][:, None],
                                      w_out[i].astype(BF16), bd, g_moe[i][None], wrh, wrl, brt)

    e_flat = jnp.transpose(route_t[:, :TOP_K], (1, 0, 2)).reshape(TOP_K * N).astype(jnp.int32)
    block_e, row_a, n_live = _dispatch(e_flat, N)
    y = _moe_experts(block_e, row_a, n_live, h2.reshape(N, ROW_CHUNKS, LANES), w_gate[i], w_up[i], w_down[i])

    out = _final(x1.reshape(N, D), y, route.reshape(N, LANES), p[i].reshape(N, -1), g_ple[i][None],
                 w_ple_gate[i].astype(BF16), b_ple_gate[i][None], w_ple[i].astype(BF16), g_final[None])
    return out.reshape(B, S, D)
```

```python
import functools
import math

import numpy as np
import jax
import jax.numpy as jnp
from jax import lax
from jax.experimental import pallas as pl
from jax.experimental.pallas import tpu as pltpu

F32 = jnp.float32
BF16 = jnp.bfloat16

D_MODEL = 1024
EPS = 1e-6
GRID_W = 64
N_HEADS = 8
N_KV_HEADS = 2
HEAD_DIM = 64
D_ATTN = N_HEADS * HEAD_DIM
D_KV = N_KV_HEADS * HEAD_DIM
ROPE_THETA = 10000.0
D_HYENA = 512
HYENA_HEAD = 64
FILTER_EMB = 33
FAST_DECAY_PCT = 0.3
SLOW_DECAY_PCT = 1.5
DECAY_TARGET = 1e-2
N_GROUPS = 4
EXPERTS_PER_GROUP = 8
N_EXPERTS = N_GROUPS * EXPERTS_PER_GROUP
TOP_K = 2
D_EXPERT = 512

LANES = 128
MXU_TILE = 256
FFT_N1 = 64
FFT_N2 = 128
VMEM_LIMIT = 56 * 1024 * 1024

TM_PROJ = 512
TQ_ATTN = 256
C_HY = 32
ROW_CHUNKS = D_MODEL // 2 // LANES
SEQ_UNROLL = 32
TB_MOE = 256


def _cparams(sem):
    return pltpu.CompilerParams(dimension_semantics=sem, vmem_limit_bytes=VMEM_LIMIT)


def _rope_tables(S):
    half = HEAD_DIM // 2
    t = jnp.arange(S, dtype=F32)
    r_idx = jnp.floor(t / GRID_W)
    c_idx = t - r_idx * GRID_W
    inv = ROPE_THETA ** (-jnp.arange(0, half, 2, dtype=F32) / half)
    ang_r = r_idx[:, None] * inv[None]
    ang_c = c_idx[:, None] * inv[None]
    cos_h = jnp.concatenate([jnp.cos(ang_r), jnp.cos(ang_r), jnp.cos(ang_c), jnp.cos(ang_c)], axis=-1)
    sin_h = jnp.concatenate([-jnp.sin(ang_r), jnp.sin(ang_r), -jnp.sin(ang_c), jnp.sin(ang_c)], axis=-1)
    return jnp.tile(cos_h, (1, 2)), jnp.tile(sin_h, (1, 2))


def _dft_constants():
    n1, n2 = FFT_N1, FFT_N2
    n = n1 * n2
    a = np.arange(n1)
    ang = 2.0 * np.pi * np.outer(a, a) / n1
    far, fai = np.cos(ang), -np.sin(ang)
    hlf = n1 // 2
    ma = np.block([[far[:, :hlf], -fai[:, :hlf]], [fai[:, :hlf], far[:, :hlf]]])
    maf = np.concatenate([far, fai], axis=0)
    b = np.arange(n2)
    angt = 2.0 * np.pi * np.outer(a, b) / n
    tw = np.concatenate([np.cos(angt), -np.sin(angt)], axis=1)
    angb = 2.0 * np.pi * np.outer(b, b) / n2
    fbr, fbi = np.cos(angb), -np.sin(angb)
    g = np.block([[fbr, fbi], [-fbi, fbr]])
    ginv = np.block([[fbr, -fbi], [fbi, fbr]])
    minv_r = np.concatenate([far[:hlf], -fai[:hlf]], axis=0) / n
    minv_i = np.concatenate([fai[:hlf], far[:hlf]], axis=0) / n
    f = lambda m: jnp.asarray(m.astype(np.float32))
    return dict(ma=f(ma), maf=f(maf), tw=f(tw), g=f(g), ginv=f(ginv), minv_r=f(minv_r), minv_i=f(minv_i))


def _block_diag_ones(width, group):
    i = np.arange(width) // group
    return jnp.asarray((i[:, None] == i[None, :]).astype(np.float32)).astype(BF16)


def _group_sumsq(a, bd):
    sq = a * a
    hi = sq.astype(BF16)
    lo = (sq - hi.astype(F32)).astype(BF16)
    w = min(a.shape[-1], MXU_TILE)
    return jnp.concatenate(
        [jnp.dot(hi[:, c:c + w], bd[c:c + w, c:c + w], preferred_element_type=F32)
         + jnp.dot(lo[:, c:c + w], bd[c:c + w, c:c + w], preferred_element_type=F32)
         for c in range(0, a.shape[-1], w)], axis=-1)


def _head_norm_rope(a, gain, bd, cos, sin):
    width = a.shape[-1]
    n = a * lax.rsqrt(_group_sumsq(a, bd) * (1.0 / HEAD_DIM) + EPS) * gain
    rep = width // LANES
    if rep > 1:
        cos = jnp.concatenate([cos] * rep, axis=-1)
        sin = jnp.concatenate([sin] * rep, axis=-1)
    fwd = pltpu.roll(n, width - 16, 1)
    bwd = pltpu.roll(n, 16, 1)
    lane = lax.broadcasted_iota(jnp.int32, n.shape, 1)
    sw = jnp.where((lane % 32) < 16, fwd, bwd)
    return n * cos + sw * sin


def _inproj_body(x_ref, g_ref, wqkv_ref, wu_ref, bd_ref, qg_ref, kg_ref, cos_ref, sin_ref,
                 q_ref, kw_ref, vw_ref, ut_ref):
    x = x_ref[...]
    h = x * lax.rsqrt(jnp.mean(x * x, axis=-1, keepdims=True) + EPS) * g_ref[...]
    hb = h.astype(BF16)
    qkv = jnp.dot(hb, wqkv_ref[...], preferred_element_type=F32)
    cos = cos_ref[...]
    sin = sin_ref[...]
    bd = bd_ref[...]
    q = _head_norm_rope(qkv[:, :D_ATTN], qg_ref[...], bd, cos, sin)
    q_ref[...] = (q * (HEAD_DIM ** -0.5 * math.log2(math.e))).astype(BF16)
    k = _head_norm_rope(qkv[:, D_ATTN:D_ATTN + D_KV], kg_ref[...], bd[:D_KV, :D_KV], cos, sin)
    kt = k.T.astype(BF16)
    zero = jnp.zeros((HEAD_DIM, kt.shape[1]), BF16)
    for h in range(N_KV_HEADS):
        kh = kt[h * HEAD_DIM:(h + 1) * HEAD_DIM]
        kw_ref[h, 0, :HEAD_DIM] = kh
        kw_ref[h, 0, HEAD_DIM:] = zero
        kw_ref[h, 1, :HEAD_DIM] = zero
        kw_ref[h, 1, HEAD_DIM:] = kh
    v = qkv[:, D_ATTN + D_KV:]
    vr = pltpu.roll(v, HEAD_DIM, 1)
    first = lax.broadcasted_iota(jnp.int32, v.shape, 1) < HEAD_DIM
    vw_ref[0, 0] = jnp.where(first, v, 1.0).astype(BF16)
    vw_ref[0, 1] = jnp.where(first, 1.0, vr).astype(BF16)
    vw_ref[1, 0] = jnp.where(first, vr, 1.0).astype(BF16)
    vw_ref[1, 1] = jnp.where(first, 1.0, v).astype(BF16)
    ut_ref[...] = lax.dot_general(wu_ref[...], hb, (((1,), (1,)), ((), ())),
                                  preferred_element_type=F32)


def _inproj(x, g_mix, wqkv, wut, bd, qg, kg, cos, sin):
    B, S, D = x.shape
    tm = TM_PROJ
    du = wut.shape[0]
    full = lambda shape: pl.BlockSpec(shape, lambda b, i: (0,) * len(shape))
    return pl.pallas_call(
        _inproj_body,
        grid=(B, S // tm),
        in_specs=[
            pl.BlockSpec((None, tm, D), lambda b, i: (b, i, 0)),
            full((1, D)), full(wqkv.shape), full(wut.shape), full(bd.shape),
            full((1, D_ATTN)), full((1, D_KV)),
            pl.BlockSpec((tm, LANES), lambda b, i: (i, 0)),
            pl.BlockSpec((tm, LANES), lambda b, i: (i, 0)),
        ],
        out_specs=[
            pl.BlockSpec((None, tm, D_ATTN), lambda b, i: (b, i, 0)),
            pl.BlockSpec((None, N_KV_HEADS, 2, LANES, tm), lambda b, i: (b, 0, 0, 0, i)),
            pl.BlockSpec((None, N_KV_HEADS, 2, tm, LANES), lambda b, i: (b, 0, 0, i, 0)),
            pl.BlockSpec((None, du, tm), lambda b, i: (b, 0, i)),
        ],
        out_shape=[
            jax.ShapeDtypeStruct((B, S, D_ATTN), BF16),
            jax.ShapeDtypeStruct((B, N_KV_HEADS, 2, LANES, S), BF16),
            jax.ShapeDtypeStruct((B, N_KV_HEADS, 2, S, LANES), BF16),
            jax.ShapeDtypeStruct((B, du, S), F32),
        ],
        compiler_params=_cparams(("parallel", "parallel")),
        name="inproj",
    )(x, g_mix, wqkv, wut, bd, qg, kg, cos, sin)


def _attn_body(q_ref, kw_ref, vw_ref, o_ref):

    def one_head(q, kw, vw):
        s = jnp.dot(q, kw, preferred_element_type=F32)
        m = jnp.max(s, axis=-1, keepdims=True)
        p = jnp.exp2(s - m).astype(BF16)
        return jnp.dot(p, vw, preferred_element_type=F32)

    for pair in range(D_ATTN // LANES):
        h = pair // (N_HEADS // N_KV_HEADS // 2)
        q = q_ref[:, pair * LANES:(pair + 1) * LANES]
        oe = one_head(q, kw_ref[h, 0], vw_ref[h, 0])
        oo = one_head(q, kw_ref[h, 1], vw_ref[h, 1])
        first = lax.broadcasted_iota(jnp.int32, oe.shape, 1) < HEAD_DIM
        num = jnp.where(first, oe, oo)
        den = jnp.where(first, pltpu.roll(oe, HEAD_DIM, 1), pltpu.roll(oo, HEAD_DIM, 1))
        o_ref[:, pair * LANES:(pair + 1) * LANES] = num / den


def _attention(q, kw, vw):
    B, S, _ = q.shape
    tq = TQ_ATTN
    return pl.pallas_call(
        _attn_body,
        grid=(B, S // tq),
        in_specs=[
            pl.BlockSpec((None, tq, D_ATTN), lambda b, i: (b, i, 0)),
            pl.BlockSpec((None, N_KV_HEADS, 2, LANES, S), lambda b, i: (b, 0, 0, 0, 0)),
            pl.BlockSpec((None, N_KV_HEADS, 2, S, LANES), lambda b, i: (b, 0, 0, 0, 0)),
        ],
        out_specs=pl.BlockSpec((None, tq, D_ATTN), lambda b, i: (b, i, 0)),
        out_shape=jax.ShapeDtypeStruct((B, S, D_ATTN), F32),
        compiler_params=_cparams(("parallel", "arbitrary")),
        name="attention",
    )(q, kw, vw)


def _fwd_twiddle_store(y, tw_ref, s1_ref, row0):
    yr, yi = y[:FFT_N1], y[FFT_N1:]
    twr, twi = tw_ref[:, :LANES], tw_ref[:, LANES:]
    s1_ref[pl.ds(row0, FFT_N1), :LANES] = (yr * twr - yi * twi).astype(BF16)
    s1_ref[pl.ds(row0, FFT_N1), LANES:] = (yr * twi + yi * twr).astype(BF16)


def _filtfft_body(x_ref, maf_ref, tw_ref, g_ref, h_ref, s1_ref):
    C = x_ref.shape[0]

    def step_a(c, carry):
        y = jnp.dot(maf_ref[...], x_ref[c].astype(BF16), preferred_element_type=F32)
        _fwd_twiddle_store(y, tw_ref, s1_ref, pl.multiple_of(c * FFT_N1, FFT_N1))
        return carry

    lax.fori_loop(0, C, step_a, 0, unroll=SEQ_UNROLL)
    z = jnp.dot(s1_ref[...], g_ref[...], preferred_element_type=F32)
    h_ref[...] = z.reshape(C, FFT_N1, 2 * LANES)


def _filter_fft(circ, cst):
    n_seq = circ.shape[0]
    C = C_HY
    full = lambda a: pl.BlockSpec(a.shape, lambda i: (0,) * a.ndim)
    maf, tw, g = cst["maf"].astype(BF16), cst["tw"], cst["g"].astype(BF16)
    return pl.pallas_call(
        _filtfft_body,
        grid=(n_seq // C,),
        in_specs=[pl.BlockSpec((C, FFT_N1, FFT_N2), lambda i: (i, 0, 0)), full(maf), full(tw), full(g)],
        out_specs=pl.BlockSpec((C, FFT_N1, 2 * LANES), lambda i: (i, 0, 0)),
        out_shape=jax.ShapeDtypeStruct((n_seq, FFT_N1, 2 * LANES), F32),
        scratch_shapes=[pltpu.VMEM((C * FFT_N1, 2 * LANES), BF16)],
        compiler_params=_cparams(("parallel",)),
        name="filter_fft",
    )(circ, maf, tw, g)


def _short_conv(x, par_ref, c):
    rows, lanes = x.shape
    a_i = lax.broadcasted_iota(jnp.int32, x.shape, 0)
    b_i = lax.broadcasted_iota(jnp.int32, x.shape, 1)
    l1 = pltpu.roll(x, 1, 1)
    l2 = pltpu.roll(l1, 1, 0)
    prev = jnp.where(b_i == 0, l2, l1)
    prev = jnp.where((a_i == 0) & (b_i == 0), 0.0, prev)
    r1 = pltpu.roll(x, lanes - 1, 1)
    r2 = pltpu.roll(r1, rows - 1, 0)
    nxt = jnp.where(b_i == lanes - 1, r2, r1)
    nxt = jnp.where((a_i == rows - 1) & (b_i == lanes - 1), 0.0, nxt)
    w0 = par_ref[0, pl.ds(c, 1), :]
    w1 = par_ref[1, pl.ds(c, 1), :]
    w2 = par_ref[2, pl.ds(c, 1), :]
    cb = par_ref[3, pl.ds(c, 1), :]
    return cb + prev * w0 + x * w1 + nxt * w2


def _hyena_body(v_ref, x1_ref, x2_ref, pv_ref, p1_ref, p2_ref, fb_ref, h_ref,
                ma_ref, tw_ref, g_ref, ginv_ref, mir_ref, mii_ref,
                o_ref, s1_ref, s2_ref, vc_ref, z1_ref):
    C = v_ref.shape[1]
    half = FFT_N1 // 2

    def spectral(order):
        z = jnp.dot(s1_ref[...], g_ref[...], preferred_element_type=F32)
        hs = h_ref[order].reshape(C * FFT_N1, 2 * LANES)
        zr, zi = z[:, :LANES], z[:, LANES:]
        hr, hi = hs[:, :LANES], hs[:, LANES:]
        pb = jnp.concatenate([zr * hr - zi * hi, zr * hi + zi * hr], axis=1).astype(BF16)
        s2_ref[...] = jnp.dot(pb, ginv_ref[...], preferred_element_type=F32)

    def inv_a(c):
        row0 = pl.multiple_of(c * FFT_N1, FFT_N1)
        y = s2_ref[pl.ds(row0, FFT_N1), :]
        yr, yi = y[:, :LANES], y[:, LANES:]
        twr, twi = tw_ref[:, :LANES], tw_ref[:, LANES:]
        ur = (yr * twr + yi * twi).astype(BF16)
        ui = (yi * twr - yr * twi).astype(BF16)
        out = (jnp.dot(mir_ref[...], ur, preferred_element_type=F32)
               + jnp.dot(mii_ref[...], ui, preferred_element_type=F32))
        return out[:half], out[half:]

    def fwd_a(c, xr, xi):
        xs = jnp.concatenate([xr, xi], axis=0).astype(BF16)
        y = jnp.dot(ma_ref[...], xs, preferred_element_type=F32)
        _fwd_twiddle_store(y, tw_ref, s1_ref, pl.multiple_of(c * FFT_N1, FFT_N1))

    def pass1_a(c, carry):
        vr = _short_conv(v_ref[0, c], pv_ref, c)
        vi = _short_conv(v_ref[1, c], pv_ref, c)
        vc_ref[0, c] = vr
        vc_ref[1, c] = vi
        fwd_a(c, vr, vi)
        return carry

    def pass1_b(c, carry):
        cr, ci = inv_a(c)
        bias = fb_ref[0, pl.ds(c, 1), :]
        zr = _short_conv(x1_ref[0, c], p1_ref, c) * (cr + bias * vc_ref[0, c])
        zi = _short_conv(x1_ref[1, c], p1_ref, c) * (ci + bias * vc_ref[1, c])
        z1_ref[0, c] = zr
        z1_ref[1, c] = zi
        fwd_a(c, zr, zi)
        return carry

    def pass2_b(c, carry):
        cr, ci = inv_a(c)
        bias = fb_ref[1, pl.ds(c, 1), :]
        vc_ref[0, c] = _short_conv(x2_ref[0, c], p2_ref, c) * (cr + bias * z1_ref[0, c])
        vc_ref[1, c] = _short_conv(x2_ref[1, c], p2_ref, c) * (ci + bias * z1_ref[1, c])
        return carry

    lax.fori_loop(0, C, pass1_a, 0, unroll=SEQ_UNROLL)
    spectral(0)
    lax.fori_loop(0, C, pass1_b, 0, unroll=SEQ_UNROLL)
    spectral(1)
    lax.fori_loop(0, C, pass2_b, 0, unroll=SEQ_UNROLL)
    for b2 in range(2):
        tiles = pltpu.einshape("cab->acb", vc_ref[b2])
        for a in range(tiles.shape[0]):
            o_ref[b2, :, a * LANES:(a + 1) * LANES] = tiles[a]


def _hyena(u4, par_u, fb, hspec, cst):
    B = u4.shape[0]
    C = C_HY
    J = D_HYENA // C
    rows = u4.shape[2]
    full = lambda a: pl.BlockSpec(a.shape, lambda j, p: (0,) * a.ndim)
    ma, g, ginv = cst["ma"].astype(BF16), cst["g"].astype(BF16), cst["ginv"].astype(BF16)
    mir, mii = cst["minv_r"].astype(BF16), cst["minv_i"].astype(BF16)
    tw = cst["tw"]
    u_spec = lambda k: pl.BlockSpec((2, C, rows, LANES), lambda j, p, k=k: (p, j + k * J, 0, 0))
    par_spec = lambda k: pl.BlockSpec((4, C, LANES), lambda j, p, k=k: (0, j + k * J, 0))
    return pl.pallas_call(
        _hyena_body,
        grid=(J, B // 2),
        in_specs=[
            u_spec(0), u_spec(1), u_spec(2), par_spec(0), par_spec(1), par_spec(2),
            pl.BlockSpec((2, C, LANES), lambda j, p: (0, j, 0)),
            pl.BlockSpec((2, C, FFT_N1, 2 * LANES), lambda j, p: (0, j, 0, 0)),
            full(ma), full(tw), full(g), full(ginv), full(mir), full(mii),
        ],
        out_specs=pl.BlockSpec((2, C, rows * LANES), lambda j, p: (p, j, 0)),
        out_shape=jax.ShapeDtypeStruct((B, D_HYENA, rows * LANES), F32),
        scratch_shapes=[
            pltpu.VMEM((C * FFT_N1, 2 * LANES), BF16),
            pltpu.VMEM((C * FFT_N1, 2 * LANES), F32),
            pltpu.VMEM((2, C, rows, LANES), F32),
            pltpu.VMEM((2, C, rows, LANES), F32),
        ],
        compiler_params=_cparams(("parallel", "arbitrary")),
        name="hyena",
    )(u4, u4, u4, par_u, par_u, par_u, fb, hspec, ma, tw, g, ginv, mir, mii)


def _dot3(a, b):
    ah = a.astype(BF16)
    al = (a - ah.astype(F32)).astype(BF16)
    bh = b.astype(BF16)
    bl = (b - bh.astype(F32)).astype(BF16)
    return (jnp.dot(ah, bh, preferred_element_type=F32) + jnp.dot(al, bh, preferred_element_type=F32)
            + jnp.dot(ah, bl, preferred_element_type=F32))


def _filtgen_body(zt_ref, w1_ref, b1_ref, f1_ref, w2_ref, b2_ref, f2_ref, w3f_ref, w3b_ref, ad_ref, tt_ref,
                  o_ref, hid_ref):
    L = hid_ref.shape[1] // 2

    @pl.when(pl.program_id(0) == 0)
    def _():
        h1 = jnp.sin(f1_ref[...] * (_dot3(w1_ref[...], zt_ref[...]) + b1_ref[...]))
        hid_ref[...] = jnp.sin(f2_ref[...] * (_dot3(w2_ref[...], h1) + b2_ref[...]))

    ad = ad_ref[...]
    hf = _dot3(w3f_ref[...], hid_ref[:, :L]) * jnp.exp(-ad * tt_ref[:, :L])
    hb = _dot3(w3b_ref[...], hid_ref[:, L:]) * jnp.exp(-ad * tt_ref[:, L:])
    hf = hf / (jnp.sum(jnp.abs(hf), axis=-1, keepdims=True) + EPS)
    hb = hb / (jnp.sum(jnp.abs(hb), axis=-1, keepdims=True) + EPS)
    first = lax.broadcasted_iota(jnp.int32, hb.shape, 1) == 0
    cf = hf + jnp.where(first, hb, 0.0)
    cb = jnp.where(first, 0.0, hb)
    chunks = [c[:, a * LANES:(a + 1) * LANES] for c in (cf, cb) for a in range(L // LANES)]
    o_ref[...] = pltpu.einshape("arl->ral", jnp.stack(chunks, axis=0))


def _filter_gen(L, w_f1, b_f1, freq1, w_f2, b_f2, freq2, w_f3):
    bands = (FILTER_EMB - 1) // 2
    t = jnp.linspace(0.0, 1.0, L, dtype=F32)[:, None]
    w = (2.0 * math.pi / L) * jnp.arange(L, dtype=F32)[:, None]
    f = jnp.linspace(1e-4, bands - 1, bands, dtype=F32)[None]
    zf = f * w
    z = jnp.concatenate([t, jnp.cos(zf), -jnp.sin(zf)], axis=-1)
    back = lambda a: jnp.roll(a[::-1], 1, axis=0)
    kpad = 48
    zt = jnp.pad(jnp.concatenate([z, back(z)], axis=0).T, ((0, kpad - FILTER_EMB), (0, 0)))
    tt = jnp.concatenate([t, back(t)], axis=0).T
    w1t = jnp.pad(w_f1.T, ((0, 0), (0, kpad - FILTER_EMB)))
    w3 = w_f3.reshape(-1, 2, 2, D_HYENA)
    w3f = jnp.transpose(w3[:, :, 0], (1, 2, 0)).reshape(2 * D_HYENA, -1)
    w3b = jnp.transpose(w3[:, :, 1], (1, 2, 0)).reshape(2 * D_HYENA, -1)
    max_decay = math.log(DECAY_TARGET) / FAST_DECAY_PCT
    min_decay = math.log(DECAY_TARGET) / SLOW_DECAY_PCT
    deltas = jnp.linspace(min_decay, max_decay, D_HYENA, dtype=F32)
    ad = jnp.tile(jnp.abs(deltas), 2)[:, None]
    col = lambda v: v[:, None]
    R = 128
    n_rows = 2 * D_HYENA
    full = lambda a: pl.BlockSpec(a.shape, lambda i: (0,) * a.ndim)
    rows = lambda a: pl.BlockSpec((R, a.shape[1]), lambda i: (i, 0))
    args = (zt, w1t, col(b_f1), col(freq1), w_f2.T, col(b_f2), col(freq2))
    return pl.pallas_call(
        _filtgen_body,
        grid=(n_rows // R,),
        in_specs=[full(a) for a in args] + [rows(w3f), rows(w3b), rows(ad), full(tt)],
        out_specs=pl.BlockSpec((R, 2 * L // LANES, LANES), lambda i: (i, 0, 0)),
        out_shape=jax.ShapeDtypeStruct((n_rows, 2 * L // LANES, LANES), F32),
        scratch_shapes=[pltpu.VMEM((w_f2.shape[1], 2 * L), F32)],
        compiler_params=_cparams(("arbitrary",)),
        name="filter_gen",
    )(*args, w3f, w3b, ad, tt)


def _route_lanes(lg):
    neg = -1e30
    lane = lax.broadcasted_iota(jnp.int32, lg.shape, 1)
    gmask = lane < N_GROUPS
    gl = jnp.where(gmask, lg, neg)
    gm = jnp.max(gl, axis=-1, keepdims=True)
    gsum = jnp.sum(jnp.where(gmask, jnp.exp(gl - gm), 0.0), axis=-1, keepdims=True)
    g_top = 1.0 / gsum
    g_sel = jnp.min(jnp.where(gl == gm, lane, LANES), axis=-1, keepdims=True)
    lo = N_GROUPS + EXPERTS_PER_GROUP * g_sel
    el = jnp.where((lane >= lo) & (lane < lo + EXPERTS_PER_GROUP), lg, neg)
    m1 = jnp.max(el, axis=-1, keepdims=True)
    i1 = jnp.min(jnp.where(el == m1, lane, LANES), axis=-1, keepdims=True)
    el2 = jnp.where(lane == i1, neg, el)
    m2 = jnp.max(el2, axis=-1, keepdims=True)
    i2 = jnp.min(jnp.where(el2 == m2, lane, LANES), axis=-1, keepdims=True)
    d = jnp.exp(m2 - m1)
    p1 = 1.0 / (1.0 + d)
    p2 = d / (1.0 + d)
    e1 = (i1 - N_GROUPS).astype(F32)
    e2 = (i2 - N_GROUPS).astype(F32)
    return jnp.where(lane == 0, e1, jnp.where(lane == 1, e2, jnp.where(lane == 2, g_top * p1,
                     jnp.where(lane == 3, g_top * p2, 0.0))))


def _pack_bf16_halves(a):
    w = a.shape[1] // 2
    bits = pltpu.bitcast(a.astype(BF16).astype(F32), jnp.uint32)
    return (bits[:, :w] >> 16) | (bits[:, w:] & jnp.uint32(0xFFFF0000))


def _unpack_bf16_halves(wd):
    lo = pltpu.bitcast(wd << 16, F32)
    hi = pltpu.bitcast(wd & jnp.uint32(0xFFFF0000), F32)
    return jnp.concatenate([lo, hi], axis=1)


def _store_row_tiles(ref, packed):
    chunks = jnp.stack([packed[:, j * LANES:(j + 1) * LANES] for j in range(ROW_CHUNKS)], axis=0)
    ref[...] = pltpu.einshape("jrl->rjl", chunks)


def _load_row_tiles(ref):
    chunks = pltpu.einshape("rjl->jrl", ref[...])
    return jnp.concatenate([chunks[j] for j in range(ROW_CHUNKS)], axis=1)


def _outproj_body(ya_ref, yh_ref, x_ref, ga_ref, gh_ref, wo_ref, bd_ref, gm_ref, wrh_ref, wrl_ref, brt_ref,
                  x1_ref, h2_ref, rt_ref, rtt_ref):
    ya = ya_ref[...]
    yan = ya * lax.rsqrt(_group_sumsq(ya, bd_ref[...]) * (1.0 / HEAD_DIM) + EPS) * ga_ref[...]
    yh = yh_ref[...]
    tm = yh.shape[1]
    yh3 = yh.reshape(D_HYENA // HYENA_HEAD, HYENA_HEAD, tm)
    ms = jnp.mean(yh3 * yh3, axis=1, keepdims=True)
    yhn = (yh3 * lax.rsqrt(ms + EPS)).reshape(D_HYENA, tm) * gh_ref[...]
    mix = (jnp.dot(yan.astype(BF16), wo_ref[:D_ATTN, :], preferred_element_type=F32)
           + jnp.dot(yhn.T.astype(BF16), wo_ref[D_ATTN:, :], preferred_element_type=F32))
    x1 = x_ref[...] + mix
    x1_ref[...] = x1
    h2 = x1 * lax.rsqrt(jnp.mean(x1 * x1, axis=-1, keepdims=True) + EPS) * gm_ref[...]
    _store_row_tiles(h2_ref, _pack_bf16_halves(h2))
    hi = h2.astype(BF16)
    lo = (h2 - hi.astype(F32)).astype(BF16)
    lg = (jnp.dot(hi, wrh_ref[...], preferred_element_type=F32)
          + jnp.dot(lo, wrh_ref[...], preferred_element_type=F32)
          + jnp.dot(hi, wrl_ref[...], preferred_element_type=F32)) + brt_ref[...]
    route = _route_lanes(lg)
    rt_ref[...] = route
    rtt_ref[...] = route.T[:8]


def _outproj(ya, yht, x, ga, gh, wo, bd, gm, wrh, wrl, brt):
    B, S, D = x.shape
    tm = TM_PROJ
    full = lambda a: pl.BlockSpec(a.shape, lambda b, i: (0,) * a.ndim)
    return pl.pallas_call(
        _outproj_body,
        grid=(B, S // tm),
        in_specs=[
            pl.BlockSpec((None, tm, D_ATTN), lambda b, i: (b, i, 0)),
            pl.BlockSpec((None, D_HYENA, tm), lambda b, i: (b, 0, i)),
            pl.BlockSpec((None, tm, D), lambda b, i: (b, i, 0)),
            full(ga), full(gh), full(wo), full(bd), full(gm), full(wrh), full(wrl), full(brt),
        ],
        out_specs=[
            pl.BlockSpec((None, tm, D), lambda b, i: (b, i, 0)),
            pl.BlockSpec((None, tm, ROW_CHUNKS, LANES), lambda b, i: (b, i, 0, 0)),
            pl.BlockSpec((None, tm, LANES), lambda b, i: (b, i, 0)),
            pl.BlockSpec((None, 8, tm), lambda b, i: (b, 0, i)),
        ],
        out_shape=[
            jax.ShapeDtypeStruct((B, S, D), F32),
            jax.ShapeDtypeStruct((B, S, ROW_CHUNKS, LANES), jnp.uint32),
            jax.ShapeDtypeStruct((B, S, LANES), F32),
            jax.ShapeDtypeStruct((B, 8, S), F32),
        ],
        compiler_params=_cparams(("parallel", "parallel")),
        name="outproj",
    )(ya, yht, x, ga, gh, wo, bd, gm, wrh, wrl, brt)


def _moe_body(be_ref, ra_ref, nlive_ref, h2_hbm, wg_ref, wu_ref, wd_ref, y_hbm,
              wg_s, wu_s, wd_s, xbuf, ybuf, zbuf, sem_in, sem_out, sem_z, *, n_tok, n_rows):
    i = pl.program_id(0)
    nb = nlive_ref[0]
    T = xbuf.shape[1]
    slot = i % 2

    def issue_gathers(blk, sl):
        for r in range(T):
            tok = ra_ref[blk * T + r] & (n_tok - 1)
            pltpu.make_async_copy(h2_hbm.at[tok], xbuf.at[sl, r], sem_in.at[sl]).start(priority=r % 2)

    def issue_scatters(blk, sl, spare):
        for r in range(T):
            dst = jnp.where(spare, n_rows + r, ra_ref[blk * T + r])
            pltpu.make_async_copy(ybuf.at[sl, r], y_hbm.at[dst], sem_out.at[sl]).start(priority=r % 2)

    def block_in_wait(sl):
        pltpu.make_async_copy(h2_hbm.at[pl.ds(0, T)], xbuf.at[sl], sem_in.at[sl]).wait()

    def block_out_wait(sl):
        pltpu.make_async_copy(ybuf.at[sl], y_hbm.at[pl.ds(0, T)], sem_out.at[sl]).wait()

    @pl.when(i == 0)
    def _():
        ybuf[...] = jnp.zeros(ybuf.shape, ybuf.dtype)
        zbuf[...] = jnp.zeros(zbuf.shape, zbuf.dtype)
        issue_gathers(0, 0)

    @pl.when(i >= nb)
    def _():
        fill = pltpu.make_async_copy(zbuf, y_hbm.at[pl.ds(i * T, T)], sem_z.at[0])
        fill.start()
        fill.wait()

    prev = be_ref[jnp.maximum(i - 1, 0)]

    @pl.when((i == 0) | (be_ref[i] != prev))
    def _():
        wg_s[...] = wg_ref[...].astype(BF16)
        wu_s[...] = wu_ref[...].astype(BF16)
        wd_s[...] = wd_ref[...].astype(BF16)

    @pl.when(i < nb)
    def _():
        block_in_wait(slot)
        x = _unpack_bf16_halves(_load_row_tiles(xbuf.at[slot])).astype(BF16)
        issue_gathers(jnp.minimum(i + 1, nb - 1), 1 - slot)
        issue_scatters(jnp.maximum(i - 1, 0), 1 - slot, i == 0)
        a = jnp.dot(x, wg_s[...], preferred_element_type=F32)
        b = jnp.dot(x, wu_s[...], preferred_element_type=F32)
        hmid = (a * jax.nn.sigmoid(a)) * b
        y = _pack_bf16_halves(jnp.dot(hmid.astype(BF16), wd_s[...], preferred_element_type=F32))

        @pl.when(i >= 1)
        def _():
            block_out_wait(slot)

        _store_row_tiles(ybuf.at[slot], y)

        @pl.when(i == nb - 1)
        def _():
            issue_scatters(i, slot, False)
            block_in_wait(1 - slot)
            block_out_wait(1 - slot)
            block_out_wait(slot)


def _moe_experts(block_e, row_a, n_live, h2p, w_gate, w_up, w_down):
    n_tok = h2p.shape[0]
    n_rows = row_a.shape[0]
    row = h2p.shape[1:]
    D = w_gate.shape[1]
    T = TB_MOE
    assert n_tok & (n_tok - 1) == 0
    grid_spec = pltpu.PrefetchScalarGridSpec(
        num_scalar_prefetch=3,
        grid=(row_a.shape[0] // T,),
        in_specs=[
            pl.BlockSpec(memory_space=pl.ANY),
            pl.BlockSpec((None, D, D_EXPERT), lambda i, be, ra, nl: (be[i], 0, 0)),
            pl.BlockSpec((None, D, D_EXPERT), lambda i, be, ra, nl: (be[i], 0, 0)),
            pl.BlockSpec((None, D_EXPERT, D), lambda i, be, ra, nl: (be[i], 0, 0)),
        ],
        out_specs=pl.BlockSpec(memory_space=pl.ANY),
        scratch_shapes=[
            pltpu.VMEM((D, D_EXPERT), BF16), pltpu.VMEM((D, D_EXPERT), BF16), pltpu.VMEM((D_EXPERT, D), BF16),
            pltpu.VMEM((2, T) + row, jnp.uint32), pltpu.VMEM((2, T) + row, jnp.uint32),
            pltpu.VMEM((T,) + row, jnp.uint32),
            pltpu.SemaphoreType.DMA((2,)), pltpu.SemaphoreType.DMA((2,)), pltpu.SemaphoreType.DMA((1,)),
        ],
    )
    return pl.pallas_call(
        functools.partial(_moe_body, n_tok=n_tok, n_rows=n_rows),
        grid_spec=grid_spec,
        out_shape=jax.ShapeDtypeStruct((n_rows + T,) + row, jnp.uint32),
        compiler_params=_cparams(("arbitrary",)),
        name="moe_experts",
    )(block_e, row_a, n_live, h2p, w_gate, w_up, w_down)


def _dispatch(e_flat, N):
    T = TB_MOE
    NK = N * TOP_K
    experts = jnp.arange(N_EXPERTS, dtype=jnp.int32)
    counts = jnp.sum((e_flat[:, None] == experts[None]).astype(jnp.int32), axis=0)
    padded = (counts + T - 1) // T * T
    pad = padded - counts
    pends = jnp.cumsum(padded)
    n_rows = -(-(NK + N_EXPERTS * (T - 1)) // T) * T
    blk_start = jnp.arange(n_rows // T, dtype=jnp.int32) * T
    block_e = jnp.clip(jnp.sum((pends[None, :] <= blk_start[:, None]).astype(jnp.int32), axis=1),
                       0, N_EXPERTS - 1)
    shift = 17
    assert NK <= 1 << (shift - 1) and N_EXPERTS << shift < 1 << 30
    big = jnp.int32(1 << 30)
    j = jnp.arange(T - 1, dtype=jnp.int32)[None]
    real_key = e_flat * (1 << shift) + jnp.arange(NK, dtype=jnp.int32)
    pad_key = jnp.where(j < pad[:, None], experts[:, None] * (1 << shift) + (1 << (shift - 1)) + j, big)
    pad_val = NK + (jnp.cumsum(pad) - pad)[:, None] + j
    n_fill = n_rows - NK - N_EXPERTS * (T - 1)
    keys = jnp.concatenate([real_key, pad_key.reshape(-1), jnp.full((n_fill,), big, jnp.int32)])
    vals = jnp.concatenate([jnp.arange(NK, dtype=jnp.int32), pad_val.reshape(-1).astype(jnp.int32),
                            jnp.zeros((n_fill,), jnp.int32)])
    _, row_a = lax.sort((keys, vals), num_keys=1)
    n_live = (pends[-1:] // T).astype(jnp.int32)
    return block_e.astype(jnp.int32), row_a, n_live


def _final_body(x1_ref, y0_ref, y1_ref, rt_ref, p_ref, gp_ref, wg_ref, bg_ref, wp_ref, gf_ref, o_ref):
    w0 = rt_ref[:, 2:3]
    w1 = rt_ref[:, 3:4]
    y0 = _unpack_bf16_halves(_load_row_tiles(y0_ref))
    y1 = _unpack_bf16_halves(_load_row_tiles(y1_ref))
    x2 = x1_ref[...] + (y0 * w0 + y1 * w1)
    hp = x2 * lax.rsqrt(jnp.mean(x2 * x2, axis=-1, keepdims=True) + EPS) * gp_ref[...]
    gate = jax.nn.sigmoid(jnp.dot(hp.astype(BF16), wg_ref[...], preferred_element_type=F32) + bg_ref[...])
    pe = jnp.dot(p_ref[...].astype(BF16), wp_ref[...], preferred_element_type=F32)
    x3 = x2 + pe * gate
    o_ref[...] = x3 * lax.rsqrt(jnp.mean(x3 * x3, axis=-1, keepdims=True) + EPS) * gf_ref[...]


def _final(x1, y, route, p, gp, wg, bg, wp, gf):
    N, D = x1.shape
    tm = TM_PROJ
    row = lambda w: pl.BlockSpec((tm, w), lambda i: (i, 0))
    full = lambda a: pl.BlockSpec(a.shape, lambda i: (0,) * a.ndim)
    y0, y1 = y, y
    return pl.pallas_call(
        _final_body,
        grid=(N // tm,),
        in_specs=[row(D), pl.BlockSpec((tm, ROW_CHUNKS, LANES), lambda i: (i, 0, 0)),
                  pl.BlockSpec((tm, ROW_CHUNKS, LANES), lambda i: (i + N // tm, 0, 0)),
                  row(LANES), row(p.shape[1]),
                  full(gp), full(wg), full(bg), full(wp), full(gf)],
        out_specs=row(D),
        out_shape=jax.ShapeDtypeStruct((N, D), F32),
        compiler_params=_cparams(("parallel",)),
        name="ple_final",
    )(x1, y0, y1, route, p, gp, wg, bg, wp, gf)


def kernel(x, p, g_mix, w_in, q_gain, k_gain, conv_w, conv_b, w_f1, b_f1, freq1, w_f2, b_f2, freq2, w_f3, filt_bias, g_attn_out, g_hyena_out, w_out, g_moe, w_group, b_group, w_router, b_router, w_gate, w_up, w_down, g_ple, w_ple_gate, b_ple_gate, w_ple, g_final):
    B, S, D = x.shape
    N = B * S
    assert p.shape[0] == 1 and S == (FFT_N1 // 2) * FFT_N2 and B % 2 == 0
    i = 0
    cst = _dft_constants()
    cos, sin = _rope_tables(S)
    bd = _block_diag_ones(D_ATTN, HEAD_DIM)

    n_qkv = D_ATTN + 2 * D_KV
    wqkv = w_in[i][:, :n_qkv].astype(BF16)
    wut = w_in[i][:, n_qkv:].T.astype(BF16)
    q, kw, vw, ut = _inproj(x, g_mix[i][None], wqkv, wut, bd,
                            jnp.tile(q_gain[i], N_HEADS)[None], jnp.tile(k_gain[i], N_KV_HEADS)[None], cos, sin)

    ya = _attention(q, kw, vw)

    circ = _filter_gen(S, w_f1[i], b_f1[i], freq1[i], w_f2[i], b_f2[i], freq2[i], w_f3[i])
    hspec = _filter_fft(circ, cst)
    hspec = hspec.reshape(2, D_HYENA, FFT_N1, 2 * LANES)
    du = ut.shape[1]
    u4 = ut.reshape(B, du, S // LANES, LANES)
    par_u = jnp.broadcast_to(jnp.concatenate([conv_w[i], conv_b[i][None]], 0)[:, :, None], (4, du, LANES))
    fb = jnp.broadcast_to(filt_bias[i][:, :, None], (2, D_HYENA, LANES))
    yht = _hyena(u4, par_u, fb, hspec, cst)

    wrt = jnp.zeros((D, LANES), F32).at[:, :N_GROUPS].set(w_group[i]).at[:, N_GROUPS:N_GROUPS + N_EXPERTS].set(w_router[i])
    brt = jnp.zeros((1, LANES), F32).at[0, :N_GROUPS].set(b_group[i]).at[0, N_GROUPS:N_GROUPS + N_EXPERTS].set(b_router[i])
    wrh = wrt.astype(BF16)
    wrl = (wrt - wrh.astype(F32)).astype(BF16)
    x1, h2, route, route_t = _outproj(ya, yht, x, g_attn_out[i][None], g_hyena_out[i][:, None],
                                      w_out[i].astype(BF16), bd, g_moe[i][None], wrh, wrl, brt)

    e_flat = jnp.transpose(route_t[:, :TOP_K], (1, 0, 2)).reshape(TOP_K * N).astype(jnp.int32)
    block_e, row_a, n_live = _dispatch(e_flat, N)
    y = _moe_experts(block_e, row_a, n_live, h2.reshape(N, ROW_CHUNKS, LANES), w_gate[i], w_up[i], w_down[i])

    out = _final(x1.reshape(N, D), y, route.reshape(N, LANES), p[i].reshape(N, -1), g_ple[i][None],
                 w_ple_gate[i].astype(BF16), b_ple_gate[i][None], w_ple[i].astype(BF16), g_final[None])
    return out.reshape(B, S, D)
```

```python
import functools
import math

import numpy as np
import jax
import jax.numpy as jnp
from jax import lax
from jax.experimental import pallas as pl
from jax.experimental.pallas import tpu as pltpu

F32 = jnp.float32
BF16 = jnp.bfloat16

D_MODEL = 1024
EPS = 1e-6
GRID_W = 64
N_HEADS = 8
N_KV_HEADS = 2
HEAD_DIM = 64
D_ATTN = N_HEADS * HEAD_DIM
D_KV = N_KV_HEADS * HEAD_DIM
ROPE_THETA = 10000.0
D_HYENA = 512
HYENA_HEAD = 64
FILTER_EMB = 33
FAST_DECAY_PCT = 0.3
SLOW_DECAY_PCT = 1.5
DECAY_TARGET = 1e-2
N_GROUPS = 4
EXPERTS_PER_GROUP = 8
N_EXPERTS = N_GROUPS * EXPERTS_PER_GROUP
TOP_K = 2
D_EXPERT = 512

LANES = 128
MXU_TILE = 256
FFT_N1 = 64
FFT_N2 = 128
VMEM_LIMIT = 56 * 1024 * 1024

TM_PROJ = 512
TQ_ATTN = 256
C_HY = 32
C_FILT = 64
ROW_CHUNKS = D_MODEL // 2 // LANES
SEQ_UNROLL = 32
TB_MOE = 256


def _cparams(sem):
    return pltpu.CompilerParams(dimension_semantics=sem, vmem_limit_bytes=VMEM_LIMIT)


def _rope_tables(S):
    half = HEAD_DIM // 2
    t = jnp.arange(S, dtype=F32)
    r_idx = jnp.floor(t / GRID_W)
    c_idx = t - r_idx * GRID_W
    inv = ROPE_THETA ** (-jnp.arange(0, half, 2, dtype=F32) / half)
    ang_r = r_idx[:, None] * inv[None]
    ang_c = c_idx[:, None] * inv[None]
    cos_h = jnp.concatenate([jnp.cos(ang_r), jnp.cos(ang_r), jnp.cos(ang_c), jnp.cos(ang_c)], axis=-1)
    sin_h = jnp.concatenate([-jnp.sin(ang_r), jnp.sin(ang_r), -jnp.sin(ang_c), jnp.sin(ang_c)], axis=-1)
    return jnp.tile(cos_h, (1, 2)), jnp.tile(sin_h, (1, 2))


def _dft_constants():
    n1, n2 = FFT_N1, FFT_N2
    n = n1 * n2
    a = np.arange(n1)
    ang = 2.0 * np.pi * np.outer(a, a) / n1
    far, fai = np.cos(ang), -np.sin(ang)
    hlf = n1 // 2
    ma = np.block([[far[:, :hlf], -fai[:, :hlf]], [fai[:, :hlf], far[:, :hlf]]])
    maf = np.concatenate([far, fai], axis=0)
    b = np.arange(n2)
    angt = 2.0 * np.pi * np.outer(a, b) / n
    tw = np.concatenate([np.cos(angt), -np.sin(angt)], axis=1)
    angb = 2.0 * np.pi * np.outer(b, b) / n2
    fbr, fbi = np.cos(angb), -np.sin(angb)
    g = np.block([[fbr, fbi], [-fbi, fbr]])
    ginv = np.block([[fbr, -fbi], [fbi, fbr]])
    minv_r = np.concatenate([far[:hlf], -fai[:hlf]], axis=0) / n
    minv_i = np.concatenate([fai[:hlf], far[:hlf]], axis=0) / n
    f = lambda m: jnp.asarray(m.astype(np.float32))
    return dict(ma=f(ma), maf=f(maf), tw=f(tw), g=f(g), ginv=f(ginv), minv_r=f(minv_r), minv_i=f(minv_i))


def _block_diag_ones(width, group):
    i = np.arange(width) // group
    return jnp.asarray((i[:, None] == i[None, :]).astype(np.float32)).astype(BF16)


def _group_sumsq(a, bd):
    sq = a * a
    hi = sq.astype(BF16)
    lo = (sq - hi.astype(F32)).astype(BF16)
    w = min(a.shape[-1], MXU_TILE)
    return jnp.concatenate(
        [jnp.dot(hi[:, c:c + w], bd[c:c + w, c:c + w], preferred_element_type=F32)
         + jnp.dot(lo[:, c:c + w], bd[c:c + w, c:c + w], preferred_element_type=F32)
         for c in range(0, a.shape[-1], w)], axis=-1)


def _head_norm_rope(a, gain, bd, cos, sin):
    width = a.shape[-1]
    n = a * lax.rsqrt(_group_sumsq(a, bd) * (1.0 / HEAD_DIM) + EPS) * gain
    rep = width // LANES
    if rep > 1:
        cos = jnp.concatenate([cos] * rep, axis=-1)
        sin = jnp.concatenate([sin] * rep, axis=-1)
    fwd = pltpu.roll(n, width - 16, 1)
    bwd = pltpu.roll(n, 16, 1)
    lane = lax.broadcasted_iota(jnp.int32, n.shape, 1)
    sw = jnp.where((lane % 32) < 16, fwd, bwd)
    return n * cos + sw * sin


def _inproj_body(x_ref, g_ref, wqkv_ref, wu_ref, bd_ref, qg_ref, kg_ref, cos_ref, sin_ref,
                 q_ref, kw_ref, vw_ref, ut_ref):
    x = x_ref[...]
    h = x * lax.rsqrt(jnp.mean(x * x, axis=-1, keepdims=True) + EPS) * g_ref[...]
    hb = h.astype(BF16)
    qkv = jnp.dot(hb, wqkv_ref[...], preferred_element_type=F32)
    cos = cos_ref[...]
    sin = sin_ref[...]
    bd = bd_ref[...]
    q = _head_norm_rope(qkv[:, :D_ATTN], qg_ref[...], bd, cos, sin)
    q_ref[...] = (q * (HEAD_DIM ** -0.5 * math.log2(math.e))).astype(BF16)
    k = _head_norm_rope(qkv[:, D_ATTN:D_ATTN + D_KV], kg_ref[...], bd[:D_KV, :D_KV], cos, sin)
    kt = k.T.astype(BF16)
    zero = jnp.zeros((HEAD_DIM, kt.shape[1]), BF16)
    for h in range(N_KV_HEADS):
        kh = kt[h * HEAD_DIM:(h + 1) * HEAD_DIM]
        kw_ref[h, 0, :HEAD_DIM] = kh
        kw_ref[h, 0, HEAD_DIM:] = zero
        kw_ref[h, 1, :HEAD_DIM] = zero
        kw_ref[h, 1, HEAD_DIM:] = kh
    v = qkv[:, D_ATTN + D_KV:]
    vr = pltpu.roll(v, HEAD_DIM, 1)
    first = lax.broadcasted_iota(jnp.int32, v.shape, 1) < HEAD_DIM
    vw_ref[0, 0] = jnp.where(first, v, 1.0).astype(BF16)
    vw_ref[0, 1] = jnp.where(first, 1.0, vr).astype(BF16)
    vw_ref[1, 0] = jnp.where(first, vr, 1.0).astype(BF16)
    vw_ref[1, 1] = jnp.where(first, 1.0, v).astype(BF16)
    ut_ref[...] = lax.dot_general(wu_ref[...], hb, (((1,), (1,)), ((), ())),
                                  preferred_element_type=F32)


def _inproj(x, g_mix, wqkv, wut, bd, qg, kg, cos, sin):
    B, S, D = x.shape
    tm = TM_PROJ
    du = wut.shape[0]
    full = lambda shape: pl.BlockSpec(shape, lambda b, i: (0,) * len(shape))
    return pl.pallas_call(
        _inproj_body,
        grid=(B, S // tm),
        in_specs=[
            pl.BlockSpec((None, tm, D), lambda b, i: (b, i, 0)),
            full((1, D)), full(wqkv.shape), full(wut.shape), full(bd.shape),
            full((1, D_ATTN)), full((1, D_KV)),
            pl.BlockSpec((tm, LANES), lambda b, i: (i, 0)),
            pl.BlockSpec((tm, LANES), lambda b, i: (i, 0)),
        ],
        out_specs=[
            pl.BlockSpec((None, tm, D_ATTN), lambda b, i: (b, i, 0)),
            pl.BlockSpec((None, N_KV_HEADS, 2, LANES, tm), lambda b, i: (b, 0, 0, 0, i)),
            pl.BlockSpec((None, N_KV_HEADS, 2, tm, LANES), lambda b, i: (b, 0, 0, i, 0)),
            pl.BlockSpec((None, du, tm), lambda b, i: (b, 0, i)),
        ],
        out_shape=[
            jax.ShapeDtypeStruct((B, S, D_ATTN), BF16),
            jax.ShapeDtypeStruct((B, N_KV_HEADS, 2, LANES, S), BF16),
            jax.ShapeDtypeStruct((B, N_KV_HEADS, 2, S, LANES), BF16),
            jax.ShapeDtypeStruct((B, du, S), F32),
        ],
        compiler_params=_cparams(("parallel", "parallel")),
        name="inproj",
    )(x, g_mix, wqkv, wut, bd, qg, kg, cos, sin)


def _attn_body(q_ref, kw_ref, vw_ref, o_ref):

    def one_head(q, kw, vw):
        s = jnp.dot(q, kw, preferred_element_type=F32)
        m = jnp.max(s, axis=-1, keepdims=True)
        p = jnp.exp2(s - m).astype(BF16)
        return jnp.dot(p, vw, preferred_element_type=F32)

    for pair in range(D_ATTN // LANES):
        h = pair // (N_HEADS // N_KV_HEADS // 2)
        q = q_ref[:, pair * LANES:(pair + 1) * LANES]
        oe = one_head(q, kw_ref[h, 0], vw_ref[h, 0])
        oo = one_head(q, kw_ref[h, 1], vw_ref[h, 1])
        first = lax.broadcasted_iota(jnp.int32, oe.shape, 1) < HEAD_DIM
        num = jnp.where(first, oe, oo)
        den = jnp.where(first, pltpu.roll(oe, HEAD_DIM, 1), pltpu.roll(oo, HEAD_DIM, 1))
        o_ref[:, pair * LANES:(pair + 1) * LANES] = num / den


def _attention(q, kw, vw):
    B, S, _ = q.shape
    tq = TQ_ATTN
    return pl.pallas_call(
        _attn_body,
        grid=(B, S // tq),
        in_specs=[
            pl.BlockSpec((None, tq, D_ATTN), lambda b, i: (b, i, 0)),
            pl.BlockSpec((None, N_KV_HEADS, 2, LANES, S), lambda b, i: (b, 0, 0, 0, 0)),
            pl.BlockSpec((None, N_KV_HEADS, 2, S, LANES), lambda b, i: (b, 0, 0, 0, 0)),
        ],
        out_specs=pl.BlockSpec((None, tq, D_ATTN), lambda b, i: (b, i, 0)),
        out_shape=jax.ShapeDtypeStruct((B, S, D_ATTN), F32),
        compiler_params=_cparams(("parallel", "arbitrary")),
        name="attention",
    )(q, kw, vw)


def _fwd_twiddle_store(y, tw_ref, s1_ref, row0):
    yr, yi = y[:FFT_N1], y[FFT_N1:]
    twr, twi = tw_ref[:, :LANES], tw_ref[:, LANES:]
    s1_ref[pl.ds(row0, FFT_N1), :LANES] = (yr * twr - yi * twi).astype(BF16)
    s1_ref[pl.ds(row0, FFT_N1), LANES:] = (yr * twi + yi * twr).astype(BF16)


def _filtfft_body(x_ref, maf_ref, tw_ref, g_ref, h_ref, s1_ref):
    C = x_ref.shape[0]

    def step_a(c, carry):
        y = jnp.dot(maf_ref[...], x_ref[c].astype(BF16), preferred_element_type=F32)
        _fwd_twiddle_store(y, tw_ref, s1_ref, pl.multiple_of(c * FFT_N1, FFT_N1))
        return carry

    lax.fori_loop(0, C, step_a, 0, unroll=SEQ_UNROLL)
    z = jnp.dot(s1_ref[...], g_ref[...], preferred_element_type=F32)
    h_ref[...] = z.reshape(C, FFT_N1, 2 * LANES)


def _filter_fft(circ, cst):
    n_seq = circ.shape[0]
    C = C_FILT
    full = lambda a: pl.BlockSpec(a.shape, lambda i: (0,) * a.ndim)
    maf, tw, g = cst["maf"].astype(BF16), cst["tw"], cst["g"].astype(BF16)
    return pl.pallas_call(
        _filtfft_body,
        grid=(n_seq // C,),
        in_specs=[pl.BlockSpec((C, FFT_N1, FFT_N2), lambda i: (i, 0, 0)), full(maf), full(tw), full(g)],
        out_specs=pl.BlockSpec((C, FFT_N1, 2 * LANES), lambda i: (i, 0, 0)),
        out_shape=jax.ShapeDtypeStruct((n_seq, FFT_N1, 2 * LANES), F32),
        scratch_shapes=[pltpu.VMEM((C * FFT_N1, 2 * LANES), BF16)],
        compiler_params=_cparams(("parallel",)),
        name="filter_fft",
    )(circ, maf, tw, g)


def _short_conv(x, par_ref, c):
    rows, lanes = x.shape
    a_i = lax.broadcasted_iota(jnp.int32, x.shape, 0)
    b_i = lax.broadcasted_iota(jnp.int32, x.shape, 1)
    l1 = pltpu.roll(x, 1, 1)
    l2 = pltpu.roll(l1, 1, 0)
    prev = jnp.where(b_i == 0, l2, l1)
    prev = jnp.where((a_i == 0) & (b_i == 0), 0.0, prev)
    r1 = pltpu.roll(x, lanes - 1, 1)
    r2 = pltpu.roll(r1, rows - 1, 0)
    nxt = jnp.where(b_i == lanes - 1, r2, r1)
    nxt = jnp.where((a_i == rows - 1) & (b_i == lanes - 1), 0.0, nxt)
    w0 = par_ref[0, pl.ds(c, 1), :]
    w1 = par_ref[1, pl.ds(c, 1), :]
    w2 = par_ref[2, pl.ds(c, 1), :]
    cb = par_ref[3, pl.ds(c, 1), :]
    return cb + prev * w0 + x * w1 + nxt * w2


def _hyena_body(v_ref, x1_ref, x2_ref, pv_ref, p1_ref, p2_ref, fb_ref, h_ref,
                ma_ref, tw_ref, g_ref, ginv_ref, mir_ref, mii_ref,
                o_ref, s1_ref, s2_ref, vc_ref, z1_ref):
    C = v_ref.shape[1]
    half = FFT_N1 // 2

    def spectral(order):
        z = jnp.dot(s1_ref[...], g_ref[...], preferred_element_type=F32)
        hs = h_ref[order].reshape(C * FFT_N1, 2 * LANES)
        zr, zi = z[:, :LANES], z[:, LANES:]
        hr, hi = hs[:, :LANES], hs[:, LANES:]
        pb = jnp.concatenate([zr * hr - zi * hi, zr * hi + zi * hr], axis=1).astype(BF16)
        s2_ref[...] = jnp.dot(pb, ginv_ref[...], preferred_element_type=F32)

    def inv_a(c):
        row0 = pl.multiple_of(c * FFT_N1, FFT_N1)
        y = s2_ref[pl.ds(row0, FFT_N1), :]
        yr, yi = y[:, :LANES], y[:, LANES:]
        twr, twi = tw_ref[:, :LANES], tw_ref[:, LANES:]
        ur = (yr * twr + yi * twi).astype(BF16)
        ui = (yi * twr - yr * twi).astype(BF16)
        out = (jnp.dot(mir_ref[...], ur, preferred_element_type=F32)
               + jnp.dot(mii_ref[...], ui, preferred_element_type=F32))
        return out[:half], out[half:]

    def fwd_a(c, xr, xi):
        xs = jnp.concatenate([xr, xi], axis=0).astype(BF16)
        y = jnp.dot(ma_ref[...], xs, preferred_element_type=F32)
        _fwd_twiddle_store(y, tw_ref, s1_ref, pl.multiple_of(c * FFT_N1, FFT_N1))

    def pass1_a(c, carry):
        vr = _short_conv(v_ref[0, c], pv_ref, c)
        vi = _short_conv(v_ref[1, c], pv_ref, c)
        vc_ref[0, c] = vr
        vc_ref[1, c] = vi
        fwd_a(c, vr, vi)
        return carry

    def pass1_b(c, carry):
        cr, ci = inv_a(c)
        bias = fb_ref[0, pl.ds(c, 1), :]
        zr = _short_conv(x1_ref[0, c], p1_ref, c) * (cr + bias * vc_ref[0, c])
        zi = _short_conv(x1_ref[1, c], p1_ref, c) * (ci + bias * vc_ref[1, c])
        z1_ref[0, c] = zr
        z1_ref[1, c] = zi
        fwd_a(c, zr, zi)
        return carry

    def pass2_b(c, carry):
        cr, ci = inv_a(c)
        bias = fb_ref[1, pl.ds(c, 1), :]
        vc_ref[0, c] = _short_conv(x2_ref[0, c], p2_ref, c) * (cr + bias * z1_ref[0, c])
        vc_ref[1, c] = _short_conv(x2_ref[1, c], p2_ref, c) * (ci + bias * z1_ref[1, c])
        return carry

    lax.fori_loop(0, C, pass1_a, 0, unroll=SEQ_UNROLL)
    spectral(0)
    lax.fori_loop(0, C, pass1_b, 0, unroll=SEQ_UNROLL)
    spectral(1)
    lax.fori_loop(0, C, pass2_b, 0, unroll=SEQ_UNROLL)
    for b2 in range(2):
        tiles = pltpu.einshape("cab->acb", vc_ref[b2])
        for a in range(tiles.shape[0]):
            o_ref[b2, :, a * LANES:(a + 1) * LANES] = tiles[a]


def _hyena(u4, par_u, fb, hspec, cst):
    B = u4.shape[0]
    C = C_HY
    J = D_HYENA // C
    rows = u4.shape[2]
    full = lambda a: pl.BlockSpec(a.shape, lambda j, p: (0,) * a.ndim)
    ma, g, ginv = cst["ma"].astype(BF16), cst["g"].astype(BF16), cst["ginv"].astype(BF16)
    mir, mii = cst["minv_r"].astype(BF16), cst["minv_i"].astype(BF16)
    tw = cst["tw"]
    u_spec = lambda k: pl.BlockSpec((2, C, rows, LANES), lambda j, p, k=k: (p, j + k * J, 0, 0))
    par_spec = lambda k: pl.BlockSpec((4, C, LANES), lambda j, p, k=k: (0, j + k * J, 0))
    return pl.pallas_call(
        _hyena_body,
        grid=(J, B // 2),
        in_specs=[
            u_spec(0), u_spec(1), u_spec(2), par_spec(0), par_spec(1), par_spec(2),
            pl.BlockSpec((2, C, LANES), lambda j, p: (0, j, 0)),
            pl.BlockSpec((2, C, FFT_N1, 2 * LANES), lambda j, p: (0, j, 0, 0)),
            full(ma), full(tw), full(g), full(ginv), full(mir), full(mii),
        ],
        out_specs=pl.BlockSpec((2, C, rows * LANES), lambda j, p: (p, j, 0)),
        out_shape=jax.ShapeDtypeStruct((B, D_HYENA, rows * LANES), F32),
        scratch_shapes=[
            pltpu.VMEM((C * FFT_N1, 2 * LANES), BF16),
            pltpu.VMEM((C * FFT_N1, 2 * LANES), F32),
            pltpu.VMEM((2, C, rows, LANES), F32),
            pltpu.VMEM((2, C, rows, LANES), F32),
        ],
        compiler_params=_cparams(("parallel", "arbitrary")),
        name="hyena",
    )(u4, u4, u4, par_u, par_u, par_u, fb, hspec, ma, tw, g, ginv, mir, mii)


def _dot3(a, b):
    ah = a.astype(BF16)
    al = (a - ah.astype(F32)).astype(BF16)
    bh = b.astype(BF16)
    bl = (b - bh.astype(F32)).astype(BF16)
    return (jnp.dot(ah, bh, preferred_element_type=F32) + jnp.dot(al, bh, preferred_element_type=F32)
            + jnp.dot(ah, bl, preferred_element_type=F32))


def _filtgen_body(zt_ref, w1_ref, b1_ref, f1_ref, w2_ref, b2_ref, f2_ref, w3f_ref, w3b_ref, ad_ref, tt_ref,
                  o_ref, hid_ref):
    L = hid_ref.shape[1] // 2

    @pl.when(pl.program_id(0) == 0)
    def _():
        h1 = jnp.sin(f1_ref[...] * (_dot3(w1_ref[...], zt_ref[...]) + b1_ref[...]))
        hid_ref[...] = jnp.sin(f2_ref[...] * (_dot3(w2_ref[...], h1) + b2_ref[...]))

    ad = ad_ref[...]
    hf = _dot3(w3f_ref[...], hid_ref[:, :L]) * jnp.exp(-ad * tt_ref[:, :L])
    hb = _dot3(w3b_ref[...], hid_ref[:, L:]) * jnp.exp(-ad * tt_ref[:, L:])
    hf = hf / (jnp.sum(jnp.abs(hf), axis=-1, keepdims=True) + EPS)
    hb = hb / (jnp.sum(jnp.abs(hb), axis=-1, keepdims=True) + EPS)
    first = lax.broadcasted_iota(jnp.int32, hb.shape, 1) == 0
    cf = hf + jnp.where(first, hb, 0.0)
    cb = jnp.where(first, 0.0, hb)
    chunks = [c[:, a * LANES:(a + 1) * LANES] for c in (cf, cb) for a in range(L // LANES)]
    o_ref[...] = pltpu.einshape("arl->ral", jnp.stack(chunks, axis=0))


def _filter_gen(L, w_f1, b_f1, freq1, w_f2, b_f2, freq2, w_f3):
    bands = (FILTER_EMB - 1) // 2
    t = jnp.linspace(0.0, 1.0, L, dtype=F32)[:, None]
    w = (2.0 * math.pi / L) * jnp.arange(L, dtype=F32)[:, None]
    f = jnp.linspace(1e-4, bands - 1, bands, dtype=F32)[None]
    zf = f * w
    z = jnp.concatenate([t, jnp.cos(zf), -jnp.sin(zf)], axis=-1)
    back = lambda a: jnp.roll(a[::-1], 1, axis=0)
    kpad = 48
    zt = jnp.pad(jnp.concatenate([z, back(z)], axis=0).T, ((0, kpad - FILTER_EMB), (0, 0)))
    tt = jnp.concatenate([t, back(t)], axis=0).T
    w1t = jnp.pad(w_f1.T, ((0, 0), (0, kpad - FILTER_EMB)))
    w3 = w_f3.reshape(-1, 2, 2, D_HYENA)
    w3f = jnp.transpose(w3[:, :, 0], (1, 2, 0)).reshape(2 * D_HYENA, -1)
    w3b = jnp.transpose(w3[:, :, 1], (1, 2, 0)).reshape(2 * D_HYENA, -1)
    max_decay = math.log(DECAY_TARGET) / FAST_DECAY_PCT
    min_decay = math.log(DECAY_TARGET) / SLOW_DECAY_PCT
    deltas = jnp.linspace(min_decay, max_decay, D_HYENA, dtype=F32)
    ad = jnp.tile(jnp.abs(deltas), 2)[:, None]
    col = lambda v: v[:, None]
    R = 128
    n_rows = 2 * D_HYENA
    full = lambda a: pl.BlockSpec(a.shape, lambda i: (0,) * a.ndim)
    rows = lambda a: pl.BlockSpec((R, a.shape[1]), lambda i: (i, 0))
    args = (zt, w1t, col(b_f1), col(freq1), w_f2.T, col(b_f2), col(freq2))
    return pl.pallas_call(
        _filtgen_body,
        grid=(n_rows // R,),
        in_specs=[full(a) for a in args] + [rows(w3f), rows(w3b), rows(ad), full(tt)],
        out_specs=pl.BlockSpec((R, 2 * L // LANES, LANES), lambda i: (i, 0, 0)),
        out_shape=jax.ShapeDtypeStruct((n_rows, 2 * L // LANES, LANES), F32),
        scratch_shapes=[pltpu.VMEM((w_f2.shape[1], 2 * L), F32)],
        compiler_params=_cparams(("arbitrary",)),
        name="filter_gen",
    )(*args, w3f, w3b, ad, tt)


def _route_lanes(lg):
    neg = -1e30
    lane = lax.broadcasted_iota(jnp.int32, lg.shape, 1)
    gmask = lane < N_GROUPS
    gl = jnp.where(gmask, lg, neg)
    gm = jnp.max(gl, axis=-1, keepdims=True)
    gsum = jnp.sum(jnp.where(gmask, jnp.exp(gl - gm), 0.0), axis=-1, keepdims=True)
    g_top = 1.0 / gsum
    g_sel = jnp.min(jnp.where(gl == gm, lane, LANES), axis=-1, keepdims=True)
    lo = N_GROUPS + EXPERTS_PER_GROUP * g_sel
    el = jnp.where((lane >= lo) & (lane < lo + EXPERTS_PER_GROUP), lg, neg)
    m1 = jnp.max(el, axis=-1, keepdims=True)
    i1 = jnp.min(jnp.where(el == m1, lane, LANES), axis=-1, keepdims=True)
    el2 = jnp.where(lane == i1, neg, el)
    m2 = jnp.max(el2, axis=-1, keepdims=True)
    i2 = jnp.min(jnp.where(el2 == m2, lane, LANES), axis=-1, keepdims=True)
    d = jnp.exp(m2 - m1)
    p1 = 1.0 / (1.0 + d)
    p2 = d / (1.0 + d)
    e1 = (i1 - N_GROUPS).astype(F32)
    e2 = (i2 - N_GROUPS).astype(F32)
    return jnp.where(lane == 0, e1, jnp.where(lane == 1, e2, jnp.where(lane == 2, g_top * p1,
                     jnp.where(lane == 3, g_top * p2, 0.0))))


def _pack_bf16_halves(a):
    w = a.shape[1] // 2
    bits = pltpu.bitcast(a.astype(BF16).astype(F32), jnp.uint32)
    return (bits[:, :w] >> 16) | (bits[:, w:] & jnp.uint32(0xFFFF0000))


def _unpack_bf16_halves(wd):
    lo = pltpu.bitcast(wd << 16, F32)
    hi = pltpu.bitcast(wd & jnp.uint32(0xFFFF0000), F32)
    return jnp.concatenate([lo, hi], axis=1)


def _store_row_tiles(ref, packed):
    chunks = jnp.stack([packed[:, j * LANES:(j + 1) * LANES] for j in range(ROW_CHUNKS)], axis=0)
    ref[...] = pltpu.einshape("jrl->rjl", chunks)


def _load_row_tiles(ref):
    chunks = pltpu.einshape("rjl->jrl", ref[...])
    return jnp.concatenate([chunks[j] for j in range(ROW_CHUNKS)], axis=1)


def _outproj_body(ya_ref, yh_ref, x_ref, ga_ref, gh_ref, wo_ref, bd_ref, gm_ref, wrh_ref, wrl_ref, brt_ref,
                  x1_ref, h2_ref, rt_ref, rtt_ref):
    ya = ya_ref[...]
    yan = ya * lax.rsqrt(_group_sumsq(ya, bd_ref[...]) * (1.0 / HEAD_DIM) + EPS) * ga_ref[...]
    yh = yh_ref[...]
    tm = yh.shape[1]
    yh3 = yh.reshape(D_HYENA // HYENA_HEAD, HYENA_HEAD, tm)
    ms = jnp.mean(yh3 * yh3, axis=1, keepdims=True)
    yhn = (yh3 * lax.rsqrt(ms + EPS)).reshape(D_HYENA, tm) * gh_ref[...]
    mix = (jnp.dot(yan.astype(BF16), wo_ref[:D_ATTN, :], preferred_element_type=F32)
           + jnp.dot(yhn.T.astype(BF16), wo_ref[D_ATTN:, :], preferred_element_type=F32))
    x1 = x_ref[...] + mix
    x1_ref[...] = x1
    h2 = x1 * lax.rsqrt(jnp.mean(x1 * x1, axis=-1, keepdims=True) + EPS) * gm_ref[...]
    _store_row_tiles(h2_ref, _pack_bf16_halves(h2))
    hi = h2.astype(BF16)
    lo = (h2 - hi.astype(F32)).astype(BF16)
    lg = (jnp.dot(hi, wrh_ref[...], preferred_element_type=F32)
          + jnp.dot(lo, wrh_ref[...], preferred_element_type=F32)
          + jnp.dot(hi, wrl_ref[...], preferred_element_type=F32)) + brt_ref[...]
    route = _route_lanes(lg)
    rt_ref[...] = route
    rtt_ref[...] = route.T[:8]


def _outproj(ya, yht, x, ga, gh, wo, bd, gm, wrh, wrl, brt):
    B, S, D = x.shape
    tm = TM_PROJ
    full = lambda a: pl.BlockSpec(a.shape, lambda b, i: (0,) * a.ndim)
    return pl.pallas_call(
        _outproj_body,
        grid=(B, S // tm),
        in_specs=[
            pl.BlockSpec((None, tm, D_ATTN), lambda b, i: (b, i, 0)),
            pl.BlockSpec((None, D_HYENA, tm), lambda b, i: (b, 0, i)),
            pl.BlockSpec((None, tm, D), lambda b, i: (b, i, 0)),
            full(ga), full(gh), full(wo), full(bd), full(gm), full(wrh), full(wrl), full(brt),
        ],
        out_specs=[
            pl.BlockSpec((None, tm, D), lambda b, i: (b, i, 0)),
            pl.BlockSpec((None, tm, ROW_CHUNKS, LANES), lambda b, i: (b, i, 0, 0)),
            pl.BlockSpec((None, tm, LANES), lambda b, i: (b, i, 0)),
            pl.BlockSpec((None, 8, tm), lambda b, i: (b, 0, i)),
        ],
        out_shape=[
            jax.ShapeDtypeStruct((B, S, D), F32),
            jax.ShapeDtypeStruct((B, S, ROW_CHUNKS, LANES), jnp.uint32),
            jax.ShapeDtypeStruct((B, S, LANES), F32),
            jax.ShapeDtypeStruct((B, 8, S), F32),
        ],
        compiler_params=_cparams(("parallel", "parallel")),
        name="outproj",
    )(ya, yht, x, ga, gh, wo, bd, gm, wrh, wrl, brt)


def _moe_body(be_ref, ra_ref, nlive_ref, h2_hbm, wg_ref, wu_ref, wd_ref, y_hbm,
              wg_s, wu_s, wd_s, xbuf, ybuf, zbuf, sem_in, sem_out, sem_z, *, n_tok, n_rows):
    i = pl.program_id(0)
    nb = nlive_ref[0]
    T = xbuf.shape[1]
    slot = i % 2

    def issue_gathers(blk, sl):
        for r in range(T):
            tok = ra_ref[blk * T + r] & (n_tok - 1)
            pltpu.make_async_copy(h2_hbm.at[tok], xbuf.at[sl, r], sem_in.at[sl]).start(priority=r % 2)

    def issue_scatters(blk, sl, spare):
        for r in range(T):
            dst = jnp.where(spare, n_rows + r, ra_ref[blk * T + r])
            pltpu.make_async_copy(ybuf.at[sl, r], y_hbm.at[dst], sem_out.at[sl]).start(priority=r % 2)

    def block_in_wait(sl):
        pltpu.make_async_copy(h2_hbm.at[pl.ds(0, T)], xbuf.at[sl], sem_in.at[sl]).wait()

    def block_out_wait(sl):
        pltpu.make_async_copy(ybuf.at[sl], y_hbm.at[pl.ds(0, T)], sem_out.at[sl]).wait()

    @pl.when(i == 0)
    def _():
        ybuf[...] = jnp.zeros(ybuf.shape, ybuf.dtype)
        zbuf[...] = jnp.zeros(zbuf.shape, zbuf.dtype)
        issue_gathers(0, 0)

    @pl.when(i >= nb)
    def _():
        fill = pltpu.make_async_copy(zbuf, y_hbm.at[pl.ds(i * T, T)], sem_z.at[0])
        fill.start()
        fill.wait()

    prev = be_ref[jnp.maximum(i - 1, 0)]

    @pl.when((i == 0) | (be_ref[i] != prev))
    def _():
        wg_s[...] = wg_ref[...].astype(BF16)
        wu_s[...] = wu_ref[...].astype(BF16)
        wd_s[...] = wd_ref[...].astype(BF16)

    @pl.when(i < nb)
    def _():
        block_in_wait(slot)
        x = _unpack_bf16_halves(_load_row_tiles(xbuf.at[slot])).astype(BF16)
        issue_gathers(jnp.minimum(i + 1, nb - 1), 1 - slot)
        issue_scatters(jnp.maximum(i - 1, 0), 1 - slot, i == 0)
        a = jnp.dot(x, wg_s[...], preferred_element_type=F32)
        b = jnp.dot(x, wu_s[...], preferred_element_type=F32)
        hmid = (a * jax.nn.sigmoid(a)) * b
        y = _pack_bf16_halves(jnp.dot(hmid.astype(BF16), wd_s[...], preferred_element_type=F32))

        @pl.when(i >= 1)
        def _():
            block_out_wait(slot)

        _store_row_tiles(ybuf.at[slot], y)

        @pl.when(i == nb - 1)
        def _():
            issue_scatters(i, slot, False)
            block_in_wait(1 - slot)
            block_out_wait(1 - slot)
            block_out_wait(slot)


def _moe_experts(block_e, row_a, n_live, h2p, w_gate, w_up, w_down):
    n_tok = h2p.shape[0]
    n_rows = row_a.shape[0]
    row = h2p.shape[1:]
    D = w_gate.shape[1]
    T = TB_MOE
    assert n_tok & (n_tok - 1) == 0
    grid_spec = pltpu.PrefetchScalarGridSpec(
        num_scalar_prefetch=3,
        grid=(row_a.shape[0] // T,),
        in_specs=[
            pl.BlockSpec(memory_space=pl.ANY),
            pl.BlockSpec((None, D, D_EXPERT), lambda i, be, ra, nl: (be[i], 0, 0)),
            pl.BlockSpec((None, D, D_EXPERT), lambda i, be, ra, nl: (be[i], 0, 0)),
            pl.BlockSpec((None, D_EXPERT, D), lambda i, be, ra, nl: (be[i], 0, 0)),
        ],
        out_specs=pl.BlockSpec(memory_space=pl.ANY),
        scratch_shapes=[
            pltpu.VMEM((D, D_EXPERT), BF16), pltpu.VMEM((D, D_EXPERT), BF16), pltpu.VMEM((D_EXPERT, D), BF16),
            pltpu.VMEM((2, T) + row, jnp.uint32), pltpu.VMEM((2, T) + row, jnp.uint32),
            pltpu.VMEM((T,) + row, jnp.uint32),
            pltpu.SemaphoreType.DMA((2,)), pltpu.SemaphoreType.DMA((2,)), pltpu.SemaphoreType.DMA((1,)),
        ],
    )
    return pl.pallas_call(
        functools.partial(_moe_body, n_tok=n_tok, n_rows=n_rows),
        grid_spec=grid_spec,
        out_shape=jax.ShapeDtypeStruct((n_rows + T,) + row, jnp.uint32),
        compiler_params=_cparams(("arbitrary",)),
        name="moe_experts",
    )(block_e, row_a, n_live, h2p, w_gate, w_up, w_down)


def _dispatch(e_flat, N):
    T = TB_MOE
    NK = N * TOP_K
    experts = jnp.arange(N_EXPERTS, dtype=jnp.int32)
    order = jnp.argsort(e_flat).astype(jnp.int32)
    onehot = (e_flat[:, None] == experts[None]).astype(jnp.int32)
    counts = jnp.sum(onehot, axis=0)
    ends = jnp.cumsum(counts)
    starts = ends - counts
    padded = (counts + T - 1) // T * T
    pends = jnp.cumsum(padded)
    pstarts = pends - padded
    n_rows = -(-(NK + N_EXPERTS * (T - 1)) // T) * T
    n_blocks = n_rows // T
    blk_start = jnp.arange(n_blocks, dtype=jnp.int32) * T
    block_e = jnp.clip(jnp.sum((pends[None, :] <= blk_start[:, None]).astype(jnp.int32), axis=1),
                       0, N_EXPERTS - 1)
    oh_b = (block_e[:, None] == experts[None]).astype(jnp.int32)
    base = jnp.sum(oh_b * (starts - pstarts)[None], axis=1) + blk_start
    end_b = jnp.sum(oh_b * ends[None], axis=1)
    lane = jnp.arange(T, dtype=jnp.int32)[None]
    src = base[:, None] + lane
    pad_id = NK + blk_start[:, None] + lane - end_b[:, None]
    row_a = jnp.where(src < end_b[:, None], order[jnp.clip(src, 0, NK - 1)], pad_id)
    n_live = (pends[-1:] // T).astype(jnp.int32)
    return block_e.astype(jnp.int32), row_a.reshape(n_rows).astype(jnp.int32), n_live


def _final_body(x1_ref, y0_ref, y1_ref, rt_ref, p_ref, gp_ref, wg_ref, bg_ref, wp_ref, gf_ref, o_ref):
    w0 = rt_ref[:, 2:3]
    w1 = rt_ref[:, 3:4]
    y0 = _unpack_bf16_halves(_load_row_tiles(y0_ref))
    y1 = _unpack_bf16_halves(_load_row_tiles(y1_ref))
    x2 = x1_ref[...] + (y0 * w0 + y1 * w1)
    hp = x2 * lax.rsqrt(jnp.mean(x2 * x2, axis=-1, keepdims=True) + EPS) * gp_ref[...]
    gate = jax.nn.sigmoid(jnp.dot(hp.astype(BF16), wg_ref[...], preferred_element_type=F32) + bg_ref[...])
    pe = jnp.dot(p_ref[...].astype(BF16), wp_ref[...], preferred_element_type=F32)
    x3 = x2 + pe * gate
    o_ref[...] = x3 * lax.rsqrt(jnp.mean(x3 * x3, axis=-1, keepdims=True) + EPS) * gf_ref[...]


def _final(x1, y, route, p, gp, wg, bg, wp, gf):
    N, D = x1.shape
    tm = TM_PROJ
    row = lambda w: pl.BlockSpec((tm, w), lambda i: (i, 0))
    full = lambda a: pl.BlockSpec(a.shape, lambda i: (0,) * a.ndim)
    y0, y1 = y, y
    return pl.pallas_call(
        _final_body,
        grid=(N // tm,),
        in_specs=[row(D), pl.BlockSpec((tm, ROW_CHUNKS, LANES), lambda i: (i, 0, 0)),
                  pl.BlockSpec((tm, ROW_CHUNKS, LANES), lambda i: (i + N // tm, 0, 0)),
                  row(LANES), row(p.shape[1]),
                  full(gp), full(wg), full(bg), full(wp), full(gf)],
        out_specs=row(D),
        out_shape=jax.ShapeDtypeStruct((N, D), F32),
        compiler_params=_cparams(("parallel",)),
        name="ple_final",
    )(x1, y0, y1, route, p, gp, wg, bg, wp, gf)


def kernel(x, p, g_mix, w_in, q_gain, k_gain, conv_w, conv_b, w_f1, b_f1, freq1, w_f2, b_f2, freq2, w_f3, filt_bias, g_attn_out, g_hyena_out, w_out, g_moe, w_group, b_group, w_router, b_router, w_gate, w_up, w_down, g_ple, w_ple_gate, b_ple_gate, w_ple, g_final):
    B, S, D = x.shape
    N = B * S
    assert p.shape[0] == 1 and S == (FFT_N1 // 2) * FFT_N2 and B % 2 == 0
    i = 0
    cst = _dft_constants()
    cos, sin = _rope_tables(S)
    bd = _block_diag_ones(D_ATTN, HEAD_DIM)

    n_qkv = D_ATTN + 2 * D_KV
    wqkv = w_in[i][:, :n_qkv].astype(BF16)
    wut = w_in[i][:, n_qkv:].T.astype(BF16)
    q, kw, vw, ut = _inproj(x, g_mix[i][None], wqkv, wut, bd,
                            jnp.tile(q_gain[i], N_HEADS)[None], jnp.tile(k_gain[i], N_KV_HEADS)[None], cos, sin)

    ya = _attention(q, kw, vw)

    circ = _filter_gen(S, w_f1[i], b_f1[i], freq1[i], w_f2[i], b_f2[i], freq2[i], w_f3[i])
    hspec = _filter_fft(circ, cst)
    hspec = hspec.reshape(2, D_HYENA, FFT_N1, 2 * LANES)
    du = ut.shape[1]
    u4 = ut.reshape(B, du, S // LANES, LANES)
    par_u = jnp.broadcast_to(jnp.concatenate([conv_w[i], conv_b[i][None]], 0)[:, :, None], (4, du, LANES))
    fb = jnp.broadcast_to(filt_bias[i][:, :, None], (2, D_HYENA, LANES))
    yht = _hyena(u4, par_u, fb, hspec, cst)

    wrt = jnp.zeros((D, LANES), F32).at[:, :N_GROUPS].set(w_group[i]).at[:, N_GROUPS:N_GROUPS + N_EXPERTS].set(w_router[i])
    brt = jnp.zeros((1, LANES), F32).at[0, :N_GROUPS].set(b_group[i]).at[0, N_GROUPS:N_GROUPS + N_EXPERTS].set(b_router[i])
    wrh = wrt.astype(BF16)
    wrl = (wrt - wrh.astype(F32)).astype(BF16)
    x1, h2, route, route_t = _outproj(ya, yht, x, g_attn_out[i][None], g_hyena_out[i][:, None],
                                      w_out[i].astype(BF16), bd, g_moe[i][None], wrh, wrl, brt)

    e_flat = jnp.transpose(route_t[:, :TOP_K], (1, 0, 2)).reshape(TOP_K * N).astype(jnp.int32)
    block_e, row_a, n_live = _dispatch(e_flat, N)
    y = _moe_experts(block_e, row_a, n_live, h2.reshape(N, ROW_CHUNKS, LANES), w_gate[i], w_up[i], w_down[i])

    out = _final(x1.reshape(N, D), y, route.reshape(N, LANES), p[i].reshape(N, -1), g_ple[i][None],
                 w_ple_gate[i].astype(BF16), b_ple_gate[i][None], w_ple[i].astype(BF16), g_final[None])
    return out.reshape(B, S, D)
```

```python
import functools
import math

import numpy as np
import jax
import jax.numpy as jnp
from jax import lax
from jax.experimental import pallas as pl
from jax.experimental.pallas import tpu as pltpu

F32 = jnp.float32
BF16 = jnp.bfloat16

D_MODEL = 1024
EPS = 1e-6
GRID_W = 64
N_HEADS = 8
N_KV_HEADS = 2
HEAD_DIM = 64
D_ATTN = N_HEADS * HEAD_DIM
D_KV = N_KV_HEADS * HEAD_DIM
ROPE_THETA = 10000.0
D_HYENA = 512
HYENA_HEAD = 64
FILTER_EMB = 33
FAST_DECAY_PCT = 0.3
SLOW_DECAY_PCT = 1.5
DECAY_TARGET = 1e-2
N_GROUPS = 4
EXPERTS_PER_GROUP = 8
N_EXPERTS = N_GROUPS * EXPERTS_PER_GROUP
TOP_K = 2
D_EXPERT = 512

LANES = 128
MXU_TILE = 256
FFT_N1 = 64
FFT_N2 = 128
VMEM_LIMIT = 56 * 1024 * 1024

TM_PROJ = 512
TQ_ATTN = 256
C_HY = 32
C_FILT = 128
ROW_CHUNKS = D_MODEL // 2 // LANES
SEQ_UNROLL = 32
TB_MOE = 256


def _cparams(sem):
    return pltpu.CompilerParams(dimension_semantics=sem, vmem_limit_bytes=VMEM_LIMIT)


def _rope_tables(S):
    half = HEAD_DIM // 2
    t = jnp.arange(S, dtype=F32)
    r_idx = jnp.floor(t / GRID_W)
    c_idx = t - r_idx * GRID_W
    inv = ROPE_THETA ** (-jnp.arange(0, half, 2, dtype=F32) / half)
    ang_r = r_idx[:, None] * inv[None]
    ang_c = c_idx[:, None] * inv[None]
    cos_h = jnp.concatenate([jnp.cos(ang_r), jnp.cos(ang_r), jnp.cos(ang_c), jnp.cos(ang_c)], axis=-1)
    sin_h = jnp.concatenate([-jnp.sin(ang_r), jnp.sin(ang_r), -jnp.sin(ang_c), jnp.sin(ang_c)], axis=-1)
    return jnp.tile(cos_h, (1, 2)), jnp.tile(sin_h, (1, 2))


def _dft_constants():
    n1, n2 = FFT_N1, FFT_N2
    n = n1 * n2
    a = np.arange(n1)
    ang = 2.0 * np.pi * np.outer(a, a) / n1
    far, fai = np.cos(ang), -np.sin(ang)
    hlf = n1 // 2
    ma = np.block([[far[:, :hlf], -fai[:, :hlf]], [fai[:, :hlf], far[:, :hlf]]])
    maf = np.concatenate([far, fai], axis=0)
    b = np.arange(n2)
    angt = 2.0 * np.pi * np.outer(a, b) / n
    tw = np.concatenate([np.cos(angt), -np.sin(angt)], axis=1)
    angb = 2.0 * np.pi * np.outer(b, b) / n2
    fbr, fbi = np.cos(angb), -np.sin(angb)
    g = np.block([[fbr, fbi], [-fbi, fbr]])
    ginv = np.block([[fbr, -fbi], [fbi, fbr]])
    minv_r = np.concatenate([far[:hlf], -fai[:hlf]], axis=0) / n
    minv_i = np.concatenate([fai[:hlf], far[:hlf]], axis=0) / n
    f = lambda m: jnp.asarray(m.astype(np.float32))
    return dict(ma=f(ma), maf=f(maf), tw=f(tw), g=f(g), ginv=f(ginv), minv_r=f(minv_r), minv_i=f(minv_i))


def _block_diag_ones(width, group):
    i = np.arange(width) // group
    return jnp.asarray((i[:, None] == i[None, :]).astype(np.float32)).astype(BF16)


def _group_sumsq(a, bd):
    sq = a * a
    hi = sq.astype(BF16)
    lo = (sq - hi.astype(F32)).astype(BF16)
    w = min(a.shape[-1], MXU_TILE)
    return jnp.concatenate(
        [jnp.dot(hi[:, c:c + w], bd[c:c + w, c:c + w], preferred_element_type=F32)
         + jnp.dot(lo[:, c:c + w], bd[c:c + w, c:c + w], preferred_element_type=F32)
         for c in range(0, a.shape[-1], w)], axis=-1)


def _head_norm_rope(a, gain, bd, cos, sin):
    width = a.shape[-1]
    n = a * lax.rsqrt(_group_sumsq(a, bd) * (1.0 / HEAD_DIM) + EPS) * gain
    rep = width // LANES
    if rep > 1:
        cos = jnp.concatenate([cos] * rep, axis=-1)
        sin = jnp.concatenate([sin] * rep, axis=-1)
    fwd = pltpu.roll(n, width - 16, 1)
    bwd = pltpu.roll(n, 16, 1)
    lane = lax.broadcasted_iota(jnp.int32, n.shape, 1)
    sw = jnp.where((lane % 32) < 16, fwd, bwd)
    return n * cos + sw * sin


def _inproj_body(x_ref, g_ref, wqkv_ref, wu_ref, bd_ref, qg_ref, kg_ref, cos_ref, sin_ref,
                 q_ref, kw_ref, vw_ref, ut_ref):
    x = x_ref[...]
    h = x * lax.rsqrt(jnp.mean(x * x, axis=-1, keepdims=True) + EPS) * g_ref[...]
    hb = h.astype(BF16)
    qkv = jnp.dot(hb, wqkv_ref[...], preferred_element_type=F32)
    cos = cos_ref[...]
    sin = sin_ref[...]
    bd = bd_ref[...]
    q = _head_norm_rope(qkv[:, :D_ATTN], qg_ref[...], bd, cos, sin)
    q_ref[...] = (q * (HEAD_DIM ** -0.5 * math.log2(math.e))).astype(BF16)
    k = _head_norm_rope(qkv[:, D_ATTN:D_ATTN + D_KV], kg_ref[...], bd[:D_KV, :D_KV], cos, sin)
    kt = k.T.astype(BF16)
    zero = jnp.zeros((HEAD_DIM, kt.shape[1]), BF16)
    for h in range(N_KV_HEADS):
        kh = kt[h * HEAD_DIM:(h + 1) * HEAD_DIM]
        kw_ref[h, 0, :HEAD_DIM] = kh
        kw_ref[h, 0, HEAD_DIM:] = zero
        kw_ref[h, 1, :HEAD_DIM] = zero
        kw_ref[h, 1, HEAD_DIM:] = kh
    v = qkv[:, D_ATTN + D_KV:]
    vr = pltpu.roll(v, HEAD_DIM, 1)
    first = lax.broadcasted_iota(jnp.int32, v.shape, 1) < HEAD_DIM
    vw_ref[0, 0] = jnp.where(first, v, 1.0).astype(BF16)
    vw_ref[0, 1] = jnp.where(first, 1.0, vr).astype(BF16)
    vw_ref[1, 0] = jnp.where(first, vr, 1.0).astype(BF16)
    vw_ref[1, 1] = jnp.where(first, 1.0, v).astype(BF16)
    ut_ref[...] = lax.dot_general(wu_ref[...], hb, (((1,), (1,)), ((), ())),
                                  preferred_element_type=F32)


def _inproj(x, g_mix, wqkv, wut, bd, qg, kg, cos, sin):
    B, S, D = x.shape
    tm = TM_PROJ
    du = wut.shape[0]
    full = lambda shape: pl.BlockSpec(shape, lambda b, i: (0,) * len(shape))
    return pl.pallas_call(
        _inproj_body,
        grid=(B, S // tm),
        in_specs=[
            pl.BlockSpec((None, tm, D), lambda b, i: (b, i, 0)),
            full((1, D)), full(wqkv.shape), full(wut.shape), full(bd.shape),
            full((1, D_ATTN)), full((1, D_KV)),
            pl.BlockSpec((tm, LANES), lambda b, i: (i, 0)),
            pl.BlockSpec((tm, LANES), lambda b, i: (i, 0)),
        ],
        out_specs=[
            pl.BlockSpec((None, tm, D_ATTN), lambda b, i: (b, i, 0)),
            pl.BlockSpec((None, N_KV_HEADS, 2, LANES, tm), lambda b, i: (b, 0, 0, 0, i)),
            pl.BlockSpec((None, N_KV_HEADS, 2, tm, LANES), lambda b, i: (b, 0, 0, i, 0)),
            pl.BlockSpec((None, du, tm), lambda b, i: (b, 0, i)),
        ],
        out_shape=[
            jax.ShapeDtypeStruct((B, S, D_ATTN), BF16),
            jax.ShapeDtypeStruct((B, N_KV_HEADS, 2, LANES, S), BF16),
            jax.ShapeDtypeStruct((B, N_KV_HEADS, 2, S, LANES), BF16),
            jax.ShapeDtypeStruct((B, du, S), F32),
        ],
        compiler_params=_cparams(("parallel", "parallel")),
        name="inproj",
    )(x, g_mix, wqkv, wut, bd, qg, kg, cos, sin)


def _attn_body(q_ref, kw_ref, vw_ref, o_ref):

    def one_head(q, kw, vw):
        s = jnp.dot(q, kw, preferred_element_type=F32)
        m = jnp.max(s, axis=-1, keepdims=True)
        p = jnp.exp2(s - m).astype(BF16)
        return jnp.dot(p, vw, preferred_element_type=F32)

    for pair in range(D_ATTN // LANES):
        h = pair // (N_HEADS // N_KV_HEADS // 2)
        q = q_ref[:, pair * LANES:(pair + 1) * LANES]
        oe = one_head(q, kw_ref[h, 0], vw_ref[h, 0])
        oo = one_head(q, kw_ref[h, 1], vw_ref[h, 1])
        first = lax.broadcasted_iota(jnp.int32, oe.shape, 1) < HEAD_DIM
        num = jnp.where(first, oe, oo)
        den = jnp.where(first, pltpu.roll(oe, HEAD_DIM, 1), pltpu.roll(oo, HEAD_DIM, 1))
        o_ref[:, pair * LANES:(pair + 1) * LANES] = num / den


def _attention(q, kw, vw):
    B, S, _ = q.shape
    tq = TQ_ATTN
    return pl.pallas_call(
        _attn_body,
        grid=(B, S // tq),
        in_specs=[
            pl.BlockSpec((None, tq, D_ATTN), lambda b, i: (b, i, 0)),
            pl.BlockSpec((None, N_KV_HEADS, 2, LANES, S), lambda b, i: (b, 0, 0, 0, 0)),
            pl.BlockSpec((None, N_KV_HEADS, 2, S, LANES), lambda b, i: (b, 0, 0, 0, 0)),
        ],
        out_specs=pl.BlockSpec((None, tq, D_ATTN), lambda b, i: (b, i, 0)),
        out_shape=jax.ShapeDtypeStruct((B, S, D_ATTN), F32),
        compiler_params=_cparams(("parallel", "arbitrary")),
        name="attention",
    )(q, kw, vw)


def _fwd_twiddle_store(y, tw_ref, s1_ref, row0):
    yr, yi = y[:FFT_N1], y[FFT_N1:]
    twr, twi = tw_ref[:, :LANES], tw_ref[:, LANES:]
    s1_ref[pl.ds(row0, FFT_N1), :LANES] = (yr * twr - yi * twi).astype(BF16)
    s1_ref[pl.ds(row0, FFT_N1), LANES:] = (yr * twi + yi * twr).astype(BF16)


def _filtfft_body(x_ref, maf_ref, tw_ref, g_ref, h_ref, s1_ref):
    C = x_ref.shape[0]

    def step_a(c, carry):
        y = jnp.dot(maf_ref[...], x_ref[c].astype(BF16), preferred_element_type=F32)
        _fwd_twiddle_store(y, tw_ref, s1_ref, pl.multiple_of(c * FFT_N1, FFT_N1))
        return carry

    lax.fori_loop(0, C, step_a, 0, unroll=SEQ_UNROLL)
    z = jnp.dot(s1_ref[...], g_ref[...], preferred_element_type=F32)
    h_ref[...] = z.reshape(C, FFT_N1, 2 * LANES)


def _filter_fft(circ, cst):
    n_seq = circ.shape[0]
    C = C_FILT
    full = lambda a: pl.BlockSpec(a.shape, lambda i: (0,) * a.ndim)
    maf, tw, g = cst["maf"].astype(BF16), cst["tw"], cst["g"].astype(BF16)
    return pl.pallas_call(
        _filtfft_body,
        grid=(n_seq // C,),
        in_specs=[pl.BlockSpec((C, FFT_N1, FFT_N2), lambda i: (i, 0, 0)), full(maf), full(tw), full(g)],
        out_specs=pl.BlockSpec((C, FFT_N1, 2 * LANES), lambda i: (i, 0, 0)),
        out_shape=jax.ShapeDtypeStruct((n_seq, FFT_N1, 2 * LANES), F32),
        scratch_shapes=[pltpu.VMEM((C * FFT_N1, 2 * LANES), BF16)],
        compiler_params=_cparams(("parallel",)),
        name="filter_fft",
    )(circ, maf, tw, g)


def _short_conv(x, par_ref, c):
    rows, lanes = x.shape
    a_i = lax.broadcasted_iota(jnp.int32, x.shape, 0)
    b_i = lax.broadcasted_iota(jnp.int32, x.shape, 1)
    l1 = pltpu.roll(x, 1, 1)
    l2 = pltpu.roll(l1, 1, 0)
    prev = jnp.where(b_i == 0, l2, l1)
    prev = jnp.where((a_i == 0) & (b_i == 0), 0.0, prev)
    r1 = pltpu.roll(x, lanes - 1, 1)
    r2 = pltpu.roll(r1, rows - 1, 0)
    nxt = jnp.where(b_i == lanes - 1, r2, r1)
    nxt = jnp.where((a_i == rows - 1) & (b_i == lanes - 1), 0.0, nxt)
    w0 = par_ref[0, pl.ds(c, 1), :]
    w1 = par_ref[1, pl.ds(c, 1), :]
    w2 = par_ref[2, pl.ds(c, 1), :]
    cb = par_ref[3, pl.ds(c, 1), :]
    return cb + prev * w0 + x * w1 + nxt * w2


def _hyena_body(v_ref, x1_ref, x2_ref, pv_ref, p1_ref, p2_ref, fb_ref, h_ref,
                ma_ref, tw_ref, g_ref, ginv_ref, mir_ref, mii_ref,
                o_ref, s1_ref, s2_ref, vc_ref, z1_ref):
    C = v_ref.shape[1]
    half = FFT_N1 // 2

    def spectral(order):
        z = jnp.dot(s1_ref[...], g_ref[...], preferred_element_type=F32)
        hs = h_ref[order].reshape(C * FFT_N1, 2 * LANES)
        zr, zi = z[:, :LANES], z[:, LANES:]
        hr, hi = hs[:, :LANES], hs[:, LANES:]
        pb = jnp.concatenate([zr * hr - zi * hi, zr * hi + zi * hr], axis=1).astype(BF16)
        s2_ref[...] = jnp.dot(pb, ginv_ref[...], preferred_element_type=F32)

    def inv_a(c):
        row0 = pl.multiple_of(c * FFT_N1, FFT_N1)
        y = s2_ref[pl.ds(row0, FFT_N1), :]
        yr, yi = y[:, :LANES], y[:, LANES:]
        twr, twi = tw_ref[:, :LANES], tw_ref[:, LANES:]
        ur = (yr * twr + yi * twi).astype(BF16)
        ui = (yi * twr - yr * twi).astype(BF16)
        out = (jnp.dot(mir_ref[...], ur, preferred_element_type=F32)
               + jnp.dot(mii_ref[...], ui, preferred_element_type=F32))
        return out[:half], out[half:]

    def fwd_a(c, xr, xi):
        xs = jnp.concatenate([xr, xi], axis=0).astype(BF16)
        y = jnp.dot(ma_ref[...], xs, preferred_element_type=F32)
        _fwd_twiddle_store(y, tw_ref, s1_ref, pl.multiple_of(c * FFT_N1, FFT_N1))

    def pass1_a(c, carry):
        vr = _short_conv(v_ref[0, c], pv_ref, c)
        vi = _short_conv(v_ref[1, c], pv_ref, c)
        vc_ref[0, c] = vr
        vc_ref[1, c] = vi
        fwd_a(c, vr, vi)
        return carry

    def pass1_b(c, carry):
        cr, ci = inv_a(c)
        bias = fb_ref[0, pl.ds(c, 1), :]
        zr = _short_conv(x1_ref[0, c], p1_ref, c) * (cr + bias * vc_ref[0, c])
        zi = _short_conv(x1_ref[1, c], p1_ref, c) * (ci + bias * vc_ref[1, c])
        z1_ref[0, c] = zr
        z1_ref[1, c] = zi
        fwd_a(c, zr, zi)
        return carry

    def pass2_b(c, carry):
        cr, ci = inv_a(c)
        bias = fb_ref[1, pl.ds(c, 1), :]
        vc_ref[0, c] = _short_conv(x2_ref[0, c], p2_ref, c) * (cr + bias * z1_ref[0, c])
        vc_ref[1, c] = _short_conv(x2_ref[1, c], p2_ref, c) * (ci + bias * z1_ref[1, c])
        return carry

    lax.fori_loop(0, C, pass1_a, 0, unroll=SEQ_UNROLL)
    spectral(0)
    lax.fori_loop(0, C, pass1_b, 0, unroll=SEQ_UNROLL)
    spectral(1)
    lax.fori_loop(0, C, pass2_b, 0, unroll=SEQ_UNROLL)
    for b2 in range(2):
        tiles = pltpu.einshape("cab->acb", vc_ref[b2])
        for a in range(tiles.shape[0]):
            o_ref[b2, :, a * LANES:(a + 1) * LANES] = tiles[a]


def _hyena(u4, par_u, fb, hspec, cst):
    B = u4.shape[0]
    C = C_HY
    J = D_HYENA // C
    rows = u4.shape[2]
    full = lambda a: pl.BlockSpec(a.shape, lambda j, p: (0,) * a.ndim)
    ma, g, ginv = cst["ma"].astype(BF16), cst["g"].astype(BF16), cst["ginv"].astype(BF16)
    mir, mii = cst["minv_r"].astype(BF16), cst["minv_i"].astype(BF16)
    tw = cst["tw"]
    u_spec = lambda k: pl.BlockSpec((2, C, rows, LANES), lambda j, p, k=k: (p, j + k * J, 0, 0))
    par_spec = lambda k: pl.BlockSpec((4, C, LANES), lambda j, p, k=k: (0, j + k * J, 0))
    return pl.pallas_call(
        _hyena_body,
        grid=(J, B // 2),
        in_specs=[
            u_spec(0), u_spec(1), u_spec(2), par_spec(0), par_spec(1), par_spec(2),
            pl.BlockSpec((2, C, LANES), lambda j, p: (0, j, 0)),
            pl.BlockSpec((2, C, FFT_N1, 2 * LANES), lambda j, p: (0, j, 0, 0)),
            full(ma), full(tw), full(g), full(ginv), full(mir), full(mii),
        ],
        out_specs=pl.BlockSpec((2, C, rows * LANES), lambda j, p: (p, j, 0)),
        out_shape=jax.ShapeDtypeStruct((B, D_HYENA, rows * LANES), F32),
        scratch_shapes=[
            pltpu.VMEM((C * FFT_N1, 2 * LANES), BF16),
            pltpu.VMEM((C * FFT_N1, 2 * LANES), F32),
            pltpu.VMEM((2, C, rows, LANES), F32),
            pltpu.VMEM((2, C, rows, LANES), F32),
        ],
        compiler_params=_cparams(("parallel", "arbitrary")),
        name="hyena",
    )(u4, u4, u4, par_u, par_u, par_u, fb, hspec, ma, tw, g, ginv, mir, mii)


def _dot3(a, b):
    ah = a.astype(BF16)
    al = (a - ah.astype(F32)).astype(BF16)
    bh = b.astype(BF16)
    bl = (b - bh.astype(F32)).astype(BF16)
    return (jnp.dot(ah, bh, preferred_element_type=F32) + jnp.dot(al, bh, preferred_element_type=F32)
            + jnp.dot(ah, bl, preferred_element_type=F32))


def _filtgen_body(zt_ref, w1_ref, b1_ref, f1_ref, w2_ref, b2_ref, f2_ref, w3f_ref, w3b_ref, ad_ref, tt_ref,
                  o_ref, hid_ref):
    L = hid_ref.shape[1] // 2

    @pl.when(pl.program_id(0) == 0)
    def _():
        h1 = jnp.sin(f1_ref[...] * (_dot3(w1_ref[...], zt_ref[...]) + b1_ref[...]))
        hid_ref[...] = jnp.sin(f2_ref[...] * (_dot3(w2_ref[...], h1) + b2_ref[...]))

    ad = ad_ref[...]
    hf = _dot3(w3f_ref[...], hid_ref[:, :L]) * jnp.exp(-ad * tt_ref[:, :L])
    hb = _dot3(w3b_ref[...], hid_ref[:, L:]) * jnp.exp(-ad * tt_ref[:, L:])
    hf = hf / (jnp.sum(jnp.abs(hf), axis=-1, keepdims=True) + EPS)
    hb = hb / (jnp.sum(jnp.abs(hb), axis=-1, keepdims=True) + EPS)
    first = lax.broadcasted_iota(jnp.int32, hb.shape, 1) == 0
    cf = hf + jnp.where(first, hb, 0.0)
    cb = jnp.where(first, 0.0, hb)
    chunks = [c[:, a * LANES:(a + 1) * LANES] for c in (cf, cb) for a in range(L // LANES)]
    o_ref[...] = pltpu.einshape("arl->ral", jnp.stack(chunks, axis=0))


def _filter_gen(L, w_f1, b_f1, freq1, w_f2, b_f2, freq2, w_f3):
    bands = (FILTER_EMB - 1) // 2
    t = jnp.linspace(0.0, 1.0, L, dtype=F32)[:, None]
    w = (2.0 * math.pi / L) * jnp.arange(L, dtype=F32)[:, None]
    f = jnp.linspace(1e-4, bands - 1, bands, dtype=F32)[None]
    zf = f * w
    z = jnp.concatenate([t, jnp.cos(zf), -jnp.sin(zf)], axis=-1)
    back = lambda a: jnp.roll(a[::-1], 1, axis=0)
    kpad = 48
    zt = jnp.pad(jnp.concatenate([z, back(z)], axis=0).T, ((0, kpad - FILTER_EMB), (0, 0)))
    tt = jnp.concatenate([t, back(t)], axis=0).T
    w1t = jnp.pad(w_f1.T, ((0, 0), (0, kpad - FILTER_EMB)))
    w3 = w_f3.reshape(-1, 2, 2, D_HYENA)
    w3f = jnp.transpose(w3[:, :, 0], (1, 2, 0)).reshape(2 * D_HYENA, -1)
    w3b = jnp.transpose(w3[:, :, 1], (1, 2, 0)).reshape(2 * D_HYENA, -1)
    max_decay = math.log(DECAY_TARGET) / FAST_DECAY_PCT
    min_decay = math.log(DECAY_TARGET) / SLOW_DECAY_PCT
    deltas = jnp.linspace(min_decay, max_decay, D_HYENA, dtype=F32)
    ad = jnp.tile(jnp.abs(deltas), 2)[:, None]
    col = lambda v: v[:, None]
    R = 128
    n_rows = 2 * D_HYENA
    full = lambda a: pl.BlockSpec(a.shape, lambda i: (0,) * a.ndim)
    rows = lambda a: pl.BlockSpec((R, a.shape[1]), lambda i: (i, 0))
    args = (zt, w1t, col(b_f1), col(freq1), w_f2.T, col(b_f2), col(freq2))
    return pl.pallas_call(
        _filtgen_body,
        grid=(n_rows // R,),
        in_specs=[full(a) for a in args] + [rows(w3f), rows(w3b), rows(ad), full(tt)],
        out_specs=pl.BlockSpec((R, 2 * L // LANES, LANES), lambda i: (i, 0, 0)),
        out_shape=jax.ShapeDtypeStruct((n_rows, 2 * L // LANES, LANES), F32),
        scratch_shapes=[pltpu.VMEM((w_f2.shape[1], 2 * L), F32)],
        compiler_params=_cparams(("arbitrary",)),
        name="filter_gen",
    )(*args, w3f, w3b, ad, tt)


def _route_lanes(lg):
    neg = -1e30
    lane = lax.broadcasted_iota(jnp.int32, lg.shape, 1)
    gmask = lane < N_GROUPS
    gl = jnp.where(gmask, lg, neg)
    gm = jnp.max(gl, axis=-1, keepdims=True)
    gsum = jnp.sum(jnp.where(gmask, jnp.exp(gl - gm), 0.0), axis=-1, keepdims=True)
    g_top = 1.0 / gsum
    g_sel = jnp.min(jnp.where(gl == gm, lane, LANES), axis=-1, keepdims=True)
    lo = N_GROUPS + EXPERTS_PER_GROUP * g_sel
    el = jnp.where((lane >= lo) & (lane < lo + EXPERTS_PER_GROUP), lg, neg)
    m1 = jnp.max(el, axis=-1, keepdims=True)
    i1 = jnp.min(jnp.where(el == m1, lane, LANES), axis=-1, keepdims=True)
    el2 = jnp.where(lane == i1, neg, el)
    m2 = jnp.max(el2, axis=-1, keepdims=True)
    i2 = jnp.min(jnp.where(el2 == m2, lane, LANES), axis=-1, keepdims=True)
    d = jnp.exp(m2 - m1)
    p1 = 1.0 / (1.0 + d)
    p2 = d / (1.0 + d)
    e1 = (i1 - N_GROUPS).astype(F32)
    e2 = (i2 - N_GROUPS).astype(F32)
    return jnp.where(lane == 0, e1, jnp.where(lane == 1, e2, jnp.where(lane == 2, g_top * p1,
                     jnp.where(lane == 3, g_top * p2, 0.0))))


def _pack_bf16_halves(a):
    w = a.shape[1] // 2
    bits = pltpu.bitcast(a.astype(BF16).astype(F32), jnp.uint32)
    return (bits[:, :w] >> 16) | (bits[:, w:] & jnp.uint32(0xFFFF0000))


def _unpack_bf16_halves(wd):
    lo = pltpu.bitcast(wd << 16, F32)
    hi = pltpu.bitcast(wd & jnp.uint32(0xFFFF0000), F32)
    return jnp.concatenate([lo, hi], axis=1)


def _store_row_tiles(ref, packed):
    chunks = jnp.stack([packed[:, j * LANES:(j + 1) * LANES] for j in range(ROW_CHUNKS)], axis=0)
    ref[...] = pltpu.einshape("jrl->rjl", chunks)


def _load_row_tiles(ref):
    chunks = pltpu.einshape("rjl->jrl", ref[...])
    return jnp.concatenate([chunks[j] for j in range(ROW_CHUNKS)], axis=1)


def _outproj_body(ya_ref, yh_ref, x_ref, ga_ref, gh_ref, wo_ref, bd_ref, gm_ref, wrh_ref, wrl_ref, brt_ref,
                  x1_ref, h2_ref, rt_ref, rtt_ref):
    ya = ya_ref[...]
    yan = ya * lax.rsqrt(_group_sumsq(ya, bd_ref[...]) * (1.0 / HEAD_DIM) + EPS) * ga_ref[...]
    yh = yh_ref[...]
    tm = yh.shape[1]
    yh3 = yh.reshape(D_HYENA // HYENA_HEAD, HYENA_HEAD, tm)
    ms = jnp.mean(yh3 * yh3, axis=1, keepdims=True)
    yhn = (yh3 * lax.rsqrt(ms + EPS)).reshape(D_HYENA, tm) * gh_ref[...]
    mix = (jnp.dot(yan.astype(BF16), wo_ref[:D_ATTN, :], preferred_element_type=F32)
           + jnp.dot(yhn.T.astype(BF16), wo_ref[D_ATTN:, :], preferred_element_type=F32))
    x1 = x_ref[...] + mix
    x1_ref[...] = x1
    h2 = x1 * lax.rsqrt(jnp.mean(x1 * x1, axis=-1, keepdims=True) + EPS) * gm_ref[...]
    _store_row_tiles(h2_ref, _pack_bf16_halves(h2))
    hi = h2.astype(BF16)
    lo = (h2 - hi.astype(F32)).astype(BF16)
    lg = (jnp.dot(hi, wrh_ref[...], preferred_element_type=F32)
          + jnp.dot(lo, wrh_ref[...], preferred_element_type=F32)
          + jnp.dot(hi, wrl_ref[...], preferred_element_type=F32)) + brt_ref[...]
    route = _route_lanes(lg)
    rt_ref[...] = route
    rtt_ref[...] = route.T[:8]


def _outproj(ya, yht, x, ga, gh, wo, bd, gm, wrh, wrl, brt):
    B, S, D = x.shape
    tm = TM_PROJ
    full = lambda a: pl.BlockSpec(a.shape, lambda b, i: (0,) * a.ndim)
    return pl.pallas_call(
        _outproj_body,
        grid=(B, S // tm),
        in_specs=[
            pl.BlockSpec((None, tm, D_ATTN), lambda b, i: (b, i, 0)),
            pl.BlockSpec((None, D_HYENA, tm), lambda b, i: (b, 0, i)),
            pl.BlockSpec((None, tm, D), lambda b, i: (b, i, 0)),
            full(ga), full(gh), full(wo), full(bd), full(gm), full(wrh), full(wrl), full(brt),
        ],
        out_specs=[
            pl.BlockSpec((None, tm, D), lambda b, i: (b, i, 0)),
            pl.BlockSpec((None, tm, ROW_CHUNKS, LANES), lambda b, i: (b, i, 0, 0)),
            pl.BlockSpec((None, tm, LANES), lambda b, i: (b, i, 0)),
            pl.BlockSpec((None, 8, tm), lambda b, i: (b, 0, i)),
        ],
        out_shape=[
            jax.ShapeDtypeStruct((B, S, D), F32),
            jax.ShapeDtypeStruct((B, S, ROW_CHUNKS, LANES), jnp.uint32),
            jax.ShapeDtypeStruct((B, S, LANES), F32),
            jax.ShapeDtypeStruct((B, 8, S), F32),
        ],
        compiler_params=_cparams(("parallel", "parallel")),
        name="outproj",
    )(ya, yht, x, ga, gh, wo, bd, gm, wrh, wrl, brt)


def _moe_body(be_ref, ra_ref, nlive_ref, h2_hbm, wg_ref, wu_ref, wd_ref, y_hbm,
              wg_s, wu_s, wd_s, xbuf, ybuf, zbuf, sem_in, sem_out, sem_z, *, n_tok, n_rows):
    i = pl.program_id(0)
    nb = nlive_ref[0]
    T = xbuf.shape[1]
    slot = i % 2

    def issue_gathers(blk, sl):
        for r in range(T):
            tok = ra_ref[blk * T + r] & (n_tok - 1)
            pltpu.make_async_copy(h2_hbm.at[tok], xbuf.at[sl, r], sem_in.at[sl]).start()

    def issue_scatters(blk, sl, spare):
        for r in range(T):
            dst = jnp.where(spare, n_rows + r, ra_ref[blk * T + r])
            pltpu.make_async_copy(ybuf.at[sl, r], y_hbm.at[dst], sem_out.at[sl]).start()

    def block_in_wait(sl):
        pltpu.make_async_copy(h2_hbm.at[pl.ds(0, T)], xbuf.at[sl], sem_in.at[sl]).wait()

    def block_out_wait(sl):
        pltpu.make_async_copy(ybuf.at[sl], y_hbm.at[pl.ds(0, T)], sem_out.at[sl]).wait()

    @pl.when(i == 0)
    def _():
        ybuf[...] = jnp.zeros(ybuf.shape, ybuf.dtype)
        zbuf[...] = jnp.zeros(zbuf.shape, zbuf.dtype)
        issue_gathers(0, 0)

    @pl.when(i >= nb)
    def _():
        fill = pltpu.make_async_copy(zbuf, y_hbm.at[pl.ds(i * T, T)], sem_z.at[0])
        fill.start()
        fill.wait()

    prev = be_ref[jnp.maximum(i - 1, 0)]

    @pl.when((i == 0) | (be_ref[i] != prev))
    def _():
        wg_s[...] = wg_ref[...].astype(BF16)
        wu_s[...] = wu_ref[...].astype(BF16)
        wd_s[...] = wd_ref[...].astype(BF16)

    @pl.when(i < nb)
    def _():
        block_in_wait(slot)
        x = _unpack_bf16_halves(_load_row_tiles(xbuf.at[slot])).astype(BF16)
        issue_gathers(jnp.minimum(i + 1, nb - 1), 1 - slot)
        issue_scatters(jnp.maximum(i - 1, 0), 1 - slot, i == 0)
        a = jnp.dot(x, wg_s[...], preferred_element_type=F32)
        b = jnp.dot(x, wu_s[...], preferred_element_type=F32)
        hmid = (a * jax.nn.sigmoid(a)) * b
        y = _pack_bf16_halves(jnp.dot(hmid.astype(BF16), wd_s[...], preferred_element_type=F32))

        @pl.when(i >= 1)
        def _():
            block_out_wait(slot)

        _store_row_tiles(ybuf.at[slot], y)

        @pl.when(i == nb - 1)
        def _():
            issue_scatters(i, slot, False)
            block_in_wait(1 - slot)
            block_out_wait(1 - slot)
            block_out_wait(slot)


def _moe_experts(block_e, row_a, n_live, h2p, w_gate, w_up, w_down):
    n_tok = h2p.shape[0]
    n_rows = row_a.shape[0]
    row = h2p.shape[1:]
    D = w_gate.shape[1]
    T = TB_MOE
    assert n_tok & (n_tok - 1) == 0
    grid_spec = pltpu.PrefetchScalarGridSpec(
        num_scalar_prefetch=3,
        grid=(row_a.shape[0] // T,),
        in_specs=[
            pl.BlockSpec(memory_space=pl.ANY),
            pl.BlockSpec((None, D, D_EXPERT), lambda i, be, ra, nl: (be[i], 0, 0)),
            pl.BlockSpec((None, D, D_EXPERT), lambda i, be, ra, nl: (be[i], 0, 0)),
            pl.BlockSpec((None, D_EXPERT, D), lambda i, be, ra, nl: (be[i], 0, 0)),
        ],
        out_specs=pl.BlockSpec(memory_space=pl.ANY),
        scratch_shapes=[
            pltpu.VMEM((D, D_EXPERT), BF16), pltpu.VMEM((D, D_EXPERT), BF16), pltpu.VMEM((D_EXPERT, D), BF16),
            pltpu.VMEM((2, T) + row, jnp.uint32), pltpu.VMEM((2, T) + row, jnp.uint32),
            pltpu.VMEM((T,) + row, jnp.uint32),
            pltpu.SemaphoreType.DMA((2,)), pltpu.SemaphoreType.DMA((2,)), pltpu.SemaphoreType.DMA((1,)),
        ],
    )
    return pl.pallas_call(
        functools.partial(_moe_body, n_tok=n_tok, n_rows=n_rows),
        grid_spec=grid_spec,
        out_shape=jax.ShapeDtypeStruct((n_rows + T,) + row, jnp.uint32),
        compiler_params=_cparams(("arbitrary",)),
        name="moe_experts",
    )(block_e, row_a, n_live, h2p, w_gate, w_up, w_down)


def _dispatch(e_flat, N):
    T = TB_MOE
    NK = N * TOP_K
    experts = jnp.arange(N_EXPERTS, dtype=jnp.int32)
    order = jnp.argsort(e_flat).astype(jnp.int32)
    onehot = (e_flat[:, None] == experts[None]).astype(jnp.int32)
    counts = jnp.sum(onehot, axis=0)
    ends = jnp.cumsum(counts)
    starts = ends - counts
    padded = (counts + T - 1) // T * T
    pends = jnp.cumsum(padded)
    pstarts = pends - padded
    n_rows = -(-(NK + N_EXPERTS * (T - 1)) // T) * T
    n_blocks = n_rows // T
    blk_start = jnp.arange(n_blocks, dtype=jnp.int32) * T
    block_e = jnp.clip(jnp.sum((pends[None, :] <= blk_start[:, None]).astype(jnp.int32), axis=1),
                       0, N_EXPERTS - 1)
    oh_b = (block_e[:, None] == experts[None]).astype(jnp.int32)
    base = jnp.sum(oh_b * (starts - pstarts)[None], axis=1) + blk_start
    end_b = jnp.sum(oh_b * ends[None], axis=1)
    lane = jnp.arange(T, dtype=jnp.int32)[None]
    src = base[:, None] + lane
    pad_id = NK + blk_start[:, None] + lane - end_b[:, None]
    row_a = jnp.where(src < end_b[:, None], order[jnp.clip(src, 0, NK - 1)], pad_id)
    n_live = (pends[-1:] // T).astype(jnp.int32)
    return block_e.astype(jnp.int32), row_a.reshape(n_rows).astype(jnp.int32), n_live


def _final_body(x1_ref, y0_ref, y1_ref, rt_ref, p_ref, gp_ref, wg_ref, bg_ref, wp_ref, gf_ref, o_ref):
    w0 = rt_ref[:, 2:3]
    w1 = rt_ref[:, 3:4]
    y0 = _unpack_bf16_halves(_load_row_tiles(y0_ref))
    y1 = _unpack_bf16_halves(_load_row_tiles(y1_ref))
    x2 = x1_ref[...] + (y0 * w0 + y1 * w1)
    hp = x2 * lax.rsqrt(jnp.mean(x2 * x2, axis=-1, keepdims=True) + EPS) * gp_ref[...]
    gate = jax.nn.sigmoid(jnp.dot(hp.astype(BF16), wg_ref[...], preferred_element_type=F32) + bg_ref[...])
    pe = jnp.dot(p_ref[...].astype(BF16), wp_ref[...], preferred_element_type=F32)
    x3 = x2 + pe * gate
    o_ref[...] = x3 * lax.rsqrt(jnp.mean(x3 * x3, axis=-1, keepdims=True) + EPS) * gf_ref[...]


def _final(x1, y, route, p, gp, wg, bg, wp, gf):
    N, D = x1.shape
    tm = TM_PROJ
    row = lambda w: pl.BlockSpec((tm, w), lambda i: (i, 0))
    full = lambda a: pl.BlockSpec(a.shape, lambda i: (0,) * a.ndim)
    y0, y1 = y, y
    return pl.pallas_call(
        _final_body,
        grid=(N // tm,),
        in_specs=[row(D), pl.BlockSpec((tm, ROW_CHUNKS, LANES), lambda i: (i, 0, 0)),
                  pl.BlockSpec((tm, ROW_CHUNKS, LANES), lambda i: (i + N // tm, 0, 0)),
                  row(LANES), row(p.shape[1]),
                  full(gp), full(wg), full(bg), full(wp), full(gf)],
        out_specs=row(D),
        out_shape=jax.ShapeDtypeStruct((N, D), F32),
        compiler_params=_cparams(("parallel",)),
        name="ple_final",
    )(x1, y0, y1, route, p, gp, wg, bg, wp, gf)


def kernel(x, p, g_mix, w_in, q_gain, k_gain, conv_w, conv_b, w_f1, b_f1, freq1, w_f2, b_f2, freq2, w_f3, filt_bias, g_attn_out, g_hyena_out, w_out, g_moe, w_group, b_group, w_router, b_router, w_gate, w_up, w_down, g_ple, w_ple_gate, b_ple_gate, w_ple, g_final):
    B, S, D = x.shape
    N = B * S
    assert p.shape[0] == 1 and S == (FFT_N1 // 2) * FFT_N2 and B % 2 == 0
    i = 0
    cst = _dft_constants()
    cos, sin = _rope_tables(S)
    bd = _block_diag_ones(D_ATTN, HEAD_DIM)

    n_qkv = D_ATTN + 2 * D_KV
    wqkv = w_in[i][:, :n_qkv].astype(BF16)
    wut = w_in[i][:, n_qkv:].T.astype(BF16)
    q, kw, vw, ut = _inproj(x, g_mix[i][None], wqkv, wut, bd,
                            jnp.tile(q_gain[i], N_HEADS)[None], jnp.tile(k_gain[i], N_KV_HEADS)[None], cos, sin)

    ya = _attention(q, kw, vw)

    circ = _filter_gen(S, w_f1[i], b_f1[i], freq1[i], w_f2[i], b_f2[i], freq2[i], w_f3[i])
    hspec = _filter_fft(circ, cst)
    hspec = hspec.reshape(2, D_HYENA, FFT_N1, 2 * LANES)
    du = ut.shape[1]
    u4 = ut.reshape(B, du, S // LANES, LANES)
    par_u = jnp.broadcast_to(jnp.concatenate([conv_w[i], conv_b[i][None]], 0)[:, :, None], (4, du, LANES))
    fb = jnp.broadcast_to(filt_bias[i][:, :, None], (2, D_HYENA, LANES))
    yht = _hyena(u4, par_u, fb, hspec, cst)

    wrt = jnp.zeros((D, LANES), F32).at[:, :N_GROUPS].set(w_group[i]).at[:, N_GROUPS:N_GROUPS + N_EXPERTS].set(w_router[i])
    brt = jnp.zeros((1, LANES), F32).at[0, :N_GROUPS].set(b_group[i]).at[0, N_GROUPS:N_GROUPS + N_EXPERTS].set(b_router[i])
    wrh = wrt.astype(BF16)
    wrl = (wrt - wrh.astype(F32)).astype(BF16)
    x1, h2, route, route_t = _outproj(ya, yht, x, g_attn_out[i][None], g_hyena_out[i][:, None],
                                      w_out[i].astype(BF16), bd, g_moe[i][None], wrh, wrl, brt)

    e_flat = jnp.transpose(route_t[:, :TOP_K], (1, 0, 2)).reshape(TOP_K * N).astype(jnp.int32)
    block_e, row_a, n_live = _dispatch(e_flat, N)
    y = _moe_experts(block_e, row_a, n_live, h2.reshape(N, ROW_CHUNKS, LANES), w_gate[i], w_up[i], w_down[i])

    out = _final(x1.reshape(N, D), y, route.reshape(N, LANES), p[i].reshape(N, -1), g_ple[i][None],
                 w_ple_gate[i].astype(BF16), b_ple_gate[i][None], w_ple[i].astype(BF16), g_final[None])
    return out.reshape(B, S, D)
```

```python
import functools
import math

import numpy as np
import jax
import jax.numpy as jnp
from jax import lax
from jax.experimental import pallas as pl
from jax.experimental.pallas import tpu as pltpu

F32 = jnp.float32
BF16 = jnp.bfloat16

D_MODEL = 1024
EPS = 1e-6
GRID_W = 64
N_HEADS = 8
N_KV_HEADS = 2
HEAD_DIM = 64
D_ATTN = N_HEADS * HEAD_DIM
D_KV = N_KV_HEADS * HEAD_DIM
ROPE_THETA = 10000.0
D_HYENA = 512
HYENA_HEAD = 64
FILTER_EMB = 33
FAST_DECAY_PCT = 0.3
SLOW_DECAY_PCT = 1.5
DECAY_TARGET = 1e-2
N_GROUPS = 4
EXPERTS_PER_GROUP = 8
N_EXPERTS = N_GROUPS * EXPERTS_PER_GROUP
TOP_K = 2
D_EXPERT = 512

LANES = 128
MXU_TILE = 256
FFT_N1 = 64
FFT_N2 = 128
VMEM_LIMIT = 56 * 1024 * 1024

TM_PROJ = 512
TQ_ATTN = 256
C_HY = 32
C_FILT = 128
ROW_CHUNKS = D_MODEL // 2 // LANES
SEQ_UNROLL = 32
TB_MOE = 256


def _cparams(sem):
    return pltpu.CompilerParams(dimension_semantics=sem, vmem_limit_bytes=VMEM_LIMIT)


def _rope_tables(S):
    half = HEAD_DIM // 2
    t = jnp.arange(S, dtype=F32)
    r_idx = jnp.floor(t / GRID_W)
    c_idx = t - r_idx * GRID_W
    inv = ROPE_THETA ** (-jnp.arange(0, half, 2, dtype=F32) / half)
    ang_r = r_idx[:, None] * inv[None]
    ang_c = c_idx[:, None] * inv[None]
    cos_h = jnp.concatenate([jnp.cos(ang_r), jnp.cos(ang_r), jnp.cos(ang_c), jnp.cos(ang_c)], axis=-1)
    sin_h = jnp.concatenate([-jnp.sin(ang_r), jnp.sin(ang_r), -jnp.sin(ang_c), jnp.sin(ang_c)], axis=-1)
    return jnp.tile(cos_h, (1, 2)), jnp.tile(sin_h, (1, 2))


def _dft_constants():
    n1, n2 = FFT_N1, FFT_N2
    n = n1 * n2
    a = np.arange(n1)
    ang = 2.0 * np.pi * np.outer(a, a) / n1
    far, fai = np.cos(ang), -np.sin(ang)
    hlf = n1 // 2
    ma = np.block([[far[:, :hlf], -fai[:, :hlf]], [fai[:, :hlf], far[:, :hlf]]])
    maf = np.concatenate([far, fai], axis=0)
    b = np.arange(n2)
    angt = 2.0 * np.pi * np.outer(a, b) / n
    tw = np.concatenate([np.cos(angt), -np.sin(angt)], axis=1)
    angb = 2.0 * np.pi * np.outer(b, b) / n2
    fbr, fbi = np.cos(angb), -np.sin(angb)
    g = np.block([[fbr, fbi], [-fbi, fbr]])
    ginv = np.block([[fbr, -fbi], [fbi, fbr]])
    minv_r = np.concatenate([far[:hlf], -fai[:hlf]], axis=0) / n
    minv_i = np.concatenate([fai[:hlf], far[:hlf]], axis=0) / n
    f = lambda m: jnp.asarray(m.astype(np.float32))
    return dict(ma=f(ma), maf=f(maf), tw=f(tw), g=f(g), ginv=f(ginv), minv_r=f(minv_r), minv_i=f(minv_i))


def _block_diag_ones(width, group):
    i = np.arange(width) // group
    return jnp.asarray((i[:, None] == i[None, :]).astype(np.float32)).astype(BF16)


def _group_sumsq(a, bd):
    sq = a * a
    hi = sq.astype(BF16)
    lo = (sq - hi.astype(F32)).astype(BF16)
    w = min(a.shape[-1], MXU_TILE)
    return jnp.concatenate(
        [jnp.dot(hi[:, c:c + w], bd[c:c + w, c:c + w], preferred_element_type=F32)
         + jnp.dot(lo[:, c:c + w], bd[c:c + w, c:c + w], preferred_element_type=F32)
         for c in range(0, a.shape[-1], w)], axis=-1)


def _head_norm_rope(a, gain, bd, cos, sin):
    width = a.shape[-1]
    n = a * lax.rsqrt(_group_sumsq(a, bd) * (1.0 / HEAD_DIM) + EPS) * gain
    rep = width // LANES
    if rep > 1:
        cos = jnp.concatenate([cos] * rep, axis=-1)
        sin = jnp.concatenate([sin] * rep, axis=-1)
    fwd = pltpu.roll(n, width - 16, 1)
    bwd = pltpu.roll(n, 16, 1)
    lane = lax.broadcasted_iota(jnp.int32, n.shape, 1)
    sw = jnp.where((lane % 32) < 16, fwd, bwd)
    return n * cos + sw * sin


def _inproj_body(x_ref, g_ref, wqkv_ref, wu_ref, bd_ref, qg_ref, kg_ref, cos_ref, sin_ref,
                 q_ref, kw_ref, vw_ref, ut_ref):
    x = x_ref[...]
    h = x * lax.rsqrt(jnp.mean(x * x, axis=-1, keepdims=True) + EPS) * g_ref[...]
    hb = h.astype(BF16)
    qkv = jnp.dot(hb, wqkv_ref[...], preferred_element_type=F32)
    cos = cos_ref[...]
    sin = sin_ref[...]
    bd = bd_ref[...]
    q = _head_norm_rope(qkv[:, :D_ATTN], qg_ref[...], bd, cos, sin)
    q_ref[...] = (q * (HEAD_DIM ** -0.5 * math.log2(math.e))).astype(BF16)
    k = _head_norm_rope(qkv[:, D_ATTN:D_ATTN + D_KV], kg_ref[...], bd[:D_KV, :D_KV], cos, sin)
    kt = k.T.astype(BF16)
    zero = jnp.zeros((HEAD_DIM, kt.shape[1]), BF16)
    for h in range(N_KV_HEADS):
        kh = kt[h * HEAD_DIM:(h + 1) * HEAD_DIM]
        kw_ref[h, 0, :HEAD_DIM] = kh
        kw_ref[h, 0, HEAD_DIM:] = zero
        kw_ref[h, 1, :HEAD_DIM] = zero
        kw_ref[h, 1, HEAD_DIM:] = kh
    v = qkv[:, D_ATTN + D_KV:]
    vr = pltpu.roll(v, HEAD_DIM, 1)
    first = lax.broadcasted_iota(jnp.int32, v.shape, 1) < HEAD_DIM
    vw_ref[0, 0] = jnp.where(first, v, 1.0).astype(BF16)
    vw_ref[0, 1] = jnp.where(first, 1.0, vr).astype(BF16)
    vw_ref[1, 0] = jnp.where(first, vr, 1.0).astype(BF16)
    vw_ref[1, 1] = jnp.where(first, 1.0, v).astype(BF16)
    ut_ref[...] = lax.dot_general(wu_ref[...], hb, (((1,), (1,)), ((), ())),
                                  preferred_element_type=F32)


def _inproj(x, g_mix, wqkv, wut, bd, qg, kg, cos, sin):
    B, S, D = x.shape
    tm = TM_PROJ
    du = wut.shape[0]
    full = lambda shape: pl.BlockSpec(shape, lambda b, i: (0,) * len(shape))
    return pl.pallas_call(
        _inproj_body,
        grid=(B, S // tm),
        in_specs=[
            pl.BlockSpec((None, tm, D), lambda b, i: (b, i, 0)),
            full((1, D)), full(wqkv.shape), full(wut.shape), full(bd.shape),
            full((1, D_ATTN)), full((1, D_KV)),
            pl.BlockSpec((tm, LANES), lambda b, i: (i, 0)),
            pl.BlockSpec((tm, LANES), lambda b, i: (i, 0)),
        ],
        out_specs=[
            pl.BlockSpec((None, tm, D_ATTN), lambda b, i: (b, i, 0)),
            pl.BlockSpec((None, N_KV_HEADS, 2, LANES, tm), lambda b, i: (b, 0, 0, 0, i)),
            pl.BlockSpec((None, N_KV_HEADS, 2, tm, LANES), lambda b, i: (b, 0, 0, i, 0)),
            pl.BlockSpec((None, du, tm), lambda b, i: (b, 0, i)),
        ],
        out_shape=[
            jax.ShapeDtypeStruct((B, S, D_ATTN), BF16),
            jax.ShapeDtypeStruct((B, N_KV_HEADS, 2, LANES, S), BF16),
            jax.ShapeDtypeStruct((B, N_KV_HEADS, 2, S, LANES), BF16),
            jax.ShapeDtypeStruct((B, du, S), F32),
        ],
        compiler_params=_cparams(("parallel", "parallel")),
        name="inproj",
    )(x, g_mix, wqkv, wut, bd, qg, kg, cos, sin)


def _attn_body(q_ref, kw_ref, vw_ref, o_ref):

    def one_head(q, kw, vw):
        s = jnp.dot(q, kw, preferred_element_type=F32)
        m = jnp.max(s, axis=-1, keepdims=True)
        p = jnp.exp2(s - m).astype(BF16)
        return jnp.dot(p, vw, preferred_element_type=F32)

    for pair in range(D_ATTN // LANES):
        h = pair // (N_HEADS // N_KV_HEADS // 2)
        q = q_ref[:, pair * LANES:(pair + 1) * LANES]
        oe = one_head(q, kw_ref[h, 0], vw_ref[h, 0])
        oo = one_head(q, kw_ref[h, 1], vw_ref[h, 1])
        first = lax.broadcasted_iota(jnp.int32, oe.shape, 1) < HEAD_DIM
        num = jnp.where(first, oe, oo)
        den = jnp.where(first, pltpu.roll(oe, HEAD_DIM, 1), pltpu.roll(oo, HEAD_DIM, 1))
        o_ref[:, pair * LANES:(pair + 1) * LANES] = num / den


def _attention(q, kw, vw):
    B, S, _ = q.shape
    tq = TQ_ATTN
    return pl.pallas_call(
        _attn_body,
        grid=(B, S // tq),
        in_specs=[
            pl.BlockSpec((None, tq, D_ATTN), lambda b, i: (b, i, 0)),
            pl.BlockSpec((None, N_KV_HEADS, 2, LANES, S), lambda b, i: (b, 0, 0, 0, 0)),
            pl.BlockSpec((None, N_KV_HEADS, 2, S, LANES), lambda b, i: (b, 0, 0, 0, 0)),
        ],
        out_specs=pl.BlockSpec((None, tq, D_ATTN), lambda b, i: (b, i, 0)),
        out_shape=jax.ShapeDtypeStruct((B, S, D_ATTN), F32),
        compiler_params=_cparams(("parallel", "arbitrary")),
        name="attention",
    )(q, kw, vw)


def _fwd_twiddle_store(y, tw_ref, s1_ref, row0):
    yr, yi = y[:FFT_N1], y[FFT_N1:]
    twr, twi = tw_ref[:, :LANES], tw_ref[:, LANES:]
    s1_ref[pl.ds(row0, FFT_N1), :LANES] = (yr * twr - yi * twi).astype(BF16)
    s1_ref[pl.ds(row0, FFT_N1), LANES:] = (yr * twi + yi * twr).astype(BF16)


def _filtfft_body(x_ref, maf_ref, tw_ref, g_ref, h_ref, s1_ref):
    C = x_ref.shape[0]

    def step_a(c, carry):
        y = jnp.dot(maf_ref[...], x_ref[c].astype(BF16), preferred_element_type=F32)
        _fwd_twiddle_store(y, tw_ref, s1_ref, pl.multiple_of(c * FFT_N1, FFT_N1))
        return carry

    lax.fori_loop(0, C, step_a, 0, unroll=SEQ_UNROLL)
    z = jnp.dot(s1_ref[...], g_ref[...], preferred_element_type=F32)
    h_ref[...] = z.reshape(C, FFT_N1, 2 * LANES)


def _filter_fft(circ, cst):
    n_seq = circ.shape[0]
    C = C_FILT
    full = lambda a: pl.BlockSpec(a.shape, lambda i: (0,) * a.ndim)
    maf, tw, g = cst["maf"].astype(BF16), cst["tw"], cst["g"].astype(BF16)
    return pl.pallas_call(
        _filtfft_body,
        grid=(n_seq // C,),
        in_specs=[pl.BlockSpec((C, FFT_N1, FFT_N2), lambda i: (i, 0, 0)), full(maf), full(tw), full(g)],
        out_specs=pl.BlockSpec((C, FFT_N1, 2 * LANES), lambda i: (i, 0, 0)),
        out_shape=jax.ShapeDtypeStruct((n_seq, FFT_N1, 2 * LANES), F32),
        scratch_shapes=[pltpu.VMEM((C * FFT_N1, 2 * LANES), BF16)],
        compiler_params=_cparams(("parallel",)),
        name="filter_fft",
    )(circ, maf, tw, g)


def _short_conv(x, par_ref, c):
    rows, lanes = x.shape
    a_i = lax.broadcasted_iota(jnp.int32, x.shape, 0)
    b_i = lax.broadcasted_iota(jnp.int32, x.shape, 1)
    l1 = pltpu.roll(x, 1, 1)
    l2 = pltpu.roll(l1, 1, 0)
    prev = jnp.where(b_i == 0, l2, l1)
    prev = jnp.where((a_i == 0) & (b_i == 0), 0.0, prev)
    r1 = pltpu.roll(x, lanes - 1, 1)
    r2 = pltpu.roll(r1, rows - 1, 0)
    nxt = jnp.where(b_i == lanes - 1, r2, r1)
    nxt = jnp.where((a_i == rows - 1) & (b_i == lanes - 1), 0.0, nxt)
    w0 = par_ref[0, pl.ds(c, 1), :]
    w1 = par_ref[1, pl.ds(c, 1), :]
    w2 = par_ref[2, pl.ds(c, 1), :]
    cb = par_ref[3, pl.ds(c, 1), :]
    return cb + prev * w0 + x * w1 + nxt * w2


def _hyena_body(v_ref, x1_ref, x2_ref, pv_ref, p1_ref, p2_ref, fb_ref, h_ref,
                ma_ref, tw_ref, g_ref, ginv_ref, mir_ref, mii_ref,
                o_ref, s1_ref, s2_ref, vc_ref, z1_ref):
    C = v_ref.shape[1]
    half = FFT_N1 // 2

    def spectral(order):
        z = jnp.dot(s1_ref[...], g_ref[...], preferred_element_type=F32)
        hs = h_ref[order].reshape(C * FFT_N1, 2 * LANES)
        zr, zi = z[:, :LANES], z[:, LANES:]
        hr, hi = hs[:, :LANES], hs[:, LANES:]
        pb = jnp.concatenate([zr * hr - zi * hi, zr * hi + zi * hr], axis=1).astype(BF16)
        s2_ref[...] = jnp.dot(pb, ginv_ref[...], preferred_element_type=F32)

    def inv_a(c):
        row0 = pl.multiple_of(c * FFT_N1, FFT_N1)
        y = s2_ref[pl.ds(row0, FFT_N1), :]
        yr, yi = y[:, :LANES], y[:, LANES:]
        twr, twi = tw_ref[:, :LANES], tw_ref[:, LANES:]
        ur = (yr * twr + yi * twi).astype(BF16)
        ui = (yi * twr - yr * twi).astype(BF16)
        out = (jnp.dot(mir_ref[...], ur, preferred_element_type=F32)
               + jnp.dot(mii_ref[...], ui, preferred_element_type=F32))
        return out[:half], out[half:]

    def fwd_a(c, xr, xi):
        xs = jnp.concatenate([xr, xi], axis=0).astype(BF16)
        y = jnp.dot(ma_ref[...], xs, preferred_element_type=F32)
        _fwd_twiddle_store(y, tw_ref, s1_ref, pl.multiple_of(c * FFT_N1, FFT_N1))

    def pass1_a(c, carry):
        vr = _short_conv(v_ref[0, c], pv_ref, c)
        vi = _short_conv(v_ref[1, c], pv_ref, c)
        vc_ref[0, c] = vr
        vc_ref[1, c] = vi
        fwd_a(c, vr, vi)
        return carry

    def pass1_b(c, carry):
        cr, ci = inv_a(c)
        bias = fb_ref[0, pl.ds(c, 1), :]
        zr = _short_conv(x1_ref[0, c], p1_ref, c) * (cr + bias * vc_ref[0, c])
        zi = _short_conv(x1_ref[1, c], p1_ref, c) * (ci + bias * vc_ref[1, c])
        z1_ref[0, c] = zr
        z1_ref[1, c] = zi
        fwd_a(c, zr, zi)
        return carry

    def pass2_b(c, carry):
        cr, ci = inv_a(c)
        bias = fb_ref[1, pl.ds(c, 1), :]
        vc_ref[0, c] = _short_conv(x2_ref[0, c], p2_ref, c) * (cr + bias * z1_ref[0, c])
        vc_ref[1, c] = _short_conv(x2_ref[1, c], p2_ref, c) * (ci + bias * z1_ref[1, c])
        return carry

    lax.fori_loop(0, C, pass1_a, 0, unroll=SEQ_UNROLL)
    spectral(0)
    lax.fori_loop(0, C, pass1_b, 0, unroll=SEQ_UNROLL)
    spectral(1)
    lax.fori_loop(0, C, pass2_b, 0, unroll=SEQ_UNROLL)
    for b2 in range(2):
        tiles = pltpu.einshape("cab->acb", vc_ref[b2])
        for a in range(tiles.shape[0]):
            o_ref[b2, :, a * LANES:(a + 1) * LANES] = tiles[a]


def _hyena(u4, par_u, fb, hspec, cst):
    B = u4.shape[0]
    C = C_HY
    J = D_HYENA // C
    rows = u4.shape[2]
    full = lambda a: pl.BlockSpec(a.shape, lambda j, p: (0,) * a.ndim)
    ma, g, ginv = cst["ma"].astype(BF16), cst["g"].astype(BF16), cst["ginv"].astype(BF16)
    mir, mii = cst["minv_r"].astype(BF16), cst["minv_i"].astype(BF16)
    tw = cst["tw"]
    u_spec = lambda k: pl.BlockSpec((2, C, rows, LANES), lambda j, p, k=k: (p, j + k * J, 0, 0))
    par_spec = lambda k: pl.BlockSpec((4, C, LANES), lambda j, p, k=k: (0, j + k * J, 0))
    return pl.pallas_call(
        _hyena_body,
        grid=(J, B // 2),
        in_specs=[
            u_spec(0), u_spec(1), u_spec(2), par_spec(0), par_spec(1), par_spec(2),
            pl.BlockSpec((2, C, LANES), lambda j, p: (0, j, 0)),
            pl.BlockSpec((2, C, FFT_N1, 2 * LANES), lambda j, p: (0, j, 0, 0)),
            full(ma), full(tw), full(g), full(ginv), full(mir), full(mii),
        ],
        out_specs=pl.BlockSpec((2, C, rows * LANES), lambda j, p: (p, j, 0)),
        out_shape=jax.ShapeDtypeStruct((B, D_HYENA, rows * LANES), F32),
        scratch_shapes=[
            pltpu.VMEM((C * FFT_N1, 2 * LANES), BF16),
            pltpu.VMEM((C * FFT_N1, 2 * LANES), F32),
            pltpu.VMEM((2, C, rows, LANES), F32),
            pltpu.VMEM((2, C, rows, LANES), F32),
        ],
        compiler_params=_cparams(("parallel", "arbitrary")),
        name="hyena",
    )(u4, u4, u4, par_u, par_u, par_u, fb, hspec, ma, tw, g, ginv, mir, mii)


def _dot3(a, b):
    ah = a.astype(BF16)
    al = (a - ah.astype(F32)).astype(BF16)
    bh = b.astype(BF16)
    bl = (b - bh.astype(F32)).astype(BF16)
    return (jnp.dot(ah, bh, preferred_element_type=F32) + jnp.dot(al, bh, preferred_element_type=F32)
            + jnp.dot(ah, bl, preferred_element_type=F32))


def _filtgen_body(zt_ref, w1_ref, b1_ref, f1_ref, w2_ref, b2_ref, f2_ref, w3f_ref, w3b_ref, ad_ref, tt_ref,
                  o_ref, hid_ref):
    L = hid_ref.shape[1] // 2

    @pl.when(pl.program_id(0) == 0)
    def _():
        h1 = jnp.sin(f1_ref[...] * (_dot3(w1_ref[...], zt_ref[...]) + b1_ref[...]))
        hid_ref[...] = jnp.sin(f2_ref[...] * (_dot3(w2_ref[...], h1) + b2_ref[...]))

    ad = ad_ref[...]
    hf = _dot3(w3f_ref[...], hid_ref[:, :L]) * jnp.exp(-ad * tt_ref[:, :L])
    hb = _dot3(w3b_ref[...], hid_ref[:, L:]) * jnp.exp(-ad * tt_ref[:, L:])
    hf = hf / (jnp.sum(jnp.abs(hf), axis=-1, keepdims=True) + EPS)
    hb = hb / (jnp.sum(jnp.abs(hb), axis=-1, keepdims=True) + EPS)
    first = lax.broadcasted_iota(jnp.int32, hb.shape, 1) == 0
    cf = hf + jnp.where(first, hb, 0.0)
    cb = jnp.where(first, 0.0, hb)
    chunks = [c[:, a * LANES:(a + 1) * LANES] for c in (cf, cb) for a in range(L // LANES)]
    o_ref[...] = pltpu.einshape("arl->ral", jnp.stack(chunks, axis=0))


def _filter_gen(L, w_f1, b_f1, freq1, w_f2, b_f2, freq2, w_f3):
    bands = (FILTER_EMB - 1) // 2
    t = jnp.linspace(0.0, 1.0, L, dtype=F32)[:, None]
    w = (2.0 * math.pi / L) * jnp.arange(L, dtype=F32)[:, None]
    f = jnp.linspace(1e-4, bands - 1, bands, dtype=F32)[None]
    zf = f * w
    z = jnp.concatenate([t, jnp.cos(zf), -jnp.sin(zf)], axis=-1)
    back = lambda a: jnp.roll(a[::-1], 1, axis=0)
    kpad = 48
    zt = jnp.pad(jnp.concatenate([z, back(z)], axis=0).T, ((0, kpad - FILTER_EMB), (0, 0)))
    tt = jnp.concatenate([t, back(t)], axis=0).T
    w1t = jnp.pad(w_f1.T, ((0, 0), (0, kpad - FILTER_EMB)))
    w3 = w_f3.reshape(-1, 2, 2, D_HYENA)
    w3f = jnp.transpose(w3[:, :, 0], (1, 2, 0)).reshape(2 * D_HYENA, -1)
    w3b = jnp.transpose(w3[:, :, 1], (1, 2, 0)).reshape(2 * D_HYENA, -1)
    max_decay = math.log(DECAY_TARGET) / FAST_DECAY_PCT
    min_decay = math.log(DECAY_TARGET) / SLOW_DECAY_PCT
    deltas = jnp.linspace(min_decay, max_decay, D_HYENA, dtype=F32)
    ad = jnp.tile(jnp.abs(deltas), 2)[:, None]
    col = lambda v: v[:, None]
    R = 128
    n_rows = 2 * D_HYENA
    full = lambda a: pl.BlockSpec(a.shape, lambda i: (0,) * a.ndim)
    rows = lambda a: pl.BlockSpec((R, a.shape[1]), lambda i: (i, 0))
    args = (zt, w1t, col(b_f1), col(freq1), w_f2.T, col(b_f2), col(freq2))
    return pl.pallas_call(
        _filtgen_body,
        grid=(n_rows // R,),
        in_specs=[full(a) for a in args] + [rows(w3f), rows(w3b), rows(ad), full(tt)],
        out_specs=pl.BlockSpec((R, 2 * L // LANES, LANES), lambda i: (i, 0, 0)),
        out_shape=jax.ShapeDtypeStruct((n_rows, 2 * L // LANES, LANES), F32),
        scratch_shapes=[pltpu.VMEM((w_f2.shape[1], 2 * L), F32)],
        compiler_params=_cparams(("arbitrary",)),
        name="filter_gen",
    )(*args, w3f, w3b, ad, tt)


def _route_lanes(lg):
    neg = -1e30
    lane = lax.broadcasted_iota(jnp.int32, lg.shape, 1)
    gmask = lane < N_GROUPS
    gl = jnp.where(gmask, lg, neg)
    gm = jnp.max(gl, axis=-1, keepdims=True)
    gsum = jnp.sum(jnp.where(gmask, jnp.exp(gl - gm), 0.0), axis=-1, keepdims=True)
    g_top = 1.0 / gsum
    g_sel = jnp.min(jnp.where(gl == gm, lane, LANES), axis=-1, keepdims=True)
    lo = N_GROUPS + EXPERTS_PER_GROUP * g_sel
    el = jnp.where((lane >= lo) & (lane < lo + EXPERTS_PER_GROUP), lg, neg)
    m1 = jnp.max(el, axis=-1, keepdims=True)
    i1 = jnp.min(jnp.where(el == m1, lane, LANES), axis=-1, keepdims=True)
    el2 = jnp.where(lane == i1, neg, el)
    m2 = jnp.max(el2, axis=-1, keepdims=True)
    i2 = jnp.min(jnp.where(el2 == m2, lane, LANES), axis=-1, keepdims=True)
    d = jnp.exp(m2 - m1)
    p1 = 1.0 / (1.0 + d)
    p2 = d / (1.0 + d)
    e1 = (i1 - N_GROUPS).astype(F32)
    e2 = (i2 - N_GROUPS).astype(F32)
    return jnp.where(lane == 0, e1, jnp.where(lane == 1, e2, jnp.where(lane == 2, g_top * p1,
                     jnp.where(lane == 3, g_top * p2, 0.0))))


def _pack_bf16_halves(a):
    w = a.shape[1] // 2
    bits = pltpu.bitcast(a.astype(BF16).astype(F32), jnp.uint32)
    return (bits[:, :w] >> 16) | (bits[:, w:] & jnp.uint32(0xFFFF0000))


def _unpack_bf16_halves(wd):
    lo = pltpu.bitcast(wd << 16, F32)
    hi = pltpu.bitcast(wd & jnp.uint32(0xFFFF0000), F32)
    return jnp.concatenate([lo, hi], axis=1)


def _store_row_tiles(ref, packed):
    chunks = jnp.stack([packed[:, j * LANES:(j + 1) * LANES] for j in range(ROW_CHUNKS)], axis=0)
    ref[...] = pltpu.einshape("jrl->rjl", chunks)


def _load_row_tiles(ref):
    chunks = pltpu.einshape("rjl->jrl", ref[...])
    return jnp.concatenate([chunks[j] for j in range(ROW_CHUNKS)], axis=1)


def _outproj_body(ya_ref, yh_ref, x_ref, ga_ref, gh_ref, wo_ref, bd_ref, gm_ref, wrh_ref, wrl_ref, brt_ref,
                  x1_ref, h2_ref, rt_ref, rtt_ref):
    ya = ya_ref[...]
    yan = ya * lax.rsqrt(_group_sumsq(ya, bd_ref[...]) * (1.0 / HEAD_DIM) + EPS) * ga_ref[...]
    yh = yh_ref[...]
    tm = yh.shape[1]
    yh3 = yh.reshape(D_HYENA // HYENA_HEAD, HYENA_HEAD, tm)
    ms = jnp.mean(yh3 * yh3, axis=1, keepdims=True)
    yhn = (yh3 * lax.rsqrt(ms + EPS)).reshape(D_HYENA, tm) * gh_ref[...]
    mix = (jnp.dot(yan.astype(BF16), wo_ref[:D_ATTN, :], preferred_element_type=F32)
           + jnp.dot(yhn.T.astype(BF16), wo_ref[D_ATTN:, :], preferred_element_type=F32))
    x1 = x_ref[...] + mix
    x1_ref[...] = x1
    h2 = x1 * lax.rsqrt(jnp.mean(x1 * x1, axis=-1, keepdims=True) + EPS) * gm_ref[...]
    _store_row_tiles(h2_ref, _pack_bf16_halves(h2))
    hi = h2.astype(BF16)
    lo = (h2 - hi.astype(F32)).astype(BF16)
    hw = jnp.dot(hi, jnp.concatenate([wrh_ref[...], wrl_ref[...]], axis=1),
                 preferred_element_type=F32)
    lg = hw[:, :LANES] + hw[:, LANES:] + jnp.dot(lo, wrh_ref[...], preferred_element_type=F32) + brt_ref[...]
    route = _route_lanes(lg)
    rt_ref[...] = route
    rtt_ref[...] = route.T[:8]


def _outproj(ya, yht, x, ga, gh, wo, bd, gm, wrh, wrl, brt):
    B, S, D = x.shape
    tm = TM_PROJ
    full = lambda a: pl.BlockSpec(a.shape, lambda b, i: (0,) * a.ndim)
    return pl.pallas_call(
        _outproj_body,
        grid=(B, S // tm),
        in_specs=[
            pl.BlockSpec((None, tm, D_ATTN), lambda b, i: (b, i, 0)),
            pl.BlockSpec((None, D_HYENA, tm), lambda b, i: (b, 0, i)),
            pl.BlockSpec((None, tm, D), lambda b, i: (b, i, 0)),
            full(ga), full(gh), full(wo), full(bd), full(gm), full(wrh), full(wrl), full(brt),
        ],
        out_specs=[
            pl.BlockSpec((None, tm, D), lambda b, i: (b, i, 0)),
            pl.BlockSpec((None, tm, ROW_CHUNKS, LANES), lambda b, i: (b, i, 0, 0)),
            pl.BlockSpec((None, tm, LANES), lambda b, i: (b, i, 0)),
            pl.BlockSpec((None, 8, tm), lambda b, i: (b, 0, i)),
        ],
        out_shape=[
            jax.ShapeDtypeStruct((B, S, D), F32),
            jax.ShapeDtypeStruct((B, S, ROW_CHUNKS, LANES), jnp.uint32),
            jax.ShapeDtypeStruct((B, S, LANES), F32),
            jax.ShapeDtypeStruct((B, 8, S), F32),
        ],
        compiler_params=_cparams(("parallel", "parallel")),
        name="outproj",
    )(ya, yht, x, ga, gh, wo, bd, gm, wrh, wrl, brt)


def _moe_body(be_ref, ra_ref, nlive_ref, h2_hbm, wg_ref, wu_ref, wd_ref, y_hbm,
              wg_s, wu_s, wd_s, xbuf, ybuf, zbuf, sem_in, sem_out, sem_z, *, n_tok, n_rows):
    i = pl.program_id(0)
    nb = nlive_ref[0]
    T = xbuf.shape[1]
    slot = i % 2

    def issue_gathers(blk, sl):
        for r in range(T):
            tok = ra_ref[blk * T + r] & (n_tok - 1)
            pltpu.make_async_copy(h2_hbm.at[tok], xbuf.at[sl, r], sem_in.at[sl]).start()

    def issue_scatters(blk, sl, spare):
        for r in range(T):
            dst = jnp.where(spare, n_rows + r, ra_ref[blk * T + r])
            pltpu.make_async_copy(ybuf.at[sl, r], y_hbm.at[dst], sem_out.at[sl]).start()

    def block_in_wait(sl):
        pltpu.make_async_copy(h2_hbm.at[pl.ds(0, T)], xbuf.at[sl], sem_in.at[sl]).wait()

    def block_out_wait(sl):
        pltpu.make_async_copy(ybuf.at[sl], y_hbm.at[pl.ds(0, T)], sem_out.at[sl]).wait()

    @pl.when(i == 0)
    def _():
        ybuf[...] = jnp.zeros(ybuf.shape, ybuf.dtype)
        zbuf[...] = jnp.zeros(zbuf.shape, zbuf.dtype)
        issue_gathers(0, 0)

    @pl.when(i >= nb)
    def _():
        fill = pltpu.make_async_copy(zbuf, y_hbm.at[pl.ds(i * T, T)], sem_z.at[0])
        fill.start()
        fill.wait()

    prev = be_ref[jnp.maximum(i - 1, 0)]

    @pl.when((i == 0) | (be_ref[i] != prev))
    def _():
        wg_s[...] = wg_ref[...].astype(BF16)
        wu_s[...] = wu_ref[...].astype(BF16)
        wd_s[...] = wd_ref[...].astype(BF16)

    @pl.when(i < nb)
    def _():
        block_in_wait(slot)
        x = _unpack_bf16_halves(_load_row_tiles(xbuf.at[slot])).astype(BF16)
        issue_gathers(jnp.minimum(i + 1, nb - 1), 1 - slot)
        issue_scatters(jnp.maximum(i - 1, 0), 1 - slot, i == 0)
        a = jnp.dot(x, wg_s[...], preferred_element_type=F32)
        b = jnp.dot(x, wu_s[...], preferred_element_type=F32)
        hmid = (a * jax.nn.sigmoid(a)) * b
        y = _pack_bf16_halves(jnp.dot(hmid.astype(BF16), wd_s[...], preferred_element_type=F32))

        @pl.when(i >= 1)
        def _():
            block_out_wait(slot)

        _store_row_tiles(ybuf.at[slot], y)

        @pl.when(i == nb - 1)
        def _():
            issue_scatters(i, slot, False)
            block_in_wait(1 - slot)
            block_out_wait(1 - slot)
            block_out_wait(slot)


def _moe_experts(block_e, row_a, n_live, h2p, w_gate, w_up, w_down):
    n_tok = h2p.shape[0]
    n_rows = row_a.shape[0]
    row = h2p.shape[1:]
    D = w_gate.shape[1]
    T = TB_MOE
    assert n_tok & (n_tok - 1) == 0
    grid_spec = pltpu.PrefetchScalarGridSpec(
        num_scalar_prefetch=3,
        grid=(row_a.shape[0] // T,),
        in_specs=[
            pl.BlockSpec(memory_space=pl.ANY),
            pl.BlockSpec((None, D, D_EXPERT), lambda i, be, ra, nl: (be[i], 0, 0)),
            pl.BlockSpec((None, D, D_EXPERT), lambda i, be, ra, nl: (be[i], 0, 0)),
            pl.BlockSpec((None, D_EXPERT, D), lambda i, be, ra, nl: (be[i], 0, 0)),
        ],
        out_specs=pl.BlockSpec(memory_space=pl.ANY),
        scratch_shapes=[
            pltpu.VMEM((D, D_EXPERT), BF16), pltpu.VMEM((D, D_EXPERT), BF16), pltpu.VMEM((D_EXPERT, D), BF16),
            pltpu.VMEM((2, T) + row, jnp.uint32), pltpu.VMEM((2, T) + row, jnp.uint32),
            pltpu.VMEM((T,) + row, jnp.uint32),
            pltpu.SemaphoreType.DMA((2,)), pltpu.SemaphoreType.DMA((2,)), pltpu.SemaphoreType.DMA((1,)),
        ],
    )
    return pl.pallas_call(
        functools.partial(_moe_body, n_tok=n_tok, n_rows=n_rows),
        grid_spec=grid_spec,
        out_shape=jax.ShapeDtypeStruct((n_rows + T,) + row, jnp.uint32),
        compiler_params=_cparams(("arbitrary",)),
        name="moe_experts",
    )(block_e, row_a, n_live, h2p, w_gate, w_up, w_down)


def _dispatch(e_flat, N):
    T = TB_MOE
    NK = N * TOP_K
    experts = jnp.arange(N_EXPERTS, dtype=jnp.int32)
    order = jnp.argsort(e_flat).astype(jnp.int32)
    onehot = (e_flat[:, None] == experts[None]).astype(jnp.int32)
    counts = jnp.sum(onehot, axis=0)
    ends = jnp.cumsum(counts)
    starts = ends - counts
    padded = (counts + T - 1) // T * T
    pends = jnp.cumsum(padded)
    pstarts = pends - padded
    n_rows = -(-(NK + N_EXPERTS * (T - 1)) // T) * T
    n_blocks = n_rows // T
    blk_start = jnp.arange(n_blocks, dtype=jnp.int32) * T
    block_e = jnp.clip(jnp.sum((pends[None, :] <= blk_start[:, None]).astype(jnp.int32), axis=1),
                       0, N_EXPERTS - 1)
    oh_b = (block_e[:, None] == experts[None]).astype(jnp.int32)
    base = jnp.sum(oh_b * (starts - pstarts)[None], axis=1) + blk_start
    end_b = jnp.sum(oh_b * ends[None], axis=1)
    lane = jnp.arange(T, dtype=jnp.int32)[None]
    src = base[:, None] + lane
    pad_id = NK + blk_start[:, None] + lane - end_b[:, None]
    row_a = jnp.where(src < end_b[:, None], order[jnp.clip(src, 0, NK - 1)], pad_id)
    n_live = (pends[-1:] // T).astype(jnp.int32)
    return block_e.astype(jnp.int32), row_a.reshape(n_rows).astype(jnp.int32), n_live


def _final_body(x1_ref, y0_ref, y1_ref, rt_ref, p_ref, gp_ref, wg_ref, bg_ref, wp_ref, gf_ref, o_ref):
    w0 = rt_ref[:, 2:3]
    w1 = rt_ref[:, 3:4]
    y0 = _unpack_bf16_halves(_load_row_tiles(y0_ref))
    y1 = _unpack_bf16_halves(_load_row_tiles(y1_ref))
    x2 = x1_ref[...] + (y0 * w0 + y1 * w1)
    hp = x2 * lax.rsqrt(jnp.mean(x2 * x2, axis=-1, keepdims=True) + EPS) * gp_ref[...]
    gate = jax.nn.sigmoid(jnp.dot(hp.astype(BF16), wg_ref[...], preferred_element_type=F32) + bg_ref[...])
    pe = jnp.dot(p_ref[...].astype(BF16), wp_ref[...], preferred_element_type=F32)
    x3 = x2 + pe * gate
    o_ref[...] = x3 * lax.rsqrt(jnp.mean(x3 * x3, axis=-1, keepdims=True) + EPS) * gf_ref[...]


def _final(x1, y, route, p, gp, wg, bg, wp, gf):
    N, D = x1.shape
    tm = TM_PROJ
    row = lambda w: pl.BlockSpec((tm, w), lambda i: (i, 0))
    full = lambda a: pl.BlockSpec(a.shape, lambda i: (0,) * a.ndim)
    y0, y1 = y, y
    return pl.pallas_call(
        _final_body,
        grid=(N // tm,),
        in_specs=[row(D), pl.BlockSpec((tm, ROW_CHUNKS, LANES), lambda i: (i, 0, 0)),
                  pl.BlockSpec((tm, ROW_CHUNKS, LANES), lambda i: (i + N // tm, 0, 0)),
                  row(LANES), row(p.shape[1]),
                  full(gp), full(wg), full(bg), full(wp), full(gf)],
        out_specs=row(D),
        out_shape=jax.ShapeDtypeStruct((N, D), F32),
        compiler_params=_cparams(("parallel",)),
        name="ple_final",
    )(x1, y0, y1, route, p, gp, wg, bg, wp, gf)


def kernel(x, p, g_mix, w_in, q_gain, k_gain, conv_w, conv_b, w_f1, b_f1, freq1, w_f2, b_f2, freq2, w_f3, filt_bias, g_attn_out, g_hyena_out, w_out, g_moe, w_group, b_group, w_router, b_router, w_gate, w_up, w_down, g_ple, w_ple_gate, b_ple_gate, w_ple, g_final):
    B, S, D = x.shape
    N = B * S
    assert p.shape[0] == 1 and S == (FFT_N1 // 2) * FFT_N2 and B % 2 == 0
    i = 0
    cst = _dft_constants()
    cos, sin = _rope_tables(S)
    bd = _block_diag_ones(D_ATTN, HEAD_DIM)

    n_qkv = D_ATTN + 2 * D_KV
    wqkv = w_in[i][:, :n_qkv].astype(BF16)
    wut = w_in[i][:, n_qkv:].T.astype(BF16)
    q, kw, vw, ut = _inproj(x, g_mix[i][None], wqkv, wut, bd,
                            jnp.tile(q_gain[i], N_HEADS)[None], jnp.tile(k_gain[i], N_KV_HEADS)[None], cos, sin)

    ya = _attention(q, kw, vw)

    circ = _filter_gen(S, w_f1[i], b_f1[i], freq1[i], w_f2[i], b_f2[i], freq2[i], w_f3[i])
    hspec = _filter_fft(circ, cst)
    hspec = hspec.reshape(2, D_HYENA, FFT_N1, 2 * LANES)
    du = ut.shape[1]
    u4 = ut.reshape(B, du, S // LANES, LANES)
    par_u = jnp.broadcast_to(jnp.concatenate([conv_w[i], conv_b[i][None]], 0)[:, :, None], (4, du, LANES))
    fb = jnp.broadcast_to(filt_bias[i][:, :, None], (2, D_HYENA, LANES))
    yht = _hyena(u4, par_u, fb, hspec, cst)

    wrt = jnp.zeros((D, LANES), F32).at[:, :N_GROUPS].set(w_group[i]).at[:, N_GROUPS:N_GROUPS + N_EXPERTS].set(w_router[i])
    brt = jnp.zeros((1, LANES), F32).at[0, :N_GROUPS].set(b_group[i]).at[0, N_GROUPS:N_GROUPS + N_EXPERTS].set(b_router[i])
    wrh = wrt.astype(BF16)
    wrl = (wrt - wrh.astype(F32)).astype(BF16)
    x1, h2, route, route_t = _outproj(ya, yht, x, g_attn_out[i][None], g_hyena_out[i][:, None],
                                      w_out[i].astype(BF16), bd, g_moe[i][None], wrh, wrl, brt)

    e_flat = jnp.transpose(route_t[:, :TOP_K], (1, 0, 2)).reshape(TOP_K * N).astype(jnp.int32)
    block_e, row_a, n_live = _dispatch(e_flat, N)
    y = _moe_experts(block_e, row_a, n_live, h2.reshape(N, ROW_CHUNKS, LANES), w_gate[i], w_up[i], w_down[i])

    out = _final(x1.reshape(N, D), y, route.reshape(N, LANES), p[i].reshape(N, -1), g_ple[i][None],
                 w_ple_gate[i].astype(BF16), b_ple_gate[i][None], w_ple[i].astype(BF16), g_final[None])
    return out.reshape(B, S, D)
```

```python
import functools
import math

import numpy as np
import jax
import jax.numpy as jnp
from jax import lax
from jax.experimental import pallas as pl
from jax.experimental.pallas import tpu as pltpu

F32 = jnp.float32
BF16 = jnp.bfloat16

D_MODEL = 1024
EPS = 1e-6
GRID_W = 64
N_HEADS = 8
N_KV_HEADS = 2
HEAD_DIM = 64
D_ATTN = N_HEADS * HEAD_DIM
D_KV = N_KV_HEADS * HEAD_DIM
ROPE_THETA = 10000.0
D_HYENA = 512
HYENA_HEAD = 64
FILTER_EMB = 33
FAST_DECAY_PCT = 0.3
SLOW_DECAY_PCT = 1.5
DECAY_TARGET = 1e-2
N_GROUPS = 4
EXPERTS_PER_GROUP = 8
N_EXPERTS = N_GROUPS * EXPERTS_PER_GROUP
TOP_K = 2
D_EXPERT = 512

LANES = 128
MXU_TILE = 256
FFT_N1 = 64
FFT_N2 = 128
VMEM_LIMIT = 56 * 1024 * 1024

TM_PROJ = 512
TQ_ATTN = 256
C_HY = 32
C_FILT = 128
ROW_CHUNKS = D_MODEL // 2 // LANES
SEQ_UNROLL = 32
TB_MOE = 256


def _cparams(sem):
    return pltpu.CompilerParams(dimension_semantics=sem, vmem_limit_bytes=VMEM_LIMIT)


def _rope_tables(S):
    half = HEAD_DIM // 2
    t = jnp.arange(S, dtype=F32)
    r_idx = jnp.floor(t / GRID_W)
    c_idx = t - r_idx * GRID_W
    inv = ROPE_THETA ** (-jnp.arange(0, half, 2, dtype=F32) / half)
    ang_r = r_idx[:, None] * inv[None]
    ang_c = c_idx[:, None] * inv[None]
    cos_h = jnp.concatenate([jnp.cos(ang_r), jnp.cos(ang_r), jnp.cos(ang_c), jnp.cos(ang_c)], axis=-1)
    sin_h = jnp.concatenate([-jnp.sin(ang_r), jnp.sin(ang_r), -jnp.sin(ang_c), jnp.sin(ang_c)], axis=-1)
    return jnp.tile(cos_h, (1, 2)), jnp.tile(sin_h, (1, 2))


def _dft_constants():
    n1, n2 = FFT_N1, FFT_N2
    n = n1 * n2
    a = np.arange(n1)
    ang = 2.0 * np.pi * np.outer(a, a) / n1
    far, fai = np.cos(ang), -np.sin(ang)
    hlf = n1 // 2
    ma = np.block([[far[:, :hlf], -fai[:, :hlf]], [fai[:, :hlf], far[:, :hlf]]])
    maf = np.concatenate([far, fai], axis=0)
    b = np.arange(n2)
    angt = 2.0 * np.pi * np.outer(a, b) / n
    tw = np.concatenate([np.cos(angt), -np.sin(angt)], axis=1)
    angb = 2.0 * np.pi * np.outer(b, b) / n2
    fbr, fbi = np.cos(angb), -np.sin(angb)
    g = np.block([[fbr, fbi], [-fbi, fbr]])
    ginv = np.block([[fbr, -fbi], [fbi, fbr]])
    minv_r = np.concatenate([far[:hlf], -fai[:hlf]], axis=0) / n
    minv_i = np.concatenate([fai[:hlf], far[:hlf]], axis=0) / n
    f = lambda m: jnp.asarray(m.astype(np.float32))
    return dict(ma=f(ma), maf=f(maf), tw=f(tw), g=f(g), ginv=f(ginv), minv_r=f(minv_r), minv_i=f(minv_i))


def _block_diag_ones(width, group):
    i = np.arange(width) // group
    return jnp.asarray((i[:, None] == i[None, :]).astype(np.float32)).astype(BF16)


def _group_sumsq(a, bd):
    sq = a * a
    hi = sq.astype(BF16)
    lo = (sq - hi.astype(F32)).astype(BF16)
    w = min(a.shape[-1], MXU_TILE)
    return jnp.concatenate(
        [jnp.dot(hi[:, c:c + w], bd[c:c + w, c:c + w], preferred_element_type=F32)
         + jnp.dot(lo[:, c:c + w], bd[c:c + w, c:c + w], preferred_element_type=F32)
         for c in range(0, a.shape[-1], w)], axis=-1)


def _head_norm_rope(a, gain, bd, cos, sin):
    width = a.shape[-1]
    n = a * lax.rsqrt(_group_sumsq(a, bd) * (1.0 / HEAD_DIM) + EPS) * gain
    rep = width // LANES
    if rep > 1:
        cos = jnp.concatenate([cos] * rep, axis=-1)
        sin = jnp.concatenate([sin] * rep, axis=-1)
    fwd = pltpu.roll(n, width - 16, 1)
    bwd = pltpu.roll(n, 16, 1)
    lane = lax.broadcasted_iota(jnp.int32, n.shape, 1)
    sw = jnp.where((lane % 32) < 16, fwd, bwd)
    return n * cos + sw * sin


def _inproj_body(x_ref, g_ref, wqkv_ref, wu_ref, bd_ref, qg_ref, kg_ref, cos_ref, sin_ref,
                 q_ref, kw_ref, vw_ref, ut_ref):
    x = x_ref[...]
    h = x * lax.rsqrt(jnp.mean(x * x, axis=-1, keepdims=True) + EPS) * g_ref[...]
    hb = h.astype(BF16)
    qkv = jnp.dot(hb, wqkv_ref[...], preferred_element_type=F32)
    cos = cos_ref[...]
    sin = sin_ref[...]
    bd = bd_ref[...]
    q = _head_norm_rope(qkv[:, :D_ATTN], qg_ref[...], bd, cos, sin)
    q_ref[...] = (q * (HEAD_DIM ** -0.5 * math.log2(math.e))).astype(BF16)
    k = _head_norm_rope(qkv[:, D_ATTN:D_ATTN + D_KV], kg_ref[...], bd[:D_KV, :D_KV], cos, sin)
    kt = k.T.astype(BF16)
    zero = jnp.zeros((HEAD_DIM, kt.shape[1]), BF16)
    for h in range(N_KV_HEADS):
        kh = kt[h * HEAD_DIM:(h + 1) * HEAD_DIM]
        kw_ref[h, 0, :HEAD_DIM] = kh
        kw_ref[h, 0, HEAD_DIM:] = zero
        kw_ref[h, 1, :HEAD_DIM] = zero
        kw_ref[h, 1, HEAD_DIM:] = kh
    v = qkv[:, D_ATTN + D_KV:]
    vr = pltpu.roll(v, HEAD_DIM, 1)
    first = lax.broadcasted_iota(jnp.int32, v.shape, 1) < HEAD_DIM
    vw_ref[0, 0] = jnp.where(first, v, 1.0).astype(BF16)
    vw_ref[0, 1] = jnp.where(first, 1.0, vr).astype(BF16)
    vw_ref[1, 0] = jnp.where(first, vr, 1.0).astype(BF16)
    vw_ref[1, 1] = jnp.where(first, 1.0, v).astype(BF16)
    ut_ref[...] = lax.dot_general(wu_ref[...], hb, (((1,), (1,)), ((), ())),
                                  preferred_element_type=F32)


def _inproj(x, g_mix, wqkv, wut, bd, qg, kg, cos, sin):
    B, S, D = x.shape
    tm = TM_PROJ
    du = wut.shape[0]
    full = lambda shape: pl.BlockSpec(shape, lambda b, i: (0,) * len(shape))
    return pl.pallas_call(
        _inproj_body,
        grid=(B, S // tm),
        in_specs=[
            pl.BlockSpec((None, tm, D), lambda b, i: (b, i, 0)),
            full((1, D)), full(wqkv.shape), full(wut.shape), full(bd.shape),
            full((1, D_ATTN)), full((1, D_KV)),
            pl.BlockSpec((tm, LANES), lambda b, i: (i, 0)),
            pl.BlockSpec((tm, LANES), lambda b, i: (i, 0)),
        ],
        out_specs=[
            pl.BlockSpec((None, tm, D_ATTN), lambda b, i: (b, i, 0)),
            pl.BlockSpec((None, N_KV_HEADS, 2, LANES, tm), lambda b, i: (b, 0, 0, 0, i)),
            pl.BlockSpec((None, N_KV_HEADS, 2, tm, LANES), lambda b, i: (b, 0, 0, i, 0)),
            pl.BlockSpec((None, du, tm), lambda b, i: (b, 0, i)),
        ],
        out_shape=[
            jax.ShapeDtypeStruct((B, S, D_ATTN), BF16),
            jax.ShapeDtypeStruct((B, N_KV_HEADS, 2, LANES, S), BF16),
            jax.ShapeDtypeStruct((B, N_KV_HEADS, 2, S, LANES), BF16),
            jax.ShapeDtypeStruct((B, du, S), F32),
        ],
        compiler_params=_cparams(("parallel", "parallel")),
        name="inproj",
    )(x, g_mix, wqkv, wut, bd, qg, kg, cos, sin)


def _attn_body(q_ref, kw_ref, vw_ref, o_ref):

    def one_head(q, kw, vw):
        s = jnp.dot(q, kw, preferred_element_type=F32)
        m = jnp.max(s, axis=-1, keepdims=True)
        p = jnp.exp2(s - m).astype(BF16)
        return jnp.dot(p, vw, preferred_element_type=F32)

    for pair in range(D_ATTN // LANES):
        h = pair // (N_HEADS // N_KV_HEADS // 2)
        q = q_ref[:, pair * LANES:(pair + 1) * LANES]
        oe = one_head(q, kw_ref[h, 0], vw_ref[h, 0])
        oo = one_head(q, kw_ref[h, 1], vw_ref[h, 1])
        first = lax.broadcasted_iota(jnp.int32, oe.shape, 1) < HEAD_DIM
        num = jnp.where(first, oe, oo)
        den = jnp.where(first, pltpu.roll(oe, HEAD_DIM, 1), pltpu.roll(oo, HEAD_DIM, 1))
        o_ref[:, pair * LANES:(pair + 1) * LANES] = num / den


def _attention(q, kw, vw):
    B, S, _ = q.shape
    tq = TQ_ATTN
    return pl.pallas_call(
        _attn_body,
        grid=(B, S // tq),
        in_specs=[
            pl.BlockSpec((None, tq, D_ATTN), lambda b, i: (b, i, 0)),
            pl.BlockSpec((None, N_KV_HEADS, 2, LANES, S), lambda b, i: (b, 0, 0, 0, 0)),
            pl.BlockSpec((None, N_KV_HEADS, 2, S, LANES), lambda b, i: (b, 0, 0, 0, 0)),
        ],
        out_specs=pl.BlockSpec((None, tq, D_ATTN), lambda b, i: (b, i, 0)),
        out_shape=jax.ShapeDtypeStruct((B, S, D_ATTN), F32),
        compiler_params=_cparams(("parallel", "arbitrary")),
        name="attention",
    )(q, kw, vw)


def _fwd_twiddle_store(y, tw_ref, s1_ref, row0):
    yr, yi = y[:FFT_N1], y[FFT_N1:]
    twr, twi = tw_ref[:, :LANES], tw_ref[:, LANES:]
    s1_ref[pl.ds(row0, FFT_N1), :LANES] = (yr * twr - yi * twi).astype(BF16)
    s1_ref[pl.ds(row0, FFT_N1), LANES:] = (yr * twi + yi * twr).astype(BF16)


def _filtfft_body(x_ref, maf_ref, tw_ref, g_ref, h_ref, s1_ref):
    C = x_ref.shape[0]

    def step_a(c, carry):
        y = jnp.dot(maf_ref[...], x_ref[c].astype(BF16), preferred_element_type=F32)
        _fwd_twiddle_store(y, tw_ref, s1_ref, pl.multiple_of(c * FFT_N1, FFT_N1))
        return carry

    lax.fori_loop(0, C, step_a, 0, unroll=SEQ_UNROLL)
    z = jnp.dot(s1_ref[...], g_ref[...], preferred_element_type=F32)
    h_ref[...] = z.reshape(C, FFT_N1, 2 * LANES)


def _filter_fft(circ, cst):
    n_seq = circ.shape[0]
    C = C_FILT
    full = lambda a: pl.BlockSpec(a.shape, lambda i: (0,) * a.ndim)
    maf, tw, g = cst["maf"].astype(BF16), cst["tw"], cst["g"].astype(BF16)
    return pl.pallas_call(
        _filtfft_body,
        grid=(n_seq // C,),
        in_specs=[pl.BlockSpec((C, FFT_N1, FFT_N2), lambda i: (i, 0, 0)), full(maf), full(tw), full(g)],
        out_specs=pl.BlockSpec((C, FFT_N1, 2 * LANES), lambda i: (i, 0, 0)),
        out_shape=jax.ShapeDtypeStruct((n_seq, FFT_N1, 2 * LANES), F32),
        scratch_shapes=[pltpu.VMEM((C * FFT_N1, 2 * LANES), BF16)],
        compiler_params=_cparams(("parallel",)),
        name="filter_fft",
    )(circ, maf, tw, g)


def _short_conv(x, par_ref, c):
    rows, lanes = x.shape
    a_i = lax.broadcasted_iota(jnp.int32, x.shape, 0)
    b_i = lax.broadcasted_iota(jnp.int32, x.shape, 1)
    l1 = pltpu.roll(x, 1, 1)
    l2 = pltpu.roll(l1, 1, 0)
    prev = jnp.where(b_i == 0, l2, l1)
    prev = jnp.where((a_i == 0) & (b_i == 0), 0.0, prev)
    r1 = pltpu.roll(x, lanes - 1, 1)
    r2 = pltpu.roll(r1, rows - 1, 0)
    nxt = jnp.where(b_i == lanes - 1, r2, r1)
    nxt = jnp.where((a_i == rows - 1) & (b_i == lanes - 1), 0.0, nxt)
    w0 = par_ref[0, pl.ds(c, 1), :]
    w1 = par_ref[1, pl.ds(c, 1), :]
    w2 = par_ref[2, pl.ds(c, 1), :]
    cb = par_ref[3, pl.ds(c, 1), :]
    return cb + prev * w0 + x * w1 + nxt * w2


def _hyena_body(v_ref, x1_ref, x2_ref, pv_ref, p1_ref, p2_ref, fb_ref, h_ref,
                ma_ref, tw_ref, g_ref, ginv_ref, mir_ref, mii_ref,
                o_ref, s1_ref, s2_ref, vc_ref, z1_ref):
    C = v_ref.shape[1]
    half = FFT_N1 // 2

    def spectral(order):
        z = jnp.dot(s1_ref[...], g_ref[...], preferred_element_type=F32)
        hs = h_ref[order].reshape(C * FFT_N1, 2 * LANES)
        zr, zi = z[:, :LANES], z[:, LANES:]
        hr, hi = hs[:, :LANES], hs[:, LANES:]
        pb = jnp.concatenate([zr * hr - zi * hi, zr * hi + zi * hr], axis=1).astype(BF16)
        s2_ref[...] = jnp.dot(pb, ginv_ref[...], preferred_element_type=F32)

    def inv_a(c):
        row0 = pl.multiple_of(c * FFT_N1, FFT_N1)
        y = s2_ref[pl.ds(row0, FFT_N1), :]
        yr, yi = y[:, :LANES], y[:, LANES:]
        twr, twi = tw_ref[:, :LANES], tw_ref[:, LANES:]
        ur = (yr * twr + yi * twi).astype(BF16)
        ui = (yi * twr - yr * twi).astype(BF16)
        out = (jnp.dot(mir_ref[...], ur, preferred_element_type=F32)
               + jnp.dot(mii_ref[...], ui, preferred_element_type=F32))
        return out[:half], out[half:]

    def fwd_a(c, xr, xi):
        xs = jnp.concatenate([xr, xi], axis=0).astype(BF16)
        y = jnp.dot(ma_ref[...], xs, preferred_element_type=F32)
        _fwd_twiddle_store(y, tw_ref, s1_ref, pl.multiple_of(c * FFT_N1, FFT_N1))

    def pass1_a(c, carry):
        vr = _short_conv(v_ref[0, c], pv_ref, c)
        vi = _short_conv(v_ref[1, c], pv_ref, c)
        vc_ref[0, c] = vr
        vc_ref[1, c] = vi
        fwd_a(c, vr, vi)
        return carry

    def pass1_b(c, carry):
        cr, ci = inv_a(c)
        bias = fb_ref[0, pl.ds(c, 1), :]
        zr = _short_conv(x1_ref[0, c], p1_ref, c) * (cr + bias * vc_ref[0, c])
        zi = _short_conv(x1_ref[1, c], p1_ref, c) * (ci + bias * vc_ref[1, c])
        z1_ref[0, c] = zr
        z1_ref[1, c] = zi
        fwd_a(c, zr, zi)
        return carry

    def pass2_b(c, carry):
        cr, ci = inv_a(c)
        bias = fb_ref[1, pl.ds(c, 1), :]
        vc_ref[0, c] = _short_conv(x2_ref[0, c], p2_ref, c) * (cr + bias * z1_ref[0, c])
        vc_ref[1, c] = _short_conv(x2_ref[1, c], p2_ref, c) * (ci + bias * z1_ref[1, c])
        return carry

    lax.fori_loop(0, C, pass1_a, 0, unroll=SEQ_UNROLL)
    spectral(0)
    lax.fori_loop(0, C, pass1_b, 0, unroll=SEQ_UNROLL)
    spectral(1)
    lax.fori_loop(0, C, pass2_b, 0, unroll=SEQ_UNROLL)
    for b2 in range(2):
        tiles = pltpu.einshape("cab->acb", vc_ref[b2])
        for a in range(tiles.shape[0]):
            o_ref[b2, :, a * LANES:(a + 1) * LANES] = tiles[a]


def _hyena(u4, par_u, fb, hspec, cst):
    B = u4.shape[0]
    C = C_HY
    J = D_HYENA // C
    rows = u4.shape[2]
    full = lambda a: pl.BlockSpec(a.shape, lambda j, p: (0,) * a.ndim)
    ma, g, ginv = cst["ma"].astype(BF16), cst["g"].astype(BF16), cst["ginv"].astype(BF16)
    mir, mii = cst["minv_r"].astype(BF16), cst["minv_i"].astype(BF16)
    tw = cst["tw"]
    u_spec = lambda k: pl.BlockSpec((2, C, rows, LANES), lambda j, p, k=k: (p, j + k * J, 0, 0))
    par_spec = lambda k: pl.BlockSpec((4, C, LANES), lambda j, p, k=k: (0, j + k * J, 0))
    return pl.pallas_call(
        _hyena_body,
        grid=(J, B // 2),
        in_specs=[
            u_spec(0), u_spec(1), u_spec(2), par_spec(0), par_spec(1), par_spec(2),
            pl.BlockSpec((2, C, LANES), lambda j, p: (0, j, 0)),
            pl.BlockSpec((2, C, FFT_N1, 2 * LANES), lambda j, p: (0, j, 0, 0)),
            full(ma), full(tw), full(g), full(ginv), full(mir), full(mii),
        ],
        out_specs=pl.BlockSpec((2, C, rows * LANES), lambda j, p: (p, j, 0)),
        out_shape=jax.ShapeDtypeStruct((B, D_HYENA, rows * LANES), F32),
        scratch_shapes=[
            pltpu.VMEM((C * FFT_N1, 2 * LANES), BF16),
            pltpu.VMEM((C * FFT_N1, 2 * LANES), F32),
            pltpu.VMEM((2, C, rows, LANES), F32),
            pltpu.VMEM((2, C, rows, LANES), F32),
        ],
        compiler_params=_cparams(("parallel", "arbitrary")),
        name="hyena",
    )(u4, u4, u4, par_u, par_u, par_u, fb, hspec, ma, tw, g, ginv, mir, mii)


def _dot3(a, b):
    ah = a.astype(BF16)
    al = (a - ah.astype(F32)).astype(BF16)
    bh = b.astype(BF16)
    bl = (b - bh.astype(F32)).astype(BF16)
    return jnp.dot(jnp.concatenate([ah, al, ah], axis=1), jnp.concatenate([bh, bh, bl], axis=0),
                   preferred_element_type=F32)


def _filtgen_body(zt_ref, w1_ref, b1_ref, f1_ref, w2_ref, b2_ref, f2_ref, w3f_ref, w3b_ref, ad_ref, tt_ref,
                  o_ref, hid_ref):
    L = hid_ref.shape[1] // 2

    @pl.when(pl.program_id(0) == 0)
    def _():
        h1 = jnp.sin(f1_ref[...] * (_dot3(w1_ref[...], zt_ref[...]) + b1_ref[...]))
        hid_ref[...] = jnp.sin(f2_ref[...] * (_dot3(w2_ref[...], h1) + b2_ref[...]))

    ad = ad_ref[...]
    hf = _dot3(w3f_ref[...], hid_ref[:, :L]) * jnp.exp(-ad * tt_ref[:, :L])
    hb = _dot3(w3b_ref[...], hid_ref[:, L:]) * jnp.exp(-ad * tt_ref[:, L:])
    hf = hf / (jnp.sum(jnp.abs(hf), axis=-1, keepdims=True) + EPS)
    hb = hb / (jnp.sum(jnp.abs(hb), axis=-1, keepdims=True) + EPS)
    first = lax.broadcasted_iota(jnp.int32, hb.shape, 1) == 0
    cf = hf + jnp.where(first, hb, 0.0)
    cb = jnp.where(first, 0.0, hb)
    chunks = [c[:, a * LANES:(a + 1) * LANES] for c in (cf, cb) for a in range(L // LANES)]
    o_ref[...] = pltpu.einshape("arl->ral", jnp.stack(chunks, axis=0))


def _filter_gen(L, w_f1, b_f1, freq1, w_f2, b_f2, freq2, w_f3):
    bands = (FILTER_EMB - 1) // 2
    t = jnp.linspace(0.0, 1.0, L, dtype=F32)[:, None]
    w = (2.0 * math.pi / L) * jnp.arange(L, dtype=F32)[:, None]
    f = jnp.linspace(1e-4, bands - 1, bands, dtype=F32)[None]
    zf = f * w
    z = jnp.concatenate([t, jnp.cos(zf), -jnp.sin(zf)], axis=-1)
    back = lambda a: jnp.roll(a[::-1], 1, axis=0)
    kpad = 48
    zt = jnp.pad(jnp.concatenate([z, back(z)], axis=0).T, ((0, kpad - FILTER_EMB), (0, 0)))
    tt = jnp.concatenate([t, back(t)], axis=0).T
    w1t = jnp.pad(w_f1.T, ((0, 0), (0, kpad - FILTER_EMB)))
    w3 = w_f3.reshape(-1, 2, 2, D_HYENA)
    w3f = jnp.transpose(w3[:, :, 0], (1, 2, 0)).reshape(2 * D_HYENA, -1)
    w3b = jnp.transpose(w3[:, :, 1], (1, 2, 0)).reshape(2 * D_HYENA, -1)
    max_decay = math.log(DECAY_TARGET) / FAST_DECAY_PCT
    min_decay = math.log(DECAY_TARGET) / SLOW_DECAY_PCT
    deltas = jnp.linspace(min_decay, max_decay, D_HYENA, dtype=F32)
    ad = jnp.tile(jnp.abs(deltas), 2)[:, None]
    col = lambda v: v[:, None]
    R = 128
    n_rows = 2 * D_HYENA
    full = lambda a: pl.BlockSpec(a.shape, lambda i: (0,) * a.ndim)
    rows = lambda a: pl.BlockSpec((R, a.shape[1]), lambda i: (i, 0))
    args = (zt, w1t, col(b_f1), col(freq1), w_f2.T, col(b_f2), col(freq2))
    return pl.pallas_call(
        _filtgen_body,
        grid=(n_rows // R,),
        in_specs=[full(a) for a in args] + [rows(w3f), rows(w3b), rows(ad), full(tt)],
        out_specs=pl.BlockSpec((R, 2 * L // LANES, LANES), lambda i: (i, 0, 0)),
        out_shape=jax.ShapeDtypeStruct((n_rows, 2 * L // LANES, LANES), F32),
        scratch_shapes=[pltpu.VMEM((w_f2.shape[1], 2 * L), F32)],
        compiler_params=_cparams(("arbitrary",)),
        name="filter_gen",
    )(*args, w3f, w3b, ad, tt)


def _route_lanes(lg):
    neg = -1e30
    lane = lax.broadcasted_iota(jnp.int32, lg.shape, 1)
    gmask = lane < N_GROUPS
    gl = jnp.where(gmask, lg, neg)
    gm = jnp.max(gl, axis=-1, keepdims=True)
    gsum = jnp.sum(jnp.where(gmask, jnp.exp(gl - gm), 0.0), axis=-1, keepdims=True)
    g_top = 1.0 / gsum
    g_sel = jnp.min(jnp.where(gl == gm, lane, LANES), axis=-1, keepdims=True)
    lo = N_GROUPS + EXPERTS_PER_GROUP * g_sel
    el = jnp.where((lane >= lo) & (lane < lo + EXPERTS_PER_GROUP), lg, neg)
    m1 = jnp.max(el, axis=-1, keepdims=True)
    i1 = jnp.min(jnp.where(el == m1, lane, LANES), axis=-1, keepdims=True)
    el2 = jnp.where(lane == i1, neg, el)
    m2 = jnp.max(el2, axis=-1, keepdims=True)
    i2 = jnp.min(jnp.where(el2 == m2, lane, LANES), axis=-1, keepdims=True)
    d = jnp.exp(m2 - m1)
    p1 = 1.0 / (1.0 + d)
    p2 = d / (1.0 + d)
    e1 = (i1 - N_GROUPS).astype(F32)
    e2 = (i2 - N_GROUPS).astype(F32)
    return jnp.where(lane == 0, e1, jnp.where(lane == 1, e2, jnp.where(lane == 2, g_top * p1,
                     jnp.where(lane == 3, g_top * p2, 0.0))))


def _pack_bf16_halves(a):
    w = a.shape[1] // 2
    bits = pltpu.bitcast(a.astype(BF16).astype(F32), jnp.uint32)
    return (bits[:, :w] >> 16) | (bits[:, w:] & jnp.uint32(0xFFFF0000))


def _unpack_bf16_halves(wd):
    lo = pltpu.bitcast(wd << 16, F32)
    hi = pltpu.bitcast(wd & jnp.uint32(0xFFFF0000), F32)
    return jnp.concatenate([lo, hi], axis=1)


def _store_row_tiles(ref, packed):
    chunks = jnp.stack([packed[:, j * LANES:(j + 1) * LANES] for j in range(ROW_CHUNKS)], axis=0)
    ref[...] = pltpu.einshape("jrl->rjl", chunks)


def _load_row_tiles(ref):
    chunks = pltpu.einshape("rjl->jrl", ref[...])
    return jnp.concatenate([chunks[j] for j in range(ROW_CHUNKS)], axis=1)


def _outproj_body(ya_ref, yh_ref, x_ref, ga_ref, gh_ref, wo_ref, bd_ref, gm_ref, wrh_ref, wrl_ref, brt_ref,
                  x1_ref, h2_ref, rt_ref, rtt_ref):
    ya = ya_ref[...]
    yan = ya * lax.rsqrt(_group_sumsq(ya, bd_ref[...]) * (1.0 / HEAD_DIM) + EPS) * ga_ref[...]
    yh = yh_ref[...]
    tm = yh.shape[1]
    yh3 = yh.reshape(D_HYENA // HYENA_HEAD, HYENA_HEAD, tm)
    ms = jnp.mean(yh3 * yh3, axis=1, keepdims=True)
    yhn = (yh3 * lax.rsqrt(ms + EPS)).reshape(D_HYENA, tm) * gh_ref[...]
    mix = (jnp.dot(yan.astype(BF16), wo_ref[:D_ATTN, :], preferred_element_type=F32)
           + jnp.dot(yhn.T.astype(BF16), wo_ref[D_ATTN:, :], preferred_element_type=F32))
    x1 = x_ref[...] + mix
    x1_ref[...] = x1
    h2 = x1 * lax.rsqrt(jnp.mean(x1 * x1, axis=-1, keepdims=True) + EPS) * gm_ref[...]
    _store_row_tiles(h2_ref, _pack_bf16_halves(h2))
    hi = h2.astype(BF16)
    lo = (h2 - hi.astype(F32)).astype(BF16)
    hw = jnp.dot(hi, jnp.concatenate([wrh_ref[...], wrl_ref[...]], axis=1),
                 preferred_element_type=F32)
    lg = hw[:, :LANES] + hw[:, LANES:] + jnp.dot(lo, wrh_ref[...], preferred_element_type=F32) + brt_ref[...]
    route = _route_lanes(lg)
    rt_ref[...] = route
    rtt_ref[...] = route.T[:8]


def _outproj(ya, yht, x, ga, gh, wo, bd, gm, wrh, wrl, brt):
    B, S, D = x.shape
    tm = TM_PROJ
    full = lambda a: pl.BlockSpec(a.shape, lambda b, i: (0,) * a.ndim)
    return pl.pallas_call(
        _outproj_body,
        grid=(B, S // tm),
        in_specs=[
            pl.BlockSpec((None, tm, D_ATTN), lambda b, i: (b, i, 0)),
            pl.BlockSpec((None, D_HYENA, tm), lambda b, i: (b, 0, i)),
            pl.BlockSpec((None, tm, D), lambda b, i: (b, i, 0)),
            full(ga), full(gh), full(wo), full(bd), full(gm), full(wrh), full(wrl), full(brt),
        ],
        out_specs=[
            pl.BlockSpec((None, tm, D), lambda b, i: (b, i, 0)),
            pl.BlockSpec((None, tm, ROW_CHUNKS, LANES), lambda b, i: (b, i, 0, 0)),
            pl.BlockSpec((None, tm, LANES), lambda b, i: (b, i, 0)),
            pl.BlockSpec((None, 8, tm), lambda b, i: (b, 0, i)),
        ],
        out_shape=[
            jax.ShapeDtypeStruct((B, S, D), F32),
            jax.ShapeDtypeStruct((B, S, ROW_CHUNKS, LANES), jnp.uint32),
            jax.ShapeDtypeStruct((B, S, LANES), F32),
            jax.ShapeDtypeStruct((B, 8, S), F32),
        ],
        compiler_params=_cparams(("parallel", "parallel")),
        name="outproj",
    )(ya, yht, x, ga, gh, wo, bd, gm, wrh, wrl, brt)


def _moe_body(be_ref, ra_ref, nlive_ref, h2_hbm, wg_ref, wu_ref, wd_ref, y_hbm,
              wg_s, wu_s, wd_s, xbuf, ybuf, zbuf, sem_in, sem_out, sem_z, *, n_tok, n_rows):
    i = pl.program_id(0)
    nb = nlive_ref[0]
    T = xbuf.shape[1]
    slot = i % 2

    def issue_gathers(blk, sl):
        for r in range(T):
            tok = ra_ref[blk * T + r] & (n_tok - 1)
            pltpu.make_async_copy(h2_hbm.at[tok], xbuf.at[sl, r], sem_in.at[sl]).start()

    def issue_scatters(blk, sl, spare):
        for r in range(T):
            dst = jnp.where(spare, n_rows + r, ra_ref[blk * T + r])
            pltpu.make_async_copy(ybuf.at[sl, r], y_hbm.at[dst], sem_out.at[sl]).start()

    def block_in_wait(sl):
        pltpu.make_async_copy(h2_hbm.at[pl.ds(0, T)], xbuf.at[sl], sem_in.at[sl]).wait()

    def block_out_wait(sl):
        pltpu.make_async_copy(ybuf.at[sl], y_hbm.at[pl.ds(0, T)], sem_out.at[sl]).wait()

    @pl.when(i == 0)
    def _():
        ybuf[...] = jnp.zeros(ybuf.shape, ybuf.dtype)
        zbuf[...] = jnp.zeros(zbuf.shape, zbuf.dtype)
        issue_gathers(0, 0)

    @pl.when(i >= nb)
    def _():
        fill = pltpu.make_async_copy(zbuf, y_hbm.at[pl.ds(i * T, T)], sem_z.at[0])
        fill.start()
        fill.wait()

    prev = be_ref[jnp.maximum(i - 1, 0)]

    @pl.when((i == 0) | (be_ref[i] != prev))
    def _():
        wg_s[...] = wg_ref[...].astype(BF16)
        wu_s[...] = wu_ref[...].astype(BF16)
        wd_s[...] = wd_ref[...].astype(BF16)

    @pl.when(i < nb)
    def _():
        block_in_wait(slot)
        x = _unpack_bf16_halves(_load_row_tiles(xbuf.at[slot])).astype(BF16)
        issue_gathers(jnp.minimum(i + 1, nb - 1), 1 - slot)
        issue_scatters(jnp.maximum(i - 1, 0), 1 - slot, i == 0)
        a = jnp.dot(x, wg_s[...], preferred_element_type=F32)
        b = jnp.dot(x, wu_s[...], preferred_element_type=F32)
        hmid = (a * jax.nn.sigmoid(a)) * b
        y = _pack_bf16_halves(jnp.dot(hmid.astype(BF16), wd_s[...], preferred_element_type=F32))

        @pl.when(i >= 1)
        def _():
            block_out_wait(slot)

        _store_row_tiles(ybuf.at[slot], y)

        @pl.when(i == nb - 1)
        def _():
            issue_scatters(i, slot, False)
            block_in_wait(1 - slot)
            block_out_wait(1 - slot)
            block_out_wait(slot)


def _moe_experts(block_e, row_a, n_live, h2p, w_gate, w_up, w_down):
    n_tok = h2p.shape[0]
    n_rows = row_a.shape[0]
    row = h2p.shape[1:]
    D = w_gate.shape[1]
    T = TB_MOE
    assert n_tok & (n_tok - 1) == 0
    grid_spec = pltpu.PrefetchScalarGridSpec(
        num_scalar_prefetch=3,
        grid=(row_a.shape[0] // T,),
        in_specs=[
            pl.BlockSpec(memory_space=pl.ANY),
            pl.BlockSpec((None, D, D_EXPERT), lambda i, be, ra, nl: (be[i], 0, 0)),
            pl.BlockSpec((None, D, D_EXPERT), lambda i, be, ra, nl: (be[i], 0, 0)),
            pl.BlockSpec((None, D_EXPERT, D), lambda i, be, ra, nl: (be[i], 0, 0)),
        ],
        out_specs=pl.BlockSpec(memory_space=pl.ANY),
        scratch_shapes=[
            pltpu.VMEM((D, D_EXPERT), BF16), pltpu.VMEM((D, D_EXPERT), BF16), pltpu.VMEM((D_EXPERT, D), BF16),
            pltpu.VMEM((2, T) + row, jnp.uint32), pltpu.VMEM((2, T) + row, jnp.uint32),
            pltpu.VMEM((T,) + row, jnp.uint32),
            pltpu.SemaphoreType.DMA((2,)), pltpu.SemaphoreType.DMA((2,)), pltpu.SemaphoreType.DMA((1,)),
        ],
    )
    return pl.pallas_call(
        functools.partial(_moe_body, n_tok=n_tok, n_rows=n_rows),
        grid_spec=grid_spec,
        out_shape=jax.ShapeDtypeStruct((n_rows + T,) + row, jnp.uint32),
        compiler_params=_cparams(("arbitrary",)),
        name="moe_experts",
    )(block_e, row_a, n_live, h2p, w_gate, w_up, w_down)


def _dispatch(e_flat, N):
    T = TB_MOE
    NK = N * TOP_K
    experts = jnp.arange(N_EXPERTS, dtype=jnp.int32)
    order = jnp.argsort(e_flat).astype(jnp.int32)
    onehot = (e_flat[:, None] == experts[None]).astype(jnp.int32)
    counts = jnp.sum(onehot, axis=0)
    ends = jnp.cumsum(counts)
    starts = ends - counts
    padded = (counts + T - 1) // T * T
    pends = jnp.cumsum(padded)
    pstarts = pends - padded
    n_rows = -(-(NK + N_EXPERTS * (T - 1)) // T) * T
    n_blocks = n_rows // T
    blk_start = jnp.arange(n_blocks, dtype=jnp.int32) * T
    block_e = jnp.clip(jnp.sum((pends[None, :] <= blk_start[:, None]).astype(jnp.int32), axis=1),
                       0, N_EXPERTS - 1)
    oh_b = (block_e[:, None] == experts[None]).astype(jnp.int32)
    base = jnp.sum(oh_b * (starts - pstarts)[None], axis=1) + blk_start
    end_b = jnp.sum(oh_b * ends[None], axis=1)
    lane = jnp.arange(T, dtype=jnp.int32)[None]
    src = base[:, None] + lane
    pad_id = NK + blk_start[:, None] + lane - end_b[:, None]
    row_a = jnp.where(src < end_b[:, None], order[jnp.clip(src, 0, NK - 1)], pad_id)
    n_live = (pends[-1:] // T).astype(jnp.int32)
    return block_e.astype(jnp.int32), row_a.reshape(n_rows).astype(jnp.int32), n_live


def _final_body(x1_ref, y0_ref, y1_ref, rt_ref, p_ref, gp_ref, wg_ref, bg_ref, wp_ref, gf_ref, o_ref):
    w0 = rt_ref[:, 2:3]
    w1 = rt_ref[:, 3:4]
    y0 = _unpack_bf16_halves(_load_row_tiles(y0_ref))
    y1 = _unpack_bf16_halves(_load_row_tiles(y1_ref))
    x2 = x1_ref[...] + (y0 * w0 + y1 * w1)
    hp = x2 * lax.rsqrt(jnp.mean(x2 * x2, axis=-1, keepdims=True) + EPS) * gp_ref[...]
    gate = jax.nn.sigmoid(jnp.dot(hp.astype(BF16), wg_ref[...], preferred_element_type=F32) + bg_ref[...])
    pe = jnp.dot(p_ref[...].astype(BF16), wp_ref[...], preferred_element_type=F32)
    x3 = x2 + pe * gate
    o_ref[...] = x3 * lax.rsqrt(jnp.mean(x3 * x3, axis=-1, keepdims=True) + EPS) * gf_ref[...]


def _final(x1, y, route, p, gp, wg, bg, wp, gf):
    N, D = x1.shape
    tm = TM_PROJ
    row = lambda w: pl.BlockSpec((tm, w), lambda i: (i, 0))
    full = lambda a: pl.BlockSpec(a.shape, lambda i: (0,) * a.ndim)
    y0, y1 = y, y
    return pl.pallas_call(
        _final_body,
        grid=(N // tm,),
        in_specs=[row(D), pl.BlockSpec((tm, ROW_CHUNKS, LANES), lambda i: (i, 0, 0)),
                  pl.BlockSpec((tm, ROW_CHUNKS, LANES), lambda i: (i + N // tm, 0, 0)),
                  row(LANES), row(p.shape[1]),
                  full(gp), full(wg), full(bg), full(wp), full(gf)],
        out_specs=row(D),
        out_shape=jax.ShapeDtypeStruct((N, D), F32),
        compiler_params=_cparams(("parallel",)),
        name="ple_final",
    )(x1, y0, y1, route, p, gp, wg, bg, wp, gf)


def kernel(x, p, g_mix, w_in, q_gain, k_gain, conv_w, conv_b, w_f1, b_f1, freq1, w_f2, b_f2, freq2, w_f3, filt_bias, g_attn_out, g_hyena_out, w_out, g_moe, w_group, b_group, w_router, b_router, w_gate, w_up, w_down, g_ple, w_ple_gate, b_ple_gate, w_ple, g_final):
    B, S, D = x.shape
    N = B * S
    assert p.shape[0] == 1 and S == (FFT_N1 // 2) * FFT_N2 and B % 2 == 0
    i = 0
    cst = _dft_constants()
    cos, sin = _rope_tables(S)
    bd = _block_diag_ones(D_ATTN, HEAD_DIM)

    n_qkv = D_ATTN + 2 * D_KV
    wqkv = w_in[i][:, :n_qkv].astype(BF16)
    wut = w_in[i][:, n_qkv:].T.astype(BF16)
    q, kw, vw, ut = _inproj(x, g_mix[i][None], wqkv, wut, bd,
                            jnp.tile(q_gain[i], N_HEADS)[None], jnp.tile(k_gain[i], N_KV_HEADS)[None], cos, sin)

    ya = _attention(q, kw, vw)

    circ = _filter_gen(S, w_f1[i], b_f1[i], freq1[i], w_f2[i], b_f2[i], freq2[i], w_f3[i])
    hspec = _filter_fft(circ, cst)
    hspec = hspec.reshape(2, D_HYENA, FFT_N1, 2 * LANES)
    du = ut.shape[1]
    u4 = ut.reshape(B, du, S // LANES, LANES)
    par_u = jnp.broadcast_to(jnp.concatenate([conv_w[i], conv_b[i][None]], 0)[:, :, None], (4, du, LANES))
    fb = jnp.broadcast_to(filt_bias[i][:, :, None], (2, D_HYENA, LANES))
    yht = _hyena(u4, par_u, fb, hspec, cst)

    wrt = jnp.zeros((D, LANES), F32).at[:, :N_GROUPS].set(w_group[i]).at[:, N_GROUPS:N_GROUPS + N_EXPERTS].set(w_router[i])
    brt = jnp.zeros((1, LANES), F32).at[0, :N_GROUPS].set(b_group[i]).at[0, N_GROUPS:N_GROUPS + N_EXPERTS].set(b_router[i])
    wrh = wrt.astype(BF16)
    wrl = (wrt - wrh.astype(F32)).astype(BF16)
    x1, h2, route, route_t = _outproj(ya, yht, x, g_attn_out[i][None], g_hyena_out[i][:, None],
                                      w_out[i].astype(BF16), bd, g_moe[i][None], wrh, wrl, brt)

    e_flat = jnp.transpose(route_t[:, :TOP_K], (1, 0, 2)).reshape(TOP_K * N).astype(jnp.int32)
    block_e, row_a, n_live = _dispatch(e_flat, N)
    y = _moe_experts(block_e, row_a, n_live, h2.reshape(N, ROW_CHUNKS, LANES), w_gate[i], w_up[i], w_down[i])

    out = _final(x1.reshape(N, D), y, route.reshape(N, LANES), p[i].reshape(N, -1), g_ple[i][None],
                 w_ple_gate[i].astype(BF16), b_ple_gate[i][None], w_ple[i].astype(BF16), g_final[None])
    return out.reshape(B, S, D)
```

```python
import functools
import math

import numpy as np
import jax
import jax.numpy as jnp
from jax import lax
from jax.experimental import pallas as pl
from jax.experimental.pallas import tpu as pltpu

F32 = jnp.float32
BF16 = jnp.bfloat16

D_MODEL = 1024
EPS = 1e-6
GRID_W = 64
N_HEADS = 8
N_KV_HEADS = 2
HEAD_DIM = 64
D_ATTN = N_HEADS * HEAD_DIM
D_KV = N_KV_HEADS * HEAD_DIM
ROPE_THETA = 10000.0
D_HYENA = 512
HYENA_HEAD = 64
FILTER_EMB = 33
FAST_DECAY_PCT = 0.3
SLOW_DECAY_PCT = 1.5
DECAY_TARGET = 1e-2
N_GROUPS = 4
EXPERTS_PER_GROUP = 8
N_EXPERTS = N_GROUPS * EXPERTS_PER_GROUP
TOP_K = 2
D_EXPERT = 512

LANES = 128
MXU_TILE = 256
FFT_N1 = 64
FFT_N2 = 128
VMEM_LIMIT = 56 * 1024 * 1024

TM_PROJ = 512
TQ_ATTN = 256
C_HY = 32
C_FILT = 128
ROW_CHUNKS = D_MODEL // 2 // LANES
SEQ_UNROLL = 32
TB_MOE = 256


def _cparams(sem):
    return pltpu.CompilerParams(dimension_semantics=sem, vmem_limit_bytes=VMEM_LIMIT)


def _rope_tables(S):
    half = HEAD_DIM // 2
    t = jnp.arange(S, dtype=F32)
    r_idx = jnp.floor(t / GRID_W)
    c_idx = t - r_idx * GRID_W
    inv = ROPE_THETA ** (-jnp.arange(0, half, 2, dtype=F32) / half)
    ang_r = r_idx[:, None] * inv[None]
    ang_c = c_idx[:, None] * inv[None]
    cos_h = jnp.concatenate([jnp.cos(ang_r), jnp.cos(ang_r), jnp.cos(ang_c), jnp.cos(ang_c)], axis=-1)
    sin_h = jnp.concatenate([-jnp.sin(ang_r), jnp.sin(ang_r), -jnp.sin(ang_c), jnp.sin(ang_c)], axis=-1)
    return jnp.tile(cos_h, (1, 2)), jnp.tile(sin_h, (1, 2))


def _dft_constants():
    n1, n2 = FFT_N1, FFT_N2
    n = n1 * n2
    a = np.arange(n1)
    ang = 2.0 * np.pi * np.outer(a, a) / n1
    far, fai = np.cos(ang), -np.sin(ang)
    hlf = n1 // 2
    ma = np.block([[far[:, :hlf], -fai[:, :hlf]], [fai[:, :hlf], far[:, :hlf]]])
    maf = np.concatenate([far, fai], axis=0)
    b = np.arange(n2)
    angt = 2.0 * np.pi * np.outer(a, b) / n
    tw = np.concatenate([np.cos(angt), -np.sin(angt)], axis=1)
    angb = 2.0 * np.pi * np.outer(b, b) / n2
    fbr, fbi = np.cos(angb), -np.sin(angb)
    g = np.block([[fbr, fbi], [-fbi, fbr]])
    ginv = np.block([[fbr, -fbi], [fbi, fbr]])
    minv_r = np.concatenate([far[:hlf], -fai[:hlf]], axis=0) / n
    minv_i = np.concatenate([fai[:hlf], far[:hlf]], axis=0) / n
    f = lambda m: jnp.asarray(m.astype(np.float32))
    return dict(ma=f(ma), maf=f(maf), tw=f(tw), g=f(g), ginv=f(ginv), minv_r=f(minv_r), minv_i=f(minv_i))


def _block_diag_ones(width, group):
    i = np.arange(width) // group
    return jnp.asarray((i[:, None] == i[None, :]).astype(np.float32)).astype(BF16)


def _group_sumsq(a, bd):
    sq = a * a
    hi = sq.astype(BF16)
    lo = (sq - hi.astype(F32)).astype(BF16)
    w = min(a.shape[-1], MXU_TILE)
    return jnp.concatenate(
        [jnp.dot(hi[:, c:c + w], bd[c:c + w, c:c + w], preferred_element_type=F32)
         + jnp.dot(lo[:, c:c + w], bd[c:c + w, c:c + w], preferred_element_type=F32)
         for c in range(0, a.shape[-1], w)], axis=-1)


def _head_norm_rope(a, gain, bd, cos, sin):
    width = a.shape[-1]
    n = a * lax.rsqrt(_group_sumsq(a, bd) * (1.0 / HEAD_DIM) + EPS) * gain
    rep = width // LANES
    if rep > 1:
        cos = jnp.concatenate([cos] * rep, axis=-1)
        sin = jnp.concatenate([sin] * rep, axis=-1)
    fwd = pltpu.roll(n, width - 16, 1)
    bwd = pltpu.roll(n, 16, 1)
    lane = lax.broadcasted_iota(jnp.int32, n.shape, 1)
    sw = jnp.where((lane % 32) < 16, fwd, bwd)
    return n * cos + sw * sin


def _inproj_body(x_ref, g_ref, wqkv_ref, wu_ref, bd_ref, qg_ref, kg_ref, cos_ref, sin_ref,
                 q_ref, kw_ref, vw_ref, ut_ref):
    x = x_ref[...]
    h = x * lax.rsqrt(jnp.mean(x * x, axis=-1, keepdims=True) + EPS) * g_ref[...]
    hb = h.astype(BF16)
    qkv = jnp.dot(hb, wqkv_ref[...], preferred_element_type=F32)
    cos = cos_ref[...]
    sin = sin_ref[...]
    bd = bd_ref[...]
    q = _head_norm_rope(qkv[:, :D_ATTN], qg_ref[...], bd, cos, sin)
    q_ref[...] = (q * (HEAD_DIM ** -0.5 * math.log2(math.e))).astype(BF16)
    k = _head_norm_rope(qkv[:, D_ATTN:D_ATTN + D_KV], kg_ref[...], bd[:D_KV, :D_KV], cos, sin)
    kt = k.T.astype(BF16)
    zero = jnp.zeros((HEAD_DIM, kt.shape[1]), BF16)
    for h in range(N_KV_HEADS):
        kh = kt[h * HEAD_DIM:(h + 1) * HEAD_DIM]
        kw_ref[h, 0, :HEAD_DIM] = kh
        kw_ref[h, 0, HEAD_DIM:] = zero
        kw_ref[h, 1, :HEAD_DIM] = zero
        kw_ref[h, 1, HEAD_DIM:] = kh
    v = qkv[:, D_ATTN + D_KV:]
    vr = pltpu.roll(v, HEAD_DIM, 1)
    first = lax.broadcasted_iota(jnp.int32, v.shape, 1) < HEAD_DIM
    vw_ref[0, 0] = jnp.where(first, v, 1.0).astype(BF16)
    vw_ref[0, 1] = jnp.where(first, 1.0, vr).astype(BF16)
    vw_ref[1, 0] = jnp.where(first, vr, 1.0).astype(BF16)
    vw_ref[1, 1] = jnp.where(first, 1.0, v).astype(BF16)
    ut_ref[...] = lax.dot_general(wu_ref[...], hb, (((1,), (1,)), ((), ())),
                                  preferred_element_type=F32)


def _inproj(x, g_mix, wqkv, wut, bd, qg, kg, cos, sin):
    B, S, D = x.shape
    tm = TM_PROJ
    du = wut.shape[0]
    full = lambda shape: pl.BlockSpec(shape, lambda b, i: (0,) * len(shape))
    return pl.pallas_call(
        _inproj_body,
        grid=(B, S // tm),
        in_specs=[
            pl.BlockSpec((None, tm, D), lambda b, i: (b, i, 0)),
            full((1, D)), full(wqkv.shape), full(wut.shape), full(bd.shape),
            full((1, D_ATTN)), full((1, D_KV)),
            pl.BlockSpec((tm, LANES), lambda b, i: (i, 0)),
            pl.BlockSpec((tm, LANES), lambda b, i: (i, 0)),
        ],
        out_specs=[
            pl.BlockSpec((None, tm, D_ATTN), lambda b, i: (b, i, 0)),
            pl.BlockSpec((None, N_KV_HEADS, 2, LANES, tm), lambda b, i: (b, 0, 0, 0, i)),
            pl.BlockSpec((None, N_KV_HEADS, 2, tm, LANES), lambda b, i: (b, 0, 0, i, 0)),
            pl.BlockSpec((None, du, tm), lambda b, i: (b, 0, i)),
        ],
        out_shape=[
            jax.ShapeDtypeStruct((B, S, D_ATTN), BF16),
            jax.ShapeDtypeStruct((B, N_KV_HEADS, 2, LANES, S), BF16),
            jax.ShapeDtypeStruct((B, N_KV_HEADS, 2, S, LANES), BF16),
            jax.ShapeDtypeStruct((B, du, S), F32),
        ],
        compiler_params=_cparams(("parallel", "parallel")),
        name="inproj",
    )(x, g_mix, wqkv, wut, bd, qg, kg, cos, sin)


def _attn_body(q_ref, kw_ref, vw_ref, o_ref):

    def one_head(q, kw, vw):
        s = jnp.dot(q, kw, preferred_element_type=F32)
        m = jnp.max(s, axis=-1, keepdims=True)
        p = jnp.exp2(s - m).astype(BF16)
        return jnp.dot(p, vw, preferred_element_type=F32)

    for pair in range(D_ATTN // LANES):
        h = pair // (N_HEADS // N_KV_HEADS // 2)
        q = q_ref[:, pair * LANES:(pair + 1) * LANES]
        oe = one_head(q, kw_ref[h, 0], vw_ref[h, 0])
        oo = one_head(q, kw_ref[h, 1], vw_ref[h, 1])
        first = lax.broadcasted_iota(jnp.int32, oe.shape, 1) < HEAD_DIM
        num = jnp.where(first, oe, oo)
        den = jnp.where(first, pltpu.roll(oe, HEAD_DIM, 1), pltpu.roll(oo, HEAD_DIM, 1))
        o_ref[:, pair * LANES:(pair + 1) * LANES] = num / den


def _attention(q, kw, vw):
    B, S, _ = q.shape
    tq = TQ_ATTN
    return pl.pallas_call(
        _attn_body,
        grid=(B, S // tq),
        in_specs=[
            pl.BlockSpec((None, tq, D_ATTN), lambda b, i: (b, i, 0)),
            pl.BlockSpec((None, N_KV_HEADS, 2, LANES, S), lambda b, i: (b, 0, 0, 0, 0)),
            pl.BlockSpec((None, N_KV_HEADS, 2, S, LANES), lambda b, i: (b, 0, 0, 0, 0)),
        ],
        out_specs=pl.BlockSpec((None, tq, D_ATTN), lambda b, i: (b, i, 0)),
        out_shape=jax.ShapeDtypeStruct((B, S, D_ATTN), F32),
        compiler_params=_cparams(("parallel", "arbitrary")),
        name="attention",
    )(q, kw, vw)


def _fwd_twiddle_store(y, tw_ref, s1_ref, row0):
    yr, yi = y[:FFT_N1], y[FFT_N1:]
    twr, twi = tw_ref[:, :LANES], tw_ref[:, LANES:]
    s1_ref[pl.ds(row0, FFT_N1), :LANES] = (yr * twr - yi * twi).astype(BF16)
    s1_ref[pl.ds(row0, FFT_N1), LANES:] = (yr * twi + yi * twr).astype(BF16)


def _filtfft_body(x_ref, maf_ref, tw_ref, g_ref, h_ref, s1_ref):
    C = x_ref.shape[0]

    def step_a(c, carry):
        y = jnp.dot(maf_ref[...], x_ref[c].astype(BF16), preferred_element_type=F32)
        _fwd_twiddle_store(y, tw_ref, s1_ref, pl.multiple_of(c * FFT_N1, FFT_N1))
        return carry

    lax.fori_loop(0, C, step_a, 0, unroll=SEQ_UNROLL)
    z = jnp.dot(s1_ref[...], g_ref[...], preferred_element_type=F32)
    h_ref[...] = z.reshape(C, FFT_N1, 2 * LANES)


def _short_conv(x, par_ref, c):
    rows, lanes = x.shape
    a_i = lax.broadcasted_iota(jnp.int32, x.shape, 0)
    b_i = lax.broadcasted_iota(jnp.int32, x.shape, 1)
    l1 = pltpu.roll(x, 1, 1)
    l2 = pltpu.roll(l1, 1, 0)
    prev = jnp.where(b_i == 0, l2, l1)
    prev = jnp.where((a_i == 0) & (b_i == 0), 0.0, prev)
    r1 = pltpu.roll(x, lanes - 1, 1)
    r2 = pltpu.roll(r1, rows - 1, 0)
    nxt = jnp.where(b_i == lanes - 1, r2, r1)
    nxt = jnp.where((a_i == rows - 1) & (b_i == lanes - 1), 0.0, nxt)
    w0 = par_ref[0, pl.ds(c, 1), :]
    w1 = par_ref[1, pl.ds(c, 1), :]
    w2 = par_ref[2, pl.ds(c, 1), :]
    cb = par_ref[3, pl.ds(c, 1), :]
    return cb + prev * w0 + x * w1 + nxt * w2


def _hyena_body(v_ref, x1_ref, x2_ref, pv_ref, p1_ref, p2_ref, fb_ref, h_ref,
                ma_ref, tw_ref, g_ref, ginv_ref, mir_ref, mii_ref,
                o_ref, s1_ref, s2_ref, vc_ref, z1_ref):
    C = v_ref.shape[1]
    half = FFT_N1 // 2

    def spectral(order):
        z = jnp.dot(s1_ref[...], g_ref[...], preferred_element_type=F32)
        hs = h_ref[order].reshape(C * FFT_N1, 2 * LANES)
        zr, zi = z[:, :LANES], z[:, LANES:]
        hr, hi = hs[:, :LANES], hs[:, LANES:]
        pb = jnp.concatenate([zr * hr - zi * hi, zr * hi + zi * hr], axis=1).astype(BF16)
        s2_ref[...] = jnp.dot(pb, ginv_ref[...], preferred_element_type=F32)

    def inv_a(c):
        row0 = pl.multiple_of(c * FFT_N1, FFT_N1)
        y = s2_ref[pl.ds(row0, FFT_N1), :]
        yr, yi = y[:, :LANES], y[:, LANES:]
        twr, twi = tw_ref[:, :LANES], tw_ref[:, LANES:]
        ur = (yr * twr + yi * twi).astype(BF16)
        ui = (yi * twr - yr * twi).astype(BF16)
        out = (jnp.dot(mir_ref[...], ur, preferred_element_type=F32)
               + jnp.dot(mii_ref[...], ui, preferred_element_type=F32))
        return out[:half], out[half:]

    def fwd_a(c, xr, xi):
        xs = jnp.concatenate([xr, xi], axis=0).astype(BF16)
        y = jnp.dot(ma_ref[...], xs, preferred_element_type=F32)
        _fwd_twiddle_store(y, tw_ref, s1_ref, pl.multiple_of(c * FFT_N1, FFT_N1))

    def pass1_a(c, carry):
        vr = _short_conv(v_ref[0, c], pv_ref, c)
        vi = _short_conv(v_ref[1, c], pv_ref, c)
        vc_ref[0, c] = vr
        vc_ref[1, c] = vi
        fwd_a(c, vr, vi)
        return carry

    def pass1_b(c, carry):
        cr, ci = inv_a(c)
        bias = fb_ref[0, pl.ds(c, 1), :]
        zr = _short_conv(x1_ref[0, c], p1_ref, c) * (cr + bias * vc_ref[0, c])
        zi = _short_conv(x1_ref[1, c], p1_ref, c) * (ci + bias * vc_ref[1, c])
        z1_ref[0, c] = zr
        z1_ref[1, c] = zi
        fwd_a(c, zr, zi)
        return carry

    def pass2_b(c, carry):
        cr, ci = inv_a(c)
        bias = fb_ref[1, pl.ds(c, 1), :]
        vc_ref[0, c] = _short_conv(x2_ref[0, c], p2_ref, c) * (cr + bias * z1_ref[0, c])
        vc_ref[1, c] = _short_conv(x2_ref[1, c], p2_ref, c) * (ci + bias * z1_ref[1, c])
        return carry

    lax.fori_loop(0, C, pass1_a, 0, unroll=SEQ_UNROLL)
    spectral(0)
    lax.fori_loop(0, C, pass1_b, 0, unroll=SEQ_UNROLL)
    spectral(1)
    lax.fori_loop(0, C, pass2_b, 0, unroll=SEQ_UNROLL)
    for b2 in range(2):
        tiles = pltpu.einshape("cab->acb", vc_ref[b2])
        for a in range(tiles.shape[0]):
            o_ref[b2, :, a * LANES:(a + 1) * LANES] = tiles[a]


def _hyena(u4, par_u, fb, hspec, cst):
    B = u4.shape[0]
    C = C_HY
    J = D_HYENA // C
    rows = u4.shape[2]
    full = lambda a: pl.BlockSpec(a.shape, lambda j, p: (0,) * a.ndim)
    ma, g, ginv = cst["ma"].astype(BF16), cst["g"].astype(BF16), cst["ginv"].astype(BF16)
    mir, mii = cst["minv_r"].astype(BF16), cst["minv_i"].astype(BF16)
    tw = cst["tw"]
    u_spec = lambda k: pl.BlockSpec((2, C, rows, LANES), lambda j, p, k=k: (p, j + k * J, 0, 0))
    par_spec = lambda k: pl.BlockSpec((4, C, LANES), lambda j, p, k=k: (0, j + k * J, 0))
    return pl.pallas_call(
        _hyena_body,
        grid=(J, B // 2),
        in_specs=[
            u_spec(0), u_spec(1), u_spec(2), par_spec(0), par_spec(1), par_spec(2),
            pl.BlockSpec((2, C, LANES), lambda j, p: (0, j, 0)),
            pl.BlockSpec((2, C, FFT_N1, 2 * LANES), lambda j, p: (0, j, 0, 0)),
            full(ma), full(tw), full(g), full(ginv), full(mir), full(mii),
        ],
        out_specs=pl.BlockSpec((2, C, rows * LANES), lambda j, p: (p, j, 0)),
        out_shape=jax.ShapeDtypeStruct((B, D_HYENA, rows * LANES), F32),
        scratch_shapes=[
            pltpu.VMEM((C * FFT_N1, 2 * LANES), BF16),
            pltpu.VMEM((C * FFT_N1, 2 * LANES), F32),
            pltpu.VMEM((2, C, rows, LANES), F32),
            pltpu.VMEM((2, C, rows, LANES), F32),
        ],
        compiler_params=_cparams(("parallel", "arbitrary")),
        name="hyena",
    )(u4, u4, u4, par_u, par_u, par_u, fb, hspec, ma, tw, g, ginv, mir, mii)


def _dot3(a, b):
    ah = a.astype(BF16)
    al = (a - ah.astype(F32)).astype(BF16)
    bh = b.astype(BF16)
    bl = (b - bh.astype(F32)).astype(BF16)
    return jnp.dot(jnp.concatenate([ah, al, ah], axis=1), jnp.concatenate([bh, bh, bl], axis=0),
                   preferred_element_type=F32)


def _filtgen_body(zt_ref, w1_ref, b1_ref, f1_ref, w2_ref, b2_ref, f2_ref, w3f_ref, w3b_ref, ad_ref, tt_ref,
                  o_ref, hid_ref):
    L = hid_ref.shape[1] // 2

    @pl.when(pl.program_id(0) == 0)
    def _():
        h1 = jnp.sin(f1_ref[...] * (_dot3(w1_ref[...], zt_ref[...]) + b1_ref[...]))
        hid_ref[...] = jnp.sin(f2_ref[...] * (_dot3(w2_ref[...], h1) + b2_ref[...]))

    ad = ad_ref[...]
    hf = _dot3(w3f_ref[...], hid_ref[:, :L]) * jnp.exp(-ad * tt_ref[:, :L])
    hb = _dot3(w3b_ref[...], hid_ref[:, L:]) * jnp.exp(-ad * tt_ref[:, L:])
    hf = hf / (jnp.sum(jnp.abs(hf), axis=-1, keepdims=True) + EPS)
    hb = hb / (jnp.sum(jnp.abs(hb), axis=-1, keepdims=True) + EPS)
    first = lax.broadcasted_iota(jnp.int32, hb.shape, 1) == 0
    cf = hf + jnp.where(first, hb, 0.0)
    cb = jnp.where(first, 0.0, hb)
    chunks = [c[:, a * LANES:(a + 1) * LANES] for c in (cf, cb) for a in range(L // LANES)]
    o_ref[...] = pltpu.einshape("arl->ral", jnp.stack(chunks, axis=0))


def _filter_spectra(L, w_f1, b_f1, freq1, w_f2, b_f2, freq2, w_f3, cst):
    bands = (FILTER_EMB - 1) // 2
    t = jnp.linspace(0.0, 1.0, L, dtype=F32)[:, None]
    w = (2.0 * math.pi / L) * jnp.arange(L, dtype=F32)[:, None]
    f = jnp.linspace(1e-4, bands - 1, bands, dtype=F32)[None]
    zf = f * w
    z = jnp.concatenate([t, jnp.cos(zf), -jnp.sin(zf)], axis=-1)
    back = lambda a: jnp.roll(a[::-1], 1, axis=0)
    kpad = 48
    zt = jnp.pad(jnp.concatenate([z, back(z)], axis=0).T, ((0, kpad - FILTER_EMB), (0, 0)))
    tt = jnp.concatenate([t, back(t)], axis=0).T
    w1t = jnp.pad(w_f1.T, ((0, 0), (0, kpad - FILTER_EMB)))
    w3 = w_f3.reshape(-1, 2, 2, D_HYENA)
    w3f = jnp.transpose(w3[:, :, 0], (1, 2, 0)).reshape(2 * D_HYENA, -1)
    w3b = jnp.transpose(w3[:, :, 1], (1, 2, 0)).reshape(2 * D_HYENA, -1)
    max_decay = math.log(DECAY_TARGET) / FAST_DECAY_PCT
    min_decay = math.log(DECAY_TARGET) / SLOW_DECAY_PCT
    deltas = jnp.linspace(min_decay, max_decay, D_HYENA, dtype=F32)
    ad = jnp.tile(jnp.abs(deltas), 2)[:, None]
    col = lambda v: v[:, None]
    R = C_FILT
    n_rows = 2 * D_HYENA
    full = lambda a: pl.BlockSpec(a.shape, lambda i: (0,) * a.ndim)
    rows = lambda a: pl.BlockSpec((R, a.shape[1]), lambda i: (i, 0))
    args = (zt, w1t, col(b_f1), col(freq1), w_f2.T, col(b_f2), col(freq2))
    maf, tw, g = cst["maf"].astype(BF16), cst["tw"], cst["g"].astype(BF16)

    def body(*refs):
        gen_in, (maf_ref, tw_ref, g_ref, h_ref, hid_ref, circ_ref, s1_ref) = refs[:11], refs[11:]
        _filtgen_body(*gen_in, circ_ref, hid_ref)
        _filtfft_body(circ_ref, maf_ref, tw_ref, g_ref, h_ref, s1_ref)

    return pl.pallas_call(
        body,
        grid=(n_rows // R,),
        in_specs=[full(a) for a in args] + [rows(w3f), rows(w3b), rows(ad), full(tt), full(maf), full(tw), full(g)],
        out_specs=pl.BlockSpec((R, FFT_N1, 2 * LANES), lambda i: (i, 0, 0)),
        out_shape=jax.ShapeDtypeStruct((n_rows, FFT_N1, 2 * LANES), F32),
        scratch_shapes=[pltpu.VMEM((w_f2.shape[1], 2 * L), F32),
                        pltpu.VMEM((R, 2 * L // LANES, LANES), F32),
                        pltpu.VMEM((R * FFT_N1, 2 * LANES), BF16)],
        compiler_params=_cparams(("arbitrary",)),
        name="filter_spectra",
    )(*args, w3f, w3b, ad, tt, maf, tw, g)


def _route_lanes(lg):
    neg = -1e30
    lane = lax.broadcasted_iota(jnp.int32, lg.shape, 1)
    gmask = lane < N_GROUPS
    gl = jnp.where(gmask, lg, neg)
    gm = jnp.max(gl, axis=-1, keepdims=True)
    gsum = jnp.sum(jnp.where(gmask, jnp.exp(gl - gm), 0.0), axis=-1, keepdims=True)
    g_top = 1.0 / gsum
    g_sel = jnp.min(jnp.where(gl == gm, lane, LANES), axis=-1, keepdims=True)
    lo = N_GROUPS + EXPERTS_PER_GROUP * g_sel
    el = jnp.where((lane >= lo) & (lane < lo + EXPERTS_PER_GROUP), lg, neg)
    m1 = jnp.max(el, axis=-1, keepdims=True)
    i1 = jnp.min(jnp.where(el == m1, lane, LANES), axis=-1, keepdims=True)
    el2 = jnp.where(lane == i1, neg, el)
    m2 = jnp.max(el2, axis=-1, keepdims=True)
    i2 = jnp.min(jnp.where(el2 == m2, lane, LANES), axis=-1, keepdims=True)
    d = jnp.exp(m2 - m1)
    p1 = 1.0 / (1.0 + d)
    p2 = d / (1.0 + d)
    e1 = (i1 - N_GROUPS).astype(F32)
    e2 = (i2 - N_GROUPS).astype(F32)
    return jnp.where(lane == 0, e1, jnp.where(lane == 1, e2, jnp.where(lane == 2, g_top * p1,
                     jnp.where(lane == 3, g_top * p2, 0.0))))


def _pack_bf16_halves(a):
    w = a.shape[1] // 2
    bits = pltpu.bitcast(a.astype(BF16).astype(F32), jnp.uint32)
    return (bits[:, :w] >> 16) | (bits[:, w:] & jnp.uint32(0xFFFF0000))


def _unpack_bf16_halves(wd):
    lo = pltpu.bitcast(wd << 16, F32)
    hi = pltpu.bitcast(wd & jnp.uint32(0xFFFF0000), F32)
    return jnp.concatenate([lo, hi], axis=1)


def _store_row_tiles(ref, packed):
    chunks = jnp.stack([packed[:, j * LANES:(j + 1) * LANES] for j in range(ROW_CHUNKS)], axis=0)
    ref[...] = pltpu.einshape("jrl->rjl", chunks)


def _load_row_tiles(ref):
    chunks = pltpu.einshape("rjl->jrl", ref[...])
    return jnp.concatenate([chunks[j] for j in range(ROW_CHUNKS)], axis=1)


def _outproj_body(ya_ref, yh_ref, x_ref, ga_ref, gh_ref, wo_ref, bd_ref, gm_ref, wrh_ref, wrl_ref, brt_ref,
                  x1_ref, h2_ref, rt_ref, rtt_ref):
    ya = ya_ref[...]
    yan = ya * lax.rsqrt(_group_sumsq(ya, bd_ref[...]) * (1.0 / HEAD_DIM) + EPS) * ga_ref[...]
    yh = yh_ref[...]
    tm = yh.shape[1]
    yh3 = yh.reshape(D_HYENA // HYENA_HEAD, HYENA_HEAD, tm)
    ms = jnp.mean(yh3 * yh3, axis=1, keepdims=True)
    yhn = (yh3 * lax.rsqrt(ms + EPS)).reshape(D_HYENA, tm) * gh_ref[...]
    mix = (jnp.dot(yan.astype(BF16), wo_ref[:D_ATTN, :], preferred_element_type=F32)
           + jnp.dot(yhn.T.astype(BF16), wo_ref[D_ATTN:, :], preferred_element_type=F32))
    x1 = x_ref[...] + mix
    x1_ref[...] = x1
    h2 = x1 * lax.rsqrt(jnp.mean(x1 * x1, axis=-1, keepdims=True) + EPS) * gm_ref[...]
    _store_row_tiles(h2_ref, _pack_bf16_halves(h2))
    hi = h2.astype(BF16)
    lo = (h2 - hi.astype(F32)).astype(BF16)
    hw = jnp.dot(hi, jnp.concatenate([wrh_ref[...], wrl_ref[...]], axis=1),
                 preferred_element_type=F32)
    lg = hw[:, :LANES] + hw[:, LANES:] + jnp.dot(lo, wrh_ref[...], preferred_element_type=F32) + brt_ref[...]
    route = _route_lanes(lg)
    rt_ref[...] = route
    rtt_ref[...] = route.T[:8]


def _outproj(ya, yht, x, ga, gh, wo, bd, gm, wrh, wrl, brt):
    B, S, D = x.shape
    tm = TM_PROJ
    full = lambda a: pl.BlockSpec(a.shape, lambda b, i: (0,) * a.ndim)
    return pl.pallas_call(
        _outproj_body,
        grid=(B, S // tm),
        in_specs=[
            pl.BlockSpec((None, tm, D_ATTN), lambda b, i: (b, i, 0)),
            pl.BlockSpec((None, D_HYENA, tm), lambda b, i: (b, 0, i)),
            pl.BlockSpec((None, tm, D), lambda b, i: (b, i, 0)),
            full(ga), full(gh), full(wo), full(bd), full(gm), full(wrh), full(wrl), full(brt),
        ],
        out_specs=[
            pl.BlockSpec((None, tm, D), lambda b, i: (b, i, 0)),
            pl.BlockSpec((None, tm, ROW_CHUNKS, LANES), lambda b, i: (b, i, 0, 0)),
            pl.BlockSpec((None, tm, LANES), lambda b, i: (b, i, 0)),
            pl.BlockSpec((None, 8, tm), lambda b, i: (b, 0, i)),
        ],
        out_shape=[
            jax.ShapeDtypeStruct((B, S, D), F32),
            jax.ShapeDtypeStruct((B, S, ROW_CHUNKS, LANES), jnp.uint32),
            jax.ShapeDtypeStruct((B, S, LANES), F32),
            jax.ShapeDtypeStruct((B, 8, S), F32),
        ],
        compiler_params=_cparams(("parallel", "parallel")),
        name="outproj",
    )(ya, yht, x, ga, gh, wo, bd, gm, wrh, wrl, brt)


def _moe_body(be_ref, ra_ref, nlive_ref, h2_hbm, wg_ref, wu_ref, wd_ref, y_hbm,
              wg_s, wu_s, wd_s, xbuf, ybuf, zbuf, sem_in, sem_out, sem_z, *, n_tok, n_rows):
    i = pl.program_id(0)
    nb = nlive_ref[0]
    T = xbuf.shape[1]
    slot = i % 2

    def issue_gathers(blk, sl):
        for r in range(T):
            tok = ra_ref[blk * T + r] & (n_tok - 1)
            pltpu.make_async_copy(h2_hbm.at[tok], xbuf.at[sl, r], sem_in.at[sl]).start()

    def issue_scatters(blk, sl, spare):
        for r in range(T):
            dst = jnp.where(spare, n_rows + r, ra_ref[blk * T + r])
            pltpu.make_async_copy(ybuf.at[sl, r], y_hbm.at[dst], sem_out.at[sl]).start()

    def block_in_wait(sl):
        pltpu.make_async_copy(h2_hbm.at[pl.ds(0, T)], xbuf.at[sl], sem_in.at[sl]).wait()

    def block_out_wait(sl):
        pltpu.make_async_copy(ybuf.at[sl], y_hbm.at[pl.ds(0, T)], sem_out.at[sl]).wait()

    @pl.when(i == 0)
    def _():
        ybuf[...] = jnp.zeros(ybuf.shape, ybuf.dtype)
        zbuf[...] = jnp.zeros(zbuf.shape, zbuf.dtype)
        issue_gathers(0, 0)

    @pl.when(i >= nb)
    def _():
        fill = pltpu.make_async_copy(zbuf, y_hbm.at[pl.ds(i * T, T)], sem_z.at[0])
        fill.start()
        fill.wait()

    prev = be_ref[jnp.maximum(i - 1, 0)]

    @pl.when((i == 0) | (be_ref[i] != prev))
    def _():
        wg_s[...] = wg_ref[...].astype(BF16)
        wu_s[...] = wu_ref[...].astype(BF16)
        wd_s[...] = wd_ref[...].astype(BF16)

    @pl.when(i < nb)
    def _():
        block_in_wait(slot)
        x = _unpack_bf16_halves(_load_row_tiles(xbuf.at[slot])).astype(BF16)
        issue_gathers(jnp.minimum(i + 1, nb - 1), 1 - slot)
        issue_scatters(jnp.maximum(i - 1, 0), 1 - slot, i == 0)
        a = jnp.dot(x, wg_s[...], preferred_element_type=F32)
        b = jnp.dot(x, wu_s[...], preferred_element_type=F32)
        hmid = (a * jax.nn.sigmoid(a)) * b
        y = _pack_bf16_halves(jnp.dot(hmid.astype(BF16), wd_s[...], preferred_element_type=F32))

        @pl.when(i >= 1)
        def _():
            block_out_wait(slot)

        _store_row_tiles(ybuf.at[slot], y)

        @pl.when(i == nb - 1)
        def _():
            issue_scatters(i, slot, False)
            block_in_wait(1 - slot)
            block_out_wait(1 - slot)
            block_out_wait(slot)


def _moe_experts(block_e, row_a, n_live, h2p, w_gate, w_up, w_down):
    n_tok = h2p.shape[0]
    n_rows = row_a.shape[0]
    row = h2p.shape[1:]
    D = w_gate.shape[1]
    T = TB_MOE
    assert n_tok & (n_tok - 1) == 0
    grid_spec = pltpu.PrefetchScalarGridSpec(
        num_scalar_prefetch=3,
        grid=(row_a.shape[0] // T,),
        in_specs=[
            pl.BlockSpec(memory_space=pl.ANY),
            pl.BlockSpec((None, D, D_EXPERT), lambda i, be, ra, nl: (be[i], 0, 0)),
            pl.BlockSpec((None, D, D_EXPERT), lambda i, be, ra, nl: (be[i], 0, 0)),
            pl.BlockSpec((None, D_EXPERT, D), lambda i, be, ra, nl: (be[i], 0, 0)),
        ],
        out_specs=pl.BlockSpec(memory_space=pl.ANY),
        scratch_shapes=[
            pltpu.VMEM((D, D_EXPERT), BF16), pltpu.VMEM((D, D_EXPERT), BF16), pltpu.VMEM((D_EXPERT, D), BF16),
            pltpu.VMEM((2, T) + row, jnp.uint32), pltpu.VMEM((2, T) + row, jnp.uint32),
            pltpu.VMEM((T,) + row, jnp.uint32),
            pltpu.SemaphoreType.DMA((2,)), pltpu.SemaphoreType.DMA((2,)), pltpu.SemaphoreType.DMA((1,)),
        ],
    )
    return pl.pallas_call(
        functools.partial(_moe_body, n_tok=n_tok, n_rows=n_rows),
        grid_spec=grid_spec,
        out_shape=jax.ShapeDtypeStruct((n_rows + T,) + row, jnp.uint32),
        compiler_params=_cparams(("arbitrary",)),
        name="moe_experts",
    )(block_e, row_a, n_live, h2p, w_gate, w_up, w_down)


def _dispatch(e_flat, N):
    T = TB_MOE
    NK = N * TOP_K
    experts = jnp.arange(N_EXPERTS, dtype=jnp.int32)
    order = jnp.argsort(e_flat).astype(jnp.int32)
    onehot = (e_flat[:, None] == experts[None]).astype(jnp.int32)
    counts = jnp.sum(onehot, axis=0)
    ends = jnp.cumsum(counts)
    starts = ends - counts
    padded = (counts + T - 1) // T * T
    pends = jnp.cumsum(padded)
    pstarts = pends - padded
    n_rows = -(-(NK + N_EXPERTS * (T - 1)) // T) * T
    n_blocks = n_rows // T
    blk_start = jnp.arange(n_blocks, dtype=jnp.int32) * T
    block_e = jnp.clip(jnp.sum((pends[None, :] <= blk_start[:, None]).astype(jnp.int32), axis=1),
                       0, N_EXPERTS - 1)
    oh_b = (block_e[:, None] == experts[None]).astype(jnp.int32)
    base = jnp.sum(oh_b * (starts - pstarts)[None], axis=1) + blk_start
    end_b = jnp.sum(oh_b * ends[None], axis=1)
    lane = jnp.arange(T, dtype=jnp.int32)[None]
    src = base[:, None] + lane
    pad_id = NK + blk_start[:, None] + lane - end_b[:, None]
    row_a = jnp.where(src < end_b[:, None], order[jnp.clip(src, 0, NK - 1)], pad_id)
    n_live = (pends[-1:] // T).astype(jnp.int32)
    return block_e.astype(jnp.int32), row_a.reshape(n_rows).astype(jnp.int32), n_live


def _final_body(x1_ref, y0_ref, y1_ref, rt_ref, p_ref, gp_ref, wg_ref, bg_ref, wp_ref, gf_ref, o_ref):
    w0 = rt_ref[:, 2:3]
    w1 = rt_ref[:, 3:4]
    y0 = _unpack_bf16_halves(_load_row_tiles(y0_ref))
    y1 = _unpack_bf16_halves(_load_row_tiles(y1_ref))
    x2 = x1_ref[...] + (y0 * w0 + y1 * w1)
    hp = x2 * lax.rsqrt(jnp.mean(x2 * x2, axis=-1, keepdims=True) + EPS) * gp_ref[...]
    gate = jax.nn.sigmoid(jnp.dot(hp.astype(BF16), wg_ref[...], preferred_element_type=F32) + bg_ref[...])
    pe = jnp.dot(p_ref[...].astype(BF16), wp_ref[...], preferred_element_type=F32)
    x3 = x2 + pe * gate
    o_ref[...] = x3 * lax.rsqrt(jnp.mean(x3 * x3, axis=-1, keepdims=True) + EPS) * gf_ref[...]


def _final(x1, y, route, p, gp, wg, bg, wp, gf):
    N, D = x1.shape
    tm = TM_PROJ
    row = lambda w: pl.BlockSpec((tm, w), lambda i: (i, 0))
    full = lambda a: pl.BlockSpec(a.shape, lambda i: (0,) * a.ndim)
    y0, y1 = y, y
    return pl.pallas_call(
        _final_body,
        grid=(N // tm,),
        in_specs=[row(D), pl.BlockSpec((tm, ROW_CHUNKS, LANES), lambda i: (i, 0, 0)),
                  pl.BlockSpec((tm, ROW_CHUNKS, LANES), lambda i: (i + N // tm, 0, 0)),
                  row(LANES), row(p.shape[1]),
                  full(gp), full(wg), full(bg), full(wp), full(gf)],
        out_specs=row(D),
        out_shape=jax.ShapeDtypeStruct((N, D), F32),
        compiler_params=_cparams(("parallel",)),
        name="ple_final",
    )(x1, y0, y1, route, p, gp, wg, bg, wp, gf)


def kernel(x, p, g_mix, w_in, q_gain, k_gain, conv_w, conv_b, w_f1, b_f1, freq1, w_f2, b_f2, freq2, w_f3, filt_bias, g_attn_out, g_hyena_out, w_out, g_moe, w_group, b_group, w_router, b_router, w_gate, w_up, w_down, g_ple, w_ple_gate, b_ple_gate, w_ple, g_final):
    B, S, D = x.shape
    N = B * S
    assert p.shape[0] == 1 and S == (FFT_N1 // 2) * FFT_N2 and B % 2 == 0
    i = 0
    cst = _dft_constants()
    cos, sin = _rope_tables(S)
    bd = _block_diag_ones(D_ATTN, HEAD_DIM)

    n_qkv = D_ATTN + 2 * D_KV
    wqkv = w_in[i][:, :n_qkv].astype(BF16)
    wut = w_in[i][:, n_qkv:].T.astype(BF16)
    q, kw, vw, ut = _inproj(x, g_mix[i][None], wqkv, wut, bd,
                            jnp.tile(q_gain[i], N_HEADS)[None], jnp.tile(k_gain[i], N_KV_HEADS)[None], cos, sin)

    ya = _attention(q, kw, vw)

    hspec = _filter_spectra(S, w_f1[i], b_f1[i], freq1[i], w_f2[i], b_f2[i], freq2[i], w_f3[i], cst)
    hspec = hspec.reshape(2, D_HYENA, FFT_N1, 2 * LANES)
    du = ut.shape[1]
    u4 = ut.reshape(B, du, S // LANES, LANES)
    par_u = jnp.broadcast_to(jnp.concatenate([conv_w[i], conv_b[i][None]], 0)[:, :, None], (4, du, LANES))
    fb = jnp.broadcast_to(filt_bias[i][:, :, None], (2, D_HYENA, LANES))
    yht = _hyena(u4, par_u, fb, hspec, cst)

    wrt = jnp.zeros((D, LANES), F32).at[:, :N_GROUPS].set(w_group[i]).at[:, N_GROUPS:N_GROUPS + N_EXPERTS].set(w_router[i])
    brt = jnp.zeros((1, LANES), F32).at[0, :N_GROUPS].set(b_group[i]).at[0, N_GROUPS:N_GROUPS + N_EXPERTS].set(b_router[i])
    wrh = wrt.astype(BF16)
    wrl = (wrt - wrh.astype(F32)).astype(BF16)
    x1, h2, route, route_t = _outproj(ya, yht, x, g_attn_out[i][None], g_hyena_out[i][:, None],
                                      w_out[i].astype(BF16), bd, g_moe[i][None], wrh, wrl, brt)

    e_flat = jnp.transpose(route_t[:, :TOP_K], (1, 0, 2)).reshape(TOP_K * N).astype(jnp.int32)
    block_e, row_a, n_live = _dispatch(e_flat, N)
    y = _moe_experts(block_e, row_a, n_live, h2.reshape(N, ROW_CHUNKS, LANES), w_gate[i], w_up[i], w_down[i])

    out = _final(x1.reshape(N, D), y, route.reshape(N, LANES), p[i].reshape(N, -1), g_ple[i][None],
                 w_ple_gate[i].astype(BF16), b_ple_gate[i][None], w_ple[i].astype(BF16), g_final[None])
    return out.reshape(B, S, D)
```

```python
import functools
import math

import numpy as np
import jax
import jax.numpy as jnp
from jax import lax
from jax.experimental import pallas as pl
from jax.experimental.pallas import tpu as pltpu

F32 = jnp.float32
BF16 = jnp.bfloat16

D_MODEL = 1024
EPS = 1e-6
GRID_W = 64
N_HEADS = 8
N_KV_HEADS = 2
HEAD_DIM = 64
D_ATTN = N_HEADS * HEAD_DIM
D_KV = N_KV_HEADS * HEAD_DIM
ROPE_THETA = 10000.0
D_HYENA = 512
HYENA_HEAD = 64
FILTER_EMB = 33
FAST_DECAY_PCT = 0.3
SLOW_DECAY_PCT = 1.5
DECAY_TARGET = 1e-2
N_GROUPS = 4
EXPERTS_PER_GROUP = 8
N_EXPERTS = N_GROUPS * EXPERTS_PER_GROUP
TOP_K = 2
D_EXPERT = 512

LANES = 128
MXU_TILE = 256
FFT_N1 = 64
FFT_N2 = 128
VMEM_LIMIT = 56 * 1024 * 1024

TM_PROJ = 512
TQ_ATTN = 256
C_HY = 32
C_FILT = 128
ROW_CHUNKS = D_MODEL // 2 // LANES
SEQ_UNROLL = 32
TB_MOE = 256


def _cparams(sem):
    return pltpu.CompilerParams(dimension_semantics=sem, vmem_limit_bytes=VMEM_LIMIT)


def _rope_tables(S):
    half = HEAD_DIM // 2
    t = np.arange(S, dtype=np.float32)
    r_idx = np.floor(t / GRID_W)
    c_idx = t - r_idx * GRID_W
    inv = (ROPE_THETA ** (-np.arange(0, half, 2, dtype=np.float32) / half)).astype(np.float32)
    ang_r = r_idx[:, None] * inv[None]
    ang_c = c_idx[:, None] * inv[None]
    cos_h = np.concatenate([np.cos(ang_r), np.cos(ang_r), np.cos(ang_c), np.cos(ang_c)], axis=-1)
    sin_h = np.concatenate([-np.sin(ang_r), np.sin(ang_r), -np.sin(ang_c), np.sin(ang_c)], axis=-1)
    return jnp.asarray(np.tile(cos_h, (1, 2)), F32), jnp.asarray(np.tile(sin_h, (1, 2)), F32)


def _dft_constants():
    n1, n2 = FFT_N1, FFT_N2
    n = n1 * n2
    a = np.arange(n1)
    ang = 2.0 * np.pi * np.outer(a, a) / n1
    far, fai = np.cos(ang), -np.sin(ang)
    hlf = n1 // 2
    ma = np.block([[far[:, :hlf], -fai[:, :hlf]], [fai[:, :hlf], far[:, :hlf]]])
    maf = np.concatenate([far, fai], axis=0)
    b = np.arange(n2)
    angt = 2.0 * np.pi * np.outer(a, b) / n
    tw = np.concatenate([np.cos(angt), -np.sin(angt)], axis=1)
    angb = 2.0 * np.pi * np.outer(b, b) / n2
    fbr, fbi = np.cos(angb), -np.sin(angb)
    g = np.block([[fbr, fbi], [-fbi, fbr]])
    ginv = np.block([[fbr, -fbi], [fbi, fbr]])
    minv_r = np.concatenate([far[:hlf], -fai[:hlf]], axis=0) / n
    minv_i = np.concatenate([fai[:hlf], far[:hlf]], axis=0) / n
    f = lambda m: jnp.asarray(m.astype(np.float32))
    return dict(ma=f(ma), maf=f(maf), tw=f(tw), g=f(g), ginv=f(ginv), minv_r=f(minv_r), minv_i=f(minv_i))


def _block_diag_ones(width, group):
    i = np.arange(width) // group
    return jnp.asarray((i[:, None] == i[None, :]).astype(np.float32)).astype(BF16)


def _group_sumsq(a, bd):
    sq = a * a
    hi = sq.astype(BF16)
    lo = (sq - hi.astype(F32)).astype(BF16)
    w = min(a.shape[-1], MXU_TILE)
    return jnp.concatenate(
        [jnp.dot(hi[:, c:c + w], bd[c:c + w, c:c + w], preferred_element_type=F32)
         + jnp.dot(lo[:, c:c + w], bd[c:c + w, c:c + w], preferred_element_type=F32)
         for c in range(0, a.shape[-1], w)], axis=-1)


def _head_norm_rope(a, gain, bd, cos, sin):
    width = a.shape[-1]
    n = a * lax.rsqrt(_group_sumsq(a, bd) * (1.0 / HEAD_DIM) + EPS) * gain
    rep = width // LANES
    if rep > 1:
        cos = jnp.concatenate([cos] * rep, axis=-1)
        sin = jnp.concatenate([sin] * rep, axis=-1)
    fwd = pltpu.roll(n, width - 16, 1)
    bwd = pltpu.roll(n, 16, 1)
    lane = lax.broadcasted_iota(jnp.int32, n.shape, 1)
    sw = jnp.where((lane % 32) < 16, fwd, bwd)
    return n * cos + sw * sin


def _inproj_body(x_ref, g_ref, wqkv_ref, wu_ref, bd_ref, qg_ref, kg_ref, cos_ref, sin_ref,
                 q_ref, kw_ref, vw_ref, ut_ref):
    x = x_ref[...]
    h = x * lax.rsqrt(jnp.mean(x * x, axis=-1, keepdims=True) + EPS) * g_ref[...]
    hb = h.astype(BF16)
    qkv = jnp.dot(hb, wqkv_ref[...], preferred_element_type=F32)
    cos = cos_ref[...]
    sin = sin_ref[...]
    bd = bd_ref[...]
    q = _head_norm_rope(qkv[:, :D_ATTN], qg_ref[...], bd, cos, sin)
    q_ref[...] = (q * (HEAD_DIM ** -0.5 * math.log2(math.e))).astype(BF16)
    k = _head_norm_rope(qkv[:, D_ATTN:D_ATTN + D_KV], kg_ref[...], bd[:D_KV, :D_KV], cos, sin)
    kt = k.T.astype(BF16)
    zero = jnp.zeros((HEAD_DIM, kt.shape[1]), BF16)
    for h in range(N_KV_HEADS):
        kh = kt[h * HEAD_DIM:(h + 1) * HEAD_DIM]
        kw_ref[h, 0, :HEAD_DIM] = kh
        kw_ref[h, 0, HEAD_DIM:] = zero
        kw_ref[h, 1, :HEAD_DIM] = zero
        kw_ref[h, 1, HEAD_DIM:] = kh
    v = qkv[:, D_ATTN + D_KV:]
    vr = pltpu.roll(v, HEAD_DIM, 1)
    first = lax.broadcasted_iota(jnp.int32, v.shape, 1) < HEAD_DIM
    vw_ref[0, 0] = jnp.where(first, v, 1.0).astype(BF16)
    vw_ref[0, 1] = jnp.where(first, 1.0, vr).astype(BF16)
    vw_ref[1, 0] = jnp.where(first, vr, 1.0).astype(BF16)
    vw_ref[1, 1] = jnp.where(first, 1.0, v).astype(BF16)
    ut_ref[...] = lax.dot_general(wu_ref[...], hb, (((1,), (1,)), ((), ())),
                                  preferred_element_type=F32)


def _inproj(x, g_mix, wqkv, wut, bd, qg, kg, cos, sin):
    B, S, D = x.shape
    tm = TM_PROJ
    du = wut.shape[0]
    full = lambda shape: pl.BlockSpec(shape, lambda b, i: (0,) * len(shape))
    return pl.pallas_call(
        _inproj_body,
        grid=(B, S // tm),
        in_specs=[
            pl.BlockSpec((None, tm, D), lambda b, i: (b, i, 0)),
            full((1, D)), full(wqkv.shape), full(wut.shape), full(bd.shape),
            full((1, D_ATTN)), full((1, D_KV)),
            pl.BlockSpec((tm, LANES), lambda b, i: (i, 0)),
            pl.BlockSpec((tm, LANES), lambda b, i: (i, 0)),
        ],
        out_specs=[
            pl.BlockSpec((None, tm, D_ATTN), lambda b, i: (b, i, 0)),
            pl.BlockSpec((None, N_KV_HEADS, 2, LANES, tm), lambda b, i: (b, 0, 0, 0, i)),
            pl.BlockSpec((None, N_KV_HEADS, 2, tm, LANES), lambda b, i: (b, 0, 0, i, 0)),
            pl.BlockSpec((None, du, tm), lambda b, i: (b, 0, i)),
        ],
        out_shape=[
            jax.ShapeDtypeStruct((B, S, D_ATTN), BF16),
            jax.ShapeDtypeStruct((B, N_KV_HEADS, 2, LANES, S), BF16),
            jax.ShapeDtypeStruct((B, N_KV_HEADS, 2, S, LANES), BF16),
            jax.ShapeDtypeStruct((B, du, S), F32),
        ],
        compiler_params=_cparams(("parallel", "parallel")),
        name="inproj",
    )(x, g_mix, wqkv, wut, bd, qg, kg, cos, sin)


def _attn_body(q_ref, kw_ref, vw_ref, o_ref):

    def one_head(q, kw, vw):
        s = jnp.dot(q, kw, preferred_element_type=F32)
        m = jnp.max(s, axis=-1, keepdims=True)
        p = jnp.exp2(s - m).astype(BF16)
        return jnp.dot(p, vw, preferred_element_type=F32)

    for pair in range(D_ATTN // LANES):
        h = pair // (N_HEADS // N_KV_HEADS // 2)
        q = q_ref[:, pair * LANES:(pair + 1) * LANES]
        oe = one_head(q, kw_ref[h, 0], vw_ref[h, 0])
        oo = one_head(q, kw_ref[h, 1], vw_ref[h, 1])
        first = lax.broadcasted_iota(jnp.int32, oe.shape, 1) < HEAD_DIM
        num = jnp.where(first, oe, oo)
        den = jnp.where(first, pltpu.roll(oe, HEAD_DIM, 1), pltpu.roll(oo, HEAD_DIM, 1))
        o_ref[:, pair * LANES:(pair + 1) * LANES] = num / den


def _attention(q, kw, vw):
    B, S, _ = q.shape
    tq = TQ_ATTN
    return pl.pallas_call(
        _attn_body,
        grid=(B, S // tq),
        in_specs=[
            pl.BlockSpec((None, tq, D_ATTN), lambda b, i: (b, i, 0)),
            pl.BlockSpec((None, N_KV_HEADS, 2, LANES, S), lambda b, i: (b, 0, 0, 0, 0)),
            pl.BlockSpec((None, N_KV_HEADS, 2, S, LANES), lambda b, i: (b, 0, 0, 0, 0)),
        ],
        out_specs=pl.BlockSpec((None, tq, D_ATTN), lambda b, i: (b, i, 0)),
        out_shape=jax.ShapeDtypeStruct((B, S, D_ATTN), F32),
        compiler_params=_cparams(("parallel", "arbitrary")),
        name="attention",
    )(q, kw, vw)


def _fwd_twiddle_store(y, tw_ref, s1_ref, row0):
    yr, yi = y[:FFT_N1], y[FFT_N1:]
    twr, twi = tw_ref[:, :LANES], tw_ref[:, LANES:]
    s1_ref[pl.ds(row0, FFT_N1), :LANES] = (yr * twr - yi * twi).astype(BF16)
    s1_ref[pl.ds(row0, FFT_N1), LANES:] = (yr * twi + yi * twr).astype(BF16)


def _filtfft_body(x_ref, maf_ref, tw_ref, g_ref, h_ref, s1_ref):
    C = x_ref.shape[0]

    def step_a(c, carry):
        y = jnp.dot(maf_ref[...], x_ref[c].astype(BF16), preferred_element_type=F32)
        _fwd_twiddle_store(y, tw_ref, s1_ref, pl.multiple_of(c * FFT_N1, FFT_N1))
        return carry

    lax.fori_loop(0, C, step_a, 0, unroll=SEQ_UNROLL)
    z = jnp.dot(s1_ref[...], g_ref[...], preferred_element_type=F32)
    h_ref[...] = z.reshape(C, FFT_N1, 2 * LANES)


def _short_conv(x, par_ref, c):
    rows, lanes = x.shape
    a_i = lax.broadcasted_iota(jnp.int32, x.shape, 0)
    b_i = lax.broadcasted_iota(jnp.int32, x.shape, 1)
    l1 = pltpu.roll(x, 1, 1)
    l2 = pltpu.roll(l1, 1, 0)
    prev = jnp.where(b_i == 0, l2, l1)
    prev = jnp.where((a_i == 0) & (b_i == 0), 0.0, prev)
    r1 = pltpu.roll(x, lanes - 1, 1)
    r2 = pltpu.roll(r1, rows - 1, 0)
    nxt = jnp.where(b_i == lanes - 1, r2, r1)
    nxt = jnp.where((a_i == rows - 1) & (b_i == lanes - 1), 0.0, nxt)
    w0 = par_ref[0, pl.ds(c, 1), :]
    w1 = par_ref[1, pl.ds(c, 1), :]
    w2 = par_ref[2, pl.ds(c, 1), :]
    cb = par_ref[3, pl.ds(c, 1), :]
    return cb + prev * w0 + x * w1 + nxt * w2


def _hyena_body(v_ref, x1_ref, x2_ref, pv_ref, p1_ref, p2_ref, fb_ref, h_ref,
                ma_ref, tw_ref, g_ref, ginv_ref, mir_ref, mii_ref,
                o_ref, s1_ref, s2_ref, vc_ref, z1_ref):
    C = v_ref.shape[1]
    half = FFT_N1 // 2

    def spectral(order):
        z = jnp.dot(s1_ref[...], g_ref[...], preferred_element_type=F32)
        hs = h_ref[order].reshape(C * FFT_N1, 2 * LANES)
        zr, zi = z[:, :LANES], z[:, LANES:]
        hr, hi = hs[:, :LANES], hs[:, LANES:]
        pb = jnp.concatenate([zr * hr - zi * hi, zr * hi + zi * hr], axis=1).astype(BF16)
        s2_ref[...] = jnp.dot(pb, ginv_ref[...], preferred_element_type=F32)

    def inv_a(c):
        row0 = pl.multiple_of(c * FFT_N1, FFT_N1)
        y = s2_ref[pl.ds(row0, FFT_N1), :]
        yr, yi = y[:, :LANES], y[:, LANES:]
        twr, twi = tw_ref[:, :LANES], tw_ref[:, LANES:]
        ur = (yr * twr + yi * twi).astype(BF16)
        ui = (yi * twr - yr * twi).astype(BF16)
        out = (jnp.dot(mir_ref[...], ur, preferred_element_type=F32)
               + jnp.dot(mii_ref[...], ui, preferred_element_type=F32))
        return out[:half], out[half:]

    def fwd_a(c, xr, xi):
        xs = jnp.concatenate([xr, xi], axis=0).astype(BF16)
        y = jnp.dot(ma_ref[...], xs, preferred_element_type=F32)
        _fwd_twiddle_store(y, tw_ref, s1_ref, pl.multiple_of(c * FFT_N1, FFT_N1))

    def pass1_a(c, carry):
        vr = _short_conv(v_ref[0, c], pv_ref, c)
        vi = _short_conv(v_ref[1, c], pv_ref, c)
        vc_ref[0, c] = vr
        vc_ref[1, c] = vi
        fwd_a(c, vr, vi)
        return carry

    def pass1_b(c, carry):
        cr, ci = inv_a(c)
        bias = fb_ref[0, pl.ds(c, 1), :]
        zr = _short_conv(x1_ref[0, c], p1_ref, c) * (cr + bias * vc_ref[0, c])
        zi = _short_conv(x1_ref[1, c], p1_ref, c) * (ci + bias * vc_ref[1, c])
        z1_ref[0, c] = zr
        z1_ref[1, c] = zi
        fwd_a(c, zr, zi)
        return carry

    def pass2_b(c, carry):
        cr, ci = inv_a(c)
        bias = fb_ref[1, pl.ds(c, 1), :]
        vc_ref[0, c] = _short_conv(x2_ref[0, c], p2_ref, c) * (cr + bias * z1_ref[0, c])
        vc_ref[1, c] = _short_conv(x2_ref[1, c], p2_ref, c) * (ci + bias * z1_ref[1, c])
        return carry

    lax.fori_loop(0, C, pass1_a, 0, unroll=SEQ_UNROLL)
    spectral(0)
    lax.fori_loop(0, C, pass1_b, 0, unroll=SEQ_UNROLL)
    spectral(1)
    lax.fori_loop(0, C, pass2_b, 0, unroll=SEQ_UNROLL)
    for b2 in range(2):
        tiles = pltpu.einshape("cab->acb", vc_ref[b2])
        for a in range(tiles.shape[0]):
            o_ref[b2, :, a * LANES:(a + 1) * LANES] = tiles[a]


def _hyena(u4, par_u, fb, hspec, cst):
    B = u4.shape[0]
    C = C_HY
    J = D_HYENA // C
    rows = u4.shape[2]
    full = lambda a: pl.BlockSpec(a.shape, lambda j, p: (0,) * a.ndim)
    ma, g, ginv = cst["ma"].astype(BF16), cst["g"].astype(BF16), cst["ginv"].astype(BF16)
    mir, mii = cst["minv_r"].astype(BF16), cst["minv_i"].astype(BF16)
    tw = cst["tw"]
    u_spec = lambda k: pl.BlockSpec((2, C, rows, LANES), lambda j, p, k=k: (p, j + k * J, 0, 0))
    par_spec = lambda k: pl.BlockSpec((4, C, LANES), lambda j, p, k=k: (0, j + k * J, 0))
    return pl.pallas_call(
        _hyena_body,
        grid=(J, B // 2),
        in_specs=[
            u_spec(0), u_spec(1), u_spec(2), par_spec(0), par_spec(1), par_spec(2),
            pl.BlockSpec((2, C, LANES), lambda j, p: (0, j, 0)),
            pl.BlockSpec((2, C, FFT_N1, 2 * LANES), lambda j, p: (0, j, 0, 0)),
            full(ma), full(tw), full(g), full(ginv), full(mir), full(mii),
        ],
        out_specs=pl.BlockSpec((2, C, rows * LANES), lambda j, p: (p, j, 0)),
        out_shape=jax.ShapeDtypeStruct((B, D_HYENA, rows * LANES), F32),
        scratch_shapes=[
            pltpu.VMEM((C * FFT_N1, 2 * LANES), BF16),
            pltpu.VMEM((C * FFT_N1, 2 * LANES), F32),
            pltpu.VMEM((2, C, rows, LANES), F32),
            pltpu.VMEM((2, C, rows, LANES), F32),
        ],
        compiler_params=_cparams(("parallel", "arbitrary")),
        name="hyena",
    )(u4, u4, u4, par_u, par_u, par_u, fb, hspec, ma, tw, g, ginv, mir, mii)


def _dot3(a, b):
    ah = a.astype(BF16)
    al = (a - ah.astype(F32)).astype(BF16)
    bh = b.astype(BF16)
    bl = (b - bh.astype(F32)).astype(BF16)
    return jnp.dot(jnp.concatenate([ah, al, ah], axis=1), jnp.concatenate([bh, bh, bl], axis=0),
                   preferred_element_type=F32)


def _filtgen_body(zt_ref, w1_ref, b1_ref, f1_ref, w2_ref, b2_ref, f2_ref, w3f_ref, w3b_ref, ad_ref, tt_ref,
                  o_ref, hid_ref):
    L = hid_ref.shape[1] // 2

    @pl.when(pl.program_id(0) == 0)
    def _():
        h1 = jnp.sin(f1_ref[...] * (_dot3(w1_ref[...], zt_ref[...]) + b1_ref[...]))
        hid_ref[...] = jnp.sin(f2_ref[...] * (_dot3(w2_ref[...], h1) + b2_ref[...]))

    ad = ad_ref[...]
    hf = _dot3(w3f_ref[...], hid_ref[:, :L]) * jnp.exp(-ad * tt_ref[:, :L])
    hb = _dot3(w3b_ref[...], hid_ref[:, L:]) * jnp.exp(-ad * tt_ref[:, L:])
    hf = hf / (jnp.sum(jnp.abs(hf), axis=-1, keepdims=True) + EPS)
    hb = hb / (jnp.sum(jnp.abs(hb), axis=-1, keepdims=True) + EPS)
    first = lax.broadcasted_iota(jnp.int32, hb.shape, 1) == 0
    cf = hf + jnp.where(first, hb, 0.0)
    cb = jnp.where(first, 0.0, hb)
    chunks = [c[:, a * LANES:(a + 1) * LANES] for c in (cf, cb) for a in range(L // LANES)]
    o_ref[...] = pltpu.einshape("arl->ral", jnp.stack(chunks, axis=0))


def _filter_spectra(L, w_f1, b_f1, freq1, w_f2, b_f2, freq2, w_f3, cst):
    bands = (FILTER_EMB - 1) // 2
    t = np.linspace(0.0, 1.0, L, dtype=np.float32)[:, None]
    w = (np.float32(2.0 * math.pi / L) * np.arange(L, dtype=np.float32))[:, None]
    f = np.linspace(1e-4, bands - 1, bands, dtype=np.float32)[None]
    zf = (f * w).astype(np.float32)
    z = np.concatenate([t, np.cos(zf), -np.sin(zf)], axis=-1)
    back = lambda a: np.roll(a[::-1], 1, axis=0)
    kpad = 48
    zt = jnp.asarray(np.pad(np.concatenate([z, back(z)], axis=0).T, ((0, kpad - FILTER_EMB), (0, 0))), F32)
    tt = jnp.asarray(np.concatenate([t, back(t)], axis=0).T, F32)
    w1t = jnp.pad(w_f1.T, ((0, 0), (0, kpad - FILTER_EMB)))
    w3 = w_f3.reshape(-1, 2, 2, D_HYENA)
    w3f = jnp.transpose(w3[:, :, 0], (1, 2, 0)).reshape(2 * D_HYENA, -1)
    w3b = jnp.transpose(w3[:, :, 1], (1, 2, 0)).reshape(2 * D_HYENA, -1)
    max_decay = math.log(DECAY_TARGET) / FAST_DECAY_PCT
    min_decay = math.log(DECAY_TARGET) / SLOW_DECAY_PCT
    deltas = jnp.linspace(min_decay, max_decay, D_HYENA, dtype=F32)
    ad = jnp.tile(jnp.abs(deltas), 2)[:, None]
    col = lambda v: v[:, None]
    R = C_FILT
    n_rows = 2 * D_HYENA
    full = lambda a: pl.BlockSpec(a.shape, lambda i: (0,) * a.ndim)
    rows = lambda a: pl.BlockSpec((R, a.shape[1]), lambda i: (i, 0))
    args = (zt, w1t, col(b_f1), col(freq1), w_f2.T, col(b_f2), col(freq2))
    maf, tw, g = cst["maf"].astype(BF16), cst["tw"], cst["g"].astype(BF16)

    def body(*refs):
        gen_in, (maf_ref, tw_ref, g_ref, h_ref, hid_ref, circ_ref, s1_ref) = refs[:11], refs[11:]
        _filtgen_body(*gen_in, circ_ref, hid_ref)
        _filtfft_body(circ_ref, maf_ref, tw_ref, g_ref, h_ref, s1_ref)

    return pl.pallas_call(
        body,
        grid=(n_rows // R,),
        in_specs=[full(a) for a in args] + [rows(w3f), rows(w3b), rows(ad), full(tt), full(maf), full(tw), full(g)],
        out_specs=pl.BlockSpec((R, FFT_N1, 2 * LANES), lambda i: (i, 0, 0)),
        out_shape=jax.ShapeDtypeStruct((n_rows, FFT_N1, 2 * LANES), F32),
        scratch_shapes=[pltpu.VMEM((w_f2.shape[1], 2 * L), F32),
                        pltpu.VMEM((R, 2 * L // LANES, LANES), F32),
                        pltpu.VMEM((R * FFT_N1, 2 * LANES), BF16)],
        compiler_params=_cparams(("arbitrary",)),
        name="filter_spectra",
    )(*args, w3f, w3b, ad, tt, maf, tw, g)


def _route_lanes(lg):
    neg = -1e30
    lane = lax.broadcasted_iota(jnp.int32, lg.shape, 1)
    gmask = lane < N_GROUPS
    gl = jnp.where(gmask, lg, neg)
    gm = jnp.max(gl, axis=-1, keepdims=True)
    gsum = jnp.sum(jnp.where(gmask, jnp.exp(gl - gm), 0.0), axis=-1, keepdims=True)
    g_top = 1.0 / gsum
    g_sel = jnp.min(jnp.where(gl == gm, lane, LANES), axis=-1, keepdims=True)
    lo = N_GROUPS + EXPERTS_PER_GROUP * g_sel
    el = jnp.where((lane >= lo) & (lane < lo + EXPERTS_PER_GROUP), lg, neg)
    m1 = jnp.max(el, axis=-1, keepdims=True)
    i1 = jnp.min(jnp.where(el == m1, lane, LANES), axis=-1, keepdims=True)
    el2 = jnp.where(lane == i1, neg, el)
    m2 = jnp.max(el2, axis=-1, keepdims=True)
    i2 = jnp.min(jnp.where(el2 == m2, lane, LANES), axis=-1, keepdims=True)
    d = jnp.exp(m2 - m1)
    p1 = 1.0 / (1.0 + d)
    p2 = d / (1.0 + d)
    e1 = (i1 - N_GROUPS).astype(F32)
    e2 = (i2 - N_GROUPS).astype(F32)
    return jnp.where(lane == 0, e1, jnp.where(lane == 1, e2, jnp.where(lane == 2, g_top * p1,
                     jnp.where(lane == 3, g_top * p2, 0.0))))


def _pack_bf16_halves(a):
    w = a.shape[1] // 2
    bits = pltpu.bitcast(a.astype(BF16).astype(F32), jnp.uint32)
    return (bits[:, :w] >> 16) | (bits[:, w:] & jnp.uint32(0xFFFF0000))


def _unpack_bf16_halves(wd):
    lo = pltpu.bitcast(wd << 16, F32)
    hi = pltpu.bitcast(wd & jnp.uint32(0xFFFF0000), F32)
    return jnp.concatenate([lo, hi], axis=1)


def _store_row_tiles(ref, packed):
    chunks = jnp.stack([packed[:, j * LANES:(j + 1) * LANES] for j in range(ROW_CHUNKS)], axis=0)
    ref[...] = pltpu.einshape("jrl->rjl", chunks)


def _load_row_tiles(ref):
    chunks = pltpu.einshape("rjl->jrl", ref[...])
    return jnp.concatenate([chunks[j] for j in range(ROW_CHUNKS)], axis=1)


def _outproj_body(ya_ref, yh_ref, x_ref, ga_ref, gh_ref, wo_ref, bd_ref, gm_ref, wrh_ref, wrl_ref, brt_ref,
                  x1_ref, h2_ref, rt_ref, rtt_ref):
    ya = ya_ref[...]
    yan = ya * lax.rsqrt(_group_sumsq(ya, bd_ref[...]) * (1.0 / HEAD_DIM) + EPS) * ga_ref[...]
    yh = yh_ref[...]
    tm = yh.shape[1]
    yh3 = yh.reshape(D_HYENA // HYENA_HEAD, HYENA_HEAD, tm)
    ms = jnp.mean(yh3 * yh3, axis=1, keepdims=True)
    yhn = (yh3 * lax.rsqrt(ms + EPS)).reshape(D_HYENA, tm) * gh_ref[...]
    mix = (jnp.dot(yan.astype(BF16), wo_ref[:D_ATTN, :], preferred_element_type=F32)
           + jnp.dot(yhn.T.astype(BF16), wo_ref[D_ATTN:, :], preferred_element_type=F32))
    x1 = x_ref[...] + mix
    x1_ref[...] = x1
    h2 = x1 * lax.rsqrt(jnp.mean(x1 * x1, axis=-1, keepdims=True) + EPS) * gm_ref[...]
    _store_row_tiles(h2_ref, _pack_bf16_halves(h2))
    hi = h2.astype(BF16)
    lo = (h2 - hi.astype(F32)).astype(BF16)
    hw = jnp.dot(hi, jnp.concatenate([wrh_ref[...], wrl_ref[...]], axis=1),
                 preferred_element_type=F32)
    lg = hw[:, :LANES] + hw[:, LANES:] + jnp.dot(lo, wrh_ref[...], preferred_element_type=F32) + brt_ref[...]
    route = _route_lanes(lg)
    rt_ref[...] = route
    rtt_ref[...] = route.T[:8]


def _outproj(ya, yht, x, ga, gh, wo, bd, gm, wrh, wrl, brt):
    B, S, D = x.shape
    tm = TM_PROJ
    full = lambda a: pl.BlockSpec(a.shape, lambda b, i: (0,) * a.ndim)
    return pl.pallas_call(
        _outproj_body,
        grid=(B, S // tm),
        in_specs=[
            pl.BlockSpec((None, tm, D_ATTN), lambda b, i: (b, i, 0)),
            pl.BlockSpec((None, D_HYENA, tm), lambda b, i: (b, 0, i)),
            pl.BlockSpec((None, tm, D), lambda b, i: (b, i, 0)),
            full(ga), full(gh), full(wo), full(bd), full(gm), full(wrh), full(wrl), full(brt),
        ],
        out_specs=[
            pl.BlockSpec((None, tm, D), lambda b, i: (b, i, 0)),
            pl.BlockSpec((None, tm, ROW_CHUNKS, LANES), lambda b, i: (b, i, 0, 0)),
            pl.BlockSpec((None, tm, LANES), lambda b, i: (b, i, 0)),
            pl.BlockSpec((None, 8, tm), lambda b, i: (b, 0, i)),
        ],
        out_shape=[
            jax.ShapeDtypeStruct((B, S, D), F32),
            jax.ShapeDtypeStruct((B, S, ROW_CHUNKS, LANES), jnp.uint32),
            jax.ShapeDtypeStruct((B, S, LANES), F32),
            jax.ShapeDtypeStruct((B, 8, S), F32),
        ],
        compiler_params=_cparams(("parallel", "parallel")),
        name="outproj",
    )(ya, yht, x, ga, gh, wo, bd, gm, wrh, wrl, brt)


def _moe_body(be_ref, ra_ref, nlive_ref, h2_hbm, wg_ref, wu_ref, wd_ref, y_hbm,
              wg_s, wu_s, wd_s, xbuf, ybuf, zbuf, sem_in, sem_out, sem_z, *, n_tok, n_rows):
    i = pl.program_id(0)
    nb = nlive_ref[0]
    T = xbuf.shape[1]
    slot = i % 2

    def issue_gathers(blk, sl):
        for r in range(T):
            tok = ra_ref[blk * T + r] & (n_tok - 1)
            pltpu.make_async_copy(h2_hbm.at[tok], xbuf.at[sl, r], sem_in.at[sl]).start()

    def issue_scatters(blk, sl, spare):
        for r in range(T):
            dst = jnp.where(spare, n_rows + r, ra_ref[blk * T + r])
            pltpu.make_async_copy(ybuf.at[sl, r], y_hbm.at[dst], sem_out.at[sl]).start()

    def block_in_wait(sl):
        pltpu.make_async_copy(h2_hbm.at[pl.ds(0, T)], xbuf.at[sl], sem_in.at[sl]).wait()

    def block_out_wait(sl):
        pltpu.make_async_copy(ybuf.at[sl], y_hbm.at[pl.ds(0, T)], sem_out.at[sl]).wait()

    @pl.when(i == 0)
    def _():
        ybuf[...] = jnp.zeros(ybuf.shape, ybuf.dtype)
        zbuf[...] = jnp.zeros(zbuf.shape, zbuf.dtype)
        issue_gathers(0, 0)

    @pl.when(i >= nb)
    def _():
        fill = pltpu.make_async_copy(zbuf, y_hbm.at[pl.ds(i * T, T)], sem_z.at[0])
        fill.start()
        fill.wait()

    prev = be_ref[jnp.maximum(i - 1, 0)]

    @pl.when((i == 0) | (be_ref[i] != prev))
    def _():
        wg_s[...] = wg_ref[...].astype(BF16)
        wu_s[...] = wu_ref[...].astype(BF16)
        wd_s[...] = wd_ref[...].astype(BF16)

    @pl.when(i < nb)
    def _():
        block_in_wait(slot)
        x = _unpack_bf16_halves(_load_row_tiles(xbuf.at[slot])).astype(BF16)
        issue_gathers(jnp.minimum(i + 1, nb - 1), 1 - slot)
        issue_scatters(jnp.maximum(i - 1, 0), 1 - slot, i == 0)
        a = jnp.dot(x, wg_s[...], preferred_element_type=F32)
        b = jnp.dot(x, wu_s[...], preferred_element_type=F32)
        hmid = (a * jax.nn.sigmoid(a)) * b
        y = _pack_bf16_halves(jnp.dot(hmid.astype(BF16), wd_s[...], preferred_element_type=F32))

        @pl.when(i >= 1)
        def _():
            block_out_wait(slot)

        _store_row_tiles(ybuf.at[slot], y)

        @pl.when(i == nb - 1)
        def _():
            issue_scatters(i, slot, False)
            block_in_wait(1 - slot)
            block_out_wait(1 - slot)
            block_out_wait(slot)


def _moe_experts(block_e, row_a, n_live, h2p, w_gate, w_up, w_down):
    n_tok = h2p.shape[0]
    n_rows = row_a.shape[0]
    row = h2p.shape[1:]
    D = w_gate.shape[1]
    T = TB_MOE
    assert n_tok & (n_tok - 1) == 0
    grid_spec = pltpu.PrefetchScalarGridSpec(
        num_scalar_prefetch=3,
        grid=(row_a.shape[0] // T,),
        in_specs=[
            pl.BlockSpec(memory_space=pl.ANY),
            pl.BlockSpec((None, D, D_EXPERT), lambda i, be, ra, nl: (be[i], 0, 0)),
            pl.BlockSpec((None, D, D_EXPERT), lambda i, be, ra, nl: (be[i], 0, 0)),
            pl.BlockSpec((None, D_EXPERT, D), lambda i, be, ra, nl: (be[i], 0, 0)),
        ],
        out_specs=pl.BlockSpec(memory_space=pl.ANY),
        scratch_shapes=[
            pltpu.VMEM((D, D_EXPERT), BF16), pltpu.VMEM((D, D_EXPERT), BF16), pltpu.VMEM((D_EXPERT, D), BF16),
            pltpu.VMEM((2, T) + row, jnp.uint32), pltpu.VMEM((2, T) + row, jnp.uint32),
            pltpu.VMEM((T,) + row, jnp.uint32),
            pltpu.SemaphoreType.DMA((2,)), pltpu.SemaphoreType.DMA((2,)), pltpu.SemaphoreType.DMA((1,)),
        ],
    )
    return pl.pallas_call(
        functools.partial(_moe_body, n_tok=n_tok, n_rows=n_rows),
        grid_spec=grid_spec,
        out_shape=jax.ShapeDtypeStruct((n_rows + T,) + row, jnp.uint32),
        compiler_params=_cparams(("arbitrary",)),
        name="moe_experts",
    )(block_e, row_a, n_live, h2p, w_gate, w_up, w_down)


def _dispatch(e_flat, N):
    T = TB_MOE
    NK = N * TOP_K
    experts = jnp.arange(N_EXPERTS, dtype=jnp.int32)
    order = jnp.argsort(e_flat).astype(jnp.int32)
    onehot = (e_flat[:, None] == experts[None]).astype(jnp.int32)
    counts = jnp.sum(onehot, axis=0)
    ends = jnp.cumsum(counts)
    starts = ends - counts
    padded = (counts + T - 1) // T * T
    pends = jnp.cumsum(padded)
    pstarts = pends - padded
    n_rows = -(-(NK + N_EXPERTS * (T - 1)) // T) * T
    n_blocks = n_rows // T
    blk_start = jnp.arange(n_blocks, dtype=jnp.int32) * T
    block_e = jnp.clip(jnp.sum((pends[None, :] <= blk_start[:, None]).astype(jnp.int32), axis=1),
                       0, N_EXPERTS - 1)
    oh_b = (block_e[:, None] == experts[None]).astype(jnp.int32)
    base = jnp.sum(oh_b * (starts - pstarts)[None], axis=1) + blk_start
    end_b = jnp.sum(oh_b * ends[None], axis=1)
    lane = jnp.arange(T, dtype=jnp.int32)[None]
    src = base[:, None] + lane
    pad_id = NK + blk_start[:, None] + lane - end_b[:, None]
    row_a = jnp.where(src < end_b[:, None], order[jnp.clip(src, 0, NK - 1)], pad_id)
    n_live = (pends[-1:] // T).astype(jnp.int32)
    return block_e.astype(jnp.int32), row_a.reshape(n_rows).astype(jnp.int32), n_live


def _final_body(x1_ref, y0_ref, y1_ref, rt_ref, p_ref, gp_ref, wg_ref, bg_ref, wp_ref, gf_ref, o_ref):
    w0 = rt_ref[:, 2:3]
    w1 = rt_ref[:, 3:4]
    y0 = _unpack_bf16_halves(_load_row_tiles(y0_ref))
    y1 = _unpack_bf16_halves(_load_row_tiles(y1_ref))
    x2 = x1_ref[...] + (y0 * w0 + y1 * w1)
    hp = x2 * lax.rsqrt(jnp.mean(x2 * x2, axis=-1, keepdims=True) + EPS) * gp_ref[...]
    gate = jax.nn.sigmoid(jnp.dot(hp.astype(BF16), wg_ref[...], preferred_element_type=F32) + bg_ref[...])
    pe = jnp.dot(p_ref[...].astype(BF16), wp_ref[...], preferred_element_type=F32)
    x3 = x2 + pe * gate
    o_ref[...] = x3 * lax.rsqrt(jnp.mean(x3 * x3, axis=-1, keepdims=True) + EPS) * gf_ref[...]


def _final(x1, y, route, p, gp, wg, bg, wp, gf):
    N, D = x1.shape
    tm = TM_PROJ
    row = lambda w: pl.BlockSpec((tm, w), lambda i: (i, 0))
    full = lambda a: pl.BlockSpec(a.shape, lambda i: (0,) * a.ndim)
    y0, y1 = y, y
    return pl.pallas_call(
        _final_body,
        grid=(N // tm,),
        in_specs=[row(D), pl.BlockSpec((tm, ROW_CHUNKS, LANES), lambda i: (i, 0, 0)),
                  pl.BlockSpec((tm, ROW_CHUNKS, LANES), lambda i: (i + N // tm, 0, 0)),
                  row(LANES), row(p.shape[1]),
                  full(gp), full(wg), full(bg), full(wp), full(gf)],
        out_specs=row(D),
        out_shape=jax.ShapeDtypeStruct((N, D), F32),
        compiler_params=_cparams(("parallel",)),
        name="ple_final",
    )(x1, y0, y1, route, p, gp, wg, bg, wp, gf)


def kernel(x, p, g_mix, w_in, q_gain, k_gain, conv_w, conv_b, w_f1, b_f1, freq1, w_f2, b_f2, freq2, w_f3, filt_bias, g_attn_out, g_hyena_out, w_out, g_moe, w_group, b_group, w_router, b_router, w_gate, w_up, w_down, g_ple, w_ple_gate, b_ple_gate, w_ple, g_final):
    B, S, D = x.shape
    N = B * S
    assert p.shape[0] == 1 and S == (FFT_N1 // 2) * FFT_N2 and B % 2 == 0
    i = 0
    cst = _dft_constants()
    cos, sin = _rope_tables(S)
    bd = _block_diag_ones(D_ATTN, HEAD_DIM)

    n_qkv = D_ATTN + 2 * D_KV
    wqkv = w_in[i][:, :n_qkv].astype(BF16)
    wut = w_in[i][:, n_qkv:].T.astype(BF16)
    q, kw, vw, ut = _inproj(x, g_mix[i][None], wqkv, wut, bd,
                            jnp.tile(q_gain[i], N_HEADS)[None], jnp.tile(k_gain[i], N_KV_HEADS)[None], cos, sin)

    ya = _attention(q, kw, vw)

    hspec = _filter_spectra(S, w_f1[i], b_f1[i], freq1[i], w_f2[i], b_f2[i], freq2[i], w_f3[i], cst)
    hspec = hspec.reshape(2, D_HYENA, FFT_N1, 2 * LANES)
    du = ut.shape[1]
    u4 = ut.reshape(B, du, S // LANES, LANES)
    par_u = jnp.broadcast_to(jnp.concatenate([conv_w[i], conv_b[i][None]], 0)[:, :, None], (4, du, LANES))
    fb = jnp.broadcast_to(filt_bias[i][:, :, None], (2, D_HYENA, LANES))
    yht = _hyena(u4, par_u, fb, hspec, cst)

    wrt = jnp.zeros((D, LANES), F32).at[:, :N_GROUPS].set(w_group[i]).at[:, N_GROUPS:N_GROUPS + N_EXPERTS].set(w_router[i])
    brt = jnp.zeros((1, LANES), F32).at[0, :N_GROUPS].set(b_group[i]).at[0, N_GROUPS:N_GROUPS + N_EXPERTS].set(b_router[i])
    wrh = wrt.astype(BF16)
    wrl = (wrt - wrh.astype(F32)).astype(BF16)
    x1, h2, route, route_t = _outproj(ya, yht, x, g_attn_out[i][None], g_hyena_out[i][:, None],
                                      w_out[i].astype(BF16), bd, g_moe[i][None], wrh, wrl, brt)

    e_flat = jnp.transpose(route_t[:, :TOP_K], (1, 0, 2)).reshape(TOP_K * N).astype(jnp.int32)
    block_e, row_a, n_live = _dispatch(e_flat, N)
    y = _moe_experts(block_e, row_a, n_live, h2.reshape(N, ROW_CHUNKS, LANES), w_gate[i], w_up[i], w_down[i])

    out = _final(x1.reshape(N, D), y, route.reshape(N, LANES), p[i].reshape(N, -1), g_ple[i][None],
                 w_ple_gate[i].astype(BF16), b_ple_gate[i][None], w_ple[i].astype(BF16), g_final[None])
    return out.reshape(B, S, D)
```
